```python
import jax, jax.numpy as jnp
from jax import lax
import numpy as np

D_MODEL = 1024
BATCH = 8
SEQ = 4096
DEPTH = 4

D_MIX = D_MODEL
HEAD_DIM = 64
N_ATT_HEADS = 8
D_ATT = N_ATT_HEADS * HEAD_DIM
N_GM_GROUPS = 8
GM_GROUP_DIM = 64
D_GM = N_GM_GROUPS * GM_GROUP_DIM
BLOCK = 128
CHUNK = 128
D_IN = 3 * D_ATT + N_ATT_HEADS + 2 * D_GM
D_FF = ((8 * D_MODEL // 3 + 255) // 256) * 256
PLE_DIM = 256
EPS = 1e-6
NEG_INF = -1e30

kernel_name = "hybrid_fox_gmlp_sandwich_ple"


def rmsnorm(x, gain=None):
    xf = x.astype(jnp.float32)
    y = xf * lax.rsqrt(jnp.mean(xf * xf, axis=-1, keepdims=True) + EPS)
    if gain is not None:
        y = y * gain.astype(jnp.float32)
    return y.astype(x.dtype)


def fox_attention(q, k, v, log_f):
    S = q.shape[1]
    dh = q.shape[-1]
    scale = dh ** -0.5
    c = jnp.cumsum(log_f.astype(jnp.float32), axis=1).transpose(0, 2, 1)
    q_idx = jnp.arange(BLOCK)
    outs = []
    for i in range(S // BLOCK):
        lo, hi = i * BLOCK, (i + 1) * BLOCK
        s = jnp.einsum('bqhd,bkhd->bhqk', q[:, lo:hi], k[:, :hi],
                       preferred_element_type=jnp.float32) * scale
        s = s + c[:, :, lo:hi, None] - c[:, :, None, :hi]
        causal = (lo + q_idx)[:, None] >= jnp.arange(hi)[None, :]
        s = jnp.where(causal, s, NEG_INF)
        w = jax.nn.softmax(s, axis=-1).astype(v.dtype)
        outs.append(jnp.einsum('bhqk,bkhd->bqhd', w, v[:, :hi]))
    return jnp.concatenate(outs, axis=1)


def chunked_spatial_gating(g, w_s, b_s, v_gain):
    B, S, _ = g.shape
    u, vv = jnp.split(jax.nn.gelu(g), 2, axis=-1)
    shp = (B, S // CHUNK, CHUNK, N_GM_GROUPS, GM_GROUP_DIM)
    vf = vv.reshape(shp).astype(jnp.float32)
    mu = jnp.mean(vf, axis=-1, keepdims=True)
    var = jnp.mean(jnp.square(vf - mu), axis=-1, keepdims=True)
    vn = ((vf - mu) * lax.rsqrt(var + EPS) * v_gain.reshape(N_GM_GROUPS, GM_GROUP_DIM).astype(jnp.float32)).astype(g.dtype)
    tril = jnp.tril(jnp.ones((CHUNK, CHUNK), dtype=w_s.dtype))
    w = w_s * tril[None]
    mixed = jnp.einsum('gts,bnsgd->bntgd', w, vn) + b_s.T[:, :, None]
    return (u.reshape(shp) * mixed).reshape(B, S, D_GM)


def _fwd_setup_inputs(seed: int = 0) -> dict:
    key = jax.random.key(seed)
    ks = jax.random.split(key, 20)
    L = DEPTH
    f32 = jnp.float32
    def nrm(k, shape, scale):
        return jax.random.normal(k, shape, f32) * scale
    def gain(k, shape):
        return 1.0 + 0.05 * jax.random.normal(k, shape, f32)
    return {
        "x": jax.random.normal(ks[0], (BATCH, SEQ, D_MODEL), f32),
        "p": jax.random.normal(ks[1], (DEPTH, BATCH, SEQ, PLE_DIM), f32),
        "mix_pre_norm": gain(ks[2], (L, D_MODEL)),
        "mix_post_norm": gain(ks[3], (L, D_MODEL)),
        "w_in": nrm(ks[4], (L, D_MODEL, D_IN), D_MODEL ** -0.5),
        "b_forget": jax.random.uniform(ks[5], (L, N_ATT_HEADS), f32, 2.0, 6.0),
        "gm_v_norm": gain(ks[6], (L, D_GM)),
        "gm_w_s": nrm(ks[7], (L, N_GM_GROUPS, CHUNK, CHUNK), 0.5 * CHUNK ** -0.5),
        "gm_b_s": 1.0 + 0.05 * jax.random.normal(ks[8], (L, N_GM_GROUPS, CHUNK), f32),
        "mix_out_norm": gain(ks[9], (L, D_MIX)),
        "w_out": nrm(ks[10], (L, D_MIX, D_MODEL), D_MIX ** -0.5),
        "ffn_pre_norm": gain(ks[11], (L, D_MODEL)),
        "ffn_post_norm": gain(ks[12], (L, D_MODEL)),
        "w_ffn_in": nrm(ks[13], (L, D_MODEL, 2 * D_FF), D_MODEL ** -0.5),
        "w_ffn_out": nrm(ks[14], (L, D_FF, D_MODEL), D_FF ** -0.5),
        "w_ple": nrm(ks[15], (L, PLE_DIM, D_MODEL), PLE_DIM ** -0.5),
        "ple_norm": gain(ks[16], (L, D_MODEL)),
        "w_ple_gate": nrm(ks[17], (L, D_MODEL, D_MODEL), D_MODEL ** -0.5),
    }


def _fwd_reference(x, p, mix_pre_norm, mix_post_norm, w_in, b_forget, gm_v_norm, gm_w_s,
              gm_b_s, mix_out_norm, w_out, ffn_pre_norm, ffn_post_norm, w_ffn_in,
              w_ffn_out, w_ple, ple_norm, w_ple_gate):
    B, S, _ = x.shape
    h = x
    for i in range(DEPTH):
        hn = rmsnorm(h, mix_pre_norm[i])
        z = hn @ w_in[i]
        q, k, v, f_logit, g = jnp.split(
            z, [D_ATT, 2 * D_ATT, 3 * D_ATT, 3 * D_ATT + N_ATT_HEADS], axis=-1)
        q = q.reshape(B, S, N_ATT_HEADS, HEAD_DIM)
        k = k.reshape(B, S, N_ATT_HEADS, HEAD_DIM)
        v = v.reshape(B, S, N_ATT_HEADS, HEAD_DIM)
        log_f = jax.nn.log_sigmoid(f_logit.astype(jnp.float32) + b_forget[i].astype(jnp.float32))
        att = fox_attention(q, k, v, log_f).reshape(B, S, D_ATT)
        gm = chunked_spatial_gating(g, gm_w_s[i], gm_b_s[i], gm_v_norm[i])
        g_att, g_gm = jnp.split(mix_out_norm[i], [D_ATT])
        mixed = jnp.concatenate([rmsnorm(att, g_att), rmsnorm(gm, g_gm)], axis=-1)
        h = h + rmsnorm(mixed @ w_out[i], mix_post_norm[i])
        hn = rmsnorm(h, ffn_pre_norm[i])
        a, b = jnp.split(hn @ w_ffn_in[i], 2, axis=-1)
        h = h + rmsnorm((jax.nn.silu(a) * b) @ w_ffn_out[i], ffn_post_norm[i])
        e = rmsnorm(p[i] @ w_ple[i], ple_norm[i])
        gate = jax.nn.sigmoid(rmsnorm(h) @ w_ple_gate[i])
        h = h + gate * e
    return h


import jax as _jax
import jax.numpy as _jnp

TWIN_FORMAT = 'train_step'
FWD_PARAMS = ['x', 'p', 'mix_pre_norm', 'mix_post_norm', 'w_in', 'b_forget', 'gm_v_norm', 'gm_w_s', 'gm_b_s', 'mix_out_norm', 'w_out', 'ffn_pre_norm', 'ffn_post_norm', 'w_ffn_in', 'w_ffn_out', 'w_ple', 'ple_norm', 'w_ple_gate']
TWIN_WEIGHTS = ['mix_pre_norm', 'mix_post_norm', 'w_in', 'b_forget', 'gm_v_norm', 'gm_w_s', 'gm_b_s', 'mix_out_norm', 'w_out', 'ffn_pre_norm', 'ffn_post_norm', 'w_ffn_in', 'w_ffn_out', 'w_ple', 'ple_norm', 'w_ple_gate']
TWIN_DIFF_INPUT = 'x'
TWIN_INPUTS = ['x', 'p', 'mix_pre_norm', 'mix_post_norm', 'w_in', 'b_forget', 'gm_v_norm', 'gm_w_s', 'gm_b_s', 'mix_out_norm', 'w_out', 'ffn_pre_norm', 'ffn_post_norm', 'w_ffn_in', 'w_ffn_out', 'w_ple', 'ple_norm', 'w_ple_gate', 'loss_target', 'm_mix_pre_norm', 'm_mix_post_norm', 'm_w_in', 'm_b_forget', 'm_gm_v_norm', 'm_gm_w_s', 'm_gm_b_s', 'm_mix_out_norm', 'm_w_out', 'm_ffn_pre_norm', 'm_ffn_post_norm', 'm_w_ffn_in', 'm_w_ffn_out', 'm_w_ple', 'm_ple_norm', 'm_w_ple_gate', 'v_mix_pre_norm', 'v_mix_post_norm', 'v_w_in', 'v_b_forget', 'v_gm_v_norm', 'v_gm_w_s', 'v_gm_b_s', 'v_mix_out_norm', 'v_w_out', 'v_ffn_pre_norm', 'v_ffn_post_norm', 'v_w_ffn_in', 'v_w_ffn_out', 'v_w_ple', 'v_ple_norm', 'v_w_ple_gate']
TWIN_OUTPUTS = ['loss', 'grad_x', 'grad_mix_pre_norm', 'grad_mix_post_norm', 'grad_w_in', 'grad_b_forget', 'grad_gm_v_norm', 'grad_gm_w_s', 'grad_gm_b_s', 'grad_mix_out_norm', 'grad_w_out', 'grad_ffn_pre_norm', 'grad_ffn_post_norm', 'grad_w_ffn_in', 'grad_w_ffn_out', 'grad_w_ple', 'grad_ple_norm', 'grad_w_ple_gate', 'delta_mix_pre_norm', 'delta_mix_post_norm', 'delta_w_in', 'delta_b_forget', 'delta_gm_v_norm', 'delta_gm_w_s', 'delta_gm_b_s', 'delta_mix_out_norm', 'delta_w_out', 'delta_ffn_pre_norm', 'delta_ffn_post_norm', 'delta_w_ffn_in', 'delta_w_ffn_out', 'delta_w_ple', 'delta_ple_norm', 'delta_w_ple_gate', 'new_m_mix_pre_norm', 'new_m_mix_post_norm', 'new_m_w_in', 'new_m_b_forget', 'new_m_gm_v_norm', 'new_m_gm_w_s', 'new_m_gm_b_s', 'new_m_mix_out_norm', 'new_m_w_out', 'new_m_ffn_pre_norm', 'new_m_ffn_post_norm', 'new_m_w_ffn_in', 'new_m_w_ffn_out', 'new_m_w_ple', 'new_m_ple_norm', 'new_m_w_ple_gate', 'new_v_mix_pre_norm', 'new_v_mix_post_norm', 'new_v_w_in', 'new_v_b_forget', 'new_v_gm_v_norm', 'new_v_gm_w_s', 'new_v_gm_b_s', 'new_v_mix_out_norm', 'new_v_w_out', 'new_v_ffn_pre_norm', 'new_v_ffn_post_norm', 'new_v_w_ffn_in', 'new_v_w_ffn_out', 'new_v_w_ple', 'new_v_ple_norm', 'new_v_w_ple_gate']
TWIN_LEAF_KINDS = {'loss': 'loss', 'grad_x': 'grad_x', 'grad_mix_pre_norm': 'grad_w', 'grad_mix_post_norm': 'grad_w', 'grad_w_in': 'grad_w', 'grad_b_forget': 'grad_w', 'grad_gm_v_norm': 'grad_w', 'grad_gm_w_s': 'grad_w', 'grad_gm_b_s': 'grad_w', 'grad_mix_out_norm': 'grad_w', 'grad_w_out': 'grad_w', 'grad_ffn_pre_norm': 'grad_w', 'grad_ffn_post_norm': 'grad_w', 'grad_w_ffn_in': 'grad_w', 'grad_w_ffn_out': 'grad_w', 'grad_w_ple': 'grad_w', 'grad_ple_norm': 'grad_w', 'grad_w_ple_gate': 'grad_w', 'delta_mix_pre_norm': 'delta_w', 'delta_mix_post_norm': 'delta_w', 'delta_w_in': 'delta_w', 'delta_b_forget': 'delta_w', 'delta_gm_v_norm': 'delta_w', 'delta_gm_w_s': 'delta_w', 'delta_gm_b_s': 'delta_w', 'delta_mix_out_norm': 'delta_w', 'delta_w_out': 'delta_w', 'delta_ffn_pre_norm': 'delta_w', 'delta_ffn_post_norm': 'delta_w', 'delta_w_ffn_in': 'delta_w', 'delta_w_ffn_out': 'delta_w', 'delta_w_ple': 'delta_w', 'delta_ple_norm': 'delta_w', 'delta_w_ple_gate': 'delta_w', 'new_m_mix_pre_norm': 'new_m', 'new_m_mix_post_norm': 'new_m', 'new_m_w_in': 'new_m', 'new_m_b_forget': 'new_m', 'new_m_gm_v_norm': 'new_m', 'new_m_gm_w_s': 'new_m', 'new_m_gm_b_s': 'new_m', 'new_m_mix_out_norm': 'new_m', 'new_m_w_out': 'new_m', 'new_m_ffn_pre_norm': 'new_m', 'new_m_ffn_post_norm': 'new_m', 'new_m_w_ffn_in': 'new_m', 'new_m_w_ffn_out': 'new_m', 'new_m_w_ple': 'new_m', 'new_m_ple_norm': 'new_m', 'new_m_w_ple_gate': 'new_m', 'new_v_mix_pre_norm': 'new_v', 'new_v_mix_post_norm': 'new_v', 'new_v_w_in': 'new_v', 'new_v_b_forget': 'new_v', 'new_v_gm_v_norm': 'new_v', 'new_v_gm_w_s': 'new_v', 'new_v_gm_b_s': 'new_v', 'new_v_mix_out_norm': 'new_v', 'new_v_w_out': 'new_v', 'new_v_ffn_pre_norm': 'new_v', 'new_v_ffn_post_norm': 'new_v', 'new_v_w_ffn_in': 'new_v', 'new_v_w_ffn_out': 'new_v', 'new_v_w_ple': 'new_v', 'new_v_ple_norm': 'new_v', 'new_v_w_ple_gate': 'new_v'}


def _forward(args):
    return _fwd_reference(*[args[k] for k in FWD_PARAMS])


def _output_shape():
    out = _jax.eval_shape(lambda: _forward(_fwd_setup_inputs(0)))
    return out.shape, out.dtype

N_MICROBATCH = 1
ADAM_LR = 0.001
ADAM_B1 = 0.9
ADAM_B2 = 0.999
ADAM_EPS = 1e-08
ADAM_WD = 0.01
ADAM_STEP = 10
PER_EXAMPLE_BATCH_AXIS = {'x': 0, 'p': 1, 'loss_target': 0}
SHARED_INPUTS = []
_WEIGHT_DTYPES = {'mix_pre_norm': _jnp.float32, 'mix_post_norm': _jnp.float32, 'w_in': _jnp.float32, 'b_forget': _jnp.float32, 'gm_v_norm': _jnp.float32, 'gm_w_s': _jnp.float32, 'gm_b_s': _jnp.float32, 'mix_out_norm': _jnp.float32, 'w_out': _jnp.float32, 'ffn_pre_norm': _jnp.float32, 'ffn_post_norm': _jnp.float32, 'w_ffn_in': _jnp.float32, 'w_ffn_out': _jnp.float32, 'w_ple': _jnp.float32, 'ple_norm': _jnp.float32, 'w_ple_gate': _jnp.float32}
MOMENT_SCALE = {'mix_pre_norm': 1.058899e+01, 'mix_post_norm': 3.758811e+01, 'w_in': 6.572162e+00, 'b_forget': 1.465246e+01, 'gm_v_norm': 4.155010e-01, 'gm_w_s': 5.577752e-01, 'gm_b_s': 9.644265e-01, 'mix_out_norm': 1.816001e+01, 'w_out': 1.755958e+01, 'ffn_pre_norm': 4.783847e+00, 'ffn_post_norm': 3.226070e+01, 'w_ffn_in': 1.985494e+00, 'w_ffn_out': 4.051144e+00, 'w_ple': 5.274043e-01, 'ple_norm': 9.359905e+00, 'w_ple_gate': 6.338333e-01}


def _to_microbatches(a, axis):
    t = _jnp.moveaxis(a, axis, 0)
    t = t.reshape((N_MICROBATCH, t.shape[0] // N_MICROBATCH) + t.shape[1:])
    return _jnp.moveaxis(t, 1, axis + 1)


def setup_inputs(seed: int = 0) -> dict:
    inp = _fwd_setup_inputs(seed)
    key = _jax.random.fold_in(_jax.random.key(seed), 7919)
    shape, _ = _output_shape()
    out = dict(inp)
    out["loss_target"] = _jax.random.normal(_jax.random.fold_in(key, 0), shape, _jnp.float32)
    for i, name in enumerate(TWIN_WEIGHTS):
        w = inp[name].astype(_jnp.float32)
        if MOMENT_SCALE is None:
            s = _jnp.sqrt(_jnp.mean(_jnp.square(w)) + 1e-30)
        else:
            s = MOMENT_SCALE[name]
        km, kv = _jax.random.split(_jax.random.fold_in(key, i + 1))
        out[name] = w
        out["m_" + name] = s * _jax.random.normal(km, w.shape, _jnp.float32)
        out["v_" + name] = (s * s) * _jax.random.uniform(kv, w.shape, _jnp.float32, 0.5, 1.5)
    if N_MICROBATCH > 1:
        for name, axis in PER_EXAMPLE_BATCH_AXIS.items():
            out[name] = _to_microbatches(out[name], axis)
    return {'x': out['x'], 'p': out['p'], 'mix_pre_norm': out['mix_pre_norm'], 'mix_post_norm': out['mix_post_norm'], 'w_in': out['w_in'], 'b_forget': out['b_forget'], 'gm_v_norm': out['gm_v_norm'], 'gm_w_s': out['gm_w_s'], 'gm_b_s': out['gm_b_s'], 'mix_out_norm': out['mix_out_norm'], 'w_out': out['w_out'], 'ffn_pre_norm': out['ffn_pre_norm'], 'ffn_post_norm': out['ffn_post_norm'], 'w_ffn_in': out['w_ffn_in'], 'w_ffn_out': out['w_ffn_out'], 'w_ple': out['w_ple'], 'ple_norm': out['ple_norm'], 'w_ple_gate': out['w_ple_gate'], 'loss_target': out['loss_target'], 'm_mix_pre_norm': out['m_mix_pre_norm'], 'm_mix_post_norm': out['m_mix_post_norm'], 'm_w_in': out['m_w_in'], 'm_b_forget': out['m_b_forget'], 'm_gm_v_norm': out['m_gm_v_norm'], 'm_gm_w_s': out['m_gm_w_s'], 'm_gm_b_s': out['m_gm_b_s'], 'm_mix_out_norm': out['m_mix_out_norm'], 'm_w_out': out['m_w_out'], 'm_ffn_pre_norm': out['m_ffn_pre_norm'], 'm_ffn_post_norm': out['m_ffn_post_norm'], 'm_w_ffn_in': out['m_w_ffn_in'], 'm_w_ffn_out': out['m_w_ffn_out'], 'm_w_ple': out['m_w_ple'], 'm_ple_norm': out['m_ple_norm'], 'm_w_ple_gate': out['m_w_ple_gate'], 'v_mix_pre_norm': out['v_mix_pre_norm'], 'v_mix_post_norm': out['v_mix_post_norm'], 'v_w_in': out['v_w_in'], 'v_b_forget': out['v_b_forget'], 'v_gm_v_norm': out['v_gm_v_norm'], 'v_gm_w_s': out['v_gm_w_s'], 'v_gm_b_s': out['v_gm_b_s'], 'v_mix_out_norm': out['v_mix_out_norm'], 'v_w_out': out['v_w_out'], 'v_ffn_pre_norm': out['v_ffn_pre_norm'], 'v_ffn_post_norm': out['v_ffn_post_norm'], 'v_w_ffn_in': out['v_w_ffn_in'], 'v_w_ffn_out': out['v_w_ffn_out'], 'v_w_ple': out['v_w_ple'], 'v_ple_norm': out['v_ple_norm'], 'v_w_ple_gate': out['v_w_ple_gate']}


def _loss(weights, diff, rest, loss_target):
    with _jax.named_scope("forward"):
        args = {**rest, TWIN_DIFF_INPUT: diff, **{k: w.astype(_WEIGHT_DTYPES[k]) for k, w in weights.items()}}
        y = _forward(args)
    with _jax.named_scope("loss_head"):
        err = _jnp.square(y.astype(_jnp.float32) - loss_target)
        return 0.5 * _jnp.sum(_jnp.mean(err, axis=-1)) if err.ndim else 0.5 * err


def _adamw(w, g, m, v):
    m = ADAM_B1 * m + (1.0 - ADAM_B1) * g
    v = ADAM_B2 * v + (1.0 - ADAM_B2) * _jnp.square(g)
    m_hat = m / (1.0 - ADAM_B1 ** ADAM_STEP)
    v_hat = v / (1.0 - ADAM_B2 ** ADAM_STEP)
    delta = -ADAM_LR * (m_hat / (_jnp.sqrt(v_hat) + ADAM_EPS) + ADAM_WD * w)
    return delta, m, v


def reference(x, p, mix_pre_norm, mix_post_norm, w_in, b_forget, gm_v_norm, gm_w_s, gm_b_s, mix_out_norm, w_out, ffn_pre_norm, ffn_post_norm, w_ffn_in, w_ffn_out, w_ple, ple_norm, w_ple_gate, loss_target, m_mix_pre_norm, m_mix_post_norm, m_w_in, m_b_forget, m_gm_v_norm, m_gm_w_s, m_gm_b_s, m_mix_out_norm, m_w_out, m_ffn_pre_norm, m_ffn_post_norm, m_w_ffn_in, m_w_ffn_out, m_w_ple, m_ple_norm, m_w_ple_gate, v_mix_pre_norm, v_mix_post_norm, v_w_in, v_b_forget, v_gm_v_norm, v_gm_w_s, v_gm_b_s, v_mix_out_norm, v_w_out, v_ffn_pre_norm, v_ffn_post_norm, v_w_ffn_in, v_w_ffn_out, v_w_ple, v_ple_norm, v_w_ple_gate):
    given = dict(x=x, p=p, mix_pre_norm=mix_pre_norm, mix_post_norm=mix_post_norm, w_in=w_in, b_forget=b_forget, gm_v_norm=gm_v_norm, gm_w_s=gm_w_s, gm_b_s=gm_b_s, mix_out_norm=mix_out_norm, w_out=w_out, ffn_pre_norm=ffn_pre_norm, ffn_post_norm=ffn_post_norm, w_ffn_in=w_ffn_in, w_ffn_out=w_ffn_out, w_ple=w_ple, ple_norm=ple_norm, w_ple_gate=w_ple_gate, loss_target=loss_target, m_mix_pre_norm=m_mix_pre_norm, m_mix_post_norm=m_mix_post_norm, m_w_in=m_w_in, m_b_forget=m_b_forget, m_gm_v_norm=m_gm_v_norm, m_gm_w_s=m_gm_w_s, m_gm_b_s=m_gm_b_s, m_mix_out_norm=m_mix_out_norm, m_w_out=m_w_out, m_ffn_pre_norm=m_ffn_pre_norm, m_ffn_post_norm=m_ffn_post_norm, m_w_ffn_in=m_w_ffn_in, m_w_ffn_out=m_w_ffn_out, m_w_ple=m_w_ple, m_ple_norm=m_ple_norm, m_w_ple_gate=m_w_ple_gate, v_mix_pre_norm=v_mix_pre_norm, v_mix_post_norm=v_mix_post_norm, v_w_in=v_w_in, v_b_forget=v_b_forget, v_gm_v_norm=v_gm_v_norm, v_gm_w_s=v_gm_w_s, v_gm_b_s=v_gm_b_s, v_mix_out_norm=v_mix_out_norm, v_w_out=v_w_out, v_ffn_pre_norm=v_ffn_pre_norm, v_ffn_post_norm=v_ffn_post_norm, v_w_ffn_in=v_w_ffn_in, v_w_ffn_out=v_w_ffn_out, v_w_ple=v_w_ple, v_ple_norm=v_ple_norm, v_w_ple_gate=v_w_ple_gate)
    weights = {n: given[n] for n in TWIN_WEIGHTS}
    shared = {n: given[n] for n in SHARED_INPUTS}
    per_example = {n: given[n] for n in ['x', 'p']}
    grad_fn = _jax.value_and_grad(_loss, argnums=(0, 1))

    def one_microbatch(ex, loss_target):
        ex = dict(ex)
        diff = ex.pop(TWIN_DIFF_INPUT)
        return grad_fn(weights, diff, {**shared, **ex}, loss_target)

    if N_MICROBATCH == 1:
        loss, (grad_w, grad_x) = one_microbatch(per_example, given["loss_target"])
    else:
        def body(carry, xs):
            loss_sum, grad_sum = carry
            l_k, (gw_k, gx_k) = one_microbatch(xs[0], xs[1])
            with _jax.named_scope("update"):
                return (loss_sum + l_k, _jax.tree.map(_jnp.add, grad_sum, gw_k)), gx_k

        init = (_jnp.zeros((), _jnp.float32), _jax.tree.map(_jnp.zeros_like, weights))
        (loss, grad_w), grad_x = _jax.lax.scan(body, init, (per_example, given["loss_target"]))
    with _jax.named_scope("update"):
        delta_w, new_m, new_v = {}, {}, {}
        for n in TWIN_WEIGHTS:
            delta_w[n], new_m[n], new_v[n] = _adamw(weights[n], grad_w[n], given["m_" + n], given["v_" + n])
    return (loss, grad_x, *[grad_w[n] for n in TWIN_WEIGHTS], *[delta_w[n] for n in TWIN_WEIGHTS],
            *[new_m[n] for n in TWIN_WEIGHTS], *[new_v[n] for n in TWIN_WEIGHTS])
```

```python
import functools
import math

import jax
import jax.numpy as jnp
from jax import lax
from jax.experimental import pallas as pl
from jax.experimental.pallas import tpu as pltpu

F32 = jnp.float32
BF16 = jnp.bfloat16
MESH = pl.DeviceIdType.MESH
AXES = ("x", "y", "c")
N_DEV = 8

EPS = 1e-6
NEG_INF = -1e30
N_HEADS = 8
HEAD_DIM = 64
D_ATT = N_HEADS * HEAD_DIM
N_GROUPS = 8
GROUP_DIM = 64
D_GM = N_GROUPS * GROUP_DIM
CHUNK = 128
ATT_SCALE = HEAD_DIM ** -0.5
ATT_BLOCK = 512
D_IN = 3 * D_ATT + N_HEADS + 2 * D_GM
D_IN_PAD = 3 * D_ATT + 2 * D_GM + 128
F_OFF = 3 * D_ATT + 2 * D_GM

ADAM_LR = 0.001
ADAM_B1 = 0.9
ADAM_B2 = 0.999
ADAM_EPS = 1e-08
ADAM_WD = 0.01
ADAM_STEP = 10

LANE = 128
VMEM_LIMIT = 48 * 1024 * 1024
PACK_COLS = 512
PACK_ROWS = 512
SMALL_COLS = 128
SMALL_ROWS = 512

MATRIX_WEIGHTS = ("w_in", "w_out", "w_ffn_in", "w_ffn_out", "w_ple", "w_ple_gate")
SHARD_AXIS = {"w_in": 2, "w_out": 1, "w_ffn_in": 2, "w_ffn_out": 1, "w_ple": 2, "w_ple_gate": 1}
SMALL_WEIGHTS = ("mix_pre_norm", "mix_post_norm", "b_forget", "gm_v_norm", "gm_w_s", "gm_b_s",
                 "mix_out_norm", "ffn_pre_norm", "ffn_post_norm", "ple_norm")
WEIGHT_ORDER = ("mix_pre_norm", "mix_post_norm", "w_in", "b_forget", "gm_v_norm", "gm_w_s", "gm_b_s",
                "mix_out_norm", "w_out", "ffn_pre_norm", "ffn_post_norm", "w_ffn_in", "w_ffn_out",
                "w_ple", "ple_norm", "w_ple_gate")


def _tile(n, pref, unit=LANE):
    best = None
    t = unit
    while t <= min(n, pref):
        if n % t == 0:
            best = t
        t += unit
    return n if best is None else best


def _cparams(*semantics):
    return pltpu.CompilerParams(dimension_semantics=semantics or None, vmem_limit_bytes=VMEM_LIMIT)


def _matmul(a, b, dims, out_dtype, name, tm=512, tn=1024, tk=1024):
    if dims == "nn":
        (m, k), (_, n) = a.shape, b.shape
    elif dims == "nt":
        (m, k), (n, _) = a.shape, b.shape
    else:
        (k, m), (_, n) = a.shape, b.shape
    tm, tn, tk = _tile(m, tm), _tile(n, tn), _tile(k, tk)
    nk = k // tk
    if dims == "nn":
        a_spec = pl.BlockSpec((tm, tk), lambda i, j, kk: (i, kk))
        b_spec = pl.BlockSpec((tk, tn), lambda i, j, kk: (kk, j))
        dn = (((1,), (0,)), ((), ()))
    elif dims == "nt":
        a_spec = pl.BlockSpec((tm, tk), lambda i, j, kk: (i, kk))
        b_spec = pl.BlockSpec((tn, tk), lambda i, j, kk: (j, kk))
        dn = (((1,), (1,)), ((), ()))
    else:
        a_spec = pl.BlockSpec((tk, tm), lambda i, j, kk: (kk, i))
        b_spec = pl.BlockSpec((tk, tn), lambda i, j, kk: (kk, j))
        dn = (((0,), (0,)), ((), ()))

    def body(a_ref, b_ref, o_ref, acc_ref):
        kk = pl.program_id(2)

        @pl.when(kk == 0)
        def _():
            acc_ref[...] = jnp.zeros_like(acc_ref)

        acc_ref[...] += lax.dot_general(a_ref[...].astype(BF16), b_ref[...].astype(BF16), dn,
                                        preferred_element_type=F32)

        @pl.when(kk == nk - 1)
        def _():
            o_ref[...] = acc_ref[...].astype(out_dtype)

    return pl.pallas_call(
        body, name=name, out_shape=jax.ShapeDtypeStruct((m, n), out_dtype),
        grid=(m // tm, n // tn, nk), in_specs=[a_spec, b_spec],
        out_specs=pl.BlockSpec((tm, tn), lambda i, j, kk: (i, j)),
        scratch_shapes=[pltpu.VMEM((tm, tn), F32)],
        compiler_params=_cparams("parallel", "parallel", "arbitrary"),
    )(a, b)


def _rowwise(fn, rows, vecs, outs, reds, name, ts):
    rows = [r if isinstance(r, tuple) else (r, 0, r.shape[1]) for r in rows]
    s = rows[0][0].shape[0]
    ts = _tile(s, ts, 8)
    nr, nv, no = len(rows), len(vecs), len(outs)

    def body(*refs):
        vals = fn(*[r[...] for r in refs[:nr + nv]])
        vals = vals if isinstance(vals, tuple) else (vals,)
        o_refs = refs[nr + nv:nr + nv + no]
        r_refs = refs[nr + nv + no:]
        for o_ref, val in zip(o_refs, vals[:no]):
            o_ref[...] = val.astype(o_ref.dtype)
        if r_refs:
            @pl.when(pl.program_id(0) == 0)
            def _():
                for r_ref in r_refs:
                    r_ref[...] = jnp.zeros_like(r_ref)

            for r_ref, val in zip(r_refs, vals[no:]):
                r_ref[...] += val

    in_specs = [pl.BlockSpec((ts, w), functools.partial(lambda i, cb: (i, cb), cb=cb)) for _, cb, w in rows]
    in_specs += [pl.BlockSpec(v.shape, lambda i: (0, 0)) for v in vecs]
    out_specs = [pl.BlockSpec((ts, c), lambda i: (i, 0)) for c, _ in outs]
    out_specs += [pl.BlockSpec((1, c), lambda i: (0, 0)) for c in reds]
    out_shape = [jax.ShapeDtypeStruct((s, c), dt) for c, dt in outs]
    out_shape += [jax.ShapeDtypeStruct((1, c), F32) for c in reds]
    res = pl.pallas_call(
        body, name=name, out_shape=out_shape, grid=(s // ts,), in_specs=in_specs, out_specs=out_specs,
        compiler_params=_cparams("arbitrary" if reds else "parallel"),
    )(*[r[0] for r in rows], *vecs)
    return res


def _rms(x):
    r = lax.rsqrt(jnp.mean(x * x, axis=-1, keepdims=True) + EPS)
    return x * r, r


def _rms_bwd(xhat, r, dyg):
    return r * (dyg - xhat * jnp.mean(dyg * xhat, axis=-1, keepdims=True))


def _colsum(x):
    return jnp.sum(x, axis=0, keepdims=True)


def _sigmoid(x):
    return 1.0 / (1.0 + jnp.exp(-x))


GELU_C = math.sqrt(2.0 / math.pi)
GELU_A = 0.044715


def _gelu(x):
    return 0.5 * x * (1.0 + jnp.tanh(GELU_C * (x + GELU_A * x * x * x)))


def _gelu_grad(x):
    t = jnp.tanh(GELU_C * (x + GELU_A * x * x * x))
    return 0.5 * (1.0 + t) + 0.5 * x * (1.0 - t * t) * GELU_C * (1.0 + 3.0 * GELU_A * x * x)


def _forget_fwd(fl_t, b_col):
    h, s = fl_t.shape
    nb = s // LANE

    def body(fl_ref, b_ref, c_ref):
        upper = (lax.broadcasted_iota(jnp.int32, (LANE, LANE), 0)
                 <= lax.broadcasted_iota(jnp.int32, (LANE, LANE), 1)).astype(F32)

        def step(i, carry):
            x = fl_ref[i] + b_ref[...]
            lf = jnp.minimum(x, 0.0) - jnp.log(1.0 + jnp.exp(-jnp.abs(x)))
            cs = jnp.dot(lf, upper, precision=lax.Precision.HIGHEST, preferred_element_type=F32) + carry
            c_ref[i] = cs
            return cs[:, LANE - 1:LANE]

        lax.fori_loop(0, nb, step, jnp.zeros((h, 1), F32))

    out = pl.pallas_call(
        body, name="forget_fwd", out_shape=jax.ShapeDtypeStruct((nb, h, LANE), F32),
        compiler_params=_cparams(),
    )(fl_t.reshape(h, nb, LANE).transpose(1, 0, 2), b_col)
    return out.transpose(1, 0, 2).reshape(h, s)


def _forget_bwd(dc_t, fl_t, b_col):
    h, s = fl_t.shape
    nb = s // LANE

    def body(dc_ref, fl_ref, b_ref, dfl_ref, db_ref):
        lower = (lax.broadcasted_iota(jnp.int32, (LANE, LANE), 0)
                 >= lax.broadcasted_iota(jnp.int32, (LANE, LANE), 1)).astype(F32)

        def step(t, carry):
            tail, db = carry
            i = nb - 1 - t
            rc = jnp.dot(dc_ref[i], lower, precision=lax.Precision.HIGHEST, preferred_element_type=F32) + tail
            dfl = rc * (1.0 - _sigmoid(fl_ref[i] + b_ref[...]))
            dfl_ref[i] = dfl
            return rc[:, 0:1], db + jnp.sum(dfl, axis=1, keepdims=True)

        _, db = lax.fori_loop(0, nb, step, (jnp.zeros((h, 1), F32), jnp.zeros((h, 1), F32)))
        db_ref[...] = db

    blocked = lambda a: a.reshape(h, nb, LANE).transpose(1, 0, 2)
    dfl, db = pl.pallas_call(
        body, name="forget_bwd",
        out_shape=(jax.ShapeDtypeStruct((nb, h, LANE), F32), jax.ShapeDtypeStruct((h, 1), F32)),
        compiler_params=_cparams(),
    )(blocked(dc_t), blocked(fl_t), b_col)
    return dfl.transpose(1, 0, 2).reshape(h, s), db


def _causal_mask(t):
    return lax.broadcasted_iota(jnp.int32, (t, t), 0) >= lax.broadcasted_iota(jnp.int32, (t, t), 1)


NT = (((1,), (1,)), ((), ()))
TN = (((0,), (0,)), ((), ()))


def _attn_fwd(q, k, v, c_col, c_row, tq):
    h, s, dh = q.shape
    nq = s // tq

    def body(q_ref, k_ref, v_ref, cc_ref, cr_ref, o_ref, lse_ref):
        i = pl.program_id(1)
        qi = q_ref[0]
        ci = cc_ref[0]

        def block(j, carry, masked):
            m, l, acc = carry
            off = pl.multiple_of(j * tq, tq)
            kj = k_ref[0, pl.ds(off, tq), :]
            vj = v_ref[0, pl.ds(off, tq), :]
            sc = lax.dot_general(qi, kj, NT, preferred_element_type=F32) * ATT_SCALE + ci - cr_ref[0, j]
            if masked:
                sc = jnp.where(_causal_mask(tq), sc, NEG_INF)
            m_new = jnp.maximum(m, jnp.max(sc, axis=-1, keepdims=True))
            alpha = jnp.exp(m - m_new)
            p = jnp.exp(sc - m_new)
            l = alpha * l + jnp.sum(p, axis=-1, keepdims=True)
            p_hi = p.astype(BF16)
            p_lo = (p - p_hi.astype(F32)).astype(BF16)
            acc = (alpha * acc + jnp.dot(p_hi, vj, preferred_element_type=F32)
                   + jnp.dot(p_lo, vj, preferred_element_type=F32))
            return m_new, l, acc

        init = (jnp.full((tq, 1), NEG_INF, F32), jnp.zeros((tq, 1), F32), jnp.zeros((tq, dh), F32))
        carry = lax.fori_loop(0, i, lambda j, c: block(j, c, False), init)
        m, l, acc = block(i, carry, True)
        o_ref[0] = acc / l
        lse_ref[0] = m + jnp.log(l)

    whole = lambda hh, i: (hh, 0, 0)
    return pl.pallas_call(
        body, name="attn_fwd",
        out_shape=(jax.ShapeDtypeStruct((h, s, dh), F32), jax.ShapeDtypeStruct((h, s, 1), F32)),
        grid=(h, nq),
        in_specs=[pl.BlockSpec((1, tq, dh), lambda hh, i: (hh, i, 0)),
                  pl.BlockSpec((1, s, dh), whole), pl.BlockSpec((1, s, dh), whole),
                  pl.BlockSpec((1, tq, 1), lambda hh, i: (hh, i, 0)),
                  pl.BlockSpec((1, nq, 1, tq), lambda hh, i: (hh, 0, 0, 0))],
        out_specs=(pl.BlockSpec((1, tq, dh), lambda hh, i: (hh, i, 0)),
                   pl.BlockSpec((1, tq, 1), lambda hh, i: (hh, i, 0))),
        compiler_params=_cparams("parallel", "parallel"),
    )(q, k, v, c_col, c_row)


def _attn_delta(o, do, tq):
    h, s, dh = o.shape

    def body(o_ref, do_ref, d_ref):
        d_ref[0] = jnp.sum(o_ref[0] * do_ref[0].astype(F32), axis=-1, keepdims=True)

    blk = pl.BlockSpec((1, tq, dh), lambda hh, i: (hh, i, 0))
    return pl.pallas_call(
        body, name="attn_delta", out_shape=jax.ShapeDtypeStruct((h, s, 1), F32), grid=(h, s // tq),
        in_specs=[blk, blk], out_specs=pl.BlockSpec((1, tq, 1), lambda hh, i: (hh, i, 0)),
        compiler_params=_cparams("parallel", "parallel"),
    )(o, do)


def _attn_bwd(q, k, v, do, lse, delta, c_col, c_row, tq):
    h, s, dh = q.shape
    nq = s // tq

    def body(q_ref, do_ref, lse_ref, dl_ref, cc_ref, k_ref, v_ref, cr_ref, dq_ref, dk_ref, dv_ref, dc_ref):
        j = pl.program_id(1)

        @pl.when(j == 0)
        def _():
            dq_ref[...] = jnp.zeros_like(dq_ref)

        kj = k_ref[0]
        vj = v_ref[0]
        crj = cr_ref[0, 0]

        def step(i, carry, masked):
            dk, dv, dcs = carry
            off = pl.multiple_of(i * tq, tq)
            rows = pl.ds(off, tq)
            qi = q_ref[0, rows, :]
            doi = do_ref[0, rows, :]
            sc = lax.dot_general(qi, kj, NT, preferred_element_type=F32) * ATT_SCALE + cc_ref[0, rows, :] - crj
            if masked:
                sc = jnp.where(_causal_mask(tq), sc, NEG_INF)
            p = jnp.exp(sc - lse_ref[0, rows, :])
            dv = dv + lax.dot_general(p.astype(BF16), doi, TN, preferred_element_type=F32)
            dp = lax.dot_general(doi, vj, NT, preferred_element_type=F32)
            ds = p * (dp - dl_ref[0, rows, :])
            dsb = ds.astype(BF16)
            dk = dk + lax.dot_general(dsb, qi, TN, preferred_element_type=F32)
            dq_ref[0, rows, :] += jnp.dot(dsb, kj, preferred_element_type=F32) * ATT_SCALE
            return dk, dv, dcs + jnp.sum(ds, axis=0, keepdims=True)

        init = (jnp.zeros((tq, dh), F32), jnp.zeros((tq, dh), F32), jnp.zeros((1, tq), F32))
        carry = step(j, init, True)
        dk, dv, dcs = lax.fori_loop(j + 1, nq, lambda i, c: step(i, c, False), carry)
        dk_ref[0] = dk * ATT_SCALE
        dv_ref[0] = dv
        dc_ref[0, 0] = -dcs

    whole = lambda hh, j: (hh, 0, 0)
    blk = pl.BlockSpec((1, tq, dh), lambda hh, j: (hh, j, 0))
    crow = pl.BlockSpec((1, 1, 1, tq), lambda hh, j: (hh, j, 0, 0))
    return pl.pallas_call(
        body, name="attn_bwd",
        out_shape=(jax.ShapeDtypeStruct((h, s, dh), F32), jax.ShapeDtypeStruct((h, s, dh), F32),
                   jax.ShapeDtypeStruct((h, s, dh), F32), jax.ShapeDtypeStruct((h, nq, 1, tq), F32)),
        grid=(h, nq),
        in_specs=[pl.BlockSpec((1, s, dh), whole), pl.BlockSpec((1, s, dh), whole),
                  pl.BlockSpec((1, s, 1), whole), pl.BlockSpec((1, s, 1), whole), pl.BlockSpec((1, s, 1), whole),
                  blk, blk, crow],
        out_specs=(pl.BlockSpec((1, s, dh), whole), blk, blk, crow),
        compiler_params=_cparams("parallel", "arbitrary"),
    )(q, do, lse, delta, c_col, k, v, c_row)


def _gm_group_norm(vg):
    mu = jnp.mean(vg, axis=-1, keepdims=True)
    d = vg - mu
    rstd = lax.rsqrt(jnp.mean(d * d, axis=-1, keepdims=True) + EPS)
    return d * rstd, rstd


def _gmlp_fwd(z, wt, bs_t, vgain):
    s = z.shape[0]

    def body(gu_ref, gv_ref, wt_ref, bs_ref, vg_ref, o_ref):
        for g in range(N_GROUPS):
            sl = slice(g * GROUP_DIM, (g + 1) * GROUP_DIM)
            vhat, _ = _gm_group_norm(_gelu(gv_ref[:, sl]))
            vn = vhat * vg_ref[:, sl]
            mixed = jnp.dot(wt_ref[g], vn.astype(BF16), preferred_element_type=F32) + bs_ref[:, g:g + 1]
            o_ref[:, sl] = _gelu(gu_ref[:, sl]) * mixed

    full = lambda a: pl.BlockSpec(a.shape, lambda n: (0,) * a.ndim)
    return pl.pallas_call(
        body, name="gmlp_fwd", out_shape=jax.ShapeDtypeStruct((s, D_GM), F32), grid=(s // CHUNK,),
        in_specs=[pl.BlockSpec((CHUNK, D_GM), lambda n: (n, 3)), pl.BlockSpec((CHUNK, D_GM), lambda n: (n, 4)),
                  full(wt), full(bs_t), full(vgain)],
        out_specs=pl.BlockSpec((CHUNK, D_GM), lambda n: (n, 0)),
        compiler_params=_cparams("parallel"),
    )(z, z, wt, bs_t, vgain)


def _gmlp_bwd(z, dgm, wt, wt_t, bs_t, vgain):
    s = z.shape[0]

    def body(gu_ref, gv_ref, dgm_ref, wt_ref, wtt_ref, bs_ref, vg_ref, dgu_ref, dgv_ref, dwt_ref, dbs_ref, dvg_ref):
        @pl.when(pl.program_id(0) == 0)
        def _():
            dwt_ref[...] = jnp.zeros_like(dwt_ref)
            dbs_ref[...] = jnp.zeros_like(dbs_ref)
            dvg_ref[...] = jnp.zeros_like(dvg_ref)

        for g in range(N_GROUPS):
            sl = slice(g * GROUP_DIM, (g + 1) * GROUP_DIM)
            gu = gu_ref[:, sl]
            gv = gv_ref[:, sl]
            dgm = dgm_ref[:, sl]
            vhat, rstd = _gm_group_norm(_gelu(gv))
            gain = vg_ref[:, sl]
            vn = (vhat * gain).astype(BF16)
            mixed = jnp.dot(wt_ref[g], vn, preferred_element_type=F32) + bs_ref[:, g:g + 1]
            dgu_ref[:, sl] = (dgm * mixed * _gelu_grad(gu)).astype(BF16)
            dmixed = dgm * _gelu(gu)
            dmb = dmixed.astype(BF16)
            dbs_ref[:, g:g + 1] += jnp.sum(dmixed, axis=-1, keepdims=True)
            dwt_ref[g] += lax.dot_general(dmb, vn, NT, preferred_element_type=F32)
            dvn = jnp.dot(wtt_ref[g], dmb, preferred_element_type=F32)
            dvg_ref[:, sl] += _colsum(dvn * vhat)
            dvhat = dvn * gain
            dvf = rstd * (dvhat - jnp.mean(dvhat, axis=-1, keepdims=True)
                          - vhat * jnp.mean(dvhat * vhat, axis=-1, keepdims=True))
            dgv_ref[:, sl] = (dvf * _gelu_grad(gv)).astype(BF16)

    full = lambda a: pl.BlockSpec(a.shape, lambda n: (0,) * a.ndim)
    chunk = pl.BlockSpec((CHUNK, D_GM), lambda n: (n, 0))
    return pl.pallas_call(
        body, name="gmlp_bwd",
        out_shape=(jax.ShapeDtypeStruct((s, D_GM), BF16), jax.ShapeDtypeStruct((s, D_GM), BF16),
                   jax.ShapeDtypeStruct(wt.shape, F32), jax.ShapeDtypeStruct(bs_t.shape, F32),
                   jax.ShapeDtypeStruct(vgain.shape, F32)),
        grid=(s // CHUNK,),
        in_specs=[pl.BlockSpec((CHUNK, D_GM), lambda n: (n, 3)), pl.BlockSpec((CHUNK, D_GM), lambda n: (n, 4)),
                  chunk, full(wt), full(wt_t), full(bs_t), full(vgain)],
        out_specs=(chunk, chunk, full(wt), full(bs_t), full(vgain)),
        compiler_params=_cparams("arbitrary"),
    )(z, z, dgm, wt, wt_t, bs_t, vgain)


def _heads(a):
    s = a.shape[0]
    return a.reshape(s, N_HEADS, HEAD_DIM).transpose(1, 0, 2)


def _unheads(a):
    s = a.shape[1]
    return a.transpose(1, 0, 2).reshape(s, D_ATT)


def _layer_fwd(h0, p_i, w, tq):
    s, d = h0.shape
    f = w["w_ffn_out"].shape[0]
    nq = s // tq
    sv = {"h0": h0}

    (hn1,) = _rowwise(lambda h, g: _rms(h)[0] * g, [h0], [w["mix_pre_norm"]], [(d, BF16)], [], "pre_mix", 256)
    z = _matmul(hn1, w["w_in"], "nn", F32, "mm_in")
    fl_t = z[:, F_OFF:F_OFF + N_HEADS].T
    c_t = _forget_fwd(fl_t, w["b_forget"])
    c_col = c_t[:, :, None]
    c_row = c_t.reshape(N_HEADS, nq, 1, tq)
    q = _heads(z[:, 0:D_ATT]).astype(BF16)
    k = _heads(z[:, D_ATT:2 * D_ATT]).astype(BF16)
    v = _heads(z[:, 2 * D_ATT:3 * D_ATT]).astype(BF16)
    o, lse = _attn_fwd(q, k, v, c_col, c_row, tq)
    att = _unheads(o)
    gm = _gmlp_fwd(z, w["wt"], w["bs_t"], w["gm_v_norm"])

    def mix_out(att, gm, g):
        return jnp.concatenate([_rms(att)[0] * g[:, :D_ATT], _rms(gm)[0] * g[:, D_ATT:]], axis=-1)

    (mc,) = _rowwise(mix_out, [att, gm], [w["mix_out_norm"]], [(D_ATT + D_GM, BF16)], [], "mix_out", 256)
    y1 = _matmul(mc, w["w_out"], "nn", F32, "mm_out")

    def post_mix(h0, y1, gpost, gpre):
        h1 = h0 + _rms(y1)[0] * gpost
        return h1, _rms(h1)[0] * gpre

    h1, hn2 = _rowwise(post_mix, [h0, y1], [w["mix_post_norm"], w["ffn_pre_norm"]],
                       [(d, F32), (d, BF16)], [], "post_mix", 256)
    ab = _matmul(hn2, w["w_ffn_in"], "nn", F32, "mm_ffn_in")
    (t,) = _rowwise(lambda a, b: a * _sigmoid(a) * b, [(ab, 0, f), (ab, 1, f)], [], [(f, BF16)], [], "swiglu", 128)
    y2 = _matmul(t, w["w_ffn_out"], "nn", F32, "mm_ffn_out")

    def post_ffn(h1, y2, g):
        h2 = h1 + _rms(y2)[0] * g
        return h2, _rms(h2)[0]

    h2, hr = _rowwise(post_ffn, [h1, y2], [w["ffn_post_norm"]], [(d, F32), (d, BF16)], [], "post_ffn", 256)
    gl = _matmul(hr, w["w_ple_gate"], "nn", F32, "mm_gate")
    pe = _matmul(p_i, w["w_ple"], "nn", F32, "mm_ple")
    (h3,) = _rowwise(lambda h2, gl, pe, g: h2 + _sigmoid(gl) * (_rms(pe)[0] * g), [h2, gl, pe], [w["ple_norm"]],
                     [(d, F32)], [], "ple_out", 256)
    sv.update(hn1=hn1, z=z, fl_t=fl_t, c_col=c_col, c_row=c_row, q=q, k=k, v=v, o=o, lse=lse, att=att, gm=gm,
              mc=mc, y1=y1, h1=h1, hn2=hn2, ab=ab, y2=y2, h2=h2, hr=hr, gl=gl, pe=pe, p_i=p_i)
    return h3, sv


def _layer_bwd(dh3, sv, w, tq):
    s, d = dh3.shape
    f = w["w_ffn_out"].shape[0]
    g = {}

    def ple_bwd(dh3, gl, pe, gple):
        gate = _sigmoid(gl)
        pehat, rpe = _rms(pe)
        dgl = dh3 * (pehat * gple) * gate * (1.0 - gate)
        de = dh3 * gate
        return dgl, _rms_bwd(pehat, rpe, de * gple), _colsum(de * pehat)

    dgl, dpe, g["ple_norm"] = _rowwise(ple_bwd, [dh3, sv["gl"], sv["pe"]], [w["ple_norm"]],
                                       [(d, BF16), (d, BF16)], [d], "ple_bwd", 256)
    g["w_ple_gate"] = _matmul(sv["hr"], dgl, "tn", F32, "mm_dgate")
    dhr = _matmul(dgl, w["w_ple_gate"], "nt", F32, "mm_dhr")
    g["w_ple"] = _matmul(sv["p_i"], dpe, "tn", F32, "mm_dple")

    def ffn_post_bwd(dh3, dhr, h2, y2, gpost):
        h2hat, r2 = _rms(h2)
        dh2 = dh3 + _rms_bwd(h2hat, r2, dhr)
        y2hat, ry = _rms(y2)
        return dh2, _rms_bwd(y2hat, ry, dh2 * gpost), _colsum(dh2 * y2hat)

    dh2, dy2, g["ffn_post_norm"] = _rowwise(ffn_post_bwd, [dh3, dhr, sv["h2"], sv["y2"]], [w["ffn_post_norm"]],
                                            [(d, F32), (d, BF16)], [d], "ffn_post_bwd", 256)
    dt = _matmul(dy2, w["w_ffn_out"], "nt", F32, "mm_dt")

    def swiglu_bwd(a, b, dt):
        sig = _sigmoid(a)
        silu = a * sig
        da = dt * b * (sig * (1.0 + a * (1.0 - sig)))
        return silu * b, jnp.concatenate([da, dt * silu], axis=-1)

    ab = sv["ab"]
    t, dab = _rowwise(swiglu_bwd, [(ab, 0, f), (ab, 1, f), dt], [], [(f, BF16), (2 * f, BF16)], [], "swiglu_bwd", 128)
    g["w_ffn_out"] = _matmul(t, dy2, "tn", F32, "mm_dffn_out")
    dhn2 = _matmul(dab, w["w_ffn_in"], "nt", F32, "mm_dhn2")
    g["w_ffn_in"] = _matmul(sv["hn2"], dab, "tn", F32, "mm_dffn_in")

    def mix_post_bwd(dh2, dhn2, h1, y1, gpre, gpost):
        h1hat, r1 = _rms(h1)
        dh1 = dh2 + _rms_bwd(h1hat, r1, dhn2 * gpre)
        y1hat, ry = _rms(y1)
        return dh1, _rms_bwd(y1hat, ry, dh1 * gpost), _colsum(dhn2 * h1hat), _colsum(dh1 * y1hat)

    dh1, dy1, g["ffn_pre_norm"], g["mix_post_norm"] = _rowwise(
        mix_post_bwd, [dh2, dhn2, sv["h1"], sv["y1"]], [w["ffn_pre_norm"], w["mix_post_norm"]],
        [(d, F32), (d, BF16)], [d, d], "mix_post_bwd", 256)
    dmc = _matmul(dy1, w["w_out"], "nt", F32, "mm_dmc")
    g["w_out"] = _matmul(sv["mc"], dy1, "tn", F32, "mm_dout")

    def mix_out_bwd(da, dg, att, gm, gain):
        atthat, ra = _rms(att)
        gmhat, rg = _rms(gm)
        dgain = jnp.concatenate([_colsum(da * atthat), _colsum(dg * gmhat)], axis=-1)
        return _rms_bwd(atthat, ra, da * gain[:, :D_ATT]), _rms_bwd(gmhat, rg, dg * gain[:, D_ATT:]), dgain

    datt, dgm, g["mix_out_norm"] = _rowwise(
        mix_out_bwd, [(dmc, 0, D_ATT), (dmc, 1, D_GM), sv["att"], sv["gm"]], [w["mix_out_norm"]],
        [(D_ATT, F32), (D_GM, F32)], [D_ATT + D_GM], "mix_out_bwd", 256)

    dgu, dgv, dwt, dbs_t, g["gm_v_norm"] = _gmlp_bwd(sv["z"], dgm, w["wt"], w["wt_t"], w["bs_t"], w["gm_v_norm"])
    g["gm_w_s"] = dwt * jnp.tril(jnp.ones((CHUNK, CHUNK), F32))[None]
    g["gm_b_s"] = dbs_t.T

    do = _heads(datt).astype(BF16)
    delta = _attn_delta(sv["o"], do, tq)
    dq, dk, dv, dc_row = _attn_bwd(sv["q"], sv["k"], sv["v"], do, sv["lse"], delta, sv["c_col"], sv["c_row"], tq)
    dfl_t, db = _forget_bwd(dc_row.reshape(N_HEADS, s), sv["fl_t"], w["b_forget"])
    g["b_forget"] = db.reshape(1, N_HEADS)
    dz = jnp.concatenate([_unheads(dq).astype(BF16), _unheads(dk).astype(BF16), _unheads(dv).astype(BF16),
                          dgu, dgv, dfl_t.T.astype(BF16),
                          jnp.zeros((s, D_IN_PAD - F_OFF - N_HEADS), BF16)], axis=-1)
    dhn1 = _matmul(dz, w["w_in"], "nt", F32, "mm_dhn1")
    g["w_in"] = _matmul(sv["hn1"], dz, "tn", F32, "mm_din")

    def mix_pre_bwd(dh1, dhn1, h0, gpre):
        h0hat, r0 = _rms(h0)
        return dh1 + _rms_bwd(h0hat, r0, dhn1 * gpre), _colsum(dhn1 * h0hat)

    dh0, g["mix_pre_norm"] = _rowwise(mix_pre_bwd, [dh1, dhn1, sv["h0"]], [w["mix_pre_norm"]],
                                      [(d, F32)], [d], "mix_pre_bwd", 256)
    return dh0, g


ANY = pl.BlockSpec(memory_space=pl.ANY)


def _all_gather(x, name):
    def body(x_ref, out_ref, send_sems, recv_sems, local_sem):
        x, y, c = lax.axis_index("x"), lax.axis_index("y"), lax.axis_index("c")
        me, sibling = (x, y, c), (x, y, 1 - c)
        chips = [(1 - x, y), (x, 1 - y), (1 - x, 1 - y)]

        def rows(px, py, pc):
            return out_ref.at[4 * px + 2 * py + pc]

        def copy(kk, block, to, src=None):
            return pltpu.make_async_remote_copy(
                src_ref=rows(*block) if src is None else src, dst_ref=rows(*block),
                send_sem=send_sems.at[kk], recv_sem=recv_sems.at[kk], device_id=to, device_id_type=MESH)

        mine = pltpu.make_async_copy(x_ref, rows(*me), local_sem)
        mine.start()
        first = [copy(0, me, sibling, src=x_ref)]
        first += [copy(1 + j, me, (*chip, c), src=x_ref) for j, chip in enumerate(chips)]
        for cp in first:
            cp.start()
        passed = [copy(4 + j, (*chip, c), sibling) for j, chip in enumerate(chips)]
        for j, chip in enumerate(chips):
            copy(1 + j, (*chip, c), me).wait_recv()
            passed[j].start()
        copy(0, sibling, me).wait_recv()
        for j, chip in enumerate(chips):
            copy(4 + j, (*chip, 1 - c), me).wait_recv()
        for cp in first + passed:
            cp.wait_send()
        mine.wait()

    return pl.pallas_call(
        body, name=name, out_shape=jax.ShapeDtypeStruct((N_DEV,) + x.shape, x.dtype),
        in_specs=[ANY], out_specs=ANY,
        scratch_shapes=[pltpu.SemaphoreType.DMA((7,)), pltpu.SemaphoreType.DMA((7,)), pltpu.SemaphoreType.DMA],
    )(x)


def _sibling_exchange(grads):
    _, r, c = grads.shape

    def body(g_ref, got_ref, send_sems, recv_sems):
        x, y, cc = lax.axis_index("x"), lax.axis_index("y"), lax.axis_index("c")
        copies = [pltpu.make_async_remote_copy(
            src_ref=g_ref.at[2 * kk + (1 - cc)], dst_ref=got_ref.at[kk], send_sem=send_sems.at[kk],
            recv_sem=recv_sems.at[kk], device_id=(x, y, 1 - cc), device_id_type=MESH) for kk in range(4)]
        for cp in copies:
            cp.start()
        for cp in copies:
            cp.wait_recv()
        for cp in copies:
            cp.wait_send()

    return pl.pallas_call(
        body, name="rs_sibling_exchange", out_shape=jax.ShapeDtypeStruct((4, r, c), grads.dtype),
        in_specs=[ANY], out_specs=ANY,
        scratch_shapes=[pltpu.SemaphoreType.DMA((4,)), pltpu.SemaphoreType.DMA((4,))],
    )(grads)


def _chip_exchange(part):
    _, r, c = part.shape

    def body(p_ref, got_ref, send_sems, recv_sems):
        x, y, cc = lax.axis_index("x"), lax.axis_index("y"), lax.axis_index("c")
        chips = [(1 - x, y), (x, 1 - y), (1 - x, 1 - y)]
        copies = [pltpu.make_async_remote_copy(
            src_ref=p_ref.at[2 * cx + cy], dst_ref=got_ref.at[j], send_sem=send_sems.at[j],
            recv_sem=recv_sems.at[j], device_id=(cx, cy, cc), device_id_type=MESH) for j, (cx, cy) in enumerate(chips)]
        for cp in copies:
            cp.start()
        for cp in copies:
            cp.wait_recv()
        for cp in copies:
            cp.wait_send()

    return pl.pallas_call(
        body, name="rs_chip_exchange", out_shape=jax.ShapeDtypeStruct((3, r, c), part.dtype),
        in_specs=[ANY], out_specs=ANY,
        scratch_shapes=[pltpu.SemaphoreType.DMA((3,)), pltpu.SemaphoreType.DMA((3,))],
    )(part)


def _pair_sum(grads, got, core):
    _, r, c = grads.shape

    def body(core_ref, g_ref, got_ref, o_ref):
        o_ref[...] = (g_ref[...] + got_ref[...]).astype(BF16)

    blk = lambda fn: pl.BlockSpec((1, PACK_ROWS, c), fn)
    return pl.pallas_call(
        body, name="rs_pair_sum", out_shape=jax.ShapeDtypeStruct((4, r, c), BF16),
        grid_spec=pltpu.PrefetchScalarGridSpec(
            num_scalar_prefetch=1, grid=(4, r // PACK_ROWS),
            in_specs=[blk(lambda kk, i, core: (2 * kk + core[0], i, 0)), blk(lambda kk, i, core: (kk, i, 0))],
            out_specs=blk(lambda kk, i, core: (kk, i, 0))),
        compiler_params=_cparams("parallel", "parallel"),
    )(core, grads, got)


def _chip_sum(part, got, chip):
    _, r, c = part.shape

    def body(chip_ref, p_ref, got_ref, o_ref):
        acc = p_ref[0].astype(F32)
        for j in range(3):
            acc = acc + got_ref[j].astype(F32)
        o_ref[...] = acc

    return pl.pallas_call(
        body, name="rs_chip_sum", out_shape=jax.ShapeDtypeStruct((r, c), F32),
        grid_spec=pltpu.PrefetchScalarGridSpec(
            num_scalar_prefetch=1, grid=(r // PACK_ROWS,),
            in_specs=[pl.BlockSpec((1, PACK_ROWS, c), lambda i, chip: (chip[0], i, 0)),
                      pl.BlockSpec((3, PACK_ROWS, c), lambda i, chip: (0, i, 0))],
            out_specs=pl.BlockSpec((PACK_ROWS, c), lambda i, chip: (i, 0))),
        compiler_params=_cparams("parallel"),
    )(chip, part, got)


def _sum_devices(parts):
    _, r, c = parts.shape

    def body(p_ref, o_ref):
        acc = p_ref[0]
        for j in range(1, N_DEV):
            acc = acc + p_ref[j]
        o_ref[...] = acc

    return pl.pallas_call(
        body, name="small_sum", out_shape=jax.ShapeDtypeStruct((r, c), F32), grid=(r // SMALL_ROWS,),
        in_specs=[pl.BlockSpec((N_DEV, SMALL_ROWS, c), lambda i: (0, i, 0))],
        out_specs=pl.BlockSpec((SMALL_ROWS, c), lambda i: (i, 0)),
        compiler_params=_cparams("parallel"),
    )(parts)


def _adamw(w, g, m, v, name):
    def fn(w, g, m, v):
        m = ADAM_B1 * m + (1.0 - ADAM_B1) * g
        v = ADAM_B2 * v + (1.0 - ADAM_B2) * (g * g)
        m_hat = m / (1.0 - ADAM_B1 ** ADAM_STEP)
        v_hat = v / (1.0 - ADAM_B2 ** ADAM_STEP)
        return -ADAM_LR * (m_hat / (jnp.sqrt(v_hat) + ADAM_EPS) + ADAM_WD * w), m, v

    c = w.shape[1]
    return _rowwise(fn, [w, g, m, v], [], [(c, F32)] * 3, [], name, 512)


def _pack(pieces, cols, row_tile, lead=()):
    flat = jnp.concatenate(pieces, axis=-1)
    n = flat.shape[-1]
    total = -(-n // (cols * row_tile)) * cols * row_tile
    flat = jnp.pad(flat, [(0, 0)] * len(lead) + [(0, total - n)])
    return flat.reshape(lead + (total // cols, cols))


def _by_owner(a, axis):
    shape = a.shape
    a = a.reshape(shape[:axis] + (N_DEV, shape[axis] // N_DEV) + shape[axis + 1:])
    return jnp.moveaxis(a, axis, 0).reshape(N_DEV, -1)


def _from_owners(a, shard_shape, axis):
    a = a.reshape((N_DEV,) + shard_shape)
    a = jnp.moveaxis(a, 0, axis)
    return a.reshape(shard_shape[:axis] + (N_DEV * shard_shape[axis],) + shard_shape[axis + 1:])


def _pad_w_in(w_in):
    pad = jnp.zeros(w_in.shape[:-1] + (D_IN_PAD - D_IN,), w_in.dtype)
    return jnp.concatenate([w_in[..., :3 * D_ATT], w_in[..., 3 * D_ATT + N_HEADS:],
                            w_in[..., 3 * D_ATT:3 * D_ATT + N_HEADS], pad], axis=-1)


def _unpad_w_in(g):
    return jnp.concatenate([g[..., :3 * D_ATT], g[..., F_OFF:F_OFF + N_HEADS], g[..., 3 * D_ATT:F_OFF]], axis=-1)


def kernel(x, p, mix_pre_norm, mix_post_norm, w_in, b_forget, gm_v_norm, gm_w_s, gm_b_s, mix_out_norm, w_out, ffn_pre_norm, ffn_post_norm, w_ffn_in, w_ffn_out, w_ple, ple_norm, w_ple_gate, loss_target, m_mix_pre_norm, m_mix_post_norm, m_w_in, m_b_forget, m_gm_v_norm, m_gm_w_s, m_gm_b_s, m_mix_out_norm, m_w_out, m_ffn_pre_norm, m_ffn_post_norm, m_w_ffn_in, m_w_ffn_out, m_w_ple, m_ple_norm, m_w_ple_gate, v_mix_pre_norm, v_mix_post_norm, v_w_in, v_b_forget, v_gm_v_norm, v_gm_w_s, v_gm_b_s, v_mix_out_norm, v_w_out, v_ffn_pre_norm, v_ffn_post_norm, v_w_ffn_in, v_w_ffn_out, v_w_ple, v_ple_norm, v_w_ple_gate):
    given = dict(locals())
    weights = {n: given[n] for n in WEIGHT_ORDER}
    mom_m = {n: given["m_" + n] for n in WEIGHT_ORDER}
    mom_v = {n: given["v_" + n] for n in WEIGHT_ORDER}
    depth = w_in.shape[0]
    s, d = x.shape[1], x.shape[2]
    tq = _tile(s, ATT_BLOCK)
    core = lax.axis_index("c").astype(jnp.int32).reshape(1)
    chip = (2 * lax.axis_index("x") + lax.axis_index("y")).astype(jnp.int32).reshape(1)

    shard_shapes = {n: weights[n].shape[1:] for n in MATRIX_WEIGHTS}
    mine = _pack([weights[n].astype(BF16).reshape(-1) for n in MATRIX_WEIGHTS], PACK_COLS, PACK_ROWS)
    gathered = _all_gather(mine, "weights_all_gather").reshape(N_DEV, -1)
    full = {}
    off = 0
    for n in MATRIX_WEIGHTS:
        size = weights[n].size
        full[n] = _from_owners(gathered[:, off:off + size], weights[n].shape, SHARD_AXIS[n])
        off += size
    full["w_in"] = _pad_w_in(full["w_in"])
    tril = jnp.tril(jnp.ones((CHUNK, CHUNK), F32))

    layer_w = []
    for i in range(depth):
        wt = gm_w_s[i] * tril[None]
        lw = {n: full[n][i] for n in MATRIX_WEIGHTS}
        lw.update({n: weights[n][i][None] for n in ("mix_pre_norm", "mix_post_norm", "gm_v_norm", "mix_out_norm",
                                                    "ffn_pre_norm", "ffn_post_norm", "ple_norm")})
        lw.update(b_forget=b_forget[i][:, None], wt=wt.astype(BF16), wt_t=wt.transpose(0, 2, 1).astype(BF16),
                  bs_t=gm_b_s[i].T)
        layer_w.append(lw)

    h = x[0]
    saved = []
    for i in range(depth):
        h, sv = _layer_fwd(h, p[i, 0], layer_w[i], tq)
        saved.append(sv)

    def loss_head(y, t):
        err = y - t
        return err * (1.0 / d), _colsum(err * err)

    dh, sq = _rowwise(loss_head, [h, loss_target[0]], [], [(d, F32)], [d], "loss_head", 256)
    loss = lax.psum(0.5 * jnp.sum(sq) / d, AXES)

    layer_g = [None] * depth
    for i in reversed(range(depth)):
        dh, layer_g[i] = _layer_bwd(dh, saved[i], layer_w[i], tq)
    grad_x = dh[None]

    pieces = []
    for n in MATRIX_WEIGHTS:
        for i in range(depth):
            gi = layer_g[i][n]
            if n == "w_in":
                gi = _unpad_w_in(gi)
            pieces.append(_by_owner(gi, SHARD_AXIS[n] - 1))
    packed = _pack(pieces, PACK_COLS, PACK_ROWS, lead=(N_DEV,))
    from_sibling = _sibling_exchange(packed)
    part = _pair_sum(packed, from_sibling, core)
    from_chips = _chip_exchange(part)
    g_mine = _chip_sum(part, from_chips, chip).reshape(-1)

    grads, deltas, new_m, new_v = {}, {}, {}, {}
    off = 0
    for n in MATRIX_WEIGHTS:
        shp = weights[n].shape
        two_d = (shp[0] * shp[1], shp[2])
        gn = g_mine[off:off + weights[n].size].reshape(two_d)
        off += weights[n].size
        dl, nm, nv = _adamw(weights[n].reshape(two_d), gn, mom_m[n].reshape(two_d), mom_v[n].reshape(two_d),
                            "adamw_" + n)
        grads[n], deltas[n], new_m[n], new_v[n] = (a.reshape(shp) for a in (gn, dl, nm, nv))

    small_g = _pack([jnp.stack([layer_g[i][n].reshape(-1) for i in range(depth)]).reshape(-1)
                     for n in SMALL_WEIGHTS], SMALL_COLS, SMALL_ROWS)
    g_small = _sum_devices(_all_gather(small_g, "small_grads_all_gather"))
    pack_small = lambda t: _pack([t[n].reshape(-1) for n in SMALL_WEIGHTS], SMALL_COLS, SMALL_ROWS)
    dl, nm, nv = _adamw(pack_small(weights), g_small, pack_small(mom_m), pack_small(mom_v), "adamw_small")
    off = 0
    for n in SMALL_WEIGHTS:
        shp, size = weights[n].shape, weights[n].size
        grads[n], deltas[n], new_m[n], new_v[n] = (a.reshape(-1)[off:off + size].reshape(shp)
                                                   for a in (g_small, dl, nm, nv))
        off += size

    return (loss, grad_x, *[grads[n] for n in WEIGHT_ORDER], *[deltas[n] for n in WEIGHT_ORDER],
            *[new_m[n] for n in WEIGHT_ORDER], *[new_v[n] for n in WEIGHT_ORDER])
```

```python
import functools
import math

import jax
import jax.numpy as jnp
from jax import lax
from jax.experimental import pallas as pl
from jax.experimental.pallas import tpu as pltpu

F32 = jnp.float32
BF16 = jnp.bfloat16
MESH = pl.DeviceIdType.MESH
AXES = ("x", "y", "c")
N_DEV = 8

EPS = 1e-6
NEG_INF = -1e30
N_HEADS = 8
HEAD_DIM = 64
D_ATT = N_HEADS * HEAD_DIM
N_GROUPS = 8
GROUP_DIM = 64
D_GM = N_GROUPS * GROUP_DIM
CHUNK = 128
ATT_SCALE = HEAD_DIM ** -0.5
ATT_BLOCK = 512
D_IN = 3 * D_ATT + N_HEADS + 2 * D_GM
D_IN_PAD = 3 * D_ATT + 2 * D_GM + 128
F_OFF = 3 * D_ATT + 2 * D_GM

ADAM_LR = 0.001
ADAM_B1 = 0.9
ADAM_B2 = 0.999
ADAM_EPS = 1e-08
ADAM_WD = 0.01
ADAM_STEP = 10

LANE = 128
VMEM_LIMIT = 48 * 1024 * 1024
SMALL_COLS = 128
SMALL_ROWS = 512

MATRIX_WEIGHTS = ("w_in", "w_out", "w_ffn_in", "w_ffn_out", "w_ple", "w_ple_gate")
SMALL_WEIGHTS = ("mix_pre_norm", "mix_post_norm", "b_forget", "gm_v_norm", "gm_w_s", "gm_b_s",
                 "mix_out_norm", "ffn_pre_norm", "ffn_post_norm", "ple_norm")
WEIGHT_ORDER = ("mix_pre_norm", "mix_post_norm", "w_in", "b_forget", "gm_v_norm", "gm_w_s", "gm_b_s",
                "mix_out_norm", "w_out", "ffn_pre_norm", "ffn_post_norm", "w_ffn_in", "w_ffn_out",
                "w_ple", "ple_norm", "w_ple_gate")


def _tile(n, pref, unit=LANE):
    best = None
    t = unit
    while t <= min(n, pref):
        if n % t == 0:
            best = t
        t += unit
    return n if best is None else best


def _cparams(*semantics):
    return pltpu.CompilerParams(dimension_semantics=semantics or None, vmem_limit_bytes=VMEM_LIMIT)


NN = (((1,), (0,)), ((), ()))
NT = (((1,), (1,)), ((), ()))
TN = (((0,), (0,)), ((), ()))
_MM_AXES = {
    "nn": ("i", "k", "k", "j"), "nt": ("i", "k", "j", "k"), "tn": ("k", "i", "k", "j")}
_MM_DN = {"nn": NN, "nt": NT, "tn": TN}


def _mm(a, b, dims, out_dtype, name, a3=None, b3=None, o3=None, tm=512, tn=1024, tk=1024):
    ar, ac, br, bc = _MM_AXES[dims]
    letter = {"i": "m", "j": "n", "k": "k"}
    size = {}

    def measure(x, rows, cols, stacked):
        shape = x.shape
        if stacked is None:
            size.setdefault(letter[rows], shape[0])
            size.setdefault(letter[cols], shape[1])
        else:
            for ax, n in ((rows, shape[1]), (cols, shape[2])):
                size.setdefault(letter[ax], n * shape[0] if letter[ax] == stacked else n)

    measure(a, ar, ac, a3)
    measure(b, br, bc, b3)
    m, n, k = size["m"], size["n"], size["k"]
    slab = {}
    for x, stacked, rows, cols in ((a, a3, ar, ac), (b, b3, br, bc)):
        if stacked is not None:
            slab[stacked] = x.shape[1] if letter[rows] == stacked else x.shape[2]
    if o3 is not None:
        slab.setdefault(o3, slab.get(o3, None) or {"m": m, "n": n}[o3] // N_DEV)
    tile = {"m": slab.get("m") or _tile(m, tm), "n": slab.get("n") or _tile(n, tn), "k": slab.get("k") or _tile(k, tk)}
    nk = k // tile["k"]

    def spec(rows, cols, stacked):
        tr, tc = tile[letter[rows]], tile[letter[cols]]
        if stacked is None:
            return pl.BlockSpec((tr, tc), lambda i, j, kk: ({"i": i, "j": j, "k": kk}[rows], {"i": i, "j": j, "k": kk}[cols]))

        def imap(i, j, kk):
            g = {"i": i, "j": j, "k": kk}
            return (g[{"m": "i", "n": "j", "k": "k"}[stacked]],
                    0 if letter[rows] == stacked else g[rows], 0 if letter[cols] == stacked else g[cols])

        return pl.BlockSpec((1, tr, tc), imap)

    dn = _MM_DN[dims]

    def body(a_ref, b_ref, o_ref, acc_ref):
        kk = pl.program_id(2)

        @pl.when(kk == 0)
        def _():
            acc_ref[...] = jnp.zeros_like(acc_ref)

        av = a_ref[...] if a3 is None else a_ref[0]
        bv = b_ref[...] if b3 is None else b_ref[0]
        acc_ref[...] += lax.dot_general(av.astype(BF16), bv.astype(BF16), dn, preferred_element_type=F32)

        @pl.when(kk == nk - 1)
        def _():
            if o3 is None:
                o_ref[...] = acc_ref[...].astype(out_dtype)
            else:
                o_ref[0] = acc_ref[...].astype(out_dtype)

    if o3 is None:
        out_shape = (m, n)
    elif o3 == "m":
        out_shape = (m // tile["m"], tile["m"], n)
    else:
        out_shape = (n // tile["n"], m, tile["n"])
    return pl.pallas_call(
        body, name=name, out_shape=jax.ShapeDtypeStruct(out_shape, out_dtype),
        grid=(m // tile["m"], n // tile["n"], nk),
        in_specs=[spec(ar, ac, a3), spec(br, bc, b3)], out_specs=spec("i", "j", o3),
        scratch_shapes=[pltpu.VMEM((tile["m"], tile["n"]), F32)],
        compiler_params=_cparams("parallel", "parallel", "arbitrary"),
    )(a, b)


def _rowwise(fn, rows, vecs, outs, reds, name, ts):
    rows = [r if isinstance(r, tuple) else (r, 0, r.shape[1]) for r in rows]
    s = rows[0][0].shape[0]
    ts = _tile(s, ts, 8)
    nr, nv, no = len(rows), len(vecs), len(outs)

    def body(*refs):
        vals = fn(*[r[...] for r in refs[:nr + nv]])
        vals = vals if isinstance(vals, tuple) else (vals,)
        o_refs = refs[nr + nv:nr + nv + no]
        r_refs = refs[nr + nv + no:]
        for o_ref, val in zip(o_refs, vals[:no]):
            o_ref[...] = val.astype(o_ref.dtype)
        if r_refs:
            @pl.when(pl.program_id(0) == 0)
            def _():
                for r_ref in r_refs:
                    r_ref[...] = jnp.zeros_like(r_ref)

            for r_ref, val in zip(r_refs, vals[no:]):
                r_ref[...] += val

    in_specs = [pl.BlockSpec((ts, w), functools.partial(lambda i, cb: (i, cb), cb=cb)) for _, cb, w in rows]
    in_specs += [pl.BlockSpec(v.shape, lambda i: (0, 0)) for v in vecs]
    out_specs = [pl.BlockSpec((ts, c), lambda i: (i, 0)) for c, _ in outs]
    out_specs += [pl.BlockSpec((1, c), lambda i: (0, 0)) for c in reds]
    out_shape = [jax.ShapeDtypeStruct((s, c), dt) for c, dt in outs]
    out_shape += [jax.ShapeDtypeStruct((1, c), F32) for c in reds]
    return pl.pallas_call(
        body, name=name, out_shape=out_shape, grid=(s // ts,), in_specs=in_specs, out_specs=out_specs,
        compiler_params=_cparams("arbitrary" if reds else "parallel"),
    )(*[r[0] for r in rows], *vecs)


def _rms(x):
    r = lax.rsqrt(jnp.mean(x * x, axis=-1, keepdims=True) + EPS)
    return x * r, r


def _rms_bwd(xhat, r, dyg):
    return r * (dyg - xhat * jnp.mean(dyg * xhat, axis=-1, keepdims=True))


def _colsum(x):
    return jnp.sum(x, axis=0, keepdims=True)


def _sigmoid(x):
    return 1.0 / (1.0 + jnp.exp(-x))


GELU_C = math.sqrt(2.0 / math.pi)
GELU_A = 0.044715


def _gelu(x):
    return 0.5 * x * (1.0 + jnp.tanh(GELU_C * (x + GELU_A * x * x * x)))


def _gelu_grad(x):
    t = jnp.tanh(GELU_C * (x + GELU_A * x * x * x))
    return 0.5 * (1.0 + t) + 0.5 * x * (1.0 - t * t) * GELU_C * (1.0 + 3.0 * GELU_A * x * x)


def _swiglu_fwd(ab, ts):
    _, g, s, n = ab.shape
    ts = _tile(s, ts, 8)

    def body(ab_ref, t_ref):
        a = ab_ref[0, 0]
        t_ref[0] = (a * _sigmoid(a) * ab_ref[1, 0]).astype(BF16)

    return pl.pallas_call(
        body, name="swiglu", out_shape=jax.ShapeDtypeStruct((g, s, n), BF16), grid=(g, s // ts),
        in_specs=[pl.BlockSpec((2, 1, ts, n), lambda j, i: (0, j, i, 0))],
        out_specs=pl.BlockSpec((1, ts, n), lambda j, i: (j, i, 0)),
        compiler_params=_cparams("parallel", "parallel"),
    )(ab)


def _swiglu_bwd(ab, dt, ts):
    _, g, s, n = ab.shape
    ts = _tile(s, ts, 8)

    def body(ab_ref, dt_ref, t_ref, dab_ref):
        a = ab_ref[0, 0]
        b = ab_ref[1, 0]
        dt = dt_ref[0]
        sig = _sigmoid(a)
        silu = a * sig
        t_ref[0] = (silu * b).astype(BF16)
        dab_ref[0, 0] = (dt * b * (sig * (1.0 + a * (1.0 - sig)))).astype(BF16)
        dab_ref[1, 0] = (dt * silu).astype(BF16)

    both = pl.BlockSpec((2, 1, ts, n), lambda j, i: (0, j, i, 0))
    one = pl.BlockSpec((1, ts, n), lambda j, i: (j, i, 0))
    return pl.pallas_call(
        body, name="swiglu_bwd",
        out_shape=(jax.ShapeDtypeStruct((g, s, n), BF16), jax.ShapeDtypeStruct((2, g, s, n), BF16)),
        grid=(g, s // ts), in_specs=[both, one], out_specs=(one, both),
        compiler_params=_cparams("parallel", "parallel"),
    )(ab, dt)


def _forget_fwd(fl_t, b_col):
    h, s = fl_t.shape
    nb = s // LANE

    def body(fl_ref, b_ref, c_ref):
        upper = (lax.broadcasted_iota(jnp.int32, (LANE, LANE), 0)
                 <= lax.broadcasted_iota(jnp.int32, (LANE, LANE), 1)).astype(F32)

        def step(i, carry):
            x = fl_ref[i] + b_ref[...]
            lf = jnp.minimum(x, 0.0) - jnp.log(1.0 + jnp.exp(-jnp.abs(x)))
            cs = jnp.dot(lf, upper, precision=lax.Precision.HIGHEST, preferred_element_type=F32) + carry
            c_ref[i] = cs
            return cs[:, LANE - 1:LANE]

        lax.fori_loop(0, nb, step, jnp.zeros((h, 1), F32))

    out = pl.pallas_call(
        body, name="forget_fwd", out_shape=jax.ShapeDtypeStruct((nb, h, LANE), F32),
        compiler_params=_cparams(),
    )(fl_t.reshape(h, nb, LANE).transpose(1, 0, 2), b_col)
    return out.transpose(1, 0, 2).reshape(h, s)


def _forget_bwd(dc_t, fl_t, b_col):
    h, s = fl_t.shape
    nb = s // LANE

    def body(dc_ref, fl_ref, b_ref, dfl_ref, db_ref):
        lower = (lax.broadcasted_iota(jnp.int32, (LANE, LANE), 0)
                 >= lax.broadcasted_iota(jnp.int32, (LANE, LANE), 1)).astype(F32)

        def step(t, carry):
            tail, db = carry
            i = nb - 1 - t
            rc = jnp.dot(dc_ref[i], lower, precision=lax.Precision.HIGHEST, preferred_element_type=F32) + tail
            dfl = rc * (1.0 - _sigmoid(fl_ref[i] + b_ref[...]))
            dfl_ref[i] = dfl
            return rc[:, 0:1], db + jnp.sum(dfl, axis=1, keepdims=True)

        _, db = lax.fori_loop(0, nb, step, (jnp.zeros((h, 1), F32), jnp.zeros((h, 1), F32)))
        db_ref[...] = db

    blocked = lambda a: a.reshape(h, nb, LANE).transpose(1, 0, 2)
    dfl, db = pl.pallas_call(
        body, name="forget_bwd",
        out_shape=(jax.ShapeDtypeStruct((nb, h, LANE), F32), jax.ShapeDtypeStruct((h, 1), F32)),
        compiler_params=_cparams(),
    )(blocked(dc_t), blocked(fl_t), b_col)
    return dfl.transpose(1, 0, 2).reshape(h, s), db


N_PAIRS = N_HEADS // 2


def _causal_mask(t):
    return lax.broadcasted_iota(jnp.int32, (t, t), 0) >= lax.broadcasted_iota(jnp.int32, (t, t), 1)


def _head_lanes():
    return lax.broadcasted_iota(jnp.int32, (1, 2 * HEAD_DIM), 1) < HEAD_DIM


def _pick(x2, first, hh):
    zero = jnp.zeros_like(x2)
    return jnp.where(first, x2, zero) if hh == 0 else jnp.where(first, zero, x2)


def _attn_fwd(qkv, c_col, c_row, tq):
    s = qkv.shape[0]
    nq = s // tq
    w = 2 * HEAD_DIM

    def body(q_ref, k_ref, v_ref, cc_ref, cr_ref, o_ref, lse_ref):
        i = pl.program_id(1)
        first = _head_lanes()
        q2 = q_ref[...]
        qs = [_pick(q2, first, hh) for hh in range(2)]

        def block(j, carry, masked):
            off = pl.multiple_of(j * tq, tq)
            k2 = k_ref[pl.ds(off, tq), :]
            v2 = v_ref[pl.ds(off, tq), :]
            new = []
            for hh in range(2):
                m, l, acc = carry[hh]
                sc = lax.dot_general(qs[hh], k2, NT, preferred_element_type=F32) * ATT_SCALE + cc_ref[hh] - cr_ref[hh, j]
                if masked:
                    sc = jnp.where(_causal_mask(tq), sc, NEG_INF)
                m_new = jnp.maximum(m, jnp.max(sc, axis=-1, keepdims=True))
                alpha = jnp.exp(m - m_new)
                p = jnp.exp(sc - m_new)
                l = alpha * l + jnp.sum(p, axis=-1, keepdims=True)
                p_hi = p.astype(BF16)
                p_lo = (p - p_hi.astype(F32)).astype(BF16)
                acc = (alpha * acc + jnp.dot(p_hi, v2, preferred_element_type=F32)
                       + jnp.dot(p_lo, v2, preferred_element_type=F32))
                new.append((m_new, l, acc))
            return tuple(new)

        one = (jnp.full((tq, 1), NEG_INF, F32), jnp.zeros((tq, 1), F32), jnp.zeros((tq, w), F32))
        carry = lax.fori_loop(0, i, lambda j, c: block(j, c, False), (one, one))
        (m0, l0, a0), (m1, l1, a1) = block(i, carry, True)
        o_ref[...] = jnp.where(first, a0 / l0, a1 / l1)
        lse_ref[0] = m0 + jnp.log(l0)
        lse_ref[1] = m1 + jnp.log(l1)

    return pl.pallas_call(
        body, name="attn_fwd",
        out_shape=(jax.ShapeDtypeStruct((s, D_ATT), F32), jax.ShapeDtypeStruct((N_HEADS, s, 1), F32)),
        grid=(N_PAIRS, nq),
        in_specs=[pl.BlockSpec((tq, w), lambda hp, i: (i, hp)),
                  pl.BlockSpec((s, w), lambda hp, i: (0, N_PAIRS + hp)),
                  pl.BlockSpec((s, w), lambda hp, i: (0, 2 * N_PAIRS + hp)),
                  pl.BlockSpec((2, tq, 1), lambda hp, i: (hp, i, 0)),
                  pl.BlockSpec((2, nq, 1, tq), lambda hp, i: (hp, 0, 0, 0))],
        out_specs=(pl.BlockSpec((tq, w), lambda hp, i: (i, hp)),
                   pl.BlockSpec((2, tq, 1), lambda hp, i: (hp, i, 0))),
        compiler_params=_cparams("parallel", "parallel"),
    )(qkv, qkv, qkv, c_col, c_row)


def _attn_delta(o, do, tq):
    s = o.shape[0]
    w = 2 * HEAD_DIM

    def body(o_ref, do_ref, d_ref):
        first = _head_lanes()
        prod = o_ref[...] * do_ref[...].astype(F32)
        d_ref[0] = jnp.sum(_pick(prod, first, 0), axis=-1, keepdims=True)
        d_ref[1] = jnp.sum(_pick(prod, first, 1), axis=-1, keepdims=True)

    blk = pl.BlockSpec((tq, w), lambda hp, i: (i, hp))
    return pl.pallas_call(
        body, name="attn_delta", out_shape=jax.ShapeDtypeStruct((N_HEADS, s, 1), F32), grid=(N_PAIRS, s // tq),
        in_specs=[blk, blk], out_specs=pl.BlockSpec((2, tq, 1), lambda hp, i: (hp, i, 0)),
        compiler_params=_cparams("parallel", "parallel"),
    )(o, do)


def _attn_bwd(qkv, do, lse, delta, c_col, c_row, tq):
    s = qkv.shape[0]
    nq = s // tq
    w = 2 * HEAD_DIM

    def body(q_ref, do_ref, lse_ref, dl_ref, cc_ref, k_ref, v_ref, cr_ref, dq_ref, dk_ref, dv_ref, dc_ref):
        j = pl.program_id(1)
        first = _head_lanes()

        @pl.when(j == 0)
        def _():
            dq_ref[...] = jnp.zeros_like(dq_ref)

        k2 = k_ref[...]
        v2 = v_ref[...]

        def step(i, carry, masked):
            off = pl.multiple_of(i * tq, tq)
            rows = pl.ds(off, tq)
            q2 = q_ref[rows, :]
            do2 = do_ref[rows, :]
            new, dqs = [], []
            for hh in range(2):
                dk, dv, dcs = carry[hh]
                sc = (lax.dot_general(_pick(q2, first, hh), k2, NT, preferred_element_type=F32) * ATT_SCALE
                      + cc_ref[hh, rows, :] - cr_ref[hh, 0])
                if masked:
                    sc = jnp.where(_causal_mask(tq), sc, NEG_INF)
                p = jnp.exp(sc - lse_ref[hh, rows, :])
                dv = dv + lax.dot_general(p.astype(BF16), do2, TN, preferred_element_type=F32)
                dp = lax.dot_general(_pick(do2, first, hh), v2, NT, preferred_element_type=F32)
                ds = p * (dp - dl_ref[hh, rows, :])
                dsb = ds.astype(BF16)
                dk = dk + lax.dot_general(dsb, q2, TN, preferred_element_type=F32)
                dqs.append(jnp.dot(dsb, k2, preferred_element_type=F32))
                new.append((dk, dv, dcs + jnp.sum(ds, axis=0, keepdims=True)))
            dq_ref[rows, :] += jnp.where(first, dqs[0], dqs[1]) * ATT_SCALE
            return tuple(new)

        one = (jnp.zeros((tq, w), F32), jnp.zeros((tq, w), F32), jnp.zeros((1, tq), F32))
        carry = step(j, (one, one), True)
        (dk0, dv0, dc0), (dk1, dv1, dc1) = lax.fori_loop(j + 1, nq, lambda i, c: step(i, c, False), carry)
        dk_ref[...] = jnp.where(first, dk0, dk1) * ATT_SCALE
        dv_ref[...] = jnp.where(first, dv0, dv1)
        dc_ref[0, 0] = -dc0
        dc_ref[1, 0] = -dc1

    whole_cols = lambda off: pl.BlockSpec((s, w), lambda hp, j: (0, off + hp))
    whole_heads = pl.BlockSpec((2, s, 1), lambda hp, j: (hp, 0, 0))
    blk = lambda off: pl.BlockSpec((tq, w), lambda hp, j: (j, off + hp))
    crow = pl.BlockSpec((2, 1, 1, tq), lambda hp, j: (hp, j, 0, 0))
    return pl.pallas_call(
        body, name="attn_bwd",
        out_shape=(jax.ShapeDtypeStruct((s, D_ATT), F32), jax.ShapeDtypeStruct((s, D_ATT), F32),
                   jax.ShapeDtypeStruct((s, D_ATT), F32), jax.ShapeDtypeStruct((N_HEADS, nq, 1, tq), F32)),
        grid=(N_PAIRS, nq),
        in_specs=[whole_cols(0), whole_cols(0), whole_heads, whole_heads, whole_heads,
                  blk(N_PAIRS), blk(2 * N_PAIRS), crow],
        out_specs=(whole_cols(0), blk(0), blk(0), crow),
        compiler_params=_cparams("parallel", "arbitrary"),
    )(qkv, do, lse, delta, c_col, qkv, qkv, c_row)


def _gm_group_norm(vg):
    mu = jnp.mean(vg, axis=-1, keepdims=True)
    d = vg - mu
    rstd = lax.rsqrt(jnp.mean(d * d, axis=-1, keepdims=True) + EPS)
    return d * rstd, rstd


def _gmlp_fwd(z, wt, bs_t, vgain):
    s = z.shape[0]

    def body(gu_ref, gv_ref, wt_ref, bs_ref, vg_ref, o_ref):
        for g in range(N_GROUPS):
            sl = slice(g * GROUP_DIM, (g + 1) * GROUP_DIM)
            vhat, _ = _gm_group_norm(_gelu(gv_ref[:, sl]))
            vn = vhat * vg_ref[:, sl]
            mixed = jnp.dot(wt_ref[g], vn.astype(BF16), preferred_element_type=F32) + bs_ref[:, g:g + 1]
            o_ref[:, sl] = _gelu(gu_ref[:, sl]) * mixed

    full = lambda a: pl.BlockSpec(a.shape, lambda n: (0,) * a.ndim)
    return pl.pallas_call(
        body, name="gmlp_fwd", out_shape=jax.ShapeDtypeStruct((s, D_GM), F32), grid=(s // CHUNK,),
        in_specs=[pl.BlockSpec((CHUNK, D_GM), lambda n: (n, 3)), pl.BlockSpec((CHUNK, D_GM), lambda n: (n, 4)),
                  full(wt), full(bs_t), full(vgain)],
        out_specs=pl.BlockSpec((CHUNK, D_GM), lambda n: (n, 0)),
        compiler_params=_cparams("parallel"),
    )(z, z, wt, bs_t, vgain)


def _gmlp_bwd(z, dgm, wt, wt_t, bs_t, vgain):
    s = z.shape[0]

    def body(gu_ref, gv_ref, dgm_ref, wt_ref, wtt_ref, bs_ref, vg_ref, dgu_ref, dgv_ref, dwt_ref, dbs_ref, dvg_ref):
        @pl.when(pl.program_id(0) == 0)
        def _():
            dwt_ref[...] = jnp.zeros_like(dwt_ref)
            dbs_ref[...] = jnp.zeros_like(dbs_ref)
            dvg_ref[...] = jnp.zeros_like(dvg_ref)

        for g in range(N_GROUPS):
            sl = slice(g * GROUP_DIM, (g + 1) * GROUP_DIM)
            gu = gu_ref[:, sl]
            gv = gv_ref[:, sl]
            dgm = dgm_ref[:, sl]
            vhat, rstd = _gm_group_norm(_gelu(gv))
            gain = vg_ref[:, sl]
            vn = (vhat * gain).astype(BF16)
            mixed = jnp.dot(wt_ref[g], vn, preferred_element_type=F32) + bs_ref[:, g:g + 1]
            dgu_ref[:, sl] = (dgm * mixed * _gelu_grad(gu)).astype(BF16)
            dmixed = dgm * _gelu(gu)
            dmb = dmixed.astype(BF16)
            dbs_ref[:, g:g + 1] += jnp.sum(dmixed, axis=-1, keepdims=True)
            dwt_ref[g] += lax.dot_general(dmb, vn, NT, preferred_element_type=F32)
            dvn = jnp.dot(wtt_ref[g], dmb, preferred_element_type=F32)
            dvg_ref[:, sl] += _colsum(dvn * vhat)
            dvhat = dvn * gain
            dvf = rstd * (dvhat - jnp.mean(dvhat, axis=-1, keepdims=True)
                          - vhat * jnp.mean(dvhat * vhat, axis=-1, keepdims=True))
            dgv_ref[:, sl] = (dvf * _gelu_grad(gv)).astype(BF16)

    full = lambda a: pl.BlockSpec(a.shape, lambda n: (0,) * a.ndim)
    chunk = pl.BlockSpec((CHUNK, D_GM), lambda n: (n, 0))
    return pl.pallas_call(
        body, name="gmlp_bwd",
        out_shape=(jax.ShapeDtypeStruct((s, D_GM), BF16), jax.ShapeDtypeStruct((s, D_GM), BF16),
                   jax.ShapeDtypeStruct(wt.shape, F32), jax.ShapeDtypeStruct(bs_t.shape, F32),
                   jax.ShapeDtypeStruct(vgain.shape, F32)),
        grid=(s // CHUNK,),
        in_specs=[pl.BlockSpec((CHUNK, D_GM), lambda n: (n, 3)), pl.BlockSpec((CHUNK, D_GM), lambda n: (n, 4)),
                  chunk, full(wt), full(wt_t), full(bs_t), full(vgain)],
        out_specs=(chunk, chunk, full(wt), full(bs_t), full(vgain)),
        compiler_params=_cparams("arbitrary"),
    )(z, z, dgm, wt, wt_t, bs_t, vgain)


def _layer_fwd(h0, p_i, w, tq):
    s, d = h0.shape
    nq = s // tq
    sv = {"h0": h0}

    (hn1,) = _rowwise(lambda h, g: _rms(h)[0] * g, [h0], [w["mix_pre_norm"]], [(d, BF16)], [], "pre_mix", 256)
    z = _mm(hn1, w["w_in"], "nn", F32, "mm_in")
    fl_t = z[:, F_OFF:F_OFF + N_HEADS].T
    c_t = _forget_fwd(fl_t, w["b_forget"])
    c_col = c_t[:, :, None]
    c_row = c_t.reshape(N_HEADS, nq, 1, tq)
    qkv = z[:, :3 * D_ATT].astype(BF16)
    att, lse = _attn_fwd(qkv, c_col, c_row, tq)
    gm = _gmlp_fwd(z, w["wt"], w["bs_t"], w["gm_v_norm"])

    def mix_out(att, gm, g):
        return jnp.concatenate([_rms(att)[0] * g[:, :D_ATT], _rms(gm)[0] * g[:, D_ATT:]], axis=-1)

    (mc,) = _rowwise(mix_out, [att, gm], [w["mix_out_norm"]], [(D_ATT + D_GM, BF16)], [], "mix_out", 256)
    y1 = _mm(mc, w["w_out"], "nn", F32, "mm_out")

    def post_mix(h0, y1, gpost, gpre):
        h1 = h0 + _rms(y1)[0] * gpost
        return h1, _rms(h1)[0] * gpre

    h1, hn2 = _rowwise(post_mix, [h0, y1], [w["mix_post_norm"], w["ffn_pre_norm"]],
                       [(d, F32), (d, BF16)], [], "post_mix", 256)
    ab = _mm(hn2, w["w_ffn_in"], "nn", F32, "mm_ffn_in", b3="n", o3="n")
    ab = ab.reshape((2, N_DEV // 2) + ab.shape[1:])
    t = _swiglu_fwd(ab, 256)
    y2 = _mm(t, w["w_ffn_out"], "nn", F32, "mm_ffn_out", a3="k", b3="k")

    def post_ffn(h1, y2, g):
        h2 = h1 + _rms(y2)[0] * g
        return h2, _rms(h2)[0]

    h2, hr = _rowwise(post_ffn, [h1, y2], [w["ffn_post_norm"]], [(d, F32), (d, BF16)], [], "post_ffn", 256)
    gl = _mm(hr, w["w_ple_gate"], "nn", F32, "mm_gate")
    pe = _mm(p_i, w["w_ple"], "nn", F32, "mm_ple", b3="n")
    (h3,) = _rowwise(lambda h2, gl, pe, g: h2 + _sigmoid(gl) * (_rms(pe)[0] * g), [h2, gl, pe], [w["ple_norm"]],
                     [(d, F32)], [], "ple_out", 256)
    sv.update(hn1=hn1, z=z, fl_t=fl_t, c_col=c_col, c_row=c_row, qkv=qkv, lse=lse, att=att, gm=gm,
              mc=mc, y1=y1, h1=h1, hn2=hn2, ab=ab, y2=y2, h2=h2, hr=hr, gl=gl, pe=pe, p_i=p_i)
    return h3, sv


def _layer_bwd(dh3, sv, w, tq):
    s, d = dh3.shape
    g = {}
    by_rows = lambda a: a.reshape(N_DEV, -1, a.shape[-1])

    def ple_bwd(dh3, gl, pe, gple):
        gate = _sigmoid(gl)
        pehat, rpe = _rms(pe)
        dgl = dh3 * (pehat * gple) * gate * (1.0 - gate)
        de = dh3 * gate
        return dgl, _rms_bwd(pehat, rpe, de * gple), _colsum(de * pehat)

    dgl, dpe, g["ple_norm"] = _rowwise(ple_bwd, [dh3, sv["gl"], sv["pe"]], [w["ple_norm"]],
                                       [(d, BF16), (d, BF16)], [d], "ple_bwd", 256)
    g["w_ple_gate"] = by_rows(_mm(sv["hr"], dgl, "tn", F32, "mm_dgate"))
    dhr = _mm(dgl, w["w_ple_gate"], "nt", F32, "mm_dhr")
    g["w_ple"] = _mm(sv["p_i"], dpe, "tn", F32, "mm_dple", o3="n")

    def ffn_post_bwd(dh3, dhr, h2, y2, gpost):
        h2hat, r2 = _rms(h2)
        dh2 = dh3 + _rms_bwd(h2hat, r2, dhr)
        y2hat, ry = _rms(y2)
        return dh2, _rms_bwd(y2hat, ry, dh2 * gpost), _colsum(dh2 * y2hat)

    dh2, dy2, g["ffn_post_norm"] = _rowwise(ffn_post_bwd, [dh3, dhr, sv["h2"], sv["y2"]], [w["ffn_post_norm"]],
                                            [(d, F32), (d, BF16)], [d], "ffn_post_bwd", 256)
    dt = _mm(dy2, w["w_ffn_out"], "nt", F32, "mm_dt", b3="n", o3="n")
    t, dab = _swiglu_bwd(sv["ab"], dt, 256)
    dab = dab.reshape((N_DEV,) + dab.shape[2:])
    g["w_ffn_out"] = by_rows(_mm(t, dy2, "tn", F32, "mm_dffn_out", a3="m", o3="m"))
    dhn2 = _mm(dab, w["w_ffn_in"], "nt", F32, "mm_dhn2", a3="k", b3="k")
    g["w_ffn_in"] = _mm(sv["hn2"], dab, "tn", F32, "mm_dffn_in", b3="n", o3="n")

    def mix_post_bwd(dh2, dhn2, h1, y1, gpre, gpost):
        h1hat, r1 = _rms(h1)
        dh1 = dh2 + _rms_bwd(h1hat, r1, dhn2 * gpre)
        y1hat, ry = _rms(y1)
        return dh1, _rms_bwd(y1hat, ry, dh1 * gpost), _colsum(dhn2 * h1hat), _colsum(dh1 * y1hat)

    dh1, dy1, g["ffn_pre_norm"], g["mix_post_norm"] = _rowwise(
        mix_post_bwd, [dh2, dhn2, sv["h1"], sv["y1"]], [w["ffn_pre_norm"], w["mix_post_norm"]],
        [(d, F32), (d, BF16)], [d, d], "mix_post_bwd", 256)
    dmc = _mm(dy1, w["w_out"], "nt", F32, "mm_dmc")
    g["w_out"] = by_rows(_mm(sv["mc"], dy1, "tn", F32, "mm_dout"))

    def mix_out_bwd(da, dg, att, gm, gain):
        atthat, ra = _rms(att)
        gmhat, rg = _rms(gm)
        dgain = jnp.concatenate([_colsum(da * atthat), _colsum(dg * gmhat)], axis=-1)
        return _rms_bwd(atthat, ra, da * gain[:, :D_ATT]), _rms_bwd(gmhat, rg, dg * gain[:, D_ATT:]), dgain

    datt, dgm, g["mix_out_norm"] = _rowwise(
        mix_out_bwd, [(dmc, 0, D_ATT), (dmc, 1, D_GM), sv["att"], sv["gm"]], [w["mix_out_norm"]],
        [(D_ATT, BF16), (D_GM, F32)], [D_ATT + D_GM], "mix_out_bwd", 256)

    dgu, dgv, dwt, dbs_t, g["gm_v_norm"] = _gmlp_bwd(sv["z"], dgm, w["wt"], w["wt_t"], w["bs_t"], w["gm_v_norm"])
    g["gm_w_s"] = dwt * jnp.tril(jnp.ones((CHUNK, CHUNK), F32))[None]
    g["gm_b_s"] = dbs_t.T

    delta = _attn_delta(sv["att"], datt, tq)
    dq, dk, dv, dc_row = _attn_bwd(sv["qkv"], datt, sv["lse"], delta, sv["c_col"], sv["c_row"], tq)
    dfl_t, db = _forget_bwd(dc_row.reshape(N_HEADS, s), sv["fl_t"], w["b_forget"])
    g["b_forget"] = db.reshape(1, N_HEADS)
    dz = jnp.concatenate([dq.astype(BF16), dk.astype(BF16), dv.astype(BF16), dgu, dgv, dfl_t.T.astype(BF16),
                          jnp.zeros((s, D_IN_PAD - F_OFF - N_HEADS), BF16)], axis=-1)
    dhn1 = _mm(dz, w["w_in"], "nt", F32, "mm_dhn1")
    din = _mm(sv["hn1"], dz, "tn", F32, "mm_din")
    din = jnp.concatenate([din[:, :3 * D_ATT], din[:, F_OFF:F_OFF + N_HEADS], din[:, 3 * D_ATT:F_OFF]], axis=-1)
    n_in = D_IN // N_DEV
    g["w_in"] = jnp.stack([din[:, j * n_in:(j + 1) * n_in] for j in range(N_DEV)])

    def mix_pre_bwd(dh1, dhn1, h0, gpre):
        h0hat, r0 = _rms(h0)
        return dh1 + _rms_bwd(h0hat, r0, dhn1 * gpre), _colsum(dhn1 * h0hat)

    dh0, g["mix_pre_norm"] = _rowwise(mix_pre_bwd, [dh1, dhn1, sv["h0"]], [w["mix_pre_norm"]],
                                      [(d, F32)], [d], "mix_pre_bwd", 256)
    return dh0, g


ANY = pl.BlockSpec(memory_space=pl.ANY)


def _all_gather(xs, layer, name):
    n = len(xs)

    def body(*refs):
        x_refs, out_refs = refs[:n], refs[n:2 * n]
        send_sems, recv_sems, local_sems = refs[2 * n:]
        x, y, c = lax.axis_index("x"), lax.axis_index("y"), lax.axis_index("c")
        me, sibling = (x, y, c), (x, y, 1 - c)
        chips = [(1 - x, y), (x, 1 - y), (1 - x, 1 - y)]

        def shard(a):
            return x_refs[a] if layer is None else x_refs[a].at[layer]

        def rows(a, px, py, pc):
            return out_refs[a].at[4 * px + 2 * py + pc]

        def copy(a, kk, block, to, from_shard=False):
            return pltpu.make_async_remote_copy(
                src_ref=shard(a) if from_shard else rows(a, *block), dst_ref=rows(a, *block),
                send_sem=send_sems.at[7 * a + kk], recv_sem=recv_sems.at[7 * a + kk],
                device_id=to, device_id_type=MESH)

        mine = [pltpu.make_async_copy(shard(a), rows(a, *me), local_sems.at[a]) for a in range(n)]
        for cp in mine:
            cp.start()
        first = []
        for a in range(n):
            first.append(copy(a, 0, me, sibling, from_shard=True))
            first += [copy(a, 1 + j, me, (*chip, c), from_shard=True) for j, chip in enumerate(chips)]
        for cp in first:
            cp.start()
        passed = []
        for j, chip in enumerate(chips):
            for a in range(n):
                copy(a, 1 + j, (*chip, c), me).wait_recv()
                passed.append(copy(a, 4 + j, (*chip, c), sibling))
                passed[-1].start()
        for a in range(n):
            copy(a, 0, sibling, me).wait_recv()
        for j, chip in enumerate(chips):
            for a in range(n):
                copy(a, 4 + j, (*chip, 1 - c), me).wait_recv()
        for cp in first + passed:
            cp.wait_send()
        for cp in mine:
            cp.wait()

    shapes = [x.shape if layer is None else x.shape[1:] for x in xs]
    return pl.pallas_call(
        body, name=name, out_shape=[jax.ShapeDtypeStruct((N_DEV,) + sh, x.dtype) for sh, x in zip(shapes, xs)],
        in_specs=[ANY] * n, out_specs=[ANY] * n,
        scratch_shapes=[pltpu.SemaphoreType.DMA((7 * n,)), pltpu.SemaphoreType.DMA((7 * n,)),
                        pltpu.SemaphoreType.DMA((n,))],
    )(*xs)


def _sibling_exchange(grads):
    n = len(grads)

    def body(*refs):
        g_refs, got_refs = refs[:n], refs[n:2 * n]
        send_sems, recv_sems = refs[2 * n:]
        x, y, c = lax.axis_index("x"), lax.axis_index("y"), lax.axis_index("c")
        copies = [pltpu.make_async_remote_copy(
            src_ref=g_refs[a].at[2 * kk + (1 - c)], dst_ref=got_refs[a].at[kk], send_sem=send_sems.at[4 * a + kk],
            recv_sem=recv_sems.at[4 * a + kk], device_id=(x, y, 1 - c), device_id_type=MESH)
            for a in range(n) for kk in range(4)]
        for cp in copies:
            cp.start()
        for cp in copies:
            cp.wait_recv()
        for cp in copies:
            cp.wait_send()

    return pl.pallas_call(
        body, name="rs_sibling_exchange",
        out_shape=[jax.ShapeDtypeStruct((4,) + g.shape[1:], g.dtype) for g in grads],
        in_specs=[ANY] * n, out_specs=[ANY] * n,
        scratch_shapes=[pltpu.SemaphoreType.DMA((4 * n,)), pltpu.SemaphoreType.DMA((4 * n,))],
    )(*grads)


def _chip_exchange(parts):
    n = len(parts)

    def body(*refs):
        p_refs, got_refs = refs[:n], refs[n:2 * n]
        send_sems, recv_sems = refs[2 * n:]
        x, y, c = lax.axis_index("x"), lax.axis_index("y"), lax.axis_index("c")
        chips = [(1 - x, y), (x, 1 - y), (1 - x, 1 - y)]
        copies = [pltpu.make_async_remote_copy(
            src_ref=p_refs[a].at[2 * cx + cy], dst_ref=got_refs[a].at[j], send_sem=send_sems.at[3 * a + j],
            recv_sem=recv_sems.at[3 * a + j], device_id=(cx, cy, c), device_id_type=MESH)
            for a in range(n) for j, (cx, cy) in enumerate(chips)]
        for cp in copies:
            cp.start()
        for cp in copies:
            cp.wait_recv()
        for cp in copies:
            cp.wait_send()

    return pl.pallas_call(
        body, name="rs_chip_exchange",
        out_shape=[jax.ShapeDtypeStruct((3,) + p.shape[1:], p.dtype) for p in parts],
        in_specs=[ANY] * n, out_specs=[ANY] * n,
        scratch_shapes=[pltpu.SemaphoreType.DMA((3 * n,)), pltpu.SemaphoreType.DMA((3 * n,))],
    )(*parts)


def _pair_sum(grads, got, core, name):
    _, a, b = grads.shape
    ta = _tile(a, 512, 16)

    def body(core_ref, g_ref, got_ref, o_ref):
        o_ref[...] = (g_ref[...] + got_ref[...]).astype(BF16)

    blk = lambda fn: pl.BlockSpec((1, ta, b), fn)
    return pl.pallas_call(
        body, name=name, out_shape=jax.ShapeDtypeStruct((4, a, b), BF16),
        grid_spec=pltpu.PrefetchScalarGridSpec(
            num_scalar_prefetch=1, grid=(4, a // ta),
            in_specs=[blk(lambda kk, i, core: (2 * kk + core[0], i, 0)), blk(lambda kk, i, core: (kk, i, 0))],
            out_specs=blk(lambda kk, i, core: (kk, i, 0))),
        compiler_params=_cparams("parallel", "parallel"),
    )(core, grads, got)


def _sum_devices(parts):
    _, r, c = parts.shape

    def body(p_ref, o_ref):
        acc = p_ref[0]
        for j in range(1, N_DEV):
            acc = acc + p_ref[j]
        o_ref[...] = acc

    return pl.pallas_call(
        body, name="small_sum", out_shape=jax.ShapeDtypeStruct((r, c), F32), grid=(r // SMALL_ROWS,),
        in_specs=[pl.BlockSpec((N_DEV, SMALL_ROWS, c), lambda i: (0, i, 0))],
        out_specs=pl.BlockSpec((SMALL_ROWS, c), lambda i: (i, 0)),
        compiler_params=_cparams("parallel"),
    )(parts)


def _adamw_math(w, g, m, v):
    m = ADAM_B1 * m + (1.0 - ADAM_B1) * g
    v = ADAM_B2 * v + (1.0 - ADAM_B2) * (g * g)
    m_hat = m / (1.0 - ADAM_B1 ** ADAM_STEP)
    v_hat = v / (1.0 - ADAM_B2 ** ADAM_STEP)
    return -ADAM_LR * (m_hat / (jnp.sqrt(v_hat) + ADAM_EPS) + ADAM_WD * w), m, v


def _adamw_shard(w, m, v, part, got, chip, layer, name):
    _, a, b = w.shape
    ta = _tile(a, 256, 16)

    def body(chip_ref, w_ref, m_ref, v_ref, p_ref, got_ref, g_ref, d_ref, nm_ref, nv_ref):
        g = p_ref[0].astype(F32)
        for j in range(3):
            g = g + got_ref[j].astype(F32)
        g_ref[...] = g
        d_ref[...], nm_ref[...], nv_ref[...] = _adamw_math(w_ref[0], g, m_ref[0], v_ref[0])

    mine = pl.BlockSpec((1, ta, b), lambda i, chip: (layer, i, 0))
    out = pl.BlockSpec((ta, b), lambda i, chip: (i, 0))
    return pl.pallas_call(
        body, name=name, out_shape=[jax.ShapeDtypeStruct((a, b), F32)] * 4,
        grid_spec=pltpu.PrefetchScalarGridSpec(
            num_scalar_prefetch=1, grid=(a // ta,),
            in_specs=[mine, mine, mine, pl.BlockSpec((1, ta, b), lambda i, chip: (chip[0], i, 0)),
                      pl.BlockSpec((3, ta, b), lambda i, chip: (0, i, 0))],
            out_specs=[out] * 4),
        compiler_params=_cparams("parallel"),
    )(chip, w, m, v, part, got)


def _pack_small(pieces):
    flat = jnp.concatenate([p.reshape(-1) for p in pieces])
    total = -(-flat.shape[0] // (SMALL_COLS * SMALL_ROWS)) * SMALL_COLS * SMALL_ROWS
    return jnp.pad(flat, (0, total - flat.shape[0])).reshape(-1, SMALL_COLS)


def kernel(x, p, mix_pre_norm, mix_post_norm, w_in, b_forget, gm_v_norm, gm_w_s, gm_b_s, mix_out_norm, w_out, ffn_pre_norm, ffn_post_norm, w_ffn_in, w_ffn_out, w_ple, ple_norm, w_ple_gate, loss_target, m_mix_pre_norm, m_mix_post_norm, m_w_in, m_b_forget, m_gm_v_norm, m_gm_w_s, m_gm_b_s, m_mix_out_norm, m_w_out, m_ffn_pre_norm, m_ffn_post_norm, m_w_ffn_in, m_w_ffn_out, m_w_ple, m_ple_norm, m_w_ple_gate, v_mix_pre_norm, v_mix_post_norm, v_w_in, v_b_forget, v_gm_v_norm, v_gm_w_s, v_gm_b_s, v_mix_out_norm, v_w_out, v_ffn_pre_norm, v_ffn_post_norm, v_w_ffn_in, v_w_ffn_out, v_w_ple, v_ple_norm, v_w_ple_gate):
    given = dict(locals())
    weights = {n: given[n] for n in WEIGHT_ORDER}
    mom_m = {n: given["m_" + n] for n in WEIGHT_ORDER}
    mom_v = {n: given["v_" + n] for n in WEIGHT_ORDER}
    depth = w_in.shape[0]
    s, d = x.shape[1], x.shape[2]
    tq = _tile(s, ATT_BLOCK)
    core = lax.axis_index("c").astype(jnp.int32).reshape(1)
    chip = (2 * lax.axis_index("x") + lax.axis_index("y")).astype(jnp.int32).reshape(1)
    tril = jnp.tril(jnp.ones((CHUNK, CHUNK), F32))
    shards = [weights[n].astype(BF16) for n in MATRIX_WEIGHTS]

    def layer_weights(i):
        got = dict(zip(MATRIX_WEIGHTS, _all_gather(shards, i, "weights_all_gather")))
        w_in_full = jnp.concatenate([got["w_in"][j] for j in range(N_DEV)], axis=-1)
        pad = jnp.zeros((d, D_IN_PAD - D_IN), BF16)
        wt = gm_w_s[i] * tril[None]
        lw = dict(
            w_in=jnp.concatenate([w_in_full[:, :3 * D_ATT], w_in_full[:, 3 * D_ATT + N_HEADS:],
                                  w_in_full[:, 3 * D_ATT:3 * D_ATT + N_HEADS], pad], axis=-1),
            w_out=got["w_out"].reshape(-1, d), w_ffn_in=got["w_ffn_in"],
            w_ffn_out=got["w_ffn_out"].reshape(N_DEV // 2, -1, d), w_ple=got["w_ple"],
            w_ple_gate=got["w_ple_gate"].reshape(-1, d),
            b_forget=b_forget[i][:, None], wt=wt.astype(BF16), wt_t=wt.transpose(0, 2, 1).astype(BF16),
            bs_t=gm_b_s[i].T)
        lw.update({n: weights[n][i][None] for n in ("mix_pre_norm", "mix_post_norm", "gm_v_norm", "mix_out_norm",
                                                    "ffn_pre_norm", "ffn_post_norm", "ple_norm")})
        return lw

    layer_w = [layer_weights(i) for i in range(depth)]

    h = x[0]
    saved = []
    for i in range(depth):
        h, sv = _layer_fwd(h, p[i, 0], layer_w[i], tq)
        saved.append(sv)

    def loss_head(y, t):
        err = y - t
        return err * (1.0 / d), _colsum(err * err)

    dh, sq = _rowwise(loss_head, [h, loss_target[0]], [], [(d, F32)], [d], "loss_head", 256)
    loss = lax.psum(0.5 * jnp.sum(sq) / d, AXES)

    layer_g = [None] * depth
    shard_out = {n: [None] * depth for n in MATRIX_WEIGHTS}
    for i in reversed(range(depth)):
        dh, layer_g[i] = _layer_bwd(dh, saved[i], layer_w[i], tq)
        full_g = [layer_g[i][n] for n in MATRIX_WEIGHTS]
        from_sibling = _sibling_exchange(full_g)
        parts = [_pair_sum(gf, gs, core, "rs_pair_sum_" + n) for gf, gs, n in zip(full_g, from_sibling, MATRIX_WEIGHTS)]
        from_chips = _chip_exchange(parts)
        for n, part, got in zip(MATRIX_WEIGHTS, parts, from_chips):
            shard_out[n][i] = _adamw_shard(weights[n], mom_m[n], mom_v[n], part, got, chip, i, "adamw_" + n)
    grad_x = dh[None]

    grads, deltas, new_m, new_v = {}, {}, {}, {}
    for n in MATRIX_WEIGHTS:
        grads[n], deltas[n], new_m[n], new_v[n] = (jnp.stack([shard_out[n][i][k] for i in range(depth)])
                                                   for k in range(4))

    small_g = _pack_small([jnp.stack([layer_g[i][n].reshape(-1) for i in range(depth)]) for n in SMALL_WEIGHTS])
    (gathered,) = _all_gather([small_g], None, "small_grads_all_gather")
    g_small = _sum_devices(gathered)
    pack = lambda t: _pack_small([t[n] for n in SMALL_WEIGHTS])
    dl, nm, nv = _rowwise(_adamw_math, [pack(weights), g_small, pack(mom_m), pack(mom_v)], [],
                          [(SMALL_COLS, F32)] * 3, [], "adamw_small", SMALL_ROWS)
    off = 0
    for n in SMALL_WEIGHTS:
        shp, size = weights[n].shape, weights[n].size
        grads[n], deltas[n], new_m[n], new_v[n] = (a.reshape(-1)[off:off + size].reshape(shp)
                                                   for a in (g_small, dl, nm, nv))
        off += size

    return (loss, grad_x, *[grads[n] for n in WEIGHT_ORDER], *[deltas[n] for n in WEIGHT_ORDER],
            *[new_m[n] for n in WEIGHT_ORDER], *[new_v[n] for n in WEIGHT_ORDER])
```

```python
import functools
import math

import jax
import jax.numpy as jnp
from jax import lax
from jax.experimental import pallas as pl
from jax.experimental.pallas import tpu as pltpu

F32 = jnp.float32
BF16 = jnp.bfloat16
MESH = pl.DeviceIdType.MESH
AXES = ("x", "y", "c")
N_DEV = 8

EPS = 1e-6
NEG_INF = -1e30
N_HEADS = 8
HEAD_DIM = 64
D_ATT = N_HEADS * HEAD_DIM
N_GROUPS = 8
GROUP_DIM = 64
D_GM = N_GROUPS * GROUP_DIM
CHUNK = 128
ATT_SCALE = HEAD_DIM ** -0.5
ATT_BLOCK = 512
D_IN = 3 * D_ATT + N_HEADS + 2 * D_GM
D_IN_PAD = 3 * D_ATT + 2 * D_GM + 128
F_OFF = 3 * D_ATT + 2 * D_GM

ADAM_LR = 0.001
ADAM_B1 = 0.9
ADAM_B2 = 0.999
ADAM_EPS = 1e-08
ADAM_WD = 0.01
ADAM_STEP = 10

LANE = 128
VMEM_LIMIT = 48 * 1024 * 1024
SMALL_COLS = 128
SMALL_ROWS = 512

MATRIX_WEIGHTS = ("w_in", "w_out", "w_ffn_in", "w_ffn_out", "w_ple", "w_ple_gate")
SMALL_WEIGHTS = ("mix_pre_norm", "mix_post_norm", "b_forget", "gm_v_norm", "gm_w_s", "gm_b_s",
                 "mix_out_norm", "ffn_pre_norm", "ffn_post_norm", "ple_norm")
WEIGHT_ORDER = ("mix_pre_norm", "mix_post_norm", "w_in", "b_forget", "gm_v_norm", "gm_w_s", "gm_b_s",
                "mix_out_norm", "w_out", "ffn_pre_norm", "ffn_post_norm", "w_ffn_in", "w_ffn_out",
                "w_ple", "ple_norm", "w_ple_gate")


def _tile(n, pref, unit=LANE):
    best = None
    t = unit
    while t <= min(n, pref):
        if n % t == 0:
            best = t
        t += unit
    return n if best is None else best


def _cparams(*semantics):
    return pltpu.CompilerParams(dimension_semantics=semantics or None, vmem_limit_bytes=VMEM_LIMIT)


NN = (((1,), (0,)), ((), ()))
NT = (((1,), (1,)), ((), ()))
TN = (((0,), (0,)), ((), ()))
_MM_AXES = {
    "nn": ("i", "k", "k", "j"), "nt": ("i", "k", "j", "k"), "tn": ("k", "i", "k", "j")}
_MM_DN = {"nn": NN, "nt": NT, "tn": TN}


def _mm(a, b, dims, out_dtype, name, a3=None, b3=None, o3=None, tm=1024, tn=1024, tk=1024):
    ar, ac, br, bc = _MM_AXES[dims]
    letter = {"i": "m", "j": "n", "k": "k"}
    size = {}

    def measure(x, rows, cols, stacked):
        shape = x.shape
        if stacked is None:
            size.setdefault(letter[rows], shape[0])
            size.setdefault(letter[cols], shape[1])
        else:
            for ax, n in ((rows, shape[1]), (cols, shape[2])):
                size.setdefault(letter[ax], n * shape[0] if letter[ax] == stacked else n)

    measure(a, ar, ac, a3)
    measure(b, br, bc, b3)
    m, n, k = size["m"], size["n"], size["k"]
    slab = {}
    for x, stacked, rows, cols in ((a, a3, ar, ac), (b, b3, br, bc)):
        if stacked is not None:
            slab[stacked] = x.shape[1] if letter[rows] == stacked else x.shape[2]
    if o3 is not None:
        slab.setdefault(o3, slab.get(o3, None) or {"m": m, "n": n}[o3] // N_DEV)
    tile = {"m": slab.get("m") or _tile(m, tm), "n": slab.get("n") or _tile(n, tn), "k": slab.get("k") or _tile(k, tk)}
    nk = k // tile["k"]

    def spec(rows, cols, stacked):
        tr, tc = tile[letter[rows]], tile[letter[cols]]
        if stacked is None:
            return pl.BlockSpec((tr, tc), lambda i, j, kk: ({"i": i, "j": j, "k": kk}[rows], {"i": i, "j": j, "k": kk}[cols]))

        def imap(i, j, kk):
            g = {"i": i, "j": j, "k": kk}
            return (g[{"m": "i", "n": "j", "k": "k"}[stacked]],
                    0 if letter[rows] == stacked else g[rows], 0 if letter[cols] == stacked else g[cols])

        return pl.BlockSpec((1, tr, tc), imap)

    dn = _MM_DN[dims]

    def body(a_ref, b_ref, o_ref, *acc):
        av = a_ref[...] if a3 is None else a_ref[0]
        bv = b_ref[...] if b3 is None else b_ref[0]
        prod = lax.dot_general(av.astype(BF16), bv.astype(BF16), dn, preferred_element_type=F32)

        def emit(val):
            if o3 is None:
                o_ref[...] = val.astype(out_dtype)
            else:
                o_ref[0] = val.astype(out_dtype)

        if nk == 1:
            emit(prod)
            return
        (acc_ref,) = acc
        kk = pl.program_id(2)

        @pl.when(kk == 0)
        def _():
            acc_ref[...] = prod

        @pl.when(kk > 0)
        def _():
            acc_ref[...] += prod

        @pl.when(kk == nk - 1)
        def _():
            emit(acc_ref[...])

    if o3 is None:
        out_shape = (m, n)
    elif o3 == "m":
        out_shape = (m // tile["m"], tile["m"], n)
    else:
        out_shape = (n // tile["n"], m, tile["n"])
    return pl.pallas_call(
        body, name=name, out_shape=jax.ShapeDtypeStruct(out_shape, out_dtype),
        grid=(m // tile["m"], n // tile["n"], nk),
        in_specs=[spec(ar, ac, a3), spec(br, bc, b3)], out_specs=spec("i", "j", o3),
        scratch_shapes=[] if nk == 1 else [pltpu.VMEM((tile["m"], tile["n"]), F32)],
        compiler_params=_cparams("parallel", "parallel", "arbitrary"),
    )(a, b)


def _rowwise(fn, rows, vecs, outs, reds, name, ts):
    rows = [r if isinstance(r, tuple) else (r, 0, r.shape[1]) for r in rows]
    s = rows[0][0].shape[0]
    ts = _tile(s, ts, 8)
    nr, nv, no = len(rows), len(vecs), len(outs)

    def body(*refs):
        vals = fn(*[r[...] for r in refs[:nr + nv]])
        vals = vals if isinstance(vals, tuple) else (vals,)
        o_refs = refs[nr + nv:nr + nv + no]
        r_refs = refs[nr + nv + no:]
        for o_ref, val in zip(o_refs, vals[:no]):
            o_ref[...] = val.astype(o_ref.dtype)
        if r_refs:
            @pl.when(pl.program_id(0) == 0)
            def _():
                for r_ref in r_refs:
                    r_ref[...] = jnp.zeros_like(r_ref)

            for r_ref, val in zip(r_refs, vals[no:]):
                r_ref[...] += val

    in_specs = [pl.BlockSpec((ts, w), functools.partial(lambda i, cb: (i, cb), cb=cb)) for _, cb, w in rows]
    in_specs += [pl.BlockSpec(v.shape, lambda i: (0, 0)) for v in vecs]
    out_specs = [pl.BlockSpec((ts, c), lambda i: (i, 0)) for c, _ in outs]
    out_specs += [pl.BlockSpec((1, c), lambda i: (0, 0)) for c in reds]
    out_shape = [jax.ShapeDtypeStruct((s, c), dt) for c, dt in outs]
    out_shape += [jax.ShapeDtypeStruct((1, c), F32) for c in reds]
    return pl.pallas_call(
        body, name=name, out_shape=out_shape, grid=(s // ts,), in_specs=in_specs, out_specs=out_specs,
        compiler_params=_cparams("arbitrary" if reds else "parallel"),
    )(*[r[0] for r in rows], *vecs)


def _rms(x):
    r = lax.rsqrt(jnp.mean(x * x, axis=-1, keepdims=True) + EPS)
    return x * r, r


def _rms_bwd(xhat, r, dyg):
    return r * (dyg - xhat * jnp.mean(dyg * xhat, axis=-1, keepdims=True))


def _colsum(x):
    return jnp.sum(x, axis=0, keepdims=True)


def _sigmoid(x):
    return 1.0 / (1.0 + jnp.exp(-x))


GELU_C = math.sqrt(2.0 / math.pi)
GELU_A = 0.044715


def _gelu(x):
    return 0.5 * x * (1.0 + jnp.tanh(GELU_C * (x + GELU_A * x * x * x)))


def _gelu_grad(x):
    t = jnp.tanh(GELU_C * (x + GELU_A * x * x * x))
    return 0.5 * (1.0 + t) + 0.5 * x * (1.0 - t * t) * GELU_C * (1.0 + 3.0 * GELU_A * x * x)


def _ffn_in_swiglu(hn, wg):
    s, d = hn.shape
    g2, _, n = wg.shape
    g = g2 // 2
    tm = _tile(s, 1024)

    def body(h_ref, wa_ref, wb_ref, ab_ref, t_ref):
        hv = h_ref[...]
        a = jnp.dot(hv, wa_ref[0], preferred_element_type=F32)
        b = jnp.dot(hv, wb_ref[0], preferred_element_type=F32)
        ab_ref[0, 0] = a.astype(BF16)
        ab_ref[1, 0] = b.astype(BF16)
        t_ref[0] = (a * _sigmoid(a) * b).astype(BF16)

    return pl.pallas_call(
        body, name="mm_ffn_in_swiglu",
        out_shape=(jax.ShapeDtypeStruct((2, g, s, n), BF16), jax.ShapeDtypeStruct((g, s, n), BF16)),
        grid=(s // tm, g),
        in_specs=[pl.BlockSpec((tm, d), lambda i, j: (i, 0)), pl.BlockSpec((1, d, n), lambda i, j: (j, 0, 0)),
                  pl.BlockSpec((1, d, n), lambda i, j: (j + g, 0, 0))],
        out_specs=(pl.BlockSpec((2, 1, tm, n), lambda i, j: (0, j, i, 0)),
                   pl.BlockSpec((1, tm, n), lambda i, j: (j, i, 0))),
        compiler_params=_cparams("parallel", "parallel"),
    )(hn, wg, wg)


def _swiglu_bwd(ab, dt, ts):
    _, g, s, n = ab.shape
    ts = _tile(s, ts, 8)

    def body(ab_ref, dt_ref, t_ref, dab_ref):
        a = ab_ref[0, 0].astype(F32)
        b = ab_ref[1, 0].astype(F32)
        dt = dt_ref[0]
        sig = _sigmoid(a)
        silu = a * sig
        t_ref[0] = (silu * b).astype(BF16)
        dab_ref[0, 0] = (dt * b * (sig * (1.0 + a * (1.0 - sig)))).astype(BF16)
        dab_ref[1, 0] = (dt * silu).astype(BF16)

    both = pl.BlockSpec((2, 1, ts, n), lambda j, i: (0, j, i, 0))
    one = pl.BlockSpec((1, ts, n), lambda j, i: (j, i, 0))
    return pl.pallas_call(
        body, name="swiglu_bwd",
        out_shape=(jax.ShapeDtypeStruct((g, s, n), BF16), jax.ShapeDtypeStruct((2, g, s, n), BF16)),
        grid=(g, s // ts), in_specs=[both, one], out_specs=(one, both),
        compiler_params=_cparams("parallel", "parallel"),
    )(ab, dt)


def _forget_fwd(fl_t, b_col):
    h, s = fl_t.shape
    nb = s // LANE

    def body(fl_ref, b_ref, c_ref):
        upper = (lax.broadcasted_iota(jnp.int32, (LANE, LANE), 0)
                 <= lax.broadcasted_iota(jnp.int32, (LANE, LANE), 1)).astype(F32)

        def step(i, carry):
            x = fl_ref[i] + b_ref[...]
            lf = jnp.minimum(x, 0.0) - jnp.log(1.0 + jnp.exp(-jnp.abs(x)))
            cs = jnp.dot(lf, upper, precision=lax.Precision.HIGHEST, preferred_element_type=F32) + carry
            c_ref[i] = cs
            return cs[:, LANE - 1:LANE]

        lax.fori_loop(0, nb, step, jnp.zeros((h, 1), F32))

    out = pl.pallas_call(
        body, name="forget_fwd", out_shape=jax.ShapeDtypeStruct((nb, h, LANE), F32),
        compiler_params=_cparams(),
    )(fl_t.reshape(h, nb, LANE).transpose(1, 0, 2), b_col)
    return out.transpose(1, 0, 2).reshape(h, s)


def _forget_bwd(dc_t, fl_t, b_col):
    h, s = fl_t.shape
    nb = s // LANE

    def body(dc_ref, fl_ref, b_ref, dfl_ref, db_ref):
        lower = (lax.broadcasted_iota(jnp.int32, (LANE, LANE), 0)
                 >= lax.broadcasted_iota(jnp.int32, (LANE, LANE), 1)).astype(F32)

        def step(t, carry):
            tail, db = carry
            i = nb - 1 - t
            rc = jnp.dot(dc_ref[i], lower, precision=lax.Precision.HIGHEST, preferred_element_type=F32) + tail
            dfl = rc * (1.0 - _sigmoid(fl_ref[i] + b_ref[...]))
            dfl_ref[i] = dfl
            return rc[:, 0:1], db + jnp.sum(dfl, axis=1, keepdims=True)

        _, db = lax.fori_loop(0, nb, step, (jnp.zeros((h, 1), F32), jnp.zeros((h, 1), F32)))
        db_ref[...] = db

    blocked = lambda a: a.reshape(h, nb, LANE).transpose(1, 0, 2)
    dfl, db = pl.pallas_call(
        body, name="forget_bwd",
        out_shape=(jax.ShapeDtypeStruct((nb, h, LANE), F32), jax.ShapeDtypeStruct((h, 1), F32)),
        compiler_params=_cparams(),
    )(blocked(dc_t), blocked(fl_t), b_col)
    return dfl.transpose(1, 0, 2).reshape(h, s), db


N_PAIRS = N_HEADS // 2


def _causal_mask(t):
    return lax.broadcasted_iota(jnp.int32, (t, t), 0) >= lax.broadcasted_iota(jnp.int32, (t, t), 1)


def _head_lanes():
    return lax.broadcasted_iota(jnp.int32, (1, 2 * HEAD_DIM), 1) < HEAD_DIM


def _pick(x2, first, hh):
    zero = jnp.zeros_like(x2)
    return jnp.where(first, x2, zero) if hh == 0 else jnp.where(first, zero, x2)


def _attn_fwd(qkv, c_col, c_row, tq):
    s = qkv.shape[0]
    nq = s // tq
    w = 2 * HEAD_DIM

    def body(q_ref, k_ref, v_ref, cc_ref, cr_ref, o_ref, lse_ref):
        i = pl.program_id(1)
        first = _head_lanes()
        q2 = q_ref[...] * ATT_SCALE
        qs = [_pick(q2, first, hh) for hh in range(2)]

        def block(j, carry, masked):
            off = pl.multiple_of(j * tq, tq)
            k2 = k_ref[pl.ds(off, tq), :]
            v2 = v_ref[pl.ds(off, tq), :]
            new = []
            for hh in range(2):
                m, l, acc = carry[hh]
                sc = lax.dot_general(qs[hh], k2, NT, preferred_element_type=F32) + cc_ref[hh] - cr_ref[hh, j]
                if masked:
                    sc = jnp.where(_causal_mask(tq), sc, NEG_INF)
                m_new = jnp.maximum(m, jnp.max(sc, axis=-1, keepdims=True))
                alpha = jnp.exp(m - m_new)
                p = jnp.exp(sc - m_new)
                l = alpha * l + jnp.sum(p, axis=-1, keepdims=True)
                p_hi = p.astype(BF16)
                p_lo = (p - p_hi.astype(F32)).astype(BF16)
                acc = (alpha * acc + jnp.dot(p_hi, v2, preferred_element_type=F32)
                       + jnp.dot(p_lo, v2, preferred_element_type=F32))
                new.append((m_new, l, acc))
            return tuple(new)

        one = (jnp.full((tq, 1), NEG_INF, F32), jnp.zeros((tq, 1), F32), jnp.zeros((tq, w), F32))
        carry = lax.fori_loop(0, i, lambda j, c: block(j, c, False), (one, one))
        (m0, l0, a0), (m1, l1, a1) = block(i, carry, True)
        o_ref[...] = jnp.where(first, a0 / l0, a1 / l1)
        lse_ref[0] = m0 + jnp.log(l0)
        lse_ref[1] = m1 + jnp.log(l1)

    return pl.pallas_call(
        body, name="attn_fwd",
        out_shape=(jax.ShapeDtypeStruct((s, D_ATT), F32), jax.ShapeDtypeStruct((N_HEADS, s, 1), F32)),
        grid=(N_PAIRS, nq),
        in_specs=[pl.BlockSpec((tq, w), lambda hp, i: (i, hp)),
                  pl.BlockSpec((s, w), lambda hp, i: (0, N_PAIRS + hp)),
                  pl.BlockSpec((s, w), lambda hp, i: (0, 2 * N_PAIRS + hp)),
                  pl.BlockSpec((2, tq, 1), lambda hp, i: (hp, i, 0)),
                  pl.BlockSpec((2, nq, 1, tq), lambda hp, i: (hp, 0, 0, 0))],
        out_specs=(pl.BlockSpec((tq, w), lambda hp, i: (i, hp)),
                   pl.BlockSpec((2, tq, 1), lambda hp, i: (hp, i, 0))),
        compiler_params=_cparams("parallel", "parallel"),
    )(qkv, qkv, qkv, c_col, c_row)


def _attn_delta(o, do, tq):
    s = o.shape[0]
    w = 2 * HEAD_DIM

    def body(o_ref, do_ref, d_ref):
        first = _head_lanes()
        prod = o_ref[...] * do_ref[...].astype(F32)
        d_ref[0] = jnp.sum(_pick(prod, first, 0), axis=-1, keepdims=True)
        d_ref[1] = jnp.sum(_pick(prod, first, 1), axis=-1, keepdims=True)

    blk = pl.BlockSpec((tq, w), lambda hp, i: (i, hp))
    return pl.pallas_call(
        body, name="attn_delta", out_shape=jax.ShapeDtypeStruct((N_HEADS, s, 1), F32), grid=(N_PAIRS, s // tq),
        in_specs=[blk, blk], out_specs=pl.BlockSpec((2, tq, 1), lambda hp, i: (hp, i, 0)),
        compiler_params=_cparams("parallel", "parallel"),
    )(o, do)


def _attn_bwd(qkv, do, lse, delta, c_col, c_row, tq):
    s = qkv.shape[0]
    nq = s // tq
    w = 2 * HEAD_DIM

    def body(q_ref, do_ref, lse_ref, dl_ref, cc_ref, k_ref, v_ref, cr_ref, dq_ref, dk_ref, dv_ref, dc_ref):
        j = pl.program_id(1)
        first = _head_lanes()

        @pl.when(j == 0)
        def _():
            dq_ref[...] = jnp.zeros_like(dq_ref)

        k2 = k_ref[...]
        v2 = v_ref[...]

        def step(i, carry, masked):
            off = pl.multiple_of(i * tq, tq)
            rows = pl.ds(off, tq)
            q2 = q_ref[rows, :] * ATT_SCALE
            do2 = do_ref[rows, :]
            new, dqs = [], []
            for hh in range(2):
                dk, dv, dcs = carry[hh]
                sc = (lax.dot_general(_pick(q2, first, hh), k2, NT, preferred_element_type=F32)
                      + cc_ref[hh, rows, :] - cr_ref[hh, 0])
                if masked:
                    sc = jnp.where(_causal_mask(tq), sc, NEG_INF)
                p = jnp.exp(sc - lse_ref[hh, rows, :])
                dv = dv + lax.dot_general(p.astype(BF16), do2, TN, preferred_element_type=F32)
                dp = lax.dot_general(_pick(do2, first, hh), v2, NT, preferred_element_type=F32)
                ds = p * (dp - dl_ref[hh, rows, :])
                dsb = ds.astype(BF16)
                dk = dk + lax.dot_general(dsb, q2, TN, preferred_element_type=F32)
                dqs.append(jnp.dot(dsb, k2, preferred_element_type=F32))
                new.append((dk, dv, dcs + jnp.sum(ds, axis=0, keepdims=True)))
            dq_ref[rows, :] += jnp.where(first, dqs[0], dqs[1]) * ATT_SCALE
            return tuple(new)

        one = (jnp.zeros((tq, w), F32), jnp.zeros((tq, w), F32), jnp.zeros((1, tq), F32))
        carry = step(j, (one, one), True)
        (dk0, dv0, dc0), (dk1, dv1, dc1) = lax.fori_loop(j + 1, nq, lambda i, c: step(i, c, False), carry)
        dk_ref[...] = jnp.where(first, dk0, dk1)
        dv_ref[...] = jnp.where(first, dv0, dv1)
        dc_ref[0, 0] = -dc0
        dc_ref[1, 0] = -dc1

    whole_cols = lambda off: pl.BlockSpec((s, w), lambda hp, j: (0, off + hp))
    whole_heads = pl.BlockSpec((2, s, 1), lambda hp, j: (hp, 0, 0))
    blk = lambda off: pl.BlockSpec((tq, w), lambda hp, j: (j, off + hp))
    crow = pl.BlockSpec((2, 1, 1, tq), lambda hp, j: (hp, j, 0, 0))
    return pl.pallas_call(
        body, name="attn_bwd",
        out_shape=(jax.ShapeDtypeStruct((s, D_ATT), F32), jax.ShapeDtypeStruct((s, D_ATT), F32),
                   jax.ShapeDtypeStruct((s, D_ATT), F32), jax.ShapeDtypeStruct((N_HEADS, nq, 1, tq), F32)),
        grid=(N_PAIRS, nq),
        in_specs=[whole_cols(0), whole_cols(0), whole_heads, whole_heads, whole_heads,
                  blk(N_PAIRS), blk(2 * N_PAIRS), crow],
        out_specs=(whole_cols(0), blk(0), blk(0), crow),
        compiler_params=_cparams("parallel", "arbitrary"),
    )(qkv, do, lse, delta, c_col, qkv, qkv, c_row)


def _gm_group_norm(vg):
    mu = jnp.mean(vg, axis=-1, keepdims=True)
    d = vg - mu
    rstd = lax.rsqrt(jnp.mean(d * d, axis=-1, keepdims=True) + EPS)
    return d * rstd, rstd


def _gmlp_fwd(z, wt, bs_t, vgain):
    s = z.shape[0]

    def body(gu_ref, gv_ref, wt_ref, bs_ref, vg_ref, o_ref):
        for g in range(N_GROUPS):
            sl = slice(g * GROUP_DIM, (g + 1) * GROUP_DIM)
            vhat, _ = _gm_group_norm(_gelu(gv_ref[:, sl]))
            vn = vhat * vg_ref[:, sl]
            mixed = jnp.dot(wt_ref[g], vn.astype(BF16), preferred_element_type=F32) + bs_ref[:, g:g + 1]
            o_ref[:, sl] = _gelu(gu_ref[:, sl]) * mixed

    full = lambda a: pl.BlockSpec(a.shape, lambda n: (0,) * a.ndim)
    return pl.pallas_call(
        body, name="gmlp_fwd", out_shape=jax.ShapeDtypeStruct((s, D_GM), F32), grid=(s // CHUNK,),
        in_specs=[pl.BlockSpec((CHUNK, D_GM), lambda n: (n, 3)), pl.BlockSpec((CHUNK, D_GM), lambda n: (n, 4)),
                  full(wt), full(bs_t), full(vgain)],
        out_specs=pl.BlockSpec((CHUNK, D_GM), lambda n: (n, 0)),
        compiler_params=_cparams("parallel"),
    )(z, z, wt, bs_t, vgain)


def _gmlp_bwd(z, dgm, wt, wt_t, bs_t, vgain):
    s = z.shape[0]

    def body(gu_ref, gv_ref, dgm_ref, wt_ref, wtt_ref, bs_ref, vg_ref, dgu_ref, dgv_ref, dwt_ref, dbs_ref, dvg_ref):
        @pl.when(pl.program_id(0) == 0)
        def _():
            dwt_ref[...] = jnp.zeros_like(dwt_ref)
            dbs_ref[...] = jnp.zeros_like(dbs_ref)
            dvg_ref[...] = jnp.zeros_like(dvg_ref)

        for g in range(N_GROUPS):
            sl = slice(g * GROUP_DIM, (g + 1) * GROUP_DIM)
            gu = gu_ref[:, sl]
            gv = gv_ref[:, sl]
            dgm = dgm_ref[:, sl]
            vhat, rstd = _gm_group_norm(_gelu(gv))
            gain = vg_ref[:, sl]
            vn = (vhat * gain).astype(BF16)
            mixed = jnp.dot(wt_ref[g], vn, preferred_element_type=F32) + bs_ref[:, g:g + 1]
            dgu_ref[:, sl] = (dgm * mixed * _gelu_grad(gu)).astype(BF16)
            dmixed = dgm * _gelu(gu)
            dmb = dmixed.astype(BF16)
            dbs_ref[:, g:g + 1] += jnp.sum(dmixed, axis=-1, keepdims=True)
            dwt_ref[g] += lax.dot_general(dmb, vn, NT, preferred_element_type=F32)
            dvn = jnp.dot(wtt_ref[g], dmb, preferred_element_type=F32)
            dvg_ref[:, sl] += _colsum(dvn * vhat)
            dvhat = dvn * gain
            dvf = rstd * (dvhat - jnp.mean(dvhat, axis=-1, keepdims=True)
                          - vhat * jnp.mean(dvhat * vhat, axis=-1, keepdims=True))
            dgv_ref[:, sl] = (dvf * _gelu_grad(gv)).astype(BF16)

    full = lambda a: pl.BlockSpec(a.shape, lambda n: (0,) * a.ndim)
    chunk = pl.BlockSpec((CHUNK, D_GM), lambda n: (n, 0))
    return pl.pallas_call(
        body, name="gmlp_bwd",
        out_shape=(jax.ShapeDtypeStruct((s, D_GM), BF16), jax.ShapeDtypeStruct((s, D_GM), BF16),
                   jax.ShapeDtypeStruct(wt.shape, F32), jax.ShapeDtypeStruct(bs_t.shape, F32),
                   jax.ShapeDtypeStruct(vgain.shape, F32)),
        grid=(s // CHUNK,),
        in_specs=[pl.BlockSpec((CHUNK, D_GM), lambda n: (n, 3)), pl.BlockSpec((CHUNK, D_GM), lambda n: (n, 4)),
                  chunk, full(wt), full(wt_t), full(bs_t), full(vgain)],
        out_specs=(chunk, chunk, full(wt), full(bs_t), full(vgain)),
        compiler_params=_cparams("arbitrary"),
    )(z, z, dgm, wt, wt_t, bs_t, vgain)


def _layer_fwd(h0, p_i, w, tq):
    s, d = h0.shape
    nq = s // tq
    sv = {"h0": h0}

    (hn1,) = _rowwise(lambda h, g: _rms(h)[0] * g, [h0], [w["mix_pre_norm"]], [(d, BF16)], [], "pre_mix", 256)
    z = _mm(hn1, w["w_in"], "nn", F32, "mm_in")
    fl_t = z[:, F_OFF:F_OFF + N_HEADS].T
    c_t = _forget_fwd(fl_t, w["b_forget"])
    c_col = c_t[:, :, None]
    c_row = c_t.reshape(N_HEADS, nq, 1, tq)
    qkv = z[:, :3 * D_ATT].astype(BF16)
    att, lse = _attn_fwd(qkv, c_col, c_row, tq)
    gm = _gmlp_fwd(z, w["wt"], w["bs_t"], w["gm_v_norm"])

    def mix_out(att, gm, g):
        return jnp.concatenate([_rms(att)[0] * g[:, :D_ATT], _rms(gm)[0] * g[:, D_ATT:]], axis=-1)

    (mc,) = _rowwise(mix_out, [att, gm], [w["mix_out_norm"]], [(D_ATT + D_GM, BF16)], [], "mix_out", 256)
    y1 = _mm(mc, w["w_out"], "nn", F32, "mm_out")

    def post_mix(h0, y1, gpost, gpre):
        h1 = h0 + _rms(y1)[0] * gpost
        return h1, _rms(h1)[0] * gpre

    h1, hn2 = _rowwise(post_mix, [h0, y1], [w["mix_post_norm"], w["ffn_pre_norm"]],
                       [(d, F32), (d, BF16)], [], "post_mix", 256)
    ab, t = _ffn_in_swiglu(hn2, w["w_ffn_in"])
    y2 = _mm(t, w["w_ffn_out"], "nn", F32, "mm_ffn_out", a3="k", b3="k")

    def post_ffn(h1, y2, g):
        h2 = h1 + _rms(y2)[0] * g
        return h2, _rms(h2)[0]

    h2, hr = _rowwise(post_ffn, [h1, y2], [w["ffn_post_norm"]], [(d, F32), (d, BF16)], [], "post_ffn", 256)
    gl = _mm(hr, w["w_ple_gate"], "nn", F32, "mm_gate")
    pe = _mm(p_i, w["w_ple"], "nn", F32, "mm_ple")
    (h3,) = _rowwise(lambda h2, gl, pe, g: h2 + _sigmoid(gl) * (_rms(pe)[0] * g), [h2, gl, pe], [w["ple_norm"]],
                     [(d, F32)], [], "ple_out", 256)
    sv.update(hn1=hn1, z=z, fl_t=fl_t, c_col=c_col, c_row=c_row, qkv=qkv, lse=lse, att=att, gm=gm,
              mc=mc, y1=y1, h1=h1, hn2=hn2, ab=ab, y2=y2, h2=h2, hr=hr, gl=gl, pe=pe, p_i=p_i)
    return h3, sv


def _layer_bwd(dh3, sv, w, tq):
    s, d = dh3.shape
    g = {}
    by_rows = lambda a: a.reshape(N_DEV, -1, a.shape[-1])
    by_cols = lambda a: jnp.stack(jnp.split(a, N_DEV, axis=-1))

    def ple_bwd(dh3, gl, pe, gple):
        gate = _sigmoid(gl)
        pehat, rpe = _rms(pe)
        dgl = dh3 * (pehat * gple) * gate * (1.0 - gate)
        de = dh3 * gate
        return dgl, _rms_bwd(pehat, rpe, de * gple), _colsum(de * pehat)

    dgl, dpe, g["ple_norm"] = _rowwise(ple_bwd, [dh3, sv["gl"], sv["pe"]], [w["ple_norm"]],
                                       [(d, BF16), (d, BF16)], [d], "ple_bwd", 256)
    g["w_ple_gate"] = by_rows(_mm(sv["hr"], dgl, "tn", F32, "mm_dgate"))
    dhr = _mm(dgl, w["w_ple_gate"], "nt", F32, "mm_dhr")
    g["w_ple"] = by_cols(_mm(sv["p_i"], dpe, "tn", F32, "mm_dple"))

    def ffn_post_bwd(dh3, dhr, h2, y2, gpost):
        h2hat, r2 = _rms(h2)
        dh2 = dh3 + _rms_bwd(h2hat, r2, dhr)
        y2hat, ry = _rms(y2)
        return dh2, _rms_bwd(y2hat, ry, dh2 * gpost), _colsum(dh2 * y2hat)

    dh2, dy2, g["ffn_post_norm"] = _rowwise(ffn_post_bwd, [dh3, dhr, sv["h2"], sv["y2"]], [w["ffn_post_norm"]],
                                            [(d, F32), (d, BF16)], [d], "ffn_post_bwd", 256)
    dt = _mm(dy2, w["w_ffn_out"], "nt", F32, "mm_dt", b3="n", o3="n")
    t, dab = _swiglu_bwd(sv["ab"], dt, 256)
    dab = dab.reshape((N_DEV,) + dab.shape[2:])
    g["w_ffn_out"] = by_rows(_mm(t, dy2, "tn", F32, "mm_dffn_out", a3="m", o3="m"))
    dhn2 = _mm(dab, w["w_ffn_in"], "nt", F32, "mm_dhn2", a3="k", b3="k")
    g["w_ffn_in"] = _mm(sv["hn2"], dab, "tn", F32, "mm_dffn_in", b3="n", o3="n")

    def mix_post_bwd(dh2, dhn2, h1, y1, gpre, gpost):
        h1hat, r1 = _rms(h1)
        dh1 = dh2 + _rms_bwd(h1hat, r1, dhn2 * gpre)
        y1hat, ry = _rms(y1)
        return dh1, _rms_bwd(y1hat, ry, dh1 * gpost), _colsum(dhn2 * h1hat), _colsum(dh1 * y1hat)

    dh1, dy1, g["ffn_pre_norm"], g["mix_post_norm"] = _rowwise(
        mix_post_bwd, [dh2, dhn2, sv["h1"], sv["y1"]], [w["ffn_pre_norm"], w["mix_post_norm"]],
        [(d, F32), (d, BF16)], [d, d], "mix_post_bwd", 256)
    dmc = _mm(dy1, w["w_out"], "nt", F32, "mm_dmc")
    g["w_out"] = by_rows(_mm(sv["mc"], dy1, "tn", F32, "mm_dout"))

    def mix_out_bwd(da, dg, att, gm, gain):
        atthat, ra = _rms(att)
        gmhat, rg = _rms(gm)
        dgain = jnp.concatenate([_colsum(da * atthat), _colsum(dg * gmhat)], axis=-1)
        return _rms_bwd(atthat, ra, da * gain[:, :D_ATT]), _rms_bwd(gmhat, rg, dg * gain[:, D_ATT:]), dgain

    datt, dgm, g["mix_out_norm"] = _rowwise(
        mix_out_bwd, [(dmc, 0, D_ATT), (dmc, 1, D_GM), sv["att"], sv["gm"]], [w["mix_out_norm"]],
        [(D_ATT, BF16), (D_GM, F32)], [D_ATT + D_GM], "mix_out_bwd", 256)

    dgu, dgv, dwt, dbs_t, g["gm_v_norm"] = _gmlp_bwd(sv["z"], dgm, w["wt"], w["wt_t"], w["bs_t"], w["gm_v_norm"])
    g["gm_w_s"] = dwt * jnp.tril(jnp.ones((CHUNK, CHUNK), F32))[None]
    g["gm_b_s"] = dbs_t.T

    delta = _attn_delta(sv["att"], datt, tq)
    dq, dk, dv, dc_row = _attn_bwd(sv["qkv"], datt, sv["lse"], delta, sv["c_col"], sv["c_row"], tq)
    dfl_t, db = _forget_bwd(dc_row.reshape(N_HEADS, s), sv["fl_t"], w["b_forget"])
    g["b_forget"] = db.reshape(1, N_HEADS)
    dz = jnp.concatenate([dq.astype(BF16), dk.astype(BF16), dv.astype(BF16), dgu, dgv, dfl_t.T.astype(BF16),
                          jnp.zeros((s, D_IN_PAD - F_OFF - N_HEADS), BF16)], axis=-1)
    dhn1 = _mm(dz, w["w_in"], "nt", F32, "mm_dhn1")
    din = _mm(sv["hn1"], dz, "tn", F32, "mm_din")
    din = jnp.concatenate([din[:, :3 * D_ATT], din[:, F_OFF:F_OFF + N_HEADS], din[:, 3 * D_ATT:F_OFF]], axis=-1)
    g["w_in"] = by_cols(din)

    def mix_pre_bwd(dh1, dhn1, h0, gpre):
        h0hat, r0 = _rms(h0)
        return dh1 + _rms_bwd(h0hat, r0, dhn1 * gpre), _colsum(dhn1 * h0hat)

    dh0, g["mix_pre_norm"] = _rowwise(mix_pre_bwd, [dh1, dhn1, sv["h0"]], [w["mix_pre_norm"]],
                                      [(d, F32)], [d], "mix_pre_bwd", 256)
    return dh0, g


ANY = pl.BlockSpec(memory_space=pl.ANY)


def _all_gather(xs, layer, name):
    n = len(xs)

    def body(*refs):
        x_refs, out_refs = refs[:n], refs[n:2 * n]
        send_sems, recv_sems, local_sems = refs[2 * n:]
        x, y, c = lax.axis_index("x"), lax.axis_index("y"), lax.axis_index("c")
        me, sibling = (x, y, c), (x, y, 1 - c)
        chips = [(1 - x, y), (x, 1 - y), (1 - x, 1 - y)]

        def shard(a):
            return x_refs[a] if layer is None else x_refs[a].at[layer]

        def rows(a, px, py, pc):
            return out_refs[a].at[4 * px + 2 * py + pc]

        def copy(a, kk, block, to, from_shard=False):
            return pltpu.make_async_remote_copy(
                src_ref=shard(a) if from_shard else rows(a, *block), dst_ref=rows(a, *block),
                send_sem=send_sems.at[7 * a + kk], recv_sem=recv_sems.at[7 * a + kk],
                device_id=to, device_id_type=MESH)

        mine = [pltpu.make_async_copy(shard(a), rows(a, *me), local_sems.at[a]) for a in range(n)]
        for cp in mine:
            cp.start()
        first = []
        for a in range(n):
            first.append(copy(a, 0, me, sibling, from_shard=True))
            first += [copy(a, 1 + j, me, (*chip, c), from_shard=True) for j, chip in enumerate(chips)]
        for cp in first:
            cp.start()
        passed = []
        for j, chip in enumerate(chips):
            for a in range(n):
                copy(a, 1 + j, (*chip, c), me).wait_recv()
                passed.append(copy(a, 4 + j, (*chip, c), sibling))
                passed[-1].start()
        for a in range(n):
            copy(a, 0, sibling, me).wait_recv()
        for j, chip in enumerate(chips):
            for a in range(n):
                copy(a, 4 + j, (*chip, 1 - c), me).wait_recv()
        for cp in first + passed:
            cp.wait_send()
        for cp in mine:
            cp.wait()

    shapes = [x.shape if layer is None else x.shape[1:] for x in xs]
    return pl.pallas_call(
        body, name=name, out_shape=[jax.ShapeDtypeStruct((N_DEV,) + sh, x.dtype) for sh, x in zip(shapes, xs)],
        in_specs=[ANY] * n, out_specs=[ANY] * n,
        scratch_shapes=[pltpu.SemaphoreType.DMA((7 * n,)), pltpu.SemaphoreType.DMA((7 * n,)),
                        pltpu.SemaphoreType.DMA((n,))],
    )(*xs)


def _sibling_exchange(grads):
    n = len(grads)

    def body(*refs):
        g_refs, got_refs = refs[:n], refs[n:2 * n]
        send_sems, recv_sems = refs[2 * n:]
        x, y, c = lax.axis_index("x"), lax.axis_index("y"), lax.axis_index("c")
        copies = [pltpu.make_async_remote_copy(
            src_ref=g_refs[a].at[2 * kk + (1 - c)], dst_ref=got_refs[a].at[kk], send_sem=send_sems.at[4 * a + kk],
            recv_sem=recv_sems.at[4 * a + kk], device_id=(x, y, 1 - c), device_id_type=MESH)
            for a in range(n) for kk in range(4)]
        for cp in copies:
            cp.start()
        for cp in copies:
            cp.wait_recv()
        for cp in copies:
            cp.wait_send()

    return pl.pallas_call(
        body, name="rs_sibling_exchange",
        out_shape=[jax.ShapeDtypeStruct((4,) + g.shape[1:], g.dtype) for g in grads],
        in_specs=[ANY] * n, out_specs=[ANY] * n,
        scratch_shapes=[pltpu.SemaphoreType.DMA((4 * n,)), pltpu.SemaphoreType.DMA((4 * n,))],
    )(*grads)


def _chip_exchange(parts):
    n = len(parts)

    def body(*refs):
        p_refs, got_refs = refs[:n], refs[n:2 * n]
        send_sems, recv_sems = refs[2 * n:]
        x, y, c = lax.axis_index("x"), lax.axis_index("y"), lax.axis_index("c")
        chips = [(1 - x, y), (x, 1 - y), (1 - x, 1 - y)]
        copies = [pltpu.make_async_remote_copy(
            src_ref=p_refs[a].at[2 * cx + cy], dst_ref=got_refs[a].at[j], send_sem=send_sems.at[3 * a + j],
            recv_sem=recv_sems.at[3 * a + j], device_id=(cx, cy, c), device_id_type=MESH)
            for a in range(n) for j, (cx, cy) in enumerate(chips)]
        for cp in copies:
            cp.start()
        for cp in copies:
            cp.wait_recv()
        for cp in copies:
            cp.wait_send()

    return pl.pallas_call(
        body, name="rs_chip_exchange",
        out_shape=[jax.ShapeDtypeStruct((3,) + p.shape[1:], p.dtype) for p in parts],
        in_specs=[ANY] * n, out_specs=[ANY] * n,
        scratch_shapes=[pltpu.SemaphoreType.DMA((3 * n,)), pltpu.SemaphoreType.DMA((3 * n,))],
    )(*parts)


def _pair_sum(grads, got, core, name):
    _, a, b = grads.shape
    ta = _tile(a, 512, 16)

    def body(core_ref, g_ref, got_ref, o_ref):
        o_ref[...] = (g_ref[...] + got_ref[...]).astype(BF16)

    blk = lambda fn: pl.BlockSpec((1, ta, b), fn)
    return pl.pallas_call(
        body, name=name, out_shape=jax.ShapeDtypeStruct((4, a, b), BF16),
        grid_spec=pltpu.PrefetchScalarGridSpec(
            num_scalar_prefetch=1, grid=(4, a // ta),
            in_specs=[blk(lambda kk, i, core: (2 * kk + core[0], i, 0)), blk(lambda kk, i, core: (kk, i, 0))],
            out_specs=blk(lambda kk, i, core: (kk, i, 0))),
        compiler_params=_cparams("parallel", "parallel"),
    )(core, grads, got)


def _sum_devices(parts):
    _, r, c = parts.shape

    def body(p_ref, o_ref):
        acc = p_ref[0]
        for j in range(1, N_DEV):
            acc = acc + p_ref[j]
        o_ref[...] = acc

    return pl.pallas_call(
        body, name="small_sum", out_shape=jax.ShapeDtypeStruct((r, c), F32), grid=(r // SMALL_ROWS,),
        in_specs=[pl.BlockSpec((N_DEV, SMALL_ROWS, c), lambda i: (0, i, 0))],
        out_specs=pl.BlockSpec((SMALL_ROWS, c), lambda i: (i, 0)),
        compiler_params=_cparams("parallel"),
    )(parts)


def _adamw_math(w, g, m, v):
    m = ADAM_B1 * m + (1.0 - ADAM_B1) * g
    v = ADAM_B2 * v + (1.0 - ADAM_B2) * (g * g)
    m_hat = m / (1.0 - ADAM_B1 ** ADAM_STEP)
    v_hat = v / (1.0 - ADAM_B2 ** ADAM_STEP)
    return -ADAM_LR * (m_hat / (jnp.sqrt(v_hat) + ADAM_EPS) + ADAM_WD * w), m, v


def _adamw_shard(w, m, v, part, got, chip, layer, name):
    _, a, b = w.shape
    ta = _tile(a, 256, 16)

    def body(chip_ref, w_ref, m_ref, v_ref, p_ref, got_ref, g_ref, d_ref, nm_ref, nv_ref):
        g = p_ref[0].astype(F32)
        for j in range(3):
            g = g + got_ref[j].astype(F32)
        g_ref[...] = g
        d_ref[...], nm_ref[...], nv_ref[...] = _adamw_math(w_ref[0], g, m_ref[0], v_ref[0])

    mine = pl.BlockSpec((1, ta, b), lambda i, chip: (layer, i, 0))
    out = pl.BlockSpec((ta, b), lambda i, chip: (i, 0))
    return pl.pallas_call(
        body, name=name, out_shape=[jax.ShapeDtypeStruct((a, b), F32)] * 4,
        grid_spec=pltpu.PrefetchScalarGridSpec(
            num_scalar_prefetch=1, grid=(a // ta,),
            in_specs=[mine, mine, mine, pl.BlockSpec((1, ta, b), lambda i, chip: (chip[0], i, 0)),
                      pl.BlockSpec((3, ta, b), lambda i, chip: (0, i, 0))],
            out_specs=[out] * 4),
        compiler_params=_cparams("parallel"),
    )(chip, w, m, v, part, got)


def _pack_small(pieces):
    flat = jnp.concatenate([p.reshape(-1) for p in pieces])
    total = -(-flat.shape[0] // (SMALL_COLS * SMALL_ROWS)) * SMALL_COLS * SMALL_ROWS
    return jnp.pad(flat, (0, total - flat.shape[0])).reshape(-1, SMALL_COLS)


def kernel(x, p, mix_pre_norm, mix_post_norm, w_in, b_forget, gm_v_norm, gm_w_s, gm_b_s, mix_out_norm, w_out, ffn_pre_norm, ffn_post_norm, w_ffn_in, w_ffn_out, w_ple, ple_norm, w_ple_gate, loss_target, m_mix_pre_norm, m_mix_post_norm, m_w_in, m_b_forget, m_gm_v_norm, m_gm_w_s, m_gm_b_s, m_mix_out_norm, m_w_out, m_ffn_pre_norm, m_ffn_post_norm, m_w_ffn_in, m_w_ffn_out, m_w_ple, m_ple_norm, m_w_ple_gate, v_mix_pre_norm, v_mix_post_norm, v_w_in, v_b_forget, v_gm_v_norm, v_gm_w_s, v_gm_b_s, v_mix_out_norm, v_w_out, v_ffn_pre_norm, v_ffn_post_norm, v_w_ffn_in, v_w_ffn_out, v_w_ple, v_ple_norm, v_w_ple_gate):
    given = dict(locals())
    weights = {n: given[n] for n in WEIGHT_ORDER}
    mom_m = {n: given["m_" + n] for n in WEIGHT_ORDER}
    mom_v = {n: given["v_" + n] for n in WEIGHT_ORDER}
    depth = w_in.shape[0]
    s, d = x.shape[1], x.shape[2]
    tq = _tile(s, ATT_BLOCK)
    core = lax.axis_index("c").astype(jnp.int32).reshape(1)
    chip = (2 * lax.axis_index("x") + lax.axis_index("y")).astype(jnp.int32).reshape(1)
    tril = jnp.tril(jnp.ones((CHUNK, CHUNK), F32))
    shards = [weights[n].astype(BF16) for n in MATRIX_WEIGHTS]

    def layer_weights(i):
        got = dict(zip(MATRIX_WEIGHTS, _all_gather(shards, i, "weights_all_gather")))
        w_in_full = jnp.concatenate([got["w_in"][j] for j in range(N_DEV)], axis=-1)
        pad = jnp.zeros((d, D_IN_PAD - D_IN), BF16)
        wt = gm_w_s[i] * tril[None]
        lw = dict(
            w_in=jnp.concatenate([w_in_full[:, :3 * D_ATT], w_in_full[:, 3 * D_ATT + N_HEADS:],
                                  w_in_full[:, 3 * D_ATT:3 * D_ATT + N_HEADS], pad], axis=-1),
            w_out=got["w_out"].reshape(-1, d), w_ffn_in=got["w_ffn_in"],
            w_ffn_out=got["w_ffn_out"].reshape(N_DEV // 2, -1, d),
            w_ple=jnp.concatenate([got["w_ple"][j] for j in range(N_DEV)], axis=-1),
            w_ple_gate=got["w_ple_gate"].reshape(-1, d),
            b_forget=b_forget[i][:, None], wt=wt.astype(BF16), wt_t=wt.transpose(0, 2, 1).astype(BF16),
            bs_t=gm_b_s[i].T)
        lw.update({n: weights[n][i][None] for n in ("mix_pre_norm", "mix_post_norm", "gm_v_norm", "mix_out_norm",
                                                    "ffn_pre_norm", "ffn_post_norm", "ple_norm")})
        return lw

    layer_w = [layer_weights(i) for i in range(depth)]

    h = x[0]
    saved = []
    for i in range(depth):
        h, sv = _layer_fwd(h, p[i, 0], layer_w[i], tq)
        saved.append(sv)

    def loss_head(y, t):
        err = y - t
        return err * (1.0 / d), _colsum(err * err)

    dh, sq = _rowwise(loss_head, [h, loss_target[0]], [], [(d, F32)], [d], "loss_head", 256)
    loss = lax.psum(0.5 * jnp.sum(sq) / d, AXES)

    layer_g = [None] * depth
    shard_out = {n: [None] * depth for n in MATRIX_WEIGHTS}
    for i in reversed(range(depth)):
        dh, layer_g[i] = _layer_bwd(dh, saved[i], layer_w[i], tq)
        full_g = [layer_g[i][n] for n in MATRIX_WEIGHTS]
        from_sibling = _sibling_exchange(full_g)
        parts = [_pair_sum(gf, gs, core, "rs_pair_sum_" + n) for gf, gs, n in zip(full_g, from_sibling, MATRIX_WEIGHTS)]
        from_chips = _chip_exchange(parts)
        for n, part, got in zip(MATRIX_WEIGHTS, parts, from_chips):
            shard_out[n][i] = _adamw_shard(weights[n], mom_m[n], mom_v[n], part, got, chip, i, "adamw_" + n)
    grad_x = dh[None]

    grads, deltas, new_m, new_v = {}, {}, {}, {}
    for n in MATRIX_WEIGHTS:
        grads[n], deltas[n], new_m[n], new_v[n] = (jnp.stack([shard_out[n][i][k] for i in range(depth)])
                                                   for k in range(4))

    small_g = _pack_small([jnp.stack([layer_g[i][n].reshape(-1) for i in range(depth)]) for n in SMALL_WEIGHTS])
    (gathered,) = _all_gather([small_g], None, "small_grads_all_gather")
    g_small = _sum_devices(gathered)
    pack = lambda t: _pack_small([t[n] for n in SMALL_WEIGHTS])
    dl, nm, nv = _rowwise(_adamw_math, [pack(weights), g_small, pack(mom_m), pack(mom_v)], [],
                          [(SMALL_COLS, F32)] * 3, [], "adamw_small", SMALL_ROWS)
    off = 0
    for n in SMALL_WEIGHTS:
        shp, size = weights[n].shape, weights[n].size
        grads[n], deltas[n], new_m[n], new_v[n] = (a.reshape(-1)[off:off + size].reshape(shp)
                                                   for a in (g_small, dl, nm, nv))
        off += size

    return (loss, grad_x, *[grads[n] for n in WEIGHT_ORDER], *[deltas[n] for n in WEIGHT_ORDER],
            *[new_m[n] for n in WEIGHT_ORDER], *[new_v[n] for n in WEIGHT_ORDER])
```

```python
import functools
import math

import jax
import jax.numpy as jnp
from jax import lax
from jax.experimental import pallas as pl
from jax.experimental.pallas import tpu as pltpu

F32 = jnp.float32
BF16 = jnp.bfloat16
MESH = pl.DeviceIdType.MESH
AXES = ("x", "y", "c")
N_DEV = 8

EPS = 1e-6
NEG_INF = -1e30
N_HEADS = 8
HEAD_DIM = 64
D_ATT = N_HEADS * HEAD_DIM
N_GROUPS = 8
GROUP_DIM = 64
D_GM = N_GROUPS * GROUP_DIM
CHUNK = 128
ATT_SCALE = HEAD_DIM ** -0.5
ATT_BLOCK = 512
D_IN = 3 * D_ATT + N_HEADS + 2 * D_GM
D_IN_PAD = 3 * D_ATT + 2 * D_GM + 128
F_OFF = 3 * D_ATT + 2 * D_GM

ADAM_LR = 0.001
ADAM_B1 = 0.9
ADAM_B2 = 0.999
ADAM_EPS = 1e-08
ADAM_WD = 0.01
ADAM_STEP = 10

LANE = 128
VMEM_LIMIT = 48 * 1024 * 1024
SMALL_COLS = 128
SMALL_ROWS = 512

MATRIX_WEIGHTS = ("w_in", "w_out", "w_ffn_in", "w_ffn_out", "w_ple", "w_ple_gate")
SMALL_WEIGHTS = ("mix_pre_norm", "mix_post_norm", "b_forget", "gm_v_norm", "gm_w_s", "gm_b_s",
                 "mix_out_norm", "ffn_pre_norm", "ffn_post_norm", "ple_norm")
WEIGHT_ORDER = ("mix_pre_norm", "mix_post_norm", "w_in", "b_forget", "gm_v_norm", "gm_w_s", "gm_b_s",
                "mix_out_norm", "w_out", "ffn_pre_norm", "ffn_post_norm", "w_ffn_in", "w_ffn_out",
                "w_ple", "ple_norm", "w_ple_gate")


def _tile(n, pref, unit=LANE):
    best = None
    t = unit
    while t <= min(n, pref):
        if n % t == 0:
            best = t
        t += unit
    return n if best is None else best


def _cparams(*semantics):
    return pltpu.CompilerParams(dimension_semantics=semantics or None, vmem_limit_bytes=VMEM_LIMIT)


NN = (((1,), (0,)), ((), ()))
NT = (((1,), (1,)), ((), ()))
TN = (((0,), (0,)), ((), ()))
_MM_AXES = {
    "nn": ("i", "k", "k", "j"), "nt": ("i", "k", "j", "k"), "tn": ("k", "i", "k", "j")}
_MM_DN = {"nn": NN, "nt": NT, "tn": TN}


def _mm(a, b, dims, out_dtype, name, a3=None, b3=None, o3=None, tm=1024, tn=1024, tk=1024):
    ar, ac, br, bc = _MM_AXES[dims]
    letter = {"i": "m", "j": "n", "k": "k"}
    size = {}

    def measure(x, rows, cols, stacked):
        shape = x.shape
        if stacked is None:
            size.setdefault(letter[rows], shape[0])
            size.setdefault(letter[cols], shape[1])
        else:
            for ax, n in ((rows, shape[1]), (cols, shape[2])):
                size.setdefault(letter[ax], n * shape[0] if letter[ax] == stacked else n)

    measure(a, ar, ac, a3)
    measure(b, br, bc, b3)
    m, n, k = size["m"], size["n"], size["k"]
    slab = {}
    for x, stacked, rows, cols in ((a, a3, ar, ac), (b, b3, br, bc)):
        if stacked is not None:
            slab[stacked] = x.shape[1] if letter[rows] == stacked else x.shape[2]
    if o3 is not None:
        slab.setdefault(o3, slab.get(o3, None) or {"m": m, "n": n}[o3] // N_DEV)
    tile = {"m": slab.get("m") or _tile(m, tm), "n": slab.get("n") or _tile(n, tn), "k": slab.get("k") or _tile(k, tk)}
    nk = k // tile["k"]

    def spec(rows, cols, stacked):
        tr, tc = tile[letter[rows]], tile[letter[cols]]
        if stacked is None:
            return pl.BlockSpec((tr, tc), lambda i, j, kk: ({"i": i, "j": j, "k": kk}[rows], {"i": i, "j": j, "k": kk}[cols]))

        def imap(i, j, kk):
            g = {"i": i, "j": j, "k": kk}
            return (g[{"m": "i", "n": "j", "k": "k"}[stacked]],
                    0 if letter[rows] == stacked else g[rows], 0 if letter[cols] == stacked else g[cols])

        return pl.BlockSpec((1, tr, tc), imap)

    dn = _MM_DN[dims]

    def body(a_ref, b_ref, o_ref, *acc):
        av = a_ref[...] if a3 is None else a_ref[0]
        bv = b_ref[...] if b3 is None else b_ref[0]
        prod = lax.dot_general(av.astype(BF16), bv.astype(BF16), dn, preferred_element_type=F32)

        def emit(val):
            if o3 is None:
                o_ref[...] = val.astype(out_dtype)
            else:
                o_ref[0] = val.astype(out_dtype)

        if nk == 1:
            emit(prod)
            return
        (acc_ref,) = acc
        kk = pl.program_id(2)

        @pl.when(kk == 0)
        def _():
            acc_ref[...] = prod

        @pl.when(kk > 0)
        def _():
            acc_ref[...] += prod

        @pl.when(kk == nk - 1)
        def _():
            emit(acc_ref[...])

    if o3 is None:
        out_shape = (m, n)
    elif o3 == "m":
        out_shape = (m // tile["m"], tile["m"], n)
    else:
        out_shape = (n // tile["n"], m, tile["n"])
    return pl.pallas_call(
        body, name=name, out_shape=jax.ShapeDtypeStruct(out_shape, out_dtype),
        grid=(m // tile["m"], n // tile["n"], nk),
        in_specs=[spec(ar, ac, a3), spec(br, bc, b3)], out_specs=spec("i", "j", o3),
        scratch_shapes=[] if nk == 1 else [pltpu.VMEM((tile["m"], tile["n"]), F32)],
        compiler_params=_cparams("parallel", "parallel", "arbitrary"),
    )(a, b)


def _rowwise(fn, rows, vecs, outs, reds, name, ts):
    rows = [r if isinstance(r, tuple) else (r, 0, r.shape[1]) for r in rows]
    s = rows[0][0].shape[0]
    ts = _tile(s, ts, 8)
    nr, nv, no = len(rows), len(vecs), len(outs)

    def body(*refs):
        vals = fn(*[r[...] for r in refs[:nr + nv]])
        vals = vals if isinstance(vals, tuple) else (vals,)
        o_refs = refs[nr + nv:nr + nv + no]
        r_refs = refs[nr + nv + no:]
        for o_ref, val in zip(o_refs, vals[:no]):
            o_ref[...] = val.astype(o_ref.dtype)
        if r_refs:
            @pl.when(pl.program_id(0) == 0)
            def _():
                for r_ref in r_refs:
                    r_ref[...] = jnp.zeros_like(r_ref)

            for r_ref, val in zip(r_refs, vals[no:]):
                r_ref[...] += val

    in_specs = [pl.BlockSpec((ts, w), functools.partial(lambda i, cb: (i, cb), cb=cb)) for _, cb, w in rows]
    in_specs += [pl.BlockSpec(v.shape, lambda i: (0, 0)) for v in vecs]
    out_specs = [pl.BlockSpec((ts, c), lambda i: (i, 0)) for c, _ in outs]
    out_specs += [pl.BlockSpec((1, c), lambda i: (0, 0)) for c in reds]
    out_shape = [jax.ShapeDtypeStruct((s, c), dt) for c, dt in outs]
    out_shape += [jax.ShapeDtypeStruct((1, c), F32) for c in reds]
    return pl.pallas_call(
        body, name=name, out_shape=out_shape, grid=(s // ts,), in_specs=in_specs, out_specs=out_specs,
        compiler_params=_cparams("arbitrary" if reds else "parallel"),
    )(*[r[0] for r in rows], *vecs)


def _rms(x):
    r = lax.rsqrt(jnp.mean(x * x, axis=-1, keepdims=True) + EPS)
    return x * r, r


def _rms_bwd(xhat, r, dyg):
    return r * (dyg - xhat * jnp.mean(dyg * xhat, axis=-1, keepdims=True))


def _colsum(x):
    return jnp.sum(x, axis=0, keepdims=True)


def _sigmoid(x):
    return 1.0 / (1.0 + jnp.exp(-x))


GELU_C = math.sqrt(2.0 / math.pi)
GELU_A = 0.044715


def _gelu(x):
    return 0.5 * x * (1.0 + jnp.tanh(GELU_C * (x + GELU_A * x * x * x)))


def _gelu_grad(x):
    t = jnp.tanh(GELU_C * (x + GELU_A * x * x * x))
    return 0.5 * (1.0 + t) + 0.5 * x * (1.0 - t * t) * GELU_C * (1.0 + 3.0 * GELU_A * x * x)


def _ffn_in_swiglu(hn, wg):
    s, d = hn.shape
    g2, _, n = wg.shape
    g = g2 // 2
    tm = _tile(s, 1024)

    def body(h_ref, wa_ref, wb_ref, ab_ref, t_ref):
        hv = h_ref[...]
        a = jnp.dot(hv, wa_ref[0], preferred_element_type=F32)
        b = jnp.dot(hv, wb_ref[0], preferred_element_type=F32)
        ab_ref[0, 0] = a.astype(BF16)
        ab_ref[1, 0] = b.astype(BF16)
        t_ref[0] = (a * _sigmoid(a) * b).astype(BF16)

    return pl.pallas_call(
        body, name="mm_ffn_in_swiglu",
        out_shape=(jax.ShapeDtypeStruct((2, g, s, n), BF16), jax.ShapeDtypeStruct((g, s, n), BF16)),
        grid=(s // tm, g),
        in_specs=[pl.BlockSpec((tm, d), lambda i, j: (i, 0)), pl.BlockSpec((1, d, n), lambda i, j: (j, 0, 0)),
                  pl.BlockSpec((1, d, n), lambda i, j: (j + g, 0, 0))],
        out_specs=(pl.BlockSpec((2, 1, tm, n), lambda i, j: (0, j, i, 0)),
                   pl.BlockSpec((1, tm, n), lambda i, j: (j, i, 0))),
        compiler_params=_cparams("parallel", "parallel"),
    )(hn, wg, wg)


def _swiglu_bwd(ab, dt, ts):
    _, g, s, n = ab.shape
    ts = _tile(s, ts, 8)

    def body(ab_ref, dt_ref, t_ref, dab_ref):
        a = ab_ref[0, 0].astype(F32)
        b = ab_ref[1, 0].astype(F32)
        dt = dt_ref[0]
        sig = _sigmoid(a)
        silu = a * sig
        t_ref[0] = (silu * b).astype(BF16)
        dab_ref[0, 0] = (dt * b * (sig * (1.0 + a * (1.0 - sig)))).astype(BF16)
        dab_ref[1, 0] = (dt * silu).astype(BF16)

    both = pl.BlockSpec((2, 1, ts, n), lambda j, i: (0, j, i, 0))
    one = pl.BlockSpec((1, ts, n), lambda j, i: (j, i, 0))
    return pl.pallas_call(
        body, name="swiglu_bwd",
        out_shape=(jax.ShapeDtypeStruct((g, s, n), BF16), jax.ShapeDtypeStruct((2, g, s, n), BF16)),
        grid=(g, s // ts), in_specs=[both, one], out_specs=(one, both),
        compiler_params=_cparams("parallel", "parallel"),
    )(ab, dt)


def _forget_fwd(fl_t, b_col):
    h, s = fl_t.shape
    nb = s // LANE

    def body(fl_ref, b_ref, c_ref):
        upper = (lax.broadcasted_iota(jnp.int32, (LANE, LANE), 0)
                 <= lax.broadcasted_iota(jnp.int32, (LANE, LANE), 1)).astype(F32)

        def step(i, carry):
            x = fl_ref[i] + b_ref[...]
            lf = jnp.minimum(x, 0.0) - jnp.log(1.0 + jnp.exp(-jnp.abs(x)))
            cs = jnp.dot(lf, upper, precision=lax.Precision.HIGHEST, preferred_element_type=F32) + carry
            c_ref[i] = cs
            return cs[:, LANE - 1:LANE]

        lax.fori_loop(0, nb, step, jnp.zeros((h, 1), F32))

    out = pl.pallas_call(
        body, name="forget_fwd", out_shape=jax.ShapeDtypeStruct((nb, h, LANE), F32),
        compiler_params=_cparams(),
    )(fl_t.reshape(h, nb, LANE).transpose(1, 0, 2), b_col)
    return out.transpose(1, 0, 2).reshape(h, s)


def _forget_bwd(dc_t, fl_t, b_col):
    h, s = fl_t.shape
    nb = s // LANE

    def body(dc_ref, fl_ref, b_ref, dfl_ref, db_ref):
        lower = (lax.broadcasted_iota(jnp.int32, (LANE, LANE), 0)
                 >= lax.broadcasted_iota(jnp.int32, (LANE, LANE), 1)).astype(F32)

        def step(t, carry):
            tail, db = carry
            i = nb - 1 - t
            rc = jnp.dot(dc_ref[i], lower, precision=lax.Precision.HIGHEST, preferred_element_type=F32) + tail
            dfl = rc * (1.0 - _sigmoid(fl_ref[i] + b_ref[...]))
            dfl_ref[i] = dfl
            return rc[:, 0:1], db + jnp.sum(dfl, axis=1, keepdims=True)

        _, db = lax.fori_loop(0, nb, step, (jnp.zeros((h, 1), F32), jnp.zeros((h, 1), F32)))
        db_ref[...] = db

    blocked = lambda a: a.reshape(h, nb, LANE).transpose(1, 0, 2)
    dfl, db = pl.pallas_call(
        body, name="forget_bwd",
        out_shape=(jax.ShapeDtypeStruct((nb, h, LANE), F32), jax.ShapeDtypeStruct((h, 1), F32)),
        compiler_params=_cparams(),
    )(blocked(dc_t), blocked(fl_t), b_col)
    return dfl.transpose(1, 0, 2).reshape(h, s), db


N_PAIRS = N_HEADS // 2


def _causal_mask(t):
    return lax.broadcasted_iota(jnp.int32, (t, t), 0) >= lax.broadcasted_iota(jnp.int32, (t, t), 1)


def _head_lanes():
    return lax.broadcasted_iota(jnp.int32, (1, 2 * HEAD_DIM), 1) < HEAD_DIM


def _pick(x2, first, hh):
    zero = jnp.zeros_like(x2)
    return jnp.where(first, x2, zero) if hh == 0 else jnp.where(first, zero, x2)


def _attn_fwd(qkv, c_col, c_row, tq):
    s = qkv.shape[0]
    nq = s // tq
    w = 2 * HEAD_DIM

    def body(q_ref, k_ref, v_ref, cc_ref, cr_ref, o_ref, lse_ref):
        i = pl.program_id(1)
        first = _head_lanes()
        q2 = q_ref[...] * ATT_SCALE
        qs = [_pick(q2, first, hh) for hh in range(2)]

        def block(j, carry, masked):
            off = pl.multiple_of(j * tq, tq)
            k2 = k_ref[pl.ds(off, tq), :]
            v2 = v_ref[pl.ds(off, tq), :]
            new = []
            for hh in range(2):
                m, l, acc = carry[hh]
                sc = lax.dot_general(qs[hh], k2, NT, preferred_element_type=F32) + cc_ref[hh] - cr_ref[hh, j]
                if masked:
                    sc = jnp.where(_causal_mask(tq), sc, NEG_INF)
                m_new = jnp.maximum(m, jnp.max(sc, axis=-1, keepdims=True))
                alpha = jnp.exp(m - m_new)
                p = jnp.exp(sc - m_new)
                l = alpha * l + jnp.sum(p, axis=-1, keepdims=True)
                p_hi = p.astype(BF16)
                p_lo = (p - p_hi.astype(F32)).astype(BF16)
                acc = (alpha * acc + jnp.dot(p_hi, v2, preferred_element_type=F32)
                       + jnp.dot(p_lo, v2, preferred_element_type=F32))
                new.append((m_new, l, acc))
            return tuple(new)

        one = (jnp.full((tq, 1), NEG_INF, F32), jnp.zeros((tq, 1), F32), jnp.zeros((tq, w), F32))
        carry = lax.fori_loop(0, i, lambda j, c: block(j, c, False), (one, one))
        (m0, l0, a0), (m1, l1, a1) = block(i, carry, True)
        o_ref[...] = jnp.where(first, a0 / l0, a1 / l1)
        lse_ref[0] = m0 + jnp.log(l0)
        lse_ref[1] = m1 + jnp.log(l1)

    return pl.pallas_call(
        body, name="attn_fwd",
        out_shape=(jax.ShapeDtypeStruct((s, D_ATT), F32), jax.ShapeDtypeStruct((N_HEADS, s, 1), F32)),
        grid=(N_PAIRS, nq),
        in_specs=[pl.BlockSpec((tq, w), lambda hp, i: (i, hp)),
                  pl.BlockSpec((s, w), lambda hp, i: (0, N_PAIRS + hp)),
                  pl.BlockSpec((s, w), lambda hp, i: (0, 2 * N_PAIRS + hp)),
                  pl.BlockSpec((2, tq, 1), lambda hp, i: (hp, i, 0)),
                  pl.BlockSpec((2, nq, 1, tq), lambda hp, i: (hp, 0, 0, 0))],
        out_specs=(pl.BlockSpec((tq, w), lambda hp, i: (i, hp)),
                   pl.BlockSpec((2, tq, 1), lambda hp, i: (hp, i, 0))),
        compiler_params=_cparams("parallel", "parallel"),
    )(qkv, qkv, qkv, c_col, c_row)


def _attn_delta(o, do, tq):
    s = o.shape[0]
    w = 2 * HEAD_DIM

    def body(o_ref, do_ref, d_ref):
        first = _head_lanes()
        prod = o_ref[...] * do_ref[...].astype(F32)
        d_ref[0] = jnp.sum(_pick(prod, first, 0), axis=-1, keepdims=True)
        d_ref[1] = jnp.sum(_pick(prod, first, 1), axis=-1, keepdims=True)

    blk = pl.BlockSpec((tq, w), lambda hp, i: (i, hp))
    return pl.pallas_call(
        body, name="attn_delta", out_shape=jax.ShapeDtypeStruct((N_HEADS, s, 1), F32), grid=(N_PAIRS, s // tq),
        in_specs=[blk, blk], out_specs=pl.BlockSpec((2, tq, 1), lambda hp, i: (hp, i, 0)),
        compiler_params=_cparams("parallel", "parallel"),
    )(o, do)


def _attn_bwd(qkv, do, lse, delta, c_col, c_row, tq):
    s = qkv.shape[0]
    nq = s // tq
    w = 2 * HEAD_DIM

    def body(q_ref, do_ref, lse_ref, dl_ref, cc_ref, k_ref, v_ref, cr_ref, dq_ref, dk_ref, dv_ref, dc_ref):
        j = pl.program_id(1)
        first = _head_lanes()

        @pl.when(j == 0)
        def _():
            dq_ref[...] = jnp.zeros_like(dq_ref)

        k2 = k_ref[...]
        v2 = v_ref[...]

        def step(i, carry, masked):
            off = pl.multiple_of(i * tq, tq)
            rows = pl.ds(off, tq)
            q2 = q_ref[rows, :] * ATT_SCALE
            do2 = do_ref[rows, :]
            new, dqs = [], []
            for hh in range(2):
                dk, dv, dcs = carry[hh]
                sc = (lax.dot_general(_pick(q2, first, hh), k2, NT, preferred_element_type=F32)
                      + cc_ref[hh, rows, :] - cr_ref[hh, 0])
                if masked:
                    sc = jnp.where(_causal_mask(tq), sc, NEG_INF)
                p = jnp.exp(sc - lse_ref[hh, rows, :])
                dv = dv + lax.dot_general(p.astype(BF16), do2, TN, preferred_element_type=F32)
                dp = lax.dot_general(_pick(do2, first, hh), v2, NT, preferred_element_type=F32)
                ds = p * (dp - dl_ref[hh, rows, :])
                dsb = ds.astype(BF16)
                dk = dk + lax.dot_general(dsb, q2, TN, preferred_element_type=F32)
                dqs.append(jnp.dot(dsb, k2, preferred_element_type=F32))
                new.append((dk, dv, dcs + jnp.sum(ds, axis=0, keepdims=True)))
            dq_ref[rows, :] += jnp.where(first, dqs[0], dqs[1]) * ATT_SCALE
            return tuple(new)

        one = (jnp.zeros((tq, w), F32), jnp.zeros((tq, w), F32), jnp.zeros((1, tq), F32))
        carry = step(j, (one, one), True)
        (dk0, dv0, dc0), (dk1, dv1, dc1) = lax.fori_loop(j + 1, nq, lambda i, c: step(i, c, False), carry)
        dk_ref[...] = jnp.where(first, dk0, dk1)
        dv_ref[...] = jnp.where(first, dv0, dv1)
        dc_ref[0, 0] = -dc0
        dc_ref[1, 0] = -dc1

    whole_cols = lambda off: pl.BlockSpec((s, w), lambda hp, j: (0, off + hp))
    whole_heads = pl.BlockSpec((2, s, 1), lambda hp, j: (hp, 0, 0))
    blk = lambda off: pl.BlockSpec((tq, w), lambda hp, j: (j, off + hp))
    crow = pl.BlockSpec((2, 1, 1, tq), lambda hp, j: (hp, j, 0, 0))
    return pl.pallas_call(
        body, name="attn_bwd",
        out_shape=(jax.ShapeDtypeStruct((s, D_ATT), F32), jax.ShapeDtypeStruct((s, D_ATT), F32),
                   jax.ShapeDtypeStruct((s, D_ATT), F32), jax.ShapeDtypeStruct((N_HEADS, nq, 1, tq), F32)),
        grid=(N_PAIRS, nq),
        in_specs=[whole_cols(0), whole_cols(0), whole_heads, whole_heads, whole_heads,
                  blk(N_PAIRS), blk(2 * N_PAIRS), crow],
        out_specs=(whole_cols(0), blk(0), blk(0), crow),
        compiler_params=_cparams("parallel", "arbitrary"),
    )(qkv, do, lse, delta, c_col, qkv, qkv, c_row)


def _gm_group_norm(vg):
    mu = jnp.mean(vg, axis=-1, keepdims=True)
    d = vg - mu
    rstd = lax.rsqrt(jnp.mean(d * d, axis=-1, keepdims=True) + EPS)
    return d * rstd, rstd


def _gmlp_fwd(z, wt, bs_t, vgain):
    s = z.shape[0]

    def body(gu_ref, gv_ref, wt_ref, bs_ref, vg_ref, o_ref):
        for g in range(N_GROUPS):
            sl = slice(g * GROUP_DIM, (g + 1) * GROUP_DIM)
            vhat, _ = _gm_group_norm(_gelu(gv_ref[:, sl]))
            vn = vhat * vg_ref[:, sl]
            mixed = jnp.dot(wt_ref[g], vn.astype(BF16), preferred_element_type=F32) + bs_ref[:, g:g + 1]
            o_ref[:, sl] = _gelu(gu_ref[:, sl]) * mixed

    full = lambda a: pl.BlockSpec(a.shape, lambda n: (0,) * a.ndim)
    return pl.pallas_call(
        body, name="gmlp_fwd", out_shape=jax.ShapeDtypeStruct((s, D_GM), F32), grid=(s // CHUNK,),
        in_specs=[pl.BlockSpec((CHUNK, D_GM), lambda n: (n, 3)), pl.BlockSpec((CHUNK, D_GM), lambda n: (n, 4)),
                  full(wt), full(bs_t), full(vgain)],
        out_specs=pl.BlockSpec((CHUNK, D_GM), lambda n: (n, 0)),
        compiler_params=_cparams("parallel"),
    )(z, z, wt, bs_t, vgain)


def _gmlp_bwd(z, dgm, wt, wt_t, bs_t, vgain):
    s = z.shape[0]

    def body(gu_ref, gv_ref, dgm_ref, wt_ref, wtt_ref, bs_ref, vg_ref, dgu_ref, dgv_ref, dwt_ref, dbs_ref, dvg_ref):
        @pl.when(pl.program_id(0) == 0)
        def _():
            dwt_ref[...] = jnp.zeros_like(dwt_ref)
            dbs_ref[...] = jnp.zeros_like(dbs_ref)
            dvg_ref[...] = jnp.zeros_like(dvg_ref)

        for g in range(N_GROUPS):
            sl = slice(g * GROUP_DIM, (g + 1) * GROUP_DIM)
            gu = gu_ref[:, sl]
            gv = gv_ref[:, sl]
            dgm = dgm_ref[:, sl]
            vhat, rstd = _gm_group_norm(_gelu(gv))
            gain = vg_ref[:, sl]
            vn = (vhat * gain).astype(BF16)
            mixed = jnp.dot(wt_ref[g], vn, preferred_element_type=F32) + bs_ref[:, g:g + 1]
            dgu_ref[:, sl] = (dgm * mixed * _gelu_grad(gu)).astype(BF16)
            dmixed = dgm * _gelu(gu)
            dmb = dmixed.astype(BF16)
            dbs_ref[:, g:g + 1] += jnp.sum(dmixed, axis=-1, keepdims=True)
            dwt_ref[g] += lax.dot_general(dmb, vn, NT, preferred_element_type=F32)
            dvn = jnp.dot(wtt_ref[g], dmb, preferred_element_type=F32)
            dvg_ref[:, sl] += _colsum(dvn * vhat)
            dvhat = dvn * gain
            dvf = rstd * (dvhat - jnp.mean(dvhat, axis=-1, keepdims=True)
                          - vhat * jnp.mean(dvhat * vhat, axis=-1, keepdims=True))
            dgv_ref[:, sl] = (dvf * _gelu_grad(gv)).astype(BF16)

    full = lambda a: pl.BlockSpec(a.shape, lambda n: (0,) * a.ndim)
    chunk = pl.BlockSpec((CHUNK, D_GM), lambda n: (n, 0))
    return pl.pallas_call(
        body, name="gmlp_bwd",
        out_shape=(jax.ShapeDtypeStruct((s, D_GM), BF16), jax.ShapeDtypeStruct((s, D_GM), BF16),
                   jax.ShapeDtypeStruct(wt.shape, F32), jax.ShapeDtypeStruct(bs_t.shape, F32),
                   jax.ShapeDtypeStruct(vgain.shape, F32)),
        grid=(s // CHUNK,),
        in_specs=[pl.BlockSpec((CHUNK, D_GM), lambda n: (n, 3)), pl.BlockSpec((CHUNK, D_GM), lambda n: (n, 4)),
                  chunk, full(wt), full(wt_t), full(bs_t), full(vgain)],
        out_specs=(chunk, chunk, full(wt), full(bs_t), full(vgain)),
        compiler_params=_cparams("arbitrary"),
    )(z, z, dgm, wt, wt_t, bs_t, vgain)


def _layer_fwd(h0, p_i, w, tq):
    s, d = h0.shape
    nq = s // tq
    sv = {"h0": h0}

    (hn1,) = _rowwise(lambda h, g: _rms(h)[0] * g, [h0], [w["mix_pre_norm"]], [(d, BF16)], [], "pre_mix", 256)
    z = _mm(hn1, w["w_in"], "nn", F32, "mm_in")
    fl_t = z[:, F_OFF:F_OFF + N_HEADS].T
    c_t = _forget_fwd(fl_t, w["b_forget"])
    c_col = c_t[:, :, None]
    c_row = c_t.reshape(N_HEADS, nq, 1, tq)
    qkv = z[:, :3 * D_ATT].astype(BF16)
    att, lse = _attn_fwd(qkv, c_col, c_row, tq)
    gm = _gmlp_fwd(z, w["wt"], w["bs_t"], w["gm_v_norm"])

    def mix_out(att, gm, g):
        return jnp.concatenate([_rms(att)[0] * g[:, :D_ATT], _rms(gm)[0] * g[:, D_ATT:]], axis=-1)

    (mc,) = _rowwise(mix_out, [att, gm], [w["mix_out_norm"]], [(D_ATT + D_GM, BF16)], [], "mix_out", 256)
    y1 = _mm(mc, w["w_out"], "nn", F32, "mm_out")

    def post_mix(h0, y1, gpost, gpre):
        h1 = h0 + _rms(y1)[0] * gpost
        return h1, _rms(h1)[0] * gpre

    h1, hn2 = _rowwise(post_mix, [h0, y1], [w["mix_post_norm"], w["ffn_pre_norm"]],
                       [(d, F32), (d, BF16)], [], "post_mix", 256)
    ab, t = _ffn_in_swiglu(hn2, w["w_ffn_in"])
    y2 = _mm(t, w["w_ffn_out"], "nn", F32, "mm_ffn_out", a3="k", b3="k")

    def post_ffn(h1, y2, g):
        h2 = h1 + _rms(y2)[0] * g
        return h2, _rms(h2)[0]

    h2, hr = _rowwise(post_ffn, [h1, y2], [w["ffn_post_norm"]], [(d, F32), (d, BF16)], [], "post_ffn", 256)
    gl = _mm(hr, w["w_ple_gate"], "nn", F32, "mm_gate")
    pe = _mm(p_i, w["w_ple"], "nn", F32, "mm_ple")
    (h3,) = _rowwise(lambda h2, gl, pe, g: h2 + _sigmoid(gl) * (_rms(pe)[0] * g), [h2, gl, pe], [w["ple_norm"]],
                     [(d, F32)], [], "ple_out", 256)
    sv.update(hn1=hn1, z=z, fl_t=fl_t, c_col=c_col, c_row=c_row, qkv=qkv, lse=lse, att=att, gm=gm,
              mc=mc, y1=y1, h1=h1, hn2=hn2, ab=ab, y2=y2, h2=h2, hr=hr, gl=gl, pe=pe, p_i=p_i)
    return h3, sv


def _layer_bwd(dh3, sv, w, tq):
    s, d = dh3.shape
    g = {}
    by_rows = lambda a: a.reshape(N_DEV, -1, a.shape[-1])
    by_cols = lambda a: jnp.stack(jnp.split(a, N_DEV, axis=-1))

    def ple_bwd(dh3, gl, pe, gple):
        gate = _sigmoid(gl)
        pehat, rpe = _rms(pe)
        dgl = dh3 * (pehat * gple) * gate * (1.0 - gate)
        de = dh3 * gate
        return dgl, _rms_bwd(pehat, rpe, de * gple), _colsum(de * pehat)

    dgl, dpe, g["ple_norm"] = _rowwise(ple_bwd, [dh3, sv["gl"], sv["pe"]], [w["ple_norm"]],
                                       [(d, BF16), (d, BF16)], [d], "ple_bwd", 256)
    g["w_ple_gate"] = by_rows(_mm(sv["hr"], dgl, "tn", BF16, "mm_dgate"))
    dhr = _mm(dgl, w["w_ple_gate"], "nt", F32, "mm_dhr")
    g["w_ple"] = by_cols(_mm(sv["p_i"], dpe, "tn", BF16, "mm_dple"))

    def ffn_post_bwd(dh3, dhr, h2, y2, gpost):
        h2hat, r2 = _rms(h2)
        dh2 = dh3 + _rms_bwd(h2hat, r2, dhr)
        y2hat, ry = _rms(y2)
        return dh2, _rms_bwd(y2hat, ry, dh2 * gpost), _colsum(dh2 * y2hat)

    dh2, dy2, g["ffn_post_norm"] = _rowwise(ffn_post_bwd, [dh3, dhr, sv["h2"], sv["y2"]], [w["ffn_post_norm"]],
                                            [(d, F32), (d, BF16)], [d], "ffn_post_bwd", 256)
    dt = _mm(dy2, w["w_ffn_out"], "nt", F32, "mm_dt", b3="n", o3="n")
    t, dab = _swiglu_bwd(sv["ab"], dt, 256)
    dab = dab.reshape((N_DEV,) + dab.shape[2:])
    g["w_ffn_out"] = by_rows(_mm(t, dy2, "tn", BF16, "mm_dffn_out", a3="m", o3="m"))
    dhn2 = _mm(dab, w["w_ffn_in"], "nt", F32, "mm_dhn2", a3="k", b3="k")
    g["w_ffn_in"] = _mm(sv["hn2"], dab, "tn", BF16, "mm_dffn_in", b3="n", o3="n")

    def mix_post_bwd(dh2, dhn2, h1, y1, gpre, gpost):
        h1hat, r1 = _rms(h1)
        dh1 = dh2 + _rms_bwd(h1hat, r1, dhn2 * gpre)
        y1hat, ry = _rms(y1)
        return dh1, _rms_bwd(y1hat, ry, dh1 * gpost), _colsum(dhn2 * h1hat), _colsum(dh1 * y1hat)

    dh1, dy1, g["ffn_pre_norm"], g["mix_post_norm"] = _rowwise(
        mix_post_bwd, [dh2, dhn2, sv["h1"], sv["y1"]], [w["ffn_pre_norm"], w["mix_post_norm"]],
        [(d, F32), (d, BF16)], [d, d], "mix_post_bwd", 256)
    dmc = _mm(dy1, w["w_out"], "nt", F32, "mm_dmc")
    g["w_out"] = by_rows(_mm(sv["mc"], dy1, "tn", BF16, "mm_dout"))

    def mix_out_bwd(da, dg, att, gm, gain):
        atthat, ra = _rms(att)
        gmhat, rg = _rms(gm)
        dgain = jnp.concatenate([_colsum(da * atthat), _colsum(dg * gmhat)], axis=-1)
        return _rms_bwd(atthat, ra, da * gain[:, :D_ATT]), _rms_bwd(gmhat, rg, dg * gain[:, D_ATT:]), dgain

    datt, dgm, g["mix_out_norm"] = _rowwise(
        mix_out_bwd, [(dmc, 0, D_ATT), (dmc, 1, D_GM), sv["att"], sv["gm"]], [w["mix_out_norm"]],
        [(D_ATT, BF16), (D_GM, F32)], [D_ATT + D_GM], "mix_out_bwd", 256)

    dgu, dgv, dwt, dbs_t, g["gm_v_norm"] = _gmlp_bwd(sv["z"], dgm, w["wt"], w["wt_t"], w["bs_t"], w["gm_v_norm"])
    g["gm_w_s"] = dwt * jnp.tril(jnp.ones((CHUNK, CHUNK), F32))[None]
    g["gm_b_s"] = dbs_t.T

    delta = _attn_delta(sv["att"], datt, tq)
    dq, dk, dv, dc_row = _attn_bwd(sv["qkv"], datt, sv["lse"], delta, sv["c_col"], sv["c_row"], tq)
    dfl_t, db = _forget_bwd(dc_row.reshape(N_HEADS, s), sv["fl_t"], w["b_forget"])
    g["b_forget"] = db.reshape(1, N_HEADS)
    dz = jnp.concatenate([dq.astype(BF16), dk.astype(BF16), dv.astype(BF16), dgu, dgv, dfl_t.T.astype(BF16),
                          jnp.zeros((s, D_IN_PAD - F_OFF - N_HEADS), BF16)], axis=-1)
    dhn1 = _mm(dz, w["w_in"], "nt", F32, "mm_dhn1")
    din = _mm(sv["hn1"], dz, "tn", BF16, "mm_din")
    din = jnp.concatenate([din[:, :3 * D_ATT], din[:, F_OFF:F_OFF + N_HEADS], din[:, 3 * D_ATT:F_OFF]], axis=-1)
    g["w_in"] = by_cols(din)

    def mix_pre_bwd(dh1, dhn1, h0, gpre):
        h0hat, r0 = _rms(h0)
        return dh1 + _rms_bwd(h0hat, r0, dhn1 * gpre), _colsum(dhn1 * h0hat)

    dh0, g["mix_pre_norm"] = _rowwise(mix_pre_bwd, [dh1, dhn1, sv["h0"]], [w["mix_pre_norm"]],
                                      [(d, F32)], [d], "mix_pre_bwd", 256)
    return dh0, g


ANY = pl.BlockSpec(memory_space=pl.ANY)


def _all_gather(xs, layer, name):
    n = len(xs)

    def body(*refs):
        x_refs, out_refs = refs[:n], refs[n:2 * n]
        send_sems, recv_sems, local_sems = refs[2 * n:]
        x, y, c = lax.axis_index("x"), lax.axis_index("y"), lax.axis_index("c")
        me, sibling = (x, y, c), (x, y, 1 - c)
        chips = [(1 - x, y), (x, 1 - y), (1 - x, 1 - y)]

        def shard(a):
            return x_refs[a] if layer is None else x_refs[a].at[layer]

        def rows(a, px, py, pc):
            return out_refs[a].at[4 * px + 2 * py + pc]

        def copy(a, kk, block, to, from_shard=False):
            return pltpu.make_async_remote_copy(
                src_ref=shard(a) if from_shard else rows(a, *block), dst_ref=rows(a, *block),
                send_sem=send_sems.at[7 * a + kk], recv_sem=recv_sems.at[7 * a + kk],
                device_id=to, device_id_type=MESH)

        mine = [pltpu.make_async_copy(shard(a), rows(a, *me), local_sems.at[a]) for a in range(n)]
        for cp in mine:
            cp.start()
        first = []
        for a in range(n):
            first.append(copy(a, 0, me, sibling, from_shard=True))
            first += [copy(a, 1 + j, me, (*chip, c), from_shard=True) for j, chip in enumerate(chips)]
        for cp in first:
            cp.start()
        passed = []
        for j, chip in enumerate(chips):
            for a in range(n):
                copy(a, 1 + j, (*chip, c), me).wait_recv()
                passed.append(copy(a, 4 + j, (*chip, c), sibling))
                passed[-1].start()
        for a in range(n):
            copy(a, 0, sibling, me).wait_recv()
        for j, chip in enumerate(chips):
            for a in range(n):
                copy(a, 4 + j, (*chip, 1 - c), me).wait_recv()
        for cp in first + passed:
            cp.wait_send()
        for cp in mine:
            cp.wait()

    shapes = [x.shape if layer is None else x.shape[1:] for x in xs]
    return pl.pallas_call(
        body, name=name, out_shape=[jax.ShapeDtypeStruct((N_DEV,) + sh, x.dtype) for sh, x in zip(shapes, xs)],
        in_specs=[ANY] * n, out_specs=[ANY] * n,
        scratch_shapes=[pltpu.SemaphoreType.DMA((7 * n,)), pltpu.SemaphoreType.DMA((7 * n,)),
                        pltpu.SemaphoreType.DMA((n,))],
    )(*xs)


HBM = pl.BlockSpec(memory_space=pltpu.HBM)
SEMS = pl.BlockSpec(memory_space=pltpu.SEMAPHORE)
EFFECT = pltpu.SideEffectType.DATAFLOW_SIDE_EFFECTING
FLIPS = tuple((fx, fy, fc) for fx in (0, 1) for fy in (0, 1) for fc in (0, 1))[1:]


def _exchange_copies(src_refs, land_refs, send_sems, recv_sems, layer, scatter):
    x, y, c = lax.axis_index("x"), lax.axis_index("y"), lax.axis_index("c")
    me = 4 * x + 2 * y + c
    copies = []
    for a, (src, land) in enumerate(zip(src_refs, land_refs)):
        for f, (fx, fy, fc) in enumerate(FLIPS):
            px, py, pc = (1 - x if fx else x), (1 - y if fy else y), (1 - c if fc else c)
            if scatter:
                block = src.at[4 * px + 2 * py + pc]
            else:
                block = src if layer is None else src.at[layer]
            copies.append(pltpu.make_async_remote_copy(
                src_ref=block, dst_ref=land.at[me], send_sem=send_sems.at[7 * a + f], recv_sem=recv_sems.at[7 * a + f],
                device_id=(px, py, pc), device_id_type=MESH))
    return copies


def _exchange_start(srcs, lands, layer, scatter, name):
    n = len(srcs)

    def body(*refs):
        for cp in _exchange_copies(refs[:n], refs[n:2 * n], refs[2 * n], refs[2 * n + 1], layer, scatter):
            cp.start()
        token = refs[-1]
        token[...] = jnp.zeros_like(token)

    operands = list(srcs) + list(lands)
    outs = pl.pallas_call(
        body, name=name,
        out_shape=(pltpu.SemaphoreType.DMA((7 * n,)), pltpu.SemaphoreType.DMA((7 * n,)),
                   *[pltpu.HBM(a.shape, a.dtype) for a in operands], jax.ShapeDtypeStruct((8, LANE), F32)),
        in_specs=[HBM] * (2 * n),
        out_specs=(SEMS, SEMS, *[HBM] * (2 * n), pl.BlockSpec(memory_space=pltpu.VMEM)),
        input_output_aliases={i: 2 + i for i in range(2 * n)},
        compiler_params=pltpu.CompilerParams(has_side_effects=EFFECT),
    )(*[pltpu.with_memory_space_constraint(a, pltpu.HBM) for a in operands])
    return outs[0], outs[1], outs[2:2 + n], outs[2 + n:2 + 2 * n], outs[-1]


def _exchange_wait(started, after, layer, scatter, name):
    send_sems, recv_sems, srcs, lands, _ = started
    n = len(srcs)

    def body(*refs):
        for cp in _exchange_copies(refs[:n], refs[n:2 * n], refs[2 * n], refs[2 * n + 1], layer, scatter):
            cp.wait_send()
            cp.wait_recv()

    operands = list(srcs) + list(lands)
    outs = pl.pallas_call(
        body, name=name, out_shape=tuple(pltpu.HBM(a.shape, a.dtype) for a in operands),
        in_specs=[HBM] * (2 * n) + [SEMS, SEMS, ANY], out_specs=[HBM] * (2 * n),
        input_output_aliases={i: i for i in range(2 * n)},
        compiler_params=pltpu.CompilerParams(has_side_effects=EFFECT),
    )(*operands, send_sems, recv_sems, after)
    return outs[:n], outs[n:]


def _sum_devices(parts):
    _, r, c = parts.shape

    def body(p_ref, o_ref):
        acc = p_ref[0]
        for j in range(1, N_DEV):
            acc = acc + p_ref[j]
        o_ref[...] = acc

    return pl.pallas_call(
        body, name="small_sum", out_shape=jax.ShapeDtypeStruct((r, c), F32), grid=(r // SMALL_ROWS,),
        in_specs=[pl.BlockSpec((N_DEV, SMALL_ROWS, c), lambda i: (0, i, 0))],
        out_specs=pl.BlockSpec((SMALL_ROWS, c), lambda i: (i, 0)),
        compiler_params=_cparams("parallel"),
    )(parts)


def _adamw_math(w, g, m, v):
    m = ADAM_B1 * m + (1.0 - ADAM_B1) * g
    v = ADAM_B2 * v + (1.0 - ADAM_B2) * (g * g)
    m_hat = m / (1.0 - ADAM_B1 ** ADAM_STEP)
    v_hat = v / (1.0 - ADAM_B2 ** ADAM_STEP)
    return -ADAM_LR * (m_hat / (jnp.sqrt(v_hat) + ADAM_EPS) + ADAM_WD * w), m, v


def _adamw_shard(w, m, v, parts, layer, name):
    _, a, b = w.shape
    ta = _tile(a, 256, 16)

    def body(w_ref, m_ref, v_ref, p_ref, g_ref, d_ref, nm_ref, nv_ref):
        g = p_ref[0].astype(F32)
        for j in range(1, N_DEV):
            g = g + p_ref[j].astype(F32)
        g_ref[...] = g
        d_ref[...], nm_ref[...], nv_ref[...] = _adamw_math(w_ref[0], g, m_ref[0], v_ref[0])

    mine = pl.BlockSpec((1, ta, b), lambda i: (layer, i, 0))
    out = pl.BlockSpec((ta, b), lambda i: (i, 0))
    return pl.pallas_call(
        body, name=name, out_shape=[jax.ShapeDtypeStruct((a, b), F32)] * 4, grid=(a // ta,),
        in_specs=[mine, mine, mine, pl.BlockSpec((N_DEV, ta, b), lambda i: (0, i, 0))], out_specs=[out] * 4,
        compiler_params=_cparams("parallel"),
    )(w, m, v, parts)


def _pack_small(pieces):
    flat = jnp.concatenate([p.reshape(-1) for p in pieces])
    total = -(-flat.shape[0] // (SMALL_COLS * SMALL_ROWS)) * SMALL_COLS * SMALL_ROWS
    return jnp.pad(flat, (0, total - flat.shape[0])).reshape(-1, SMALL_COLS)


def kernel(x, p, mix_pre_norm, mix_post_norm, w_in, b_forget, gm_v_norm, gm_w_s, gm_b_s, mix_out_norm, w_out, ffn_pre_norm, ffn_post_norm, w_ffn_in, w_ffn_out, w_ple, ple_norm, w_ple_gate, loss_target, m_mix_pre_norm, m_mix_post_norm, m_w_in, m_b_forget, m_gm_v_norm, m_gm_w_s, m_gm_b_s, m_mix_out_norm, m_w_out, m_ffn_pre_norm, m_ffn_post_norm, m_w_ffn_in, m_w_ffn_out, m_w_ple, m_ple_norm, m_w_ple_gate, v_mix_pre_norm, v_mix_post_norm, v_w_in, v_b_forget, v_gm_v_norm, v_gm_w_s, v_gm_b_s, v_mix_out_norm, v_w_out, v_ffn_pre_norm, v_ffn_post_norm, v_w_ffn_in, v_w_ffn_out, v_w_ple, v_ple_norm, v_w_ple_gate):
    given = dict(locals())
    weights = {n: given[n] for n in WEIGHT_ORDER}
    mom_m = {n: given["m_" + n] for n in WEIGHT_ORDER}
    mom_v = {n: given["v_" + n] for n in WEIGHT_ORDER}
    depth = w_in.shape[0]
    s, d = x.shape[1], x.shape[2]
    tq = _tile(s, ATT_BLOCK)
    me = 4 * lax.axis_index("x") + 2 * lax.axis_index("y") + lax.axis_index("c")
    tril = jnp.tril(jnp.ones((CHUNK, CHUNK), F32))

    def landing(block):
        return lax.dynamic_update_index_in_dim(lax.empty((N_DEV,) + block.shape, block.dtype), block, me, 0)

    def layer_weights(i, got):
        got = dict(zip(MATRIX_WEIGHTS, got))
        w_in_full = jnp.concatenate([got["w_in"][j] for j in range(N_DEV)], axis=-1)
        pad = jnp.zeros((d, D_IN_PAD - D_IN), BF16)
        wt = gm_w_s[i] * tril[None]
        lw = dict(
            w_in=jnp.concatenate([w_in_full[:, :3 * D_ATT], w_in_full[:, 3 * D_ATT + N_HEADS:],
                                  w_in_full[:, 3 * D_ATT:3 * D_ATT + N_HEADS], pad], axis=-1),
            w_out=got["w_out"].reshape(-1, d), w_ffn_in=got["w_ffn_in"],
            w_ffn_out=got["w_ffn_out"].reshape(N_DEV // 2, -1, d),
            w_ple=jnp.concatenate([got["w_ple"][j] for j in range(N_DEV)], axis=-1),
            w_ple_gate=got["w_ple_gate"].reshape(-1, d),
            b_forget=b_forget[i][:, None], wt=wt.astype(BF16), wt_t=wt.transpose(0, 2, 1).astype(BF16),
            bs_t=gm_b_s[i].T)
        lw.update({n: weights[n][i][None] for n in ("mix_pre_norm", "mix_post_norm", "gm_v_norm", "mix_out_norm",
                                                    "ffn_pre_norm", "ffn_post_norm", "ple_norm")})
        return lw

    def gather_start(i, shards):
        return _exchange_start(shards, [landing(sh[i]) for sh in shards], i, False, f"weights_gather_start_{i}")

    h = x[0]
    saved, layer_w = [], []
    started = gather_start(0, [weights[n].astype(BF16) for n in MATRIX_WEIGHTS])
    for i in range(depth):
        shards, got = _exchange_wait(started, h, i, False, f"weights_gather_wait_{i}")
        lw = layer_weights(i, got)
        layer_w.append(lw)
        if i + 1 < depth:
            started = gather_start(i + 1, shards)
            lw = dict(lw, mix_pre_norm=lw["mix_pre_norm"] + started[4][:1, :1])
        h, sv = _layer_fwd(h, p[i, 0], lw, tq)
        saved.append(sv)

    def loss_head(y, t):
        err = y - t
        return err * (1.0 / d), _colsum(err * err)

    dh, sq = _rowwise(loss_head, [h, loss_target[0]], [], [(d, F32)], [d], "loss_head", 256)
    loss = lax.psum(0.5 * jnp.sum(sq) / d, AXES)

    layer_g = [None] * depth
    shard_out = {n: [None] * depth for n in MATRIX_WEIGHTS}

    def scatter_finish(i, started, after):
        _, parts = _exchange_wait(started, after, None, True, f"grads_scatter_wait_{i}")
        for n, part in zip(MATRIX_WEIGHTS, parts):
            shard_out[n][i] = _adamw_shard(weights[n], mom_m[n], mom_v[n], part, i, "adamw_" + n)

    started = None
    for i in reversed(range(depth)):
        lw = layer_w[i]
        if started is not None:
            lw = dict(lw, ple_norm=lw["ple_norm"] + started[4][:1, :1])
        dh, layer_g[i] = _layer_bwd(dh, saved[i], lw, tq)
        if started is not None:
            scatter_finish(i + 1, started, dh)
        full_g = [layer_g[i][n] for n in MATRIX_WEIGHTS]
        lands = [landing(lax.dynamic_index_in_dim(gf, me, 0, keepdims=False)) for gf in full_g]
        started = _exchange_start(full_g, lands, None, True, f"grads_scatter_start_{i}")
    scatter_finish(0, started, dh)
    grad_x = dh[None]

    grads, deltas, new_m, new_v = {}, {}, {}, {}
    for n in MATRIX_WEIGHTS:
        grads[n], deltas[n], new_m[n], new_v[n] = (jnp.stack([shard_out[n][i][k] for i in range(depth)])
                                                   for k in range(4))

    small_g = _pack_small([jnp.stack([layer_g[i][n].reshape(-1) for i in range(depth)]) for n in SMALL_WEIGHTS])
    (gathered,) = _all_gather([small_g], None, "small_grads_all_gather")
    g_small = _sum_devices(gathered)
    pack = lambda t: _pack_small([t[n] for n in SMALL_WEIGHTS])
    dl, nm, nv = _rowwise(_adamw_math, [pack(weights), g_small, pack(mom_m), pack(mom_v)], [],
                          [(SMALL_COLS, F32)] * 3, [], "adamw_small", SMALL_ROWS)
    off = 0
    for n in SMALL_WEIGHTS:
        shp, size = weights[n].shape, weights[n].size
        grads[n], deltas[n], new_m[n], new_v[n] = (a.reshape(-1)[off:off + size].reshape(shp)
                                                   for a in (g_small, dl, nm, nv))
        off += size

    return (loss, grad_x, *[grads[n] for n in WEIGHT_ORDER], *[deltas[n] for n in WEIGHT_ORDER],
            *[new_m[n] for n in WEIGHT_ORDER], *[new_v[n] for n in WEIGHT_ORDER])
```

```python
import functools
import math

import jax
import jax.numpy as jnp
from jax import lax
from jax.experimental import pallas as pl
from jax.experimental.pallas import tpu as pltpu

F32 = jnp.float32
BF16 = jnp.bfloat16
MESH = pl.DeviceIdType.MESH
AXES = ("x", "y", "c")
N_DEV = 8

EPS = 1e-6
NEG_INF = -1e30
N_HEADS = 8
HEAD_DIM = 64
D_ATT = N_HEADS * HEAD_DIM
N_GROUPS = 8
GROUP_DIM = 64
D_GM = N_GROUPS * GROUP_DIM
CHUNK = 128
ATT_SCALE = HEAD_DIM ** -0.5
ATT_BLOCK = 512
D_IN = 3 * D_ATT + N_HEADS + 2 * D_GM
D_IN_PAD = 3 * D_ATT + 2 * D_GM + 128
F_OFF = 3 * D_ATT + 2 * D_GM

ADAM_LR = 0.001
ADAM_B1 = 0.9
ADAM_B2 = 0.999
ADAM_EPS = 1e-08
ADAM_WD = 0.01
ADAM_STEP = 10

LANE = 128
VMEM_LIMIT = 48 * 1024 * 1024
SMALL_COLS = 128
SMALL_ROWS = 512

MATRIX_WEIGHTS = ("w_in", "w_out", "w_ffn_in", "w_ffn_out", "w_ple", "w_ple_gate")
EXCHANGE_GROUPS = {"mix": ("w_in",), "rest": ("w_out", "w_ffn_in", "w_ffn_out", "w_ple", "w_ple_gate")}
SMALL_WEIGHTS = ("mix_pre_norm", "mix_post_norm", "b_forget", "gm_v_norm", "gm_w_s", "gm_b_s",
                 "mix_out_norm", "ffn_pre_norm", "ffn_post_norm", "ple_norm")
WEIGHT_ORDER = ("mix_pre_norm", "mix_post_norm", "w_in", "b_forget", "gm_v_norm", "gm_w_s", "gm_b_s",
                "mix_out_norm", "w_out", "ffn_pre_norm", "ffn_post_norm", "w_ffn_in", "w_ffn_out",
                "w_ple", "ple_norm", "w_ple_gate")


def _tile(n, pref, unit=LANE):
    best = None
    t = unit
    while t <= min(n, pref):
        if n % t == 0:
            best = t
        t += unit
    return n if best is None else best


def _cparams(*semantics):
    return pltpu.CompilerParams(dimension_semantics=semantics or None, vmem_limit_bytes=VMEM_LIMIT)


NN = (((1,), (0,)), ((), ()))
NT = (((1,), (1,)), ((), ()))
TN = (((0,), (0,)), ((), ()))
_MM_AXES = {
    "nn": ("i", "k", "k", "j"), "nt": ("i", "k", "j", "k"), "tn": ("k", "i", "k", "j")}
_MM_DN = {"nn": NN, "nt": NT, "tn": TN}


def _mm(a, b, dims, out_dtype, name, a3=None, b3=None, o3=None, tm=1024, tn=1024, tk=1024):
    ar, ac, br, bc = _MM_AXES[dims]
    letter = {"i": "m", "j": "n", "k": "k"}
    size = {}

    def measure(x, rows, cols, stacked):
        shape = x.shape
        if stacked is None:
            size.setdefault(letter[rows], shape[0])
            size.setdefault(letter[cols], shape[1])
        else:
            for ax, n in ((rows, shape[1]), (cols, shape[2])):
                size.setdefault(letter[ax], n * shape[0] if letter[ax] == stacked else n)

    measure(a, ar, ac, a3)
    measure(b, br, bc, b3)
    m, n, k = size["m"], size["n"], size["k"]
    slab = {}
    for x, stacked, rows, cols in ((a, a3, ar, ac), (b, b3, br, bc)):
        if stacked is not None:
            slab[stacked] = x.shape[1] if letter[rows] == stacked else x.shape[2]
    if o3 is not None:
        slab.setdefault(o3, slab.get(o3, None) or {"m": m, "n": n}[o3] // N_DEV)
    tile = {"m": slab.get("m") or _tile(m, tm), "n": slab.get("n") or _tile(n, tn), "k": slab.get("k") or _tile(k, tk)}
    nk = k // tile["k"]

    def spec(rows, cols, stacked):
        tr, tc = tile[letter[rows]], tile[letter[cols]]
        if stacked is None:
            return pl.BlockSpec((tr, tc), lambda i, j, kk: ({"i": i, "j": j, "k": kk}[rows], {"i": i, "j": j, "k": kk}[cols]))

        def imap(i, j, kk):
            g = {"i": i, "j": j, "k": kk}
            return (g[{"m": "i", "n": "j", "k": "k"}[stacked]],
                    0 if letter[rows] == stacked else g[rows], 0 if letter[cols] == stacked else g[cols])

        return pl.BlockSpec((1, tr, tc), imap)

    dn = _MM_DN[dims]

    def body(a_ref, b_ref, o_ref, *acc):
        av = a_ref[...] if a3 is None else a_ref[0]
        bv = b_ref[...] if b3 is None else b_ref[0]
        prod = lax.dot_general(av.astype(BF16), bv.astype(BF16), dn, preferred_element_type=F32)

        def emit(val):
            if o3 is None:
                o_ref[...] = val.astype(out_dtype)
            else:
                o_ref[0] = val.astype(out_dtype)

        if nk == 1:
            emit(prod)
            return
        (acc_ref,) = acc
        kk = pl.program_id(2)

        @pl.when(kk == 0)
        def _():
            acc_ref[...] = prod

        @pl.when(kk > 0)
        def _():
            acc_ref[...] += prod

        @pl.when(kk == nk - 1)
        def _():
            emit(acc_ref[...])

    if o3 is None:
        out_shape = (m, n)
    elif o3 == "m":
        out_shape = (m // tile["m"], tile["m"], n)
    else:
        out_shape = (n // tile["n"], m, tile["n"])
    return pl.pallas_call(
        body, name=name, out_shape=jax.ShapeDtypeStruct(out_shape, out_dtype),
        grid=(m // tile["m"], n // tile["n"], nk),
        in_specs=[spec(ar, ac, a3), spec(br, bc, b3)], out_specs=spec("i", "j", o3),
        scratch_shapes=[] if nk == 1 else [pltpu.VMEM((tile["m"], tile["n"]), F32)],
        compiler_params=_cparams("parallel", "parallel", "arbitrary"),
    )(a, b)


def _rowwise(fn, rows, vecs, outs, reds, name, ts):
    rows = [r if isinstance(r, tuple) else (r, 0, r.shape[1]) for r in rows]
    s = rows[0][0].shape[0]
    ts = _tile(s, ts, 8)
    nr, nv, no = len(rows), len(vecs), len(outs)

    def body(*refs):
        vals = fn(*[r[...] for r in refs[:nr + nv]])
        vals = vals if isinstance(vals, tuple) else (vals,)
        o_refs = refs[nr + nv:nr + nv + no]
        r_refs = refs[nr + nv + no:]
        for o_ref, val in zip(o_refs, vals[:no]):
            o_ref[...] = val.astype(o_ref.dtype)
        if r_refs:
            @pl.when(pl.program_id(0) == 0)
            def _():
                for r_ref in r_refs:
                    r_ref[...] = jnp.zeros_like(r_ref)

            for r_ref, val in zip(r_refs, vals[no:]):
                r_ref[...] += val

    in_specs = [pl.BlockSpec((ts, w), functools.partial(lambda i, cb: (i, cb), cb=cb)) for _, cb, w in rows]
    in_specs += [pl.BlockSpec(v.shape, lambda i: (0, 0)) for v in vecs]
    out_specs = [pl.BlockSpec((ts, c), lambda i: (i, 0)) for c, _ in outs]
    out_specs += [pl.BlockSpec((1, c), lambda i: (0, 0)) for c in reds]
    out_shape = [jax.ShapeDtypeStruct((s, c), dt) for c, dt in outs]
    out_shape += [jax.ShapeDtypeStruct((1, c), F32) for c in reds]
    return pl.pallas_call(
        body, name=name, out_shape=out_shape, grid=(s // ts,), in_specs=in_specs, out_specs=out_specs,
        compiler_params=_cparams("arbitrary" if reds else "parallel"),
    )(*[r[0] for r in rows], *vecs)


def _rms(x):
    r = lax.rsqrt(jnp.mean(x * x, axis=-1, keepdims=True) + EPS)
    return x * r, r


def _rms_bwd(xhat, r, dyg):
    return r * (dyg - xhat * jnp.mean(dyg * xhat, axis=-1, keepdims=True))


def _colsum(x):
    return jnp.sum(x, axis=0, keepdims=True)


def _sigmoid(x):
    return 1.0 / (1.0 + jnp.exp(-x))


GELU_C = math.sqrt(2.0 / math.pi)
GELU_A = 0.044715


def _gelu(x):
    return 0.5 * x * (1.0 + jnp.tanh(GELU_C * (x + GELU_A * x * x * x)))


def _gelu_grad(x):
    t = jnp.tanh(GELU_C * (x + GELU_A * x * x * x))
    return 0.5 * (1.0 + t) + 0.5 * x * (1.0 - t * t) * GELU_C * (1.0 + 3.0 * GELU_A * x * x)


def _ffn_in_swiglu(hn, wg):
    s, d = hn.shape
    g2, _, n = wg.shape
    g = g2 // 2
    tm = _tile(s, 1024)

    def body(h_ref, wa_ref, wb_ref, ab_ref, t_ref):
        hv = h_ref[...]
        a = jnp.dot(hv, wa_ref[0], preferred_element_type=F32)
        b = jnp.dot(hv, wb_ref[0], preferred_element_type=F32)
        ab_ref[0, 0] = a.astype(BF16)
        ab_ref[1, 0] = b.astype(BF16)
        t_ref[0] = (a * _sigmoid(a) * b).astype(BF16)

    return pl.pallas_call(
        body, name="mm_ffn_in_swiglu",
        out_shape=(jax.ShapeDtypeStruct((2, g, s, n), BF16), jax.ShapeDtypeStruct((g, s, n), BF16)),
        grid=(s // tm, g),
        in_specs=[pl.BlockSpec((tm, d), lambda i, j: (i, 0)), pl.BlockSpec((1, d, n), lambda i, j: (j, 0, 0)),
                  pl.BlockSpec((1, d, n), lambda i, j: (j + g, 0, 0))],
        out_specs=(pl.BlockSpec((2, 1, tm, n), lambda i, j: (0, j, i, 0)),
                   pl.BlockSpec((1, tm, n), lambda i, j: (j, i, 0))),
        compiler_params=_cparams("parallel", "parallel"),
    )(hn, wg, wg)


def _swiglu_bwd(ab, dt, ts):
    _, g, s, n = ab.shape
    ts = _tile(s, ts, 8)

    def body(ab_ref, dt_ref, t_ref, dab_ref):
        a = ab_ref[0, 0].astype(F32)
        b = ab_ref[1, 0].astype(F32)
        dt = dt_ref[0]
        sig = _sigmoid(a)
        silu = a * sig
        t_ref[0] = (silu * b).astype(BF16)
        dab_ref[0, 0] = (dt * b * (sig * (1.0 + a * (1.0 - sig)))).astype(BF16)
        dab_ref[1, 0] = (dt * silu).astype(BF16)

    both = pl.BlockSpec((2, 1, ts, n), lambda j, i: (0, j, i, 0))
    one = pl.BlockSpec((1, ts, n), lambda j, i: (j, i, 0))
    return pl.pallas_call(
        body, name="swiglu_bwd",
        out_shape=(jax.ShapeDtypeStruct((g, s, n), BF16), jax.ShapeDtypeStruct((2, g, s, n), BF16)),
        grid=(g, s // ts), in_specs=[both, one], out_specs=(one, both),
        compiler_params=_cparams("parallel", "parallel"),
    )(ab, dt)


def _forget_fwd(fl_t, b_col):
    h, s = fl_t.shape
    nb = s // LANE

    def body(fl_ref, b_ref, c_ref):
        upper = (lax.broadcasted_iota(jnp.int32, (LANE, LANE), 0)
                 <= lax.broadcasted_iota(jnp.int32, (LANE, LANE), 1)).astype(F32)

        def step(i, carry):
            x = fl_ref[i] + b_ref[...]
            lf = jnp.minimum(x, 0.0) - jnp.log(1.0 + jnp.exp(-jnp.abs(x)))
            cs = jnp.dot(lf, upper, precision=lax.Precision.HIGHEST, preferred_element_type=F32) + carry
            c_ref[i] = cs
            return cs[:, LANE - 1:LANE]

        lax.fori_loop(0, nb, step, jnp.zeros((h, 1), F32))

    out = pl.pallas_call(
        body, name="forget_fwd", out_shape=jax.ShapeDtypeStruct((nb, h, LANE), F32),
        compiler_params=_cparams(),
    )(fl_t.reshape(h, nb, LANE).transpose(1, 0, 2), b_col)
    return out.transpose(1, 0, 2).reshape(h, s)


def _forget_bwd(dc_t, fl_t, b_col):
    h, s = fl_t.shape
    nb = s // LANE

    def body(dc_ref, fl_ref, b_ref, dfl_ref, db_ref):
        lower = (lax.broadcasted_iota(jnp.int32, (LANE, LANE), 0)
                 >= lax.broadcasted_iota(jnp.int32, (LANE, LANE), 1)).astype(F32)

        def step(t, carry):
            tail, db = carry
            i = nb - 1 - t
            rc = jnp.dot(dc_ref[i], lower, precision=lax.Precision.HIGHEST, preferred_element_type=F32) + tail
            dfl = rc * (1.0 - _sigmoid(fl_ref[i] + b_ref[...]))
            dfl_ref[i] = dfl
            return rc[:, 0:1], db + jnp.sum(dfl, axis=1, keepdims=True)

        _, db = lax.fori_loop(0, nb, step, (jnp.zeros((h, 1), F32), jnp.zeros((h, 1), F32)))
        db_ref[...] = db

    blocked = lambda a: a.reshape(h, nb, LANE).transpose(1, 0, 2)
    dfl, db = pl.pallas_call(
        body, name="forget_bwd",
        out_shape=(jax.ShapeDtypeStruct((nb, h, LANE), F32), jax.ShapeDtypeStruct((h, 1), F32)),
        compiler_params=_cparams(),
    )(blocked(dc_t), blocked(fl_t), b_col)
    return dfl.transpose(1, 0, 2).reshape(h, s), db


N_PAIRS = N_HEADS // 2


def _causal_mask(t):
    return lax.broadcasted_iota(jnp.int32, (t, t), 0) >= lax.broadcasted_iota(jnp.int32, (t, t), 1)


def _head_lanes():
    return lax.broadcasted_iota(jnp.int32, (1, 2 * HEAD_DIM), 1) < HEAD_DIM


def _pick(x2, first, hh):
    zero = jnp.zeros_like(x2)
    return jnp.where(first, x2, zero) if hh == 0 else jnp.where(first, zero, x2)


def _attn_fwd(qkv, c_col, c_row, tq):
    s = qkv.shape[0]
    nq = s // tq
    w = 2 * HEAD_DIM

    def body(q_ref, k_ref, v_ref, cc_ref, cr_ref, o_ref, lse_ref):
        i = pl.program_id(1)
        first = _head_lanes()
        q2 = q_ref[...] * ATT_SCALE
        qs = [_pick(q2, first, hh) for hh in range(2)]

        def block(j, carry, masked):
            off = pl.multiple_of(j * tq, tq)
            k2 = k_ref[pl.ds(off, tq), :]
            v2 = v_ref[pl.ds(off, tq), :]
            new = []
            for hh in range(2):
                m, l, acc = carry[hh]
                sc = lax.dot_general(qs[hh], k2, NT, preferred_element_type=F32) + cc_ref[hh] - cr_ref[hh, j]
                if masked:
                    sc = jnp.where(_causal_mask(tq), sc, NEG_INF)
                m_new = jnp.maximum(m, jnp.max(sc, axis=-1, keepdims=True))
                alpha = jnp.exp(m - m_new)
                p = jnp.exp(sc - m_new)
                l = alpha * l + jnp.sum(p, axis=-1, keepdims=True)
                p_hi = p.astype(BF16)
                p_lo = (p - p_hi.astype(F32)).astype(BF16)
                acc = (alpha * acc + jnp.dot(p_hi, v2, preferred_element_type=F32)
                       + jnp.dot(p_lo, v2, preferred_element_type=F32))
                new.append((m_new, l, acc))
            return tuple(new)

        one = (jnp.full((tq, 1), NEG_INF, F32), jnp.zeros((tq, 1), F32), jnp.zeros((tq, w), F32))
        carry = lax.fori_loop(0, i, lambda j, c: block(j, c, False), (one, one))
        (m0, l0, a0), (m1, l1, a1) = block(i, carry, True)
        o_ref[...] = jnp.where(first, a0 / l0, a1 / l1)
        lse_ref[0] = m0 + jnp.log(l0)
        lse_ref[1] = m1 + jnp.log(l1)

    return pl.pallas_call(
        body, name="attn_fwd",
        out_shape=(jax.ShapeDtypeStruct((s, D_ATT), F32), jax.ShapeDtypeStruct((N_HEADS, s, 1), F32)),
        grid=(N_PAIRS, nq),
        in_specs=[pl.BlockSpec((tq, w), lambda hp, i: (i, hp)),
                  pl.BlockSpec((s, w), lambda hp, i: (0, N_PAIRS + hp)),
                  pl.BlockSpec((s, w), lambda hp, i: (0, 2 * N_PAIRS + hp)),
                  pl.BlockSpec((2, tq, 1), lambda hp, i: (hp, i, 0)),
                  pl.BlockSpec((2, nq, 1, tq), lambda hp, i: (hp, 0, 0, 0))],
        out_specs=(pl.BlockSpec((tq, w), lambda hp, i: (i, hp)),
                   pl.BlockSpec((2, tq, 1), lambda hp, i: (hp, i, 0))),
        compiler_params=_cparams("parallel", "parallel"),
    )(qkv, qkv, qkv, c_col, c_row)


def _attn_delta(o, do, tq):
    s = o.shape[0]
    w = 2 * HEAD_DIM

    def body(o_ref, do_ref, d_ref):
        first = _head_lanes()
        prod = o_ref[...] * do_ref[...].astype(F32)
        d_ref[0] = jnp.sum(_pick(prod, first, 0), axis=-1, keepdims=True)
        d_ref[1] = jnp.sum(_pick(prod, first, 1), axis=-1, keepdims=True)

    blk = pl.BlockSpec((tq, w), lambda hp, i: (i, hp))
    return pl.pallas_call(
        body, name="attn_delta", out_shape=jax.ShapeDtypeStruct((N_HEADS, s, 1), F32), grid=(N_PAIRS, s // tq),
        in_specs=[blk, blk], out_specs=pl.BlockSpec((2, tq, 1), lambda hp, i: (hp, i, 0)),
        compiler_params=_cparams("parallel", "parallel"),
    )(o, do)


def _attn_bwd(qkv, do, lse, delta, c_col, c_row, tq):
    s = qkv.shape[0]
    nq = s // tq
    w = 2 * HEAD_DIM

    def body(q_ref, do_ref, lse_ref, dl_ref, cc_ref, k_ref, v_ref, cr_ref, dq_ref, dk_ref, dv_ref, dc_ref):
        j = pl.program_id(1)
        first = _head_lanes()

        @pl.when(j == 0)
        def _():
            dq_ref[...] = jnp.zeros_like(dq_ref)

        k2 = k_ref[...]
        v2 = v_ref[...]

        def step(i, carry, masked):
            off = pl.multiple_of(i * tq, tq)
            rows = pl.ds(off, tq)
            q2 = q_ref[rows, :] * ATT_SCALE
            do2 = do_ref[rows, :]
            new, dqs = [], []
            for hh in range(2):
                dk, dv, dcs = carry[hh]
                sc = (lax.dot_general(_pick(q2, first, hh), k2, NT, preferred_element_type=F32)
                      + cc_ref[hh, rows, :] - cr_ref[hh, 0])
                if masked:
                    sc = jnp.where(_causal_mask(tq), sc, NEG_INF)
                p = jnp.exp(sc - lse_ref[hh, rows, :])
                dv = dv + lax.dot_general(p.astype(BF16), do2, TN, preferred_element_type=F32)
                dp = lax.dot_general(_pick(do2, first, hh), v2, NT, preferred_element_type=F32)
                ds = p * (dp - dl_ref[hh, rows, :])
                dsb = ds.astype(BF16)
                dk = dk + lax.dot_general(dsb, q2, TN, preferred_element_type=F32)
                dqs.append(jnp.dot(dsb, k2, preferred_element_type=F32))
                new.append((dk, dv, dcs + jnp.sum(ds, axis=0, keepdims=True)))
            dq_ref[rows, :] += jnp.where(first, dqs[0], dqs[1]) * ATT_SCALE
            return tuple(new)

        one = (jnp.zeros((tq, w), F32), jnp.zeros((tq, w), F32), jnp.zeros((1, tq), F32))
        carry = step(j, (one, one), True)
        (dk0, dv0, dc0), (dk1, dv1, dc1) = lax.fori_loop(j + 1, nq, lambda i, c: step(i, c, False), carry)
        dk_ref[...] = jnp.where(first, dk0, dk1)
        dv_ref[...] = jnp.where(first, dv0, dv1)
        dc_ref[0, 0] = -dc0
        dc_ref[1, 0] = -dc1

    whole_cols = lambda off: pl.BlockSpec((s, w), lambda hp, j: (0, off + hp))
    whole_heads = pl.BlockSpec((2, s, 1), lambda hp, j: (hp, 0, 0))
    blk = lambda off: pl.BlockSpec((tq, w), lambda hp, j: (j, off + hp))
    crow = pl.BlockSpec((2, 1, 1, tq), lambda hp, j: (hp, j, 0, 0))
    return pl.pallas_call(
        body, name="attn_bwd",
        out_shape=(jax.ShapeDtypeStruct((s, D_ATT), F32), jax.ShapeDtypeStruct((s, D_ATT), F32),
                   jax.ShapeDtypeStruct((s, D_ATT), F32), jax.ShapeDtypeStruct((N_HEADS, nq, 1, tq), F32)),
        grid=(N_PAIRS, nq),
        in_specs=[whole_cols(0), whole_cols(0), whole_heads, whole_heads, whole_heads,
                  blk(N_PAIRS), blk(2 * N_PAIRS), crow],
        out_specs=(whole_cols(0), blk(0), blk(0), crow),
        compiler_params=_cparams("parallel", "arbitrary"),
    )(qkv, do, lse, delta, c_col, qkv, qkv, c_row)


def _gm_group_norm(vg):
    mu = jnp.mean(vg, axis=-1, keepdims=True)
    d = vg - mu
    rstd = lax.rsqrt(jnp.mean(d * d, axis=-1, keepdims=True) + EPS)
    return d * rstd, rstd


def _gmlp_fwd(z, wt, bs_t, vgain):
    s = z.shape[0]

    def body(gu_ref, gv_ref, wt_ref, bs_ref, vg_ref, o_ref):
        for g in range(N_GROUPS):
            sl = slice(g * GROUP_DIM, (g + 1) * GROUP_DIM)
            vhat, _ = _gm_group_norm(_gelu(gv_ref[:, sl]))
            vn = vhat * vg_ref[:, sl]
            mixed = jnp.dot(wt_ref[g], vn.astype(BF16), preferred_element_type=F32) + bs_ref[:, g:g + 1]
            o_ref[:, sl] = _gelu(gu_ref[:, sl]) * mixed

    full = lambda a: pl.BlockSpec(a.shape, lambda n: (0,) * a.ndim)
    return pl.pallas_call(
        body, name="gmlp_fwd", out_shape=jax.ShapeDtypeStruct((s, D_GM), F32), grid=(s // CHUNK,),
        in_specs=[pl.BlockSpec((CHUNK, D_GM), lambda n: (n, 3)), pl.BlockSpec((CHUNK, D_GM), lambda n: (n, 4)),
                  full(wt), full(bs_t), full(vgain)],
        out_specs=pl.BlockSpec((CHUNK, D_GM), lambda n: (n, 0)),
        compiler_params=_cparams("parallel"),
    )(z, z, wt, bs_t, vgain)


def _gmlp_bwd(z, dgm, wt, wt_t, bs_t, vgain):
    s = z.shape[0]

    def body(gu_ref, gv_ref, dgm_ref, wt_ref, wtt_ref, bs_ref, vg_ref, dgu_ref, dgv_ref, dwt_ref, dbs_ref, dvg_ref):
        @pl.when(pl.program_id(0) == 0)
        def _():
            dwt_ref[...] = jnp.zeros_like(dwt_ref)
            dbs_ref[...] = jnp.zeros_like(dbs_ref)
            dvg_ref[...] = jnp.zeros_like(dvg_ref)

        for g in range(N_GROUPS):
            sl = slice(g * GROUP_DIM, (g + 1) * GROUP_DIM)
            gu = gu_ref[:, sl]
            gv = gv_ref[:, sl]
            dgm = dgm_ref[:, sl]
            vhat, rstd = _gm_group_norm(_gelu(gv))
            gain = vg_ref[:, sl]
            vn = (vhat * gain).astype(BF16)
            mixed = jnp.dot(wt_ref[g], vn, preferred_element_type=F32) + bs_ref[:, g:g + 1]
            dgu_ref[:, sl] = (dgm * mixed * _gelu_grad(gu)).astype(BF16)
            dmixed = dgm * _gelu(gu)
            dmb = dmixed.astype(BF16)
            dbs_ref[:, g:g + 1] += jnp.sum(dmixed, axis=-1, keepdims=True)
            dwt_ref[g] += lax.dot_general(dmb, vn, NT, preferred_element_type=F32)
            dvn = jnp.dot(wtt_ref[g], dmb, preferred_element_type=F32)
            dvg_ref[:, sl] += _colsum(dvn * vhat)
            dvhat = dvn * gain
            dvf = rstd * (dvhat - jnp.mean(dvhat, axis=-1, keepdims=True)
                          - vhat * jnp.mean(dvhat * vhat, axis=-1, keepdims=True))
            dgv_ref[:, sl] = (dvf * _gelu_grad(gv)).astype(BF16)

    full = lambda a: pl.BlockSpec(a.shape, lambda n: (0,) * a.ndim)
    chunk = pl.BlockSpec((CHUNK, D_GM), lambda n: (n, 0))
    return pl.pallas_call(
        body, name="gmlp_bwd",
        out_shape=(jax.ShapeDtypeStruct((s, D_GM), BF16), jax.ShapeDtypeStruct((s, D_GM), BF16),
                   jax.ShapeDtypeStruct(wt.shape, F32), jax.ShapeDtypeStruct(bs_t.shape, F32),
                   jax.ShapeDtypeStruct(vgain.shape, F32)),
        grid=(s // CHUNK,),
        in_specs=[pl.BlockSpec((CHUNK, D_GM), lambda n: (n, 3)), pl.BlockSpec((CHUNK, D_GM), lambda n: (n, 4)),
                  chunk, full(wt), full(wt_t), full(bs_t), full(vgain)],
        out_specs=(chunk, chunk, full(wt), full(bs_t), full(vgain)),
        compiler_params=_cparams("arbitrary"),
    )(z, z, dgm, wt, wt_t, bs_t, vgain)


def _layer_fwd(h0, p_i, w, tq, late):
    s, d = h0.shape
    nq = s // tq
    sv = {"h0": h0}

    (hn1,) = _rowwise(lambda h, g: _rms(h)[0] * g, [h0], [w["mix_pre_norm"]], [(d, BF16)], [], "pre_mix", 256)
    z = _mm(hn1, w["w_in"], "nn", F32, "mm_in")
    fl_t = z[:, F_OFF:F_OFF + N_HEADS].T
    c_t = _forget_fwd(fl_t, w["b_forget"])
    c_col = c_t[:, :, None]
    c_row = c_t.reshape(N_HEADS, nq, 1, tq)
    qkv = z[:, :3 * D_ATT].astype(BF16)
    att, lse = _attn_fwd(qkv, c_col, c_row, tq)
    gm = _gmlp_fwd(z, w["wt"], w["bs_t"], w["gm_v_norm"])
    w = dict(w, **late(att))

    def mix_out(att, gm, g):
        return jnp.concatenate([_rms(att)[0] * g[:, :D_ATT], _rms(gm)[0] * g[:, D_ATT:]], axis=-1)

    (mc,) = _rowwise(mix_out, [att, gm], [w["mix_out_norm"]], [(D_ATT + D_GM, BF16)], [], "mix_out", 256)
    y1 = _mm(mc, w["w_out"], "nn", F32, "mm_out")

    def post_mix(h0, y1, gpost, gpre):
        h1 = h0 + _rms(y1)[0] * gpost
        return h1, _rms(h1)[0] * gpre

    h1, hn2 = _rowwise(post_mix, [h0, y1], [w["mix_post_norm"], w["ffn_pre_norm"]],
                       [(d, F32), (d, BF16)], [], "post_mix", 256)
    ab, t = _ffn_in_swiglu(hn2, w["w_ffn_in"])
    y2 = _mm(t, w["w_ffn_out"], "nn", F32, "mm_ffn_out", a3="k", b3="k")

    def post_ffn(h1, y2, g):
        h2 = h1 + _rms(y2)[0] * g
        return h2, _rms(h2)[0]

    h2, hr = _rowwise(post_ffn, [h1, y2], [w["ffn_post_norm"]], [(d, F32), (d, BF16)], [], "post_ffn", 256)
    gl = _mm(hr, w["w_ple_gate"], "nn", F32, "mm_gate")
    pe = _mm(p_i, w["w_ple"], "nn", F32, "mm_ple")
    (h3,) = _rowwise(lambda h2, gl, pe, g: h2 + _sigmoid(gl) * (_rms(pe)[0] * g), [h2, gl, pe], [w["ple_norm"]],
                     [(d, F32)], [], "ple_out", 256)
    sv.update(hn1=hn1, z=z, fl_t=fl_t, c_col=c_col, c_row=c_row, qkv=qkv, lse=lse, att=att, gm=gm,
              mc=mc, y1=y1, h1=h1, hn2=hn2, ab=ab, y2=y2, h2=h2, hr=hr, gl=gl, pe=pe, p_i=p_i)
    return h3, sv


def _layer_bwd(dh3, sv, w, tq, mid):
    s, d = dh3.shape
    g = {}
    by_rows = lambda a: a.reshape(N_DEV, -1, a.shape[-1])
    by_cols = lambda a: jnp.stack(jnp.split(a, N_DEV, axis=-1))

    def ple_bwd(dh3, gl, pe, gple):
        gate = _sigmoid(gl)
        pehat, rpe = _rms(pe)
        dgl = dh3 * (pehat * gple) * gate * (1.0 - gate)
        de = dh3 * gate
        return dgl, _rms_bwd(pehat, rpe, de * gple), _colsum(de * pehat)

    dgl, dpe, g["ple_norm"] = _rowwise(ple_bwd, [dh3, sv["gl"], sv["pe"]], [w["ple_norm"]],
                                       [(d, BF16), (d, BF16)], [d], "ple_bwd", 256)
    g["w_ple_gate"] = by_rows(_mm(sv["hr"], dgl, "tn", BF16, "mm_dgate"))
    dhr = _mm(dgl, w["w_ple_gate"], "nt", F32, "mm_dhr")
    g["w_ple"] = by_cols(_mm(sv["p_i"], dpe, "tn", BF16, "mm_dple"))

    def ffn_post_bwd(dh3, dhr, h2, y2, gpost):
        h2hat, r2 = _rms(h2)
        dh2 = dh3 + _rms_bwd(h2hat, r2, dhr)
        y2hat, ry = _rms(y2)
        return dh2, _rms_bwd(y2hat, ry, dh2 * gpost), _colsum(dh2 * y2hat)

    dh2, dy2, g["ffn_post_norm"] = _rowwise(ffn_post_bwd, [dh3, dhr, sv["h2"], sv["y2"]], [w["ffn_post_norm"]],
                                            [(d, F32), (d, BF16)], [d], "ffn_post_bwd", 256)
    dt = _mm(dy2, w["w_ffn_out"], "nt", F32, "mm_dt", b3="n", o3="n")
    t, dab = _swiglu_bwd(sv["ab"], dt, 256)
    dab = dab.reshape((N_DEV,) + dab.shape[2:])
    g["w_ffn_out"] = by_rows(_mm(t, dy2, "tn", BF16, "mm_dffn_out", a3="m", o3="m"))
    dhn2 = _mm(dab, w["w_ffn_in"], "nt", F32, "mm_dhn2", a3="k", b3="k")
    g["w_ffn_in"] = _mm(sv["hn2"], dab, "tn", BF16, "mm_dffn_in", b3="n", o3="n")

    def mix_post_bwd(dh2, dhn2, h1, y1, gpre, gpost):
        h1hat, r1 = _rms(h1)
        dh1 = dh2 + _rms_bwd(h1hat, r1, dhn2 * gpre)
        y1hat, ry = _rms(y1)
        return dh1, _rms_bwd(y1hat, ry, dh1 * gpost), _colsum(dhn2 * h1hat), _colsum(dh1 * y1hat)

    dh1, dy1, g["ffn_pre_norm"], g["mix_post_norm"] = _rowwise(
        mix_post_bwd, [dh2, dhn2, sv["h1"], sv["y1"]], [w["ffn_pre_norm"], w["mix_post_norm"]],
        [(d, F32), (d, BF16)], [d, d], "mix_post_bwd", 256)
    dmc = _mm(dy1, w["w_out"], "nt", F32, "mm_dmc")
    g["w_out"] = by_rows(_mm(sv["mc"], dy1, "tn", BF16, "mm_dout"))
    w = dict(w, **mid(g, dmc))

    def mix_out_bwd(da, dg, att, gm, gain):
        atthat, ra = _rms(att)
        gmhat, rg = _rms(gm)
        dgain = jnp.concatenate([_colsum(da * atthat), _colsum(dg * gmhat)], axis=-1)
        return _rms_bwd(atthat, ra, da * gain[:, :D_ATT]), _rms_bwd(gmhat, rg, dg * gain[:, D_ATT:]), dgain

    datt, dgm, g["mix_out_norm"] = _rowwise(
        mix_out_bwd, [(dmc, 0, D_ATT), (dmc, 1, D_GM), sv["att"], sv["gm"]], [w["mix_out_norm"]],
        [(D_ATT, BF16), (D_GM, F32)], [D_ATT + D_GM], "mix_out_bwd", 256)

    dgu, dgv, dwt, dbs_t, g["gm_v_norm"] = _gmlp_bwd(sv["z"], dgm, w["wt"], w["wt_t"], w["bs_t"], w["gm_v_norm"])
    g["gm_w_s"] = dwt * jnp.tril(jnp.ones((CHUNK, CHUNK), F32))[None]
    g["gm_b_s"] = dbs_t.T

    delta = _attn_delta(sv["att"], datt, tq)
    dq, dk, dv, dc_row = _attn_bwd(sv["qkv"], datt, sv["lse"], delta, sv["c_col"], sv["c_row"], tq)
    dfl_t, db = _forget_bwd(dc_row.reshape(N_HEADS, s), sv["fl_t"], w["b_forget"])
    g["b_forget"] = db.reshape(1, N_HEADS)
    dz = jnp.concatenate([dq.astype(BF16), dk.astype(BF16), dv.astype(BF16), dgu, dgv, dfl_t.T.astype(BF16),
                          jnp.zeros((s, D_IN_PAD - F_OFF - N_HEADS), BF16)], axis=-1)
    dhn1 = _mm(dz, w["w_in"], "nt", F32, "mm_dhn1")
    din = _mm(sv["hn1"], dz, "tn", BF16, "mm_din")
    din = jnp.concatenate([din[:, :3 * D_ATT], din[:, F_OFF:F_OFF + N_HEADS], din[:, 3 * D_ATT:F_OFF]], axis=-1)
    g["w_in"] = by_cols(din)

    def mix_pre_bwd(dh1, dhn1, h0, gpre):
        h0hat, r0 = _rms(h0)
        return dh1 + _rms_bwd(h0hat, r0, dhn1 * gpre), _colsum(dhn1 * h0hat)

    dh0, g["mix_pre_norm"] = _rowwise(mix_pre_bwd, [dh1, dhn1, sv["h0"]], [w["mix_pre_norm"]],
                                      [(d, F32)], [d], "mix_pre_bwd", 256)
    return dh0, g


ANY = pl.BlockSpec(memory_space=pl.ANY)


def _all_gather(xs, layer, name):
    n = len(xs)

    def body(*refs):
        x_refs, out_refs = refs[:n], refs[n:2 * n]
        send_sems, recv_sems, local_sems = refs[2 * n:]
        x, y, c = lax.axis_index("x"), lax.axis_index("y"), lax.axis_index("c")
        me, sibling = (x, y, c), (x, y, 1 - c)
        chips = [(1 - x, y), (x, 1 - y), (1 - x, 1 - y)]

        def shard(a):
            return x_refs[a] if layer is None else x_refs[a].at[layer]

        def rows(a, px, py, pc):
            return out_refs[a].at[4 * px + 2 * py + pc]

        def copy(a, kk, block, to, from_shard=False):
            return pltpu.make_async_remote_copy(
                src_ref=shard(a) if from_shard else rows(a, *block), dst_ref=rows(a, *block),
                send_sem=send_sems.at[7 * a + kk], recv_sem=recv_sems.at[7 * a + kk],
                device_id=to, device_id_type=MESH)

        mine = [pltpu.make_async_copy(shard(a), rows(a, *me), local_sems.at[a]) for a in range(n)]
        for cp in mine:
            cp.start()
        first = []
        for a in range(n):
            first.append(copy(a, 0, me, sibling, from_shard=True))
            first += [copy(a, 1 + j, me, (*chip, c), from_shard=True) for j, chip in enumerate(chips)]
        for cp in first:
            cp.start()
        passed = []
        for j, chip in enumerate(chips):
            for a in range(n):
                copy(a, 1 + j, (*chip, c), me).wait_recv()
                passed.append(copy(a, 4 + j, (*chip, c), sibling))
                passed[-1].start()
        for a in range(n):
            copy(a, 0, sibling, me).wait_recv()
        for j, chip in enumerate(chips):
            for a in range(n):
                copy(a, 4 + j, (*chip, 1 - c), me).wait_recv()
        for cp in first + passed:
            cp.wait_send()
        for cp in mine:
            cp.wait()

    shapes = [x.shape if layer is None else x.shape[1:] for x in xs]
    return pl.pallas_call(
        body, name=name, out_shape=[jax.ShapeDtypeStruct((N_DEV,) + sh, x.dtype) for sh, x in zip(shapes, xs)],
        in_specs=[ANY] * n, out_specs=[ANY] * n,
        scratch_shapes=[pltpu.SemaphoreType.DMA((7 * n,)), pltpu.SemaphoreType.DMA((7 * n,)),
                        pltpu.SemaphoreType.DMA((n,))],
    )(*xs)


HBM = pl.BlockSpec(memory_space=pltpu.HBM)
SEMS = pl.BlockSpec(memory_space=pltpu.SEMAPHORE)
EFFECT = pltpu.SideEffectType.DATAFLOW_SIDE_EFFECTING
FLIPS = tuple((fx, fy, fc) for fx in (0, 1) for fy in (0, 1) for fc in (0, 1))[1:]


def _exchange_copies(src_refs, land_refs, send_sems, recv_sems, layer, scatter):
    x, y, c = lax.axis_index("x"), lax.axis_index("y"), lax.axis_index("c")
    me = 4 * x + 2 * y + c
    copies = []
    for a, (src, land) in enumerate(zip(src_refs, land_refs)):
        for f, (fx, fy, fc) in enumerate(FLIPS):
            px, py, pc = (1 - x if fx else x), (1 - y if fy else y), (1 - c if fc else c)
            if scatter:
                block = src.at[4 * px + 2 * py + pc]
            else:
                block = src if layer is None else src.at[layer]
            copies.append(pltpu.make_async_remote_copy(
                src_ref=block, dst_ref=land.at[me], send_sem=send_sems.at[7 * a + f], recv_sem=recv_sems.at[7 * a + f],
                device_id=(px, py, pc), device_id_type=MESH))
    return copies


def _exchange_start(srcs, lands, layer, scatter, name):
    n = len(srcs)

    def body(*refs):
        for cp in _exchange_copies(refs[:n], refs[n:2 * n], refs[2 * n], refs[2 * n + 1], layer, scatter):
            cp.start()
        token = refs[-1]
        token[...] = jnp.zeros_like(token)

    operands = list(srcs) + list(lands)
    outs = pl.pallas_call(
        body, name=name,
        out_shape=(pltpu.SemaphoreType.DMA((7 * n,)), pltpu.SemaphoreType.DMA((7 * n,)),
                   *[pltpu.HBM(a.shape, a.dtype) for a in operands], jax.ShapeDtypeStruct((8, LANE), F32)),
        in_specs=[HBM] * (2 * n),
        out_specs=(SEMS, SEMS, *[HBM] * (2 * n), pl.BlockSpec(memory_space=pltpu.VMEM)),
        input_output_aliases={i: 2 + i for i in range(2 * n)},
        compiler_params=pltpu.CompilerParams(has_side_effects=EFFECT),
    )(*[pltpu.with_memory_space_constraint(a, pltpu.HBM) for a in operands])
    return outs[0], outs[1], outs[2:2 + n], outs[2 + n:2 + 2 * n], outs[-1]


def _exchange_wait(started, after, layer, scatter, name):
    send_sems, recv_sems, srcs, lands, _ = started
    n = len(srcs)

    def body(*refs):
        for cp in _exchange_copies(refs[:n], refs[n:2 * n], refs[2 * n], refs[2 * n + 1], layer, scatter):
            cp.wait_send()
            cp.wait_recv()

    operands = list(srcs) + list(lands)
    outs = pl.pallas_call(
        body, name=name, out_shape=tuple(pltpu.HBM(a.shape, a.dtype) for a in operands),
        in_specs=[HBM] * (2 * n) + [SEMS, SEMS, ANY], out_specs=[HBM] * (2 * n),
        input_output_aliases={i: i for i in range(2 * n)},
        compiler_params=pltpu.CompilerParams(has_side_effects=EFFECT),
    )(*operands, send_sems, recv_sems, after)
    return outs[:n], outs[n:]


def _sum_devices(parts):
    _, r, c = parts.shape

    def body(p_ref, o_ref):
        acc = p_ref[0]
        for j in range(1, N_DEV):
            acc = acc + p_ref[j]
        o_ref[...] = acc

    return pl.pallas_call(
        body, name="small_sum", out_shape=jax.ShapeDtypeStruct((r, c), F32), grid=(r // SMALL_ROWS,),
        in_specs=[pl.BlockSpec((N_DEV, SMALL_ROWS, c), lambda i: (0, i, 0))],
        out_specs=pl.BlockSpec((SMALL_ROWS, c), lambda i: (i, 0)),
        compiler_params=_cparams("parallel"),
    )(parts)


def _adamw_math(w, g, m, v):
    m = ADAM_B1 * m + (1.0 - ADAM_B1) * g
    v = ADAM_B2 * v + (1.0 - ADAM_B2) * (g * g)
    m_hat = m / (1.0 - ADAM_B1 ** ADAM_STEP)
    v_hat = v / (1.0 - ADAM_B2 ** ADAM_STEP)
    return -ADAM_LR * (m_hat / (jnp.sqrt(v_hat) + ADAM_EPS) + ADAM_WD * w), m, v


def _adamw_shard(w, m, v, parts, layer, name):
    _, a, b = w.shape
    ta = _tile(a, 256, 16)

    def body(w_ref, m_ref, v_ref, p_ref, g_ref, d_ref, nm_ref, nv_ref):
        g = p_ref[0].astype(F32)
        for j in range(1, N_DEV):
            g = g + p_ref[j].astype(F32)
        g_ref[...] = g
        d_ref[...], nm_ref[...], nv_ref[...] = _adamw_math(w_ref[0], g, m_ref[0], v_ref[0])

    mine = pl.BlockSpec((1, ta, b), lambda i: (layer, i, 0))
    out = pl.BlockSpec((ta, b), lambda i: (i, 0))
    return pl.pallas_call(
        body, name=name, out_shape=[jax.ShapeDtypeStruct((a, b), F32)] * 4, grid=(a // ta,),
        in_specs=[mine, mine, mine, pl.BlockSpec((N_DEV, ta, b), lambda i: (0, i, 0))], out_specs=[out] * 4,
        compiler_params=_cparams("parallel"),
    )(w, m, v, parts)


def _pack_small(pieces):
    flat = jnp.concatenate([p.reshape(-1) for p in pieces])
    total = -(-flat.shape[0] // (SMALL_COLS * SMALL_ROWS)) * SMALL_COLS * SMALL_ROWS
    return jnp.pad(flat, (0, total - flat.shape[0])).reshape(-1, SMALL_COLS)


def kernel(x, p, mix_pre_norm, mix_post_norm, w_in, b_forget, gm_v_norm, gm_w_s, gm_b_s, mix_out_norm, w_out, ffn_pre_norm, ffn_post_norm, w_ffn_in, w_ffn_out, w_ple, ple_norm, w_ple_gate, loss_target, m_mix_pre_norm, m_mix_post_norm, m_w_in, m_b_forget, m_gm_v_norm, m_gm_w_s, m_gm_b_s, m_mix_out_norm, m_w_out, m_ffn_pre_norm, m_ffn_post_norm, m_w_ffn_in, m_w_ffn_out, m_w_ple, m_ple_norm, m_w_ple_gate, v_mix_pre_norm, v_mix_post_norm, v_w_in, v_b_forget, v_gm_v_norm, v_gm_w_s, v_gm_b_s, v_mix_out_norm, v_w_out, v_ffn_pre_norm, v_ffn_post_norm, v_w_ffn_in, v_w_ffn_out, v_w_ple, v_ple_norm, v_w_ple_gate):
    given = dict(locals())
    weights = {n: given[n] for n in WEIGHT_ORDER}
    mom_m = {n: given["m_" + n] for n in WEIGHT_ORDER}
    mom_v = {n: given["v_" + n] for n in WEIGHT_ORDER}
    depth = w_in.shape[0]
    s, d = x.shape[1], x.shape[2]
    tq = _tile(s, ATT_BLOCK)
    me = 4 * lax.axis_index("x") + 2 * lax.axis_index("y") + lax.axis_index("c")
    tril = jnp.tril(jnp.ones((CHUNK, CHUNK), F32))

    def landing(block):
        return lax.dynamic_update_index_in_dim(lax.empty((N_DEV,) + block.shape, block.dtype), block, me, 0)

    def mix_weights(i, got):
        w_in_full = jnp.concatenate([got["w_in"][j] for j in range(N_DEV)], axis=-1)
        pad = jnp.zeros((d, D_IN_PAD - D_IN), BF16)
        wt = gm_w_s[i] * tril[None]
        lw = dict(
            w_in=jnp.concatenate([w_in_full[:, :3 * D_ATT], w_in_full[:, 3 * D_ATT + N_HEADS:],
                                  w_in_full[:, 3 * D_ATT:3 * D_ATT + N_HEADS], pad], axis=-1),
            b_forget=b_forget[i][:, None], wt=wt.astype(BF16), wt_t=wt.transpose(0, 2, 1).astype(BF16),
            bs_t=gm_b_s[i].T)
        lw.update({n: weights[n][i][None] for n in ("mix_pre_norm", "mix_post_norm", "gm_v_norm", "mix_out_norm",
                                                    "ffn_pre_norm", "ffn_post_norm", "ple_norm")})
        return lw

    def rest_weights(got):
        return dict(w_out=got["w_out"].reshape(-1, d), w_ffn_in=got["w_ffn_in"],
                    w_ffn_out=got["w_ffn_out"].reshape(N_DEV // 2, -1, d),
                    w_ple=jnp.concatenate([got["w_ple"][j] for j in range(N_DEV)], axis=-1),
                    w_ple_gate=got["w_ple_gate"].reshape(-1, d))

    shards = {n: weights[n].astype(BF16) for n in MATRIX_WEIGHTS}

    def gather_start(i):
        return {tag: _exchange_start([shards[n] for n in grp], [landing(shards[n][i]) for n in grp], i, False,
                                     f"weights_gather_start_{i}_{tag}") for tag, grp in EXCHANGE_GROUPS.items()}

    def gather_finish(i, tag, pending, after):
        srcs, got = _exchange_wait(pending[tag], after, i, False, f"weights_gather_wait_{i}_{tag}")
        shards.update(zip(EXCHANGE_GROUPS[tag], srcs))
        return dict(zip(EXCHANGE_GROUPS[tag], got))

    h = x[0]
    saved, layer_w = [], []
    pending = gather_start(0)
    for i in range(depth):
        lw = mix_weights(i, gather_finish(i, "mix", pending, h))
        following = {}

        def late(att, i=i, pending=pending, lw=lw, following=following):
            rest = rest_weights(gather_finish(i, "rest", pending, att))
            lw.update(rest)
            if i + 1 == depth:
                return rest
            following.update(gather_start(i + 1))
            token = following["mix"][4][:1, :1] + following["rest"][4][:1, :1]
            return dict(rest, mix_out_norm=lw["mix_out_norm"] + token)

        h, sv = _layer_fwd(h, p[i, 0], lw, tq, late)
        layer_w.append(lw)
        saved.append(sv)
        pending = following

    def loss_head(y, t):
        err = y - t
        return err * (1.0 / d), _colsum(err * err)

    dh, sq = _rowwise(loss_head, [h, loss_target[0]], [], [(d, F32)], [d], "loss_head", 256)
    loss = lax.psum(0.5 * jnp.sum(sq) / d, AXES)

    layer_g = [None] * depth
    shard_out = {n: [None] * depth for n in MATRIX_WEIGHTS}

    def scatter_start(i, tag, g):
        full_g = [g[n] for n in EXCHANGE_GROUPS[tag]]
        lands = [landing(lax.dynamic_index_in_dim(gf, me, 0, keepdims=False)) for gf in full_g]
        return _exchange_start(full_g, lands, None, True, f"grads_scatter_start_{i}_{tag}")

    def scatter_finish(i, tag, started, after):
        _, parts = _exchange_wait(started[tag], after, None, True, f"grads_scatter_wait_{i}_{tag}")
        for n, part in zip(EXCHANGE_GROUPS[tag], parts):
            shard_out[n][i] = _adamw_shard(weights[n], mom_m[n], mom_v[n], part, i, "adamw_" + n)

    before = None
    for i in reversed(range(depth)):
        lw = layer_w[i]
        if before is not None:
            lw = dict(lw, ple_norm=lw["ple_norm"] + before[1]["mix"][4][:1, :1])
        started = {}

        def mid(g, dmc, i=i, before=before, started=started, lw=lw):
            if before is not None:
                scatter_finish(before[0], "rest", before[1], dmc)
            started["rest"] = scatter_start(i, "rest", g)
            return dict(mix_out_norm=lw["mix_out_norm"] + started["rest"][4][:1, :1])

        dh, layer_g[i] = _layer_bwd(dh, saved[i], lw, tq, mid)
        if before is not None:
            scatter_finish(before[0], "mix", before[1], dh)
        started["mix"] = scatter_start(i, "mix", layer_g[i])
        before = (i, started)
    scatter_finish(before[0], "rest", before[1], dh)
    scatter_finish(before[0], "mix", before[1], dh)
    grad_x = dh[None]

    grads, deltas, new_m, new_v = {}, {}, {}, {}
    for n in MATRIX_WEIGHTS:
        grads[n], deltas[n], new_m[n], new_v[n] = (jnp.stack([shard_out[n][i][k] for i in range(depth)])
                                                   for k in range(4))

    small_g = _pack_small([jnp.stack([layer_g[i][n].reshape(-1) for i in range(depth)]) for n in SMALL_WEIGHTS])
    (gathered,) = _all_gather([small_g], None, "small_grads_all_gather")
    g_small = _sum_devices(gathered)
    pack = lambda t: _pack_small([t[n] for n in SMALL_WEIGHTS])
    dl, nm, nv = _rowwise(_adamw_math, [pack(weights), g_small, pack(mom_m), pack(mom_v)], [],
                          [(SMALL_COLS, F32)] * 3, [], "adamw_small", SMALL_ROWS)
    off = 0
    for n in SMALL_WEIGHTS:
        shp, size = weights[n].shape, weights[n].size
        grads[n], deltas[n], new_m[n], new_v[n] = (a.reshape(-1)[off:off + size].reshape(shp)
                                                   for a in (g_small, dl, nm, nv))
        off += size

    return (loss, grad_x, *[grads[n] for n in WEIGHT_ORDER], *[deltas[n] for n in WEIGHT_ORDER],
            *[new_m[n] for n in WEIGHT_ORDER], *[new_v[n] for n in WEIGHT_ORDER])
```

```python
import functools
import math

import jax
import jax.numpy as jnp
from jax import lax
from jax.experimental import pallas as pl
from jax.experimental.pallas import tpu as pltpu

F32 = jnp.float32
BF16 = jnp.bfloat16
MESH = pl.DeviceIdType.MESH
AXES = ("x", "y", "c")
N_DEV = 8

EPS = 1e-6
NEG_INF = -1e30
N_HEADS = 8
HEAD_DIM = 64
D_ATT = N_HEADS * HEAD_DIM
N_GROUPS = 8
GROUP_DIM = 64
D_GM = N_GROUPS * GROUP_DIM
CHUNK = 128
ATT_SCALE = HEAD_DIM ** -0.5
ATT_BLOCK = 512
D_IN = 3 * D_ATT + N_HEADS + 2 * D_GM
D_IN_PAD = 3 * D_ATT + 2 * D_GM + 128
F_OFF = 3 * D_ATT + 2 * D_GM

ADAM_LR = 0.001
ADAM_B1 = 0.9
ADAM_B2 = 0.999
ADAM_EPS = 1e-08
ADAM_WD = 0.01
ADAM_STEP = 10

LANE = 128
VMEM_LIMIT = 48 * 1024 * 1024
SMALL_COLS = 128
SMALL_ROWS = 512

MATRIX_WEIGHTS = ("w_in", "w_out", "w_ffn_in", "w_ffn_out", "w_ple", "w_ple_gate")
EXCHANGE_GROUPS = {"mix": ("w_in",), "rest": ("w_out", "w_ffn_in", "w_ffn_out", "w_ple", "w_ple_gate")}
SMALL_WEIGHTS = ("mix_pre_norm", "mix_post_norm", "b_forget", "gm_v_norm", "gm_w_s", "gm_b_s",
                 "mix_out_norm", "ffn_pre_norm", "ffn_post_norm", "ple_norm")
WEIGHT_ORDER = ("mix_pre_norm", "mix_post_norm", "w_in", "b_forget", "gm_v_norm", "gm_w_s", "gm_b_s",
                "mix_out_norm", "w_out", "ffn_pre_norm", "ffn_post_norm", "w_ffn_in", "w_ffn_out",
                "w_ple", "ple_norm", "w_ple_gate")


def _tile(n, pref, unit=LANE):
    best = None
    t = unit
    while t <= min(n, pref):
        if n % t == 0:
            best = t
        t += unit
    return n if best is None else best


def _cparams(*semantics):
    return pltpu.CompilerParams(dimension_semantics=semantics or None, vmem_limit_bytes=VMEM_LIMIT)


NN = (((1,), (0,)), ((), ()))
NT = (((1,), (1,)), ((), ()))
TN = (((0,), (0,)), ((), ()))
_MM_AXES = {
    "nn": ("i", "k", "k", "j"), "nt": ("i", "k", "j", "k"), "tn": ("k", "i", "k", "j")}
_MM_DN = {"nn": NN, "nt": NT, "tn": TN}


def _mm(a, b, dims, out_dtype, name, a3=None, b3=None, o3=None, tm=1024, tn=1024, tk=1024):
    ar, ac, br, bc = _MM_AXES[dims]
    letter = {"i": "m", "j": "n", "k": "k"}
    size = {}

    def measure(x, rows, cols, stacked):
        shape = x.shape
        if stacked is None:
            size.setdefault(letter[rows], shape[0])
            size.setdefault(letter[cols], shape[1])
        else:
            for ax, n in ((rows, shape[1]), (cols, shape[2])):
                size.setdefault(letter[ax], n * shape[0] if letter[ax] == stacked else n)

    measure(a, ar, ac, a3)
    measure(b, br, bc, b3)
    m, n, k = size["m"], size["n"], size["k"]
    slab = {}
    for x, stacked, rows, cols in ((a, a3, ar, ac), (b, b3, br, bc)):
        if stacked is not None:
            slab[stacked] = x.shape[1] if letter[rows] == stacked else x.shape[2]
    if o3 is not None:
        slab.setdefault(o3, slab.get(o3, None) or {"m": m, "n": n}[o3] // N_DEV)
    tile = {"m": slab.get("m") or _tile(m, tm), "n": slab.get("n") or _tile(n, tn), "k": slab.get("k") or _tile(k, tk)}
    nk = k // tile["k"]

    def spec(rows, cols, stacked):
        tr, tc = tile[letter[rows]], tile[letter[cols]]
        if stacked is None:
            return pl.BlockSpec((tr, tc), lambda i, j, kk: ({"i": i, "j": j, "k": kk}[rows], {"i": i, "j": j, "k": kk}[cols]))

        def imap(i, j, kk):
            g = {"i": i, "j": j, "k": kk}
            return (g[{"m": "i", "n": "j", "k": "k"}[stacked]],
                    0 if letter[rows] == stacked else g[rows], 0 if letter[cols] == stacked else g[cols])

        return pl.BlockSpec((1, tr, tc), imap)

    dn = _MM_DN[dims]

    def body(a_ref, b_ref, o_ref, *acc):
        av = a_ref[...] if a3 is None else a_ref[0]
        bv = b_ref[...] if b3 is None else b_ref[0]
        prod = lax.dot_general(av.astype(BF16), bv.astype(BF16), dn, preferred_element_type=F32)

        def emit(val):
            if o3 is None:
                o_ref[...] = val.astype(out_dtype)
            else:
                o_ref[0] = val.astype(out_dtype)

        if nk == 1:
            emit(prod)
            return
        (acc_ref,) = acc
        kk = pl.program_id(2)

        @pl.when(kk == 0)
        def _():
            acc_ref[...] = prod

        @pl.when(kk > 0)
        def _():
            acc_ref[...] += prod

        @pl.when(kk == nk - 1)
        def _():
            emit(acc_ref[...])

    if o3 is None:
        out_shape = (m, n)
    elif o3 == "m":
        out_shape = (m // tile["m"], tile["m"], n)
    else:
        out_shape = (n // tile["n"], m, tile["n"])
    return pl.pallas_call(
        body, name=name, out_shape=jax.ShapeDtypeStruct(out_shape, out_dtype),
        grid=(m // tile["m"], n // tile["n"], nk),
        in_specs=[spec(ar, ac, a3), spec(br, bc, b3)], out_specs=spec("i", "j", o3),
        scratch_shapes=[] if nk == 1 else [pltpu.VMEM((tile["m"], tile["n"]), F32)],
        compiler_params=_cparams("parallel", "parallel", "arbitrary"),
    )(a, b)


def _rowwise(fn, rows, vecs, outs, reds, name, ts):
    rows = [r if isinstance(r, tuple) else (r, 0, r.shape[1]) for r in rows]
    s = rows[0][0].shape[0]
    ts = _tile(s, ts, 8)
    nr, nv, no = len(rows), len(vecs), len(outs)

    def body(*refs):
        vals = fn(*[r[...] for r in refs[:nr + nv]])
        vals = vals if isinstance(vals, tuple) else (vals,)
        o_refs = refs[nr + nv:nr + nv + no]
        r_refs = refs[nr + nv + no:]
        for o_ref, val in zip(o_refs, vals[:no]):
            o_ref[...] = val.astype(o_ref.dtype)
        if r_refs:
            @pl.when(pl.program_id(0) == 0)
            def _():
                for r_ref in r_refs:
                    r_ref[...] = jnp.zeros_like(r_ref)

            for r_ref, val in zip(r_refs, vals[no:]):
                r_ref[...] += val

    in_specs = [pl.BlockSpec((ts, w), functools.partial(lambda i, cb: (i, cb), cb=cb)) for _, cb, w in rows]
    in_specs += [pl.BlockSpec(v.shape, lambda i: (0, 0)) for v in vecs]
    out_specs = [pl.BlockSpec((ts, c), lambda i: (i, 0)) for c, _ in outs]
    out_specs += [pl.BlockSpec((1, c), lambda i: (0, 0)) for c in reds]
    out_shape = [jax.ShapeDtypeStruct((s, c), dt) for c, dt in outs]
    out_shape += [jax.ShapeDtypeStruct((1, c), F32) for c in reds]
    return pl.pallas_call(
        body, name=name, out_shape=out_shape, grid=(s // ts,), in_specs=in_specs, out_specs=out_specs,
        compiler_params=_cparams("arbitrary" if reds else "parallel"),
    )(*[r[0] for r in rows], *vecs)


def _rms(x):
    r = lax.rsqrt(jnp.mean(x * x, axis=-1, keepdims=True) + EPS)
    return x * r, r


def _rms_bwd(xhat, r, dyg):
    return r * (dyg - xhat * jnp.mean(dyg * xhat, axis=-1, keepdims=True))


def _colsum(x):
    return jnp.sum(x, axis=0, keepdims=True)


def _sigmoid(x):
    return 1.0 / (1.0 + jnp.exp(-x))


GELU_C = math.sqrt(2.0 / math.pi)
GELU_A = 0.044715


def _gelu(x):
    return 0.5 * x * (1.0 + jnp.tanh(GELU_C * (x + GELU_A * x * x * x)))


def _gelu_grad(x):
    t = jnp.tanh(GELU_C * (x + GELU_A * x * x * x))
    return 0.5 * (1.0 + t) + 0.5 * x * (1.0 - t * t) * GELU_C * (1.0 + 3.0 * GELU_A * x * x)


def _ffn_in_swiglu(hn, wg):
    s, d = hn.shape
    g2, _, n = wg.shape
    g = g2 // 2
    tm = _tile(s, 1024)

    def body(h_ref, wa_ref, wb_ref, ab_ref, t_ref):
        hv = h_ref[...]
        a = jnp.dot(hv, wa_ref[0], preferred_element_type=F32)
        b = jnp.dot(hv, wb_ref[0], preferred_element_type=F32)
        ab_ref[0, 0] = a.astype(BF16)
        ab_ref[1, 0] = b.astype(BF16)
        t_ref[0] = (a * _sigmoid(a) * b).astype(BF16)

    return pl.pallas_call(
        body, name="mm_ffn_in_swiglu",
        out_shape=(jax.ShapeDtypeStruct((2, g, s, n), BF16), jax.ShapeDtypeStruct((g, s, n), BF16)),
        grid=(s // tm, g),
        in_specs=[pl.BlockSpec((tm, d), lambda i, j: (i, 0)), pl.BlockSpec((1, d, n), lambda i, j: (j, 0, 0)),
                  pl.BlockSpec((1, d, n), lambda i, j: (j + g, 0, 0))],
        out_specs=(pl.BlockSpec((2, 1, tm, n), lambda i, j: (0, j, i, 0)),
                   pl.BlockSpec((1, tm, n), lambda i, j: (j, i, 0))),
        compiler_params=_cparams("parallel", "parallel"),
    )(hn, wg, wg)


def _swiglu_bwd(ab, dt, ts):
    _, g, s, n = ab.shape
    ts = _tile(s, ts, 8)

    def body(ab_ref, dt_ref, t_ref, dab_ref):
        a = ab_ref[0, 0].astype(F32)
        b = ab_ref[1, 0].astype(F32)
        dt = dt_ref[0]
        sig = _sigmoid(a)
        silu = a * sig
        t_ref[0] = (silu * b).astype(BF16)
        dab_ref[0, 0] = (dt * b * (sig * (1.0 + a * (1.0 - sig)))).astype(BF16)
        dab_ref[1, 0] = (dt * silu).astype(BF16)

    both = pl.BlockSpec((2, 1, ts, n), lambda j, i: (0, j, i, 0))
    one = pl.BlockSpec((1, ts, n), lambda j, i: (j, i, 0))
    return pl.pallas_call(
        body, name="swiglu_bwd",
        out_shape=(jax.ShapeDtypeStruct((g, s, n), BF16), jax.ShapeDtypeStruct((2, g, s, n), BF16)),
        grid=(g, s // ts), in_specs=[both, one], out_specs=(one, both),
        compiler_params=_cparams("parallel", "parallel"),
    )(ab, dt)


def _forget_fwd(fl_t, b_col):
    h, s = fl_t.shape
    nb = s // LANE

    def body(fl_ref, b_ref, c_ref):
        upper = (lax.broadcasted_iota(jnp.int32, (LANE, LANE), 0)
                 <= lax.broadcasted_iota(jnp.int32, (LANE, LANE), 1)).astype(F32)

        def step(i, carry):
            x = fl_ref[i] + b_ref[...]
            lf = jnp.minimum(x, 0.0) - jnp.log(1.0 + jnp.exp(-jnp.abs(x)))
            cs = jnp.dot(lf, upper, precision=lax.Precision.HIGHEST, preferred_element_type=F32) + carry
            c_ref[i] = cs
            return cs[:, LANE - 1:LANE]

        lax.fori_loop(0, nb, step, jnp.zeros((h, 1), F32))

    out = pl.pallas_call(
        body, name="forget_fwd", out_shape=jax.ShapeDtypeStruct((nb, h, LANE), F32),
        compiler_params=_cparams(),
    )(fl_t.reshape(h, nb, LANE).transpose(1, 0, 2), b_col)
    return out.transpose(1, 0, 2).reshape(h, s)


def _forget_bwd(dc_t, fl_t, b_col):
    h, s = fl_t.shape
    nb = s // LANE

    def body(dc_ref, fl_ref, b_ref, dfl_ref, db_ref):
        lower = (lax.broadcasted_iota(jnp.int32, (LANE, LANE), 0)
                 >= lax.broadcasted_iota(jnp.int32, (LANE, LANE), 1)).astype(F32)

        def step(t, carry):
            tail, db = carry
            i = nb - 1 - t
            rc = jnp.dot(dc_ref[i], lower, precision=lax.Precision.HIGHEST, preferred_element_type=F32) + tail
            dfl = rc * (1.0 - _sigmoid(fl_ref[i] + b_ref[...]))
            dfl_ref[i] = dfl
            return rc[:, 0:1], db + jnp.sum(dfl, axis=1, keepdims=True)

        _, db = lax.fori_loop(0, nb, step, (jnp.zeros((h, 1), F32), jnp.zeros((h, 1), F32)))
        db_ref[...] = db

    blocked = lambda a: a.reshape(h, nb, LANE).transpose(1, 0, 2)
    dfl, db = pl.pallas_call(
        body, name="forget_bwd",
        out_shape=(jax.ShapeDtypeStruct((nb, h, LANE), F32), jax.ShapeDtypeStruct((h, 1), F32)),
        compiler_params=_cparams(),
    )(blocked(dc_t), blocked(fl_t), b_col)
    return dfl.transpose(1, 0, 2).reshape(h, s), db


N_PAIRS = N_HEADS // 2


def _causal_mask(t):
    return lax.broadcasted_iota(jnp.int32, (t, t), 0) >= lax.broadcasted_iota(jnp.int32, (t, t), 1)


def _head_lanes():
    return lax.broadcasted_iota(jnp.int32, (1, 2 * HEAD_DIM), 1) < HEAD_DIM


def _pick(x2, first, hh):
    zero = jnp.zeros_like(x2)
    return jnp.where(first, x2, zero) if hh == 0 else jnp.where(first, zero, x2)


def _attn_fwd(qkv, c_col, c_row, tq):
    s = qkv.shape[0]
    nq = s // tq
    w = 2 * HEAD_DIM

    def body(q_ref, k_ref, v_ref, cc_ref, cr_ref, o_ref, lse_ref):
        i = pl.program_id(1)
        first = _head_lanes()
        q2 = q_ref[...] * ATT_SCALE
        qs = [_pick(q2, first, hh) for hh in range(2)]

        def block(j, carry, masked):
            off = pl.multiple_of(j * tq, tq)
            k2 = k_ref[pl.ds(off, tq), :]
            v2 = v_ref[pl.ds(off, tq), :]
            new = []
            for hh in range(2):
                m, l, acc = carry[hh]
                sc = lax.dot_general(qs[hh], k2, NT, preferred_element_type=F32) + cc_ref[hh] - cr_ref[hh, j]
                if masked:
                    sc = jnp.where(_causal_mask(tq), sc, NEG_INF)
                m_new = jnp.maximum(m, jnp.max(sc, axis=-1, keepdims=True))
                alpha = jnp.exp(m - m_new)
                p = jnp.exp(sc - m_new)
                l = alpha * l + jnp.sum(p, axis=-1, keepdims=True)
                p_hi = p.astype(BF16)
                p_lo = (p - p_hi.astype(F32)).astype(BF16)
                acc = (alpha * acc + jnp.dot(p_hi, v2, preferred_element_type=F32)
                       + jnp.dot(p_lo, v2, preferred_element_type=F32))
                new.append((m_new, l, acc))
            return tuple(new)

        one = (jnp.full((tq, 1), NEG_INF, F32), jnp.zeros((tq, 1), F32), jnp.zeros((tq, w), F32))
        carry = lax.fori_loop(0, i, lambda j, c: block(j, c, False), (one, one))
        (m0, l0, a0), (m1, l1, a1) = block(i, carry, True)
        o_ref[...] = jnp.where(first, a0 / l0, a1 / l1)
        lse_ref[0] = m0 + jnp.log(l0)
        lse_ref[1] = m1 + jnp.log(l1)

    return pl.pallas_call(
        body, name="attn_fwd",
        out_shape=(jax.ShapeDtypeStruct((s, D_ATT), F32), jax.ShapeDtypeStruct((N_HEADS, s, 1), F32)),
        grid=(N_PAIRS, nq),
        in_specs=[pl.BlockSpec((tq, w), lambda hp, i: (i, hp)),
                  pl.BlockSpec((s, w), lambda hp, i: (0, N_PAIRS + hp)),
                  pl.BlockSpec((s, w), lambda hp, i: (0, 2 * N_PAIRS + hp)),
                  pl.BlockSpec((2, tq, 1), lambda hp, i: (hp, i, 0)),
                  pl.BlockSpec((2, nq, 1, tq), lambda hp, i: (hp, 0, 0, 0))],
        out_specs=(pl.BlockSpec((tq, w), lambda hp, i: (i, hp)),
                   pl.BlockSpec((2, tq, 1), lambda hp, i: (hp, i, 0))),
        compiler_params=_cparams("parallel", "parallel"),
    )(qkv, qkv, qkv, c_col, c_row)


def _attn_delta(o, do, tq):
    s = o.shape[0]
    w = 2 * HEAD_DIM

    def body(o_ref, do_ref, d_ref):
        first = _head_lanes()
        prod = o_ref[...] * do_ref[...].astype(F32)
        d_ref[0] = jnp.sum(_pick(prod, first, 0), axis=-1, keepdims=True)
        d_ref[1] = jnp.sum(_pick(prod, first, 1), axis=-1, keepdims=True)

    blk = pl.BlockSpec((tq, w), lambda hp, i: (i, hp))
    return pl.pallas_call(
        body, name="attn_delta", out_shape=jax.ShapeDtypeStruct((N_HEADS, s, 1), F32), grid=(N_PAIRS, s // tq),
        in_specs=[blk, blk], out_specs=pl.BlockSpec((2, tq, 1), lambda hp, i: (hp, i, 0)),
        compiler_params=_cparams("parallel", "parallel"),
    )(o, do)


def _attn_bwd(qkv, do, lse, delta, c_col, c_row, tq):
    s = qkv.shape[0]
    nq = s // tq
    w = 2 * HEAD_DIM

    def body(q_ref, do_ref, lse_ref, dl_ref, cc_ref, k_ref, v_ref, cr_ref, dq_ref, dk_ref, dv_ref, dc_ref):
        j = pl.program_id(1)
        first = _head_lanes()

        @pl.when(j == 0)
        def _():
            dq_ref[...] = jnp.zeros_like(dq_ref)

        k2 = k_ref[...]
        v2 = v_ref[...]

        def step(i, carry, masked):
            off = pl.multiple_of(i * tq, tq)
            rows = pl.ds(off, tq)
            q2 = q_ref[rows, :] * ATT_SCALE
            do2 = do_ref[rows, :]
            new, dqs = [], []
            for hh in range(2):
                dk, dv, dcs = carry[hh]
                sc = (lax.dot_general(_pick(q2, first, hh), k2, NT, preferred_element_type=F32)
                      + cc_ref[hh, rows, :] - cr_ref[hh, 0])
                if masked:
                    sc = jnp.where(_causal_mask(tq), sc, NEG_INF)
                p = jnp.exp(sc - lse_ref[hh, rows, :])
                dv = dv + lax.dot_general(p.astype(BF16), do2, TN, preferred_element_type=F32)
                dp = lax.dot_general(_pick(do2, first, hh), v2, NT, preferred_element_type=F32)
                ds = p * (dp - dl_ref[hh, rows, :])
                dsb = ds.astype(BF16)
                dk = dk + lax.dot_general(dsb, q2, TN, preferred_element_type=F32)
                dqs.append(jnp.dot(dsb, k2, preferred_element_type=F32))
                new.append((dk, dv, dcs + jnp.sum(ds, axis=0, keepdims=True)))
            dq_ref[rows, :] += jnp.where(first, dqs[0], dqs[1]) * ATT_SCALE
            return tuple(new)

        one = (jnp.zeros((tq, w), F32), jnp.zeros((tq, w), F32), jnp.zeros((1, tq), F32))
        carry = step(j, (one, one), True)
        (dk0, dv0, dc0), (dk1, dv1, dc1) = lax.fori_loop(j + 1, nq, lambda i, c: step(i, c, False), carry)
        dk_ref[...] = jnp.where(first, dk0, dk1)
        dv_ref[...] = jnp.where(first, dv0, dv1)
        dc_ref[0, 0] = -dc0
        dc_ref[1, 0] = -dc1

    whole_cols = lambda off: pl.BlockSpec((s, w), lambda hp, j: (0, off + hp))
    whole_heads = pl.BlockSpec((2, s, 1), lambda hp, j: (hp, 0, 0))
    blk = lambda off: pl.BlockSpec((tq, w), lambda hp, j: (j, off + hp))
    crow = pl.BlockSpec((2, 1, 1, tq), lambda hp, j: (hp, j, 0, 0))
    return pl.pallas_call(
        body, name="attn_bwd",
        out_shape=(jax.ShapeDtypeStruct((s, D_ATT), F32), jax.ShapeDtypeStruct((s, D_ATT), F32),
                   jax.ShapeDtypeStruct((s, D_ATT), F32), jax.ShapeDtypeStruct((N_HEADS, nq, 1, tq), F32)),
        grid=(N_PAIRS, nq),
        in_specs=[whole_cols(0), whole_cols(0), whole_heads, whole_heads, whole_heads,
                  blk(N_PAIRS), blk(2 * N_PAIRS), crow],
        out_specs=(whole_cols(0), blk(0), blk(0), crow),
        compiler_params=_cparams("parallel", "arbitrary"),
    )(qkv, do, lse, delta, c_col, qkv, qkv, c_row)


def _pair_sums(x, first):
    total = jnp.sum(x, axis=-1, keepdims=True)
    head = jnp.sum(jnp.where(first, x, 0.0), axis=-1, keepdims=True)
    return head, total - head


def _pair_mean(x, first):
    head, tail = _pair_sums(x, first)
    return jnp.where(first, head, tail) * (1.0 / GROUP_DIM)


def _gm_pair_norm(v2, first):
    d = v2 - _pair_mean(v2, first)
    rstd = lax.rsqrt(_pair_mean(d * d, first) + EPS)
    return d * rstd, rstd


def _gm_pair_mix(w_ref, pr, rhs, first):
    return jnp.where(first, jnp.dot(w_ref[2 * pr], rhs, preferred_element_type=F32),
                     jnp.dot(w_ref[2 * pr + 1], rhs, preferred_element_type=F32))


def _gmlp_fwd(z, wt, bs_t, vgain):
    s = z.shape[0]

    def body(gu_ref, gv_ref, wt_ref, bs_ref, vg_ref, o_ref):
        first = _head_lanes()
        for pr in range(N_GROUPS // 2):
            sl = slice(2 * pr * GROUP_DIM, 2 * (pr + 1) * GROUP_DIM)
            vhat, _ = _gm_pair_norm(_gelu(gv_ref[:, sl]), first)
            vn = (vhat * vg_ref[:, sl]).astype(BF16)
            bias = jnp.where(first, bs_ref[:, 2 * pr:2 * pr + 1], bs_ref[:, 2 * pr + 1:2 * pr + 2])
            o_ref[:, sl] = _gelu(gu_ref[:, sl]) * (_gm_pair_mix(wt_ref, pr, vn, first) + bias)

    full = lambda a: pl.BlockSpec(a.shape, lambda n: (0,) * a.ndim)
    return pl.pallas_call(
        body, name="gmlp_fwd", out_shape=jax.ShapeDtypeStruct((s, D_GM), F32), grid=(s // CHUNK,),
        in_specs=[pl.BlockSpec((CHUNK, D_GM), lambda n: (n, 3)), pl.BlockSpec((CHUNK, D_GM), lambda n: (n, 4)),
                  full(wt), full(bs_t), full(vgain)],
        out_specs=pl.BlockSpec((CHUNK, D_GM), lambda n: (n, 0)),
        compiler_params=_cparams("parallel"),
    )(z, z, wt, bs_t, vgain)


def _gmlp_bwd(z, dgm, wt, wt_t, bs_t, vgain):
    s = z.shape[0]

    def body(gu_ref, gv_ref, dgm_ref, wt_ref, wtt_ref, bs_ref, vg_ref, dgu_ref, dgv_ref, dwt_ref, dbs_ref, dvg_ref):
        @pl.when(pl.program_id(0) == 0)
        def _():
            dwt_ref[...] = jnp.zeros_like(dwt_ref)
            dbs_ref[...] = jnp.zeros_like(dbs_ref)
            dvg_ref[...] = jnp.zeros_like(dvg_ref)

        first = _head_lanes()
        for pr in range(N_GROUPS // 2):
            g0, g1 = 2 * pr, 2 * pr + 1
            sl = slice(g0 * GROUP_DIM, (g1 + 1) * GROUP_DIM)
            gu = gu_ref[:, sl]
            gv = gv_ref[:, sl]
            dgm = dgm_ref[:, sl]
            vhat, rstd = _gm_pair_norm(_gelu(gv), first)
            gain = vg_ref[:, sl]
            vn = (vhat * gain).astype(BF16)
            bias = jnp.where(first, bs_ref[:, g0:g0 + 1], bs_ref[:, g1:g1 + 1])
            mixed = _gm_pair_mix(wt_ref, pr, vn, first) + bias
            dgu_ref[:, sl] = (dgm * mixed * _gelu_grad(gu)).astype(BF16)
            dmixed = dgm * _gelu(gu)
            db0, db1 = _pair_sums(dmixed, first)
            dbs_ref[:, g0:g0 + 1] += db0
            dbs_ref[:, g1:g1 + 1] += db1
            dwt_ref[g0] += lax.dot_general(_pick(dmixed, first, 0).astype(BF16), vn, NT, preferred_element_type=F32)
            dwt_ref[g1] += lax.dot_general(_pick(dmixed, first, 1).astype(BF16), vn, NT, preferred_element_type=F32)
            dvn = _gm_pair_mix(wtt_ref, pr, dmixed.astype(BF16), first)
            dvg_ref[:, sl] += _colsum(dvn * vhat)
            dvhat = dvn * gain
            dvf = rstd * (dvhat - _pair_mean(dvhat, first) - vhat * _pair_mean(dvhat * vhat, first))
            dgv_ref[:, sl] = (dvf * _gelu_grad(gv)).astype(BF16)

    full = lambda a: pl.BlockSpec(a.shape, lambda n: (0,) * a.ndim)
    chunk = pl.BlockSpec((CHUNK, D_GM), lambda n: (n, 0))
    return pl.pallas_call(
        body, name="gmlp_bwd",
        out_shape=(jax.ShapeDtypeStruct((s, D_GM), BF16), jax.ShapeDtypeStruct((s, D_GM), BF16),
                   jax.ShapeDtypeStruct(wt.shape, F32), jax.ShapeDtypeStruct(bs_t.shape, F32),
                   jax.ShapeDtypeStruct(vgain.shape, F32)),
        grid=(s // CHUNK,),
        in_specs=[pl.BlockSpec((CHUNK, D_GM), lambda n: (n, 3)), pl.BlockSpec((CHUNK, D_GM), lambda n: (n, 4)),
                  chunk, full(wt), full(wt_t), full(bs_t), full(vgain)],
        out_specs=(chunk, chunk, full(wt), full(bs_t), full(vgain)),
        compiler_params=_cparams("arbitrary"),
    )(z, z, dgm, wt, wt_t, bs_t, vgain)


def _layer_fwd(h0, p_i, w, tq, late):
    s, d = h0.shape
    nq = s // tq
    sv = {"h0": h0}

    (hn1,) = _rowwise(lambda h, g: _rms(h)[0] * g, [h0], [w["mix_pre_norm"]], [(d, BF16)], [], "pre_mix", 256)
    z = _mm(hn1, w["w_in"], "nn", F32, "mm_in")
    fl_t = z[:, F_OFF:F_OFF + N_HEADS].T
    c_t = _forget_fwd(fl_t, w["b_forget"])
    c_col = c_t[:, :, None]
    c_row = c_t.reshape(N_HEADS, nq, 1, tq)
    qkv = z[:, :3 * D_ATT].astype(BF16)
    att, lse = _attn_fwd(qkv, c_col, c_row, tq)
    gm = _gmlp_fwd(z, w["wt"], w["bs_t"], w["gm_v_norm"])
    w = dict(w, **late(att))

    def mix_out(att, gm, g):
        return jnp.concatenate([_rms(att)[0] * g[:, :D_ATT], _rms(gm)[0] * g[:, D_ATT:]], axis=-1)

    (mc,) = _rowwise(mix_out, [att, gm], [w["mix_out_norm"]], [(D_ATT + D_GM, BF16)], [], "mix_out", 256)
    y1 = _mm(mc, w["w_out"], "nn", F32, "mm_out")

    def post_mix(h0, y1, gpost, gpre):
        h1 = h0 + _rms(y1)[0] * gpost
        return h1, _rms(h1)[0] * gpre

    h1, hn2 = _rowwise(post_mix, [h0, y1], [w["mix_post_norm"], w["ffn_pre_norm"]],
                       [(d, F32), (d, BF16)], [], "post_mix", 256)
    ab, t = _ffn_in_swiglu(hn2, w["w_ffn_in"])
    y2 = _mm(t, w["w_ffn_out"], "nn", F32, "mm_ffn_out", a3="k", b3="k")

    def post_ffn(h1, y2, g):
        h2 = h1 + _rms(y2)[0] * g
        return h2, _rms(h2)[0]

    h2, hr = _rowwise(post_ffn, [h1, y2], [w["ffn_post_norm"]], [(d, F32), (d, BF16)], [], "post_ffn", 256)
    gl = _mm(hr, w["w_ple_gate"], "nn", F32, "mm_gate")
    pe = _mm(p_i, w["w_ple"], "nn", F32, "mm_ple")
    (h3,) = _rowwise(lambda h2, gl, pe, g: h2 + _sigmoid(gl) * (_rms(pe)[0] * g), [h2, gl, pe], [w["ple_norm"]],
                     [(d, F32)], [], "ple_out", 256)
    sv.update(hn1=hn1, z=z, fl_t=fl_t, c_col=c_col, c_row=c_row, qkv=qkv, lse=lse, att=att, gm=gm,
              mc=mc, y1=y1, h1=h1, hn2=hn2, ab=ab, y2=y2, h2=h2, hr=hr, gl=gl, pe=pe, p_i=p_i)
    return h3, sv


def _layer_bwd(dh3, sv, w, tq, mid):
    s, d = dh3.shape
    g = {}
    by_rows = lambda a: a.reshape(N_DEV, -1, a.shape[-1])
    by_cols = lambda a: jnp.stack(jnp.split(a, N_DEV, axis=-1))

    def ple_bwd(dh3, gl, pe, gple):
        gate = _sigmoid(gl)
        pehat, rpe = _rms(pe)
        dgl = dh3 * (pehat * gple) * gate * (1.0 - gate)
        de = dh3 * gate
        return dgl, _rms_bwd(pehat, rpe, de * gple), _colsum(de * pehat)

    dgl, dpe, g["ple_norm"] = _rowwise(ple_bwd, [dh3, sv["gl"], sv["pe"]], [w["ple_norm"]],
                                       [(d, BF16), (d, BF16)], [d], "ple_bwd", 256)
    g["w_ple_gate"] = by_rows(_mm(sv["hr"], dgl, "tn", BF16, "mm_dgate"))
    dhr = _mm(dgl, w["w_ple_gate"], "nt", F32, "mm_dhr")
    g["w_ple"] = by_cols(_mm(sv["p_i"], dpe, "tn", BF16, "mm_dple"))

    def ffn_post_bwd(dh3, dhr, h2, y2, gpost):
        h2hat, r2 = _rms(h2)
        dh2 = dh3 + _rms_bwd(h2hat, r2, dhr)
        y2hat, ry = _rms(y2)
        return dh2, _rms_bwd(y2hat, ry, dh2 * gpost), _colsum(dh2 * y2hat)

    dh2, dy2, g["ffn_post_norm"] = _rowwise(ffn_post_bwd, [dh3, dhr, sv["h2"], sv["y2"]], [w["ffn_post_norm"]],
                                            [(d, F32), (d, BF16)], [d], "ffn_post_bwd", 256)
    dt = _mm(dy2, w["w_ffn_out"], "nt", F32, "mm_dt", b3="n", o3="n")
    t, dab = _swiglu_bwd(sv["ab"], dt, 256)
    dab = dab.reshape((N_DEV,) + dab.shape[2:])
    g["w_ffn_out"] = by_rows(_mm(t, dy2, "tn", BF16, "mm_dffn_out", a3="m", o3="m"))
    dhn2 = _mm(dab, w["w_ffn_in"], "nt", F32, "mm_dhn2", a3="k", b3="k")
    g["w_ffn_in"] = _mm(sv["hn2"], dab, "tn", BF16, "mm_dffn_in", b3="n", o3="n")

    def mix_post_bwd(dh2, dhn2, h1, y1, gpre, gpost):
        h1hat, r1 = _rms(h1)
        dh1 = dh2 + _rms_bwd(h1hat, r1, dhn2 * gpre)
        y1hat, ry = _rms(y1)
        return dh1, _rms_bwd(y1hat, ry, dh1 * gpost), _colsum(dhn2 * h1hat), _colsum(dh1 * y1hat)

    dh1, dy1, g["ffn_pre_norm"], g["mix_post_norm"] = _rowwise(
        mix_post_bwd, [dh2, dhn2, sv["h1"], sv["y1"]], [w["ffn_pre_norm"], w["mix_post_norm"]],
        [(d, F32), (d, BF16)], [d, d], "mix_post_bwd", 256)
    dmc = _mm(dy1, w["w_out"], "nt", F32, "mm_dmc")
    g["w_out"] = by_rows(_mm(sv["mc"], dy1, "tn", BF16, "mm_dout"))
    w = dict(w, **mid(g, dmc))

    def mix_out_bwd(da, dg, att, gm, gain):
        atthat, ra = _rms(att)
        gmhat, rg = _rms(gm)
        dgain = jnp.concatenate([_colsum(da * atthat), _colsum(dg * gmhat)], axis=-1)
        return _rms_bwd(atthat, ra, da * gain[:, :D_ATT]), _rms_bwd(gmhat, rg, dg * gain[:, D_ATT:]), dgain

    datt, dgm, g["mix_out_norm"] = _rowwise(
        mix_out_bwd, [(dmc, 0, D_ATT), (dmc, 1, D_GM), sv["att"], sv["gm"]], [w["mix_out_norm"]],
        [(D_ATT, BF16), (D_GM, F32)], [D_ATT + D_GM], "mix_out_bwd", 256)

    dgu, dgv, dwt, dbs_t, g["gm_v_norm"] = _gmlp_bwd(sv["z"], dgm, w["wt"], w["wt_t"], w["bs_t"], w["gm_v_norm"])
    g["gm_w_s"] = dwt * jnp.tril(jnp.ones((CHUNK, CHUNK), F32))[None]
    g["gm_b_s"] = dbs_t.T

    delta = _attn_delta(sv["att"], datt, tq)
    dq, dk, dv, dc_row = _attn_bwd(sv["qkv"], datt, sv["lse"], delta, sv["c_col"], sv["c_row"], tq)
    dfl_t, db = _forget_bwd(dc_row.reshape(N_HEADS, s), sv["fl_t"], w["b_forget"])
    g["b_forget"] = db.reshape(1, N_HEADS)
    dz = jnp.concatenate([dq.astype(BF16), dk.astype(BF16), dv.astype(BF16), dgu, dgv, dfl_t.T.astype(BF16),
                          jnp.zeros((s, D_IN_PAD - F_OFF - N_HEADS), BF16)], axis=-1)
    dhn1 = _mm(dz, w["w_in"], "nt", F32, "mm_dhn1")
    din = _mm(sv["hn1"], dz, "tn", BF16, "mm_din")
    din = jnp.concatenate([din[:, :3 * D_ATT], din[:, F_OFF:F_OFF + N_HEADS], din[:, 3 * D_ATT:F_OFF]], axis=-1)
    g["w_in"] = by_cols(din)

    def mix_pre_bwd(dh1, dhn1, h0, gpre):
        h0hat, r0 = _rms(h0)
        return dh1 + _rms_bwd(h0hat, r0, dhn1 * gpre), _colsum(dhn1 * h0hat)

    dh0, g["mix_pre_norm"] = _rowwise(mix_pre_bwd, [dh1, dhn1, sv["h0"]], [w["mix_pre_norm"]],
                                      [(d, F32)], [d], "mix_pre_bwd", 256)
    return dh0, g


ANY = pl.BlockSpec(memory_space=pl.ANY)


def _all_gather(xs, layer, name):
    n = len(xs)

    def body(*refs):
        x_refs, out_refs = refs[:n], refs[n:2 * n]
        send_sems, recv_sems, local_sems = refs[2 * n:]
        x, y, c = lax.axis_index("x"), lax.axis_index("y"), lax.axis_index("c")
        me, sibling = (x, y, c), (x, y, 1 - c)
        chips = [(1 - x, y), (x, 1 - y), (1 - x, 1 - y)]

        def shard(a):
            return x_refs[a] if layer is None else x_refs[a].at[layer]

        def rows(a, px, py, pc):
            return out_refs[a].at[4 * px + 2 * py + pc]

        def copy(a, kk, block, to, from_shard=False):
            return pltpu.make_async_remote_copy(
                src_ref=shard(a) if from_shard else rows(a, *block), dst_ref=rows(a, *block),
                send_sem=send_sems.at[7 * a + kk], recv_sem=recv_sems.at[7 * a + kk],
                device_id=to, device_id_type=MESH)

        mine = [pltpu.make_async_copy(shard(a), rows(a, *me), local_sems.at[a]) for a in range(n)]
        for cp in mine:
            cp.start()
        first = []
        for a in range(n):
            first.append(copy(a, 0, me, sibling, from_shard=True))
            first += [copy(a, 1 + j, me, (*chip, c), from_shard=True) for j, chip in enumerate(chips)]
        for cp in first:
            cp.start()
        passed = []
        for j, chip in enumerate(chips):
            for a in range(n):
                copy(a, 1 + j, (*chip, c), me).wait_recv()
                passed.append(copy(a, 4 + j, (*chip, c), sibling))
                passed[-1].start()
        for a in range(n):
            copy(a, 0, sibling, me).wait_recv()
        for j, chip in enumerate(chips):
            for a in range(n):
                copy(a, 4 + j, (*chip, 1 - c), me).wait_recv()
        for cp in first + passed:
            cp.wait_send()
        for cp in mine:
            cp.wait()

    shapes = [x.shape if layer is None else x.shape[1:] for x in xs]
    return pl.pallas_call(
        body, name=name, out_shape=[jax.ShapeDtypeStruct((N_DEV,) + sh, x.dtype) for sh, x in zip(shapes, xs)],
        in_specs=[ANY] * n, out_specs=[ANY] * n,
        scratch_shapes=[pltpu.SemaphoreType.DMA((7 * n,)), pltpu.SemaphoreType.DMA((7 * n,)),
                        pltpu.SemaphoreType.DMA((n,))],
    )(*xs)


HBM = pl.BlockSpec(memory_space=pltpu.HBM)
SEMS = pl.BlockSpec(memory_space=pltpu.SEMAPHORE)
EFFECT = pltpu.SideEffectType.DATAFLOW_SIDE_EFFECTING
FLIPS = tuple((fx, fy, fc) for fx in (0, 1) for fy in (0, 1) for fc in (0, 1))[1:]


def _exchange_copies(src_refs, land_refs, send_sems, recv_sems, layer, scatter):
    x, y, c = lax.axis_index("x"), lax.axis_index("y"), lax.axis_index("c")
    me = 4 * x + 2 * y + c
    copies = []
    for a, (src, land) in enumerate(zip(src_refs, land_refs)):
        for f, (fx, fy, fc) in enumerate(FLIPS):
            px, py, pc = (1 - x if fx else x), (1 - y if fy else y), (1 - c if fc else c)
            if scatter:
                block = src.at[4 * px + 2 * py + pc]
            else:
                block = src if layer is None else src.at[layer]
            copies.append(pltpu.make_async_remote_copy(
                src_ref=block, dst_ref=land.at[me], send_sem=send_sems.at[7 * a + f], recv_sem=recv_sems.at[7 * a + f],
                device_id=(px, py, pc), device_id_type=MESH))
    return copies


def _exchange_start(srcs, lands, layer, scatter, name):
    n = len(srcs)

    def body(*refs):
        for cp in _exchange_copies(refs[:n], refs[n:2 * n], refs[2 * n], refs[2 * n + 1], layer, scatter):
            cp.start()
        token = refs[-1]
        token[...] = jnp.zeros_like(token)

    operands = list(srcs) + list(lands)
    outs = pl.pallas_call(
        body, name=name,
        out_shape=(pltpu.SemaphoreType.DMA((7 * n,)), pltpu.SemaphoreType.DMA((7 * n,)),
                   *[pltpu.HBM(a.shape, a.dtype) for a in operands], jax.ShapeDtypeStruct((8, LANE), F32)),
        in_specs=[HBM] * (2 * n),
        out_specs=(SEMS, SEMS, *[HBM] * (2 * n), pl.BlockSpec(memory_space=pltpu.VMEM)),
        input_output_aliases={i: 2 + i for i in range(2 * n)},
        compiler_params=pltpu.CompilerParams(has_side_effects=EFFECT),
    )(*[pltpu.with_memory_space_constraint(a, pltpu.HBM) for a in operands])
    return outs[0], outs[1], outs[2:2 + n], outs[2 + n:2 + 2 * n], outs[-1]


def _exchange_wait(started, after, layer, scatter, name):
    send_sems, recv_sems, srcs, lands, _ = started
    n = len(srcs)

    def body(*refs):
        for cp in _exchange_copies(refs[:n], refs[n:2 * n], refs[2 * n], refs[2 * n + 1], layer, scatter):
            cp.wait_send()
            cp.wait_recv()

    operands = list(srcs) + list(lands)
    outs = pl.pallas_call(
        body, name=name, out_shape=tuple(pltpu.HBM(a.shape, a.dtype) for a in operands),
        in_specs=[HBM] * (2 * n) + [SEMS, SEMS, ANY], out_specs=[HBM] * (2 * n),
        input_output_aliases={i: i for i in range(2 * n)},
        compiler_params=pltpu.CompilerParams(has_side_effects=EFFECT),
    )(*operands, send_sems, recv_sems, after)
    return outs[:n], outs[n:]


def _sum_devices(parts):
    _, r, c = parts.shape

    def body(p_ref, o_ref):
        acc = p_ref[0]
        for j in range(1, N_DEV):
            acc = acc + p_ref[j]
        o_ref[...] = acc

    return pl.pallas_call(
        body, name="small_sum", out_shape=jax.ShapeDtypeStruct((r, c), F32), grid=(r // SMALL_ROWS,),
        in_specs=[pl.BlockSpec((N_DEV, SMALL_ROWS, c), lambda i: (0, i, 0))],
        out_specs=pl.BlockSpec((SMALL_ROWS, c), lambda i: (i, 0)),
        compiler_params=_cparams("parallel"),
    )(parts)


def _adamw_math(w, g, m, v):
    m = ADAM_B1 * m + (1.0 - ADAM_B1) * g
    v = ADAM_B2 * v + (1.0 - ADAM_B2) * (g * g)
    m_hat = m / (1.0 - ADAM_B1 ** ADAM_STEP)
    v_hat = v / (1.0 - ADAM_B2 ** ADAM_STEP)
    return -ADAM_LR * (m_hat / (jnp.sqrt(v_hat) + ADAM_EPS) + ADAM_WD * w), m, v


def _adamw_shard(w, m, v, parts, layer, name):
    _, a, b = w.shape
    ta = _tile(a, 256, 16)

    def body(w_ref, m_ref, v_ref, p_ref, g_ref, d_ref, nm_ref, nv_ref):
        g = p_ref[0].astype(F32)
        for j in range(1, N_DEV):
            g = g + p_ref[j].astype(F32)
        g_ref[...] = g
        d_ref[...], nm_ref[...], nv_ref[...] = _adamw_math(w_ref[0], g, m_ref[0], v_ref[0])

    mine = pl.BlockSpec((1, ta, b), lambda i: (layer, i, 0))
    out = pl.BlockSpec((ta, b), lambda i: (i, 0))
    return pl.pallas_call(
        body, name=name, out_shape=[jax.ShapeDtypeStruct((a, b), F32)] * 4, grid=(a // ta,),
        in_specs=[mine, mine, mine, pl.BlockSpec((N_DEV, ta, b), lambda i: (0, i, 0))], out_specs=[out] * 4,
        compiler_params=_cparams("parallel"),
    )(w, m, v, parts)


def _pack_small(pieces):
    flat = jnp.concatenate([p.reshape(-1) for p in pieces])
    total = -(-flat.shape[0] // (SMALL_COLS * SMALL_ROWS)) * SMALL_COLS * SMALL_ROWS
    return jnp.pad(flat, (0, total - flat.shape[0])).reshape(-1, SMALL_COLS)


def kernel(x, p, mix_pre_norm, mix_post_norm, w_in, b_forget, gm_v_norm, gm_w_s, gm_b_s, mix_out_norm, w_out, ffn_pre_norm, ffn_post_norm, w_ffn_in, w_ffn_out, w_ple, ple_norm, w_ple_gate, loss_target, m_mix_pre_norm, m_mix_post_norm, m_w_in, m_b_forget, m_gm_v_norm, m_gm_w_s, m_gm_b_s, m_mix_out_norm, m_w_out, m_ffn_pre_norm, m_ffn_post_norm, m_w_ffn_in, m_w_ffn_out, m_w_ple, m_ple_norm, m_w_ple_gate, v_mix_pre_norm, v_mix_post_norm, v_w_in, v_b_forget, v_gm_v_norm, v_gm_w_s, v_gm_b_s, v_mix_out_norm, v_w_out, v_ffn_pre_norm, v_ffn_post_norm, v_w_ffn_in, v_w_ffn_out, v_w_ple, v_ple_norm, v_w_ple_gate):
    given = dict(locals())
    weights = {n: given[n] for n in WEIGHT_ORDER}
    mom_m = {n: given["m_" + n] for n in WEIGHT_ORDER}
    mom_v = {n: given["v_" + n] for n in WEIGHT_ORDER}
    depth = w_in.shape[0]
    s, d = x.shape[1], x.shape[2]
    tq = _tile(s, ATT_BLOCK)
    me = 4 * lax.axis_index("x") + 2 * lax.axis_index("y") + lax.axis_index("c")
    tril = jnp.tril(jnp.ones((CHUNK, CHUNK), F32))

    def landing(block):
        return lax.dynamic_update_index_in_dim(lax.empty((N_DEV,) + block.shape, block.dtype), block, me, 0)

    def mix_weights(i, got):
        w_in_full = jnp.concatenate([got["w_in"][j] for j in range(N_DEV)], axis=-1)
        pad = jnp.zeros((d, D_IN_PAD - D_IN), BF16)
        wt = gm_w_s[i] * tril[None]
        lw = dict(
            w_in=jnp.concatenate([w_in_full[:, :3 * D_ATT], w_in_full[:, 3 * D_ATT + N_HEADS:],
                                  w_in_full[:, 3 * D_ATT:3 * D_ATT + N_HEADS], pad], axis=-1),
            b_forget=b_forget[i][:, None], wt=wt.astype(BF16), wt_t=wt.transpose(0, 2, 1).astype(BF16),
            bs_t=gm_b_s[i].T)
        lw.update({n: weights[n][i][None] for n in ("mix_pre_norm", "mix_post_norm", "gm_v_norm", "mix_out_norm",
                                                    "ffn_pre_norm", "ffn_post_norm", "ple_norm")})
        return lw

    def rest_weights(got):
        return dict(w_out=got["w_out"].reshape(-1, d), w_ffn_in=got["w_ffn_in"],
                    w_ffn_out=got["w_ffn_out"].reshape(N_DEV // 2, -1, d),
                    w_ple=jnp.concatenate([got["w_ple"][j] for j in range(N_DEV)], axis=-1),
                    w_ple_gate=got["w_ple_gate"].reshape(-1, d))

    shards = {n: weights[n].astype(BF16) for n in MATRIX_WEIGHTS}

    def gather_start(i):
        started = {}
        order = jnp.zeros((), BF16)
        for tag, grp in EXCHANGE_GROUPS.items():
            started[tag] = _exchange_start([shards[n] for n in grp], [landing(shards[n][i] + order) for n in grp], i,
                                           False, f"weights_gather_start_{i}_{tag}")
            order = started[tag][4][0, 0].astype(BF16)
        return started

    def gather_finish(i, tag, pending, after):
        srcs, got = _exchange_wait(pending[tag], after, i, False, f"weights_gather_wait_{i}_{tag}")
        shards.update(zip(EXCHANGE_GROUPS[tag], srcs))
        return dict(zip(EXCHANGE_GROUPS[tag], got))

    h = x[0]
    saved, layer_w = [], []
    pending = gather_start(0)
    for i in range(depth):
        lw = mix_weights(i, gather_finish(i, "mix", pending, h))
        if i == 0:
            lw["mix_pre_norm"] = lw["mix_pre_norm"] + pending["rest"][4][:1, :1]
        following = {}

        def late(att, i=i, pending=pending, lw=lw, following=following):
            rest = rest_weights(gather_finish(i, "rest", pending, att))
            lw.update(rest)
            if i + 1 == depth:
                return rest
            following.update(gather_start(i + 1))
            token = following["mix"][4][:1, :1] + following["rest"][4][:1, :1]
            return dict(rest, mix_out_norm=lw["mix_out_norm"] + token)

        h, sv = _layer_fwd(h, p[i, 0], lw, tq, late)
        layer_w.append(lw)
        saved.append(sv)
        pending = following

    def loss_head(y, t):
        err = y - t
        return err * (1.0 / d), _colsum(err * err)

    dh, sq = _rowwise(loss_head, [h, loss_target[0]], [], [(d, F32)], [d], "loss_head", 256)
    loss = lax.psum(0.5 * jnp.sum(sq) / d, AXES)

    layer_g = [None] * depth
    shard_out = {n: [None] * depth for n in MATRIX_WEIGHTS}

    def scatter_start(i, tag, g):
        full_g = [g[n] for n in EXCHANGE_GROUPS[tag]]
        lands = [landing(lax.dynamic_index_in_dim(gf, me, 0, keepdims=False)) for gf in full_g]
        return _exchange_start(full_g, lands, None, True, f"grads_scatter_start_{i}_{tag}")

    def scatter_finish(i, tag, started, after):
        _, parts = _exchange_wait(started[tag], after, None, True, f"grads_scatter_wait_{i}_{tag}")
        for n, part in zip(EXCHANGE_GROUPS[tag], parts):
            shard_out[n][i] = _adamw_shard(weights[n], mom_m[n], mom_v[n], part, i, "adamw_" + n)

    before = None
    for i in reversed(range(depth)):
        lw = layer_w[i]
        if before is not None:
            lw = dict(lw, ple_norm=lw["ple_norm"] + before[1]["mix"][4][:1, :1])
        started = {}

        def mid(g, dmc, i=i, before=before, started=started, lw=lw):
            if before is not None:
                scatter_finish(before[0], "rest", before[1], dmc)
            started["rest"] = scatter_start(i, "rest", g)
            return dict(mix_out_norm=lw["mix_out_norm"] + started["rest"][4][:1, :1])

        dh, layer_g[i] = _layer_bwd(dh, saved[i], lw, tq, mid)
        if before is not None:
            scatter_finish(before[0], "mix", before[1], dh)
        started["mix"] = scatter_start(i, "mix", layer_g[i])
        before = (i, started)
    scatter_finish(before[0], "rest", before[1], before[1]["mix"][4])
    scatter_finish(before[0], "mix", before[1], dh)
    grad_x = dh[None]

    grads, deltas, new_m, new_v = {}, {}, {}, {}
    for n in MATRIX_WEIGHTS:
        grads[n], deltas[n], new_m[n], new_v[n] = (jnp.stack([shard_out[n][i][k] for i in range(depth)])
                                                   for k in range(4))

    small_g = _pack_small([jnp.stack([layer_g[i][n].reshape(-1) for i in range(depth)]) for n in SMALL_WEIGHTS])
    (gathered,) = _all_gather([small_g], None, "small_grads_all_gather")
    g_small = _sum_devices(gathered)
    pack = lambda t: _pack_small([t[n] for n in SMALL_WEIGHTS])
    dl, nm, nv = _rowwise(_adamw_math, [pack(weights), g_small, pack(mom_m), pack(mom_v)], [],
                          [(SMALL_COLS, F32)] * 3, [], "adamw_small", SMALL_ROWS)
    off = 0
    for n in SMALL_WEIGHTS:
        shp, size = weights[n].shape, weights[n].size
        grads[n], deltas[n], new_m[n], new_v[n] = (a.reshape(-1)[off:off + size].reshape(shp)
                                                   for a in (g_small, dl, nm, nv))
        off += size

    return (loss, grad_x, *[grads[n] for n in WEIGHT_ORDER], *[deltas[n] for n in WEIGHT_ORDER],
            *[new_m[n] for n in WEIGHT_ORDER], *[new_v[n] for n in WEIGHT_ORDER])
```

```python
import functools
import math

import jax
import jax.numpy as jnp
from jax import lax
from jax.experimental import pallas as pl
from jax.experimental.pallas import tpu as pltpu

F32 = jnp.float32
BF16 = jnp.bfloat16
MESH = pl.DeviceIdType.MESH
AXES = ("x", "y", "c")
N_DEV = 8

EPS = 1e-6
NEG_INF = -1e30
N_HEADS = 8
HEAD_DIM = 64
D_ATT = N_HEADS * HEAD_DIM
N_GROUPS = 8
GROUP_DIM = 64
D_GM = N_GROUPS * GROUP_DIM
CHUNK = 128
ATT_SCALE = HEAD_DIM ** -0.5
ATT_BLOCK = 512
D_IN = 3 * D_ATT + N_HEADS + 2 * D_GM
D_IN_PAD = 3 * D_ATT + 2 * D_GM + 128
F_OFF = 3 * D_ATT + 2 * D_GM

ADAM_LR = 0.001
ADAM_B1 = 0.9
ADAM_B2 = 0.999
ADAM_EPS = 1e-08
ADAM_WD = 0.01
ADAM_STEP = 10

LANE = 128
VMEM_LIMIT = 48 * 1024 * 1024
SMALL_COLS = 128
SMALL_ROWS = 512

MATRIX_WEIGHTS = ("w_in", "w_out", "w_ffn_in", "w_ffn_out", "w_ple", "w_ple_gate")
EXCHANGE_GROUPS = {"mix": ("w_in",), "rest": ("w_out", "w_ffn_in", "w_ffn_out", "w_ple", "w_ple_gate")}
SMALL_WEIGHTS = ("mix_pre_norm", "mix_post_norm", "b_forget", "gm_v_norm", "gm_w_s", "gm_b_s",
                 "mix_out_norm", "ffn_pre_norm", "ffn_post_norm", "ple_norm")
WEIGHT_ORDER = ("mix_pre_norm", "mix_post_norm", "w_in", "b_forget", "gm_v_norm", "gm_w_s", "gm_b_s",
                "mix_out_norm", "w_out", "ffn_pre_norm", "ffn_post_norm", "w_ffn_in", "w_ffn_out",
                "w_ple", "ple_norm", "w_ple_gate")


def _tile(n, pref, unit=LANE):
    best = None
    t = unit
    while t <= min(n, pref):
        if n % t == 0:
            best = t
        t += unit
    return n if best is None else best


def _cparams(*semantics):
    return pltpu.CompilerParams(dimension_semantics=semantics or None, vmem_limit_bytes=VMEM_LIMIT)


NN = (((1,), (0,)), ((), ()))
NT = (((1,), (1,)), ((), ()))
TN = (((0,), (0,)), ((), ()))
_MM_AXES = {
    "nn": ("i", "k", "k", "j"), "nt": ("i", "k", "j", "k"), "tn": ("k", "i", "k", "j")}
_MM_DN = {"nn": NN, "nt": NT, "tn": TN}


def _mm(a, b, dims, out_dtype, name, a3=None, b3=None, o3=None, tm=1024, tn=1024, tk=1024):
    ar, ac, br, bc = _MM_AXES[dims]
    letter = {"i": "m", "j": "n", "k": "k"}
    size = {}

    def measure(x, rows, cols, stacked):
        shape = x.shape
        if stacked is None:
            size.setdefault(letter[rows], shape[0])
            size.setdefault(letter[cols], shape[1])
        else:
            for ax, n in ((rows, shape[1]), (cols, shape[2])):
                size.setdefault(letter[ax], n * shape[0] if letter[ax] == stacked else n)

    measure(a, ar, ac, a3)
    measure(b, br, bc, b3)
    m, n, k = size["m"], size["n"], size["k"]
    slab = {}
    for x, stacked, rows, cols in ((a, a3, ar, ac), (b, b3, br, bc)):
        if stacked is not None:
            slab[stacked] = x.shape[1] if letter[rows] == stacked else x.shape[2]
    if o3 is not None:
        slab.setdefault(o3, slab.get(o3, None) or {"m": m, "n": n}[o3] // N_DEV)
    tile = {"m": slab.get("m") or _tile(m, tm), "n": slab.get("n") or _tile(n, tn), "k": slab.get("k") or _tile(k, tk)}
    nk = k // tile["k"]

    def spec(rows, cols, stacked):
        tr, tc = tile[letter[rows]], tile[letter[cols]]
        if stacked is None:
            return pl.BlockSpec((tr, tc), lambda i, j, kk: ({"i": i, "j": j, "k": kk}[rows], {"i": i, "j": j, "k": kk}[cols]))

        def imap(i, j, kk):
            g = {"i": i, "j": j, "k": kk}
            return (g[{"m": "i", "n": "j", "k": "k"}[stacked]],
                    0 if letter[rows] == stacked else g[rows], 0 if letter[cols] == stacked else g[cols])

        return pl.BlockSpec((1, tr, tc), imap)

    dn = _MM_DN[dims]

    def body(a_ref, b_ref, o_ref, *acc):
        av = a_ref[...] if a3 is None else a_ref[0]
        bv = b_ref[...] if b3 is None else b_ref[0]
        prod = lax.dot_general(av.astype(BF16), bv.astype(BF16), dn, preferred_element_type=F32)

        def emit(val):
            if o3 is None:
                o_ref[...] = val.astype(out_dtype)
            else:
                o_ref[0] = val.astype(out_dtype)

        if nk == 1:
            emit(prod)
            return
        (acc_ref,) = acc
        kk = pl.program_id(2)

        @pl.when(kk == 0)
        def _():
            acc_ref[...] = prod

        @pl.when(kk > 0)
        def _():
            acc_ref[...] += prod

        @pl.when(kk == nk - 1)
        def _():
            emit(acc_ref[...])

    if o3 is None:
        out_shape = (m, n)
    elif o3 == "m":
        out_shape = (m // tile["m"], tile["m"], n)
    else:
        out_shape = (n // tile["n"], m, tile["n"])
    return pl.pallas_call(
        body, name=name, out_shape=jax.ShapeDtypeStruct(out_shape, out_dtype),
        grid=(m // tile["m"], n // tile["n"], nk),
        in_specs=[spec(ar, ac, a3), spec(br, bc, b3)], out_specs=spec("i", "j", o3),
        scratch_shapes=[] if nk == 1 else [pltpu.VMEM((tile["m"], tile["n"]), F32)],
        compiler_params=_cparams("parallel", "parallel", "arbitrary"),
    )(a, b)


def _rowwise(fn, rows, vecs, outs, reds, name, ts):
    rows = [r if isinstance(r, tuple) else (r, 0, r.shape[1]) for r in rows]
    s = rows[0][0].shape[0]
    ts = _tile(s, ts, 8)
    nr, nv, no = len(rows), len(vecs), len(outs)

    def body(*refs):
        vals = fn(*[r[...] for r in refs[:nr + nv]])
        vals = vals if isinstance(vals, tuple) else (vals,)
        o_refs = refs[nr + nv:nr + nv + no]
        r_refs = refs[nr + nv + no:]
        for o_ref, val in zip(o_refs, vals[:no]):
            o_ref[...] = val.astype(o_ref.dtype)
        if r_refs:
            @pl.when(pl.program_id(0) == 0)
            def _():
                for r_ref in r_refs:
                    r_ref[...] = jnp.zeros_like(r_ref)

            for r_ref, val in zip(r_refs, vals[no:]):
                r_ref[...] += val

    in_specs = [pl.BlockSpec((ts, w), functools.partial(lambda i, cb: (i, cb), cb=cb)) for _, cb, w in rows]
    in_specs += [pl.BlockSpec(v.shape, lambda i: (0, 0)) for v in vecs]
    out_specs = [pl.BlockSpec((ts, c), lambda i: (i, 0)) for c, _ in outs]
    out_specs += [pl.BlockSpec((1, c), lambda i: (0, 0)) for c in reds]
    out_shape = [jax.ShapeDtypeStruct((s, c), dt) for c, dt in outs]
    out_shape += [jax.ShapeDtypeStruct((1, c), F32) for c in reds]
    return pl.pallas_call(
        body, name=name, out_shape=out_shape, grid=(s // ts,), in_specs=in_specs, out_specs=out_specs,
        compiler_params=_cparams("arbitrary" if reds else "parallel"),
    )(*[r[0] for r in rows], *vecs)


def _rms(x):
    r = lax.rsqrt(jnp.mean(x * x, axis=-1, keepdims=True) + EPS)
    return x * r, r


def _rms_bwd(xhat, r, dyg):
    return r * (dyg - xhat * jnp.mean(dyg * xhat, axis=-1, keepdims=True))


def _colsum(x):
    return jnp.sum(x, axis=0, keepdims=True)


def _sigmoid(x):
    return 1.0 / (1.0 + jnp.exp(-x))


GELU_C = math.sqrt(2.0 / math.pi)
GELU_A = 0.044715


def _gelu(x):
    return 0.5 * x * (1.0 + jnp.tanh(GELU_C * (x + GELU_A * x * x * x)))


def _gelu_grad(x):
    t = jnp.tanh(GELU_C * (x + GELU_A * x * x * x))
    return 0.5 * (1.0 + t) + 0.5 * x * (1.0 - t * t) * GELU_C * (1.0 + 3.0 * GELU_A * x * x)


def _ffn_in_swiglu(hn, wg):
    s, d = hn.shape
    g2, _, n = wg.shape
    g = g2 // 2
    tm = _tile(s, 1024)

    def body(h_ref, wa_ref, wb_ref, ab_ref, t_ref):
        hv = h_ref[...]
        a = jnp.dot(hv, wa_ref[0], preferred_element_type=F32)
        b = jnp.dot(hv, wb_ref[0], preferred_element_type=F32)
        ab_ref[0, 0] = a.astype(BF16)
        ab_ref[1, 0] = b.astype(BF16)
        t_ref[0] = (a * _sigmoid(a) * b).astype(BF16)

    return pl.pallas_call(
        body, name="mm_ffn_in_swiglu",
        out_shape=(jax.ShapeDtypeStruct((2, g, s, n), BF16), jax.ShapeDtypeStruct((g, s, n), BF16)),
        grid=(s // tm, g),
        in_specs=[pl.BlockSpec((tm, d), lambda i, j: (i, 0)), pl.BlockSpec((1, d, n), lambda i, j: (j, 0, 0)),
                  pl.BlockSpec((1, d, n), lambda i, j: (j + g, 0, 0))],
        out_specs=(pl.BlockSpec((2, 1, tm, n), lambda i, j: (0, j, i, 0)),
                   pl.BlockSpec((1, tm, n), lambda i, j: (j, i, 0))),
        compiler_params=_cparams("parallel", "parallel"),
    )(hn, wg, wg)


def _swiglu_bwd(ab, dt, ts):
    _, g, s, n = ab.shape
    ts = _tile(s, ts, 8)

    def body(ab_ref, dt_ref, t_ref, dab_ref):
        a = ab_ref[0, 0].astype(F32)
        b = ab_ref[1, 0].astype(F32)
        dt = dt_ref[0]
        sig = _sigmoid(a)
        silu = a * sig
        t_ref[0] = (silu * b).astype(BF16)
        dab_ref[0, 0] = (dt * b * (sig * (1.0 + a * (1.0 - sig)))).astype(BF16)
        dab_ref[1, 0] = (dt * silu).astype(BF16)

    both = pl.BlockSpec((2, 1, ts, n), lambda j, i: (0, j, i, 0))
    one = pl.BlockSpec((1, ts, n), lambda j, i: (j, i, 0))
    return pl.pallas_call(
        body, name="swiglu_bwd",
        out_shape=(jax.ShapeDtypeStruct((g, s, n), BF16), jax.ShapeDtypeStruct((2, g, s, n), BF16)),
        grid=(g, s // ts), in_specs=[both, one], out_specs=(one, both),
        compiler_params=_cparams("parallel", "parallel"),
    )(ab, dt)


def _forget_fwd(fl_t, b_col):
    h, s = fl_t.shape
    nb = s // LANE

    def body(fl_ref, b_ref, c_ref):
        upper = (lax.broadcasted_iota(jnp.int32, (LANE, LANE), 0)
                 <= lax.broadcasted_iota(jnp.int32, (LANE, LANE), 1)).astype(F32)

        def step(i, carry):
            x = fl_ref[i] + b_ref[...]
            lf = jnp.minimum(x, 0.0) - jnp.log(1.0 + jnp.exp(-jnp.abs(x)))
            cs = jnp.dot(lf, upper, precision=lax.Precision.HIGHEST, preferred_element_type=F32) + carry
            c_ref[i] = cs
            return cs[:, LANE - 1:LANE]

        lax.fori_loop(0, nb, step, jnp.zeros((h, 1), F32))

    out = pl.pallas_call(
        body, name="forget_fwd", out_shape=jax.ShapeDtypeStruct((nb, h, LANE), F32),
        compiler_params=_cparams(),
    )(fl_t.reshape(h, nb, LANE).transpose(1, 0, 2), b_col)
    return out.transpose(1, 0, 2).reshape(h, s)


def _forget_bwd(dc_t, fl_t, b_col):
    h, s = fl_t.shape
    nb = s // LANE

    def body(dc_ref, fl_ref, b_ref, dfl_ref, db_ref):
        lower = (lax.broadcasted_iota(jnp.int32, (LANE, LANE), 0)
                 >= lax.broadcasted_iota(jnp.int32, (LANE, LANE), 1)).astype(F32)

        def step(t, carry):
            tail, db = carry
            i = nb - 1 - t
            rc = jnp.dot(dc_ref[i], lower, precision=lax.Precision.HIGHEST, preferred_element_type=F32) + tail
            dfl = rc * (1.0 - _sigmoid(fl_ref[i] + b_ref[...]))
            dfl_ref[i] = dfl
            return rc[:, 0:1], db + jnp.sum(dfl, axis=1, keepdims=True)

        _, db = lax.fori_loop(0, nb, step, (jnp.zeros((h, 1), F32), jnp.zeros((h, 1), F32)))
        db_ref[...] = db

    blocked = lambda a: a.reshape(h, nb, LANE).transpose(1, 0, 2)
    dfl, db = pl.pallas_call(
        body, name="forget_bwd",
        out_shape=(jax.ShapeDtypeStruct((nb, h, LANE), F32), jax.ShapeDtypeStruct((h, 1), F32)),
        compiler_params=_cparams(),
    )(blocked(dc_t), blocked(fl_t), b_col)
    return dfl.transpose(1, 0, 2).reshape(h, s), db


N_PAIRS = N_HEADS // 2


def _causal_mask(t):
    return lax.broadcasted_iota(jnp.int32, (t, t), 0) >= lax.broadcasted_iota(jnp.int32, (t, t), 1)


def _head_lanes():
    return lax.broadcasted_iota(jnp.int32, (1, 2 * HEAD_DIM), 1) < HEAD_DIM


def _pick(x2, first, hh):
    zero = jnp.zeros_like(x2)
    return jnp.where(first, x2, zero) if hh == 0 else jnp.where(first, zero, x2)


BIAS_TERMS = 3


def _attn_prep(z, c, ts):
    s = z.shape[0]
    ts = _tile(s, ts, 16)
    w = 2 * HEAD_DIM

    def body(q_ref, k_ref, v_ref, c_ref, qa_ref, ka_ref, vb_ref):
        lane = lax.broadcasted_iota(jnp.int32, (1, w), 1)
        first = lane < HEAD_DIM
        cv = c_ref[...]
        for h in range(N_HEADS):
            pair = slice((h // 2) * w, (h // 2 + 1) * w)
            qh = q_ref[:, pair] * ATT_SCALE
            kh = k_ref[:, pair]
            if h % 2:
                qh = pltpu.roll(qh, HEAD_DIM, 1)
                kh = pltpu.roll(kh, HEAD_DIM, 1)
            rest = cv[:, h:h + 1]
            q_tail = jnp.zeros((1, w), F32)
            k_tail = jnp.zeros((1, w), F32)
            for t in range(BIAS_TERMS):
                term = rest.astype(BF16).astype(F32)
                rest = rest - term
                q_tail = jnp.where(lane == HEAD_DIM + t, term, jnp.where(lane == HEAD_DIM + BIAS_TERMS + t, 1.0, q_tail))
                k_tail = jnp.where(lane == HEAD_DIM + t, 1.0, jnp.where(lane == HEAD_DIM + BIAS_TERMS + t, -term, k_tail))
            qa_ref[:, h * w:(h + 1) * w] = jnp.where(first, qh, q_tail).astype(BF16)
            ka_ref[:, h * w:(h + 1) * w] = jnp.where(first, kh, k_tail).astype(BF16)
        vb_ref[...] = v_ref[...].astype(BF16)

    col = lambda cb: pl.BlockSpec((ts, D_ATT), lambda i: (i, cb))
    wide = pl.BlockSpec((ts, N_HEADS * w), lambda i: (i, 0))
    return pl.pallas_call(
        body, name="attn_prep",
        out_shape=(jax.ShapeDtypeStruct((s, N_HEADS * w), BF16), jax.ShapeDtypeStruct((s, N_HEADS * w), BF16),
                   jax.ShapeDtypeStruct((s, D_ATT), BF16)),
        grid=(s // ts,), in_specs=[col(0), col(1), col(2), pl.BlockSpec((ts, N_HEADS), lambda i: (i, 0))],
        out_specs=(wide, wide, col(0)),
        compiler_params=_cparams("parallel"),
    )(z, z, z, c)


def _attn_fwd(qa, ka, vb, tq):
    s = qa.shape[0]
    nq = s // tq
    w = 2 * HEAD_DIM

    def body(q_ref, k_ref, v_ref, o_ref, lse_ref):
        i = pl.program_id(1)
        first = _head_lanes()
        q2 = q_ref[...]

        def block(j, carry, masked):
            off = pl.multiple_of(j * tq, tq)
            k2 = k_ref[pl.ds(off, tq), :]
            v2 = v_ref[pl.ds(off, tq), :]
            new = []
            for hh in range(2):
                m, l, acc = carry[hh]
                sc = lax.dot_general(q2[:, hh * w:(hh + 1) * w], k2[:, hh * w:(hh + 1) * w], NT,
                                     preferred_element_type=F32)
                if masked:
                    sc = jnp.where(_causal_mask(tq), sc, NEG_INF)
                m_new = jnp.maximum(m, jnp.max(sc, axis=-1, keepdims=True))
                alpha = jnp.exp(m - m_new)
                p = jnp.exp(sc - m_new)
                l = alpha * l + jnp.sum(p, axis=-1, keepdims=True)
                p_hi = p.astype(BF16)
                p_lo = (p - p_hi.astype(F32)).astype(BF16)
                acc = (alpha * acc + jnp.dot(p_hi, v2, preferred_element_type=F32)
                       + jnp.dot(p_lo, v2, preferred_element_type=F32))
                new.append((m_new, l, acc))
            return tuple(new)

        one = (jnp.full((tq, 1), NEG_INF, F32), jnp.zeros((tq, 1), F32), jnp.zeros((tq, w), F32))
        carry = lax.fori_loop(0, i, lambda j, c: block(j, c, False), (one, one))
        (m0, l0, a0), (m1, l1, a1) = block(i, carry, True)
        o_ref[...] = jnp.where(first, a0 / l0, a1 / l1)
        lse_ref[0] = m0 + jnp.log(l0)
        lse_ref[1] = m1 + jnp.log(l1)

    return pl.pallas_call(
        body, name="attn_fwd",
        out_shape=(jax.ShapeDtypeStruct((s, D_ATT), F32), jax.ShapeDtypeStruct((N_HEADS, s, 1), F32)),
        grid=(N_PAIRS, nq),
        in_specs=[pl.BlockSpec((tq, 2 * w), lambda hp, i: (i, hp)),
                  pl.BlockSpec((s, 2 * w), lambda hp, i: (0, hp)),
                  pl.BlockSpec((s, w), lambda hp, i: (0, hp))],
        out_specs=(pl.BlockSpec((tq, w), lambda hp, i: (i, hp)),
                   pl.BlockSpec((2, tq, 1), lambda hp, i: (hp, i, 0))),
        compiler_params=_cparams("parallel", "parallel"),
    )(qa, ka, vb)


def _attn_delta(o, do, tq):
    s = o.shape[0]
    w = 2 * HEAD_DIM

    def body(o_ref, do_ref, d_ref):
        first = _head_lanes()
        prod = o_ref[...] * do_ref[...].astype(F32)
        d_ref[0] = jnp.sum(_pick(prod, first, 0), axis=-1, keepdims=True)
        d_ref[1] = jnp.sum(_pick(prod, first, 1), axis=-1, keepdims=True)

    blk = pl.BlockSpec((tq, w), lambda hp, i: (i, hp))
    return pl.pallas_call(
        body, name="attn_delta", out_shape=jax.ShapeDtypeStruct((N_HEADS, s, 1), F32), grid=(N_PAIRS, s // tq),
        in_specs=[blk, blk], out_specs=pl.BlockSpec((2, tq, 1), lambda hp, i: (hp, i, 0)),
        compiler_params=_cparams("parallel", "parallel"),
    )(o, do)


def _attn_bwd(qa, ka, vb, do, lse, delta, tq):
    s = qa.shape[0]
    nq = s // tq
    w = 2 * HEAD_DIM

    def body(q_ref, do_ref, lse_ref, dl_ref, k_ref, v_ref, dq_ref, dk_ref, dv_ref, dc_ref):
        j = pl.program_id(1)
        first = _head_lanes()

        @pl.when(j == 0)
        def _():
            dq_ref[...] = jnp.zeros_like(dq_ref)

        k2 = k_ref[...]
        v2 = v_ref[...]

        def step(i, carry, masked):
            off = pl.multiple_of(i * tq, tq)
            rows = pl.ds(off, tq)
            q2 = q_ref[rows, :]
            do2 = do_ref[rows, :]
            new, dqs = [], []
            for hh in range(2):
                dk, dv, dcs = carry[hh]
                qh = q2[:, hh * w:(hh + 1) * w]
                kh = k2[:, hh * w:(hh + 1) * w]
                sc = lax.dot_general(qh, kh, NT, preferred_element_type=F32)
                if masked:
                    sc = jnp.where(_causal_mask(tq), sc, NEG_INF)
                p = jnp.exp(sc - lse_ref[hh, rows, :])
                dv = dv + lax.dot_general(p.astype(BF16), do2, TN, preferred_element_type=F32)
                dp = lax.dot_general(_pick(do2, first, hh), v2, NT, preferred_element_type=F32)
                ds = p * (dp - dl_ref[hh, rows, :])
                dsb = ds.astype(BF16)
                dk = dk + lax.dot_general(dsb, qh, TN, preferred_element_type=F32)
                dqs.append(jnp.dot(dsb, kh, preferred_element_type=F32))
                new.append((dk, dv, dcs + jnp.sum(ds, axis=0, keepdims=True)))
            dq_ref[rows, :] += jnp.where(first, dqs[0], pltpu.roll(dqs[1], HEAD_DIM, 1)) * ATT_SCALE
            return tuple(new)

        one = (jnp.zeros((tq, w), F32), jnp.zeros((tq, w), F32), jnp.zeros((1, tq), F32))
        carry = step(j, (one, one), True)
        (dk0, dv0, dc0), (dk1, dv1, dc1) = lax.fori_loop(j + 1, nq, lambda i, c: step(i, c, False), carry)
        dk_ref[...] = jnp.where(first, dk0, pltpu.roll(dk1, HEAD_DIM, 1))
        dv_ref[...] = jnp.where(first, dv0, dv1)
        dc_ref[0, 0] = -dc0
        dc_ref[1, 0] = -dc1

    whole = lambda width: pl.BlockSpec((s, width), lambda hp, j: (0, hp))
    whole_heads = pl.BlockSpec((2, s, 1), lambda hp, j: (hp, 0, 0))
    blk = lambda width: pl.BlockSpec((tq, width), lambda hp, j: (j, hp))
    crow = pl.BlockSpec((2, 1, 1, tq), lambda hp, j: (hp, j, 0, 0))
    return pl.pallas_call(
        body, name="attn_bwd",
        out_shape=(jax.ShapeDtypeStruct((s, D_ATT), F32), jax.ShapeDtypeStruct((s, D_ATT), F32),
                   jax.ShapeDtypeStruct((s, D_ATT), F32), jax.ShapeDtypeStruct((N_HEADS, nq, 1, tq), F32)),
        grid=(N_PAIRS, nq),
        in_specs=[whole(2 * w), whole(w), whole_heads, whole_heads, blk(2 * w), blk(w)],
        out_specs=(whole(w), blk(w), blk(w), crow),
        compiler_params=_cparams("parallel", "arbitrary"),
    )(qa, do, lse, delta, ka, vb)


def _pair_sums(x, first):
    total = jnp.sum(x, axis=-1, keepdims=True)
    head = jnp.sum(jnp.where(first, x, 0.0), axis=-1, keepdims=True)
    return head, total - head


def _pair_mean(x, first):
    head, tail = _pair_sums(x, first)
    return jnp.where(first, head, tail) * (1.0 / GROUP_DIM)


def _gm_pair_norm(v2, first):
    d = v2 - _pair_mean(v2, first)
    rstd = lax.rsqrt(_pair_mean(d * d, first) + EPS)
    return d * rstd, rstd


def _gm_pair_mix(w_ref, pr, rhs, first):
    return jnp.where(first, jnp.dot(w_ref[2 * pr], rhs, preferred_element_type=F32),
                     jnp.dot(w_ref[2 * pr + 1], rhs, preferred_element_type=F32))


def _gmlp_fwd(z, wt, bs_t, vgain):
    s = z.shape[0]

    def body(gu_ref, gv_ref, wt_ref, bs_ref, vg_ref, o_ref):
        first = _head_lanes()
        for pr in range(N_GROUPS // 2):
            sl = slice(2 * pr * GROUP_DIM, 2 * (pr + 1) * GROUP_DIM)
            vhat, _ = _gm_pair_norm(_gelu(gv_ref[:, sl]), first)
            vn = (vhat * vg_ref[:, sl]).astype(BF16)
            bias = jnp.where(first, bs_ref[:, 2 * pr:2 * pr + 1], bs_ref[:, 2 * pr + 1:2 * pr + 2])
            o_ref[:, sl] = _gelu(gu_ref[:, sl]) * (_gm_pair_mix(wt_ref, pr, vn, first) + bias)

    full = lambda a: pl.BlockSpec(a.shape, lambda n: (0,) * a.ndim)
    return pl.pallas_call(
        body, name="gmlp_fwd", out_shape=jax.ShapeDtypeStruct((s, D_GM), F32), grid=(s // CHUNK,),
        in_specs=[pl.BlockSpec((CHUNK, D_GM), lambda n: (n, 3)), pl.BlockSpec((CHUNK, D_GM), lambda n: (n, 4)),
                  full(wt), full(bs_t), full(vgain)],
        out_specs=pl.BlockSpec((CHUNK, D_GM), lambda n: (n, 0)),
        compiler_params=_cparams("parallel"),
    )(z, z, wt, bs_t, vgain)


def _gmlp_bwd(z, dgm, wt, wt_t, bs_t, vgain):
    s = z.shape[0]

    def body(gu_ref, gv_ref, dgm_ref, wt_ref, wtt_ref, bs_ref, vg_ref, dgu_ref, dgv_ref, dwt_ref, dbs_ref, dvg_ref):
        @pl.when(pl.program_id(0) == 0)
        def _():
            dwt_ref[...] = jnp.zeros_like(dwt_ref)
            dbs_ref[...] = jnp.zeros_like(dbs_ref)
            dvg_ref[...] = jnp.zeros_like(dvg_ref)

        first = _head_lanes()
        for pr in range(N_GROUPS // 2):
            g0, g1 = 2 * pr, 2 * pr + 1
            sl = slice(g0 * GROUP_DIM, (g1 + 1) * GROUP_DIM)
            gu = gu_ref[:, sl]
            gv = gv_ref[:, sl]
            dgm = dgm_ref[:, sl]
            vhat, rstd = _gm_pair_norm(_gelu(gv), first)
            gain = vg_ref[:, sl]
            vn = (vhat * gain).astype(BF16)
            bias = jnp.where(first, bs_ref[:, g0:g0 + 1], bs_ref[:, g1:g1 + 1])
            mixed = _gm_pair_mix(wt_ref, pr, vn, first) + bias
            dgu_ref[:, sl] = (dgm * mixed * _gelu_grad(gu)).astype(BF16)
            dmixed = dgm * _gelu(gu)
            db0, db1 = _pair_sums(dmixed, first)
            dbs_ref[:, g0:g0 + 1] += db0
            dbs_ref[:, g1:g1 + 1] += db1
            dwt_ref[g0] += lax.dot_general(_pick(dmixed, first, 0).astype(BF16), vn, NT, preferred_element_type=F32)
            dwt_ref[g1] += lax.dot_general(_pick(dmixed, first, 1).astype(BF16), vn, NT, preferred_element_type=F32)
            dvn = _gm_pair_mix(wtt_ref, pr, dmixed.astype(BF16), first)
            dvg_ref[:, sl] += _colsum(dvn * vhat)
            dvhat = dvn * gain
            dvf = rstd * (dvhat - _pair_mean(dvhat, first) - vhat * _pair_mean(dvhat * vhat, first))
            dgv_ref[:, sl] = (dvf * _gelu_grad(gv)).astype(BF16)

    full = lambda a: pl.BlockSpec(a.shape, lambda n: (0,) * a.ndim)
    chunk = pl.BlockSpec((CHUNK, D_GM), lambda n: (n, 0))
    return pl.pallas_call(
        body, name="gmlp_bwd",
        out_shape=(jax.ShapeDtypeStruct((s, D_GM), BF16), jax.ShapeDtypeStruct((s, D_GM), BF16),
                   jax.ShapeDtypeStruct(wt.shape, F32), jax.ShapeDtypeStruct(bs_t.shape, F32),
                   jax.ShapeDtypeStruct(vgain.shape, F32)),
        grid=(s // CHUNK,),
        in_specs=[pl.BlockSpec((CHUNK, D_GM), lambda n: (n, 3)), pl.BlockSpec((CHUNK, D_GM), lambda n: (n, 4)),
                  chunk, full(wt), full(wt_t), full(bs_t), full(vgain)],
        out_specs=(chunk, chunk, full(wt), full(bs_t), full(vgain)),
        compiler_params=_cparams("arbitrary"),
    )(z, z, dgm, wt, wt_t, bs_t, vgain)


def _layer_fwd(h0, p_i, w, tq, late):
    s, d = h0.shape
    nq = s // tq
    sv = {"h0": h0}

    (hn1,) = _rowwise(lambda h, g: _rms(h)[0] * g, [h0], [w["mix_pre_norm"]], [(d, BF16)], [], "pre_mix", 256)
    z = _mm(hn1, w["w_in"], "nn", F32, "mm_in")
    fl_t = z[:, F_OFF:F_OFF + N_HEADS].T
    c_t = _forget_fwd(fl_t, w["b_forget"])
    qa, ka, vb = _attn_prep(z, c_t.T, 256)
    att, lse = _attn_fwd(qa, ka, vb, tq)
    gm = _gmlp_fwd(z, w["wt"], w["bs_t"], w["gm_v_norm"])
    w = dict(w, **late(att))

    def mix_out(att, gm, g):
        return jnp.concatenate([_rms(att)[0] * g[:, :D_ATT], _rms(gm)[0] * g[:, D_ATT:]], axis=-1)

    (mc,) = _rowwise(mix_out, [att, gm], [w["mix_out_norm"]], [(D_ATT + D_GM, BF16)], [], "mix_out", 256)
    y1 = _mm(mc, w["w_out"], "nn", F32, "mm_out")

    def post_mix(h0, y1, gpost, gpre):
        h1 = h0 + _rms(y1)[0] * gpost
        return h1, _rms(h1)[0] * gpre

    h1, hn2 = _rowwise(post_mix, [h0, y1], [w["mix_post_norm"], w["ffn_pre_norm"]],
                       [(d, F32), (d, BF16)], [], "post_mix", 256)
    ab, t = _ffn_in_swiglu(hn2, w["w_ffn_in"])
    y2 = _mm(t, w["w_ffn_out"], "nn", F32, "mm_ffn_out", a3="k", b3="k")

    def post_ffn(h1, y2, g):
        h2 = h1 + _rms(y2)[0] * g
        return h2, _rms(h2)[0]

    h2, hr = _rowwise(post_ffn, [h1, y2], [w["ffn_post_norm"]], [(d, F32), (d, BF16)], [], "post_ffn", 256)
    gl = _mm(hr, w["w_ple_gate"], "nn", F32, "mm_gate")
    pe = _mm(p_i, w["w_ple"], "nn", F32, "mm_ple")
    (h3,) = _rowwise(lambda h2, gl, pe, g: h2 + _sigmoid(gl) * (_rms(pe)[0] * g), [h2, gl, pe], [w["ple_norm"]],
                     [(d, F32)], [], "ple_out", 256)
    sv.update(hn1=hn1, z=z, fl_t=fl_t, qa=qa, ka=ka, vb=vb, lse=lse, att=att, gm=gm,
              mc=mc, y1=y1, h1=h1, hn2=hn2, ab=ab, y2=y2, h2=h2, hr=hr, gl=gl, pe=pe, p_i=p_i)
    return h3, sv


def _layer_bwd(dh3, sv, w, tq, mid):
    s, d = dh3.shape
    g = {}
    by_rows = lambda a: a.reshape(N_DEV, -1, a.shape[-1])
    by_cols = lambda a: jnp.stack(jnp.split(a, N_DEV, axis=-1))

    def ple_bwd(dh3, gl, pe, gple):
        gate = _sigmoid(gl)
        pehat, rpe = _rms(pe)
        dgl = dh3 * (pehat * gple) * gate * (1.0 - gate)
        de = dh3 * gate
        return dgl, _rms_bwd(pehat, rpe, de * gple), _colsum(de * pehat)

    dgl, dpe, g["ple_norm"] = _rowwise(ple_bwd, [dh3, sv["gl"], sv["pe"]], [w["ple_norm"]],
                                       [(d, BF16), (d, BF16)], [d], "ple_bwd", 256)
    g["w_ple_gate"] = by_rows(_mm(sv["hr"], dgl, "tn", BF16, "mm_dgate"))
    dhr = _mm(dgl, w["w_ple_gate"], "nt", F32, "mm_dhr")
    g["w_ple"] = by_cols(_mm(sv["p_i"], dpe, "tn", BF16, "mm_dple"))

    def ffn_post_bwd(dh3, dhr, h2, y2, gpost):
        h2hat, r2 = _rms(h2)
        dh2 = dh3 + _rms_bwd(h2hat, r2, dhr)
        y2hat, ry = _rms(y2)
        return dh2, _rms_bwd(y2hat, ry, dh2 * gpost), _colsum(dh2 * y2hat)

    dh2, dy2, g["ffn_post_norm"] = _rowwise(ffn_post_bwd, [dh3, dhr, sv["h2"], sv["y2"]], [w["ffn_post_norm"]],
                                            [(d, F32), (d, BF16)], [d], "ffn_post_bwd", 256)
    dt = _mm(dy2, w["w_ffn_out"], "nt", F32, "mm_dt", b3="n", o3="n")
    t, dab = _swiglu_bwd(sv["ab"], dt, 256)
    dab = dab.reshape((N_DEV,) + dab.shape[2:])
    g["w_ffn_out"] = by_rows(_mm(t, dy2, "tn", BF16, "mm_dffn_out", a3="m", o3="m"))
    dhn2 = _mm(dab, w["w_ffn_in"], "nt", F32, "mm_dhn2", a3="k", b3="k")
    g["w_ffn_in"] = _mm(sv["hn2"], dab, "tn", BF16, "mm_dffn_in", b3="n", o3="n")

    def mix_post_bwd(dh2, dhn2, h1, y1, gpre, gpost):
        h1hat, r1 = _rms(h1)
        dh1 = dh2 + _rms_bwd(h1hat, r1, dhn2 * gpre)
        y1hat, ry = _rms(y1)
        return dh1, _rms_bwd(y1hat, ry, dh1 * gpost), _colsum(dhn2 * h1hat), _colsum(dh1 * y1hat)

    dh1, dy1, g["ffn_pre_norm"], g["mix_post_norm"] = _rowwise(
        mix_post_bwd, [dh2, dhn2, sv["h1"], sv["y1"]], [w["ffn_pre_norm"], w["mix_post_norm"]],
        [(d, F32), (d, BF16)], [d, d], "mix_post_bwd", 256)
    dmc = _mm(dy1, w["w_out"], "nt", F32, "mm_dmc")
    g["w_out"] = by_rows(_mm(sv["mc"], dy1, "tn", BF16, "mm_dout"))
    w = dict(w, **mid(g, dmc))

    def mix_out_bwd(da, dg, att, gm, gain):
        atthat, ra = _rms(att)
        gmhat, rg = _rms(gm)
        dgain = jnp.concatenate([_colsum(da * atthat), _colsum(dg * gmhat)], axis=-1)
        return _rms_bwd(atthat, ra, da * gain[:, :D_ATT]), _rms_bwd(gmhat, rg, dg * gain[:, D_ATT:]), dgain

    datt, dgm, g["mix_out_norm"] = _rowwise(
        mix_out_bwd, [(dmc, 0, D_ATT), (dmc, 1, D_GM), sv["att"], sv["gm"]], [w["mix_out_norm"]],
        [(D_ATT, BF16), (D_GM, F32)], [D_ATT + D_GM], "mix_out_bwd", 256)

    dgu, dgv, dwt, dbs_t, g["gm_v_norm"] = _gmlp_bwd(sv["z"], dgm, w["wt"], w["wt_t"], w["bs_t"], w["gm_v_norm"])
    g["gm_w_s"] = dwt * jnp.tril(jnp.ones((CHUNK, CHUNK), F32))[None]
    g["gm_b_s"] = dbs_t.T

    delta = _attn_delta(sv["att"], datt, tq)
    dq, dk, dv, dc_row = _attn_bwd(sv["qa"], sv["ka"], sv["vb"], datt, sv["lse"], delta, tq)
    dfl_t, db = _forget_bwd(dc_row.reshape(N_HEADS, s), sv["fl_t"], w["b_forget"])
    g["b_forget"] = db.reshape(1, N_HEADS)
    dz = jnp.concatenate([dq.astype(BF16), dk.astype(BF16), dv.astype(BF16), dgu, dgv, dfl_t.T.astype(BF16),
                          jnp.zeros((s, D_IN_PAD - F_OFF - N_HEADS), BF16)], axis=-1)
    dhn1 = _mm(dz, w["w_in"], "nt", F32, "mm_dhn1")
    din = _mm(sv["hn1"], dz, "tn", BF16, "mm_din")
    din = jnp.concatenate([din[:, :3 * D_ATT], din[:, F_OFF:F_OFF + N_HEADS], din[:, 3 * D_ATT:F_OFF]], axis=-1)
    g["w_in"] = by_cols(din)

    def mix_pre_bwd(dh1, dhn1, h0, gpre):
        h0hat, r0 = _rms(h0)
        return dh1 + _rms_bwd(h0hat, r0, dhn1 * gpre), _colsum(dhn1 * h0hat)

    dh0, g["mix_pre_norm"] = _rowwise(mix_pre_bwd, [dh1, dhn1, sv["h0"]], [w["mix_pre_norm"]],
                                      [(d, F32)], [d], "mix_pre_bwd", 256)
    return dh0, g


ANY = pl.BlockSpec(memory_space=pl.ANY)


def _all_gather(xs, layer, name):
    n = len(xs)

    def body(*refs):
        x_refs, out_refs = refs[:n], refs[n:2 * n]
        send_sems, recv_sems, local_sems = refs[2 * n:]
        x, y, c = lax.axis_index("x"), lax.axis_index("y"), lax.axis_index("c")
        me, sibling = (x, y, c), (x, y, 1 - c)
        chips = [(1 - x, y), (x, 1 - y), (1 - x, 1 - y)]

        def shard(a):
            return x_refs[a] if layer is None else x_refs[a].at[layer]

        def rows(a, px, py, pc):
            return out_refs[a].at[4 * px + 2 * py + pc]

        def copy(a, kk, block, to, from_shard=False):
            return pltpu.make_async_remote_copy(
                src_ref=shard(a) if from_shard else rows(a, *block), dst_ref=rows(a, *block),
                send_sem=send_sems.at[7 * a + kk], recv_sem=recv_sems.at[7 * a + kk],
                device_id=to, device_id_type=MESH)

        mine = [pltpu.make_async_copy(shard(a), rows(a, *me), local_sems.at[a]) for a in range(n)]
        for cp in mine:
            cp.start()
        first = []
        for a in range(n):
            first.append(copy(a, 0, me, sibling, from_shard=True))
            first += [copy(a, 1 + j, me, (*chip, c), from_shard=True) for j, chip in enumerate(chips)]
        for cp in first:
            cp.start()
        passed = []
        for j, chip in enumerate(chips):
            for a in range(n):
                copy(a, 1 + j, (*chip, c), me).wait_recv()
                passed.append(copy(a, 4 + j, (*chip, c), sibling))
                passed[-1].start()
        for a in range(n):
            copy(a, 0, sibling, me).wait_recv()
        for j, chip in enumerate(chips):
            for a in range(n):
                copy(a, 4 + j, (*chip, 1 - c), me).wait_recv()
        for cp in first + passed:
            cp.wait_send()
        for cp in mine:
            cp.wait()

    shapes = [x.shape if layer is None else x.shape[1:] for x in xs]
    return pl.pallas_call(
        body, name=name, out_shape=[jax.ShapeDtypeStruct((N_DEV,) + sh, x.dtype) for sh, x in zip(shapes, xs)],
        in_specs=[ANY] * n, out_specs=[ANY] * n,
        scratch_shapes=[pltpu.SemaphoreType.DMA((7 * n,)), pltpu.SemaphoreType.DMA((7 * n,)),
                        pltpu.SemaphoreType.DMA((n,))],
    )(*xs)


HBM = pl.BlockSpec(memory_space=pltpu.HBM)
SEMS = pl.BlockSpec(memory_space=pltpu.SEMAPHORE)
EFFECT = pltpu.SideEffectType.DATAFLOW_SIDE_EFFECTING
FLIPS = tuple((fx, fy, fc) for fx in (0, 1) for fy in (0, 1) for fc in (0, 1))[1:]


def _exchange_copies(src_refs, land_refs, send_sems, recv_sems, layer, scatter):
    x, y, c = lax.axis_index("x"), lax.axis_index("y"), lax.axis_index("c")
    me = 4 * x + 2 * y + c
    copies = []
    for a, (src, land) in enumerate(zip(src_refs, land_refs)):
        for f, (fx, fy, fc) in enumerate(FLIPS):
            px, py, pc = (1 - x if fx else x), (1 - y if fy else y), (1 - c if fc else c)
            if scatter:
                block = src.at[4 * px + 2 * py + pc]
            else:
                block = src if layer is None else src.at[layer]
            copies.append(pltpu.make_async_remote_copy(
                src_ref=block, dst_ref=land.at[me], send_sem=send_sems.at[7 * a + f], recv_sem=recv_sems.at[7 * a + f],
                device_id=(px, py, pc), device_id_type=MESH))
    return copies


def _exchange_start(srcs, lands, layer, scatter, name):
    n = len(srcs)

    def body(*refs):
        for cp in _exchange_copies(refs[:n], refs[n:2 * n], refs[2 * n], refs[2 * n + 1], layer, scatter):
            cp.start()
        token = refs[-1]
        token[...] = jnp.zeros_like(token)

    operands = list(srcs) + list(lands)
    outs = pl.pallas_call(
        body, name=name,
        out_shape=(pltpu.SemaphoreType.DMA((7 * n,)), pltpu.SemaphoreType.DMA((7 * n,)),
                   *[pltpu.HBM(a.shape, a.dtype) for a in operands], jax.ShapeDtypeStruct((8, LANE), F32)),
        in_specs=[HBM] * (2 * n),
        out_specs=(SEMS, SEMS, *[HBM] * (2 * n), pl.BlockSpec(memory_space=pltpu.VMEM)),
        input_output_aliases={i: 2 + i for i in range(2 * n)},
        compiler_params=pltpu.CompilerParams(has_side_effects=EFFECT),
    )(*[pltpu.with_memory_space_constraint(a, pltpu.HBM) for a in operands])
    return outs[0], outs[1], outs[2:2 + n], outs[2 + n:2 + 2 * n], outs[-1]


def _exchange_wait(started, after, layer, scatter, name):
    send_sems, recv_sems, srcs, lands, _ = started
    n = len(srcs)

    def body(*refs):
        for cp in _exchange_copies(refs[:n], refs[n:2 * n], refs[2 * n], refs[2 * n + 1], layer, scatter):
            cp.wait_send()
            cp.wait_recv()

    operands = list(srcs) + list(lands)
    outs = pl.pallas_call(
        body, name=name, out_shape=tuple(pltpu.HBM(a.shape, a.dtype) for a in operands),
        in_specs=[HBM] * (2 * n) + [SEMS, SEMS, ANY], out_specs=[HBM] * (2 * n),
        input_output_aliases={i: i for i in range(2 * n)},
        compiler_params=pltpu.CompilerParams(has_side_effects=EFFECT),
    )(*operands, send_sems, recv_sems, after)
    return outs[:n], outs[n:]


def _sum_devices(parts):
    _, r, c = parts.shape

    def body(p_ref, o_ref):
        acc = p_ref[0]
        for j in range(1, N_DEV):
            acc = acc + p_ref[j]
        o_ref[...] = acc

    return pl.pallas_call(
        body, name="small_sum", out_shape=jax.ShapeDtypeStruct((r, c), F32), grid=(r // SMALL_ROWS,),
        in_specs=[pl.BlockSpec((N_DEV, SMALL_ROWS, c), lambda i: (0, i, 0))],
        out_specs=pl.BlockSpec((SMALL_ROWS, c), lambda i: (i, 0)),
        compiler_params=_cparams("parallel"),
    )(parts)


def _adamw_math(w, g, m, v):
    m = ADAM_B1 * m + (1.0 - ADAM_B1) * g
    v = ADAM_B2 * v + (1.0 - ADAM_B2) * (g * g)
    m_hat = m / (1.0 - ADAM_B1 ** ADAM_STEP)
    v_hat = v / (1.0 - ADAM_B2 ** ADAM_STEP)
    return -ADAM_LR * (m_hat / (jnp.sqrt(v_hat) + ADAM_EPS) + ADAM_WD * w), m, v


def _adamw_shard(w, m, v, parts, layer, name):
    _, a, b = w.shape
    ta = _tile(a, 256, 16)

    def body(w_ref, m_ref, v_ref, p_ref, g_ref, d_ref, nm_ref, nv_ref):
        g = p_ref[0].astype(F32)
        for j in range(1, N_DEV):
            g = g + p_ref[j].astype(F32)
        g_ref[...] = g
        d_ref[...], nm_ref[...], nv_ref[...] = _adamw_math(w_ref[0], g, m_ref[0], v_ref[0])

    mine = pl.BlockSpec((1, ta, b), lambda i: (layer, i, 0))
    out = pl.BlockSpec((ta, b), lambda i: (i, 0))
    return pl.pallas_call(
        body, name=name, out_shape=[jax.ShapeDtypeStruct((a, b), F32)] * 4, grid=(a // ta,),
        in_specs=[mine, mine, mine, pl.BlockSpec((N_DEV, ta, b), lambda i: (0, i, 0))], out_specs=[out] * 4,
        compiler_params=_cparams("parallel"),
    )(w, m, v, parts)


def _pack_small(pieces):
    flat = jnp.concatenate([p.reshape(-1) for p in pieces])
    total = -(-flat.shape[0] // (SMALL_COLS * SMALL_ROWS)) * SMALL_COLS * SMALL_ROWS
    return jnp.pad(flat, (0, total - flat.shape[0])).reshape(-1, SMALL_COLS)


def kernel(x, p, mix_pre_norm, mix_post_norm, w_in, b_forget, gm_v_norm, gm_w_s, gm_b_s, mix_out_norm, w_out, ffn_pre_norm, ffn_post_norm, w_ffn_in, w_ffn_out, w_ple, ple_norm, w_ple_gate, loss_target, m_mix_pre_norm, m_mix_post_norm, m_w_in, m_b_forget, m_gm_v_norm, m_gm_w_s, m_gm_b_s, m_mix_out_norm, m_w_out, m_ffn_pre_norm, m_ffn_post_norm, m_w_ffn_in, m_w_ffn_out, m_w_ple, m_ple_norm, m_w_ple_gate, v_mix_pre_norm, v_mix_post_norm, v_w_in, v_b_forget, v_gm_v_norm, v_gm_w_s, v_gm_b_s, v_mix_out_norm, v_w_out, v_ffn_pre_norm, v_ffn_post_norm, v_w_ffn_in, v_w_ffn_out, v_w_ple, v_ple_norm, v_w_ple_gate):
    given = dict(locals())
    weights = {n: given[n] for n in WEIGHT_ORDER}
    mom_m = {n: given["m_" + n] for n in WEIGHT_ORDER}
    mom_v = {n: given["v_" + n] for n in WEIGHT_ORDER}
    depth = w_in.shape[0]
    s, d = x.shape[1], x.shape[2]
    tq = _tile(s, ATT_BLOCK)
    me = 4 * lax.axis_index("x") + 2 * lax.axis_index("y") + lax.axis_index("c")
    tril = jnp.tril(jnp.ones((CHUNK, CHUNK), F32))

    def landing(block):
        return lax.dynamic_update_index_in_dim(lax.empty((N_DEV,) + block.shape, block.dtype), block, me, 0)

    def mix_weights(i, got):
        w_in_full = jnp.concatenate([got["w_in"][j] for j in range(N_DEV)], axis=-1)
        pad = jnp.zeros((d, D_IN_PAD - D_IN), BF16)
        wt = gm_w_s[i] * tril[None]
        lw = dict(
            w_in=jnp.concatenate([w_in_full[:, :3 * D_ATT], w_in_full[:, 3 * D_ATT + N_HEADS:],
                                  w_in_full[:, 3 * D_ATT:3 * D_ATT + N_HEADS], pad], axis=-1),
            b_forget=b_forget[i][:, None], wt=wt.astype(BF16), wt_t=wt.transpose(0, 2, 1).astype(BF16),
            bs_t=gm_b_s[i].T)
        lw.update({n: weights[n][i][None] for n in ("mix_pre_norm", "mix_post_norm", "gm_v_norm", "mix_out_norm",
                                                    "ffn_pre_norm", "ffn_post_norm", "ple_norm")})
        return lw

    def rest_weights(got):
        return dict(w_out=got["w_out"].reshape(-1, d), w_ffn_in=got["w_ffn_in"],
                    w_ffn_out=got["w_ffn_out"].reshape(N_DEV // 2, -1, d),
                    w_ple=jnp.concatenate([got["w_ple"][j] for j in range(N_DEV)], axis=-1),
                    w_ple_gate=got["w_ple_gate"].reshape(-1, d))

    shards = {n: weights[n].astype(BF16) for n in MATRIX_WEIGHTS}

    def gather_start(i):
        started = {}
        order = jnp.zeros((), BF16)
        for tag, grp in EXCHANGE_GROUPS.items():
            started[tag] = _exchange_start([shards[n] for n in grp], [landing(shards[n][i] + order) for n in grp], i,
                                           False, f"weights_gather_start_{i}_{tag}")
            order = started[tag][4][0, 0].astype(BF16)
        return started

    def gather_finish(i, tag, pending, after):
        srcs, got = _exchange_wait(pending[tag], after, i, False, f"weights_gather_wait_{i}_{tag}")
        shards.update(zip(EXCHANGE_GROUPS[tag], srcs))
        return dict(zip(EXCHANGE_GROUPS[tag], got))

    h = x[0]
    saved, layer_w = [], []
    pending = gather_start(0)
    for i in range(depth):
        lw = mix_weights(i, gather_finish(i, "mix", pending, h))
        if i == 0:
            lw["mix_pre_norm"] = lw["mix_pre_norm"] + pending["rest"][4][:1, :1]
        following = {}

        def late(att, i=i, pending=pending, lw=lw, following=following):
            rest = rest_weights(gather_finish(i, "rest", pending, att))
            lw.update(rest)
            if i + 1 == depth:
                return rest
            following.update(gather_start(i + 1))
            token = following["mix"][4][:1, :1] + following["rest"][4][:1, :1]
            return dict(rest, mix_out_norm=lw["mix_out_norm"] + token)

        h, sv = _layer_fwd(h, p[i, 0], lw, tq, late)
        layer_w.append(lw)
        saved.append(sv)
        pending = following

    def loss_head(y, t):
        err = y - t
        return err * (1.0 / d), _colsum(err * err)

    dh, sq = _rowwise(loss_head, [h, loss_target[0]], [], [(d, F32)], [d], "loss_head", 256)
    loss = lax.psum(0.5 * jnp.sum(sq) / d, AXES)

    layer_g = [None] * depth
    shard_out = {n: [None] * depth for n in MATRIX_WEIGHTS}

    def scatter_start(i, tag, g):
        full_g = [g[n] for n in EXCHANGE_GROUPS[tag]]
        lands = [landing(lax.dynamic_index_in_dim(gf, me, 0, keepdims=False)) for gf in full_g]
        return _exchange_start(full_g, lands, None, True, f"grads_scatter_start_{i}_{tag}")

    def scatter_finish(i, tag, started, after):
        _, parts = _exchange_wait(started[tag], after, None, True, f"grads_scatter_wait_{i}_{tag}")
        for n, part in zip(EXCHANGE_GROUPS[tag], parts):
            shard_out[n][i] = _adamw_shard(weights[n], mom_m[n], mom_v[n], part, i, "adamw_" + n)

    before = None
    for i in reversed(range(depth)):
        lw = layer_w[i]
        if before is not None:
            lw = dict(lw, ple_norm=lw["ple_norm"] + before[1]["mix"][4][:1, :1])
        started = {}

        def mid(g, dmc, i=i, before=before, started=started, lw=lw):
            if before is not None:
                scatter_finish(before[0], "rest", before[1], dmc)
            started["rest"] = scatter_start(i, "rest", g)
            return dict(mix_out_norm=lw["mix_out_norm"] + started["rest"][4][:1, :1])

        dh, layer_g[i] = _layer_bwd(dh, saved[i], lw, tq, mid)
        if before is not None:
            scatter_finish(before[0], "mix", before[1], dh)
        started["mix"] = scatter_start(i, "mix", layer_g[i])
        before = (i, started)
    scatter_finish(before[0], "rest", before[1], before[1]["mix"][4])
    scatter_finish(before[0], "mix", before[1], dh)
    grad_x = dh[None]

    grads, deltas, new_m, new_v = {}, {}, {}, {}
    for n in MATRIX_WEIGHTS:
        grads[n], deltas[n], new_m[n], new_v[n] = (jnp.stack([shard_out[n][i][k] for i in range(depth)])
                                                   for k in range(4))

    small_g = _pack_small([jnp.stack([layer_g[i][n].reshape(-1) for i in range(depth)]) for n in SMALL_WEIGHTS])
    (gathered,) = _all_gather([small_g], None, "small_grads_all_gather")
    g_small = _sum_devices(gathered)
    pack = lambda t: _pack_small([t[n] for n in SMALL_WEIGHTS])
    dl, nm, nv = _rowwise(_adamw_math, [pack(weights), g_small, pack(mom_m), pack(mom_v)], [],
                          [(SMALL_COLS, F32)] * 3, [], "adamw_small", SMALL_ROWS)
    off = 0
    for n in SMALL_WEIGHTS:
        shp, size = weights[n].shape, weights[n].size
        grads[n], deltas[n], new_m[n], new_v[n] = (a.reshape(-1)[off:off + size].reshape(shp)
                                                   for a in (g_small, dl, nm, nv))
        off += size

    return (loss, grad_x, *[grads[n] for n in WEIGHT_ORDER], *[deltas[n] for n in WEIGHT_ORDER],
            *[new_m[n] for n in WEIGHT_ORDER], *[new_v[n] for n in WEIGHT_ORDER])
```

```python
import functools
import math

import jax
import jax.numpy as jnp
from jax import lax
from jax.experimental import pallas as pl
from jax.experimental.pallas import tpu as pltpu

F32 = jnp.float32
BF16 = jnp.bfloat16
MESH = pl.DeviceIdType.MESH
AXES = ("x", "y", "c")
N_DEV = 8

EPS = 1e-6
NEG_INF = -1e30
N_HEADS = 8
HEAD_DIM = 64
D_ATT = N_HEADS * HEAD_DIM
N_GROUPS = 8
GROUP_DIM = 64
D_GM = N_GROUPS * GROUP_DIM
CHUNK = 128
ATT_SCALE = HEAD_DIM ** -0.5
ATT_BLOCK = 1024
D_IN = 3 * D_ATT + N_HEADS + 2 * D_GM
D_IN_PAD = 3 * D_ATT + 2 * D_GM + 128
F_OFF = 3 * D_ATT + 2 * D_GM

ADAM_LR = 0.001
ADAM_B1 = 0.9
ADAM_B2 = 0.999
ADAM_EPS = 1e-08
ADAM_WD = 0.01
ADAM_STEP = 10

LANE = 128
VMEM_LIMIT = 48 * 1024 * 1024
SMALL_COLS = 128
SMALL_ROWS = 512

MATRIX_WEIGHTS = ("w_in", "w_out", "w_ffn_in", "w_ffn_out", "w_ple", "w_ple_gate")
EXCHANGE_GROUPS = {"mix": ("w_in",), "rest": ("w_out", "w_ffn_in", "w_ffn_out", "w_ple", "w_ple_gate")}
SMALL_WEIGHTS = ("mix_pre_norm", "mix_post_norm", "b_forget", "gm_v_norm", "gm_w_s", "gm_b_s",
                 "mix_out_norm", "ffn_pre_norm", "ffn_post_norm", "ple_norm")
WEIGHT_ORDER = ("mix_pre_norm", "mix_post_norm", "w_in", "b_forget", "gm_v_norm", "gm_w_s", "gm_b_s",
                "mix_out_norm", "w_out", "ffn_pre_norm", "ffn_post_norm", "w_ffn_in", "w_ffn_out",
                "w_ple", "ple_norm", "w_ple_gate")


def _tile(n, pref, unit=LANE):
    best = None
    t = unit
    while t <= min(n, pref):
        if n % t == 0:
            best = t
        t += unit
    return n if best is None else best


def _cparams(*semantics):
    return pltpu.CompilerParams(dimension_semantics=semantics or None, vmem_limit_bytes=VMEM_LIMIT)


NN = (((1,), (0,)), ((), ()))
NT = (((1,), (1,)), ((), ()))
TN = (((0,), (0,)), ((), ()))
_MM_AXES = {
    "nn": ("i", "k", "k", "j"), "nt": ("i", "k", "j", "k"), "tn": ("k", "i", "k", "j")}
_MM_DN = {"nn": NN, "nt": NT, "tn": TN}


def _mm(a, b, dims, out_dtype, name, a3=None, b3=None, o3=None, tm=1024, tn=1024, tk=1024):
    ar, ac, br, bc = _MM_AXES[dims]
    letter = {"i": "m", "j": "n", "k": "k"}
    size = {}

    def measure(x, rows, cols, stacked):
        shape = x.shape
        if stacked is None:
            size.setdefault(letter[rows], shape[0])
            size.setdefault(letter[cols], shape[1])
        else:
            for ax, n in ((rows, shape[1]), (cols, shape[2])):
                size.setdefault(letter[ax], n * shape[0] if letter[ax] == stacked else n)

    measure(a, ar, ac, a3)
    measure(b, br, bc, b3)
    m, n, k = size["m"], size["n"], size["k"]
    slab = {}
    for x, stacked, rows, cols in ((a, a3, ar, ac), (b, b3, br, bc)):
        if stacked is not None:
            slab[stacked] = x.shape[1] if letter[rows] == stacked else x.shape[2]
    if o3 is not None:
        slab.setdefault(o3, slab.get(o3, None) or {"m": m, "n": n}[o3] // N_DEV)
    tile = {"m": slab.get("m") or _tile(m, tm), "n": slab.get("n") or _tile(n, tn), "k": slab.get("k") or _tile(k, tk)}
    nk = k // tile["k"]

    def spec(rows, cols, stacked):
        tr, tc = tile[letter[rows]], tile[letter[cols]]
        if stacked is None:
            return pl.BlockSpec((tr, tc), lambda i, j, kk: ({"i": i, "j": j, "k": kk}[rows], {"i": i, "j": j, "k": kk}[cols]))

        def imap(i, j, kk):
            g = {"i": i, "j": j, "k": kk}
            return (g[{"m": "i", "n": "j", "k": "k"}[stacked]],
                    0 if letter[rows] == stacked else g[rows], 0 if letter[cols] == stacked else g[cols])

        return pl.BlockSpec((1, tr, tc), imap)

    dn = _MM_DN[dims]

    def body(a_ref, b_ref, o_ref, *acc):
        av = a_ref[...] if a3 is None else a_ref[0]
        bv = b_ref[...] if b3 is None else b_ref[0]
        prod = lax.dot_general(av.astype(BF16), bv.astype(BF16), dn, preferred_element_type=F32)

        def emit(val):
            if o3 is None:
                o_ref[...] = val.astype(out_dtype)
            else:
                o_ref[0] = val.astype(out_dtype)

        if nk == 1:
            emit(prod)
            return
        (acc_ref,) = acc
        kk = pl.program_id(2)

        @pl.when(kk == 0)
        def _():
            acc_ref[...] = prod

        @pl.when(kk > 0)
        def _():
            acc_ref[...] += prod

        @pl.when(kk == nk - 1)
        def _():
            emit(acc_ref[...])

    if o3 is None:
        out_shape = (m, n)
    elif o3 == "m":
        out_shape = (m // tile["m"], tile["m"], n)
    else:
        out_shape = (n // tile["n"], m, tile["n"])
    return pl.pallas_call(
        body, name=name, out_shape=jax.ShapeDtypeStruct(out_shape, out_dtype),
        grid=(m // tile["m"], n // tile["n"], nk),
        in_specs=[spec(ar, ac, a3), spec(br, bc, b3)], out_specs=spec("i", "j", o3),
        scratch_shapes=[] if nk == 1 else [pltpu.VMEM((tile["m"], tile["n"]), F32)],
        compiler_params=_cparams("parallel", "parallel", "arbitrary"),
    )(a, b)


def _rowwise(fn, rows, vecs, outs, reds, name, ts):
    rows = [r if isinstance(r, tuple) else (r, 0, r.shape[1]) for r in rows]
    s = rows[0][0].shape[0]
    ts = _tile(s, ts, 8)
    nr, nv, no = len(rows), len(vecs), len(outs)

    def body(*refs):
        vals = fn(*[r[...] for r in refs[:nr + nv]])
        vals = vals if isinstance(vals, tuple) else (vals,)
        o_refs = refs[nr + nv:nr + nv + no]
        r_refs = refs[nr + nv + no:]
        for o_ref, val in zip(o_refs, vals[:no]):
            o_ref[...] = val.astype(o_ref.dtype)
        if r_refs:
            @pl.when(pl.program_id(0) == 0)
            def _():
                for r_ref in r_refs:
                    r_ref[...] = jnp.zeros_like(r_ref)

            for r_ref, val in zip(r_refs, vals[no:]):
                r_ref[...] += val

    in_specs = [pl.BlockSpec((ts, w), functools.partial(lambda i, cb: (i, cb), cb=cb)) for _, cb, w in rows]
    in_specs += [pl.BlockSpec(v.shape, lambda i: (0, 0)) for v in vecs]
    out_specs = [pl.BlockSpec((ts, c), lambda i: (i, 0)) for c, _ in outs]
    out_specs += [pl.BlockSpec((1, c), lambda i: (0, 0)) for c in reds]
    out_shape = [jax.ShapeDtypeStruct((s, c), dt) for c, dt in outs]
    out_shape += [jax.ShapeDtypeStruct((1, c), F32) for c in reds]
    return pl.pallas_call(
        body, name=name, out_shape=out_shape, grid=(s // ts,), in_specs=in_specs, out_specs=out_specs,
        compiler_params=_cparams("arbitrary" if reds else "parallel"),
    )(*[r[0] for r in rows], *vecs)


def _rms(x):
    r = lax.rsqrt(jnp.mean(x * x, axis=-1, keepdims=True) + EPS)
    return x * r, r


def _rms_bwd(xhat, r, dyg):
    return r * (dyg - xhat * jnp.mean(dyg * xhat, axis=-1, keepdims=True))


def _colsum(x):
    return jnp.sum(x, axis=0, keepdims=True)


def _sigmoid(x):
    return 1.0 / (1.0 + jnp.exp(-x))


GELU_C = math.sqrt(2.0 / math.pi)
GELU_A = 0.044715


def _gelu(x):
    return 0.5 * x * (1.0 + jnp.tanh(GELU_C * (x + GELU_A * x * x * x)))


def _gelu_grad(x):
    t = jnp.tanh(GELU_C * (x + GELU_A * x * x * x))
    return 0.5 * (1.0 + t) + 0.5 * x * (1.0 - t * t) * GELU_C * (1.0 + 3.0 * GELU_A * x * x)


def _ffn_in_swiglu(hn, wg):
    s, d = hn.shape
    g2, _, n = wg.shape
    g = g2 // 2
    tm = _tile(s, 1024)

    def body(h_ref, wa_ref, wb_ref, ab_ref, t_ref):
        hv = h_ref[...]
        a = jnp.dot(hv, wa_ref[0], preferred_element_type=F32)
        b = jnp.dot(hv, wb_ref[0], preferred_element_type=F32)
        ab_ref[0, 0] = a.astype(BF16)
        ab_ref[1, 0] = b.astype(BF16)
        t_ref[0] = (a * _sigmoid(a) * b).astype(BF16)

    return pl.pallas_call(
        body, name="mm_ffn_in_swiglu",
        out_shape=(jax.ShapeDtypeStruct((2, g, s, n), BF16), jax.ShapeDtypeStruct((g, s, n), BF16)),
        grid=(s // tm, g),
        in_specs=[pl.BlockSpec((tm, d), lambda i, j: (i, 0)), pl.BlockSpec((1, d, n), lambda i, j: (j, 0, 0)),
                  pl.BlockSpec((1, d, n), lambda i, j: (j + g, 0, 0))],
        out_specs=(pl.BlockSpec((2, 1, tm, n), lambda i, j: (0, j, i, 0)),
                   pl.BlockSpec((1, tm, n), lambda i, j: (j, i, 0))),
        compiler_params=_cparams("parallel", "parallel"),
    )(hn, wg, wg)


def _swiglu_bwd(ab, dt, ts):
    _, g, s, n = ab.shape
    ts = _tile(s, ts, 8)

    def body(ab_ref, dt_ref, t_ref, dab_ref):
        a = ab_ref[0, 0].astype(F32)
        b = ab_ref[1, 0].astype(F32)
        dt = dt_ref[0]
        sig = _sigmoid(a)
        silu = a * sig
        t_ref[0] = (silu * b).astype(BF16)
        dab_ref[0, 0] = (dt * b * (sig * (1.0 + a * (1.0 - sig)))).astype(BF16)
        dab_ref[1, 0] = (dt * silu).astype(BF16)

    both = pl.BlockSpec((2, 1, ts, n), lambda j, i: (0, j, i, 0))
    one = pl.BlockSpec((1, ts, n), lambda j, i: (j, i, 0))
    return pl.pallas_call(
        body, name="swiglu_bwd",
        out_shape=(jax.ShapeDtypeStruct((g, s, n), BF16), jax.ShapeDtypeStruct((2, g, s, n), BF16)),
        grid=(g, s // ts), in_specs=[both, one], out_specs=(one, both),
        compiler_params=_cparams("parallel", "parallel"),
    )(ab, dt)


def _forget_fwd(fl_t, b_col):
    h, s = fl_t.shape
    nb = s // LANE

    def body(fl_ref, b_ref, c_ref):
        upper = (lax.broadcasted_iota(jnp.int32, (LANE, LANE), 0)
                 <= lax.broadcasted_iota(jnp.int32, (LANE, LANE), 1)).astype(F32)

        def step(i, carry):
            x = fl_ref[i] + b_ref[...]
            lf = jnp.minimum(x, 0.0) - jnp.log(1.0 + jnp.exp(-jnp.abs(x)))
            cs = jnp.dot(lf, upper, precision=lax.Precision.HIGHEST, preferred_element_type=F32) + carry
            c_ref[i] = cs
            return cs[:, LANE - 1:LANE]

        lax.fori_loop(0, nb, step, jnp.zeros((h, 1), F32))

    out = pl.pallas_call(
        body, name="forget_fwd", out_shape=jax.ShapeDtypeStruct((nb, h, LANE), F32),
        compiler_params=_cparams(),
    )(fl_t.reshape(h, nb, LANE).transpose(1, 0, 2), b_col)
    return out.transpose(1, 0, 2).reshape(h, s)


def _forget_bwd(dc_t, fl_t, b_col):
    h, s = fl_t.shape
    nb = s // LANE

    def body(dc_ref, fl_ref, b_ref, dfl_ref, db_ref):
        lower = (lax.broadcasted_iota(jnp.int32, (LANE, LANE), 0)
                 >= lax.broadcasted_iota(jnp.int32, (LANE, LANE), 1)).astype(F32)

        def step(t, carry):
            tail, db = carry
            i = nb - 1 - t
            rc = jnp.dot(dc_ref[i], lower, precision=lax.Precision.HIGHEST, preferred_element_type=F32) + tail
            dfl = rc * (1.0 - _sigmoid(fl_ref[i] + b_ref[...]))
            dfl_ref[i] = dfl
            return rc[:, 0:1], db + jnp.sum(dfl, axis=1, keepdims=True)

        _, db = lax.fori_loop(0, nb, step, (jnp.zeros((h, 1), F32), jnp.zeros((h, 1), F32)))
        db_ref[...] = db

    blocked = lambda a: a.reshape(h, nb, LANE).transpose(1, 0, 2)
    dfl, db = pl.pallas_call(
        body, name="forget_bwd",
        out_shape=(jax.ShapeDtypeStruct((nb, h, LANE), F32), jax.ShapeDtypeStruct((h, 1), F32)),
        compiler_params=_cparams(),
    )(blocked(dc_t), blocked(fl_t), b_col)
    return dfl.transpose(1, 0, 2).reshape(h, s), db


N_PAIRS = N_HEADS // 2


def _causal_mask(t):
    return lax.broadcasted_iota(jnp.int32, (t, t), 0) >= lax.broadcasted_iota(jnp.int32, (t, t), 1)


def _head_lanes():
    return lax.broadcasted_iota(jnp.int32, (1, 2 * HEAD_DIM), 1) < HEAD_DIM


def _pick(x2, first, hh):
    zero = jnp.zeros_like(x2)
    return jnp.where(first, x2, zero) if hh == 0 else jnp.where(first, zero, x2)


BIAS_TERMS = 3


def _attn_prep(z, c, ts):
    s = z.shape[0]
    ts = _tile(s, ts, 16)
    w = 2 * HEAD_DIM

    def body(q_ref, k_ref, v_ref, c_ref, qa_ref, ka_ref, vb_ref):
        lane = lax.broadcasted_iota(jnp.int32, (1, w), 1)
        first = lane < HEAD_DIM
        cv = c_ref[...]
        for h in range(N_HEADS):
            pair = slice((h // 2) * w, (h // 2 + 1) * w)
            qh = q_ref[:, pair] * ATT_SCALE
            kh = k_ref[:, pair]
            if h % 2:
                qh = pltpu.roll(qh, HEAD_DIM, 1)
                kh = pltpu.roll(kh, HEAD_DIM, 1)
            rest = cv[:, h:h + 1]
            q_tail = jnp.zeros((1, w), F32)
            k_tail = jnp.zeros((1, w), F32)
            for t in range(BIAS_TERMS):
                term = rest.astype(BF16).astype(F32)
                rest = rest - term
                q_tail = jnp.where(lane == HEAD_DIM + t, term, jnp.where(lane == HEAD_DIM + BIAS_TERMS + t, 1.0, q_tail))
                k_tail = jnp.where(lane == HEAD_DIM + t, 1.0, jnp.where(lane == HEAD_DIM + BIAS_TERMS + t, -term, k_tail))
            qa_ref[:, h * w:(h + 1) * w] = jnp.where(first, qh, q_tail).astype(BF16)
            ka_ref[:, h * w:(h + 1) * w] = jnp.where(first, kh, k_tail).astype(BF16)
        vb_ref[...] = v_ref[...].astype(BF16)

    col = lambda cb: pl.BlockSpec((ts, D_ATT), lambda i: (i, cb))
    wide = pl.BlockSpec((ts, N_HEADS * w), lambda i: (i, 0))
    return pl.pallas_call(
        body, name="attn_prep",
        out_shape=(jax.ShapeDtypeStruct((s, N_HEADS * w), BF16), jax.ShapeDtypeStruct((s, N_HEADS * w), BF16),
                   jax.ShapeDtypeStruct((s, D_ATT), BF16)),
        grid=(s // ts,), in_specs=[col(0), col(1), col(2), pl.BlockSpec((ts, N_HEADS), lambda i: (i, 0))],
        out_specs=(wide, wide, col(0)),
        compiler_params=_cparams("parallel"),
    )(z, z, z, c)


def _attn_fwd(qa, ka, vb, tq):
    s = qa.shape[0]
    nq = s // tq
    w = 2 * HEAD_DIM

    def body(q_ref, k_ref, v_ref, o_ref, lse_ref):
        i = pl.program_id(1)
        first = _head_lanes()
        q2 = q_ref[...]

        def block(j, carry, masked):
            off = pl.multiple_of(j * tq, tq)
            k2 = k_ref[pl.ds(off, tq), :]
            v2 = v_ref[pl.ds(off, tq), :]
            new = []
            for hh in range(2):
                m, l, acc = carry[hh]
                sc = lax.dot_general(q2[:, hh * w:(hh + 1) * w], k2[:, hh * w:(hh + 1) * w], NT,
                                     preferred_element_type=F32)
                if masked:
                    sc = jnp.where(_causal_mask(tq), sc, NEG_INF)
                m_new = jnp.maximum(m, jnp.max(sc, axis=-1, keepdims=True))
                alpha = jnp.exp(m - m_new)
                p = jnp.exp(sc - m_new)
                l = alpha * l + jnp.sum(p, axis=-1, keepdims=True)
                p_hi = p.astype(BF16)
                p_lo = (p - p_hi.astype(F32)).astype(BF16)
                acc = (alpha * acc + jnp.dot(p_hi, v2, preferred_element_type=F32)
                       + jnp.dot(p_lo, v2, preferred_element_type=F32))
                new.append((m_new, l, acc))
            return tuple(new)

        one = (jnp.full((tq, 1), NEG_INF, F32), jnp.zeros((tq, 1), F32), jnp.zeros((tq, w), F32))
        carry = lax.fori_loop(0, i, lambda j, c: block(j, c, False), (one, one))
        (m0, l0, a0), (m1, l1, a1) = block(i, carry, True)
        o_ref[...] = jnp.where(first, a0 / l0, a1 / l1)
        lse_ref[0] = m0 + jnp.log(l0)
        lse_ref[1] = m1 + jnp.log(l1)

    return pl.pallas_call(
        body, name="attn_fwd",
        out_shape=(jax.ShapeDtypeStruct((s, D_ATT), F32), jax.ShapeDtypeStruct((N_HEADS, s, 1), F32)),
        grid=(N_PAIRS, nq),
        in_specs=[pl.BlockSpec((tq, 2 * w), lambda hp, i: (i, hp)),
                  pl.BlockSpec((s, 2 * w), lambda hp, i: (0, hp)),
                  pl.BlockSpec((s, w), lambda hp, i: (0, hp))],
        out_specs=(pl.BlockSpec((tq, w), lambda hp, i: (i, hp)),
                   pl.BlockSpec((2, tq, 1), lambda hp, i: (hp, i, 0))),
        compiler_params=_cparams("parallel", "parallel"),
    )(qa, ka, vb)


def _attn_delta(o, do, tq):
    s = o.shape[0]
    w = 2 * HEAD_DIM

    def body(o_ref, do_ref, d_ref):
        first = _head_lanes()
        prod = o_ref[...] * do_ref[...].astype(F32)
        d_ref[0] = jnp.sum(_pick(prod, first, 0), axis=-1, keepdims=True)
        d_ref[1] = jnp.sum(_pick(prod, first, 1), axis=-1, keepdims=True)

    blk = pl.BlockSpec((tq, w), lambda hp, i: (i, hp))
    return pl.pallas_call(
        body, name="attn_delta", out_shape=jax.ShapeDtypeStruct((N_HEADS, s, 1), F32), grid=(N_PAIRS, s // tq),
        in_specs=[blk, blk], out_specs=pl.BlockSpec((2, tq, 1), lambda hp, i: (hp, i, 0)),
        compiler_params=_cparams("parallel", "parallel"),
    )(o, do)


def _attn_bwd(qa, ka, vb, do, lse, delta, tq):
    s = qa.shape[0]
    nq = s // tq
    w = 2 * HEAD_DIM

    def body(q_ref, do_ref, lse_ref, dl_ref, k_ref, v_ref, dq_ref, dk_ref, dv_ref, dc_ref):
        j = pl.program_id(1)
        first = _head_lanes()

        @pl.when(j == 0)
        def _():
            dq_ref[...] = jnp.zeros_like(dq_ref)

        k2 = k_ref[...]
        v2 = v_ref[...]

        def step(i, carry, masked):
            off = pl.multiple_of(i * tq, tq)
            rows = pl.ds(off, tq)
            q2 = q_ref[rows, :]
            do2 = do_ref[rows, :]
            new, dqs = [], []
            for hh in range(2):
                dk, dv, dcs = carry[hh]
                qh = q2[:, hh * w:(hh + 1) * w]
                kh = k2[:, hh * w:(hh + 1) * w]
                sc = lax.dot_general(qh, kh, NT, preferred_element_type=F32)
                if masked:
                    sc = jnp.where(_causal_mask(tq), sc, NEG_INF)
                p = jnp.exp(sc - lse_ref[hh, rows, :])
                dv = dv + lax.dot_general(p.astype(BF16), do2, TN, preferred_element_type=F32)
                dp = lax.dot_general(_pick(do2, first, hh), v2, NT, preferred_element_type=F32)
                ds = p * (dp - dl_ref[hh, rows, :])
                dsb = ds.astype(BF16)
                dk = dk + lax.dot_general(dsb, qh, TN, preferred_element_type=F32)
                dqs.append(jnp.dot(dsb, kh, preferred_element_type=F32))
                new.append((dk, dv, dcs + jnp.sum(ds, axis=0, keepdims=True)))
            dq_ref[rows, :] += jnp.where(first, dqs[0], pltpu.roll(dqs[1], HEAD_DIM, 1)) * ATT_SCALE
            return tuple(new)

        one = (jnp.zeros((tq, w), F32), jnp.zeros((tq, w), F32), jnp.zeros((1, tq), F32))
        carry = step(j, (one, one), True)
        (dk0, dv0, dc0), (dk1, dv1, dc1) = lax.fori_loop(j + 1, nq, lambda i, c: step(i, c, False), carry)
        dk_ref[...] = jnp.where(first, dk0, pltpu.roll(dk1, HEAD_DIM, 1))
        dv_ref[...] = jnp.where(first, dv0, dv1)
        dc_ref[0, 0] = -dc0
        dc_ref[1, 0] = -dc1

    whole = lambda width: pl.BlockSpec((s, width), lambda hp, j: (0, hp))
    whole_heads = pl.BlockSpec((2, s, 1), lambda hp, j: (hp, 0, 0))
    blk = lambda width: pl.BlockSpec((tq, width), lambda hp, j: (j, hp))
    crow = pl.BlockSpec((2, 1, 1, tq), lambda hp, j: (hp, j, 0, 0))
    return pl.pallas_call(
        body, name="attn_bwd",
        out_shape=(jax.ShapeDtypeStruct((s, D_ATT), F32), jax.ShapeDtypeStruct((s, D_ATT), F32),
                   jax.ShapeDtypeStruct((s, D_ATT), F32), jax.ShapeDtypeStruct((N_HEADS, nq, 1, tq), F32)),
        grid=(N_PAIRS, nq),
        in_specs=[whole(2 * w), whole(w), whole_heads, whole_heads, blk(2 * w), blk(w)],
        out_specs=(whole(w), blk(w), blk(w), crow),
        compiler_params=_cparams("parallel", "arbitrary"),
    )(qa, do, lse, delta, ka, vb)


def _pair_sums(x, first):
    total = jnp.sum(x, axis=-1, keepdims=True)
    head = jnp.sum(jnp.where(first, x, 0.0), axis=-1, keepdims=True)
    return head, total - head


def _pair_mean(x, first):
    head, tail = _pair_sums(x, first)
    return jnp.where(first, head, tail) * (1.0 / GROUP_DIM)


def _gm_pair_norm(v2, first):
    d = v2 - _pair_mean(v2, first)
    rstd = lax.rsqrt(_pair_mean(d * d, first) + EPS)
    return d * rstd, rstd


def _gm_pair_mix(w_ref, pr, rhs, first):
    return jnp.where(first, jnp.dot(w_ref[2 * pr], rhs, preferred_element_type=F32),
                     jnp.dot(w_ref[2 * pr + 1], rhs, preferred_element_type=F32))


def _gmlp_fwd(z, wt, bs_t, vgain):
    s = z.shape[0]

    def body(gu_ref, gv_ref, wt_ref, bs_ref, vg_ref, o_ref):
        first = _head_lanes()
        for pr in range(N_GROUPS // 2):
            sl = slice(2 * pr * GROUP_DIM, 2 * (pr + 1) * GROUP_DIM)
            vhat, _ = _gm_pair_norm(_gelu(gv_ref[:, sl]), first)
            vn = (vhat * vg_ref[:, sl]).astype(BF16)
            bias = jnp.where(first, bs_ref[:, 2 * pr:2 * pr + 1], bs_ref[:, 2 * pr + 1:2 * pr + 2])
            o_ref[:, sl] = _gelu(gu_ref[:, sl]) * (_gm_pair_mix(wt_ref, pr, vn, first) + bias)

    full = lambda a: pl.BlockSpec(a.shape, lambda n: (0,) * a.ndim)
    return pl.pallas_call(
        body, name="gmlp_fwd", out_shape=jax.ShapeDtypeStruct((s, D_GM), F32), grid=(s // CHUNK,),
        in_specs=[pl.BlockSpec((CHUNK, D_GM), lambda n: (n, 3)), pl.BlockSpec((CHUNK, D_GM), lambda n: (n, 4)),
                  full(wt), full(bs_t), full(vgain)],
        out_specs=pl.BlockSpec((CHUNK, D_GM), lambda n: (n, 0)),
        compiler_params=_cparams("parallel"),
    )(z, z, wt, bs_t, vgain)


def _gmlp_bwd(z, dgm, wt, wt_t, bs_t, vgain):
    s = z.shape[0]

    def body(gu_ref, gv_ref, dgm_ref, wt_ref, wtt_ref, bs_ref, vg_ref, dgu_ref, dgv_ref, dwt_ref, dbs_ref, dvg_ref):
        @pl.when(pl.program_id(0) == 0)
        def _():
            dwt_ref[...] = jnp.zeros_like(dwt_ref)
            dbs_ref[...] = jnp.zeros_like(dbs_ref)
            dvg_ref[...] = jnp.zeros_like(dvg_ref)

        first = _head_lanes()
        for pr in range(N_GROUPS // 2):
            g0, g1 = 2 * pr, 2 * pr + 1
            sl = slice(g0 * GROUP_DIM, (g1 + 1) * GROUP_DIM)
            gu = gu_ref[:, sl]
            gv = gv_ref[:, sl]
            dgm = dgm_ref[:, sl]
            vhat, rstd = _gm_pair_norm(_gelu(gv), first)
            gain = vg_ref[:, sl]
            vn = (vhat * gain).astype(BF16)
            bias = jnp.where(first, bs_ref[:, g0:g0 + 1], bs_ref[:, g1:g1 + 1])
            mixed = _gm_pair_mix(wt_ref, pr, vn, first) + bias
            dgu_ref[:, sl] = (dgm * mixed * _gelu_grad(gu)).astype(BF16)
            dmixed = dgm * _gelu(gu)
            db0, db1 = _pair_sums(dmixed, first)
            dbs_ref[:, g0:g0 + 1] += db0
            dbs_ref[:, g1:g1 + 1] += db1
            dwt_ref[g0] += lax.dot_general(_pick(dmixed, first, 0).astype(BF16), vn, NT, preferred_element_type=F32)
            dwt_ref[g1] += lax.dot_general(_pick(dmixed, first, 1).astype(BF16), vn, NT, preferred_element_type=F32)
            dvn = _gm_pair_mix(wtt_ref, pr, dmixed.astype(BF16), first)
            dvg_ref[:, sl] += _colsum(dvn * vhat)
            dvhat = dvn * gain
            dvf = rstd * (dvhat - _pair_mean(dvhat, first) - vhat * _pair_mean(dvhat * vhat, first))
            dgv_ref[:, sl] = (dvf * _gelu_grad(gv)).astype(BF16)

    full = lambda a: pl.BlockSpec(a.shape, lambda n: (0,) * a.ndim)
    chunk = pl.BlockSpec((CHUNK, D_GM), lambda n: (n, 0))
    return pl.pallas_call(
        body, name="gmlp_bwd",
        out_shape=(jax.ShapeDtypeStruct((s, D_GM), BF16), jax.ShapeDtypeStruct((s, D_GM), BF16),
                   jax.ShapeDtypeStruct(wt.shape, F32), jax.ShapeDtypeStruct(bs_t.shape, F32),
                   jax.ShapeDtypeStruct(vgain.shape, F32)),
        grid=(s // CHUNK,),
        in_specs=[pl.BlockSpec((CHUNK, D_GM), lambda n: (n, 3)), pl.BlockSpec((CHUNK, D_GM), lambda n: (n, 4)),
                  chunk, full(wt), full(wt_t), full(bs_t), full(vgain)],
        out_specs=(chunk, chunk, full(wt), full(bs_t), full(vgain)),
        compiler_params=_cparams("arbitrary"),
    )(z, z, dgm, wt, wt_t, bs_t, vgain)


def _layer_fwd(h0, p_i, w, tq, late):
    s, d = h0.shape
    nq = s // tq
    sv = {"h0": h0}

    (hn1,) = _rowwise(lambda h, g: _rms(h)[0] * g, [h0], [w["mix_pre_norm"]], [(d, BF16)], [], "pre_mix", 256)
    z = _mm(hn1, w["w_in"], "nn", F32, "mm_in")
    fl_t = z[:, F_OFF:F_OFF + N_HEADS].T
    c_t = _forget_fwd(fl_t, w["b_forget"])
    qa, ka, vb = _attn_prep(z, c_t.T, 256)
    att, lse = _attn_fwd(qa, ka, vb, tq)
    gm = _gmlp_fwd(z, w["wt"], w["bs_t"], w["gm_v_norm"])
    w = dict(w, **late(att))

    def mix_out(att, gm, g):
        return jnp.concatenate([_rms(att)[0] * g[:, :D_ATT], _rms(gm)[0] * g[:, D_ATT:]], axis=-1)

    (mc,) = _rowwise(mix_out, [att, gm], [w["mix_out_norm"]], [(D_ATT + D_GM, BF16)], [], "mix_out", 256)
    y1 = _mm(mc, w["w_out"], "nn", F32, "mm_out")

    def post_mix(h0, y1, gpost, gpre):
        h1 = h0 + _rms(y1)[0] * gpost
        return h1, _rms(h1)[0] * gpre

    h1, hn2 = _rowwise(post_mix, [h0, y1], [w["mix_post_norm"], w["ffn_pre_norm"]],
                       [(d, F32), (d, BF16)], [], "post_mix", 256)
    ab, t = _ffn_in_swiglu(hn2, w["w_ffn_in"])
    y2 = _mm(t, w["w_ffn_out"], "nn", F32, "mm_ffn_out", a3="k", b3="k")

    def post_ffn(h1, y2, g):
        h2 = h1 + _rms(y2)[0] * g
        return h2, _rms(h2)[0]

    h2, hr = _rowwise(post_ffn, [h1, y2], [w["ffn_post_norm"]], [(d, F32), (d, BF16)], [], "post_ffn", 256)
    gl = _mm(hr, w["w_ple_gate"], "nn", F32, "mm_gate")
    pe = _mm(p_i, w["w_ple"], "nn", F32, "mm_ple")
    (h3,) = _rowwise(lambda h2, gl, pe, g: h2 + _sigmoid(gl) * (_rms(pe)[0] * g), [h2, gl, pe], [w["ple_norm"]],
                     [(d, F32)], [], "ple_out", 256)
    sv.update(hn1=hn1, z=z, fl_t=fl_t, qa=qa, ka=ka, vb=vb, lse=lse, att=att, gm=gm,
              mc=mc, y1=y1, h1=h1, hn2=hn2, ab=ab, y2=y2, h2=h2, hr=hr, gl=gl, pe=pe, p_i=p_i)
    return h3, sv


def _layer_bwd(dh3, sv, w, tq, mid):
    s, d = dh3.shape
    g = {}
    by_rows = lambda a: a.reshape(N_DEV, -1, a.shape[-1])
    by_cols = lambda a: jnp.stack(jnp.split(a, N_DEV, axis=-1))

    def ple_bwd(dh3, gl, pe, gple):
        gate = _sigmoid(gl)
        pehat, rpe = _rms(pe)
        dgl = dh3 * (pehat * gple) * gate * (1.0 - gate)
        de = dh3 * gate
        return dgl, _rms_bwd(pehat, rpe, de * gple), _colsum(de * pehat)

    dgl, dpe, g["ple_norm"] = _rowwise(ple_bwd, [dh3, sv["gl"], sv["pe"]], [w["ple_norm"]],
                                       [(d, BF16), (d, BF16)], [d], "ple_bwd", 256)
    g["w_ple_gate"] = by_rows(_mm(sv["hr"], dgl, "tn", BF16, "mm_dgate"))
    dhr = _mm(dgl, w["w_ple_gate"], "nt", F32, "mm_dhr")
    g["w_ple"] = by_cols(_mm(sv["p_i"], dpe, "tn", BF16, "mm_dple"))

    def ffn_post_bwd(dh3, dhr, h2, y2, gpost):
        h2hat, r2 = _rms(h2)
        dh2 = dh3 + _rms_bwd(h2hat, r2, dhr)
        y2hat, ry = _rms(y2)
        return dh2, _rms_bwd(y2hat, ry, dh2 * gpost), _colsum(dh2 * y2hat)

    dh2, dy2, g["ffn_post_norm"] = _rowwise(ffn_post_bwd, [dh3, dhr, sv["h2"], sv["y2"]], [w["ffn_post_norm"]],
                                            [(d, F32), (d, BF16)], [d], "ffn_post_bwd", 256)
    dt = _mm(dy2, w["w_ffn_out"], "nt", F32, "mm_dt", b3="n", o3="n")
    t, dab = _swiglu_bwd(sv["ab"], dt, 256)
    dab = dab.reshape((N_DEV,) + dab.shape[2:])
    g["w_ffn_out"] = by_rows(_mm(t, dy2, "tn", BF16, "mm_dffn_out", a3="m", o3="m"))
    dhn2 = _mm(dab, w["w_ffn_in"], "nt", F32, "mm_dhn2", a3="k", b3="k")
    g["w_ffn_in"] = _mm(sv["hn2"], dab, "tn", BF16, "mm_dffn_in", b3="n", o3="n")

    def mix_post_bwd(dh2, dhn2, h1, y1, gpre, gpost):
        h1hat, r1 = _rms(h1)
        dh1 = dh2 + _rms_bwd(h1hat, r1, dhn2 * gpre)
        y1hat, ry = _rms(y1)
        return dh1, _rms_bwd(y1hat, ry, dh1 * gpost), _colsum(dhn2 * h1hat), _colsum(dh1 * y1hat)

    dh1, dy1, g["ffn_pre_norm"], g["mix_post_norm"] = _rowwise(
        mix_post_bwd, [dh2, dhn2, sv["h1"], sv["y1"]], [w["ffn_pre_norm"], w["mix_post_norm"]],
        [(d, F32), (d, BF16)], [d, d], "mix_post_bwd", 256)
    dmc = _mm(dy1, w["w_out"], "nt", F32, "mm_dmc")
    g["w_out"] = by_rows(_mm(sv["mc"], dy1, "tn", BF16, "mm_dout"))
    w = dict(w, **mid(g, dmc))

    def mix_out_bwd(da, dg, att, gm, gain):
        atthat, ra = _rms(att)
        gmhat, rg = _rms(gm)
        dgain = jnp.concatenate([_colsum(da * atthat), _colsum(dg * gmhat)], axis=-1)
        return _rms_bwd(atthat, ra, da * gain[:, :D_ATT]), _rms_bwd(gmhat, rg, dg * gain[:, D_ATT:]), dgain

    datt, dgm, g["mix_out_norm"] = _rowwise(
        mix_out_bwd, [(dmc, 0, D_ATT), (dmc, 1, D_GM), sv["att"], sv["gm"]], [w["mix_out_norm"]],
        [(D_ATT, BF16), (D_GM, F32)], [D_ATT + D_GM], "mix_out_bwd", 256)

    dgu, dgv, dwt, dbs_t, g["gm_v_norm"] = _gmlp_bwd(sv["z"], dgm, w["wt"], w["wt_t"], w["bs_t"], w["gm_v_norm"])
    g["gm_w_s"] = dwt * jnp.tril(jnp.ones((CHUNK, CHUNK), F32))[None]
    g["gm_b_s"] = dbs_t.T

    delta = _attn_delta(sv["att"], datt, tq)
    dq, dk, dv, dc_row = _attn_bwd(sv["qa"], sv["ka"], sv["vb"], datt, sv["lse"], delta, tq)
    dfl_t, db = _forget_bwd(dc_row.reshape(N_HEADS, s), sv["fl_t"], w["b_forget"])
    g["b_forget"] = db.reshape(1, N_HEADS)
    dz = jnp.concatenate([dq.astype(BF16), dk.astype(BF16), dv.astype(BF16), dgu, dgv, dfl_t.T.astype(BF16),
                          jnp.zeros((s, D_IN_PAD - F_OFF - N_HEADS), BF16)], axis=-1)
    dhn1 = _mm(dz, w["w_in"], "nt", F32, "mm_dhn1")
    din = _mm(sv["hn1"], dz, "tn", BF16, "mm_din")
    din = jnp.concatenate([din[:, :3 * D_ATT], din[:, F_OFF:F_OFF + N_HEADS], din[:, 3 * D_ATT:F_OFF]], axis=-1)
    g["w_in"] = by_cols(din)

    def mix_pre_bwd(dh1, dhn1, h0, gpre):
        h0hat, r0 = _rms(h0)
        return dh1 + _rms_bwd(h0hat, r0, dhn1 * gpre), _colsum(dhn1 * h0hat)

    dh0, g["mix_pre_norm"] = _rowwise(mix_pre_bwd, [dh1, dhn1, sv["h0"]], [w["mix_pre_norm"]],
                                      [(d, F32)], [d], "mix_pre_bwd", 256)
    return dh0, g


ANY = pl.BlockSpec(memory_space=pl.ANY)


def _all_gather(xs, layer, name):
    n = len(xs)

    def body(*refs):
        x_refs, out_refs = refs[:n], refs[n:2 * n]
        send_sems, recv_sems, local_sems = refs[2 * n:]
        x, y, c = lax.axis_index("x"), lax.axis_index("y"), lax.axis_index("c")
        me, sibling = (x, y, c), (x, y, 1 - c)
        chips = [(1 - x, y), (x, 1 - y), (1 - x, 1 - y)]

        def shard(a):
            return x_refs[a] if layer is None else x_refs[a].at[layer]

        def rows(a, px, py, pc):
            return out_refs[a].at[4 * px + 2 * py + pc]

        def copy(a, kk, block, to, from_shard=False):
            return pltpu.make_async_remote_copy(
                src_ref=shard(a) if from_shard else rows(a, *block), dst_ref=rows(a, *block),
                send_sem=send_sems.at[7 * a + kk], recv_sem=recv_sems.at[7 * a + kk],
                device_id=to, device_id_type=MESH)

        mine = [pltpu.make_async_copy(shard(a), rows(a, *me), local_sems.at[a]) for a in range(n)]
        for cp in mine:
            cp.start()
        first = []
        for a in range(n):
            first.append(copy(a, 0, me, sibling, from_shard=True))
            first += [copy(a, 1 + j, me, (*chip, c), from_shard=True) for j, chip in enumerate(chips)]
        for cp in first:
            cp.start()
        passed = []
        for j, chip in enumerate(chips):
            for a in range(n):
                copy(a, 1 + j, (*chip, c), me).wait_recv()
                passed.append(copy(a, 4 + j, (*chip, c), sibling))
                passed[-1].start()
        for a in range(n):
            copy(a, 0, sibling, me).wait_recv()
        for j, chip in enumerate(chips):
            for a in range(n):
                copy(a, 4 + j, (*chip, 1 - c), me).wait_recv()
        for cp in first + passed:
            cp.wait_send()
        for cp in mine:
            cp.wait()

    shapes = [x.shape if layer is None else x.shape[1:] for x in xs]
    return pl.pallas_call(
        body, name=name, out_shape=[jax.ShapeDtypeStruct((N_DEV,) + sh, x.dtype) for sh, x in zip(shapes, xs)],
        in_specs=[ANY] * n, out_specs=[ANY] * n,
        scratch_shapes=[pltpu.SemaphoreType.DMA((7 * n,)), pltpu.SemaphoreType.DMA((7 * n,)),
                        pltpu.SemaphoreType.DMA((n,))],
    )(*xs)


HBM = pl.BlockSpec(memory_space=pltpu.HBM)
SEMS = pl.BlockSpec(memory_space=pltpu.SEMAPHORE)
EFFECT = pltpu.SideEffectType.DATAFLOW_SIDE_EFFECTING
FLIPS = tuple((fx, fy, fc) for fx in (0, 1) for fy in (0, 1) for fc in (0, 1))[1:]


def _exchange_copies(src_refs, land_refs, send_sems, recv_sems, layer, scatter):
    x, y, c = lax.axis_index("x"), lax.axis_index("y"), lax.axis_index("c")
    me = 4 * x + 2 * y + c
    copies = []
    for a, (src, land) in enumerate(zip(src_refs, land_refs)):
        for f, (fx, fy, fc) in enumerate(FLIPS):
            px, py, pc = (1 - x if fx else x), (1 - y if fy else y), (1 - c if fc else c)
            if scatter:
                block = src.at[4 * px + 2 * py + pc]
            else:
                block = src if layer is None else src.at[layer]
            copies.append(pltpu.make_async_remote_copy(
                src_ref=block, dst_ref=land.at[me], send_sem=send_sems.at[7 * a + f], recv_sem=recv_sems.at[7 * a + f],
                device_id=(px, py, pc), device_id_type=MESH))
    return copies


def _exchange_start(srcs, lands, layer, scatter, name):
    n = len(srcs)

    def body(*refs):
        for cp in _exchange_copies(refs[:n], refs[n:2 * n], refs[2 * n], refs[2 * n + 1], layer, scatter):
            cp.start()
        token = refs[-1]
        token[...] = jnp.zeros_like(token)

    operands = list(srcs) + list(lands)
    outs = pl.pallas_call(
        body, name=name,
        out_shape=(pltpu.SemaphoreType.DMA((7 * n,)), pltpu.SemaphoreType.DMA((7 * n,)),
                   *[pltpu.HBM(a.shape, a.dtype) for a in operands], jax.ShapeDtypeStruct((8, LANE), F32)),
        in_specs=[HBM] * (2 * n),
        out_specs=(SEMS, SEMS, *[HBM] * (2 * n), pl.BlockSpec(memory_space=pltpu.VMEM)),
        input_output_aliases={i: 2 + i for i in range(2 * n)},
        compiler_params=pltpu.CompilerParams(has_side_effects=EFFECT),
    )(*[pltpu.with_memory_space_constraint(a, pltpu.HBM) for a in operands])
    return outs[0], outs[1], outs[2:2 + n], outs[2 + n:2 + 2 * n], outs[-1]


def _exchange_wait(started, after, layer, scatter, name):
    send_sems, recv_sems, srcs, lands, _ = started
    n = len(srcs)

    def body(*refs):
        for cp in _exchange_copies(refs[:n], refs[n:2 * n], refs[2 * n], refs[2 * n + 1], layer, scatter):
            cp.wait_send()
            cp.wait_recv()

    operands = list(srcs) + list(lands)
    outs = pl.pallas_call(
        body, name=name, out_shape=tuple(pltpu.HBM(a.shape, a.dtype) for a in operands),
        in_specs=[HBM] * (2 * n) + [SEMS, SEMS, ANY], out_specs=[HBM] * (2 * n),
        input_output_aliases={i: i for i in range(2 * n)},
        compiler_params=pltpu.CompilerParams(has_side_effects=EFFECT),
    )(*operands, send_sems, recv_sems, after)
    return outs[:n], outs[n:]


def _sum_devices(parts):
    _, r, c = parts.shape

    def body(p_ref, o_ref):
        acc = p_ref[0]
        for j in range(1, N_DEV):
            acc = acc + p_ref[j]
        o_ref[...] = acc

    return pl.pallas_call(
        body, name="small_sum", out_shape=jax.ShapeDtypeStruct((r, c), F32), grid=(r // SMALL_ROWS,),
        in_specs=[pl.BlockSpec((N_DEV, SMALL_ROWS, c), lambda i: (0, i, 0))],
        out_specs=pl.BlockSpec((SMALL_ROWS, c), lambda i: (i, 0)),
        compiler_params=_cparams("parallel"),
    )(parts)


def _adamw_math(w, g, m, v):
    m = ADAM_B1 * m + (1.0 - ADAM_B1) * g
    v = ADAM_B2 * v + (1.0 - ADAM_B2) * (g * g)
    m_hat = m / (1.0 - ADAM_B1 ** ADAM_STEP)
    v_hat = v / (1.0 - ADAM_B2 ** ADAM_STEP)
    return -ADAM_LR * (m_hat / (jnp.sqrt(v_hat) + ADAM_EPS) + ADAM_WD * w), m, v


def _adamw_shard(w, m, v, parts, layer, name):
    _, a, b = w.shape
    ta = _tile(a, 256, 16)

    def body(w_ref, m_ref, v_ref, p_ref, g_ref, d_ref, nm_ref, nv_ref):
        g = p_ref[0].astype(F32)
        for j in range(1, N_DEV):
            g = g + p_ref[j].astype(F32)
        g_ref[...] = g
        d_ref[...], nm_ref[...], nv_ref[...] = _adamw_math(w_ref[0], g, m_ref[0], v_ref[0])

    mine = pl.BlockSpec((1, ta, b), lambda i: (layer, i, 0))
    out = pl.BlockSpec((ta, b), lambda i: (i, 0))
    return pl.pallas_call(
        body, name=name, out_shape=[jax.ShapeDtypeStruct((a, b), F32)] * 4, grid=(a // ta,),
        in_specs=[mine, mine, mine, pl.BlockSpec((N_DEV, ta, b), lambda i: (0, i, 0))], out_specs=[out] * 4,
        compiler_params=_cparams("parallel"),
    )(w, m, v, parts)


def _pack_small(pieces):
    flat = jnp.concatenate([p.reshape(-1) for p in pieces])
    total = -(-flat.shape[0] // (SMALL_COLS * SMALL_ROWS)) * SMALL_COLS * SMALL_ROWS
    return jnp.pad(flat, (0, total - flat.shape[0])).reshape(-1, SMALL_COLS)


def kernel(x, p, mix_pre_norm, mix_post_norm, w_in, b_forget, gm_v_norm, gm_w_s, gm_b_s, mix_out_norm, w_out, ffn_pre_norm, ffn_post_norm, w_ffn_in, w_ffn_out, w_ple, ple_norm, w_ple_gate, loss_target, m_mix_pre_norm, m_mix_post_norm, m_w_in, m_b_forget, m_gm_v_norm, m_gm_w_s, m_gm_b_s, m_mix_out_norm, m_w_out, m_ffn_pre_norm, m_ffn_post_norm, m_w_ffn_in, m_w_ffn_out, m_w_ple, m_ple_norm, m_w_ple_gate, v_mix_pre_norm, v_mix_post_norm, v_w_in, v_b_forget, v_gm_v_norm, v_gm_w_s, v_gm_b_s, v_mix_out_norm, v_w_out, v_ffn_pre_norm, v_ffn_post_norm, v_w_ffn_in, v_w_ffn_out, v_w_ple, v_ple_norm, v_w_ple_gate):
    given = dict(locals())
    weights = {n: given[n] for n in WEIGHT_ORDER}
    mom_m = {n: given["m_" + n] for n in WEIGHT_ORDER}
    mom_v = {n: given["v_" + n] for n in WEIGHT_ORDER}
    depth = w_in.shape[0]
    s, d = x.shape[1], x.shape[2]
    tq = _tile(s, ATT_BLOCK)
    me = 4 * lax.axis_index("x") + 2 * lax.axis_index("y") + lax.axis_index("c")
    tril = jnp.tril(jnp.ones((CHUNK, CHUNK), F32))

    def landing(block):
        return lax.dynamic_update_index_in_dim(lax.empty((N_DEV,) + block.shape, block.dtype), block, me, 0)

    def mix_weights(i, got):
        w_in_full = jnp.concatenate([got["w_in"][j] for j in range(N_DEV)], axis=-1)
        pad = jnp.zeros((d, D_IN_PAD - D_IN), BF16)
        wt = gm_w_s[i] * tril[None]
        lw = dict(
            w_in=jnp.concatenate([w_in_full[:, :3 * D_ATT], w_in_full[:, 3 * D_ATT + N_HEADS:],
                                  w_in_full[:, 3 * D_ATT:3 * D_ATT + N_HEADS], pad], axis=-1),
            b_forget=b_forget[i][:, None], wt=wt.astype(BF16), wt_t=wt.transpose(0, 2, 1).astype(BF16),
            bs_t=gm_b_s[i].T)
        lw.update({n: weights[n][i][None] for n in ("mix_pre_norm", "mix_post_norm", "gm_v_norm", "mix_out_norm",
                                                    "ffn_pre_norm", "ffn_post_norm", "ple_norm")})
        return lw

    def rest_weights(got):
        return dict(w_out=got["w_out"].reshape(-1, d), w_ffn_in=got["w_ffn_in"],
                    w_ffn_out=got["w_ffn_out"].reshape(N_DEV // 2, -1, d),
                    w_ple=jnp.concatenate([got["w_ple"][j] for j in range(N_DEV)], axis=-1),
                    w_ple_gate=got["w_ple_gate"].reshape(-1, d))

    shards = {n: weights[n].astype(BF16) for n in MATRIX_WEIGHTS}

    def gather_start(i):
        started = {}
        order = jnp.zeros((), BF16)
        for tag, grp in EXCHANGE_GROUPS.items():
            started[tag] = _exchange_start([shards[n] for n in grp], [landing(shards[n][i] + order) for n in grp], i,
                                           False, f"weights_gather_start_{i}_{tag}")
            order = started[tag][4][0, 0].astype(BF16)
        return started

    def gather_finish(i, tag, pending, after):
        srcs, got = _exchange_wait(pending[tag], after, i, False, f"weights_gather_wait_{i}_{tag}")
        shards.update(zip(EXCHANGE_GROUPS[tag], srcs))
        return dict(zip(EXCHANGE_GROUPS[tag], got))

    h = x[0]
    saved, layer_w = [], []
    pending = gather_start(0)
    for i in range(depth):
        lw = mix_weights(i, gather_finish(i, "mix", pending, h))
        if i == 0:
            lw["mix_pre_norm"] = lw["mix_pre_norm"] + pending["rest"][4][:1, :1]
        following = {}

        def late(att, i=i, pending=pending, lw=lw, following=following):
            rest = rest_weights(gather_finish(i, "rest", pending, att))
            lw.update(rest)
            if i + 1 == depth:
                return rest
            following.update(gather_start(i + 1))
            token = following["mix"][4][:1, :1] + following["rest"][4][:1, :1]
            return dict(rest, mix_out_norm=lw["mix_out_norm"] + token)

        h, sv = _layer_fwd(h, p[i, 0], lw, tq, late)
        layer_w.append(lw)
        saved.append(sv)
        pending = following

    def loss_head(y, t):
        err = y - t
        return err * (1.0 / d), _colsum(err * err)

    dh, sq = _rowwise(loss_head, [h, loss_target[0]], [], [(d, F32)], [d], "loss_head", 256)
    loss = lax.psum(0.5 * jnp.sum(sq) / d, AXES)

    layer_g = [None] * depth
    shard_out = {n: [None] * depth for n in MATRIX_WEIGHTS}

    def scatter_start(i, tag, g):
        full_g = [g[n] for n in EXCHANGE_GROUPS[tag]]
        lands = [landing(lax.dynamic_index_in_dim(gf, me, 0, keepdims=False)) for gf in full_g]
        return _exchange_start(full_g, lands, None, True, f"grads_scatter_start_{i}_{tag}")

    def scatter_finish(i, tag, started, after):
        _, parts = _exchange_wait(started[tag], after, None, True, f"grads_scatter_wait_{i}_{tag}")
        for n, part in zip(EXCHANGE_GROUPS[tag], parts):
            shard_out[n][i] = _adamw_shard(weights[n], mom_m[n], mom_v[n], part, i, "adamw_" + n)

    before = None
    for i in reversed(range(depth)):
        lw = layer_w[i]
        if before is not None:
            lw = dict(lw, ple_norm=lw["ple_norm"] + before[1]["mix"][4][:1, :1])
        started = {}

        def mid(g, dmc, i=i, before=before, started=started, lw=lw):
            if before is not None:
                scatter_finish(before[0], "rest", before[1], dmc)
            started["rest"] = scatter_start(i, "rest", g)
            return dict(mix_out_norm=lw["mix_out_norm"] + started["rest"][4][:1, :1])

        dh, layer_g[i] = _layer_bwd(dh, saved[i], lw, tq, mid)
        if before is not None:
            scatter_finish(before[0], "mix", before[1], dh)
        started["mix"] = scatter_start(i, "mix", layer_g[i])
        before = (i, started)
    scatter_finish(before[0], "rest", before[1], before[1]["mix"][4])
    scatter_finish(before[0], "mix", before[1], dh)
    grad_x = dh[None]

    grads, deltas, new_m, new_v = {}, {}, {}, {}
    for n in MATRIX_WEIGHTS:
        grads[n], deltas[n], new_m[n], new_v[n] = (jnp.stack([shard_out[n][i][k] for i in range(depth)])
                                                   for k in range(4))

    small_g = _pack_small([jnp.stack([layer_g[i][n].reshape(-1) for i in range(depth)]) for n in SMALL_WEIGHTS])
    (gathered,) = _all_gather([small_g], None, "small_grads_all_gather")
    g_small = _sum_devices(gathered)
    pack = lambda t: _pack_small([t[n] for n in SMALL_WEIGHTS])
    dl, nm, nv = _rowwise(_adamw_math, [pack(weights), g_small, pack(mom_m), pack(mom_v)], [],
                          [(SMALL_COLS, F32)] * 3, [], "adamw_small", SMALL_ROWS)
    off = 0
    for n in SMALL_WEIGHTS:
        shp, size = weights[n].shape, weights[n].size
        grads[n], deltas[n], new_m[n], new_v[n] = (a.reshape(-1)[off:off + size].reshape(shp)
                                                   for a in (g_small, dl, nm, nv))
        off += size

    return (loss, grad_x, *[grads[n] for n in WEIGHT_ORDER], *[deltas[n] for n in WEIGHT_ORDER],
            *[new_m[n] for n in WEIGHT_ORDER], *[new_v[n] for n in WEIGHT_ORDER])
```

```python
import functools
import math

import jax
import jax.numpy as jnp
from jax import lax
from jax.experimental import pallas as pl
from jax.experimental.pallas import tpu as pltpu

F32 = jnp.float32
BF16 = jnp.bfloat16
MESH = pl.DeviceIdType.MESH
AXES = ("x", "y", "c")
N_DEV = 8

EPS = 1e-6
NEG_INF = -1e30
N_HEADS = 8
HEAD_DIM = 64
D_ATT = N_HEADS * HEAD_DIM
N_GROUPS = 8
GROUP_DIM = 64
D_GM = N_GROUPS * GROUP_DIM
CHUNK = 128
ATT_SCALE = HEAD_DIM ** -0.5
ATT_BLOCK = 1024
D_IN = 3 * D_ATT + N_HEADS + 2 * D_GM
D_IN_PAD = 3 * D_ATT + 2 * D_GM + 128
F_OFF = 3 * D_ATT + 2 * D_GM

ADAM_LR = 0.001
ADAM_B1 = 0.9
ADAM_B2 = 0.999
ADAM_EPS = 1e-08
ADAM_WD = 0.01
ADAM_STEP = 10

LANE = 128
VMEM_LIMIT = 48 * 1024 * 1024
ROW_TILE = 512
SMALL_COLS = 128
SMALL_ROWS = 512

MATRIX_WEIGHTS = ("w_in", "w_out", "w_ffn_in", "w_ffn_out", "w_ple", "w_ple_gate")
EXCHANGE_GROUPS = {"mix": ("w_in",), "rest": ("w_out", "w_ffn_in", "w_ffn_out", "w_ple", "w_ple_gate")}
SMALL_WEIGHTS = ("mix_pre_norm", "mix_post_norm", "b_forget", "gm_v_norm", "gm_w_s", "gm_b_s",
                 "mix_out_norm", "ffn_pre_norm", "ffn_post_norm", "ple_norm")
WEIGHT_ORDER = ("mix_pre_norm", "mix_post_norm", "w_in", "b_forget", "gm_v_norm", "gm_w_s", "gm_b_s",
                "mix_out_norm", "w_out", "ffn_pre_norm", "ffn_post_norm", "w_ffn_in", "w_ffn_out",
                "w_ple", "ple_norm", "w_ple_gate")


def _tile(n, pref, unit=LANE):
    best = None
    t = unit
    while t <= min(n, pref):
        if n % t == 0:
            best = t
        t += unit
    return n if best is None else best


def _cparams(*semantics):
    return pltpu.CompilerParams(dimension_semantics=semantics or None, vmem_limit_bytes=VMEM_LIMIT)


NN = (((1,), (0,)), ((), ()))
NT = (((1,), (1,)), ((), ()))
TN = (((0,), (0,)), ((), ()))
_MM_AXES = {
    "nn": ("i", "k", "k", "j"), "nt": ("i", "k", "j", "k"), "tn": ("k", "i", "k", "j")}
_MM_DN = {"nn": NN, "nt": NT, "tn": TN}


def _mm(a, b, dims, out_dtype, name, a3=None, b3=None, o3=None, tm=1024, tn=1024, tk=1024):
    ar, ac, br, bc = _MM_AXES[dims]
    letter = {"i": "m", "j": "n", "k": "k"}
    size = {}

    def measure(x, rows, cols, stacked):
        shape = x.shape
        if stacked is None:
            size.setdefault(letter[rows], shape[0])
            size.setdefault(letter[cols], shape[1])
        else:
            for ax, n in ((rows, shape[1]), (cols, shape[2])):
                size.setdefault(letter[ax], n * shape[0] if letter[ax] == stacked else n)

    measure(a, ar, ac, a3)
    measure(b, br, bc, b3)
    m, n, k = size["m"], size["n"], size["k"]
    slab = {}
    for x, stacked, rows, cols in ((a, a3, ar, ac), (b, b3, br, bc)):
        if stacked is not None:
            slab[stacked] = x.shape[1] if letter[rows] == stacked else x.shape[2]
    if o3 is not None:
        slab.setdefault(o3, slab.get(o3, None) or {"m": m, "n": n}[o3] // N_DEV)
    tile = {"m": slab.get("m") or _tile(m, tm), "n": slab.get("n") or _tile(n, tn), "k": slab.get("k") or _tile(k, tk)}
    nk = k // tile["k"]

    def spec(rows, cols, stacked):
        tr, tc = tile[letter[rows]], tile[letter[cols]]
        if stacked is None:
            return pl.BlockSpec((tr, tc), lambda i, j, kk: ({"i": i, "j": j, "k": kk}[rows], {"i": i, "j": j, "k": kk}[cols]))

        def imap(i, j, kk):
            g = {"i": i, "j": j, "k": kk}
            return (g[{"m": "i", "n": "j", "k": "k"}[stacked]],
                    0 if letter[rows] == stacked else g[rows], 0 if letter[cols] == stacked else g[cols])

        return pl.BlockSpec((1, tr, tc), imap)

    dn = _MM_DN[dims]

    def body(a_ref, b_ref, o_ref, *acc):
        av = a_ref[...] if a3 is None else a_ref[0]
        bv = b_ref[...] if b3 is None else b_ref[0]
        prod = lax.dot_general(av.astype(BF16), bv.astype(BF16), dn, preferred_element_type=F32)

        def emit(val):
            if o3 is None:
                o_ref[...] = val.astype(out_dtype)
            else:
                o_ref[0] = val.astype(out_dtype)

        if nk == 1:
            emit(prod)
            return
        (acc_ref,) = acc
        kk = pl.program_id(2)

        @pl.when(kk == 0)
        def _():
            acc_ref[...] = prod

        @pl.when(kk > 0)
        def _():
            acc_ref[...] += prod

        @pl.when(kk == nk - 1)
        def _():
            emit(acc_ref[...])

    if o3 is None:
        out_shape = (m, n)
    elif o3 == "m":
        out_shape = (m // tile["m"], tile["m"], n)
    else:
        out_shape = (n // tile["n"], m, tile["n"])
    return pl.pallas_call(
        body, name=name, out_shape=jax.ShapeDtypeStruct(out_shape, out_dtype),
        grid=(m // tile["m"], n // tile["n"], nk),
        in_specs=[spec(ar, ac, a3), spec(br, bc, b3)], out_specs=spec("i", "j", o3),
        scratch_shapes=[] if nk == 1 else [pltpu.VMEM((tile["m"], tile["n"]), F32)],
        compiler_params=_cparams("parallel", "parallel", "arbitrary"),
    )(a, b)


def _rowwise(fn, rows, vecs, outs, reds, name, ts):
    rows = [r if isinstance(r, tuple) else (r, 0, r.shape[1]) for r in rows]
    s = rows[0][0].shape[0]
    ts = _tile(s, ts, 8)
    nr, nv, no = len(rows), len(vecs), len(outs)

    def body(*refs):
        vals = fn(*[r[...] for r in refs[:nr + nv]])
        vals = vals if isinstance(vals, tuple) else (vals,)
        o_refs = refs[nr + nv:nr + nv + no]
        r_refs = refs[nr + nv + no:]
        for o_ref, val in zip(o_refs, vals[:no]):
            o_ref[...] = val.astype(o_ref.dtype)
        if r_refs:
            @pl.when(pl.program_id(0) == 0)
            def _():
                for r_ref in r_refs:
                    r_ref[...] = jnp.zeros_like(r_ref)

            for r_ref, val in zip(r_refs, vals[no:]):
                r_ref[...] += val

    in_specs = [pl.BlockSpec((ts, w), functools.partial(lambda i, cb: (i, cb), cb=cb)) for _, cb, w in rows]
    in_specs += [pl.BlockSpec(v.shape, lambda i: (0, 0)) for v in vecs]
    out_specs = [pl.BlockSpec((ts, c), lambda i: (i, 0)) for c, _ in outs]
    out_specs += [pl.BlockSpec((1, c), lambda i: (0, 0)) for c in reds]
    out_shape = [jax.ShapeDtypeStruct((s, c), dt) for c, dt in outs]
    out_shape += [jax.ShapeDtypeStruct((1, c), F32) for c in reds]
    return pl.pallas_call(
        body, name=name, out_shape=out_shape, grid=(s // ts,), in_specs=in_specs, out_specs=out_specs,
        compiler_params=_cparams("arbitrary" if reds else "parallel"),
    )(*[r[0] for r in rows], *vecs)


def _rms(x):
    r = lax.rsqrt(jnp.mean(x * x, axis=-1, keepdims=True) + EPS)
    return x * r, r


def _rms_bwd(xhat, r, dyg):
    return r * (dyg - xhat * jnp.mean(dyg * xhat, axis=-1, keepdims=True))


def _colsum(x):
    return jnp.sum(x, axis=0, keepdims=True)


def _sigmoid(x):
    return 1.0 / (1.0 + jnp.exp(-x))


GELU_C = math.sqrt(2.0 / math.pi)
GELU_A = 0.044715


def _gelu(x):
    return 0.5 * x * (1.0 + jnp.tanh(GELU_C * (x + GELU_A * x * x * x)))


def _gelu_grad(x):
    t = jnp.tanh(GELU_C * (x + GELU_A * x * x * x))
    return 0.5 * (1.0 + t) + 0.5 * x * (1.0 - t * t) * GELU_C * (1.0 + 3.0 * GELU_A * x * x)


def _ffn_in_swiglu(hn, wg):
    s, d = hn.shape
    g2, _, n = wg.shape
    g = g2 // 2
    tm = _tile(s, 1024)

    def body(h_ref, wa_ref, wb_ref, ab_ref, t_ref):
        hv = h_ref[...]
        a = jnp.dot(hv, wa_ref[0], preferred_element_type=F32)
        b = jnp.dot(hv, wb_ref[0], preferred_element_type=F32)
        ab_ref[0, 0] = a.astype(BF16)
        ab_ref[1, 0] = b.astype(BF16)
        t_ref[0] = (a * _sigmoid(a) * b).astype(BF16)

    return pl.pallas_call(
        body, name="mm_ffn_in_swiglu",
        out_shape=(jax.ShapeDtypeStruct((2, g, s, n), BF16), jax.ShapeDtypeStruct((g, s, n), BF16)),
        grid=(s // tm, g),
        in_specs=[pl.BlockSpec((tm, d), lambda i, j: (i, 0)), pl.BlockSpec((1, d, n), lambda i, j: (j, 0, 0)),
                  pl.BlockSpec((1, d, n), lambda i, j: (j + g, 0, 0))],
        out_specs=(pl.BlockSpec((2, 1, tm, n), lambda i, j: (0, j, i, 0)),
                   pl.BlockSpec((1, tm, n), lambda i, j: (j, i, 0))),
        compiler_params=_cparams("parallel", "parallel"),
    )(hn, wg, wg)


def _swiglu_bwd(ab, dt, ts):
    _, g, s, n = ab.shape
    ts = _tile(s, ts, 8)

    def body(ab_ref, dt_ref, t_ref, dab_ref):
        a = ab_ref[0, 0].astype(F32)
        b = ab_ref[1, 0].astype(F32)
        dt = dt_ref[0]
        sig = _sigmoid(a)
        silu = a * sig
        t_ref[0] = (silu * b).astype(BF16)
        dab_ref[0, 0] = (dt * b * (sig * (1.0 + a * (1.0 - sig)))).astype(BF16)
        dab_ref[1, 0] = (dt * silu).astype(BF16)

    both = pl.BlockSpec((2, 1, ts, n), lambda j, i: (0, j, i, 0))
    one = pl.BlockSpec((1, ts, n), lambda j, i: (j, i, 0))
    return pl.pallas_call(
        body, name="swiglu_bwd",
        out_shape=(jax.ShapeDtypeStruct((g, s, n), BF16), jax.ShapeDtypeStruct((2, g, s, n), BF16)),
        grid=(g, s // ts), in_specs=[both, one], out_specs=(one, both),
        compiler_params=_cparams("parallel", "parallel"),
    )(ab, dt)


def _forget_fwd(fl_t, b_col):
    h, s = fl_t.shape
    nb = s // LANE

    def body(fl_ref, b_ref, c_ref):
        upper = (lax.broadcasted_iota(jnp.int32, (LANE, LANE), 0)
                 <= lax.broadcasted_iota(jnp.int32, (LANE, LANE), 1)).astype(F32)

        def step(i, carry):
            x = fl_ref[i] + b_ref[...]
            lf = jnp.minimum(x, 0.0) - jnp.log(1.0 + jnp.exp(-jnp.abs(x)))
            cs = jnp.dot(lf, upper, precision=lax.Precision.HIGHEST, preferred_element_type=F32) + carry
            c_ref[i] = cs
            return cs[:, LANE - 1:LANE]

        lax.fori_loop(0, nb, step, jnp.zeros((h, 1), F32))

    out = pl.pallas_call(
        body, name="forget_fwd", out_shape=jax.ShapeDtypeStruct((nb, h, LANE), F32),
        compiler_params=_cparams(),
    )(fl_t.reshape(h, nb, LANE).transpose(1, 0, 2), b_col)
    return out.transpose(1, 0, 2).reshape(h, s)


def _forget_bwd(dc_t, fl_t, b_col):
    h, s = fl_t.shape
    nb = s // LANE

    def body(dc_ref, fl_ref, b_ref, dfl_ref, db_ref):
        lower = (lax.broadcasted_iota(jnp.int32, (LANE, LANE), 0)
                 >= lax.broadcasted_iota(jnp.int32, (LANE, LANE), 1)).astype(F32)

        def step(t, carry):
            tail, db = carry
            i = nb - 1 - t
            rc = jnp.dot(dc_ref[i], lower, precision=lax.Precision.HIGHEST, preferred_element_type=F32) + tail
            dfl = rc * (1.0 - _sigmoid(fl_ref[i] + b_ref[...]))
            dfl_ref[i] = dfl
            return rc[:, 0:1], db + jnp.sum(dfl, axis=1, keepdims=True)

        _, db = lax.fori_loop(0, nb, step, (jnp.zeros((h, 1), F32), jnp.zeros((h, 1), F32)))
        db_ref[...] = db

    blocked = lambda a: a.reshape(h, nb, LANE).transpose(1, 0, 2)
    dfl, db = pl.pallas_call(
        body, name="forget_bwd",
        out_shape=(jax.ShapeDtypeStruct((nb, h, LANE), F32), jax.ShapeDtypeStruct((h, 1), F32)),
        compiler_params=_cparams(),
    )(blocked(dc_t), blocked(fl_t), b_col)
    return dfl.transpose(1, 0, 2).reshape(h, s), db


N_PAIRS = N_HEADS // 2


def _causal_mask(t):
    return lax.broadcasted_iota(jnp.int32, (t, t), 0) >= lax.broadcasted_iota(jnp.int32, (t, t), 1)


def _head_lanes():
    return lax.broadcasted_iota(jnp.int32, (1, 2 * HEAD_DIM), 1) < HEAD_DIM


def _pick(x2, first, hh):
    zero = jnp.zeros_like(x2)
    return jnp.where(first, x2, zero) if hh == 0 else jnp.where(first, zero, x2)


BIAS_TERMS = 3


def _attn_prep(z, c, ts):
    s = z.shape[0]
    ts = _tile(s, ts, 16)
    w = 2 * HEAD_DIM

    def body(q_ref, k_ref, v_ref, c_ref, qa_ref, ka_ref, vb_ref):
        lane = lax.broadcasted_iota(jnp.int32, (1, w), 1)
        first = lane < HEAD_DIM
        cv = c_ref[...]
        for h in range(N_HEADS):
            pair = slice((h // 2) * w, (h // 2 + 1) * w)
            qh = q_ref[:, pair] * ATT_SCALE
            kh = k_ref[:, pair]
            if h % 2:
                qh = pltpu.roll(qh, HEAD_DIM, 1)
                kh = pltpu.roll(kh, HEAD_DIM, 1)
            rest = cv[:, h:h + 1]
            q_tail = jnp.zeros((1, w), F32)
            k_tail = jnp.zeros((1, w), F32)
            for t in range(BIAS_TERMS):
                term = rest.astype(BF16).astype(F32)
                rest = rest - term
                q_tail = jnp.where(lane == HEAD_DIM + t, term, jnp.where(lane == HEAD_DIM + BIAS_TERMS + t, 1.0, q_tail))
                k_tail = jnp.where(lane == HEAD_DIM + t, 1.0, jnp.where(lane == HEAD_DIM + BIAS_TERMS + t, -term, k_tail))
            qa_ref[:, h * w:(h + 1) * w] = jnp.where(first, qh, q_tail).astype(BF16)
            ka_ref[:, h * w:(h + 1) * w] = jnp.where(first, kh, k_tail).astype(BF16)
        vb_ref[...] = v_ref[...].astype(BF16)

    col = lambda cb: pl.BlockSpec((ts, D_ATT), lambda i: (i, cb))
    wide = pl.BlockSpec((ts, N_HEADS * w), lambda i: (i, 0))
    return pl.pallas_call(
        body, name="attn_prep",
        out_shape=(jax.ShapeDtypeStruct((s, N_HEADS * w), BF16), jax.ShapeDtypeStruct((s, N_HEADS * w), BF16),
                   jax.ShapeDtypeStruct((s, D_ATT), BF16)),
        grid=(s // ts,), in_specs=[col(0), col(1), col(2), pl.BlockSpec((ts, N_HEADS), lambda i: (i, 0))],
        out_specs=(wide, wide, col(0)),
        compiler_params=_cparams("parallel"),
    )(z, z, z, c)


def _attn_fwd(qa, ka, vb, tq):
    s = qa.shape[0]
    nq = s // tq
    w = 2 * HEAD_DIM

    def body(q_ref, k_ref, v_ref, o_ref, lse_ref):
        i = pl.program_id(1)
        first = _head_lanes()
        q2 = q_ref[...]

        def block(j, carry, masked):
            off = pl.multiple_of(j * tq, tq)
            k2 = k_ref[pl.ds(off, tq), :]
            v2 = v_ref[pl.ds(off, tq), :]
            new = []
            for hh in range(2):
                m, l, acc = carry[hh]
                sc = lax.dot_general(q2[:, hh * w:(hh + 1) * w], k2[:, hh * w:(hh + 1) * w], NT,
                                     preferred_element_type=F32)
                if masked:
                    sc = jnp.where(_causal_mask(tq), sc, NEG_INF)
                m_new = jnp.maximum(m, jnp.max(sc, axis=-1, keepdims=True))
                alpha = jnp.exp(m - m_new)
                p = jnp.exp(sc - m_new)
                l = alpha * l + jnp.sum(p, axis=-1, keepdims=True)
                p_hi = p.astype(BF16)
                p_lo = (p - p_hi.astype(F32)).astype(BF16)
                acc = (alpha * acc + jnp.dot(p_hi, v2, preferred_element_type=F32)
                       + jnp.dot(p_lo, v2, preferred_element_type=F32))
                new.append((m_new, l, acc))
            return tuple(new)

        one = (jnp.full((tq, 1), NEG_INF, F32), jnp.zeros((tq, 1), F32), jnp.zeros((tq, w), F32))
        carry = lax.fori_loop(0, i, lambda j, c: block(j, c, False), (one, one))
        (m0, l0, a0), (m1, l1, a1) = block(i, carry, True)
        o_ref[...] = jnp.where(first, a0 / l0, a1 / l1)
        lse_ref[0] = m0 + jnp.log(l0)
        lse_ref[1] = m1 + jnp.log(l1)

    return pl.pallas_call(
        body, name="attn_fwd",
        out_shape=(jax.ShapeDtypeStruct((s, D_ATT), F32), jax.ShapeDtypeStruct((N_HEADS, s, 1), F32)),
        grid=(N_PAIRS, nq),
        in_specs=[pl.BlockSpec((tq, 2 * w), lambda hp, i: (i, hp)),
                  pl.BlockSpec((s, 2 * w), lambda hp, i: (0, hp)),
                  pl.BlockSpec((s, w), lambda hp, i: (0, hp))],
        out_specs=(pl.BlockSpec((tq, w), lambda hp, i: (i, hp)),
                   pl.BlockSpec((2, tq, 1), lambda hp, i: (hp, i, 0))),
        compiler_params=_cparams("parallel", "parallel"),
    )(qa, ka, vb)


def _attn_delta(o, do, tq):
    s = o.shape[0]
    w = 2 * HEAD_DIM

    def body(o_ref, do_ref, d_ref):
        first = _head_lanes()
        prod = o_ref[...] * do_ref[...].astype(F32)
        d_ref[0] = jnp.sum(_pick(prod, first, 0), axis=-1, keepdims=True)
        d_ref[1] = jnp.sum(_pick(prod, first, 1), axis=-1, keepdims=True)

    blk = pl.BlockSpec((tq, w), lambda hp, i: (i, hp))
    return pl.pallas_call(
        body, name="attn_delta", out_shape=jax.ShapeDtypeStruct((N_HEADS, s, 1), F32), grid=(N_PAIRS, s // tq),
        in_specs=[blk, blk], out_specs=pl.BlockSpec((2, tq, 1), lambda hp, i: (hp, i, 0)),
        compiler_params=_cparams("parallel", "parallel"),
    )(o, do)


def _attn_bwd(qa, ka, vb, do, lse, delta, tq):
    s = qa.shape[0]
    nq = s // tq
    w = 2 * HEAD_DIM

    def body(q_ref, do_ref, lse_ref, dl_ref, k_ref, v_ref, dq_ref, dk_ref, dv_ref, dc_ref):
        j = pl.program_id(1)
        first = _head_lanes()

        @pl.when(j == 0)
        def _():
            dq_ref[...] = jnp.zeros_like(dq_ref)

        k2 = k_ref[...]
        v2 = v_ref[...]

        def step(i, carry, masked):
            off = pl.multiple_of(i * tq, tq)
            rows = pl.ds(off, tq)
            q2 = q_ref[rows, :]
            do2 = do_ref[rows, :]
            new, dqs = [], []
            for hh in range(2):
                dk, dv, dcs = carry[hh]
                qh = q2[:, hh * w:(hh + 1) * w]
                kh = k2[:, hh * w:(hh + 1) * w]
                sc = lax.dot_general(qh, kh, NT, preferred_element_type=F32)
                if masked:
                    sc = jnp.where(_causal_mask(tq), sc, NEG_INF)
                p = jnp.exp(sc - lse_ref[hh, rows, :])
                dv = dv + lax.dot_general(p.astype(BF16), do2, TN, preferred_element_type=F32)
                dp = lax.dot_general(_pick(do2, first, hh), v2, NT, preferred_element_type=F32)
                ds = p * (dp - dl_ref[hh, rows, :])
                dsb = ds.astype(BF16)
                dk = dk + lax.dot_general(dsb, qh, TN, preferred_element_type=F32)
                dqs.append(jnp.dot(dsb, kh, preferred_element_type=F32))
                new.append((dk, dv, dcs + jnp.sum(ds, axis=0, keepdims=True)))
            dq_ref[rows, :] += jnp.where(first, dqs[0], pltpu.roll(dqs[1], HEAD_DIM, 1)) * ATT_SCALE
            return tuple(new)

        one = (jnp.zeros((tq, w), F32), jnp.zeros((tq, w), F32), jnp.zeros((1, tq), F32))
        carry = step(j, (one, one), True)
        (dk0, dv0, dc0), (dk1, dv1, dc1) = lax.fori_loop(j + 1, nq, lambda i, c: step(i, c, False), carry)
        dk_ref[...] = jnp.where(first, dk0, pltpu.roll(dk1, HEAD_DIM, 1))
        dv_ref[...] = jnp.where(first, dv0, dv1)
        dc_ref[0, 0] = -dc0
        dc_ref[1, 0] = -dc1

    whole = lambda width: pl.BlockSpec((s, width), lambda hp, j: (0, hp))
    whole_heads = pl.BlockSpec((2, s, 1), lambda hp, j: (hp, 0, 0))
    blk = lambda width: pl.BlockSpec((tq, width), lambda hp, j: (j, hp))
    crow = pl.BlockSpec((2, 1, 1, tq), lambda hp, j: (hp, j, 0, 0))
    return pl.pallas_call(
        body, name="attn_bwd",
        out_shape=(jax.ShapeDtypeStruct((s, D_ATT), F32), jax.ShapeDtypeStruct((s, D_ATT), F32),
                   jax.ShapeDtypeStruct((s, D_ATT), F32), jax.ShapeDtypeStruct((N_HEADS, nq, 1, tq), F32)),
        grid=(N_PAIRS, nq),
        in_specs=[whole(2 * w), whole(w), whole_heads, whole_heads, blk(2 * w), blk(w)],
        out_specs=(whole(w), blk(w), blk(w), crow),
        compiler_params=_cparams("parallel", "arbitrary"),
    )(qa, do, lse, delta, ka, vb)


def _pair_sums(x, first):
    total = jnp.sum(x, axis=-1, keepdims=True)
    head = jnp.sum(jnp.where(first, x, 0.0), axis=-1, keepdims=True)
    return head, total - head


def _pair_mean(x, first):
    head, tail = _pair_sums(x, first)
    return jnp.where(first, head, tail) * (1.0 / GROUP_DIM)


def _gm_pair_norm(v2, first):
    d = v2 - _pair_mean(v2, first)
    rstd = lax.rsqrt(_pair_mean(d * d, first) + EPS)
    return d * rstd, rstd


def _gm_pair_mix(w_ref, pr, rhs, first):
    return jnp.where(first, jnp.dot(w_ref[2 * pr], rhs, preferred_element_type=F32),
                     jnp.dot(w_ref[2 * pr + 1], rhs, preferred_element_type=F32))


def _gmlp_fwd(z, wt, bs_t, vgain):
    s = z.shape[0]

    def body(gu_ref, gv_ref, wt_ref, bs_ref, vg_ref, o_ref):
        first = _head_lanes()
        for pr in range(N_GROUPS // 2):
            sl = slice(2 * pr * GROUP_DIM, 2 * (pr + 1) * GROUP_DIM)
            vhat, _ = _gm_pair_norm(_gelu(gv_ref[:, sl]), first)
            vn = (vhat * vg_ref[:, sl]).astype(BF16)
            bias = jnp.where(first, bs_ref[:, 2 * pr:2 * pr + 1], bs_ref[:, 2 * pr + 1:2 * pr + 2])
            o_ref[:, sl] = _gelu(gu_ref[:, sl]) * (_gm_pair_mix(wt_ref, pr, vn, first) + bias)

    full = lambda a: pl.BlockSpec(a.shape, lambda n: (0,) * a.ndim)
    return pl.pallas_call(
        body, name="gmlp_fwd", out_shape=jax.ShapeDtypeStruct((s, D_GM), F32), grid=(s // CHUNK,),
        in_specs=[pl.BlockSpec((CHUNK, D_GM), lambda n: (n, 3)), pl.BlockSpec((CHUNK, D_GM), lambda n: (n, 4)),
                  full(wt), full(bs_t), full(vgain)],
        out_specs=pl.BlockSpec((CHUNK, D_GM), lambda n: (n, 0)),
        compiler_params=_cparams("parallel"),
    )(z, z, wt, bs_t, vgain)


def _gmlp_bwd(z, dgm, wt, wt_t, bs_t, vgain):
    s = z.shape[0]

    def body(gu_ref, gv_ref, dgm_ref, wt_ref, wtt_ref, bs_ref, vg_ref, dgu_ref, dgv_ref, dwt_ref, dbs_ref, dvg_ref):
        @pl.when(pl.program_id(0) == 0)
        def _():
            dwt_ref[...] = jnp.zeros_like(dwt_ref)
            dbs_ref[...] = jnp.zeros_like(dbs_ref)
            dvg_ref[...] = jnp.zeros_like(dvg_ref)

        first = _head_lanes()
        for pr in range(N_GROUPS // 2):
            g0, g1 = 2 * pr, 2 * pr + 1
            sl = slice(g0 * GROUP_DIM, (g1 + 1) * GROUP_DIM)
            gu = gu_ref[:, sl]
            gv = gv_ref[:, sl]
            dgm = dgm_ref[:, sl]
            vhat, rstd = _gm_pair_norm(_gelu(gv), first)
            gain = vg_ref[:, sl]
            vn = (vhat * gain).astype(BF16)
            bias = jnp.where(first, bs_ref[:, g0:g0 + 1], bs_ref[:, g1:g1 + 1])
            mixed = _gm_pair_mix(wt_ref, pr, vn, first) + bias
            dgu_ref[:, sl] = (dgm * mixed * _gelu_grad(gu)).astype(BF16)
            dmixed = dgm * _gelu(gu)
            db0, db1 = _pair_sums(dmixed, first)
            dbs_ref[:, g0:g0 + 1] += db0
            dbs_ref[:, g1:g1 + 1] += db1
            dwt_ref[g0] += lax.dot_general(_pick(dmixed, first, 0).astype(BF16), vn, NT, preferred_element_type=F32)
            dwt_ref[g1] += lax.dot_general(_pick(dmixed, first, 1).astype(BF16), vn, NT, preferred_element_type=F32)
            dvn = _gm_pair_mix(wtt_ref, pr, dmixed.astype(BF16), first)
            dvg_ref[:, sl] += _colsum(dvn * vhat)
            dvhat = dvn * gain
            dvf = rstd * (dvhat - _pair_mean(dvhat, first) - vhat * _pair_mean(dvhat * vhat, first))
            dgv_ref[:, sl] = (dvf * _gelu_grad(gv)).astype(BF16)

    full = lambda a: pl.BlockSpec(a.shape, lambda n: (0,) * a.ndim)
    chunk = pl.BlockSpec((CHUNK, D_GM), lambda n: (n, 0))
    return pl.pallas_call(
        body, name="gmlp_bwd",
        out_shape=(jax.ShapeDtypeStruct((s, D_GM), BF16), jax.ShapeDtypeStruct((s, D_GM), BF16),
                   jax.ShapeDtypeStruct(wt.shape, F32), jax.ShapeDtypeStruct(bs_t.shape, F32),
                   jax.ShapeDtypeStruct(vgain.shape, F32)),
        grid=(s // CHUNK,),
        in_specs=[pl.BlockSpec((CHUNK, D_GM), lambda n: (n, 3)), pl.BlockSpec((CHUNK, D_GM), lambda n: (n, 4)),
                  chunk, full(wt), full(wt_t), full(bs_t), full(vgain)],
        out_specs=(chunk, chunk, full(wt), full(bs_t), full(vgain)),
        compiler_params=_cparams("arbitrary"),
    )(z, z, dgm, wt, wt_t, bs_t, vgain)


def _layer_fwd(h0, p_i, w, tq, late):
    s, d = h0.shape
    nq = s // tq
    sv = {"h0": h0}

    (hn1,) = _rowwise(lambda h, g: _rms(h)[0] * g, [h0], [w["mix_pre_norm"]], [(d, BF16)], [], "pre_mix", ROW_TILE)
    z = _mm(hn1, w["w_in"], "nn", F32, "mm_in")
    fl_t = z[:, F_OFF:F_OFF + N_HEADS].T
    c_t = _forget_fwd(fl_t, w["b_forget"])
    qa, ka, vb = _attn_prep(z, c_t.T, ROW_TILE)
    att, lse = _attn_fwd(qa, ka, vb, tq)
    gm = _gmlp_fwd(z, w["wt"], w["bs_t"], w["gm_v_norm"])
    w = dict(w, **late(att))

    def mix_out(att, gm, g):
        return jnp.concatenate([_rms(att)[0] * g[:, :D_ATT], _rms(gm)[0] * g[:, D_ATT:]], axis=-1)

    (mc,) = _rowwise(mix_out, [att, gm], [w["mix_out_norm"]], [(D_ATT + D_GM, BF16)], [], "mix_out", ROW_TILE)
    y1 = _mm(mc, w["w_out"], "nn", F32, "mm_out")

    def post_mix(h0, y1, gpost, gpre):
        h1 = h0 + _rms(y1)[0] * gpost
        return h1, _rms(h1)[0] * gpre

    h1, hn2 = _rowwise(post_mix, [h0, y1], [w["mix_post_norm"], w["ffn_pre_norm"]],
                       [(d, F32), (d, BF16)], [], "post_mix", ROW_TILE)
    ab, t = _ffn_in_swiglu(hn2, w["w_ffn_in"])
    y2 = _mm(t, w["w_ffn_out"], "nn", F32, "mm_ffn_out", a3="k", b3="k")

    def post_ffn(h1, y2, g):
        h2 = h1 + _rms(y2)[0] * g
        return h2, _rms(h2)[0]

    h2, hr = _rowwise(post_ffn, [h1, y2], [w["ffn_post_norm"]], [(d, F32), (d, BF16)], [], "post_ffn", ROW_TILE)
    gl = _mm(hr, w["w_ple_gate"], "nn", F32, "mm_gate")
    pe = _mm(p_i, w["w_ple"], "nn", F32, "mm_ple")
    (h3,) = _rowwise(lambda h2, gl, pe, g: h2 + _sigmoid(gl) * (_rms(pe)[0] * g), [h2, gl, pe], [w["ple_norm"]],
                     [(d, F32)], [], "ple_out", ROW_TILE)
    sv.update(hn1=hn1, z=z, fl_t=fl_t, qa=qa, ka=ka, vb=vb, lse=lse, att=att, gm=gm,
              mc=mc, y1=y1, h1=h1, hn2=hn2, ab=ab, y2=y2, h2=h2, hr=hr, gl=gl, pe=pe, p_i=p_i)
    return h3, sv


def _layer_bwd(dh3, sv, w, tq, mid):
    s, d = dh3.shape
    g = {}
    by_rows = lambda a: a.reshape(N_DEV, -1, a.shape[-1])
    by_cols = lambda a: jnp.stack(jnp.split(a, N_DEV, axis=-1))

    def ple_bwd(dh3, gl, pe, gple):
        gate = _sigmoid(gl)
        pehat, rpe = _rms(pe)
        dgl = dh3 * (pehat * gple) * gate * (1.0 - gate)
        de = dh3 * gate
        return dgl, _rms_bwd(pehat, rpe, de * gple), _colsum(de * pehat)

    dgl, dpe, g["ple_norm"] = _rowwise(ple_bwd, [dh3, sv["gl"], sv["pe"]], [w["ple_norm"]],
                                       [(d, BF16), (d, BF16)], [d], "ple_bwd", ROW_TILE)
    g["w_ple_gate"] = by_rows(_mm(sv["hr"], dgl, "tn", BF16, "mm_dgate"))
    dhr = _mm(dgl, w["w_ple_gate"], "nt", F32, "mm_dhr")
    g["w_ple"] = by_cols(_mm(sv["p_i"], dpe, "tn", BF16, "mm_dple"))

    def ffn_post_bwd(dh3, dhr, h2, y2, gpost):
        h2hat, r2 = _rms(h2)
        dh2 = dh3 + _rms_bwd(h2hat, r2, dhr)
        y2hat, ry = _rms(y2)
        return dh2, _rms_bwd(y2hat, ry, dh2 * gpost), _colsum(dh2 * y2hat)

    dh2, dy2, g["ffn_post_norm"] = _rowwise(ffn_post_bwd, [dh3, dhr, sv["h2"], sv["y2"]], [w["ffn_post_norm"]],
                                            [(d, F32), (d, BF16)], [d], "ffn_post_bwd", ROW_TILE)
    dt = _mm(dy2, w["w_ffn_out"], "nt", F32, "mm_dt", b3="n", o3="n")
    t, dab = _swiglu_bwd(sv["ab"], dt, 2 * ROW_TILE)
    dab = dab.reshape((N_DEV,) + dab.shape[2:])
    g["w_ffn_out"] = by_rows(_mm(t, dy2, "tn", BF16, "mm_dffn_out", a3="m", o3="m"))
    dhn2 = _mm(dab, w["w_ffn_in"], "nt", F32, "mm_dhn2", a3="k", b3="k")
    g["w_ffn_in"] = _mm(sv["hn2"], dab, "tn", BF16, "mm_dffn_in", b3="n", o3="n")

    def mix_post_bwd(dh2, dhn2, h1, y1, gpre, gpost):
        h1hat, r1 = _rms(h1)
        dh1 = dh2 + _rms_bwd(h1hat, r1, dhn2 * gpre)
        y1hat, ry = _rms(y1)
        return dh1, _rms_bwd(y1hat, ry, dh1 * gpost), _colsum(dhn2 * h1hat), _colsum(dh1 * y1hat)

    dh1, dy1, g["ffn_pre_norm"], g["mix_post_norm"] = _rowwise(
        mix_post_bwd, [dh2, dhn2, sv["h1"], sv["y1"]], [w["ffn_pre_norm"], w["mix_post_norm"]],
        [(d, F32), (d, BF16)], [d, d], "mix_post_bwd", ROW_TILE)
    dmc = _mm(dy1, w["w_out"], "nt", F32, "mm_dmc")
    g["w_out"] = by_rows(_mm(sv["mc"], dy1, "tn", BF16, "mm_dout"))
    w = dict(w, **mid(g, dmc))

    def mix_out_bwd(da, dg, att, gm, gain):
        atthat, ra = _rms(att)
        gmhat, rg = _rms(gm)
        dgain = jnp.concatenate([_colsum(da * atthat), _colsum(dg * gmhat)], axis=-1)
        return _rms_bwd(atthat, ra, da * gain[:, :D_ATT]), _rms_bwd(gmhat, rg, dg * gain[:, D_ATT:]), dgain

    datt, dgm, g["mix_out_norm"] = _rowwise(
        mix_out_bwd, [(dmc, 0, D_ATT), (dmc, 1, D_GM), sv["att"], sv["gm"]], [w["mix_out_norm"]],
        [(D_ATT, BF16), (D_GM, F32)], [D_ATT + D_GM], "mix_out_bwd", ROW_TILE)

    dgu, dgv, dwt, dbs_t, g["gm_v_norm"] = _gmlp_bwd(sv["z"], dgm, w["wt"], w["wt_t"], w["bs_t"], w["gm_v_norm"])
    g["gm_w_s"] = dwt * jnp.tril(jnp.ones((CHUNK, CHUNK), F32))[None]
    g["gm_b_s"] = dbs_t.T

    delta = _attn_delta(sv["att"], datt, tq)
    dq, dk, dv, dc_row = _attn_bwd(sv["qa"], sv["ka"], sv["vb"], datt, sv["lse"], delta, tq)
    dfl_t, db = _forget_bwd(dc_row.reshape(N_HEADS, s), sv["fl_t"], w["b_forget"])
    g["b_forget"] = db.reshape(1, N_HEADS)
    dz = jnp.concatenate([dq.astype(BF16), dk.astype(BF16), dv.astype(BF16), dgu, dgv, dfl_t.T.astype(BF16),
                          jnp.zeros((s, D_IN_PAD - F_OFF - N_HEADS), BF16)], axis=-1)
    dhn1 = _mm(dz, w["w_in"], "nt", F32, "mm_dhn1")
    din = _mm(sv["hn1"], dz, "tn", BF16, "mm_din")
    din = jnp.concatenate([din[:, :3 * D_ATT], din[:, F_OFF:F_OFF + N_HEADS], din[:, 3 * D_ATT:F_OFF]], axis=-1)
    g["w_in"] = by_cols(din)

    def mix_pre_bwd(dh1, dhn1, h0, gpre):
        h0hat, r0 = _rms(h0)
        return dh1 + _rms_bwd(h0hat, r0, dhn1 * gpre), _colsum(dhn1 * h0hat)

    dh0, g["mix_pre_norm"] = _rowwise(mix_pre_bwd, [dh1, dhn1, sv["h0"]], [w["mix_pre_norm"]],
                                      [(d, F32)], [d], "mix_pre_bwd", ROW_TILE)
    return dh0, g


ANY = pl.BlockSpec(memory_space=pl.ANY)


def _all_gather(xs, layer, name):
    n = len(xs)

    def body(*refs):
        x_refs, out_refs = refs[:n], refs[n:2 * n]
        send_sems, recv_sems, local_sems = refs[2 * n:]
        x, y, c = lax.axis_index("x"), lax.axis_index("y"), lax.axis_index("c")
        me, sibling = (x, y, c), (x, y, 1 - c)
        chips = [(1 - x, y), (x, 1 - y), (1 - x, 1 - y)]

        def shard(a):
            return x_refs[a] if layer is None else x_refs[a].at[layer]

        def rows(a, px, py, pc):
            return out_refs[a].at[4 * px + 2 * py + pc]

        def copy(a, kk, block, to, from_shard=False):
            return pltpu.make_async_remote_copy(
                src_ref=shard(a) if from_shard else rows(a, *block), dst_ref=rows(a, *block),
                send_sem=send_sems.at[7 * a + kk], recv_sem=recv_sems.at[7 * a + kk],
                device_id=to, device_id_type=MESH)

        mine = [pltpu.make_async_copy(shard(a), rows(a, *me), local_sems.at[a]) for a in range(n)]
        for cp in mine:
            cp.start()
        first = []
        for a in range(n):
            first.append(copy(a, 0, me, sibling, from_shard=True))
            first += [copy(a, 1 + j, me, (*chip, c), from_shard=True) for j, chip in enumerate(chips)]
        for cp in first:
            cp.start()
        passed = []
        for j, chip in enumerate(chips):
            for a in range(n):
                copy(a, 1 + j, (*chip, c), me).wait_recv()
                passed.append(copy(a, 4 + j, (*chip, c), sibling))
                passed[-1].start()
        for a in range(n):
            copy(a, 0, sibling, me).wait_recv()
        for j, chip in enumerate(chips):
            for a in range(n):
                copy(a, 4 + j, (*chip, 1 - c), me).wait_recv()
        for cp in first + passed:
            cp.wait_send()
        for cp in mine:
            cp.wait()

    shapes = [x.shape if layer is None else x.shape[1:] for x in xs]
    return pl.pallas_call(
        body, name=name, out_shape=[jax.ShapeDtypeStruct((N_DEV,) + sh, x.dtype) for sh, x in zip(shapes, xs)],
        in_specs=[ANY] * n, out_specs=[ANY] * n,
        scratch_shapes=[pltpu.SemaphoreType.DMA((7 * n,)), pltpu.SemaphoreType.DMA((7 * n,)),
                        pltpu.SemaphoreType.DMA((n,))],
    )(*xs)


HBM = pl.BlockSpec(memory_space=pltpu.HBM)
SEMS = pl.BlockSpec(memory_space=pltpu.SEMAPHORE)
EFFECT = pltpu.SideEffectType.DATAFLOW_SIDE_EFFECTING
FLIPS = tuple((fx, fy, fc) for fx in (0, 1) for fy in (0, 1) for fc in (0, 1))[1:]


def _exchange_copies(src_refs, land_refs, send_sems, recv_sems, layer, scatter):
    x, y, c = lax.axis_index("x"), lax.axis_index("y"), lax.axis_index("c")
    me = 4 * x + 2 * y + c
    copies = []
    for a, (src, land) in enumerate(zip(src_refs, land_refs)):
        for f, (fx, fy, fc) in enumerate(FLIPS):
            px, py, pc = (1 - x if fx else x), (1 - y if fy else y), (1 - c if fc else c)
            if scatter:
                block = src.at[4 * px + 2 * py + pc]
            else:
                block = src if layer is None else src.at[layer]
            copies.append(pltpu.make_async_remote_copy(
                src_ref=block, dst_ref=land.at[me], send_sem=send_sems.at[7 * a + f], recv_sem=recv_sems.at[7 * a + f],
                device_id=(px, py, pc), device_id_type=MESH))
    return copies


def _exchange_start(srcs, lands, layer, scatter, name):
    n = len(srcs)

    def body(*refs):
        for cp in _exchange_copies(refs[:n], refs[n:2 * n], refs[2 * n], refs[2 * n + 1], layer, scatter):
            cp.start()
        token = refs[-1]
        token[...] = jnp.zeros_like(token)

    operands = list(srcs) + list(lands)
    outs = pl.pallas_call(
        body, name=name,
        out_shape=(pltpu.SemaphoreType.DMA((7 * n,)), pltpu.SemaphoreType.DMA((7 * n,)),
                   *[pltpu.HBM(a.shape, a.dtype) for a in operands], jax.ShapeDtypeStruct((8, LANE), F32)),
        in_specs=[HBM] * (2 * n),
        out_specs=(SEMS, SEMS, *[HBM] * (2 * n), pl.BlockSpec(memory_space=pltpu.VMEM)),
        input_output_aliases={i: 2 + i for i in range(2 * n)},
        compiler_params=pltpu.CompilerParams(has_side_effects=EFFECT),
    )(*[pltpu.with_memory_space_constraint(a, pltpu.HBM) for a in operands])
    return outs[0], outs[1], outs[2:2 + n], outs[2 + n:2 + 2 * n], outs[-1]


def _exchange_wait(started, after, layer, scatter, name):
    send_sems, recv_sems, srcs, lands, _ = started
    n = len(srcs)

    def body(*refs):
        for cp in _exchange_copies(refs[:n], refs[n:2 * n], refs[2 * n], refs[2 * n + 1], layer, scatter):
            cp.wait_send()
            cp.wait_recv()

    operands = list(srcs) + list(lands)
    outs = pl.pallas_call(
        body, name=name, out_shape=tuple(pltpu.HBM(a.shape, a.dtype) for a in operands),
        in_specs=[HBM] * (2 * n) + [SEMS, SEMS, ANY], out_specs=[HBM] * (2 * n),
        input_output_aliases={i: i for i in range(2 * n)},
        compiler_params=pltpu.CompilerParams(has_side_effects=EFFECT),
    )(*operands, send_sems, recv_sems, after)
    return outs[:n], outs[n:]


def _sum_devices(parts):
    _, r, c = parts.shape

    def body(p_ref, o_ref):
        acc = p_ref[0].astype(F32)
        for j in range(1, N_DEV):
            acc = acc + p_ref[j].astype(F32)
        o_ref[...] = acc

    return pl.pallas_call(
        body, name="small_sum", out_shape=jax.ShapeDtypeStruct((r, c), F32), grid=(r // SMALL_ROWS,),
        in_specs=[pl.BlockSpec((N_DEV, SMALL_ROWS, c), lambda i: (0, i, 0))],
        out_specs=pl.BlockSpec((SMALL_ROWS, c), lambda i: (i, 0)),
        compiler_params=_cparams("parallel"),
    )(parts)


def _adamw_math(w, g, m, v):
    m = ADAM_B1 * m + (1.0 - ADAM_B1) * g
    v = ADAM_B2 * v + (1.0 - ADAM_B2) * (g * g)
    m_hat = m / (1.0 - ADAM_B1 ** ADAM_STEP)
    v_hat = v / (1.0 - ADAM_B2 ** ADAM_STEP)
    return -ADAM_LR * (m_hat / (jnp.sqrt(v_hat) + ADAM_EPS) + ADAM_WD * w), m, v


def _adamw_shard(w, m, v, parts, layer, outs, name):
    _, a, b = w.shape
    ta = _tile(a, 256, 16)
    if outs is None:
        outs = [lax.empty(w.shape, F32) for _ in range(4)]

    def body(w_ref, m_ref, v_ref, p_ref, *refs):
        g_ref, d_ref, nm_ref, nv_ref = refs[4:]
        g = p_ref[0].astype(F32)
        for j in range(1, N_DEV):
            g = g + p_ref[j].astype(F32)
        g_ref[0] = g
        d_ref[0], nm_ref[0], nv_ref[0] = _adamw_math(w_ref[0], g, m_ref[0], v_ref[0])

    mine = pl.BlockSpec((1, ta, b), lambda i: (layer, i, 0))
    return pl.pallas_call(
        body, name=name, out_shape=[jax.ShapeDtypeStruct(w.shape, F32)] * 4, grid=(a // ta,),
        in_specs=[mine, mine, mine, pl.BlockSpec((N_DEV, ta, b), lambda i: (0, i, 0))] + [ANY] * 4,
        out_specs=[mine] * 4, input_output_aliases={4 + k: k for k in range(4)},
        compiler_params=_cparams("parallel"),
    )(w, m, v, parts, *outs)


def _pack_small(pieces):
    flat = jnp.concatenate([p.reshape(-1) for p in pieces])
    total = -(-flat.shape[0] // (SMALL_COLS * SMALL_ROWS)) * SMALL_COLS * SMALL_ROWS
    return jnp.pad(flat, (0, total - flat.shape[0])).reshape(-1, SMALL_COLS)


def kernel(x, p, mix_pre_norm, mix_post_norm, w_in, b_forget, gm_v_norm, gm_w_s, gm_b_s, mix_out_norm, w_out, ffn_pre_norm, ffn_post_norm, w_ffn_in, w_ffn_out, w_ple, ple_norm, w_ple_gate, loss_target, m_mix_pre_norm, m_mix_post_norm, m_w_in, m_b_forget, m_gm_v_norm, m_gm_w_s, m_gm_b_s, m_mix_out_norm, m_w_out, m_ffn_pre_norm, m_ffn_post_norm, m_w_ffn_in, m_w_ffn_out, m_w_ple, m_ple_norm, m_w_ple_gate, v_mix_pre_norm, v_mix_post_norm, v_w_in, v_b_forget, v_gm_v_norm, v_gm_w_s, v_gm_b_s, v_mix_out_norm, v_w_out, v_ffn_pre_norm, v_ffn_post_norm, v_w_ffn_in, v_w_ffn_out, v_w_ple, v_ple_norm, v_w_ple_gate):
    given = dict(locals())
    weights = {n: given[n] for n in WEIGHT_ORDER}
    mom_m = {n: given["m_" + n] for n in WEIGHT_ORDER}
    mom_v = {n: given["v_" + n] for n in WEIGHT_ORDER}
    depth = w_in.shape[0]
    s, d = x.shape[1], x.shape[2]
    tq = _tile(s, ATT_BLOCK)
    me = 4 * lax.axis_index("x") + 2 * lax.axis_index("y") + lax.axis_index("c")
    tril = jnp.tril(jnp.ones((CHUNK, CHUNK), F32))

    def landing(block):
        return lax.dynamic_update_index_in_dim(lax.empty((N_DEV,) + block.shape, block.dtype), block, me, 0)

    def mix_weights(i, got):
        w_in_full = jnp.concatenate([got["w_in"][j] for j in range(N_DEV)], axis=-1)
        pad = jnp.zeros((d, D_IN_PAD - D_IN), BF16)
        wt = gm_w_s[i] * tril[None]
        lw = dict(
            w_in=jnp.concatenate([w_in_full[:, :3 * D_ATT], w_in_full[:, 3 * D_ATT + N_HEADS:],
                                  w_in_full[:, 3 * D_ATT:3 * D_ATT + N_HEADS], pad], axis=-1),
            b_forget=b_forget[i][:, None], wt=wt.astype(BF16), wt_t=wt.transpose(0, 2, 1).astype(BF16),
            bs_t=gm_b_s[i].T)
        lw.update({n: weights[n][i][None] for n in ("mix_pre_norm", "mix_post_norm", "gm_v_norm", "mix_out_norm",
                                                    "ffn_pre_norm", "ffn_post_norm", "ple_norm")})
        return lw

    def rest_weights(got):
        return dict(w_out=got["w_out"].reshape(-1, d), w_ffn_in=got["w_ffn_in"],
                    w_ffn_out=got["w_ffn_out"].reshape(N_DEV // 2, -1, d),
                    w_ple=jnp.concatenate([got["w_ple"][j] for j in range(N_DEV)], axis=-1),
                    w_ple_gate=got["w_ple_gate"].reshape(-1, d))

    shards = {n: weights[n].astype(BF16) for n in MATRIX_WEIGHTS}

    def gather_start(i):
        started = {}
        order = jnp.zeros((), BF16)
        for tag, grp in EXCHANGE_GROUPS.items():
            started[tag] = _exchange_start([shards[n] for n in grp], [landing(shards[n][i] + order) for n in grp], i,
                                           False, f"weights_gather_start_{i}_{tag}")
            order = started[tag][4][0, 0].astype(BF16)
        return started

    def gather_finish(i, tag, pending, after):
        srcs, got = _exchange_wait(pending[tag], after, i, False, f"weights_gather_wait_{i}_{tag}")
        shards.update(zip(EXCHANGE_GROUPS[tag], srcs))
        return dict(zip(EXCHANGE_GROUPS[tag], got))

    h = x[0]
    saved, layer_w = [], []
    pending = gather_start(0)
    for i in range(depth):
        lw = mix_weights(i, gather_finish(i, "mix", pending, h))
        if i == 0:
            lw["mix_pre_norm"] = lw["mix_pre_norm"] + pending["rest"][4][:1, :1]
        following = {}

        def late(att, i=i, pending=pending, lw=lw, following=following):
            rest = rest_weights(gather_finish(i, "rest", pending, att))
            lw.update(rest)
            if i + 1 == depth:
                return rest
            following.update(gather_start(i + 1))
            token = following["mix"][4][:1, :1] + following["rest"][4][:1, :1]
            return dict(rest, mix_out_norm=lw["mix_out_norm"] + token)

        h, sv = _layer_fwd(h, p[i, 0], lw, tq, late)
        layer_w.append(lw)
        saved.append(sv)
        pending = following

    def loss_head(y, t):
        err = y - t
        return err * (1.0 / d), _colsum(err * err)

    dh, sq = _rowwise(loss_head, [h, loss_target[0]], [], [(d, F32)], [d], "loss_head", ROW_TILE)
    loss = lax.psum(0.5 * jnp.sum(sq) / d, AXES)

    layer_g = [None] * depth
    shard_out = {n: None for n in MATRIX_WEIGHTS}

    def scatter_start(i, tag, g):
        full_g = [g[n] for n in EXCHANGE_GROUPS[tag]]
        lands = [landing(lax.dynamic_index_in_dim(gf, me, 0, keepdims=False)) for gf in full_g]
        return _exchange_start(full_g, lands, None, True, f"grads_scatter_start_{i}_{tag}")

    def scatter_finish(i, tag, started, after):
        _, parts = _exchange_wait(started[tag], after, None, True, f"grads_scatter_wait_{i}_{tag}")
        for n, part in zip(EXCHANGE_GROUPS[tag], parts):
            shard_out[n] = _adamw_shard(weights[n], mom_m[n], mom_v[n], part, i, shard_out[n], "adamw_" + n)

    before = None
    for i in reversed(range(depth)):
        lw = layer_w[i]
        if before is not None:
            lw = dict(lw, ple_norm=lw["ple_norm"] + before[1]["mix"][4][:1, :1])
        started = {}

        def mid(g, dmc, i=i, before=before, started=started, lw=lw):
            if before is not None:
                scatter_finish(before[0], "rest", before[1], dmc)
            started["rest"] = scatter_start(i, "rest", g)
            return dict(mix_out_norm=lw["mix_out_norm"] + started["rest"][4][:1, :1])

        dh, layer_g[i] = _layer_bwd(dh, saved[i], lw, tq, mid)
        if before is not None:
            scatter_finish(before[0], "mix", before[1], dh)
        started["mix"] = scatter_start(i, "mix", layer_g[i])
        before = (i, started)
    scatter_finish(before[0], "rest", before[1], before[1]["mix"][4])
    scatter_finish(before[0], "mix", before[1], dh)
    grad_x = dh[None]

    grads, deltas, new_m, new_v = {}, {}, {}, {}
    for n in MATRIX_WEIGHTS:
        grads[n], deltas[n], new_m[n], new_v[n] = shard_out[n]

    small_g = _pack_small([jnp.stack([layer_g[i][n].reshape(-1) for i in range(depth)]) for n in SMALL_WEIGHTS])
    (gathered,) = _all_gather([small_g.astype(BF16)], None, "small_grads_all_gather")
    g_small = _sum_devices(gathered)
    pack = lambda t: _pack_small([t[n] for n in SMALL_WEIGHTS])
    dl, nm, nv = _rowwise(_adamw_math, [pack(weights), g_small, pack(mom_m), pack(mom_v)], [],
                          [(SMALL_COLS, F32)] * 3, [], "adamw_small", SMALL_ROWS)
    off = 0
    for n in SMALL_WEIGHTS:
        shp, size = weights[n].shape, weights[n].size
        grads[n], deltas[n], new_m[n], new_v[n] = (a.reshape(-1)[off:off + size].reshape(shp)
                                                   for a in (g_small, dl, nm, nv))
        off += size

    return (loss, grad_x, *[grads[n] for n in WEIGHT_ORDER], *[deltas[n] for n in WEIGHT_ORDER],
            *[new_m[n] for n in WEIGHT_ORDER], *[new_v[n] for n in WEIGHT_ORDER])
```

```python
import functools
import math

import jax
import jax.numpy as jnp
from jax import lax
from jax.experimental import pallas as pl
from jax.experimental.pallas import tpu as pltpu

F32 = jnp.float32
BF16 = jnp.bfloat16
MESH = pl.DeviceIdType.MESH
AXES = ("x", "y", "c")
N_DEV = 8

EPS = 1e-6
NEG_INF = -1e30
N_HEADS = 8
HEAD_DIM = 64
D_ATT = N_HEADS * HEAD_DIM
N_GROUPS = 8
GROUP_DIM = 64
D_GM = N_GROUPS * GROUP_DIM
CHUNK = 128
ATT_SCALE = HEAD_DIM ** -0.5
ATT_BLOCK = 1024
D_IN = 3 * D_ATT + N_HEADS + 2 * D_GM
D_IN_PAD = 3 * D_ATT + 2 * D_GM + 128
F_OFF = 3 * D_ATT + 2 * D_GM

ADAM_LR = 0.001
ADAM_B1 = 0.9
ADAM_B2 = 0.999
ADAM_EPS = 1e-08
ADAM_WD = 0.01
ADAM_STEP = 10

LANE = 128
VMEM_LIMIT = 48 * 1024 * 1024
ROW_TILE = 512
K_TILE = 4096
SMALL_COLS = 128
SMALL_ROWS = 512

MATRIX_WEIGHTS = ("w_in", "w_out", "w_ffn_in", "w_ffn_out", "w_ple", "w_ple_gate")
EXCHANGE_GROUPS = {"mix": ("w_in",), "rest": ("w_out", "w_ffn_in", "w_ffn_out", "w_ple", "w_ple_gate")}
SMALL_WEIGHTS = ("mix_pre_norm", "mix_post_norm", "b_forget", "gm_v_norm", "gm_w_s", "gm_b_s",
                 "mix_out_norm", "ffn_pre_norm", "ffn_post_norm", "ple_norm")
WEIGHT_ORDER = ("mix_pre_norm", "mix_post_norm", "w_in", "b_forget", "gm_v_norm", "gm_w_s", "gm_b_s",
                "mix_out_norm", "w_out", "ffn_pre_norm", "ffn_post_norm", "w_ffn_in", "w_ffn_out",
                "w_ple", "ple_norm", "w_ple_gate")


def _tile(n, pref, unit=LANE):
    best = None
    t = unit
    while t <= min(n, pref):
        if n % t == 0:
            best = t
        t += unit
    return n if best is None else best


def _cparams(*semantics):
    return pltpu.CompilerParams(dimension_semantics=semantics or None, vmem_limit_bytes=VMEM_LIMIT)


NN = (((1,), (0,)), ((), ()))
NT = (((1,), (1,)), ((), ()))
TN = (((0,), (0,)), ((), ()))
_MM_AXES = {
    "nn": ("i", "k", "k", "j"), "nt": ("i", "k", "j", "k"), "tn": ("k", "i", "k", "j")}
_MM_DN = {"nn": NN, "nt": NT, "tn": TN}


def _mm(a, b, dims, out_dtype, name, a3=None, b3=None, o3=None, tm=1024, tn=1024, tk=None):
    ar, ac, br, bc = _MM_AXES[dims]
    letter = {"i": "m", "j": "n", "k": "k"}
    size = {}

    def measure(x, rows, cols, stacked):
        shape = x.shape
        if stacked is None:
            size.setdefault(letter[rows], shape[0])
            size.setdefault(letter[cols], shape[1])
        else:
            for ax, n in ((rows, shape[1]), (cols, shape[2])):
                size.setdefault(letter[ax], n * shape[0] if letter[ax] == stacked else n)

    measure(a, ar, ac, a3)
    measure(b, br, bc, b3)
    m, n, k = size["m"], size["n"], size["k"]
    slab = {}
    for x, stacked, rows, cols in ((a, a3, ar, ac), (b, b3, br, bc)):
        if stacked is not None:
            slab[stacked] = x.shape[1] if letter[rows] == stacked else x.shape[2]
    if o3 is not None:
        slab.setdefault(o3, slab.get(o3, None) or {"m": m, "n": n}[o3] // N_DEV)
    tk = tk or K_TILE
    tile = {"m": slab.get("m") or _tile(m, tm), "n": slab.get("n") or _tile(n, tn), "k": slab.get("k") or _tile(k, tk)}
    group = 1
    if a3 == "k" and b3 == "k":
        group = max(g for g in range(1, a.shape[0] + 1) if a.shape[0] % g == 0 and g * tile["k"] <= max(tk, tile["k"]))
    nk = k // (group * tile["k"])

    def spec(rows, cols, stacked):
        tr, tc = tile[letter[rows]], tile[letter[cols]]
        if stacked is None:
            return pl.BlockSpec((tr, tc), lambda i, j, kk: ({"i": i, "j": j, "k": kk}[rows], {"i": i, "j": j, "k": kk}[cols]))

        def imap(i, j, kk):
            g = {"i": i, "j": j, "k": kk}
            return (g[{"m": "i", "n": "j", "k": "k"}[stacked]],
                    0 if letter[rows] == stacked else g[rows], 0 if letter[cols] == stacked else g[cols])

        return pl.BlockSpec((group if stacked == "k" else 1, tr, tc), imap)

    dn = _MM_DN[dims]

    def body(a_ref, b_ref, o_ref, *acc):
        prod = None
        for g in range(group):
            av = a_ref[...] if a3 is None else a_ref[g]
            bv = b_ref[...] if b3 is None else b_ref[g]
            term = lax.dot_general(av.astype(BF16), bv.astype(BF16), dn, preferred_element_type=F32)
            prod = term if prod is None else prod + term

        def emit(val):
            if o3 is None:
                o_ref[...] = val.astype(out_dtype)
            else:
                o_ref[0] = val.astype(out_dtype)

        if nk == 1:
            emit(prod)
            return
        (acc_ref,) = acc
        kk = pl.program_id(2)

        @pl.when(kk == 0)
        def _():
            acc_ref[...] = prod

        @pl.when(kk > 0)
        def _():
            acc_ref[...] += prod

        @pl.when(kk == nk - 1)
        def _():
            emit(acc_ref[...])

    if o3 is None:
        out_shape = (m, n)
    elif o3 == "m":
        out_shape = (m // tile["m"], tile["m"], n)
    else:
        out_shape = (n // tile["n"], m, tile["n"])
    return pl.pallas_call(
        body, name=name, out_shape=jax.ShapeDtypeStruct(out_shape, out_dtype),
        grid=(m // tile["m"], n // tile["n"], nk),
        in_specs=[spec(ar, ac, a3), spec(br, bc, b3)], out_specs=spec("i", "j", o3),
        scratch_shapes=[] if nk == 1 else [pltpu.VMEM((tile["m"], tile["n"]), F32)],
        compiler_params=_cparams("parallel", "parallel", "arbitrary"),
    )(a, b)


def _rowwise(fn, rows, vecs, outs, reds, name, ts):
    rows = [r if isinstance(r, tuple) else (r, 0, r.shape[1]) for r in rows]
    s = rows[0][0].shape[0]
    ts = _tile(s, ts, 8)
    nr, nv, no = len(rows), len(vecs), len(outs)

    def body(*refs):
        vals = fn(*[r[...] for r in refs[:nr + nv]])
        vals = vals if isinstance(vals, tuple) else (vals,)
        o_refs = refs[nr + nv:nr + nv + no]
        r_refs = refs[nr + nv + no:]
        for o_ref, val in zip(o_refs, vals[:no]):
            o_ref[...] = val.astype(o_ref.dtype)
        if r_refs:
            @pl.when(pl.program_id(0) == 0)
            def _():
                for r_ref in r_refs:
                    r_ref[...] = jnp.zeros_like(r_ref)

            for r_ref, val in zip(r_refs, vals[no:]):
                r_ref[...] += val

    in_specs = [pl.BlockSpec((ts, w), functools.partial(lambda i, cb: (i, cb), cb=cb)) for _, cb, w in rows]
    in_specs += [pl.BlockSpec(v.shape, lambda i: (0, 0)) for v in vecs]
    out_specs = [pl.BlockSpec((ts, c), lambda i: (i, 0)) for c, _ in outs]
    out_specs += [pl.BlockSpec((1, c), lambda i: (0, 0)) for c in reds]
    out_shape = [jax.ShapeDtypeStruct((s, c), dt) for c, dt in outs]
    out_shape += [jax.ShapeDtypeStruct((1, c), F32) for c in reds]
    return pl.pallas_call(
        body, name=name, out_shape=out_shape, grid=(s // ts,), in_specs=in_specs, out_specs=out_specs,
        compiler_params=_cparams("arbitrary" if reds else "parallel"),
    )(*[r[0] for r in rows], *vecs)


def _rms(x):
    r = lax.rsqrt(jnp.mean(x * x, axis=-1, keepdims=True) + EPS)
    return x * r, r


def _rms_bwd(xhat, r, dyg):
    return r * (dyg - xhat * jnp.mean(dyg * xhat, axis=-1, keepdims=True))


def _colsum(x):
    return jnp.sum(x, axis=0, keepdims=True)


def _sigmoid(x):
    return 1.0 / (1.0 + jnp.exp(-x))


GELU_C = math.sqrt(2.0 / math.pi)
GELU_A = 0.044715


def _gelu(x):
    return 0.5 * x * (1.0 + jnp.tanh(GELU_C * (x + GELU_A * x * x * x)))


def _gelu_grad(x):
    t = jnp.tanh(GELU_C * (x + GELU_A * x * x * x))
    return 0.5 * (1.0 + t) + 0.5 * x * (1.0 - t * t) * GELU_C * (1.0 + 3.0 * GELU_A * x * x)


def _ffn_in_swiglu(hn, wg):
    s, d = hn.shape
    g2, _, n = wg.shape
    g = g2 // 2
    tm = _tile(s, 1024)

    def body(h_ref, wa_ref, wb_ref, ab_ref, t_ref):
        hv = h_ref[...]
        a = jnp.dot(hv, wa_ref[0], preferred_element_type=F32)
        b = jnp.dot(hv, wb_ref[0], preferred_element_type=F32)
        ab_ref[0, 0] = a.astype(BF16)
        ab_ref[1, 0] = b.astype(BF16)
        t_ref[0] = (a * _sigmoid(a) * b).astype(BF16)

    return pl.pallas_call(
        body, name="mm_ffn_in_swiglu",
        out_shape=(jax.ShapeDtypeStruct((2, g, s, n), BF16), jax.ShapeDtypeStruct((g, s, n), BF16)),
        grid=(s // tm, g),
        in_specs=[pl.BlockSpec((tm, d), lambda i, j: (i, 0)), pl.BlockSpec((1, d, n), lambda i, j: (j, 0, 0)),
                  pl.BlockSpec((1, d, n), lambda i, j: (j + g, 0, 0))],
        out_specs=(pl.BlockSpec((2, 1, tm, n), lambda i, j: (0, j, i, 0)),
                   pl.BlockSpec((1, tm, n), lambda i, j: (j, i, 0))),
        compiler_params=_cparams("parallel", "parallel"),
    )(hn, wg, wg)


def _swiglu_bwd(ab, dt, ts):
    _, g, s, n = ab.shape
    ts = _tile(s, ts, 8)

    def body(ab_ref, dt_ref, t_ref, dab_ref):
        a = ab_ref[0, 0].astype(F32)
        b = ab_ref[1, 0].astype(F32)
        dt = dt_ref[0]
        sig = _sigmoid(a)
        silu = a * sig
        t_ref[0] = (silu * b).astype(BF16)
        dab_ref[0, 0] = (dt * b * (sig * (1.0 + a * (1.0 - sig)))).astype(BF16)
        dab_ref[1, 0] = (dt * silu).astype(BF16)

    both = pl.BlockSpec((2, 1, ts, n), lambda j, i: (0, j, i, 0))
    one = pl.BlockSpec((1, ts, n), lambda j, i: (j, i, 0))
    return pl.pallas_call(
        body, name="swiglu_bwd",
        out_shape=(jax.ShapeDtypeStruct((g, s, n), BF16), jax.ShapeDtypeStruct((2, g, s, n), BF16)),
        grid=(g, s // ts), in_specs=[both, one], out_specs=(one, both),
        compiler_params=_cparams("parallel", "parallel"),
    )(ab, dt)


def _forget_fwd(fl_t, b_col):
    h, s = fl_t.shape
    nb = s // LANE

    def body(fl_ref, b_ref, c_ref):
        upper = (lax.broadcasted_iota(jnp.int32, (LANE, LANE), 0)
                 <= lax.broadcasted_iota(jnp.int32, (LANE, LANE), 1)).astype(F32)

        def step(i, carry):
            x = fl_ref[i] + b_ref[...]
            lf = jnp.minimum(x, 0.0) - jnp.log(1.0 + jnp.exp(-jnp.abs(x)))
            cs = jnp.dot(lf, upper, precision=lax.Precision.HIGHEST, preferred_element_type=F32) + carry
            c_ref[i] = cs
            return cs[:, LANE - 1:LANE]

        lax.fori_loop(0, nb, step, jnp.zeros((h, 1), F32))

    out = pl.pallas_call(
        body, name="forget_fwd", out_shape=jax.ShapeDtypeStruct((nb, h, LANE), F32),
        compiler_params=_cparams(),
    )(fl_t.reshape(h, nb, LANE).transpose(1, 0, 2), b_col)
    return out.transpose(1, 0, 2).reshape(h, s)


def _forget_bwd(dc_t, fl_t, b_col):
    h, s = fl_t.shape
    nb = s // LANE

    def body(dc_ref, fl_ref, b_ref, dfl_ref, db_ref):
        lower = (lax.broadcasted_iota(jnp.int32, (LANE, LANE), 0)
                 >= lax.broadcasted_iota(jnp.int32, (LANE, LANE), 1)).astype(F32)

        def step(t, carry):
            tail, db = carry
            i = nb - 1 - t
            rc = jnp.dot(dc_ref[i], lower, precision=lax.Precision.HIGHEST, preferred_element_type=F32) + tail
            dfl = rc * (1.0 - _sigmoid(fl_ref[i] + b_ref[...]))
            dfl_ref[i] = dfl
            return rc[:, 0:1], db + jnp.sum(dfl, axis=1, keepdims=True)

        _, db = lax.fori_loop(0, nb, step, (jnp.zeros((h, 1), F32), jnp.zeros((h, 1), F32)))
        db_ref[...] = db

    blocked = lambda a: a.reshape(h, nb, LANE).transpose(1, 0, 2)
    dfl, db = pl.pallas_call(
        body, name="forget_bwd",
        out_shape=(jax.ShapeDtypeStruct((nb, h, LANE), F32), jax.ShapeDtypeStruct((h, 1), F32)),
        compiler_params=_cparams(),
    )(blocked(dc_t), blocked(fl_t), b_col)
    return dfl.transpose(1, 0, 2).reshape(h, s), db


N_PAIRS = N_HEADS // 2


def _causal_mask(t):
    return lax.broadcasted_iota(jnp.int32, (t, t), 0) >= lax.broadcasted_iota(jnp.int32, (t, t), 1)


def _head_lanes():
    return lax.broadcasted_iota(jnp.int32, (1, 2 * HEAD_DIM), 1) < HEAD_DIM


def _pick(x2, first, hh):
    zero = jnp.zeros_like(x2)
    return jnp.where(first, x2, zero) if hh == 0 else jnp.where(first, zero, x2)


BIAS_TERMS = 3


def _attn_prep(z, c, ts):
    s = z.shape[0]
    ts = _tile(s, ts, 16)
    w = 2 * HEAD_DIM

    def body(q_ref, k_ref, v_ref, c_ref, qa_ref, ka_ref, vb_ref):
        lane = lax.broadcasted_iota(jnp.int32, (1, w), 1)
        first = lane < HEAD_DIM
        cv = c_ref[...]
        for h in range(N_HEADS):
            pair = slice((h // 2) * w, (h // 2 + 1) * w)
            qh = q_ref[:, pair] * ATT_SCALE
            kh = k_ref[:, pair]
            if h % 2:
                qh = pltpu.roll(qh, HEAD_DIM, 1)
                kh = pltpu.roll(kh, HEAD_DIM, 1)
            rest = cv[:, h:h + 1]
            q_tail = jnp.zeros((1, w), F32)
            k_tail = jnp.zeros((1, w), F32)
            for t in range(BIAS_TERMS):
                term = rest.astype(BF16).astype(F32)
                rest = rest - term
                q_tail = jnp.where(lane == HEAD_DIM + t, term, jnp.where(lane == HEAD_DIM + BIAS_TERMS + t, 1.0, q_tail))
                k_tail = jnp.where(lane == HEAD_DIM + t, 1.0, jnp.where(lane == HEAD_DIM + BIAS_TERMS + t, -term, k_tail))
            qa_ref[:, h * w:(h + 1) * w] = jnp.where(first, qh, q_tail).astype(BF16)
            ka_ref[:, h * w:(h + 1) * w] = jnp.where(first, kh, k_tail).astype(BF16)
        vb_ref[...] = v_ref[...].astype(BF16)

    col = lambda cb: pl.BlockSpec((ts, D_ATT), lambda i: (i, cb))
    wide = pl.BlockSpec((ts, N_HEADS * w), lambda i: (i, 0))
    return pl.pallas_call(
        body, name="attn_prep",
        out_shape=(jax.ShapeDtypeStruct((s, N_HEADS * w), BF16), jax.ShapeDtypeStruct((s, N_HEADS * w), BF16),
                   jax.ShapeDtypeStruct((s, D_ATT), BF16)),
        grid=(s // ts,), in_specs=[col(0), col(1), col(2), pl.BlockSpec((ts, N_HEADS), lambda i: (i, 0))],
        out_specs=(wide, wide, col(0)),
        compiler_params=_cparams("parallel"),
    )(z, z, z, c)


def _attn_fwd(qa, ka, vb, tq):
    s = qa.shape[0]
    nq = s // tq
    w = 2 * HEAD_DIM

    def body(q_ref, k_ref, v_ref, o_ref, lse_ref):
        i = pl.program_id(1)
        first = _head_lanes()
        q2 = q_ref[...]

        def block(j, carry, masked):
            off = pl.multiple_of(j * tq, tq)
            k2 = k_ref[pl.ds(off, tq), :]
            v2 = v_ref[pl.ds(off, tq), :]
            new = []
            for hh in range(2):
                m, l, acc = carry[hh]
                sc = lax.dot_general(q2[:, hh * w:(hh + 1) * w], k2[:, hh * w:(hh + 1) * w], NT,
                                     preferred_element_type=F32)
                if masked:
                    sc = jnp.where(_causal_mask(tq), sc, NEG_INF)
                m_new = jnp.maximum(m, jnp.max(sc, axis=-1, keepdims=True))
                alpha = jnp.exp(m - m_new)
                p = jnp.exp(sc - m_new)
                l = alpha * l + jnp.sum(p, axis=-1, keepdims=True)
                p_hi = p.astype(BF16)
                p_lo = (p - p_hi.astype(F32)).astype(BF16)
                acc = (alpha * acc + jnp.dot(p_hi, v2, preferred_element_type=F32)
                       + jnp.dot(p_lo, v2, preferred_element_type=F32))
                new.append((m_new, l, acc))
            return tuple(new)

        one = (jnp.full((tq, 1), NEG_INF, F32), jnp.zeros((tq, 1), F32), jnp.zeros((tq, w), F32))
        carry = lax.fori_loop(0, i, lambda j, c: block(j, c, False), (one, one))
        (m0, l0, a0), (m1, l1, a1) = block(i, carry, True)
        o_ref[...] = jnp.where(first, a0 / l0, a1 / l1)
        lse_ref[0] = m0 + jnp.log(l0)
        lse_ref[1] = m1 + jnp.log(l1)

    return pl.pallas_call(
        body, name="attn_fwd",
        out_shape=(jax.ShapeDtypeStruct((s, D_ATT), F32), jax.ShapeDtypeStruct((N_HEADS, s, 1), F32)),
        grid=(N_PAIRS, nq),
        in_specs=[pl.BlockSpec((tq, 2 * w), lambda hp, i: (i, hp)),
                  pl.BlockSpec((s, 2 * w), lambda hp, i: (0, hp)),
                  pl.BlockSpec((s, w), lambda hp, i: (0, hp))],
        out_specs=(pl.BlockSpec((tq, w), lambda hp, i: (i, hp)),
                   pl.BlockSpec((2, tq, 1), lambda hp, i: (hp, i, 0))),
        compiler_params=_cparams("parallel", "parallel"),
    )(qa, ka, vb)


def _attn_delta(o, do, tq):
    s = o.shape[0]
    w = 2 * HEAD_DIM

    def body(o_ref, do_ref, d_ref):
        first = _head_lanes()
        prod = o_ref[...] * do_ref[...].astype(F32)
        d_ref[0] = jnp.sum(_pick(prod, first, 0), axis=-1, keepdims=True)
        d_ref[1] = jnp.sum(_pick(prod, first, 1), axis=-1, keepdims=True)

    blk = pl.BlockSpec((tq, w), lambda hp, i: (i, hp))
    return pl.pallas_call(
        body, name="attn_delta", out_shape=jax.ShapeDtypeStruct((N_HEADS, s, 1), F32), grid=(N_PAIRS, s // tq),
        in_specs=[blk, blk], out_specs=pl.BlockSpec((2, tq, 1), lambda hp, i: (hp, i, 0)),
        compiler_params=_cparams("parallel", "parallel"),
    )(o, do)


def _attn_bwd(qa, ka, vb, do, lse, delta, tq):
    s = qa.shape[0]
    nq = s // tq
    w = 2 * HEAD_DIM

    def body(q_ref, do_ref, lse_ref, dl_ref, k_ref, v_ref, dq_ref, dk_ref, dv_ref, dc_ref, dq_acc):
        j = pl.program_id(1)
        first = _head_lanes()

        @pl.when(j == 0)
        def _():
            dq_acc[...] = jnp.zeros_like(dq_acc)

        k2 = k_ref[...]
        v2 = v_ref[...]

        def step(i, carry, masked):
            off = pl.multiple_of(i * tq, tq)
            rows = pl.ds(off, tq)
            q2 = q_ref[rows, :]
            do2 = do_ref[rows, :]
            new, dqs = [], []
            for hh in range(2):
                dk, dv, dcs = carry[hh]
                qh = q2[:, hh * w:(hh + 1) * w]
                kh = k2[:, hh * w:(hh + 1) * w]
                sc = lax.dot_general(qh, kh, NT, preferred_element_type=F32)
                if masked:
                    sc = jnp.where(_causal_mask(tq), sc, NEG_INF)
                p = jnp.exp(sc - lse_ref[hh, rows, :])
                dv = dv + lax.dot_general(p.astype(BF16), do2, TN, preferred_element_type=F32)
                dp = lax.dot_general(_pick(do2, first, hh), v2, NT, preferred_element_type=F32)
                ds = p * (dp - dl_ref[hh, rows, :])
                dsb = ds.astype(BF16)
                dk = dk + lax.dot_general(dsb, qh, TN, preferred_element_type=F32)
                dqs.append(jnp.dot(dsb, kh, preferred_element_type=F32))
                new.append((dk, dv, dcs + jnp.sum(ds, axis=0, keepdims=True)))
            dq_acc[rows, :] += jnp.where(first, dqs[0], pltpu.roll(dqs[1], HEAD_DIM, 1)) * ATT_SCALE
            return tuple(new)

        one = (jnp.zeros((tq, w), F32), jnp.zeros((tq, w), F32), jnp.zeros((1, tq), F32))
        carry = step(j, (one, one), True)
        (dk0, dv0, dc0), (dk1, dv1, dc1) = lax.fori_loop(j + 1, nq, lambda i, c: step(i, c, False), carry)
        dk_ref[...] = jnp.where(first, dk0, pltpu.roll(dk1, HEAD_DIM, 1)).astype(BF16)
        dv_ref[...] = jnp.where(first, dv0, dv1).astype(BF16)
        dc_ref[0, 0] = -dc0
        dc_ref[1, 0] = -dc1

        @pl.when(j == nq - 1)
        def _():
            dq_ref[...] = dq_acc[...].astype(BF16)

    whole = lambda width: pl.BlockSpec((s, width), lambda hp, j: (0, hp))
    whole_heads = pl.BlockSpec((2, s, 1), lambda hp, j: (hp, 0, 0))
    blk = lambda width: pl.BlockSpec((tq, width), lambda hp, j: (j, hp))
    crow = pl.BlockSpec((2, 1, 1, tq), lambda hp, j: (hp, j, 0, 0))
    return pl.pallas_call(
        body, name="attn_bwd",
        out_shape=(jax.ShapeDtypeStruct((s, D_ATT), BF16), jax.ShapeDtypeStruct((s, D_ATT), BF16),
                   jax.ShapeDtypeStruct((s, D_ATT), BF16), jax.ShapeDtypeStruct((N_HEADS, nq, 1, tq), F32)),
        grid=(N_PAIRS, nq),
        in_specs=[whole(2 * w), whole(w), whole_heads, whole_heads, blk(2 * w), blk(w)],
        out_specs=(whole(w), blk(w), blk(w), crow),
        scratch_shapes=[pltpu.VMEM((s, w), F32)],
        compiler_params=_cparams("parallel", "arbitrary"),
    )(qa, do, lse, delta, ka, vb)


def _pair_sums(x, first):
    total = jnp.sum(x, axis=-1, keepdims=True)
    head = jnp.sum(jnp.where(first, x, 0.0), axis=-1, keepdims=True)
    return head, total - head


def _pair_mean(x, first):
    head, tail = _pair_sums(x, first)
    return jnp.where(first, head, tail) * (1.0 / GROUP_DIM)


def _gm_pair_norm(v2, first):
    d = v2 - _pair_mean(v2, first)
    rstd = lax.rsqrt(_pair_mean(d * d, first) + EPS)
    return d * rstd, rstd


def _gm_pair_mix(w_ref, pr, rhs, first):
    return jnp.where(first, jnp.dot(w_ref[2 * pr], rhs, preferred_element_type=F32),
                     jnp.dot(w_ref[2 * pr + 1], rhs, preferred_element_type=F32))


def _gmlp_fwd(z, wt, bs_t, vgain):
    s = z.shape[0]

    def body(gu_ref, gv_ref, wt_ref, bs_ref, vg_ref, o_ref):
        first = _head_lanes()
        for pr in range(N_GROUPS // 2):
            sl = slice(2 * pr * GROUP_DIM, 2 * (pr + 1) * GROUP_DIM)
            vhat, _ = _gm_pair_norm(_gelu(gv_ref[:, sl]), first)
            vn = (vhat * vg_ref[:, sl]).astype(BF16)
            bias = jnp.where(first, bs_ref[:, 2 * pr:2 * pr + 1], bs_ref[:, 2 * pr + 1:2 * pr + 2])
            o_ref[:, sl] = _gelu(gu_ref[:, sl]) * (_gm_pair_mix(wt_ref, pr, vn, first) + bias)

    full = lambda a: pl.BlockSpec(a.shape, lambda n: (0,) * a.ndim)
    return pl.pallas_call(
        body, name="gmlp_fwd", out_shape=jax.ShapeDtypeStruct((s, D_GM), F32), grid=(s // CHUNK,),
        in_specs=[pl.BlockSpec((CHUNK, D_GM), lambda n: (n, 3)), pl.BlockSpec((CHUNK, D_GM), lambda n: (n, 4)),
                  full(wt), full(bs_t), full(vgain)],
        out_specs=pl.BlockSpec((CHUNK, D_GM), lambda n: (n, 0)),
        compiler_params=_cparams("parallel"),
    )(z, z, wt, bs_t, vgain)


def _gmlp_bwd(z, dgm, wt, wt_t, bs_t, vgain):
    s = z.shape[0]

    def body(gu_ref, gv_ref, dgm_ref, wt_ref, wtt_ref, bs_ref, vg_ref, dgu_ref, dgv_ref, dwt_ref, dbs_ref, dvg_ref):
        @pl.when(pl.program_id(0) == 0)
        def _():
            dwt_ref[...] = jnp.zeros_like(dwt_ref)
            dbs_ref[...] = jnp.zeros_like(dbs_ref)
            dvg_ref[...] = jnp.zeros_like(dvg_ref)

        first = _head_lanes()
        for pr in range(N_GROUPS // 2):
            g0, g1 = 2 * pr, 2 * pr + 1
            sl = slice(g0 * GROUP_DIM, (g1 + 1) * GROUP_DIM)
            gu = gu_ref[:, sl]
            gv = gv_ref[:, sl]
            dgm = dgm_ref[:, sl]
            vhat, rstd = _gm_pair_norm(_gelu(gv), first)
            gain = vg_ref[:, sl]
            vn = (vhat * gain).astype(BF16)
            bias = jnp.where(first, bs_ref[:, g0:g0 + 1], bs_ref[:, g1:g1 + 1])
            mixed = _gm_pair_mix(wt_ref, pr, vn, first) + bias
            dgu_ref[:, sl] = (dgm * mixed * _gelu_grad(gu)).astype(BF16)
            dmixed = dgm * _gelu(gu)
            db0, db1 = _pair_sums(dmixed, first)
            dbs_ref[:, g0:g0 + 1] += db0
            dbs_ref[:, g1:g1 + 1] += db1
            dwt_ref[g0] += lax.dot_general(_pick(dmixed, first, 0).astype(BF16), vn, NT, preferred_element_type=F32)
            dwt_ref[g1] += lax.dot_general(_pick(dmixed, first, 1).astype(BF16), vn, NT, preferred_element_type=F32)
            dvn = _gm_pair_mix(wtt_ref, pr, dmixed.astype(BF16), first)
            dvg_ref[:, sl] += _colsum(dvn * vhat)
            dvhat = dvn * gain
            dvf = rstd * (dvhat - _pair_mean(dvhat, first) - vhat * _pair_mean(dvhat * vhat, first))
            dgv_ref[:, sl] = (dvf * _gelu_grad(gv)).astype(BF16)

    full = lambda a: pl.BlockSpec(a.shape, lambda n: (0,) * a.ndim)
    chunk = pl.BlockSpec((CHUNK, D_GM), lambda n: (n, 0))
    return pl.pallas_call(
        body, name="gmlp_bwd",
        out_shape=(jax.ShapeDtypeStruct((s, D_GM), BF16), jax.ShapeDtypeStruct((s, D_GM), BF16),
                   jax.ShapeDtypeStruct(wt.shape, F32), jax.ShapeDtypeStruct(bs_t.shape, F32),
                   jax.ShapeDtypeStruct(vgain.shape, F32)),
        grid=(s // CHUNK,),
        in_specs=[pl.BlockSpec((CHUNK, D_GM), lambda n: (n, 3)), pl.BlockSpec((CHUNK, D_GM), lambda n: (n, 4)),
                  chunk, full(wt), full(wt_t), full(bs_t), full(vgain)],
        out_specs=(chunk, chunk, full(wt), full(bs_t), full(vgain)),
        compiler_params=_cparams("arbitrary"),
    )(z, z, dgm, wt, wt_t, bs_t, vgain)


def _layer_fwd(h0, p_i, w, tq, late):
    s, d = h0.shape
    nq = s // tq
    sv = {"h0": h0}

    (hn1,) = _rowwise(lambda h, g: _rms(h)[0] * g, [h0], [w["mix_pre_norm"]], [(d, BF16)], [], "pre_mix", ROW_TILE)
    z = _mm(hn1, w["w_in"], "nn", F32, "mm_in")
    fl_t = z[:, F_OFF:F_OFF + N_HEADS].T
    c_t = _forget_fwd(fl_t, w["b_forget"])
    qa, ka, vb = _attn_prep(z, c_t.T, ROW_TILE)
    att, lse = _attn_fwd(qa, ka, vb, tq)
    gm = _gmlp_fwd(z, w["wt"], w["bs_t"], w["gm_v_norm"])
    w = dict(w, **late(att))

    def mix_out(att, gm, g):
        return jnp.concatenate([_rms(att)[0] * g[:, :D_ATT], _rms(gm)[0] * g[:, D_ATT:]], axis=-1)

    (mc,) = _rowwise(mix_out, [att, gm], [w["mix_out_norm"]], [(D_ATT + D_GM, BF16)], [], "mix_out", ROW_TILE)
    y1 = _mm(mc, w["w_out"], "nn", F32, "mm_out")

    def post_mix(h0, y1, gpost, gpre):
        h1 = h0 + _rms(y1)[0] * gpost
        return h1, _rms(h1)[0] * gpre

    h1, hn2 = _rowwise(post_mix, [h0, y1], [w["mix_post_norm"], w["ffn_pre_norm"]],
                       [(d, F32), (d, BF16)], [], "post_mix", ROW_TILE)
    ab, t = _ffn_in_swiglu(hn2, w["w_ffn_in"])
    y2 = _mm(t, w["w_ffn_out"], "nn", F32, "mm_ffn_out", a3="k", b3="k")

    def post_ffn(h1, y2, g):
        h2 = h1 + _rms(y2)[0] * g
        return h2, _rms(h2)[0]

    h2, hr = _rowwise(post_ffn, [h1, y2], [w["ffn_post_norm"]], [(d, F32), (d, BF16)], [], "post_ffn", ROW_TILE)
    gl = _mm(hr, w["w_ple_gate"], "nn", F32, "mm_gate")
    pe = _mm(p_i, w["w_ple"], "nn", F32, "mm_ple")
    (h3,) = _rowwise(lambda h2, gl, pe, g: h2 + _sigmoid(gl) * (_rms(pe)[0] * g), [h2, gl, pe], [w["ple_norm"]],
                     [(d, F32)], [], "ple_out", ROW_TILE)
    sv.update(hn1=hn1, z=z, fl_t=fl_t, qa=qa, ka=ka, vb=vb, lse=lse, att=att, gm=gm,
              mc=mc, y1=y1, h1=h1, hn2=hn2, ab=ab, y2=y2, h2=h2, hr=hr, gl=gl, pe=pe, p_i=p_i)
    return h3, sv


def _layer_bwd(dh3, sv, w, tq, mid):
    s, d = dh3.shape
    g = {}
    by_rows = lambda a: a.reshape(N_DEV, -1, a.shape[-1])
    by_cols = lambda a: jnp.stack(jnp.split(a, N_DEV, axis=-1))

    def ple_bwd(dh3, gl, pe, gple):
        gate = _sigmoid(gl)
        pehat, rpe = _rms(pe)
        dgl = dh3 * (pehat * gple) * gate * (1.0 - gate)
        de = dh3 * gate
        return dgl, _rms_bwd(pehat, rpe, de * gple), _colsum(de * pehat)

    dgl, dpe, g["ple_norm"] = _rowwise(ple_bwd, [dh3, sv["gl"], sv["pe"]], [w["ple_norm"]],
                                       [(d, BF16), (d, BF16)], [d], "ple_bwd", ROW_TILE)
    g["w_ple_gate"] = by_rows(_mm(sv["hr"], dgl, "tn", BF16, "mm_dgate"))
    dhr = _mm(dgl, w["w_ple_gate"], "nt", F32, "mm_dhr")
    g["w_ple"] = by_cols(_mm(sv["p_i"], dpe, "tn", BF16, "mm_dple"))

    def ffn_post_bwd(dh3, dhr, h2, y2, gpost):
        h2hat, r2 = _rms(h2)
        dh2 = dh3 + _rms_bwd(h2hat, r2, dhr)
        y2hat, ry = _rms(y2)
        return dh2, _rms_bwd(y2hat, ry, dh2 * gpost), _colsum(dh2 * y2hat)

    dh2, dy2, g["ffn_post_norm"] = _rowwise(ffn_post_bwd, [dh3, dhr, sv["h2"], sv["y2"]], [w["ffn_post_norm"]],
                                            [(d, F32), (d, BF16)], [d], "ffn_post_bwd", ROW_TILE)
    dt = _mm(dy2, w["w_ffn_out"], "nt", F32, "mm_dt", b3="n", o3="n")
    t, dab = _swiglu_bwd(sv["ab"], dt, 2 * ROW_TILE)
    dab = dab.reshape((N_DEV,) + dab.shape[2:])
    g["w_ffn_out"] = by_rows(_mm(t, dy2, "tn", BF16, "mm_dffn_out", a3="m", o3="m"))
    dhn2 = _mm(dab, w["w_ffn_in"], "nt", F32, "mm_dhn2", a3="k", b3="k")
    g["w_ffn_in"] = _mm(sv["hn2"], dab, "tn", BF16, "mm_dffn_in", b3="n", o3="n")

    def mix_post_bwd(dh2, dhn2, h1, y1, gpre, gpost):
        h1hat, r1 = _rms(h1)
        dh1 = dh2 + _rms_bwd(h1hat, r1, dhn2 * gpre)
        y1hat, ry = _rms(y1)
        return dh1, _rms_bwd(y1hat, ry, dh1 * gpost), _colsum(dhn2 * h1hat), _colsum(dh1 * y1hat)

    dh1, dy1, g["ffn_pre_norm"], g["mix_post_norm"] = _rowwise(
        mix_post_bwd, [dh2, dhn2, sv["h1"], sv["y1"]], [w["ffn_pre_norm"], w["mix_post_norm"]],
        [(d, F32), (d, BF16)], [d, d], "mix_post_bwd", ROW_TILE)
    dmc = _mm(dy1, w["w_out"], "nt", F32, "mm_dmc")
    g["w_out"] = by_rows(_mm(sv["mc"], dy1, "tn", BF16, "mm_dout"))
    w = dict(w, **mid(g, dmc))

    def mix_out_bwd(da, dg, att, gm, gain):
        atthat, ra = _rms(att)
        gmhat, rg = _rms(gm)
        dgain = jnp.concatenate([_colsum(da * atthat), _colsum(dg * gmhat)], axis=-1)
        return _rms_bwd(atthat, ra, da * gain[:, :D_ATT]), _rms_bwd(gmhat, rg, dg * gain[:, D_ATT:]), dgain

    datt, dgm, g["mix_out_norm"] = _rowwise(
        mix_out_bwd, [(dmc, 0, D_ATT), (dmc, 1, D_GM), sv["att"], sv["gm"]], [w["mix_out_norm"]],
        [(D_ATT, BF16), (D_GM, F32)], [D_ATT + D_GM], "mix_out_bwd", ROW_TILE)

    dgu, dgv, dwt, dbs_t, g["gm_v_norm"] = _gmlp_bwd(sv["z"], dgm, w["wt"], w["wt_t"], w["bs_t"], w["gm_v_norm"])
    g["gm_w_s"] = dwt * jnp.tril(jnp.ones((CHUNK, CHUNK), F32))[None]
    g["gm_b_s"] = dbs_t.T

    delta = _attn_delta(sv["att"], datt, tq)
    dq, dk, dv, dc_row = _attn_bwd(sv["qa"], sv["ka"], sv["vb"], datt, sv["lse"], delta, tq)
    dfl_t, db = _forget_bwd(dc_row.reshape(N_HEADS, s), sv["fl_t"], w["b_forget"])
    g["b_forget"] = db.reshape(1, N_HEADS)
    tail = jnp.concatenate([dfl_t.T.astype(BF16), jnp.zeros((s, D_IN_PAD - F_OFF - N_HEADS), BF16)], axis=-1)
    dz = jnp.concatenate([dq, dk, dv, dgu, dgv, tail], axis=-1)
    dhn1 = _mm(dz, w["w_in"], "nt", F32, "mm_dhn1")
    din = _mm(sv["hn1"], dz, "tn", BF16, "mm_din")
    din = jnp.concatenate([din[:, :3 * D_ATT], din[:, F_OFF:F_OFF + N_HEADS], din[:, 3 * D_ATT:F_OFF]], axis=-1)
    g["w_in"] = by_cols(din)

    def mix_pre_bwd(dh1, dhn1, h0, gpre):
        h0hat, r0 = _rms(h0)
        return dh1 + _rms_bwd(h0hat, r0, dhn1 * gpre), _colsum(dhn1 * h0hat)

    dh0, g["mix_pre_norm"] = _rowwise(mix_pre_bwd, [dh1, dhn1, sv["h0"]], [w["mix_pre_norm"]],
                                      [(d, F32)], [d], "mix_pre_bwd", ROW_TILE)
    return dh0, g


ANY = pl.BlockSpec(memory_space=pl.ANY)


def _all_gather(xs, layer, name):
    n = len(xs)

    def body(*refs):
        x_refs, out_refs = refs[:n], refs[n:2 * n]
        send_sems, recv_sems, local_sems = refs[2 * n:]
        x, y, c = lax.axis_index("x"), lax.axis_index("y"), lax.axis_index("c")
        me, sibling = (x, y, c), (x, y, 1 - c)
        chips = [(1 - x, y), (x, 1 - y), (1 - x, 1 - y)]

        def shard(a):
            return x_refs[a] if layer is None else x_refs[a].at[layer]

        def rows(a, px, py, pc):
            return out_refs[a].at[4 * px + 2 * py + pc]

        def copy(a, kk, block, to, from_shard=False):
            return pltpu.make_async_remote_copy(
                src_ref=shard(a) if from_shard else rows(a, *block), dst_ref=rows(a, *block),
                send_sem=send_sems.at[7 * a + kk], recv_sem=recv_sems.at[7 * a + kk],
                device_id=to, device_id_type=MESH)

        mine = [pltpu.make_async_copy(shard(a), rows(a, *me), local_sems.at[a]) for a in range(n)]
        for cp in mine:
            cp.start()
        first = []
        for a in range(n):
            first.append(copy(a, 0, me, sibling, from_shard=True))
            first += [copy(a, 1 + j, me, (*chip, c), from_shard=True) for j, chip in enumerate(chips)]
        for cp in first:
            cp.start()
        passed = []
        for j, chip in enumerate(chips):
            for a in range(n):
                copy(a, 1 + j, (*chip, c), me).wait_recv()
                passed.append(copy(a, 4 + j, (*chip, c), sibling))
                passed[-1].start()
        for a in range(n):
            copy(a, 0, sibling, me).wait_recv()
        for j, chip in enumerate(chips):
            for a in range(n):
                copy(a, 4 + j, (*chip, 1 - c), me).wait_recv()
        for cp in first + passed:
            cp.wait_send()
        for cp in mine:
            cp.wait()

    shapes = [x.shape if layer is None else x.shape[1:] for x in xs]
    return pl.pallas_call(
        body, name=name, out_shape=[jax.ShapeDtypeStruct((N_DEV,) + sh, x.dtype) for sh, x in zip(shapes, xs)],
        in_specs=[ANY] * n, out_specs=[ANY] * n,
        scratch_shapes=[pltpu.SemaphoreType.DMA((7 * n,)), pltpu.SemaphoreType.DMA((7 * n,)),
                        pltpu.SemaphoreType.DMA((n,))],
    )(*xs)


HBM = pl.BlockSpec(memory_space=pltpu.HBM)
SEMS = pl.BlockSpec(memory_space=pltpu.SEMAPHORE)
EFFECT = pltpu.SideEffectType.DATAFLOW_SIDE_EFFECTING
FLIPS = tuple((fx, fy, fc) for fx in (0, 1) for fy in (0, 1) for fc in (0, 1))[1:]


def _exchange_copies(src_refs, land_refs, send_sems, recv_sems, layer, scatter):
    x, y, c = lax.axis_index("x"), lax.axis_index("y"), lax.axis_index("c")
    me = 4 * x + 2 * y + c
    copies = []
    for a, (src, land) in enumerate(zip(src_refs, land_refs)):
        for f, (fx, fy, fc) in enumerate(FLIPS):
            px, py, pc = (1 - x if fx else x), (1 - y if fy else y), (1 - c if fc else c)
            if scatter:
                block = src.at[4 * px + 2 * py + pc]
            else:
                block = src if layer is None else src.at[layer]
            copies.append(pltpu.make_async_remote_copy(
                src_ref=block, dst_ref=land.at[me], send_sem=send_sems.at[7 * a + f], recv_sem=recv_sems.at[7 * a + f],
                device_id=(px, py, pc), device_id_type=MESH))
    return copies


def _exchange_start(srcs, lands, layer, scatter, name):
    n = len(srcs)

    def body(*refs):
        for cp in _exchange_copies(refs[:n], refs[n:2 * n], refs[2 * n], refs[2 * n + 1], layer, scatter):
            cp.start()
        token = refs[-1]
        token[...] = jnp.zeros_like(token)

    operands = list(srcs) + list(lands)
    outs = pl.pallas_call(
        body, name=name,
        out_shape=(pltpu.SemaphoreType.DMA((7 * n,)), pltpu.SemaphoreType.DMA((7 * n,)),
                   *[pltpu.HBM(a.shape, a.dtype) for a in operands], jax.ShapeDtypeStruct((8, LANE), F32)),
        in_specs=[HBM] * (2 * n),
        out_specs=(SEMS, SEMS, *[HBM] * (2 * n), pl.BlockSpec(memory_space=pltpu.VMEM)),
        input_output_aliases={i: 2 + i for i in range(2 * n)},
        compiler_params=pltpu.CompilerParams(has_side_effects=EFFECT),
    )(*[pltpu.with_memory_space_constraint(a, pltpu.HBM) for a in operands])
    return outs[0], outs[1], outs[2:2 + n], outs[2 + n:2 + 2 * n], outs[-1]


def _exchange_wait(started, after, layer, scatter, name):
    send_sems, recv_sems, srcs, lands, _ = started
    n = len(srcs)

    def body(*refs):
        for cp in _exchange_copies(refs[:n], refs[n:2 * n], refs[2 * n], refs[2 * n + 1], layer, scatter):
            cp.wait_send()
            cp.wait_recv()

    operands = list(srcs) + list(lands)
    outs = pl.pallas_call(
        body, name=name, out_shape=tuple(pltpu.HBM(a.shape, a.dtype) for a in operands),
        in_specs=[HBM] * (2 * n) + [SEMS, SEMS, ANY], out_specs=[HBM] * (2 * n),
        input_output_aliases={i: i for i in range(2 * n)},
        compiler_params=pltpu.CompilerParams(has_side_effects=EFFECT),
    )(*operands, send_sems, recv_sems, after)
    return outs[:n], outs[n:]


def _sum_devices(parts):
    _, r, c = parts.shape

    def body(p_ref, o_ref):
        acc = p_ref[0].astype(F32)
        for j in range(1, N_DEV):
            acc = acc + p_ref[j].astype(F32)
        o_ref[...] = acc

    return pl.pallas_call(
        body, name="small_sum", out_shape=jax.ShapeDtypeStruct((r, c), F32), grid=(r // SMALL_ROWS,),
        in_specs=[pl.BlockSpec((N_DEV, SMALL_ROWS, c), lambda i: (0, i, 0))],
        out_specs=pl.BlockSpec((SMALL_ROWS, c), lambda i: (i, 0)),
        compiler_params=_cparams("parallel"),
    )(parts)


def _adamw_math(w, g, m, v):
    m = ADAM_B1 * m + (1.0 - ADAM_B1) * g
    v = ADAM_B2 * v + (1.0 - ADAM_B2) * (g * g)
    m_hat = m / (1.0 - ADAM_B1 ** ADAM_STEP)
    v_hat = v / (1.0 - ADAM_B2 ** ADAM_STEP)
    return -ADAM_LR * (m_hat / (jnp.sqrt(v_hat) + ADAM_EPS) + ADAM_WD * w), m, v


def _adamw_shard(w, m, v, parts, layer, outs, name):
    _, a, b = w.shape
    ta = _tile(a, 256, 16)
    if outs is None:
        outs = [lax.empty(w.shape, F32) for _ in range(4)]

    def body(w_ref, m_ref, v_ref, p_ref, *refs):
        g_ref, d_ref, nm_ref, nv_ref = refs[4:]
        g = p_ref[0].astype(F32)
        for j in range(1, N_DEV):
            g = g + p_ref[j].astype(F32)
        g_ref[0] = g
        d_ref[0], nm_ref[0], nv_ref[0] = _adamw_math(w_ref[0], g, m_ref[0], v_ref[0])

    mine = pl.BlockSpec((1, ta, b), lambda i: (layer, i, 0))
    return pl.pallas_call(
        body, name=name, out_shape=[jax.ShapeDtypeStruct(w.shape, F32)] * 4, grid=(a // ta,),
        in_specs=[mine, mine, mine, pl.BlockSpec((N_DEV, ta, b), lambda i: (0, i, 0))] + [ANY] * 4,
        out_specs=[mine] * 4, input_output_aliases={4 + k: k for k in range(4)},
        compiler_params=_cparams("parallel"),
    )(w, m, v, parts, *outs)


def _pack_small(pieces):
    flat = jnp.concatenate([p.reshape(-1) for p in pieces])
    total = -(-flat.shape[0] // (SMALL_COLS * SMALL_ROWS)) * SMALL_COLS * SMALL_ROWS
    return jnp.pad(flat, (0, total - flat.shape[0])).reshape(-1, SMALL_COLS)


def kernel(x, p, mix_pre_norm, mix_post_norm, w_in, b_forget, gm_v_norm, gm_w_s, gm_b_s, mix_out_norm, w_out, ffn_pre_norm, ffn_post_norm, w_ffn_in, w_ffn_out, w_ple, ple_norm, w_ple_gate, loss_target, m_mix_pre_norm, m_mix_post_norm, m_w_in, m_b_forget, m_gm_v_norm, m_gm_w_s, m_gm_b_s, m_mix_out_norm, m_w_out, m_ffn_pre_norm, m_ffn_post_norm, m_w_ffn_in, m_w_ffn_out, m_w_ple, m_ple_norm, m_w_ple_gate, v_mix_pre_norm, v_mix_post_norm, v_w_in, v_b_forget, v_gm_v_norm, v_gm_w_s, v_gm_b_s, v_mix_out_norm, v_w_out, v_ffn_pre_norm, v_ffn_post_norm, v_w_ffn_in, v_w_ffn_out, v_w_ple, v_ple_norm, v_w_ple_gate):
    given = dict(locals())
    weights = {n: given[n] for n in WEIGHT_ORDER}
    mom_m = {n: given["m_" + n] for n in WEIGHT_ORDER}
    mom_v = {n: given["v_" + n] for n in WEIGHT_ORDER}
    depth = w_in.shape[0]
    s, d = x.shape[1], x.shape[2]
    tq = _tile(s, ATT_BLOCK)
    me = 4 * lax.axis_index("x") + 2 * lax.axis_index("y") + lax.axis_index("c")
    tril = jnp.tril(jnp.ones((CHUNK, CHUNK), F32))

    def landing(block):
        return lax.dynamic_update_index_in_dim(lax.empty((N_DEV,) + block.shape, block.dtype), block, me, 0)

    def mix_weights(i, got):
        w_in_full = jnp.concatenate([got["w_in"][j] for j in range(N_DEV)], axis=-1)
        pad = jnp.zeros((d, D_IN_PAD - D_IN), BF16)
        wt = gm_w_s[i] * tril[None]
        lw = dict(
            w_in=jnp.concatenate([w_in_full[:, :3 * D_ATT], w_in_full[:, 3 * D_ATT + N_HEADS:],
                                  w_in_full[:, 3 * D_ATT:3 * D_ATT + N_HEADS], pad], axis=-1),
            b_forget=b_forget[i][:, None], wt=wt.astype(BF16), wt_t=wt.transpose(0, 2, 1).astype(BF16),
            bs_t=gm_b_s[i].T)
        lw.update({n: weights[n][i][None] for n in ("mix_pre_norm", "mix_post_norm", "gm_v_norm", "mix_out_norm",
                                                    "ffn_pre_norm", "ffn_post_norm", "ple_norm")})
        return lw

    def rest_weights(got):
        return dict(w_out=got["w_out"].reshape(-1, d), w_ffn_in=got["w_ffn_in"],
                    w_ffn_out=got["w_ffn_out"].reshape(N_DEV // 2, -1, d),
                    w_ple=jnp.concatenate([got["w_ple"][j] for j in range(N_DEV)], axis=-1),
                    w_ple_gate=got["w_ple_gate"].reshape(-1, d))

    shards = {n: weights[n].astype(BF16) for n in MATRIX_WEIGHTS}

    def gather_start(i):
        started = {}
        order = jnp.zeros((), BF16)
        for tag, grp in EXCHANGE_GROUPS.items():
            started[tag] = _exchange_start([shards[n] for n in grp], [landing(shards[n][i] + order) for n in grp], i,
                                           False, f"weights_gather_start_{i}_{tag}")
            order = started[tag][4][0, 0].astype(BF16)
        return started

    def gather_finish(i, tag, pending, after):
        srcs, got = _exchange_wait(pending[tag], after, i, False, f"weights_gather_wait_{i}_{tag}")
        shards.update(zip(EXCHANGE_GROUPS[tag], srcs))
        return dict(zip(EXCHANGE_GROUPS[tag], got))

    h = x[0]
    saved, layer_w = [], []
    pending = gather_start(0)
    for i in range(depth):
        lw = mix_weights(i, gather_finish(i, "mix", pending, h))
        if i == 0:
            lw["mix_pre_norm"] = lw["mix_pre_norm"] + pending["rest"][4][:1, :1]
        following = {}

        def late(att, i=i, pending=pending, lw=lw, following=following):
            rest = rest_weights(gather_finish(i, "rest", pending, att))
            lw.update(rest)
            if i + 1 == depth:
                return rest
            following.update(gather_start(i + 1))
            token = following["mix"][4][:1, :1] + following["rest"][4][:1, :1]
            return dict(rest, mix_out_norm=lw["mix_out_norm"] + token)

        h, sv = _layer_fwd(h, p[i, 0], lw, tq, late)
        layer_w.append(lw)
        saved.append(sv)
        pending = following

    def loss_head(y, t):
        err = y - t
        return err * (1.0 / d), _colsum(err * err)

    dh, sq = _rowwise(loss_head, [h, loss_target[0]], [], [(d, F32)], [d], "loss_head", ROW_TILE)
    loss = lax.psum(0.5 * jnp.sum(sq) / d, AXES)

    layer_g = [None] * depth
    shard_out = {n: None for n in MATRIX_WEIGHTS}

    def scatter_start(i, tag, g):
        full_g = [g[n] for n in EXCHANGE_GROUPS[tag]]
        lands = [landing(lax.dynamic_index_in_dim(gf, me, 0, keepdims=False)) for gf in full_g]
        return _exchange_start(full_g, lands, None, True, f"grads_scatter_start_{i}_{tag}")

    def scatter_finish(i, tag, started, after):
        _, parts = _exchange_wait(started[tag], after, None, True, f"grads_scatter_wait_{i}_{tag}")
        for n, part in zip(EXCHANGE_GROUPS[tag], parts):
            shard_out[n] = _adamw_shard(weights[n], mom_m[n], mom_v[n], part, i, shard_out[n], "adamw_" + n)

    before = None
    for i in reversed(range(depth)):
        lw = layer_w[i]
        if before is not None:
            lw = dict(lw, ple_norm=lw["ple_norm"] + before[1]["mix"][4][:1, :1])
        started = {}

        def mid(g, dmc, i=i, before=before, started=started, lw=lw):
            if before is not None:
                scatter_finish(before[0], "rest", before[1], dmc)
            started["rest"] = scatter_start(i, "rest", g)
            return dict(mix_out_norm=lw["mix_out_norm"] + started["rest"][4][:1, :1])

        dh, layer_g[i] = _layer_bwd(dh, saved[i], lw, tq, mid)
        if before is not None:
            scatter_finish(before[0], "mix", before[1], dh)
        started["mix"] = scatter_start(i, "mix", layer_g[i])
        before = (i, started)
    scatter_finish(before[0], "rest", before[1], before[1]["mix"][4])
    scatter_finish(before[0], "mix", before[1], dh)
    grad_x = dh[None]

    grads, deltas, new_m, new_v = {}, {}, {}, {}
    for n in MATRIX_WEIGHTS:
        grads[n], deltas[n], new_m[n], new_v[n] = shard_out[n]

    small_g = _pack_small([jnp.stack([layer_g[i][n].reshape(-1) for i in range(depth)]) for n in SMALL_WEIGHTS])
    (gathered,) = _all_gather([small_g.astype(BF16)], None, "small_grads_all_gather")
    g_small = _sum_devices(gathered)
    pack = lambda t: _pack_small([t[n] for n in SMALL_WEIGHTS])
    dl, nm, nv = _rowwise(_adamw_math, [pack(weights), g_small, pack(mom_m), pack(mom_v)], [],
                          [(SMALL_COLS, F32)] * 3, [], "adamw_small", SMALL_ROWS)
    off = 0
    for n in SMALL_WEIGHTS:
        shp, size = weights[n].shape, weights[n].size
        grads[n], deltas[n], new_m[n], new_v[n] = (a.reshape(-1)[off:off + size].reshape(shp)
                                                   for a in (g_small, dl, nm, nv))
        off += size

    return (loss, grad_x, *[grads[n] for n in WEIGHT_ORDER], *[deltas[n] for n in WEIGHT_ORDER],
            *[new_m[n] for n in WEIGHT_ORDER], *[new_v[n] for n in WEIGHT_ORDER])
```

```python
import functools
import math

import jax
import jax.numpy as jnp
from jax import lax
from jax.experimental import pallas as pl
from jax.experimental.pallas import tpu as pltpu

F32 = jnp.float32
BF16 = jnp.bfloat16
MESH = pl.DeviceIdType.MESH
AXES = ("x", "y", "c")
N_DEV = 8

EPS = 1e-6
NEG_INF = -1e30
N_HEADS = 8
HEAD_DIM = 64
D_ATT = N_HEADS * HEAD_DIM
N_GROUPS = 8
GROUP_DIM = 64
D_GM = N_GROUPS * GROUP_DIM
CHUNK = 128
ATT_SCALE = HEAD_DIM ** -0.5
ATT_BLOCK = 1024
D_IN = 3 * D_ATT + N_HEADS + 2 * D_GM
D_IN_PAD = 3 * D_ATT + 2 * D_GM + 128
F_OFF = 3 * D_ATT + 2 * D_GM

ADAM_LR = 0.001
ADAM_B1 = 0.9
ADAM_B2 = 0.999
ADAM_EPS = 1e-08
ADAM_WD = 0.01
ADAM_STEP = 10

LANE = 128
VMEM_LIMIT = 48 * 1024 * 1024
ROW_TILE = 512
K_TILE = 4096
SMALL_COLS = 128
SMALL_ROWS = 512

MATRIX_WEIGHTS = ("w_in", "w_out", "w_ffn_in", "w_ffn_out", "w_ple", "w_ple_gate")
EXCHANGE_GROUPS = {"mix": ("w_in",), "rest": ("w_out", "w_ffn_in", "w_ffn_out", "w_ple", "w_ple_gate")}
SMALL_WEIGHTS = ("mix_pre_norm", "mix_post_norm", "b_forget", "gm_v_norm", "gm_w_s", "gm_b_s",
                 "mix_out_norm", "ffn_pre_norm", "ffn_post_norm", "ple_norm")
WEIGHT_ORDER = ("mix_pre_norm", "mix_post_norm", "w_in", "b_forget", "gm_v_norm", "gm_w_s", "gm_b_s",
                "mix_out_norm", "w_out", "ffn_pre_norm", "ffn_post_norm", "w_ffn_in", "w_ffn_out",
                "w_ple", "ple_norm", "w_ple_gate")


def _tile(n, pref, unit=LANE):
    best = None
    t = unit
    while t <= min(n, pref):
        if n % t == 0:
            best = t
        t += unit
    return n if best is None else best


def _cparams(*semantics):
    return pltpu.CompilerParams(dimension_semantics=semantics or None, vmem_limit_bytes=VMEM_LIMIT)


NN = (((1,), (0,)), ((), ()))
NT = (((1,), (1,)), ((), ()))
TN = (((0,), (0,)), ((), ()))
_MM_AXES = {
    "nn": ("i", "k", "k", "j"), "nt": ("i", "k", "j", "k"), "tn": ("k", "i", "k", "j")}
_MM_DN = {"nn": NN, "nt": NT, "tn": TN}


def _mm(a, b, dims, out_dtype, name, a3=None, b3=None, o3=None, tm=1024, tn=1024, tk=None):
    ar, ac, br, bc = _MM_AXES[dims]
    letter = {"i": "m", "j": "n", "k": "k"}
    size = {}

    def measure(x, rows, cols, stacked):
        shape = x.shape
        if stacked is None:
            size.setdefault(letter[rows], shape[0])
            size.setdefault(letter[cols], shape[1])
        else:
            for ax, n in ((rows, shape[1]), (cols, shape[2])):
                size.setdefault(letter[ax], n * shape[0] if letter[ax] == stacked else n)

    measure(a, ar, ac, a3)
    measure(b, br, bc, b3)
    m, n, k = size["m"], size["n"], size["k"]
    slab = {}
    for x, stacked, rows, cols in ((a, a3, ar, ac), (b, b3, br, bc)):
        if stacked is not None:
            slab[stacked] = x.shape[1] if letter[rows] == stacked else x.shape[2]
    if o3 is not None:
        slab.setdefault(o3, slab.get(o3, None) or {"m": m, "n": n}[o3] // N_DEV)
    tk = tk or K_TILE
    tile = {"m": slab.get("m") or _tile(m, tm), "n": slab.get("n") or _tile(n, tn), "k": slab.get("k") or _tile(k, tk)}
    group = 1
    if a3 == "k" and b3 == "k":
        group = max(g for g in range(1, a.shape[0] + 1) if a.shape[0] % g == 0 and g * tile["k"] <= max(tk, tile["k"]))
    nk = k // (group * tile["k"])

    def spec(rows, cols, stacked):
        tr, tc = tile[letter[rows]], tile[letter[cols]]
        if stacked is None:
            return pl.BlockSpec((tr, tc), lambda i, j, kk: ({"i": i, "j": j, "k": kk}[rows], {"i": i, "j": j, "k": kk}[cols]))

        def imap(i, j, kk):
            g = {"i": i, "j": j, "k": kk}
            return (g[{"m": "i", "n": "j", "k": "k"}[stacked]],
                    0 if letter[rows] == stacked else g[rows], 0 if letter[cols] == stacked else g[cols])

        return pl.BlockSpec((group if stacked == "k" else 1, tr, tc), imap)

    dn = _MM_DN[dims]

    def body(a_ref, b_ref, o_ref, *acc):
        prod = None
        for g in range(group):
            av = a_ref[...] if a3 is None else a_ref[g]
            bv = b_ref[...] if b3 is None else b_ref[g]
            term = lax.dot_general(av.astype(BF16), bv.astype(BF16), dn, preferred_element_type=F32)
            prod = term if prod is None else prod + term

        def emit(val):
            if o3 is None:
                o_ref[...] = val.astype(out_dtype)
            else:
                o_ref[0] = val.astype(out_dtype)

        if nk == 1:
            emit(prod)
            return
        (acc_ref,) = acc
        kk = pl.program_id(2)

        @pl.when(kk == 0)
        def _():
            acc_ref[...] = prod

        @pl.when(kk > 0)
        def _():
            acc_ref[...] += prod

        @pl.when(kk == nk - 1)
        def _():
            emit(acc_ref[...])

    if o3 is None:
        out_shape = (m, n)
    elif o3 == "m":
        out_shape = (m // tile["m"], tile["m"], n)
    else:
        out_shape = (n // tile["n"], m, tile["n"])
    return pl.pallas_call(
        body, name=name, out_shape=jax.ShapeDtypeStruct(out_shape, out_dtype),
        grid=(m // tile["m"], n // tile["n"], nk),
        in_specs=[spec(ar, ac, a3), spec(br, bc, b3)], out_specs=spec("i", "j", o3),
        scratch_shapes=[] if nk == 1 else [pltpu.VMEM((tile["m"], tile["n"]), F32)],
        compiler_params=_cparams("parallel", "parallel", "arbitrary"),
    )(a, b)


def _rowwise(fn, rows, vecs, outs, reds, name, ts):
    rows = [r if isinstance(r, tuple) else (r, 0, r.shape[1]) for r in rows]
    s = rows[0][0].shape[0]
    ts = _tile(s, ts, 8)
    nr, nv, no = len(rows), len(vecs), len(outs)

    def body(*refs):
        vals = fn(*[r[...] for r in refs[:nr + nv]])
        vals = vals if isinstance(vals, tuple) else (vals,)
        o_refs = refs[nr + nv:nr + nv + no]
        r_refs = refs[nr + nv + no:]
        for o_ref, val in zip(o_refs, vals[:no]):
            o_ref[...] = val.astype(o_ref.dtype)
        if r_refs:
            @pl.when(pl.program_id(0) == 0)
            def _():
                for r_ref in r_refs:
                    r_ref[...] = jnp.zeros_like(r_ref)

            for r_ref, val in zip(r_refs, vals[no:]):
                r_ref[...] += val

    in_specs = [pl.BlockSpec((ts, w), functools.partial(lambda i, cb: (i, cb), cb=cb)) for _, cb, w in rows]
    in_specs += [pl.BlockSpec(v.shape, lambda i: (0, 0)) for v in vecs]
    out_specs = [pl.BlockSpec((ts, c), lambda i: (i, 0)) for c, _ in outs]
    out_specs += [pl.BlockSpec((1, c), lambda i: (0, 0)) for c in reds]
    out_shape = [jax.ShapeDtypeStruct((s, c), dt) for c, dt in outs]
    out_shape += [jax.ShapeDtypeStruct((1, c), F32) for c in reds]
    return pl.pallas_call(
        body, name=name, out_shape=out_shape, grid=(s // ts,), in_specs=in_specs, out_specs=out_specs,
        compiler_params=_cparams("arbitrary" if reds else "parallel"),
    )(*[r[0] for r in rows], *vecs)


def _rms(x):
    r = lax.rsqrt(jnp.mean(x * x, axis=-1, keepdims=True) + EPS)
    return x * r, r


def _rms_bwd(xhat, r, dyg):
    return r * (dyg - xhat * jnp.mean(dyg * xhat, axis=-1, keepdims=True))


def _colsum(x):
    return jnp.sum(x, axis=0, keepdims=True)


def _sigmoid(x):
    return 1.0 / (1.0 + jnp.exp(-x))


GELU_C = math.sqrt(2.0 / math.pi)
GELU_A = 0.044715


def _gelu(x):
    return 0.5 * x * (1.0 + jnp.tanh(GELU_C * (x + GELU_A * x * x * x)))


def _gelu_grad(x):
    t = jnp.tanh(GELU_C * (x + GELU_A * x * x * x))
    return 0.5 * (1.0 + t) + 0.5 * x * (1.0 - t * t) * GELU_C * (1.0 + 3.0 * GELU_A * x * x)


def _ffn_in_swiglu(hn, wg):
    s, d = hn.shape
    g2, _, n = wg.shape
    g = g2 // 2
    tm = _tile(s, 1024)

    def body(h_ref, wa_ref, wb_ref, ab_ref, t_ref):
        hv = h_ref[...]
        a = jnp.dot(hv, wa_ref[0], preferred_element_type=F32)
        b = jnp.dot(hv, wb_ref[0], preferred_element_type=F32)
        ab_ref[0, 0] = a.astype(BF16)
        ab_ref[1, 0] = b.astype(BF16)
        t_ref[0] = (a * _sigmoid(a) * b).astype(BF16)

    return pl.pallas_call(
        body, name="mm_ffn_in_swiglu",
        out_shape=(jax.ShapeDtypeStruct((2, g, s, n), BF16), jax.ShapeDtypeStruct((g, s, n), BF16)),
        grid=(s // tm, g),
        in_specs=[pl.BlockSpec((tm, d), lambda i, j: (i, 0)), pl.BlockSpec((1, d, n), lambda i, j: (j, 0, 0)),
                  pl.BlockSpec((1, d, n), lambda i, j: (j + g, 0, 0))],
        out_specs=(pl.BlockSpec((2, 1, tm, n), lambda i, j: (0, j, i, 0)),
                   pl.BlockSpec((1, tm, n), lambda i, j: (j, i, 0))),
        compiler_params=_cparams("parallel", "parallel"),
    )(hn, wg, wg)


def _swiglu_bwd(ab, dt, ts):
    _, g, s, n = ab.shape
    ts = _tile(s, ts, 8)

    def body(ab_ref, dt_ref, t_ref, dab_ref):
        a = ab_ref[0, 0].astype(F32)
        b = ab_ref[1, 0].astype(F32)
        dt = dt_ref[0].astype(F32)
        sig = _sigmoid(a)
        silu = a * sig
        t_ref[0] = (silu * b).astype(BF16)
        dab_ref[0, 0] = (dt * b * (sig * (1.0 + a * (1.0 - sig)))).astype(BF16)
        dab_ref[1, 0] = (dt * silu).astype(BF16)

    both = pl.BlockSpec((2, 1, ts, n), lambda j, i: (0, j, i, 0))
    one = pl.BlockSpec((1, ts, n), lambda j, i: (j, i, 0))
    return pl.pallas_call(
        body, name="swiglu_bwd",
        out_shape=(jax.ShapeDtypeStruct((g, s, n), BF16), jax.ShapeDtypeStruct((2, g, s, n), BF16)),
        grid=(g, s // ts), in_specs=[both, one], out_specs=(one, both),
        compiler_params=_cparams("parallel", "parallel"),
    )(ab, dt)


def _forget_fwd(fl_t, b_col):
    h, s = fl_t.shape
    nb = s // LANE

    def body(fl_ref, b_ref, c_ref):
        upper = (lax.broadcasted_iota(jnp.int32, (LANE, LANE), 0)
                 <= lax.broadcasted_iota(jnp.int32, (LANE, LANE), 1)).astype(F32)

        def step(i, carry):
            x = fl_ref[i] + b_ref[...]
            lf = jnp.minimum(x, 0.0) - jnp.log(1.0 + jnp.exp(-jnp.abs(x)))
            cs = jnp.dot(lf, upper, precision=lax.Precision.HIGHEST, preferred_element_type=F32) + carry
            c_ref[i] = cs
            return cs[:, LANE - 1:LANE]

        lax.fori_loop(0, nb, step, jnp.zeros((h, 1), F32))

    out = pl.pallas_call(
        body, name="forget_fwd", out_shape=jax.ShapeDtypeStruct((nb, h, LANE), F32),
        compiler_params=_cparams(),
    )(fl_t.reshape(h, nb, LANE).transpose(1, 0, 2), b_col)
    return out.transpose(1, 0, 2).reshape(h, s)


def _forget_bwd(dc_t, fl_t, b_col):
    h, s = fl_t.shape
    nb = s // LANE

    def body(dc_ref, fl_ref, b_ref, dfl_ref, db_ref):
        lower = (lax.broadcasted_iota(jnp.int32, (LANE, LANE), 0)
                 >= lax.broadcasted_iota(jnp.int32, (LANE, LANE), 1)).astype(F32)

        def step(t, carry):
            tail, db = carry
            i = nb - 1 - t
            rc = jnp.dot(dc_ref[i], lower, precision=lax.Precision.HIGHEST, preferred_element_type=F32) + tail
            dfl = rc * (1.0 - _sigmoid(fl_ref[i] + b_ref[...]))
            dfl_ref[i] = dfl
            return rc[:, 0:1], db + jnp.sum(dfl, axis=1, keepdims=True)

        _, db = lax.fori_loop(0, nb, step, (jnp.zeros((h, 1), F32), jnp.zeros((h, 1), F32)))
        db_ref[...] = db

    blocked = lambda a: a.reshape(h, nb, LANE).transpose(1, 0, 2)
    dfl, db = pl.pallas_call(
        body, name="forget_bwd",
        out_shape=(jax.ShapeDtypeStruct((nb, h, LANE), F32), jax.ShapeDtypeStruct((h, 1), F32)),
        compiler_params=_cparams(),
    )(blocked(dc_t), blocked(fl_t), b_col)
    return dfl.transpose(1, 0, 2).reshape(h, s), db


N_PAIRS = N_HEADS // 2


def _causal_mask(t):
    return lax.broadcasted_iota(jnp.int32, (t, t), 0) >= lax.broadcasted_iota(jnp.int32, (t, t), 1)


def _head_lanes():
    return lax.broadcasted_iota(jnp.int32, (1, 2 * HEAD_DIM), 1) < HEAD_DIM


def _pick(x2, first, hh):
    zero = jnp.zeros_like(x2)
    return jnp.where(first, x2, zero) if hh == 0 else jnp.where(first, zero, x2)


BIAS_TERMS = 3


def _attn_prep(z, c, ts):
    s = z.shape[0]
    ts = _tile(s, ts, 16)
    w = 2 * HEAD_DIM

    def body(q_ref, k_ref, v_ref, c_ref, qa_ref, ka_ref, vb_ref):
        lane = lax.broadcasted_iota(jnp.int32, (1, w), 1)
        first = lane < HEAD_DIM
        cv = c_ref[...]
        for h in range(N_HEADS):
            pair = slice((h // 2) * w, (h // 2 + 1) * w)
            qh = q_ref[:, pair] * ATT_SCALE
            kh = k_ref[:, pair]
            if h % 2:
                qh = pltpu.roll(qh, HEAD_DIM, 1)
                kh = pltpu.roll(kh, HEAD_DIM, 1)
            rest = cv[:, h:h + 1]
            q_tail = jnp.zeros((1, w), F32)
            k_tail = jnp.zeros((1, w), F32)
            for t in range(BIAS_TERMS):
                term = rest.astype(BF16).astype(F32)
                rest = rest - term
                q_tail = jnp.where(lane == HEAD_DIM + t, term, jnp.where(lane == HEAD_DIM + BIAS_TERMS + t, 1.0, q_tail))
                k_tail = jnp.where(lane == HEAD_DIM + t, 1.0, jnp.where(lane == HEAD_DIM + BIAS_TERMS + t, -term, k_tail))
            qa_ref[:, h * w:(h + 1) * w] = jnp.where(first, qh, q_tail).astype(BF16)
            ka_ref[:, h * w:(h + 1) * w] = jnp.where(first, kh, k_tail).astype(BF16)
        vb_ref[...] = v_ref[...].astype(BF16)

    col = lambda cb: pl.BlockSpec((ts, D_ATT), lambda i: (i, cb))
    wide = pl.BlockSpec((ts, N_HEADS * w), lambda i: (i, 0))
    return pl.pallas_call(
        body, name="attn_prep",
        out_shape=(jax.ShapeDtypeStruct((s, N_HEADS * w), BF16), jax.ShapeDtypeStruct((s, N_HEADS * w), BF16),
                   jax.ShapeDtypeStruct((s, D_ATT), BF16)),
        grid=(s // ts,), in_specs=[col(0), col(1), col(2), pl.BlockSpec((ts, N_HEADS), lambda i: (i, 0))],
        out_specs=(wide, wide, col(0)),
        compiler_params=_cparams("parallel"),
    )(z, z, z, c)


def _attn_fwd(qa, ka, vb, tq):
    s = qa.shape[0]
    nq = s // tq
    w = 2 * HEAD_DIM

    def body(q_ref, k_ref, v_ref, o_ref, lse_ref):
        i = pl.program_id(1)
        first = _head_lanes()
        q2 = q_ref[...]

        def block(j, carry, masked):
            off = pl.multiple_of(j * tq, tq)
            k2 = k_ref[pl.ds(off, tq), :]
            v2 = v_ref[pl.ds(off, tq), :]
            new = []
            for hh in range(2):
                m, l, acc = carry[hh]
                sc = lax.dot_general(q2[:, hh * w:(hh + 1) * w], k2[:, hh * w:(hh + 1) * w], NT,
                                     preferred_element_type=F32)
                if masked:
                    sc = jnp.where(_causal_mask(tq), sc, NEG_INF)
                m_new = jnp.maximum(m, jnp.max(sc, axis=-1, keepdims=True))
                alpha = jnp.exp(m - m_new)
                p = jnp.exp(sc - m_new)
                l = alpha * l + jnp.sum(p, axis=-1, keepdims=True)
                p_hi = p.astype(BF16)
                p_lo = (p - p_hi.astype(F32)).astype(BF16)
                acc = (alpha * acc + jnp.dot(p_hi, v2, preferred_element_type=F32)
                       + jnp.dot(p_lo, v2, preferred_element_type=F32))
                new.append((m_new, l, acc))
            return tuple(new)

        one = (jnp.full((tq, 1), NEG_INF, F32), jnp.zeros((tq, 1), F32), jnp.zeros((tq, w), F32))
        carry = lax.fori_loop(0, i, lambda j, c: block(j, c, False), (one, one))
        (m0, l0, a0), (m1, l1, a1) = block(i, carry, True)
        o_ref[...] = jnp.where(first, a0 / l0, a1 / l1)
        lse_ref[0] = m0 + jnp.log(l0)
        lse_ref[1] = m1 + jnp.log(l1)

    return pl.pallas_call(
        body, name="attn_fwd",
        out_shape=(jax.ShapeDtypeStruct((s, D_ATT), F32), jax.ShapeDtypeStruct((N_HEADS, s, 1), F32)),
        grid=(N_PAIRS, nq),
        in_specs=[pl.BlockSpec((tq, 2 * w), lambda hp, i: (i, hp)),
                  pl.BlockSpec((s, 2 * w), lambda hp, i: (0, hp)),
                  pl.BlockSpec((s, w), lambda hp, i: (0, hp))],
        out_specs=(pl.BlockSpec((tq, w), lambda hp, i: (i, hp)),
                   pl.BlockSpec((2, tq, 1), lambda hp, i: (hp, i, 0))),
        compiler_params=_cparams("parallel", "parallel"),
    )(qa, ka, vb)


def _attn_delta(o, do, tq):
    s = o.shape[0]
    w = 2 * HEAD_DIM

    def body(o_ref, do_ref, d_ref):
        first = _head_lanes()
        prod = o_ref[...] * do_ref[...].astype(F32)
        d_ref[0] = jnp.sum(_pick(prod, first, 0), axis=-1, keepdims=True)
        d_ref[1] = jnp.sum(_pick(prod, first, 1), axis=-1, keepdims=True)

    blk = pl.BlockSpec((tq, w), lambda hp, i: (i, hp))
    return pl.pallas_call(
        body, name="attn_delta", out_shape=jax.ShapeDtypeStruct((N_HEADS, s, 1), F32), grid=(N_PAIRS, s // tq),
        in_specs=[blk, blk], out_specs=pl.BlockSpec((2, tq, 1), lambda hp, i: (hp, i, 0)),
        compiler_params=_cparams("parallel", "parallel"),
    )(o, do)


def _attn_bwd(qa, ka, vb, do, lse, delta, tq):
    s = qa.shape[0]
    nq = s // tq
    w = 2 * HEAD_DIM

    def body(q_ref, do_ref, lse_ref, dl_ref, k_ref, v_ref, dq_ref, dk_ref, dv_ref, dc_ref, dq_acc):
        j = pl.program_id(1)
        first = _head_lanes()

        @pl.when(j == 0)
        def _():
            dq_acc[...] = jnp.zeros_like(dq_acc)

        k2 = k_ref[...]
        v2 = v_ref[...]

        def step(i, carry, masked):
            off = pl.multiple_of(i * tq, tq)
            rows = pl.ds(off, tq)
            q2 = q_ref[rows, :]
            do2 = do_ref[rows, :]
            new, dqs = [], []
            for hh in range(2):
                dk, dv, dcs = carry[hh]
                qh = q2[:, hh * w:(hh + 1) * w]
                kh = k2[:, hh * w:(hh + 1) * w]
                sc = lax.dot_general(qh, kh, NT, preferred_element_type=F32)
                if masked:
                    sc = jnp.where(_causal_mask(tq), sc, NEG_INF)
                p = jnp.exp(sc - lse_ref[hh, rows, :])
                dv = dv + lax.dot_general(do2, p.astype(BF16), TN, preferred_element_type=F32)
                dp = lax.dot_general(_pick(do2, first, hh), v2, NT, preferred_element_type=F32)
                ds = p * (dp - dl_ref[hh, rows, :])
                dsb = ds.astype(BF16)
                dk = dk + lax.dot_general(qh, dsb, TN, preferred_element_type=F32)
                dqs.append(jnp.dot(dsb, kh, preferred_element_type=F32))
                new.append((dk, dv, dcs + jnp.sum(ds, axis=0, keepdims=True)))
            dq_acc[rows, :] += jnp.where(first, dqs[0], pltpu.roll(dqs[1], HEAD_DIM, 1)) * ATT_SCALE
            return tuple(new)

        one = (jnp.zeros((w, tq), F32), jnp.zeros((w, tq), F32), jnp.zeros((1, tq), F32))
        carry = step(j, (one, one), True)
        (dk0, dv0, dc0), (dk1, dv1, dc1) = lax.fori_loop(j + 1, nq, lambda i, c: step(i, c, False), carry)
        dk_ref[...] = jnp.where(first, dk0.T, pltpu.roll(dk1.T, HEAD_DIM, 1)).astype(BF16)
        dv_ref[...] = jnp.where(first, dv0.T, dv1.T).astype(BF16)
        dc_ref[0, 0] = -dc0
        dc_ref[1, 0] = -dc1

        @pl.when(j == nq - 1)
        def _():
            dq_ref[...] = dq_acc[...].astype(BF16)

    whole = lambda width: pl.BlockSpec((s, width), lambda hp, j: (0, hp))
    whole_heads = pl.BlockSpec((2, s, 1), lambda hp, j: (hp, 0, 0))
    blk = lambda width: pl.BlockSpec((tq, width), lambda hp, j: (j, hp))
    crow = pl.BlockSpec((2, 1, 1, tq), lambda hp, j: (hp, j, 0, 0))
    return pl.pallas_call(
        body, name="attn_bwd",
        out_shape=(jax.ShapeDtypeStruct((s, D_ATT), BF16), jax.ShapeDtypeStruct((s, D_ATT), BF16),
                   jax.ShapeDtypeStruct((s, D_ATT), BF16), jax.ShapeDtypeStruct((N_HEADS, nq, 1, tq), F32)),
        grid=(N_PAIRS, nq),
        in_specs=[whole(2 * w), whole(w), whole_heads, whole_heads, blk(2 * w), blk(w)],
        out_specs=(whole(w), blk(w), blk(w), crow),
        scratch_shapes=[pltpu.VMEM((s, w), F32)],
        compiler_params=_cparams("parallel", "arbitrary"),
    )(qa, do, lse, delta, ka, vb)


def _pair_sums(x, first):
    total = jnp.sum(x, axis=-1, keepdims=True)
    head = jnp.sum(jnp.where(first, x, 0.0), axis=-1, keepdims=True)
    return head, total - head


def _pair_mean(x, first):
    head, tail = _pair_sums(x, first)
    return jnp.where(first, head, tail) * (1.0 / GROUP_DIM)


def _gm_pair_norm(v2, first):
    d = v2 - _pair_mean(v2, first)
    rstd = lax.rsqrt(_pair_mean(d * d, first) + EPS)
    return d * rstd, rstd


def _gm_pair_mix(w_ref, pr, rhs, first):
    return jnp.where(first, jnp.dot(w_ref[2 * pr], rhs, preferred_element_type=F32),
                     jnp.dot(w_ref[2 * pr + 1], rhs, preferred_element_type=F32))


def _gmlp_fwd(z, wt, bs_t, vgain):
    s = z.shape[0]

    def body(gu_ref, gv_ref, wt_ref, bs_ref, vg_ref, o_ref):
        first = _head_lanes()
        for pr in range(N_GROUPS // 2):
            sl = slice(2 * pr * GROUP_DIM, 2 * (pr + 1) * GROUP_DIM)
            vhat, _ = _gm_pair_norm(_gelu(gv_ref[:, sl]), first)
            vn = (vhat * vg_ref[:, sl]).astype(BF16)
            bias = jnp.where(first, bs_ref[:, 2 * pr:2 * pr + 1], bs_ref[:, 2 * pr + 1:2 * pr + 2])
            o_ref[:, sl] = _gelu(gu_ref[:, sl]) * (_gm_pair_mix(wt_ref, pr, vn, first) + bias)

    full = lambda a: pl.BlockSpec(a.shape, lambda n: (0,) * a.ndim)
    return pl.pallas_call(
        body, name="gmlp_fwd", out_shape=jax.ShapeDtypeStruct((s, D_GM), F32), grid=(s // CHUNK,),
        in_specs=[pl.BlockSpec((CHUNK, D_GM), lambda n: (n, 3)), pl.BlockSpec((CHUNK, D_GM), lambda n: (n, 4)),
                  full(wt), full(bs_t), full(vgain)],
        out_specs=pl.BlockSpec((CHUNK, D_GM), lambda n: (n, 0)),
        compiler_params=_cparams("parallel"),
    )(z, z, wt, bs_t, vgain)


def _gmlp_bwd(z, dgm, wt, wt_t, bs_t, vgain):
    s = z.shape[0]

    def body(gu_ref, gv_ref, dgm_ref, wt_ref, wtt_ref, bs_ref, vg_ref, dgu_ref, dgv_ref, dwt_ref, dbs_ref, dvg_ref):
        @pl.when(pl.program_id(0) == 0)
        def _():
            dwt_ref[...] = jnp.zeros_like(dwt_ref)
            dbs_ref[...] = jnp.zeros_like(dbs_ref)
            dvg_ref[...] = jnp.zeros_like(dvg_ref)

        first = _head_lanes()
        for pr in range(N_GROUPS // 2):
            g0, g1 = 2 * pr, 2 * pr + 1
            sl = slice(g0 * GROUP_DIM, (g1 + 1) * GROUP_DIM)
            gu = gu_ref[:, sl]
            gv = gv_ref[:, sl]
            dgm = dgm_ref[:, sl]
            vhat, rstd = _gm_pair_norm(_gelu(gv), first)
            gain = vg_ref[:, sl]
            vn = (vhat * gain).astype(BF16)
            bias = jnp.where(first, bs_ref[:, g0:g0 + 1], bs_ref[:, g1:g1 + 1])
            mixed = _gm_pair_mix(wt_ref, pr, vn, first) + bias
            dgu_ref[:, sl] = (dgm * mixed * _gelu_grad(gu)).astype(BF16)
            dmixed = dgm * _gelu(gu)
            db0, db1 = _pair_sums(dmixed, first)
            dbs_ref[:, g0:g0 + 1] += db0
            dbs_ref[:, g1:g1 + 1] += db1
            dwt_ref[g0] += lax.dot_general(_pick(dmixed, first, 0).astype(BF16), vn, NT, preferred_element_type=F32)
            dwt_ref[g1] += lax.dot_general(_pick(dmixed, first, 1).astype(BF16), vn, NT, preferred_element_type=F32)
            dvn = _gm_pair_mix(wtt_ref, pr, dmixed.astype(BF16), first)
            dvg_ref[:, sl] += _colsum(dvn * vhat)
            dvhat = dvn * gain
            dvf = rstd * (dvhat - _pair_mean(dvhat, first) - vhat * _pair_mean(dvhat * vhat, first))
            dgv_ref[:, sl] = (dvf * _gelu_grad(gv)).astype(BF16)

    full = lambda a: pl.BlockSpec(a.shape, lambda n: (0,) * a.ndim)
    chunk = pl.BlockSpec((CHUNK, D_GM), lambda n: (n, 0))
    return pl.pallas_call(
        body, name="gmlp_bwd",
        out_shape=(jax.ShapeDtypeStruct((s, D_GM), BF16), jax.ShapeDtypeStruct((s, D_GM), BF16),
                   jax.ShapeDtypeStruct(wt.shape, F32), jax.ShapeDtypeStruct(bs_t.shape, F32),
                   jax.ShapeDtypeStruct(vgain.shape, F32)),
        grid=(s // CHUNK,),
        in_specs=[pl.BlockSpec((CHUNK, D_GM), lambda n: (n, 3)), pl.BlockSpec((CHUNK, D_GM), lambda n: (n, 4)),
                  chunk, full(wt), full(wt_t), full(bs_t), full(vgain)],
        out_specs=(chunk, chunk, full(wt), full(bs_t), full(vgain)),
        compiler_params=_cparams("arbitrary"),
    )(z, z, dgm, wt, wt_t, bs_t, vgain)


def _layer_fwd(h0, p_i, w, tq, late):
    s, d = h0.shape
    nq = s // tq
    sv = {"h0": h0}

    (hn1,) = _rowwise(lambda h, g: _rms(h)[0] * g, [h0], [w["mix_pre_norm"]], [(d, BF16)], [], "pre_mix", ROW_TILE)
    z = _mm(hn1, w["w_in"], "nn", F32, "mm_in")
    fl_t = z[:, F_OFF:F_OFF + N_HEADS].T
    c_t = _forget_fwd(fl_t, w["b_forget"])
    qa, ka, vb = _attn_prep(z, c_t.T, ROW_TILE)
    att, lse = _attn_fwd(qa, ka, vb, tq)
    gm = _gmlp_fwd(z, w["wt"], w["bs_t"], w["gm_v_norm"])
    w = dict(w, **late(att))

    def mix_out(att, gm, g):
        return jnp.concatenate([_rms(att)[0] * g[:, :D_ATT], _rms(gm)[0] * g[:, D_ATT:]], axis=-1)

    (mc,) = _rowwise(mix_out, [att, gm], [w["mix_out_norm"]], [(D_ATT + D_GM, BF16)], [], "mix_out", ROW_TILE)
    y1 = _mm(mc, w["w_out"], "nn", F32, "mm_out")

    def post_mix(h0, y1, gpost, gpre):
        h1 = h0 + _rms(y1)[0] * gpost
        return h1, _rms(h1)[0] * gpre

    h1, hn2 = _rowwise(post_mix, [h0, y1], [w["mix_post_norm"], w["ffn_pre_norm"]],
                       [(d, F32), (d, BF16)], [], "post_mix", ROW_TILE)
    ab, t = _ffn_in_swiglu(hn2, w["w_ffn_in"])
    y2 = _mm(t, w["w_ffn_out"], "nn", F32, "mm_ffn_out", a3="k", b3="k")

    def post_ffn(h1, y2, g):
        h2 = h1 + _rms(y2)[0] * g
        return h2, _rms(h2)[0]

    h2, hr = _rowwise(post_ffn, [h1, y2], [w["ffn_post_norm"]], [(d, F32), (d, BF16)], [], "post_ffn", ROW_TILE)
    gl = _mm(hr, w["w_ple_gate"], "nn", F32, "mm_gate")
    pe = _mm(p_i, w["w_ple"], "nn", F32, "mm_ple")
    (h3,) = _rowwise(lambda h2, gl, pe, g: h2 + _sigmoid(gl) * (_rms(pe)[0] * g), [h2, gl, pe], [w["ple_norm"]],
                     [(d, F32)], [], "ple_out", ROW_TILE)
    sv.update(hn1=hn1, z=z, fl_t=fl_t, qa=qa, ka=ka, vb=vb, lse=lse, att=att, gm=gm,
              mc=mc, y1=y1, h1=h1, hn2=hn2, ab=ab, y2=y2, h2=h2, hr=hr, gl=gl, pe=pe, p_i=p_i)
    return h3, sv


def _layer_bwd(dh3, sv, w, tq, mid):
    s, d = dh3.shape
    g = {}
    by_rows = lambda a: a.reshape(N_DEV, -1, a.shape[-1])
    by_cols = lambda a: jnp.stack(jnp.split(a, N_DEV, axis=-1))

    def ple_bwd(dh3, gl, pe, gple):
        gate = _sigmoid(gl)
        pehat, rpe = _rms(pe)
        dgl = dh3 * (pehat * gple) * gate * (1.0 - gate)
        de = dh3 * gate
        return dgl, _rms_bwd(pehat, rpe, de * gple), _colsum(de * pehat)

    dgl, dpe, g["ple_norm"] = _rowwise(ple_bwd, [dh3, sv["gl"], sv["pe"]], [w["ple_norm"]],
                                       [(d, BF16), (d, BF16)], [d], "ple_bwd", ROW_TILE)
    g["w_ple_gate"] = by_rows(_mm(sv["hr"], dgl, "tn", BF16, "mm_dgate"))
    dhr = _mm(dgl, w["w_ple_gate"], "nt", F32, "mm_dhr")
    g["w_ple"] = by_cols(_mm(sv["p_i"], dpe, "tn", BF16, "mm_dple"))

    def ffn_post_bwd(dh3, dhr, h2, y2, gpost):
        h2hat, r2 = _rms(h2)
        dh2 = dh3 + _rms_bwd(h2hat, r2, dhr)
        y2hat, ry = _rms(y2)
        return dh2, _rms_bwd(y2hat, ry, dh2 * gpost), _colsum(dh2 * y2hat)

    dh2, dy2, g["ffn_post_norm"] = _rowwise(ffn_post_bwd, [dh3, dhr, sv["h2"], sv["y2"]], [w["ffn_post_norm"]],
                                            [(d, F32), (d, BF16)], [d], "ffn_post_bwd", ROW_TILE)
    dt = _mm(dy2, w["w_ffn_out"], "nt", BF16, "mm_dt", b3="n", o3="n")
    t, dab = _swiglu_bwd(sv["ab"], dt, 2 * ROW_TILE)
    dab = dab.reshape((N_DEV,) + dab.shape[2:])
    g["w_ffn_out"] = by_rows(_mm(t, dy2, "tn", BF16, "mm_dffn_out", a3="m", o3="m"))
    dhn2 = _mm(dab, w["w_ffn_in"], "nt", F32, "mm_dhn2", a3="k", b3="k")
    g["w_ffn_in"] = _mm(sv["hn2"], dab, "tn", BF16, "mm_dffn_in", b3="n", o3="n")

    def mix_post_bwd(dh2, dhn2, h1, y1, gpre, gpost):
        h1hat, r1 = _rms(h1)
        dh1 = dh2 + _rms_bwd(h1hat, r1, dhn2 * gpre)
        y1hat, ry = _rms(y1)
        return dh1, _rms_bwd(y1hat, ry, dh1 * gpost), _colsum(dhn2 * h1hat), _colsum(dh1 * y1hat)

    dh1, dy1, g["ffn_pre_norm"], g["mix_post_norm"] = _rowwise(
        mix_post_bwd, [dh2, dhn2, sv["h1"], sv["y1"]], [w["ffn_pre_norm"], w["mix_post_norm"]],
        [(d, F32), (d, BF16)], [d, d], "mix_post_bwd", ROW_TILE)
    dmc = _mm(dy1, w["w_out"], "nt", F32, "mm_dmc")
    g["w_out"] = by_rows(_mm(sv["mc"], dy1, "tn", BF16, "mm_dout"))
    w = dict(w, **mid(g, dmc))

    def mix_out_bwd(da, dg, att, gm, gain):
        atthat, ra = _rms(att)
        gmhat, rg = _rms(gm)
        dgain = jnp.concatenate([_colsum(da * atthat), _colsum(dg * gmhat)], axis=-1)
        return _rms_bwd(atthat, ra, da * gain[:, :D_ATT]), _rms_bwd(gmhat, rg, dg * gain[:, D_ATT:]), dgain

    datt, dgm, g["mix_out_norm"] = _rowwise(
        mix_out_bwd, [(dmc, 0, D_ATT), (dmc, 1, D_GM), sv["att"], sv["gm"]], [w["mix_out_norm"]],
        [(D_ATT, BF16), (D_GM, F32)], [D_ATT + D_GM], "mix_out_bwd", ROW_TILE)

    dgu, dgv, dwt, dbs_t, g["gm_v_norm"] = _gmlp_bwd(sv["z"], dgm, w["wt"], w["wt_t"], w["bs_t"], w["gm_v_norm"])
    g["gm_w_s"] = dwt * jnp.tril(jnp.ones((CHUNK, CHUNK), F32))[None]
    g["gm_b_s"] = dbs_t.T

    delta = _attn_delta(sv["att"], datt, tq)
    dq, dk, dv, dc_row = _attn_bwd(sv["qa"], sv["ka"], sv["vb"], datt, sv["lse"], delta, tq)
    dfl_t, db = _forget_bwd(dc_row.reshape(N_HEADS, s), sv["fl_t"], w["b_forget"])
    g["b_forget"] = db.reshape(1, N_HEADS)
    tail = jnp.concatenate([dfl_t.T.astype(BF16), jnp.zeros((s, D_IN_PAD - F_OFF - N_HEADS), BF16)], axis=-1)
    dz = jnp.concatenate([dq, dk, dv, dgu, dgv, tail], axis=-1)
    dhn1 = _mm(dz, w["w_in"], "nt", F32, "mm_dhn1")
    din = _mm(sv["hn1"], dz, "tn", BF16, "mm_din")
    din = jnp.concatenate([din[:, :3 * D_ATT], din[:, F_OFF:F_OFF + N_HEADS], din[:, 3 * D_ATT:F_OFF]], axis=-1)
    g["w_in"] = by_cols(din)

    def mix_pre_bwd(dh1, dhn1, h0, gpre):
        h0hat, r0 = _rms(h0)
        return dh1 + _rms_bwd(h0hat, r0, dhn1 * gpre), _colsum(dhn1 * h0hat)

    dh0, g["mix_pre_norm"] = _rowwise(mix_pre_bwd, [dh1, dhn1, sv["h0"]], [w["mix_pre_norm"]],
                                      [(d, F32)], [d], "mix_pre_bwd", ROW_TILE)
    return dh0, g


ANY = pl.BlockSpec(memory_space=pl.ANY)


def _all_gather(xs, layer, name):
    n = len(xs)

    def body(*refs):
        x_refs, out_refs = refs[:n], refs[n:2 * n]
        send_sems, recv_sems, local_sems = refs[2 * n:]
        x, y, c = lax.axis_index("x"), lax.axis_index("y"), lax.axis_index("c")
        me, sibling = (x, y, c), (x, y, 1 - c)
        chips = [(1 - x, y), (x, 1 - y), (1 - x, 1 - y)]

        def shard(a):
            return x_refs[a] if layer is None else x_refs[a].at[layer]

        def rows(a, px, py, pc):
            return out_refs[a].at[4 * px + 2 * py + pc]

        def copy(a, kk, block, to, from_shard=False):
            return pltpu.make_async_remote_copy(
                src_ref=shard(a) if from_shard else rows(a, *block), dst_ref=rows(a, *block),
                send_sem=send_sems.at[7 * a + kk], recv_sem=recv_sems.at[7 * a + kk],
                device_id=to, device_id_type=MESH)

        mine = [pltpu.make_async_copy(shard(a), rows(a, *me), local_sems.at[a]) for a in range(n)]
        for cp in mine:
            cp.start()
        first = []
        for a in range(n):
            first.append(copy(a, 0, me, sibling, from_shard=True))
            first += [copy(a, 1 + j, me, (*chip, c), from_shard=True) for j, chip in enumerate(chips)]
        for cp in first:
            cp.start()
        passed = []
        for j, chip in enumerate(chips):
            for a in range(n):
                copy(a, 1 + j, (*chip, c), me).wait_recv()
                passed.append(copy(a, 4 + j, (*chip, c), sibling))
                passed[-1].start()
        for a in range(n):
            copy(a, 0, sibling, me).wait_recv()
        for j, chip in enumerate(chips):
            for a in range(n):
                copy(a, 4 + j, (*chip, 1 - c), me).wait_recv()
        for cp in first + passed:
            cp.wait_send()
        for cp in mine:
            cp.wait()

    shapes = [x.shape if layer is None else x.shape[1:] for x in xs]
    return pl.pallas_call(
        body, name=name, out_shape=[jax.ShapeDtypeStruct((N_DEV,) + sh, x.dtype) for sh, x in zip(shapes, xs)],
        in_specs=[ANY] * n, out_specs=[ANY] * n,
        scratch_shapes=[pltpu.SemaphoreType.DMA((7 * n,)), pltpu.SemaphoreType.DMA((7 * n,)),
                        pltpu.SemaphoreType.DMA((n,))],
    )(*xs)


HBM = pl.BlockSpec(memory_space=pltpu.HBM)
SEMS = pl.BlockSpec(memory_space=pltpu.SEMAPHORE)
EFFECT = pltpu.SideEffectType.DATAFLOW_SIDE_EFFECTING
FLIPS = tuple((fx, fy, fc) for fx in (0, 1) for fy in (0, 1) for fc in (0, 1))[1:]


def _exchange_copies(src_refs, land_refs, send_sems, recv_sems, layer, scatter):
    x, y, c = lax.axis_index("x"), lax.axis_index("y"), lax.axis_index("c")
    me = 4 * x + 2 * y + c
    copies = []
    for a, (src, land) in enumerate(zip(src_refs, land_refs)):
        for f, (fx, fy, fc) in enumerate(FLIPS):
            px, py, pc = (1 - x if fx else x), (1 - y if fy else y), (1 - c if fc else c)
            if scatter:
                block = src.at[4 * px + 2 * py + pc]
            else:
                block = src if layer is None else src.at[layer]
            copies.append(pltpu.make_async_remote_copy(
                src_ref=block, dst_ref=land.at[me], send_sem=send_sems.at[7 * a + f], recv_sem=recv_sems.at[7 * a + f],
                device_id=(px, py, pc), device_id_type=MESH))
    return copies


def _exchange_start(srcs, lands, layer, scatter, name):
    n = len(srcs)

    def body(*refs):
        for cp in _exchange_copies(refs[:n], refs[n:2 * n], refs[2 * n], refs[2 * n + 1], layer, scatter):
            cp.start()
        token = refs[-1]
        token[...] = jnp.zeros_like(token)

    operands = list(srcs) + list(lands)
    outs = pl.pallas_call(
        body, name=name,
        out_shape=(pltpu.SemaphoreType.DMA((7 * n,)), pltpu.SemaphoreType.DMA((7 * n,)),
                   *[pltpu.HBM(a.shape, a.dtype) for a in operands], jax.ShapeDtypeStruct((8, LANE), F32)),
        in_specs=[HBM] * (2 * n),
        out_specs=(SEMS, SEMS, *[HBM] * (2 * n), pl.BlockSpec(memory_space=pltpu.VMEM)),
        input_output_aliases={i: 2 + i for i in range(2 * n)},
        compiler_params=pltpu.CompilerParams(has_side_effects=EFFECT),
    )(*[pltpu.with_memory_space_constraint(a, pltpu.HBM) for a in operands])
    return outs[0], outs[1], outs[2:2 + n], outs[2 + n:2 + 2 * n], outs[-1]


def _exchange_wait(started, after, layer, scatter, name):
    send_sems, recv_sems, srcs, lands, _ = started
    n = len(srcs)

    def body(*refs):
        for cp in _exchange_copies(refs[:n], refs[n:2 * n], refs[2 * n], refs[2 * n + 1], layer, scatter):
            cp.wait_send()
            cp.wait_recv()

    operands = list(srcs) + list(lands)
    outs = pl.pallas_call(
        body, name=name, out_shape=tuple(pltpu.HBM(a.shape, a.dtype) for a in operands),
        in_specs=[HBM] * (2 * n) + [SEMS, SEMS, ANY], out_specs=[HBM] * (2 * n),
        input_output_aliases={i: i for i in range(2 * n)},
        compiler_params=pltpu.CompilerParams(has_side_effects=EFFECT),
    )(*operands, send_sems, recv_sems, after)
    return outs[:n], outs[n:]


def _sum_devices(parts):
    _, r, c = parts.shape

    def body(p_ref, o_ref):
        acc = p_ref[0].astype(F32)
        for j in range(1, N_DEV):
            acc = acc + p_ref[j].astype(F32)
        o_ref[...] = acc

    return pl.pallas_call(
        body, name="small_sum", out_shape=jax.ShapeDtypeStruct((r, c), F32), grid=(r // SMALL_ROWS,),
        in_specs=[pl.BlockSpec((N_DEV, SMALL_ROWS, c), lambda i: (0, i, 0))],
        out_specs=pl.BlockSpec((SMALL_ROWS, c), lambda i: (i, 0)),
        compiler_params=_cparams("parallel"),
    )(parts)


def _adamw_math(w, g, m, v):
    m = ADAM_B1 * m + (1.0 - ADAM_B1) * g
    v = ADAM_B2 * v + (1.0 - ADAM_B2) * (g * g)
    m_hat = m / (1.0 - ADAM_B1 ** ADAM_STEP)
    v_hat = v / (1.0 - ADAM_B2 ** ADAM_STEP)
    return -ADAM_LR * (m_hat / (jnp.sqrt(v_hat) + ADAM_EPS) + ADAM_WD * w), m, v


def _adamw_shard(w, m, v, parts, layer, outs, name):
    _, a, b = w.shape
    ta = _tile(a, 256, 16)
    if outs is None:
        outs = [lax.empty(w.shape, F32) for _ in range(4)]

    def body(w_ref, m_ref, v_ref, p_ref, *refs):
        g_ref, d_ref, nm_ref, nv_ref = refs[4:]
        g = p_ref[0].astype(F32)
        for j in range(1, N_DEV):
            g = g + p_ref[j].astype(F32)
        g_ref[0] = g
        d_ref[0], nm_ref[0], nv_ref[0] = _adamw_math(w_ref[0], g, m_ref[0], v_ref[0])

    mine = pl.BlockSpec((1, ta, b), lambda i: (layer, i, 0))
    return pl.pallas_call(
        body, name=name, out_shape=[jax.ShapeDtypeStruct(w.shape, F32)] * 4, grid=(a // ta,),
        in_specs=[mine, mine, mine, pl.BlockSpec((N_DEV, ta, b), lambda i: (0, i, 0))] + [ANY] * 4,
        out_specs=[mine] * 4, input_output_aliases={4 + k: k for k in range(4)},
        compiler_params=_cparams("parallel"),
    )(w, m, v, parts, *outs)


def _pack_small(pieces):
    flat = jnp.concatenate([p.reshape(-1) for p in pieces])
    total = -(-flat.shape[0] // (SMALL_COLS * SMALL_ROWS)) * SMALL_COLS * SMALL_ROWS
    return jnp.pad(flat, (0, total - flat.shape[0])).reshape(-1, SMALL_COLS)


def kernel(x, p, mix_pre_norm, mix_post_norm, w_in, b_forget, gm_v_norm, gm_w_s, gm_b_s, mix_out_norm, w_out, ffn_pre_norm, ffn_post_norm, w_ffn_in, w_ffn_out, w_ple, ple_norm, w_ple_gate, loss_target, m_mix_pre_norm, m_mix_post_norm, m_w_in, m_b_forget, m_gm_v_norm, m_gm_w_s, m_gm_b_s, m_mix_out_norm, m_w_out, m_ffn_pre_norm, m_ffn_post_norm, m_w_ffn_in, m_w_ffn_out, m_w_ple, m_ple_norm, m_w_ple_gate, v_mix_pre_norm, v_mix_post_norm, v_w_in, v_b_forget, v_gm_v_norm, v_gm_w_s, v_gm_b_s, v_mix_out_norm, v_w_out, v_ffn_pre_norm, v_ffn_post_norm, v_w_ffn_in, v_w_ffn_out, v_w_ple, v_ple_norm, v_w_ple_gate):
    given = dict(locals())
    weights = {n: given[n] for n in WEIGHT_ORDER}
    mom_m = {n: given["m_" + n] for n in WEIGHT_ORDER}
    mom_v = {n: given["v_" + n] for n in WEIGHT_ORDER}
    depth = w_in.shape[0]
    s, d = x.shape[1], x.shape[2]
    tq = _tile(s, ATT_BLOCK)
    me = 4 * lax.axis_index("x") + 2 * lax.axis_index("y") + lax.axis_index("c")
    tril = jnp.tril(jnp.ones((CHUNK, CHUNK), F32))

    def landing(block):
        return lax.dynamic_update_index_in_dim(lax.empty((N_DEV,) + block.shape, block.dtype), block, me, 0)

    def mix_weights(i, got):
        w_in_full = jnp.concatenate([got["w_in"][j] for j in range(N_DEV)], axis=-1)
        pad = jnp.zeros((d, D_IN_PAD - D_IN), BF16)
        wt = gm_w_s[i] * tril[None]
        lw = dict(
            w_in=jnp.concatenate([w_in_full[:, :3 * D_ATT], w_in_full[:, 3 * D_ATT + N_HEADS:],
                                  w_in_full[:, 3 * D_ATT:3 * D_ATT + N_HEADS], pad], axis=-1),
            b_forget=b_forget[i][:, None], wt=wt.astype(BF16), wt_t=wt.transpose(0, 2, 1).astype(BF16),
            bs_t=gm_b_s[i].T)
        lw.update({n: weights[n][i][None] for n in ("mix_pre_norm", "mix_post_norm", "gm_v_norm", "mix_out_norm",
                                                    "ffn_pre_norm", "ffn_post_norm", "ple_norm")})
        return lw

    def rest_weights(got):
        return dict(w_out=got["w_out"].reshape(-1, d), w_ffn_in=got["w_ffn_in"],
                    w_ffn_out=got["w_ffn_out"].reshape(N_DEV // 2, -1, d),
                    w_ple=jnp.concatenate([got["w_ple"][j] for j in range(N_DEV)], axis=-1),
                    w_ple_gate=got["w_ple_gate"].reshape(-1, d))

    shards = {n: weights[n].astype(BF16) for n in MATRIX_WEIGHTS}

    def gather_start(i):
        started = {}
        order = jnp.zeros((), BF16)
        for tag, grp in EXCHANGE_GROUPS.items():
            started[tag] = _exchange_start([shards[n] for n in grp], [landing(shards[n][i] + order) for n in grp], i,
                                           False, f"weights_gather_start_{i}_{tag}")
            order = started[tag][4][0, 0].astype(BF16)
        return started

    def gather_finish(i, tag, pending, after):
        srcs, got = _exchange_wait(pending[tag], after, i, False, f"weights_gather_wait_{i}_{tag}")
        shards.update(zip(EXCHANGE_GROUPS[tag], srcs))
        return dict(zip(EXCHANGE_GROUPS[tag], got))

    h = x[0]
    saved, layer_w = [], []
    pending = gather_start(0)
    for i in range(depth):
        lw = mix_weights(i, gather_finish(i, "mix", pending, h))
        if i == 0:
            lw["mix_pre_norm"] = lw["mix_pre_norm"] + pending["rest"][4][:1, :1]
        following = {}

        def late(att, i=i, pending=pending, lw=lw, following=following):
            rest = rest_weights(gather_finish(i, "rest", pending, att))
            lw.update(rest)
            if i + 1 == depth:
                return rest
            following.update(gather_start(i + 1))
            token = following["mix"][4][:1, :1] + following["rest"][4][:1, :1]
            return dict(rest, mix_out_norm=lw["mix_out_norm"] + token)

        h, sv = _layer_fwd(h, p[i, 0], lw, tq, late)
        layer_w.append(lw)
        saved.append(sv)
        pending = following

    def loss_head(y, t):
        err = y - t
        return err * (1.0 / d), _colsum(err * err)

    dh, sq = _rowwise(loss_head, [h, loss_target[0]], [], [(d, F32)], [d], "loss_head", ROW_TILE)
    loss = lax.psum(0.5 * jnp.sum(sq) / d, AXES)

    layer_g = [None] * depth
    shard_out = {n: None for n in MATRIX_WEIGHTS}

    def scatter_start(i, tag, g):
        full_g = [g[n] for n in EXCHANGE_GROUPS[tag]]
        lands = [landing(lax.dynamic_index_in_dim(gf, me, 0, keepdims=False)) for gf in full_g]
        return _exchange_start(full_g, lands, None, True, f"grads_scatter_start_{i}_{tag}")

    def scatter_finish(i, tag, started, after):
        _, parts = _exchange_wait(started[tag], after, None, True, f"grads_scatter_wait_{i}_{tag}")
        for n, part in zip(EXCHANGE_GROUPS[tag], parts):
            shard_out[n] = _adamw_shard(weights[n], mom_m[n], mom_v[n], part, i, shard_out[n], "adamw_" + n)

    before = None
    for i in reversed(range(depth)):
        lw = layer_w[i]
        if before is not None:
            lw = dict(lw, ple_norm=lw["ple_norm"] + before[1]["mix"][4][:1, :1])
        started = {}

        def mid(g, dmc, i=i, before=before, started=started, lw=lw):
            if before is not None:
                scatter_finish(before[0], "rest", before[1], dmc)
            started["rest"] = scatter_start(i, "rest", g)
            return dict(mix_out_norm=lw["mix_out_norm"] + started["rest"][4][:1, :1])

        dh, layer_g[i] = _layer_bwd(dh, saved[i], lw, tq, mid)
        if before is not None:
            scatter_finish(before[0], "mix", before[1], dh)
        started["mix"] = scatter_start(i, "mix", layer_g[i])
        before = (i, started)
    scatter_finish(before[0], "rest", before[1], before[1]["mix"][4])
    scatter_finish(before[0], "mix", before[1], dh)
    grad_x = dh[None]

    grads, deltas, new_m, new_v = {}, {}, {}, {}
    for n in MATRIX_WEIGHTS:
        grads[n], deltas[n], new_m[n], new_v[n] = shard_out[n]

    small_g = _pack_small([jnp.stack([layer_g[i][n].reshape(-1) for i in range(depth)]) for n in SMALL_WEIGHTS])
    (gathered,) = _all_gather([small_g.astype(BF16)], None, "small_grads_all_gather")
    g_small = _sum_devices(gathered)
    pack = lambda t: _pack_small([t[n] for n in SMALL_WEIGHTS])
    dl, nm, nv = _rowwise(_adamw_math, [pack(weights), g_small, pack(mom_m), pack(mom_v)], [],
                          [(SMALL_COLS, F32)] * 3, [], "adamw_small", SMALL_ROWS)
    off = 0
    for n in SMALL_WEIGHTS:
        shp, size = weights[n].shape, weights[n].size
        grads[n], deltas[n], new_m[n], new_v[n] = (a.reshape(-1)[off:off + size].reshape(shp)
                                                   for a in (g_small, dl, nm, nv))
        off += size

    return (loss, grad_x, *[grads[n] for n in WEIGHT_ORDER], *[deltas[n] for n in WEIGHT_ORDER],
            *[new_m[n] for n in WEIGHT_ORDER], *[new_v[n] for n in WEIGHT_ORDER])
```

```python
import functools
import math

import jax
import jax.numpy as jnp
from jax import lax
from jax.experimental import pallas as pl
from jax.experimental.pallas import tpu as pltpu

F32 = jnp.float32
BF16 = jnp.bfloat16
MESH = pl.DeviceIdType.MESH
AXES = ("x", "y", "c")
N_DEV = 8

EPS = 1e-6
NEG_INF = -1e30
N_HEADS = 8
HEAD_DIM = 64
D_ATT = N_HEADS * HEAD_DIM
N_GROUPS = 8
GROUP_DIM = 64
D_GM = N_GROUPS * GROUP_DIM
CHUNK = 128
ATT_SCALE = HEAD_DIM ** -0.5
ATT_BLOCK = 1024
D_IN = 3 * D_ATT + N_HEADS + 2 * D_GM
D_IN_PAD = 3 * D_ATT + 2 * D_GM + 128
F_OFF = 3 * D_ATT + 2 * D_GM

ADAM_LR = 0.001
ADAM_B1 = 0.9
ADAM_B2 = 0.999
ADAM_EPS = 1e-08
ADAM_WD = 0.01
ADAM_STEP = 10

LANE = 128
VMEM_LIMIT = 48 * 1024 * 1024
ROW_TILE = 512
K_TILE = 4096
SMALL_COLS = 128
SMALL_ROWS = 512

MATRIX_WEIGHTS = ("w_in", "w_out", "w_ffn_in", "w_ffn_out", "w_ple", "w_ple_gate")
EXCHANGE_GROUPS = {"mix": ("w_in",), "rest": ("w_out", "w_ffn_in", "w_ffn_out", "w_ple", "w_ple_gate")}
TRANSPOSED_WEIGHTS = ("w_ffn_in",)
SMALL_WEIGHTS = ("mix_pre_norm", "mix_post_norm", "b_forget", "gm_v_norm", "gm_w_s", "gm_b_s",
                 "mix_out_norm", "ffn_pre_norm", "ffn_post_norm", "ple_norm")
WEIGHT_ORDER = ("mix_pre_norm", "mix_post_norm", "w_in", "b_forget", "gm_v_norm", "gm_w_s", "gm_b_s",
                "mix_out_norm", "w_out", "ffn_pre_norm", "ffn_post_norm", "w_ffn_in", "w_ffn_out",
                "w_ple", "ple_norm", "w_ple_gate")


def _tile(n, pref, unit=LANE):
    best = None
    t = unit
    while t <= min(n, pref):
        if n % t == 0:
            best = t
        t += unit
    return n if best is None else best


def _cparams(*semantics):
    return pltpu.CompilerParams(dimension_semantics=semantics or None, vmem_limit_bytes=VMEM_LIMIT)


NN = (((1,), (0,)), ((), ()))
NT = (((1,), (1,)), ((), ()))
TN = (((0,), (0,)), ((), ()))
_MM_AXES = {
    "nn": ("i", "k", "k", "j"), "nt": ("i", "k", "j", "k"), "tn": ("k", "i", "k", "j")}
_MM_DN = {"nn": NN, "nt": NT, "tn": TN}


def _mm(a, b, dims, out_dtype, name, a3=None, b3=None, o3=None, tm=1024, tn=1024, tk=None):
    ar, ac, br, bc = _MM_AXES[dims]
    letter = {"i": "m", "j": "n", "k": "k"}
    size = {}

    def measure(x, rows, cols, stacked):
        shape = x.shape
        if stacked is None:
            size.setdefault(letter[rows], shape[0])
            size.setdefault(letter[cols], shape[1])
        else:
            for ax, n in ((rows, shape[1]), (cols, shape[2])):
                size.setdefault(letter[ax], n * shape[0] if letter[ax] == stacked else n)

    measure(a, ar, ac, a3)
    measure(b, br, bc, b3)
    m, n, k = size["m"], size["n"], size["k"]
    slab = {}
    for x, stacked, rows, cols in ((a, a3, ar, ac), (b, b3, br, bc)):
        if stacked is not None:
            slab[stacked] = x.shape[1] if letter[rows] == stacked else x.shape[2]
    if o3 is not None:
        slab.setdefault(o3, slab.get(o3, None) or {"m": m, "n": n}[o3] // N_DEV)
    tk = tk or K_TILE
    tile = {"m": slab.get("m") or _tile(m, tm), "n": slab.get("n") or _tile(n, tn), "k": slab.get("k") or _tile(k, tk)}
    group = 1
    if a3 == "k" and b3 == "k":
        group = max(g for g in range(1, a.shape[0] + 1) if a.shape[0] % g == 0 and g * tile["k"] <= max(tk, tile["k"]))
    nk = k // (group * tile["k"])

    def spec(rows, cols, stacked):
        tr, tc = tile[letter[rows]], tile[letter[cols]]
        if stacked is None:
            return pl.BlockSpec((tr, tc), lambda i, j, kk: ({"i": i, "j": j, "k": kk}[rows], {"i": i, "j": j, "k": kk}[cols]))

        def imap(i, j, kk):
            g = {"i": i, "j": j, "k": kk}
            return (g[{"m": "i", "n": "j", "k": "k"}[stacked]],
                    0 if letter[rows] == stacked else g[rows], 0 if letter[cols] == stacked else g[cols])

        return pl.BlockSpec((group if stacked == "k" else 1, tr, tc), imap)

    dn = _MM_DN[dims]

    def body(a_ref, b_ref, o_ref, *acc):
        prod = None
        for g in range(group):
            av = a_ref[...] if a3 is None else a_ref[g]
            bv = b_ref[...] if b3 is None else b_ref[g]
            term = lax.dot_general(av.astype(BF16), bv.astype(BF16), dn, preferred_element_type=F32)
            prod = term if prod is None else prod + term

        def emit(val):
            if o3 is None:
                o_ref[...] = val.astype(out_dtype)
            else:
                o_ref[0] = val.astype(out_dtype)

        if nk == 1:
            emit(prod)
            return
        (acc_ref,) = acc
        kk = pl.program_id(2)

        @pl.when(kk == 0)
        def _():
            acc_ref[...] = prod

        @pl.when(kk > 0)
        def _():
            acc_ref[...] += prod

        @pl.when(kk == nk - 1)
        def _():
            emit(acc_ref[...])

    if o3 is None:
        out_shape = (m, n)
    elif o3 == "m":
        out_shape = (m // tile["m"], tile["m"], n)
    else:
        out_shape = (n // tile["n"], m, tile["n"])
    return pl.pallas_call(
        body, name=name, out_shape=jax.ShapeDtypeStruct(out_shape, out_dtype),
        grid=(m // tile["m"], n // tile["n"], nk),
        in_specs=[spec(ar, ac, a3), spec(br, bc, b3)], out_specs=spec("i", "j", o3),
        scratch_shapes=[] if nk == 1 else [pltpu.VMEM((tile["m"], tile["n"]), F32)],
        compiler_params=_cparams("parallel", "parallel", "arbitrary"),
    )(a, b)


def _rowwise(fn, rows, vecs, outs, reds, name, ts):
    rows = [r if isinstance(r, tuple) else (r, 0, r.shape[1]) for r in rows]
    s = rows[0][0].shape[0]
    ts = _tile(s, ts, 8)
    nr, nv, no = len(rows), len(vecs), len(outs)

    def body(*refs):
        vals = fn(*[r[...] for r in refs[:nr + nv]])
        vals = vals if isinstance(vals, tuple) else (vals,)
        o_refs = refs[nr + nv:nr + nv + no]
        r_refs = refs[nr + nv + no:]
        for o_ref, val in zip(o_refs, vals[:no]):
            o_ref[...] = val.astype(o_ref.dtype)
        if r_refs:
            @pl.when(pl.program_id(0) == 0)
            def _():
                for r_ref in r_refs:
                    r_ref[...] = jnp.zeros_like(r_ref)

            for r_ref, val in zip(r_refs, vals[no:]):
                r_ref[...] += val

    in_specs = [pl.BlockSpec((ts, w), functools.partial(lambda i, cb: (i, cb), cb=cb)) for _, cb, w in rows]
    in_specs += [pl.BlockSpec(v.shape, lambda i: (0, 0)) for v in vecs]
    out_specs = [pl.BlockSpec((ts, c), lambda i: (i, 0)) for c, _ in outs]
    out_specs += [pl.BlockSpec((1, c), lambda i: (0, 0)) for c in reds]
    out_shape = [jax.ShapeDtypeStruct((s, c), dt) for c, dt in outs]
    out_shape += [jax.ShapeDtypeStruct((1, c), F32) for c in reds]
    return pl.pallas_call(
        body, name=name, out_shape=out_shape, grid=(s // ts,), in_specs=in_specs, out_specs=out_specs,
        compiler_params=_cparams("arbitrary" if reds else "parallel"),
    )(*[r[0] for r in rows], *vecs)


def _rms(x):
    r = lax.rsqrt(jnp.mean(x * x, axis=-1, keepdims=True) + EPS)
    return x * r, r


def _rms_bwd(xhat, r, dyg):
    return r * (dyg - xhat * jnp.mean(dyg * xhat, axis=-1, keepdims=True))


def _colsum(x):
    return jnp.sum(x, axis=0, keepdims=True)


def _sigmoid(x):
    return 1.0 / (1.0 + jnp.exp(-x))


GELU_C = math.sqrt(2.0 / math.pi)
GELU_A = 0.044715


def _gelu(x):
    return 0.5 * x * (1.0 + jnp.tanh(GELU_C * (x + GELU_A * x * x * x)))


def _gelu_grad(x):
    t = jnp.tanh(GELU_C * (x + GELU_A * x * x * x))
    return 0.5 * (1.0 + t) + 0.5 * x * (1.0 - t * t) * GELU_C * (1.0 + 3.0 * GELU_A * x * x)


def _ffn_in_swiglu(hn, wg):
    s, d = hn.shape
    g2, n, _ = wg.shape
    g = g2 // 2
    tm = _tile(s, 1024)

    def body(h_ref, wa_ref, wb_ref, ab_ref, t_ref):
        hv = h_ref[...]
        a = lax.dot_general(hv, wa_ref[0], NT, preferred_element_type=F32)
        b = lax.dot_general(hv, wb_ref[0], NT, preferred_element_type=F32)
        ab_ref[0, 0] = a.astype(BF16)
        ab_ref[1, 0] = b.astype(BF16)
        t_ref[0] = (a * _sigmoid(a) * b).astype(BF16)

    return pl.pallas_call(
        body, name="mm_ffn_in_swiglu",
        out_shape=(jax.ShapeDtypeStruct((2, g, s, n), BF16), jax.ShapeDtypeStruct((g, s, n), BF16)),
        grid=(s // tm, g),
        in_specs=[pl.BlockSpec((tm, d), lambda i, j: (i, 0)), pl.BlockSpec((1, n, d), lambda i, j: (j, 0, 0)),
                  pl.BlockSpec((1, n, d), lambda i, j: (j + g, 0, 0))],
        out_specs=(pl.BlockSpec((2, 1, tm, n), lambda i, j: (0, j, i, 0)),
                   pl.BlockSpec((1, tm, n), lambda i, j: (j, i, 0))),
        compiler_params=_cparams("parallel", "parallel"),
    )(hn, wg, wg)


def _swiglu_bwd(ab, dt, ts):
    _, g, s, n = ab.shape
    ts = _tile(s, ts, 8)

    def body(ab_ref, dt_ref, t_ref, dab_ref):
        a = ab_ref[0, 0].astype(F32)
        b = ab_ref[1, 0].astype(F32)
        dt = dt_ref[0].astype(F32)
        sig = _sigmoid(a)
        silu = a * sig
        t_ref[0] = (silu * b).astype(BF16)
        dab_ref[0, 0] = (dt * b * (sig * (1.0 + a * (1.0 - sig)))).astype(BF16)
        dab_ref[1, 0] = (dt * silu).astype(BF16)

    both = pl.BlockSpec((2, 1, ts, n), lambda j, i: (0, j, i, 0))
    one = pl.BlockSpec((1, ts, n), lambda j, i: (j, i, 0))
    return pl.pallas_call(
        body, name="swiglu_bwd",
        out_shape=(jax.ShapeDtypeStruct((g, s, n), BF16), jax.ShapeDtypeStruct((2, g, s, n), BF16)),
        grid=(g, s // ts), in_specs=[both, one], out_specs=(one, both),
        compiler_params=_cparams("parallel", "parallel"),
    )(ab, dt)


def _forget_fwd(fl_t, b_col):
    h, s = fl_t.shape
    nb = s // LANE

    def body(fl_ref, b_ref, c_ref):
        upper = (lax.broadcasted_iota(jnp.int32, (LANE, LANE), 0)
                 <= lax.broadcasted_iota(jnp.int32, (LANE, LANE), 1)).astype(F32)

        def step(i, carry):
            x = fl_ref[i] + b_ref[...]
            lf = jnp.minimum(x, 0.0) - jnp.log(1.0 + jnp.exp(-jnp.abs(x)))
            cs = jnp.dot(lf, upper, precision=lax.Precision.HIGHEST, preferred_element_type=F32) + carry
            c_ref[i] = cs
            return cs[:, LANE - 1:LANE]

        lax.fori_loop(0, nb, step, jnp.zeros((h, 1), F32))

    out = pl.pallas_call(
        body, name="forget_fwd", out_shape=jax.ShapeDtypeStruct((nb, h, LANE), F32),
        compiler_params=_cparams(),
    )(fl_t.reshape(h, nb, LANE).transpose(1, 0, 2), b_col)
    return out.transpose(1, 0, 2).reshape(h, s)


def _forget_bwd(dc_t, fl_t, b_col):
    h, s = fl_t.shape
    nb = s // LANE

    def body(dc_ref, fl_ref, b_ref, dfl_ref, db_ref):
        lower = (lax.broadcasted_iota(jnp.int32, (LANE, LANE), 0)
                 >= lax.broadcasted_iota(jnp.int32, (LANE, LANE), 1)).astype(F32)

        def step(t, carry):
            tail, db = carry
            i = nb - 1 - t
            rc = jnp.dot(dc_ref[i], lower, precision=lax.Precision.HIGHEST, preferred_element_type=F32) + tail
            dfl = rc * (1.0 - _sigmoid(fl_ref[i] + b_ref[...]))
            dfl_ref[i] = dfl
            return rc[:, 0:1], db + jnp.sum(dfl, axis=1, keepdims=True)

        _, db = lax.fori_loop(0, nb, step, (jnp.zeros((h, 1), F32), jnp.zeros((h, 1), F32)))
        db_ref[...] = db

    blocked = lambda a: a.reshape(h, nb, LANE).transpose(1, 0, 2)
    dfl, db = pl.pallas_call(
        body, name="forget_bwd",
        out_shape=(jax.ShapeDtypeStruct((nb, h, LANE), F32), jax.ShapeDtypeStruct((h, 1), F32)),
        compiler_params=_cparams(),
    )(blocked(dc_t), blocked(fl_t), b_col)
    return dfl.transpose(1, 0, 2).reshape(h, s), db


N_PAIRS = N_HEADS // 2


def _causal_mask(t):
    return lax.broadcasted_iota(jnp.int32, (t, t), 0) >= lax.broadcasted_iota(jnp.int32, (t, t), 1)


def _head_lanes():
    return lax.broadcasted_iota(jnp.int32, (1, 2 * HEAD_DIM), 1) < HEAD_DIM


def _pick(x2, first, hh):
    zero = jnp.zeros_like(x2)
    return jnp.where(first, x2, zero) if hh == 0 else jnp.where(first, zero, x2)


BIAS_TERMS = 3


def _attn_prep(z, c, ts):
    s = z.shape[0]
    ts = _tile(s, ts, 16)
    w = 2 * HEAD_DIM

    def body(q_ref, k_ref, v_ref, c_ref, qa_ref, ka_ref, vb_ref):
        lane = lax.broadcasted_iota(jnp.int32, (1, w), 1)
        first = lane < HEAD_DIM
        cv = c_ref[...]
        for h in range(N_HEADS):
            pair = slice((h // 2) * w, (h // 2 + 1) * w)
            qh = q_ref[:, pair] * ATT_SCALE
            kh = k_ref[:, pair]
            if h % 2:
                qh = pltpu.roll(qh, HEAD_DIM, 1)
                kh = pltpu.roll(kh, HEAD_DIM, 1)
            rest = cv[:, h:h + 1]
            q_tail = jnp.zeros((1, w), F32)
            k_tail = jnp.zeros((1, w), F32)
            for t in range(BIAS_TERMS):
                term = rest.astype(BF16).astype(F32)
                rest = rest - term
                q_tail = jnp.where(lane == HEAD_DIM + t, term, jnp.where(lane == HEAD_DIM + BIAS_TERMS + t, 1.0, q_tail))
                k_tail = jnp.where(lane == HEAD_DIM + t, 1.0, jnp.where(lane == HEAD_DIM + BIAS_TERMS + t, -term, k_tail))
            qa_ref[:, h * w:(h + 1) * w] = jnp.where(first, qh, q_tail).astype(BF16)
            ka_ref[:, h * w:(h + 1) * w] = jnp.where(first, kh, k_tail).astype(BF16)
        vb_ref[...] = v_ref[...].astype(BF16)

    col = lambda cb: pl.BlockSpec((ts, D_ATT), lambda i: (i, cb))
    wide = pl.BlockSpec((ts, N_HEADS * w), lambda i: (i, 0))
    return pl.pallas_call(
        body, name="attn_prep",
        out_shape=(jax.ShapeDtypeStruct((s, N_HEADS * w), BF16), jax.ShapeDtypeStruct((s, N_HEADS * w), BF16),
                   jax.ShapeDtypeStruct((s, D_ATT), BF16)),
        grid=(s // ts,), in_specs=[col(0), col(1), col(2), pl.BlockSpec((ts, N_HEADS), lambda i: (i, 0))],
        out_specs=(wide, wide, col(0)),
        compiler_params=_cparams("parallel"),
    )(z, z, z, c)


def _attn_fwd(qa, ka, vb, tq):
    s = qa.shape[0]
    nq = s // tq
    w = 2 * HEAD_DIM

    def body(q_ref, k_ref, v_ref, o_ref, lse_ref):
        i = pl.program_id(1)
        first = _head_lanes()
        q2 = q_ref[...]

        def block(j, carry, masked):
            off = pl.multiple_of(j * tq, tq)
            k2 = k_ref[pl.ds(off, tq), :]
            v2 = v_ref[pl.ds(off, tq), :]
            new = []
            for hh in range(2):
                m, l, acc = carry[hh]
                sc = lax.dot_general(q2[:, hh * w:(hh + 1) * w], k2[:, hh * w:(hh + 1) * w], NT,
                                     preferred_element_type=F32)
                if masked:
                    sc = jnp.where(_causal_mask(tq), sc, NEG_INF)
                m_new = jnp.maximum(m, jnp.max(sc, axis=-1, keepdims=True))
                alpha = jnp.exp(m - m_new)
                p = jnp.exp(sc - m_new)
                l = alpha * l + jnp.sum(p, axis=-1, keepdims=True)
                p_hi = p.astype(BF16)
                p_lo = (p - p_hi.astype(F32)).astype(BF16)
                acc = (alpha * acc + jnp.dot(p_hi, v2, preferred_element_type=F32)
                       + jnp.dot(p_lo, v2, preferred_element_type=F32))
                new.append((m_new, l, acc))
            return tuple(new)

        one = (jnp.full((tq, 1), NEG_INF, F32), jnp.zeros((tq, 1), F32), jnp.zeros((tq, w), F32))
        carry = lax.fori_loop(0, i, lambda j, c: block(j, c, False), (one, one))
        (m0, l0, a0), (m1, l1, a1) = block(i, carry, True)
        o_ref[...] = jnp.where(first, a0 / l0, a1 / l1)
        lse_ref[0] = m0 + jnp.log(l0)
        lse_ref[1] = m1 + jnp.log(l1)

    return pl.pallas_call(
        body, name="attn_fwd",
        out_shape=(jax.ShapeDtypeStruct((s, D_ATT), F32), jax.ShapeDtypeStruct((N_HEADS, s, 1), F32)),
        grid=(N_PAIRS, nq),
        in_specs=[pl.BlockSpec((tq, 2 * w), lambda hp, i: (i, hp)),
                  pl.BlockSpec((s, 2 * w), lambda hp, i: (0, hp)),
                  pl.BlockSpec((s, w), lambda hp, i: (0, hp))],
        out_specs=(pl.BlockSpec((tq, w), lambda hp, i: (i, hp)),
                   pl.BlockSpec((2, tq, 1), lambda hp, i: (hp, i, 0))),
        compiler_params=_cparams("parallel", "parallel"),
    )(qa, ka, vb)


def _attn_delta(o, do, tq):
    s = o.shape[0]
    w = 2 * HEAD_DIM

    def body(o_ref, do_ref, d_ref):
        first = _head_lanes()
        prod = o_ref[...] * do_ref[...].astype(F32)
        d_ref[0] = jnp.sum(_pick(prod, first, 0), axis=-1, keepdims=True)
        d_ref[1] = jnp.sum(_pick(prod, first, 1), axis=-1, keepdims=True)

    blk = pl.BlockSpec((tq, w), lambda hp, i: (i, hp))
    return pl.pallas_call(
        body, name="attn_delta", out_shape=jax.ShapeDtypeStruct((N_HEADS, s, 1), F32), grid=(N_PAIRS, s // tq),
        in_specs=[blk, blk], out_specs=pl.BlockSpec((2, tq, 1), lambda hp, i: (hp, i, 0)),
        compiler_params=_cparams("parallel", "parallel"),
    )(o, do)


def _attn_bwd(qa, ka, vb, do, lse, delta, tq):
    s = qa.shape[0]
    nq = s // tq
    w = 2 * HEAD_DIM

    def body(q_ref, do_ref, lse_ref, dl_ref, k_ref, v_ref, dq_ref, dk_ref, dv_ref, dc_ref, dq_acc):
        j = pl.program_id(1)
        first = _head_lanes()

        @pl.when(j == 0)
        def _():
            dq_acc[...] = jnp.zeros_like(dq_acc)

        k2 = k_ref[...]
        v2 = v_ref[...]

        def step(i, carry, masked):
            off = pl.multiple_of(i * tq, tq)
            rows = pl.ds(off, tq)
            q2 = q_ref[rows, :]
            do2 = do_ref[rows, :]
            new, dqs = [], []
            for hh in range(2):
                dk, dv, dcs = carry[hh]
                qh = q2[:, hh * w:(hh + 1) * w]
                kh = k2[:, hh * w:(hh + 1) * w]
                sc = lax.dot_general(qh, kh, NT, preferred_element_type=F32)
                if masked:
                    sc = jnp.where(_causal_mask(tq), sc, NEG_INF)
                p = jnp.exp(sc - lse_ref[hh, rows, :])
                dv = dv + lax.dot_general(do2, p.astype(BF16), TN, preferred_element_type=F32)
                dp = lax.dot_general(_pick(do2, first, hh), v2, NT, preferred_element_type=F32)
                ds = p * (dp - dl_ref[hh, rows, :])
                dsb = ds.astype(BF16)
                dk = dk + lax.dot_general(qh, dsb, TN, preferred_element_type=F32)
                dqs.append(jnp.dot(dsb, kh, preferred_element_type=F32))
                new.append((dk, dv, dcs + jnp.sum(ds, axis=0, keepdims=True)))
            dq_acc[rows, :] += jnp.where(first, dqs[0], pltpu.roll(dqs[1], HEAD_DIM, 1)) * ATT_SCALE
            return tuple(new)

        one = (jnp.zeros((w, tq), F32), jnp.zeros((w, tq), F32), jnp.zeros((1, tq), F32))
        carry = step(j, (one, one), True)
        (dk0, dv0, dc0), (dk1, dv1, dc1) = lax.fori_loop(j + 1, nq, lambda i, c: step(i, c, False), carry)
        dk_ref[...] = jnp.where(first, dk0.T, pltpu.roll(dk1.T, HEAD_DIM, 1)).astype(BF16)
        dv_ref[...] = jnp.where(first, dv0.T, dv1.T).astype(BF16)
        dc_ref[0, 0] = -dc0
        dc_ref[1, 0] = -dc1

        @pl.when(j == nq - 1)
        def _():
            dq_ref[...] = dq_acc[...].astype(BF16)

    whole = lambda width: pl.BlockSpec((s, width), lambda hp, j: (0, hp))
    whole_heads = pl.BlockSpec((2, s, 1), lambda hp, j: (hp, 0, 0))
    blk = lambda width: pl.BlockSpec((tq, width), lambda hp, j: (j, hp))
    crow = pl.BlockSpec((2, 1, 1, tq), lambda hp, j: (hp, j, 0, 0))
    return pl.pallas_call(
        body, name="attn_bwd",
        out_shape=(jax.ShapeDtypeStruct((s, D_ATT), BF16), jax.ShapeDtypeStruct((s, D_ATT), BF16),
                   jax.ShapeDtypeStruct((s, D_ATT), BF16), jax.ShapeDtypeStruct((N_HEADS, nq, 1, tq), F32)),
        grid=(N_PAIRS, nq),
        in_specs=[whole(2 * w), whole(w), whole_heads, whole_heads, blk(2 * w), blk(w)],
        out_specs=(whole(w), blk(w), blk(w), crow),
        scratch_shapes=[pltpu.VMEM((s, w), F32)],
        compiler_params=_cparams("parallel", "arbitrary"),
    )(qa, do, lse, delta, ka, vb)


def _pair_sums(x, first):
    total = jnp.sum(x, axis=-1, keepdims=True)
    head = jnp.sum(jnp.where(first, x, 0.0), axis=-1, keepdims=True)
    return head, total - head


def _pair_mean(x, first):
    head, tail = _pair_sums(x, first)
    return jnp.where(first, head, tail) * (1.0 / GROUP_DIM)


def _gm_pair_norm(v2, first):
    d = v2 - _pair_mean(v2, first)
    rstd = lax.rsqrt(_pair_mean(d * d, first) + EPS)
    return d * rstd, rstd


def _gm_pair_mix(w_ref, pr, rhs, first):
    return jnp.where(first, jnp.dot(w_ref[2 * pr], rhs, preferred_element_type=F32),
                     jnp.dot(w_ref[2 * pr + 1], rhs, preferred_element_type=F32))


def _gmlp_fwd(z, wt, bs_t, vgain):
    s = z.shape[0]

    def body(gu_ref, gv_ref, wt_ref, bs_ref, vg_ref, o_ref):
        first = _head_lanes()
        for pr in range(N_GROUPS // 2):
            sl = slice(2 * pr * GROUP_DIM, 2 * (pr + 1) * GROUP_DIM)
            vhat, _ = _gm_pair_norm(_gelu(gv_ref[:, sl]), first)
            vn = (vhat * vg_ref[:, sl]).astype(BF16)
            bias = jnp.where(first, bs_ref[:, 2 * pr:2 * pr + 1], bs_ref[:, 2 * pr + 1:2 * pr + 2])
            o_ref[:, sl] = _gelu(gu_ref[:, sl]) * (_gm_pair_mix(wt_ref, pr, vn, first) + bias)

    full = lambda a: pl.BlockSpec(a.shape, lambda n: (0,) * a.ndim)
    return pl.pallas_call(
        body, name="gmlp_fwd", out_shape=jax.ShapeDtypeStruct((s, D_GM), F32), grid=(s // CHUNK,),
        in_specs=[pl.BlockSpec((CHUNK, D_GM), lambda n: (n, 3)), pl.BlockSpec((CHUNK, D_GM), lambda n: (n, 4)),
                  full(wt), full(bs_t), full(vgain)],
        out_specs=pl.BlockSpec((CHUNK, D_GM), lambda n: (n, 0)),
        compiler_params=_cparams("parallel"),
    )(z, z, wt, bs_t, vgain)


def _gmlp_bwd(z, dgm, wt, wt_t, bs_t, vgain):
    s = z.shape[0]

    def body(gu_ref, gv_ref, dgm_ref, wt_ref, wtt_ref, bs_ref, vg_ref, dgu_ref, dgv_ref, dwt_ref, dbs_ref, dvg_ref):
        @pl.when(pl.program_id(0) == 0)
        def _():
            dwt_ref[...] = jnp.zeros_like(dwt_ref)
            dbs_ref[...] = jnp.zeros_like(dbs_ref)
            dvg_ref[...] = jnp.zeros_like(dvg_ref)

        first = _head_lanes()
        for pr in range(N_GROUPS // 2):
            g0, g1 = 2 * pr, 2 * pr + 1
            sl = slice(g0 * GROUP_DIM, (g1 + 1) * GROUP_DIM)
            gu = gu_ref[:, sl]
            gv = gv_ref[:, sl]
            dgm = dgm_ref[:, sl]
            vhat, rstd = _gm_pair_norm(_gelu(gv), first)
            gain = vg_ref[:, sl]
            vn = (vhat * gain).astype(BF16)
            bias = jnp.where(first, bs_ref[:, g0:g0 + 1], bs_ref[:, g1:g1 + 1])
            mixed = _gm_pair_mix(wt_ref, pr, vn, first) + bias
            dgu_ref[:, sl] = (dgm * mixed * _gelu_grad(gu)).astype(BF16)
            dmixed = dgm * _gelu(gu)
            db0, db1 = _pair_sums(dmixed, first)
            dbs_ref[:, g0:g0 + 1] += db0
            dbs_ref[:, g1:g1 + 1] += db1
            dwt_ref[g0] += lax.dot_general(_pick(dmixed, first, 0).astype(BF16), vn, NT, preferred_element_type=F32)
            dwt_ref[g1] += lax.dot_general(_pick(dmixed, first, 1).astype(BF16), vn, NT, preferred_element_type=F32)
            dvn = _gm_pair_mix(wtt_ref, pr, dmixed.astype(BF16), first)
            dvg_ref[:, sl] += _colsum(dvn * vhat)
            dvhat = dvn * gain
            dvf = rstd * (dvhat - _pair_mean(dvhat, first) - vhat * _pair_mean(dvhat * vhat, first))
            dgv_ref[:, sl] = (dvf * _gelu_grad(gv)).astype(BF16)

    full = lambda a: pl.BlockSpec(a.shape, lambda n: (0,) * a.ndim)
    chunk = pl.BlockSpec((CHUNK, D_GM), lambda n: (n, 0))
    return pl.pallas_call(
        body, name="gmlp_bwd",
        out_shape=(jax.ShapeDtypeStruct((s, D_GM), BF16), jax.ShapeDtypeStruct((s, D_GM), BF16),
                   jax.ShapeDtypeStruct(wt.shape, F32), jax.ShapeDtypeStruct(bs_t.shape, F32),
                   jax.ShapeDtypeStruct(vgain.shape, F32)),
        grid=(s // CHUNK,),
        in_specs=[pl.BlockSpec((CHUNK, D_GM), lambda n: (n, 3)), pl.BlockSpec((CHUNK, D_GM), lambda n: (n, 4)),
                  chunk, full(wt), full(wt_t), full(bs_t), full(vgain)],
        out_specs=(chunk, chunk, full(wt), full(bs_t), full(vgain)),
        compiler_params=_cparams("arbitrary"),
    )(z, z, dgm, wt, wt_t, bs_t, vgain)


def _layer_fwd(h0, p_i, w, tq, late):
    s, d = h0.shape
    nq = s // tq
    sv = {"h0": h0}

    (hn1,) = _rowwise(lambda h, g: _rms(h)[0] * g, [h0], [w["mix_pre_norm"]], [(d, BF16)], [], "pre_mix", ROW_TILE)
    z = _mm(hn1, w["w_in"], "nn", F32, "mm_in")
    fl_t = z[:, F_OFF:F_OFF + N_HEADS].T
    c_t = _forget_fwd(fl_t, w["b_forget"])
    qa, ka, vb = _attn_prep(z, c_t.T, ROW_TILE)
    att, lse = _attn_fwd(qa, ka, vb, tq)
    gm = _gmlp_fwd(z, w["wt"], w["bs_t"], w["gm_v_norm"])
    w = dict(w, **late(att))

    def mix_out(att, gm, g):
        return jnp.concatenate([_rms(att)[0] * g[:, :D_ATT], _rms(gm)[0] * g[:, D_ATT:]], axis=-1)

    (mc,) = _rowwise(mix_out, [att, gm], [w["mix_out_norm"]], [(D_ATT + D_GM, BF16)], [], "mix_out", ROW_TILE)
    y1 = _mm(mc, w["w_out"], "nn", F32, "mm_out")

    def post_mix(h0, y1, gpost, gpre):
        h1 = h0 + _rms(y1)[0] * gpost
        return h1, _rms(h1)[0] * gpre

    h1, hn2 = _rowwise(post_mix, [h0, y1], [w["mix_post_norm"], w["ffn_pre_norm"]],
                       [(d, F32), (d, BF16)], [], "post_mix", ROW_TILE)
    ab, t = _ffn_in_swiglu(hn2, w["w_ffn_in"])
    y2 = _mm(t, w["w_ffn_out"], "nn", F32, "mm_ffn_out", a3="k", b3="k")

    def post_ffn(h1, y2, g):
        h2 = h1 + _rms(y2)[0] * g
        return h2, _rms(h2)[0]

    h2, hr = _rowwise(post_ffn, [h1, y2], [w["ffn_post_norm"]], [(d, F32), (d, BF16)], [], "post_ffn", ROW_TILE)
    gl = _mm(hr, w["w_ple_gate"], "nn", F32, "mm_gate")
    pe = _mm(p_i, w["w_ple"], "nn", F32, "mm_ple")
    (h3,) = _rowwise(lambda h2, gl, pe, g: h2 + _sigmoid(gl) * (_rms(pe)[0] * g), [h2, gl, pe], [w["ple_norm"]],
                     [(d, F32)], [], "ple_out", ROW_TILE)
    sv.update(hn1=hn1, z=z, fl_t=fl_t, qa=qa, ka=ka, vb=vb, lse=lse, att=att, gm=gm,
              mc=mc, y1=y1, h1=h1, hn2=hn2, ab=ab, y2=y2, h2=h2, hr=hr, gl=gl, pe=pe, p_i=p_i)
    return h3, sv


def _layer_bwd(dh3, sv, w, tq, mid):
    s, d = dh3.shape
    g = {}
    by_rows = lambda a: a.reshape(N_DEV, -1, a.shape[-1])
    by_cols = lambda a: jnp.stack(jnp.split(a, N_DEV, axis=-1))

    def ple_bwd(dh3, gl, pe, gple):
        gate = _sigmoid(gl)
        pehat, rpe = _rms(pe)
        dgl = dh3 * (pehat * gple) * gate * (1.0 - gate)
        de = dh3 * gate
        return dgl, _rms_bwd(pehat, rpe, de * gple), _colsum(de * pehat)

    dgl, dpe, g["ple_norm"] = _rowwise(ple_bwd, [dh3, sv["gl"], sv["pe"]], [w["ple_norm"]],
                                       [(d, BF16), (d, BF16)], [d], "ple_bwd", ROW_TILE)
    g["w_ple_gate"] = by_rows(_mm(sv["hr"], dgl, "tn", BF16, "mm_dgate"))
    dhr = _mm(dgl, w["w_ple_gate"], "nt", F32, "mm_dhr")
    g["w_ple"] = by_cols(_mm(sv["p_i"], dpe, "tn", BF16, "mm_dple"))

    def ffn_post_bwd(dh3, dhr, h2, y2, gpost):
        h2hat, r2 = _rms(h2)
        dh2 = dh3 + _rms_bwd(h2hat, r2, dhr)
        y2hat, ry = _rms(y2)
        return dh2, _rms_bwd(y2hat, ry, dh2 * gpost), _colsum(dh2 * y2hat)

    dh2, dy2, g["ffn_post_norm"] = _rowwise(ffn_post_bwd, [dh3, dhr, sv["h2"], sv["y2"]], [w["ffn_post_norm"]],
                                            [(d, F32), (d, BF16)], [d], "ffn_post_bwd", ROW_TILE)
    dt = _mm(dy2, w["w_ffn_out"], "nt", BF16, "mm_dt", b3="n", o3="n")
    t, dab = _swiglu_bwd(sv["ab"], dt, 2 * ROW_TILE)
    dab = dab.reshape((N_DEV,) + dab.shape[2:])
    g["w_ffn_out"] = by_rows(_mm(t, dy2, "tn", BF16, "mm_dffn_out", a3="m", o3="m"))
    dhn2 = _mm(dab, w["w_ffn_in"], "nn", F32, "mm_dhn2", a3="k", b3="k")
    g["w_ffn_in"] = _mm(dab, sv["hn2"], "tn", BF16, "mm_dffn_in", a3="m", o3="m")

    def mix_post_bwd(dh2, dhn2, h1, y1, gpre, gpost):
        h1hat, r1 = _rms(h1)
        dh1 = dh2 + _rms_bwd(h1hat, r1, dhn2 * gpre)
        y1hat, ry = _rms(y1)
        return dh1, _rms_bwd(y1hat, ry, dh1 * gpost), _colsum(dhn2 * h1hat), _colsum(dh1 * y1hat)

    dh1, dy1, g["ffn_pre_norm"], g["mix_post_norm"] = _rowwise(
        mix_post_bwd, [dh2, dhn2, sv["h1"], sv["y1"]], [w["ffn_pre_norm"], w["mix_post_norm"]],
        [(d, F32), (d, BF16)], [d, d], "mix_post_bwd", ROW_TILE)
    dmc = _mm(dy1, w["w_out"], "nt", F32, "mm_dmc")
    g["w_out"] = by_rows(_mm(sv["mc"], dy1, "tn", BF16, "mm_dout"))
    w = dict(w, **mid(g, dmc))

    def mix_out_bwd(da, dg, att, gm, gain):
        atthat, ra = _rms(att)
        gmhat, rg = _rms(gm)
        dgain = jnp.concatenate([_colsum(da * atthat), _colsum(dg * gmhat)], axis=-1)
        return _rms_bwd(atthat, ra, da * gain[:, :D_ATT]), _rms_bwd(gmhat, rg, dg * gain[:, D_ATT:]), dgain

    datt, dgm, g["mix_out_norm"] = _rowwise(
        mix_out_bwd, [(dmc, 0, D_ATT), (dmc, 1, D_GM), sv["att"], sv["gm"]], [w["mix_out_norm"]],
        [(D_ATT, BF16), (D_GM, F32)], [D_ATT + D_GM], "mix_out_bwd", ROW_TILE)

    dgu, dgv, dwt, dbs_t, g["gm_v_norm"] = _gmlp_bwd(sv["z"], dgm, w["wt"], w["wt_t"], w["bs_t"], w["gm_v_norm"])
    g["gm_w_s"] = dwt * jnp.tril(jnp.ones((CHUNK, CHUNK), F32))[None]
    g["gm_b_s"] = dbs_t.T

    delta = _attn_delta(sv["att"], datt, tq)
    dq, dk, dv, dc_row = _attn_bwd(sv["qa"], sv["ka"], sv["vb"], datt, sv["lse"], delta, tq)
    dfl_t, db = _forget_bwd(dc_row.reshape(N_HEADS, s), sv["fl_t"], w["b_forget"])
    g["b_forget"] = db.reshape(1, N_HEADS)
    tail = jnp.concatenate([dfl_t.T.astype(BF16), jnp.zeros((s, D_IN_PAD - F_OFF - N_HEADS), BF16)], axis=-1)
    dz = jnp.concatenate([dq, dk, dv, dgu, dgv, tail], axis=-1)
    dhn1 = _mm(dz, w["w_in"], "nt", F32, "mm_dhn1")
    din = _mm(sv["hn1"], dz, "tn", BF16, "mm_din")
    din = jnp.concatenate([din[:, :3 * D_ATT], din[:, F_OFF:F_OFF + N_HEADS], din[:, 3 * D_ATT:F_OFF]], axis=-1)
    g["w_in"] = by_cols(din)

    def mix_pre_bwd(dh1, dhn1, h0, gpre):
        h0hat, r0 = _rms(h0)
        return dh1 + _rms_bwd(h0hat, r0, dhn1 * gpre), _colsum(dhn1 * h0hat)

    dh0, g["mix_pre_norm"] = _rowwise(mix_pre_bwd, [dh1, dhn1, sv["h0"]], [w["mix_pre_norm"]],
                                      [(d, F32)], [d], "mix_pre_bwd", ROW_TILE)
    return dh0, g


ANY = pl.BlockSpec(memory_space=pl.ANY)


def _all_gather(xs, layer, name):
    n = len(xs)

    def body(*refs):
        x_refs, out_refs = refs[:n], refs[n:2 * n]
        send_sems, recv_sems, local_sems = refs[2 * n:]
        x, y, c = lax.axis_index("x"), lax.axis_index("y"), lax.axis_index("c")
        me, sibling = (x, y, c), (x, y, 1 - c)
        chips = [(1 - x, y), (x, 1 - y), (1 - x, 1 - y)]

        def shard(a):
            return x_refs[a] if layer is None else x_refs[a].at[layer]

        def rows(a, px, py, pc):
            return out_refs[a].at[4 * px + 2 * py + pc]

        def copy(a, kk, block, to, from_shard=False):
            return pltpu.make_async_remote_copy(
                src_ref=shard(a) if from_shard else rows(a, *block), dst_ref=rows(a, *block),
                send_sem=send_sems.at[7 * a + kk], recv_sem=recv_sems.at[7 * a + kk],
                device_id=to, device_id_type=MESH)

        mine = [pltpu.make_async_copy(shard(a), rows(a, *me), local_sems.at[a]) for a in range(n)]
        for cp in mine:
            cp.start()
        first = []
        for a in range(n):
            first.append(copy(a, 0, me, sibling, from_shard=True))
            first += [copy(a, 1 + j, me, (*chip, c), from_shard=True) for j, chip in enumerate(chips)]
        for cp in first:
            cp.start()
        passed = []
        for j, chip in enumerate(chips):
            for a in range(n):
                copy(a, 1 + j, (*chip, c), me).wait_recv()
                passed.append(copy(a, 4 + j, (*chip, c), sibling))
                passed[-1].start()
        for a in range(n):
            copy(a, 0, sibling, me).wait_recv()
        for j, chip in enumerate(chips):
            for a in range(n):
                copy(a, 4 + j, (*chip, 1 - c), me).wait_recv()
        for cp in first + passed:
            cp.wait_send()
        for cp in mine:
            cp.wait()

    shapes = [x.shape if layer is None else x.shape[1:] for x in xs]
    return pl.pallas_call(
        body, name=name, out_shape=[jax.ShapeDtypeStruct((N_DEV,) + sh, x.dtype) for sh, x in zip(shapes, xs)],
        in_specs=[ANY] * n, out_specs=[ANY] * n,
        scratch_shapes=[pltpu.SemaphoreType.DMA((7 * n,)), pltpu.SemaphoreType.DMA((7 * n,)),
                        pltpu.SemaphoreType.DMA((n,))],
    )(*xs)


HBM = pl.BlockSpec(memory_space=pltpu.HBM)
SEMS = pl.BlockSpec(memory_space=pltpu.SEMAPHORE)
EFFECT = pltpu.SideEffectType.DATAFLOW_SIDE_EFFECTING
FLIPS = tuple((fx, fy, fc) for fx in (0, 1) for fy in (0, 1) for fc in (0, 1))[1:]


def _exchange_copies(src_refs, land_refs, send_sems, recv_sems, layer, scatter):
    x, y, c = lax.axis_index("x"), lax.axis_index("y"), lax.axis_index("c")
    me = 4 * x + 2 * y + c
    copies = []
    for a, (src, land) in enumerate(zip(src_refs, land_refs)):
        for f, (fx, fy, fc) in enumerate(FLIPS):
            px, py, pc = (1 - x if fx else x), (1 - y if fy else y), (1 - c if fc else c)
            if scatter:
                block = src.at[4 * px + 2 * py + pc]
            else:
                block = src if layer is None else src.at[layer]
            copies.append(pltpu.make_async_remote_copy(
                src_ref=block, dst_ref=land.at[me], send_sem=send_sems.at[7 * a + f], recv_sem=recv_sems.at[7 * a + f],
                device_id=(px, py, pc), device_id_type=MESH))
    return copies


def _exchange_start(srcs, lands, layer, scatter, name):
    n = len(srcs)

    def body(*refs):
        for cp in _exchange_copies(refs[:n], refs[n:2 * n], refs[2 * n], refs[2 * n + 1], layer, scatter):
            cp.start()
        token = refs[-1]
        token[...] = jnp.zeros_like(token)

    operands = list(srcs) + list(lands)
    outs = pl.pallas_call(
        body, name=name,
        out_shape=(pltpu.SemaphoreType.DMA((7 * n,)), pltpu.SemaphoreType.DMA((7 * n,)),
                   *[pltpu.HBM(a.shape, a.dtype) for a in operands], jax.ShapeDtypeStruct((8, LANE), F32)),
        in_specs=[HBM] * (2 * n),
        out_specs=(SEMS, SEMS, *[HBM] * (2 * n), pl.BlockSpec(memory_space=pltpu.VMEM)),
        input_output_aliases={i: 2 + i for i in range(2 * n)},
        compiler_params=pltpu.CompilerParams(has_side_effects=EFFECT),
    )(*[pltpu.with_memory_space_constraint(a, pltpu.HBM) for a in operands])
    return outs[0], outs[1], outs[2:2 + n], outs[2 + n:2 + 2 * n], outs[-1]


def _exchange_wait(started, after, layer, scatter, name):
    send_sems, recv_sems, srcs, lands, _ = started
    n = len(srcs)

    def body(*refs):
        for cp in _exchange_copies(refs[:n], refs[n:2 * n], refs[2 * n], refs[2 * n + 1], layer, scatter):
            cp.wait_send()
            cp.wait_recv()

    operands = list(srcs) + list(lands)
    outs = pl.pallas_call(
        body, name=name, out_shape=tuple(pltpu.HBM(a.shape, a.dtype) for a in operands),
        in_specs=[HBM] * (2 * n) + [SEMS, SEMS, ANY], out_specs=[HBM] * (2 * n),
        input_output_aliases={i: i for i in range(2 * n)},
        compiler_params=pltpu.CompilerParams(has_side_effects=EFFECT),
    )(*operands, send_sems, recv_sems, after)
    return outs[:n], outs[n:]


def _sum_devices(parts):
    _, r, c = parts.shape

    def body(p_ref, o_ref):
        acc = p_ref[0].astype(F32)
        for j in range(1, N_DEV):
            acc = acc + p_ref[j].astype(F32)
        o_ref[...] = acc

    return pl.pallas_call(
        body, name="small_sum", out_shape=jax.ShapeDtypeStruct((r, c), F32), grid=(r // SMALL_ROWS,),
        in_specs=[pl.BlockSpec((N_DEV, SMALL_ROWS, c), lambda i: (0, i, 0))],
        out_specs=pl.BlockSpec((SMALL_ROWS, c), lambda i: (i, 0)),
        compiler_params=_cparams("parallel"),
    )(parts)


def _adamw_math(w, g, m, v):
    m = ADAM_B1 * m + (1.0 - ADAM_B1) * g
    v = ADAM_B2 * v + (1.0 - ADAM_B2) * (g * g)
    m_hat = m / (1.0 - ADAM_B1 ** ADAM_STEP)
    v_hat = v / (1.0 - ADAM_B2 ** ADAM_STEP)
    return -ADAM_LR * (m_hat / (jnp.sqrt(v_hat) + ADAM_EPS) + ADAM_WD * w), m, v


def _adamw_shard(w, m, v, parts, layer, outs, name):
    _, a, b = w.shape
    ta = _tile(a, 256, 16)
    if outs is None:
        outs = [lax.empty(w.shape, F32) for _ in range(4)]

    def body(w_ref, m_ref, v_ref, p_ref, *refs):
        g_ref, d_ref, nm_ref, nv_ref = refs[4:]
        g = p_ref[0].astype(F32)
        for j in range(1, N_DEV):
            g = g + p_ref[j].astype(F32)
        g_ref[0] = g
        d_ref[0], nm_ref[0], nv_ref[0] = _adamw_math(w_ref[0], g, m_ref[0], v_ref[0])

    mine = pl.BlockSpec((1, ta, b), lambda i: (layer, i, 0))
    return pl.pallas_call(
        body, name=name, out_shape=[jax.ShapeDtypeStruct(w.shape, F32)] * 4, grid=(a // ta,),
        in_specs=[mine, mine, mine, pl.BlockSpec((N_DEV, ta, b), lambda i: (0, i, 0))] + [ANY] * 4,
        out_specs=[mine] * 4, input_output_aliases={4 + k: k for k in range(4)},
        compiler_params=_cparams("parallel"),
    )(w, m, v, parts, *outs)


def _pack_small(pieces):
    flat = jnp.concatenate([p.reshape(-1) for p in pieces])
    total = -(-flat.shape[0] // (SMALL_COLS * SMALL_ROWS)) * SMALL_COLS * SMALL_ROWS
    return jnp.pad(flat, (0, total - flat.shape[0])).reshape(-1, SMALL_COLS)


def kernel(x, p, mix_pre_norm, mix_post_norm, w_in, b_forget, gm_v_norm, gm_w_s, gm_b_s, mix_out_norm, w_out, ffn_pre_norm, ffn_post_norm, w_ffn_in, w_ffn_out, w_ple, ple_norm, w_ple_gate, loss_target, m_mix_pre_norm, m_mix_post_norm, m_w_in, m_b_forget, m_gm_v_norm, m_gm_w_s, m_gm_b_s, m_mix_out_norm, m_w_out, m_ffn_pre_norm, m_ffn_post_norm, m_w_ffn_in, m_w_ffn_out, m_w_ple, m_ple_norm, m_w_ple_gate, v_mix_pre_norm, v_mix_post_norm, v_w_in, v_b_forget, v_gm_v_norm, v_gm_w_s, v_gm_b_s, v_mix_out_norm, v_w_out, v_ffn_pre_norm, v_ffn_post_norm, v_w_ffn_in, v_w_ffn_out, v_w_ple, v_ple_norm, v_w_ple_gate):
    given = dict(locals())
    weights = {n: given[n] for n in WEIGHT_ORDER}
    mom_m = {n: given["m_" + n] for n in WEIGHT_ORDER}
    mom_v = {n: given["v_" + n] for n in WEIGHT_ORDER}
    depth = w_in.shape[0]
    s, d = x.shape[1], x.shape[2]
    tq = _tile(s, ATT_BLOCK)
    me = 4 * lax.axis_index("x") + 2 * lax.axis_index("y") + lax.axis_index("c")
    tril = jnp.tril(jnp.ones((CHUNK, CHUNK), F32))

    def landing(block):
        return lax.dynamic_update_index_in_dim(lax.empty((N_DEV,) + block.shape, block.dtype), block, me, 0)

    def mix_weights(i, got):
        w_in_full = jnp.concatenate([got["w_in"][j] for j in range(N_DEV)], axis=-1)
        pad = jnp.zeros((d, D_IN_PAD - D_IN), BF16)
        wt = gm_w_s[i] * tril[None]
        lw = dict(
            w_in=jnp.concatenate([w_in_full[:, :3 * D_ATT], w_in_full[:, 3 * D_ATT + N_HEADS:],
                                  w_in_full[:, 3 * D_ATT:3 * D_ATT + N_HEADS], pad], axis=-1),
            b_forget=b_forget[i][:, None], wt=wt.astype(BF16), wt_t=wt.transpose(0, 2, 1).astype(BF16),
            bs_t=gm_b_s[i].T)
        lw.update({n: weights[n][i][None] for n in ("mix_pre_norm", "mix_post_norm", "gm_v_norm", "mix_out_norm",
                                                    "ffn_pre_norm", "ffn_post_norm", "ple_norm")})
        return lw

    def rest_weights(got):
        return dict(w_out=got["w_out"].reshape(-1, d), w_ffn_in=got["w_ffn_in"],
                    w_ffn_out=got["w_ffn_out"].reshape(N_DEV // 2, -1, d),
                    w_ple=jnp.concatenate([got["w_ple"][j] for j in range(N_DEV)], axis=-1),
                    w_ple_gate=got["w_ple_gate"].reshape(-1, d))

    def shard_view(n, a):
        return jnp.transpose(a, (0, 2, 1)) if n in TRANSPOSED_WEIGHTS else a

    shards = {n: shard_view(n, weights[n].astype(BF16)) for n in MATRIX_WEIGHTS}

    def gather_start(i):
        started = {}
        order = jnp.zeros((), BF16)
        for tag, grp in EXCHANGE_GROUPS.items():
            started[tag] = _exchange_start([shards[n] for n in grp], [landing(shards[n][i] + order) for n in grp], i,
                                           False, f"weights_gather_start_{i}_{tag}")
            order = started[tag][4][0, 0].astype(BF16)
        return started

    def gather_finish(i, tag, pending, after):
        srcs, got = _exchange_wait(pending[tag], after, i, False, f"weights_gather_wait_{i}_{tag}")
        shards.update(zip(EXCHANGE_GROUPS[tag], srcs))
        return dict(zip(EXCHANGE_GROUPS[tag], got))

    h = x[0]
    saved, layer_w = [], []
    pending = gather_start(0)
    for i in range(depth):
        lw = mix_weights(i, gather_finish(i, "mix", pending, h))
        if i == 0:
            lw["mix_pre_norm"] = lw["mix_pre_norm"] + pending["rest"][4][:1, :1]
        following = {}

        def late(att, i=i, pending=pending, lw=lw, following=following):
            rest = rest_weights(gather_finish(i, "rest", pending, att))
            lw.update(rest)
            if i + 1 == depth:
                return rest
            following.update(gather_start(i + 1))
            token = following["mix"][4][:1, :1] + following["rest"][4][:1, :1]
            return dict(rest, mix_out_norm=lw["mix_out_norm"] + token)

        h, sv = _layer_fwd(h, p[i, 0], lw, tq, late)
        layer_w.append(lw)
        saved.append(sv)
        pending = following

    def loss_head(y, t):
        err = y - t
        return err * (1.0 / d), _colsum(err * err)

    dh, sq = _rowwise(loss_head, [h, loss_target[0]], [], [(d, F32)], [d], "loss_head", ROW_TILE)
    loss = lax.psum(0.5 * jnp.sum(sq) / d, AXES)

    layer_g = [None] * depth
    shard_out = {n: None for n in MATRIX_WEIGHTS}

    def scatter_start(i, tag, g):
        full_g = [g[n] for n in EXCHANGE_GROUPS[tag]]
        lands = [landing(lax.dynamic_index_in_dim(gf, me, 0, keepdims=False)) for gf in full_g]
        return _exchange_start(full_g, lands, None, True, f"grads_scatter_start_{i}_{tag}")

    def scatter_finish(i, tag, started, after):
        _, parts = _exchange_wait(started[tag], after, None, True, f"grads_scatter_wait_{i}_{tag}")
        for n, part in zip(EXCHANGE_GROUPS[tag], parts):
            shard_out[n] = _adamw_shard(shard_view(n, weights[n]), shard_view(n, mom_m[n]), shard_view(n, mom_v[n]),
                                        part, i, shard_out[n], "adamw_" + n)

    before = None
    for i in reversed(range(depth)):
        lw = layer_w[i]
        if before is not None:
            lw = dict(lw, ple_norm=lw["ple_norm"] + before[1]["mix"][4][:1, :1])
        started = {}

        def mid(g, dmc, i=i, before=before, started=started, lw=lw):
            if before is not None:
                scatter_finish(before[0], "rest", before[1], dmc)
            started["rest"] = scatter_start(i, "rest", g)
            return dict(mix_out_norm=lw["mix_out_norm"] + started["rest"][4][:1, :1])

        dh, layer_g[i] = _layer_bwd(dh, saved[i], lw, tq, mid)
        if before is not None:
            scatter_finish(before[0], "mix", before[1], dh)
        started["mix"] = scatter_start(i, "mix", layer_g[i])
        before = (i, started)
    scatter_finish(before[0], "rest", before[1], before[1]["mix"][4])
    scatter_finish(before[0], "mix", before[1], dh)
    grad_x = dh[None]

    grads, deltas, new_m, new_v = {}, {}, {}, {}
    for n in MATRIX_WEIGHTS:
        grads[n], deltas[n], new_m[n], new_v[n] = (shard_view(n, a) for a in shard_out[n])

    small_g = _pack_small([jnp.stack([layer_g[i][n].reshape(-1) for i in range(depth)]) for n in SMALL_WEIGHTS])
    (gathered,) = _all_gather([small_g.astype(BF16)], None, "small_grads_all_gather")
    g_small = _sum_devices(gathered)
    pack = lambda t: _pack_small([t[n] for n in SMALL_WEIGHTS])
    dl, nm, nv = _rowwise(_adamw_math, [pack(weights), g_small, pack(mom_m), pack(mom_v)], [],
                          [(SMALL_COLS, F32)] * 3, [], "adamw_small", SMALL_ROWS)
    off = 0
    for n in SMALL_WEIGHTS:
        shp, size = weights[n].shape, weights[n].size
        grads[n], deltas[n], new_m[n], new_v[n] = (a.reshape(-1)[off:off + size].reshape(shp)
                                                   for a in (g_small, dl, nm, nv))
        off += size

    return (loss, grad_x, *[grads[n] for n in WEIGHT_ORDER], *[deltas[n] for n in WEIGHT_ORDER],
            *[new_m[n] for n in WEIGHT_ORDER], *[new_v[n] for n in WEIGHT_ORDER])
```

```python
import functools
import math

import jax
import jax.numpy as jnp
from jax import lax
from jax.experimental import pallas as pl
from jax.experimental.pallas import tpu as pltpu

F32 = jnp.float32
BF16 = jnp.bfloat16
MESH = pl.DeviceIdType.MESH
AXES = ("x", "y", "c")
N_DEV = 8

EPS = 1e-6
NEG_INF = -1e30
N_HEADS = 8
HEAD_DIM = 64
D_ATT = N_HEADS * HEAD_DIM
N_GROUPS = 8
GROUP_DIM = 64
D_GM = N_GROUPS * GROUP_DIM
CHUNK = 128
ATT_SCALE = HEAD_DIM ** -0.5
ATT_BLOCK = 1024
D_IN = 3 * D_ATT + N_HEADS + 2 * D_GM
D_IN_PAD = 3 * D_ATT + 2 * D_GM + 128
F_OFF = 3 * D_ATT + 2 * D_GM

ADAM_LR = 0.001
ADAM_B1 = 0.9
ADAM_B2 = 0.999
ADAM_EPS = 1e-08
ADAM_WD = 0.01
ADAM_STEP = 10

LANE = 128
VMEM_LIMIT = 48 * 1024 * 1024
ROW_TILE = 512
K_TILE = 4096
SMALL_COLS = 128
SMALL_ROWS = 512

MATRIX_WEIGHTS = ("w_in", "w_out", "w_ffn_in", "w_ffn_out", "w_ple", "w_ple_gate")
EXCHANGE_GROUPS = {"mix": ("w_in",), "rest": ("w_out", "w_ffn_in", "w_ffn_out", "w_ple", "w_ple_gate")}
TRANSPOSED_WEIGHTS = ("w_ffn_in",)
SMALL_WEIGHTS = ("mix_pre_norm", "mix_post_norm", "b_forget", "gm_v_norm", "gm_w_s", "gm_b_s",
                 "mix_out_norm", "ffn_pre_norm", "ffn_post_norm", "ple_norm")
WEIGHT_ORDER = ("mix_pre_norm", "mix_post_norm", "w_in", "b_forget", "gm_v_norm", "gm_w_s", "gm_b_s",
                "mix_out_norm", "w_out", "ffn_pre_norm", "ffn_post_norm", "w_ffn_in", "w_ffn_out",
                "w_ple", "ple_norm", "w_ple_gate")


def _tile(n, pref, unit=LANE):
    best = None
    t = unit
    while t <= min(n, pref):
        if n % t == 0:
            best = t
        t += unit
    return n if best is None else best


def _cparams(*semantics):
    return pltpu.CompilerParams(dimension_semantics=semantics or None, vmem_limit_bytes=VMEM_LIMIT)


NN = (((1,), (0,)), ((), ()))
NT = (((1,), (1,)), ((), ()))
TN = (((0,), (0,)), ((), ()))
_MM_AXES = {
    "nn": ("i", "k", "k", "j"), "nt": ("i", "k", "j", "k"), "tn": ("k", "i", "k", "j")}
_MM_DN = {"nn": NN, "nt": NT, "tn": TN}


def _mm(a, b, dims, out_dtype, name, a3=None, b3=None, o3=None, tm=1024, tn=1024, tk=None):
    ar, ac, br, bc = _MM_AXES[dims]
    letter = {"i": "m", "j": "n", "k": "k"}
    size = {}

    def measure(x, rows, cols, stacked):
        shape = x.shape
        if stacked is None:
            size.setdefault(letter[rows], shape[0])
            size.setdefault(letter[cols], shape[1])
        else:
            for ax, n in ((rows, shape[1]), (cols, shape[2])):
                size.setdefault(letter[ax], n * shape[0] if letter[ax] == stacked else n)

    measure(a, ar, ac, a3)
    measure(b, br, bc, b3)
    m, n, k = size["m"], size["n"], size["k"]
    slab = {}
    for x, stacked, rows, cols in ((a, a3, ar, ac), (b, b3, br, bc)):
        if stacked is not None:
            slab[stacked] = x.shape[1] if letter[rows] == stacked else x.shape[2]
    if o3 is not None:
        slab.setdefault(o3, slab.get(o3, None) or {"m": m, "n": n}[o3] // N_DEV)
    tk = tk or K_TILE
    tile = {"m": slab.get("m") or _tile(m, tm), "n": slab.get("n") or _tile(n, tn), "k": slab.get("k") or _tile(k, tk)}
    group = 1
    if a3 == "k" and b3 == "k":
        group = max(g for g in range(1, a.shape[0] + 1) if a.shape[0] % g == 0 and g * tile["k"] <= max(tk, tile["k"]))
    nk = k // (group * tile["k"])

    def spec(rows, cols, stacked):
        tr, tc = tile[letter[rows]], tile[letter[cols]]
        if stacked is None:
            return pl.BlockSpec((tr, tc), lambda i, j, kk: ({"i": i, "j": j, "k": kk}[rows], {"i": i, "j": j, "k": kk}[cols]))

        def imap(i, j, kk):
            g = {"i": i, "j": j, "k": kk}
            return (g[{"m": "i", "n": "j", "k": "k"}[stacked]],
                    0 if letter[rows] == stacked else g[rows], 0 if letter[cols] == stacked else g[cols])

        return pl.BlockSpec((group if stacked == "k" else 1, tr, tc), imap)

    dn = _MM_DN[dims]

    def body(a_ref, b_ref, o_ref, *acc):
        prod = None
        for g in range(group):
            av = a_ref[...] if a3 is None else a_ref[g]
            bv = b_ref[...] if b3 is None else b_ref[g]
            term = lax.dot_general(av.astype(BF16), bv.astype(BF16), dn, preferred_element_type=F32)
            prod = term if prod is None else prod + term

        def emit(val):
            if o3 is None:
                o_ref[...] = val.astype(out_dtype)
            else:
                o_ref[0] = val.astype(out_dtype)

        if nk == 1:
            emit(prod)
            return
        (acc_ref,) = acc
        kk = pl.program_id(2)

        @pl.when(kk == 0)
        def _():
            acc_ref[...] = prod

        @pl.when(kk > 0)
        def _():
            acc_ref[...] += prod

        @pl.when(kk == nk - 1)
        def _():
            emit(acc_ref[...])

    if o3 is None:
        out_shape = (m, n)
    elif o3 == "m":
        out_shape = (m // tile["m"], tile["m"], n)
    else:
        out_shape = (n // tile["n"], m, tile["n"])
    return pl.pallas_call(
        body, name=name, out_shape=jax.ShapeDtypeStruct(out_shape, out_dtype),
        grid=(m // tile["m"], n // tile["n"], nk),
        in_specs=[spec(ar, ac, a3), spec(br, bc, b3)], out_specs=spec("i", "j", o3),
        scratch_shapes=[] if nk == 1 else [pltpu.VMEM((tile["m"], tile["n"]), F32)],
        compiler_params=_cparams("parallel", "parallel", "arbitrary"),
    )(a, b)


def _rowwise(fn, rows, vecs, outs, reds, name, ts):
    rows = [r if isinstance(r, tuple) else (r, 0, r.shape[1]) for r in rows]
    s = rows[0][0].shape[0]
    ts = _tile(s, ts, 8)
    nr, nv, no = len(rows), len(vecs), len(outs)

    def body(*refs):
        vals = fn(*[r[...] for r in refs[:nr + nv]])
        vals = vals if isinstance(vals, tuple) else (vals,)
        o_refs = refs[nr + nv:nr + nv + no]
        r_refs = refs[nr + nv + no:]
        for o_ref, val in zip(o_refs, vals[:no]):
            o_ref[...] = val.astype(o_ref.dtype)
        if r_refs:
            @pl.when(pl.program_id(0) == 0)
            def _():
                for r_ref in r_refs:
                    r_ref[...] = jnp.zeros_like(r_ref)

            for r_ref, val in zip(r_refs, vals[no:]):
                r_ref[...] += val

    in_specs = [pl.BlockSpec((ts, w), functools.partial(lambda i, cb: (i, cb), cb=cb)) for _, cb, w in rows]
    in_specs += [pl.BlockSpec(v.shape, lambda i: (0, 0)) for v in vecs]
    out_specs = [pl.BlockSpec((ts, c), lambda i: (i, 0)) for c, _ in outs]
    out_specs += [pl.BlockSpec((1, c), lambda i: (0, 0)) for c in reds]
    out_shape = [jax.ShapeDtypeStruct((s, c), dt) for c, dt in outs]
    out_shape += [jax.ShapeDtypeStruct((1, c), F32) for c in reds]
    return pl.pallas_call(
        body, name=name, out_shape=out_shape, grid=(s // ts,), in_specs=in_specs, out_specs=out_specs,
        compiler_params=_cparams("arbitrary" if reds else "parallel"),
    )(*[r[0] for r in rows], *vecs)


def _rms(x):
    r = lax.rsqrt(jnp.mean(x * x, axis=-1, keepdims=True) + EPS)
    return x * r, r


def _rms_bwd(xhat, r, dyg):
    return r * (dyg - xhat * jnp.mean(dyg * xhat, axis=-1, keepdims=True))


def _colsum(x):
    return jnp.sum(x, axis=0, keepdims=True)


def _sigmoid(x):
    return 1.0 / (1.0 + jnp.exp(-x))


GELU_C = math.sqrt(2.0 / math.pi)
GELU_A = 0.044715


def _gelu(x):
    return 0.5 * x * (1.0 + jnp.tanh(GELU_C * (x + GELU_A * x * x * x)))


def _gelu_grad(x):
    t = jnp.tanh(GELU_C * (x + GELU_A * x * x * x))
    return 0.5 * (1.0 + t) + 0.5 * x * (1.0 - t * t) * GELU_C * (1.0 + 3.0 * GELU_A * x * x)


def _ffn_in_swiglu(hn, wg):
    s, d = hn.shape
    g2, n, _ = wg.shape
    g = g2 // 2
    tm = _tile(s, 1024)

    def body(h_ref, wa_ref, wb_ref, ab_ref, t_ref):
        hv = h_ref[...]
        a = lax.dot_general(hv, wa_ref[0], NT, preferred_element_type=F32)
        b = lax.dot_general(hv, wb_ref[0], NT, preferred_element_type=F32)
        ab_ref[0, 0] = a.astype(BF16)
        ab_ref[1, 0] = b.astype(BF16)
        t_ref[0] = (a * _sigmoid(a) * b).astype(BF16)

    return pl.pallas_call(
        body, name="mm_ffn_in_swiglu",
        out_shape=(jax.ShapeDtypeStruct((2, g, s, n), BF16), jax.ShapeDtypeStruct((g, s, n), BF16)),
        grid=(s // tm, g),
        in_specs=[pl.BlockSpec((tm, d), lambda i, j: (i, 0)), pl.BlockSpec((1, n, d), lambda i, j: (j, 0, 0)),
                  pl.BlockSpec((1, n, d), lambda i, j: (j + g, 0, 0))],
        out_specs=(pl.BlockSpec((2, 1, tm, n), lambda i, j: (0, j, i, 0)),
                   pl.BlockSpec((1, tm, n), lambda i, j: (j, i, 0))),
        compiler_params=_cparams("parallel", "parallel"),
    )(hn, wg, wg)


def _swiglu_bwd(ab, dt, ts):
    _, g, s, n = ab.shape
    ts = _tile(s, ts, 8)

    def body(ab_ref, dt_ref, t_ref, dab_ref):
        a = ab_ref[0, 0].astype(F32)
        b = ab_ref[1, 0].astype(F32)
        dt = dt_ref[0].astype(F32)
        sig = _sigmoid(a)
        silu = a * sig
        t_ref[0] = (silu * b).astype(BF16)
        dab_ref[0, 0] = (dt * b * (sig * (1.0 + a * (1.0 - sig)))).astype(BF16)
        dab_ref[1, 0] = (dt * silu).astype(BF16)

    both = pl.BlockSpec((2, 1, ts, n), lambda j, i: (0, j, i, 0))
    one = pl.BlockSpec((1, ts, n), lambda j, i: (j, i, 0))
    return pl.pallas_call(
        body, name="swiglu_bwd",
        out_shape=(jax.ShapeDtypeStruct((g, s, n), BF16), jax.ShapeDtypeStruct((2, g, s, n), BF16)),
        grid=(g, s // ts), in_specs=[both, one], out_specs=(one, both),
        compiler_params=_cparams("parallel", "parallel"),
    )(ab, dt)


def _forget_fwd(fl_t, b_col):
    h, s = fl_t.shape
    nb = s // LANE

    def body(fl_ref, b_ref, c_ref):
        upper = (lax.broadcasted_iota(jnp.int32, (LANE, LANE), 0)
                 <= lax.broadcasted_iota(jnp.int32, (LANE, LANE), 1)).astype(F32)

        def step(i, carry):
            x = fl_ref[i] + b_ref[...]
            lf = jnp.minimum(x, 0.0) - jnp.log(1.0 + jnp.exp(-jnp.abs(x)))
            cs = jnp.dot(lf, upper, precision=lax.Precision.HIGHEST, preferred_element_type=F32) + carry
            c_ref[i] = cs
            return cs[:, LANE - 1:LANE]

        lax.fori_loop(0, nb, step, jnp.zeros((h, 1), F32))

    out = pl.pallas_call(
        body, name="forget_fwd", out_shape=jax.ShapeDtypeStruct((nb, h, LANE), F32),
        compiler_params=_cparams(),
    )(fl_t.reshape(h, nb, LANE).transpose(1, 0, 2), b_col)
    return out.transpose(1, 0, 2).reshape(h, s)


def _forget_bwd(dc_t, fl_t, b_col):
    h, s = fl_t.shape
    nb = s // LANE

    def body(dc_ref, fl_ref, b_ref, dfl_ref, db_ref):
        lower = (lax.broadcasted_iota(jnp.int32, (LANE, LANE), 0)
                 >= lax.broadcasted_iota(jnp.int32, (LANE, LANE), 1)).astype(F32)

        def step(t, carry):
            tail, db = carry
            i = nb - 1 - t
            rc = jnp.dot(dc_ref[i], lower, precision=lax.Precision.HIGHEST, preferred_element_type=F32) + tail
            dfl = rc * (1.0 - _sigmoid(fl_ref[i] + b_ref[...]))
            dfl_ref[i] = dfl
            return rc[:, 0:1], db + jnp.sum(dfl, axis=1, keepdims=True)

        _, db = lax.fori_loop(0, nb, step, (jnp.zeros((h, 1), F32), jnp.zeros((h, 1), F32)))
        db_ref[...] = db

    blocked = lambda a: a.reshape(h, nb, LANE).transpose(1, 0, 2)
    dfl, db = pl.pallas_call(
        body, name="forget_bwd",
        out_shape=(jax.ShapeDtypeStruct((nb, h, LANE), F32), jax.ShapeDtypeStruct((h, 1), F32)),
        compiler_params=_cparams(),
    )(blocked(dc_t), blocked(fl_t), b_col)
    return dfl.transpose(1, 0, 2).reshape(h, s), db


N_PAIRS = N_HEADS // 2


def _causal_mask(t):
    return lax.broadcasted_iota(jnp.int32, (t, t), 0) >= lax.broadcasted_iota(jnp.int32, (t, t), 1)


def _head_lanes():
    return lax.broadcasted_iota(jnp.int32, (1, 2 * HEAD_DIM), 1) < HEAD_DIM


def _pick(x2, first, hh):
    zero = jnp.zeros_like(x2)
    return jnp.where(first, x2, zero) if hh == 0 else jnp.where(first, zero, x2)


BIAS_TERMS = 3


def _attn_prep(z, c, ts):
    s = z.shape[0]
    ts = _tile(s, ts, 16)
    w = 2 * HEAD_DIM

    def body(q_ref, k_ref, v_ref, c_ref, qa_ref, ka_ref, vb_ref):
        lane = lax.broadcasted_iota(jnp.int32, (1, w), 1)
        first = lane < HEAD_DIM
        cv = c_ref[...]
        for h in range(N_HEADS):
            pair = slice((h // 2) * w, (h // 2 + 1) * w)
            qh = q_ref[:, pair] * ATT_SCALE
            kh = k_ref[:, pair]
            if h % 2:
                qh = pltpu.roll(qh, HEAD_DIM, 1)
                kh = pltpu.roll(kh, HEAD_DIM, 1)
            rest = cv[:, h:h + 1]
            q_tail = jnp.zeros((1, w), F32)
            k_tail = jnp.zeros((1, w), F32)
            for t in range(BIAS_TERMS):
                term = rest.astype(BF16).astype(F32)
                rest = rest - term
                q_tail = jnp.where(lane == HEAD_DIM + t, term, jnp.where(lane == HEAD_DIM + BIAS_TERMS + t, 1.0, q_tail))
                k_tail = jnp.where(lane == HEAD_DIM + t, 1.0, jnp.where(lane == HEAD_DIM + BIAS_TERMS + t, -term, k_tail))
            qa_ref[:, h * w:(h + 1) * w] = jnp.where(first, qh, q_tail).astype(BF16)
            ka_ref[:, h * w:(h + 1) * w] = jnp.where(first, kh, k_tail).astype(BF16)
        vb_ref[...] = v_ref[...].astype(BF16)

    col = lambda cb: pl.BlockSpec((ts, D_ATT), lambda i: (i, cb))
    wide = pl.BlockSpec((ts, N_HEADS * w), lambda i: (i, 0))
    return pl.pallas_call(
        body, name="attn_prep",
        out_shape=(jax.ShapeDtypeStruct((s, N_HEADS * w), BF16), jax.ShapeDtypeStruct((s, N_HEADS * w), BF16),
                   jax.ShapeDtypeStruct((s, D_ATT), BF16)),
        grid=(s // ts,), in_specs=[col(0), col(1), col(2), pl.BlockSpec((ts, N_HEADS), lambda i: (i, 0))],
        out_specs=(wide, wide, col(0)),
        compiler_params=_cparams("parallel"),
    )(z, z, z, c)


def _attn_fwd(qa, ka, vb, tq):
    s = qa.shape[0]
    nq = s // tq
    w = 2 * HEAD_DIM

    def body(q_ref, k_ref, v_ref, o_ref, lse_ref):
        i = pl.program_id(1)
        first = _head_lanes()
        q2 = q_ref[...]

        def block(j, carry, masked):
            off = pl.multiple_of(j * tq, tq)
            k2 = k_ref[pl.ds(off, tq), :]
            v2 = v_ref[pl.ds(off, tq), :]
            new = []
            for hh in range(2):
                m, l, acc = carry[hh]
                sc = lax.dot_general(q2[:, hh * w:(hh + 1) * w], k2[:, hh * w:(hh + 1) * w], NT,
                                     preferred_element_type=F32)
                if masked:
                    sc = jnp.where(_causal_mask(tq), sc, NEG_INF)
                m_new = jnp.maximum(m, jnp.max(sc, axis=-1, keepdims=True))
                alpha = jnp.exp(m - m_new)
                p = jnp.exp(sc - m_new)
                l = alpha * l + jnp.sum(p, axis=-1, keepdims=True)
                p_hi = p.astype(BF16)
                p_lo = (p - p_hi.astype(F32)).astype(BF16)
                acc = (alpha * acc + jnp.dot(p_hi, v2, preferred_element_type=F32)
                       + jnp.dot(p_lo, v2, preferred_element_type=F32))
                new.append((m_new, l, acc))
            return tuple(new)

        one = (jnp.full((tq, 1), NEG_INF, F32), jnp.zeros((tq, 1), F32), jnp.zeros((tq, w), F32))
        carry = lax.fori_loop(0, i, lambda j, c: block(j, c, False), (one, one))
        (m0, l0, a0), (m1, l1, a1) = block(i, carry, True)
        o_ref[...] = jnp.where(first, a0 / l0, a1 / l1)
        lse_ref[0] = m0 + jnp.log(l0)
        lse_ref[1] = m1 + jnp.log(l1)

    return pl.pallas_call(
        body, name="attn_fwd",
        out_shape=(jax.ShapeDtypeStruct((s, D_ATT), F32), jax.ShapeDtypeStruct((N_HEADS, s, 1), F32)),
        grid=(N_PAIRS, nq),
        in_specs=[pl.BlockSpec((tq, 2 * w), lambda hp, i: (i, hp)),
                  pl.BlockSpec((s, 2 * w), lambda hp, i: (0, hp)),
                  pl.BlockSpec((s, w), lambda hp, i: (0, hp))],
        out_specs=(pl.BlockSpec((tq, w), lambda hp, i: (i, hp)),
                   pl.BlockSpec((2, tq, 1), lambda hp, i: (hp, i, 0))),
        compiler_params=_cparams("parallel", "parallel"),
    )(qa, ka, vb)


def _attn_delta(o, do, tq):
    s = o.shape[0]
    w = 2 * HEAD_DIM

    def body(o_ref, do_ref, d_ref):
        first = _head_lanes()
        prod = o_ref[...] * do_ref[...].astype(F32)
        d_ref[0] = jnp.sum(_pick(prod, first, 0), axis=-1, keepdims=True)
        d_ref[1] = jnp.sum(_pick(prod, first, 1), axis=-1, keepdims=True)

    blk = pl.BlockSpec((tq, w), lambda hp, i: (i, hp))
    return pl.pallas_call(
        body, name="attn_delta", out_shape=jax.ShapeDtypeStruct((N_HEADS, s, 1), F32), grid=(N_PAIRS, s // tq),
        in_specs=[blk, blk], out_specs=pl.BlockSpec((2, tq, 1), lambda hp, i: (hp, i, 0)),
        compiler_params=_cparams("parallel", "parallel"),
    )(o, do)


def _attn_bwd(qa, ka, vb, do, lse, delta, tq):
    s = qa.shape[0]
    nq = s // tq
    w = 2 * HEAD_DIM

    def body(q_ref, do_ref, lse_ref, dl_ref, k_ref, v_ref, dq_ref, dk_ref, dv_ref, dc_ref, dq_acc):
        j = pl.program_id(1)
        first = _head_lanes()

        @pl.when(j == 0)
        def _():
            dq_acc[...] = jnp.zeros_like(dq_acc)

        k2 = k_ref[...]
        v2 = v_ref[...]

        def step(i, carry, masked):
            off = pl.multiple_of(i * tq, tq)
            rows = pl.ds(off, tq)
            q2 = q_ref[rows, :]
            do2 = do_ref[rows, :]
            new, dqs = [], []
            for hh in range(2):
                dk, dv, dcs = carry[hh]
                qh = q2[:, hh * w:(hh + 1) * w]
                kh = k2[:, hh * w:(hh + 1) * w]
                sc = lax.dot_general(qh, kh, NT, preferred_element_type=F32)
                if masked:
                    sc = jnp.where(_causal_mask(tq), sc, NEG_INF)
                p = jnp.exp(sc - lse_ref[hh, rows, :])
                dv = dv + lax.dot_general(do2, p.astype(BF16), TN, preferred_element_type=F32)
                dp = lax.dot_general(_pick(do2, first, hh), v2, NT, preferred_element_type=F32)
                ds = p * (dp - dl_ref[hh, rows, :])
                dsb = ds.astype(BF16)
                dk = dk + lax.dot_general(qh, dsb, TN, preferred_element_type=F32)
                dqs.append(jnp.dot(dsb, kh, preferred_element_type=F32))
                new.append((dk, dv, dcs + jnp.sum(ds, axis=0, keepdims=True)))
            dq_acc[rows, :] += jnp.where(first, dqs[0], pltpu.roll(dqs[1], HEAD_DIM, 1)) * ATT_SCALE
            return tuple(new)

        one = (jnp.zeros((w, tq), F32), jnp.zeros((w, tq), F32), jnp.zeros((1, tq), F32))
        carry = step(j, (one, one), True)
        (dk0, dv0, dc0), (dk1, dv1, dc1) = lax.fori_loop(j + 1, nq, lambda i, c: step(i, c, False), carry)
        dk_ref[...] = jnp.where(first, dk0.T, pltpu.roll(dk1.T, HEAD_DIM, 1)).astype(BF16)
        dv_ref[...] = jnp.where(first, dv0.T, dv1.T).astype(BF16)
        dc_ref[0, 0] = -dc0
        dc_ref[1, 0] = -dc1

        @pl.when(j == nq - 1)
        def _():
            dq_ref[...] = dq_acc[...].astype(BF16)

    whole = lambda width: pl.BlockSpec((s, width), lambda hp, j: (0, hp))
    whole_heads = pl.BlockSpec((2, s, 1), lambda hp, j: (hp, 0, 0))
    blk = lambda width: pl.BlockSpec((tq, width), lambda hp, j: (j, hp))
    crow = pl.BlockSpec((2, 1, 1, tq), lambda hp, j: (hp, j, 0, 0))
    return pl.pallas_call(
        body, name="attn_bwd",
        out_shape=(jax.ShapeDtypeStruct((s, D_ATT), BF16), jax.ShapeDtypeStruct((s, D_ATT), BF16),
                   jax.ShapeDtypeStruct((s, D_ATT), BF16), jax.ShapeDtypeStruct((N_HEADS, nq, 1, tq), F32)),
        grid=(N_PAIRS, nq),
        in_specs=[whole(2 * w), whole(w), whole_heads, whole_heads, blk(2 * w), blk(w)],
        out_specs=(whole(w), blk(w), blk(w), crow),
        scratch_shapes=[pltpu.VMEM((s, w), F32)],
        compiler_params=_cparams("parallel", "arbitrary"),
    )(qa, do, lse, delta, ka, vb)


def _pair_sums(x, first):
    total = jnp.sum(x, axis=-1, keepdims=True)
    head = jnp.sum(jnp.where(first, x, 0.0), axis=-1, keepdims=True)
    return head, total - head


def _pair_mean(x, first):
    head, tail = _pair_sums(x, first)
    return jnp.where(first, head, tail) * (1.0 / GROUP_DIM)


def _gm_pair_norm(v2, first):
    d = v2 - _pair_mean(v2, first)
    rstd = lax.rsqrt(_pair_mean(d * d, first) + EPS)
    return d * rstd, rstd


def _gm_pair_mix(w_ref, pr, rhs, first):
    return jnp.where(first, jnp.dot(w_ref[2 * pr], rhs, preferred_element_type=F32),
                     jnp.dot(w_ref[2 * pr + 1], rhs, preferred_element_type=F32))


def _gmlp_fwd(z, wt, bs_t, vgain):
    s = z.shape[0]

    def body(gu_ref, gv_ref, wt_ref, bs_ref, vg_ref, o_ref):
        first = _head_lanes()
        for pr in range(N_GROUPS // 2):
            sl = slice(2 * pr * GROUP_DIM, 2 * (pr + 1) * GROUP_DIM)
            vhat, _ = _gm_pair_norm(_gelu(gv_ref[:, sl]), first)
            vn = (vhat * vg_ref[:, sl]).astype(BF16)
            bias = jnp.where(first, bs_ref[:, 2 * pr:2 * pr + 1], bs_ref[:, 2 * pr + 1:2 * pr + 2])
            o_ref[:, sl] = _gelu(gu_ref[:, sl]) * (_gm_pair_mix(wt_ref, pr, vn, first) + bias)

    full = lambda a: pl.BlockSpec(a.shape, lambda n: (0,) * a.ndim)
    return pl.pallas_call(
        body, name="gmlp_fwd", out_shape=jax.ShapeDtypeStruct((s, D_GM), F32), grid=(s // CHUNK,),
        in_specs=[pl.BlockSpec((CHUNK, D_GM), lambda n: (n, 3)), pl.BlockSpec((CHUNK, D_GM), lambda n: (n, 4)),
                  full(wt), full(bs_t), full(vgain)],
        out_specs=pl.BlockSpec((CHUNK, D_GM), lambda n: (n, 0)),
        compiler_params=_cparams("parallel"),
    )(z, z, wt, bs_t, vgain)


def _gmlp_bwd(z, dgm, wt, wt_t, bs_t, vgain):
    s = z.shape[0]

    def body(gu_ref, gv_ref, dgm_ref, wt_ref, wtt_ref, bs_ref, vg_ref, dgu_ref, dgv_ref, dwt_ref, dbs_ref, dvg_ref):
        @pl.when(pl.program_id(0) == 0)
        def _():
            dwt_ref[...] = jnp.zeros_like(dwt_ref)
            dbs_ref[...] = jnp.zeros_like(dbs_ref)
            dvg_ref[...] = jnp.zeros_like(dvg_ref)

        first = _head_lanes()
        for pr in range(N_GROUPS // 2):
            g0, g1 = 2 * pr, 2 * pr + 1
            sl = slice(g0 * GROUP_DIM, (g1 + 1) * GROUP_DIM)
            gu = gu_ref[:, sl]
            gv = gv_ref[:, sl]
            dgm = dgm_ref[:, sl]
            vhat, rstd = _gm_pair_norm(_gelu(gv), first)
            gain = vg_ref[:, sl]
            vn = (vhat * gain).astype(BF16)
            bias = jnp.where(first, bs_ref[:, g0:g0 + 1], bs_ref[:, g1:g1 + 1])
            mixed = _gm_pair_mix(wt_ref, pr, vn, first) + bias
            dgu_ref[:, sl] = (dgm * mixed * _gelu_grad(gu)).astype(BF16)
            dmixed = dgm * _gelu(gu)
            db0, db1 = _pair_sums(dmixed, first)
            dbs_ref[:, g0:g0 + 1] += db0
            dbs_ref[:, g1:g1 + 1] += db1
            dwt_ref[g0] += lax.dot_general(_pick(dmixed, first, 0).astype(BF16), vn, NT, preferred_element_type=F32)
            dwt_ref[g1] += lax.dot_general(_pick(dmixed, first, 1).astype(BF16), vn, NT, preferred_element_type=F32)
            dvn = _gm_pair_mix(wtt_ref, pr, dmixed.astype(BF16), first)
            dvg_ref[:, sl] += _colsum(dvn * vhat)
            dvhat = dvn * gain
            dvf = rstd * (dvhat - _pair_mean(dvhat, first) - vhat * _pair_mean(dvhat * vhat, first))
            dgv_ref[:, sl] = (dvf * _gelu_grad(gv)).astype(BF16)

    full = lambda a: pl.BlockSpec(a.shape, lambda n: (0,) * a.ndim)
    chunk = pl.BlockSpec((CHUNK, D_GM), lambda n: (n, 0))
    return pl.pallas_call(
        body, name="gmlp_bwd",
        out_shape=(jax.ShapeDtypeStruct((s, D_GM), BF16), jax.ShapeDtypeStruct((s, D_GM), BF16),
                   jax.ShapeDtypeStruct(wt.shape, F32), jax.ShapeDtypeStruct(bs_t.shape, F32),
                   jax.ShapeDtypeStruct(vgain.shape, F32)),
        grid=(s // CHUNK,),
        in_specs=[pl.BlockSpec((CHUNK, D_GM), lambda n: (n, 3)), pl.BlockSpec((CHUNK, D_GM), lambda n: (n, 4)),
                  chunk, full(wt), full(wt_t), full(bs_t), full(vgain)],
        out_specs=(chunk, chunk, full(wt), full(bs_t), full(vgain)),
        compiler_params=_cparams("arbitrary"),
    )(z, z, dgm, wt, wt_t, bs_t, vgain)


def _dz_concat(wide, dfl, ts):
    s = dfl.shape[0]
    ts = _tile(s, ts, 16)
    n = len(wide)

    def body(*refs):
        o_ref = refs[-1]
        for k in range(n):
            o_ref[:, k * D_ATT:(k + 1) * D_ATT] = refs[k][...]
        o_ref[:, F_OFF:] = jnp.zeros((ts, D_IN_PAD - F_OFF), BF16)
        o_ref[:, F_OFF:F_OFF + N_HEADS] = refs[n][...].astype(BF16)

    return pl.pallas_call(
        body, name="dz_concat", out_shape=jax.ShapeDtypeStruct((s, D_IN_PAD), BF16), grid=(s // ts,),
        in_specs=[pl.BlockSpec((ts, D_ATT), lambda i: (i, 0))] * n + [pl.BlockSpec((ts, N_HEADS), lambda i: (i, 0))],
        out_specs=pl.BlockSpec((ts, D_IN_PAD), lambda i: (i, 0)),
        compiler_params=_cparams("parallel"),
    )(*wide, dfl)


def _layer_fwd(h0, p_i, w, tq, late):
    s, d = h0.shape
    nq = s // tq
    sv = {"h0": h0}

    (hn1,) = _rowwise(lambda h, g: _rms(h)[0] * g, [h0], [w["mix_pre_norm"]], [(d, BF16)], [], "pre_mix", ROW_TILE)
    z = _mm(hn1, w["w_in"], "nn", F32, "mm_in")
    fl_t = z[:, F_OFF:F_OFF + N_HEADS].T
    c_t = _forget_fwd(fl_t, w["b_forget"])
    qa, ka, vb = _attn_prep(z, c_t.T, ROW_TILE)
    att, lse = _attn_fwd(qa, ka, vb, tq)
    gm = _gmlp_fwd(z, w["wt"], w["bs_t"], w["gm_v_norm"])
    w = dict(w, **late(att))

    def mix_out(att, gm, g):
        return jnp.concatenate([_rms(att)[0] * g[:, :D_ATT], _rms(gm)[0] * g[:, D_ATT:]], axis=-1)

    (mc,) = _rowwise(mix_out, [att, gm], [w["mix_out_norm"]], [(D_ATT + D_GM, BF16)], [], "mix_out", ROW_TILE)
    y1 = _mm(mc, w["w_out"], "nn", F32, "mm_out")

    def post_mix(h0, y1, gpost, gpre):
        h1 = h0 + _rms(y1)[0] * gpost
        return h1, _rms(h1)[0] * gpre

    h1, hn2 = _rowwise(post_mix, [h0, y1], [w["mix_post_norm"], w["ffn_pre_norm"]],
                       [(d, F32), (d, BF16)], [], "post_mix", ROW_TILE)
    ab, t = _ffn_in_swiglu(hn2, w["w_ffn_in"])
    y2 = _mm(t, w["w_ffn_out"], "nn", F32, "mm_ffn_out", a3="k", b3="k")

    def post_ffn(h1, y2, g):
        h2 = h1 + _rms(y2)[0] * g
        return h2, _rms(h2)[0]

    h2, hr = _rowwise(post_ffn, [h1, y2], [w["ffn_post_norm"]], [(d, F32), (d, BF16)], [], "post_ffn", ROW_TILE)
    gl = _mm(hr, w["w_ple_gate"], "nn", F32, "mm_gate")
    pe = _mm(p_i, w["w_ple"], "nn", F32, "mm_ple")
    (h3,) = _rowwise(lambda h2, gl, pe, g: h2 + _sigmoid(gl) * (_rms(pe)[0] * g), [h2, gl, pe], [w["ple_norm"]],
                     [(d, F32)], [], "ple_out", ROW_TILE)
    sv.update(hn1=hn1, z=z, fl_t=fl_t, qa=qa, ka=ka, vb=vb, lse=lse, att=att, gm=gm,
              mc=mc, y1=y1, h1=h1, hn2=hn2, ab=ab, y2=y2, h2=h2, hr=hr, gl=gl, pe=pe, p_i=p_i)
    return h3, sv


def _layer_bwd(dh3, sv, w, tq, mid):
    s, d = dh3.shape
    g = {}
    by_rows = lambda a: a.reshape(N_DEV, -1, a.shape[-1])
    by_cols = lambda a: jnp.stack(jnp.split(a, N_DEV, axis=-1))

    def ple_bwd(dh3, gl, pe, gple):
        gate = _sigmoid(gl)
        pehat, rpe = _rms(pe)
        dgl = dh3 * (pehat * gple) * gate * (1.0 - gate)
        de = dh3 * gate
        return dgl, _rms_bwd(pehat, rpe, de * gple), _colsum(de * pehat)

    dgl, dpe, g["ple_norm"] = _rowwise(ple_bwd, [dh3, sv["gl"], sv["pe"]], [w["ple_norm"]],
                                       [(d, BF16), (d, BF16)], [d], "ple_bwd", ROW_TILE)
    g["w_ple_gate"] = by_rows(_mm(sv["hr"], dgl, "tn", BF16, "mm_dgate"))
    dhr = _mm(dgl, w["w_ple_gate"], "nt", F32, "mm_dhr")
    g["w_ple"] = by_cols(_mm(sv["p_i"], dpe, "tn", BF16, "mm_dple"))

    def ffn_post_bwd(dh3, dhr, h2, y2, gpost):
        h2hat, r2 = _rms(h2)
        dh2 = dh3 + _rms_bwd(h2hat, r2, dhr)
        y2hat, ry = _rms(y2)
        return dh2, _rms_bwd(y2hat, ry, dh2 * gpost), _colsum(dh2 * y2hat)

    dh2, dy2, g["ffn_post_norm"] = _rowwise(ffn_post_bwd, [dh3, dhr, sv["h2"], sv["y2"]], [w["ffn_post_norm"]],
                                            [(d, F32), (d, BF16)], [d], "ffn_post_bwd", ROW_TILE)
    dt = _mm(dy2, w["w_ffn_out"], "nt", BF16, "mm_dt", b3="n", o3="n")
    t, dab = _swiglu_bwd(sv["ab"], dt, 2 * ROW_TILE)
    dab = dab.reshape((N_DEV,) + dab.shape[2:])
    g["w_ffn_out"] = by_rows(_mm(t, dy2, "tn", BF16, "mm_dffn_out", a3="m", o3="m"))
    dhn2 = _mm(dab, w["w_ffn_in"], "nn", F32, "mm_dhn2", a3="k", b3="k")
    g["w_ffn_in"] = _mm(dab, sv["hn2"], "tn", BF16, "mm_dffn_in", a3="m", o3="m")

    def mix_post_bwd(dh2, dhn2, h1, y1, gpre, gpost):
        h1hat, r1 = _rms(h1)
        dh1 = dh2 + _rms_bwd(h1hat, r1, dhn2 * gpre)
        y1hat, ry = _rms(y1)
        return dh1, _rms_bwd(y1hat, ry, dh1 * gpost), _colsum(dhn2 * h1hat), _colsum(dh1 * y1hat)

    dh1, dy1, g["ffn_pre_norm"], g["mix_post_norm"] = _rowwise(
        mix_post_bwd, [dh2, dhn2, sv["h1"], sv["y1"]], [w["ffn_pre_norm"], w["mix_post_norm"]],
        [(d, F32), (d, BF16)], [d, d], "mix_post_bwd", ROW_TILE)
    dmc = _mm(dy1, w["w_out"], "nt", F32, "mm_dmc")
    g["w_out"] = by_rows(_mm(sv["mc"], dy1, "tn", BF16, "mm_dout"))
    w = dict(w, **mid(g, dmc))

    def mix_out_bwd(da, dg, att, gm, gain):
        atthat, ra = _rms(att)
        gmhat, rg = _rms(gm)
        dgain = jnp.concatenate([_colsum(da * atthat), _colsum(dg * gmhat)], axis=-1)
        return _rms_bwd(atthat, ra, da * gain[:, :D_ATT]), _rms_bwd(gmhat, rg, dg * gain[:, D_ATT:]), dgain

    datt, dgm, g["mix_out_norm"] = _rowwise(
        mix_out_bwd, [(dmc, 0, D_ATT), (dmc, 1, D_GM), sv["att"], sv["gm"]], [w["mix_out_norm"]],
        [(D_ATT, BF16), (D_GM, F32)], [D_ATT + D_GM], "mix_out_bwd", ROW_TILE)

    dgu, dgv, dwt, dbs_t, g["gm_v_norm"] = _gmlp_bwd(sv["z"], dgm, w["wt"], w["wt_t"], w["bs_t"], w["gm_v_norm"])
    g["gm_w_s"] = dwt * jnp.tril(jnp.ones((CHUNK, CHUNK), F32))[None]
    g["gm_b_s"] = dbs_t.T

    delta = _attn_delta(sv["att"], datt, tq)
    dq, dk, dv, dc_row = _attn_bwd(sv["qa"], sv["ka"], sv["vb"], datt, sv["lse"], delta, tq)
    dfl_t, db = _forget_bwd(dc_row.reshape(N_HEADS, s), sv["fl_t"], w["b_forget"])
    g["b_forget"] = db.reshape(1, N_HEADS)
    dz = _dz_concat([dq, dk, dv, dgu, dgv], dfl_t.T, ROW_TILE)
    dhn1 = _mm(dz, w["w_in"], "nt", F32, "mm_dhn1")
    g["w_in"] = _w_in_split(_mm(sv["hn1"], dz, "tn", BF16, "mm_din"))

    def mix_pre_bwd(dh1, dhn1, h0, gpre):
        h0hat, r0 = _rms(h0)
        return dh1 + _rms_bwd(h0hat, r0, dhn1 * gpre), _colsum(dhn1 * h0hat)

    dh0, g["mix_pre_norm"] = _rowwise(mix_pre_bwd, [dh1, dhn1, sv["h0"]], [w["mix_pre_norm"]],
                                      [(d, F32)], [d], "mix_pre_bwd", ROW_TILE)
    return dh0, g


ANY = pl.BlockSpec(memory_space=pl.ANY)


def _all_gather(xs, layer, name):
    n = len(xs)

    def body(*refs):
        x_refs, out_refs = refs[:n], refs[n:2 * n]
        send_sems, recv_sems, local_sems = refs[2 * n:]
        x, y, c = lax.axis_index("x"), lax.axis_index("y"), lax.axis_index("c")
        me, sibling = (x, y, c), (x, y, 1 - c)
        chips = [(1 - x, y), (x, 1 - y), (1 - x, 1 - y)]

        def shard(a):
            return x_refs[a] if layer is None else x_refs[a].at[layer]

        def rows(a, px, py, pc):
            return out_refs[a].at[4 * px + 2 * py + pc]

        def copy(a, kk, block, to, from_shard=False):
            return pltpu.make_async_remote_copy(
                src_ref=shard(a) if from_shard else rows(a, *block), dst_ref=rows(a, *block),
                send_sem=send_sems.at[7 * a + kk], recv_sem=recv_sems.at[7 * a + kk],
                device_id=to, device_id_type=MESH)

        mine = [pltpu.make_async_copy(shard(a), rows(a, *me), local_sems.at[a]) for a in range(n)]
        for cp in mine:
            cp.start()
        first = []
        for a in range(n):
            first.append(copy(a, 0, me, sibling, from_shard=True))
            first += [copy(a, 1 + j, me, (*chip, c), from_shard=True) for j, chip in enumerate(chips)]
        for cp in first:
            cp.start()
        passed = []
        for j, chip in enumerate(chips):
            for a in range(n):
                copy(a, 1 + j, (*chip, c), me).wait_recv()
                passed.append(copy(a, 4 + j, (*chip, c), sibling))
                passed[-1].start()
        for a in range(n):
            copy(a, 0, sibling, me).wait_recv()
        for j, chip in enumerate(chips):
            for a in range(n):
                copy(a, 4 + j, (*chip, 1 - c), me).wait_recv()
        for cp in first + passed:
            cp.wait_send()
        for cp in mine:
            cp.wait()

    shapes = [x.shape if layer is None else x.shape[1:] for x in xs]
    return pl.pallas_call(
        body, name=name, out_shape=[jax.ShapeDtypeStruct((N_DEV,) + sh, x.dtype) for sh, x in zip(shapes, xs)],
        in_specs=[ANY] * n, out_specs=[ANY] * n,
        scratch_shapes=[pltpu.SemaphoreType.DMA((7 * n,)), pltpu.SemaphoreType.DMA((7 * n,)),
                        pltpu.SemaphoreType.DMA((n,))],
    )(*xs)


HBM = pl.BlockSpec(memory_space=pltpu.HBM)
SEMS = pl.BlockSpec(memory_space=pltpu.SEMAPHORE)
EFFECT = pltpu.SideEffectType.DATAFLOW_SIDE_EFFECTING
FLIPS = tuple((fx, fy, fc) for fx in (0, 1) for fy in (0, 1) for fc in (0, 1))[1:]


def _exchange_copies(src_refs, land_refs, send_sems, recv_sems, layer, scatter):
    x, y, c = lax.axis_index("x"), lax.axis_index("y"), lax.axis_index("c")
    me = 4 * x + 2 * y + c
    copies = []
    for a, (src, land) in enumerate(zip(src_refs, land_refs)):
        for f, (fx, fy, fc) in enumerate(FLIPS):
            px, py, pc = (1 - x if fx else x), (1 - y if fy else y), (1 - c if fc else c)
            if scatter:
                block = src.at[4 * px + 2 * py + pc]
            else:
                block = src if layer is None else src.at[layer]
            copies.append(pltpu.make_async_remote_copy(
                src_ref=block, dst_ref=land.at[me], send_sem=send_sems.at[7 * a + f], recv_sem=recv_sems.at[7 * a + f],
                device_id=(px, py, pc), device_id_type=MESH))
    return copies


def _exchange_start(srcs, lands, layer, scatter, name):
    n = len(srcs)

    def body(*refs):
        for cp in _exchange_copies(refs[:n], refs[n:2 * n], refs[2 * n], refs[2 * n + 1], layer, scatter):
            cp.start()
        token = refs[-1]
        token[...] = jnp.zeros_like(token)

    operands = list(srcs) + list(lands)
    outs = pl.pallas_call(
        body, name=name,
        out_shape=(pltpu.SemaphoreType.DMA((7 * n,)), pltpu.SemaphoreType.DMA((7 * n,)),
                   *[pltpu.HBM(a.shape, a.dtype) for a in operands], jax.ShapeDtypeStruct((8, LANE), F32)),
        in_specs=[HBM] * (2 * n),
        out_specs=(SEMS, SEMS, *[HBM] * (2 * n), pl.BlockSpec(memory_space=pltpu.VMEM)),
        input_output_aliases={i: 2 + i for i in range(2 * n)},
        compiler_params=pltpu.CompilerParams(has_side_effects=EFFECT),
    )(*[pltpu.with_memory_space_constraint(a, pltpu.HBM) for a in operands])
    return outs[0], outs[1], outs[2:2 + n], outs[2 + n:2 + 2 * n], outs[-1]


def _exchange_wait(started, after, layer, scatter, name):
    send_sems, recv_sems, srcs, lands, _ = started
    n = len(srcs)

    def body(*refs):
        for cp in _exchange_copies(refs[:n], refs[n:2 * n], refs[2 * n], refs[2 * n + 1], layer, scatter):
            cp.wait_send()
            cp.wait_recv()

    operands = list(srcs) + list(lands)
    outs = pl.pallas_call(
        body, name=name, out_shape=tuple(pltpu.HBM(a.shape, a.dtype) for a in operands),
        in_specs=[HBM] * (2 * n) + [SEMS, SEMS, ANY], out_specs=[HBM] * (2 * n),
        input_output_aliases={i: i for i in range(2 * n)},
        compiler_params=pltpu.CompilerParams(has_side_effects=EFFECT),
    )(*operands, send_sems, recv_sems, after)
    return outs[:n], outs[n:]


def _sum_devices(parts):
    _, r, c = parts.shape

    def body(p_ref, o_ref):
        acc = p_ref[0].astype(F32)
        for j in range(1, N_DEV):
            acc = acc + p_ref[j].astype(F32)
        o_ref[...] = acc

    return pl.pallas_call(
        body, name="small_sum", out_shape=jax.ShapeDtypeStruct((r, c), F32), grid=(r // SMALL_ROWS,),
        in_specs=[pl.BlockSpec((N_DEV, SMALL_ROWS, c), lambda i: (0, i, 0))],
        out_specs=pl.BlockSpec((SMALL_ROWS, c), lambda i: (i, 0)),
        compiler_params=_cparams("parallel"),
    )(parts)


def _adamw_math(w, g, m, v):
    m = ADAM_B1 * m + (1.0 - ADAM_B1) * g
    v = ADAM_B2 * v + (1.0 - ADAM_B2) * (g * g)
    m_hat = m / (1.0 - ADAM_B1 ** ADAM_STEP)
    v_hat = v / (1.0 - ADAM_B2 ** ADAM_STEP)
    return -ADAM_LR * (m_hat / (jnp.sqrt(v_hat) + ADAM_EPS) + ADAM_WD * w), m, v


def _adamw_shard(w, m, v, parts, layer, outs, name):
    _, a, b = w.shape
    ta = _tile(a, 256, 16)
    if outs is None:
        outs = [lax.empty(w.shape, F32) for _ in range(4)]

    def body(w_ref, m_ref, v_ref, p_ref, *refs):
        g_ref, d_ref, nm_ref, nv_ref = refs[4:]
        g = p_ref[0].astype(F32)
        for j in range(1, N_DEV):
            g = g + p_ref[j].astype(F32)
        g_ref[0] = g
        d_ref[0], nm_ref[0], nv_ref[0] = _adamw_math(w_ref[0], g, m_ref[0], v_ref[0])

    mine = pl.BlockSpec((1, ta, b), lambda i: (layer, i, 0))
    return pl.pallas_call(
        body, name=name, out_shape=[jax.ShapeDtypeStruct(w.shape, F32)] * 4, grid=(a // ta,),
        in_specs=[mine, mine, mine, pl.BlockSpec((N_DEV, ta, b), lambda i: (0, i, 0))] + [ANY] * 4,
        out_specs=[mine] * 4, input_output_aliases={4 + k: k for k in range(4)},
        compiler_params=_cparams("parallel"),
    )(w, m, v, parts, *outs)


def _w_in_moves():
    n = D_IN // N_DEV
    runs = ((0, 3 * D_ATT, 0), (3 * D_ATT, 3 * D_ATT + N_HEADS, F_OFF), (3 * D_ATT + N_HEADS, D_IN, 3 * D_ATT))
    moves = []
    for j in range(N_DEV):
        for lo, hi, padded in runs:
            a, b = max(n * j, lo), min(n * (j + 1), hi)
            if a < b:
                moves.append((j, a - n * j, padded + a - lo, b - a))
    return moves


def _w_in_assemble(blocks):
    _, d, n = blocks.shape
    tr = _tile(d, 256, 16)

    def body(b_ref, o_ref):
        o_ref[:, D_IN:] = jnp.zeros((tr, D_IN_PAD - D_IN), o_ref.dtype)
        for j, col, padded, width in _w_in_moves():
            o_ref[:, padded:padded + width] = b_ref[j, :, col:col + width]

    return pl.pallas_call(
        body, name="w_in_assemble", out_shape=jax.ShapeDtypeStruct((d, D_IN_PAD), blocks.dtype), grid=(d // tr,),
        in_specs=[pl.BlockSpec((N_DEV, tr, n), lambda i: (0, i, 0))],
        out_specs=pl.BlockSpec((tr, D_IN_PAD), lambda i: (i, 0)),
        compiler_params=_cparams("parallel"),
    )(blocks)


def _w_in_split(padded):
    d = padded.shape[0]
    n = D_IN // N_DEV
    tr = _tile(d, 256, 16)

    def body(p_ref, o_ref):
        for j, col, src, width in _w_in_moves():
            o_ref[j, :, col:col + width] = p_ref[:, src:src + width]

    return pl.pallas_call(
        body, name="w_in_split", out_shape=jax.ShapeDtypeStruct((N_DEV, d, n), padded.dtype), grid=(d // tr,),
        in_specs=[pl.BlockSpec((tr, D_IN_PAD), lambda i: (i, 0))],
        out_specs=pl.BlockSpec((N_DEV, tr, n), lambda i: (0, i, 0)),
        compiler_params=_cparams("parallel"),
    )(padded)


def _small_rows(size):
    return -(-size // (8 * SMALL_COLS)) * 8


def _pack_small(pieces):
    rows = []
    for p in pieces:
        flat = p.reshape(-1)
        rows.append(jnp.pad(flat, (0, _small_rows(flat.shape[0]) * SMALL_COLS - flat.shape[0])).reshape(-1, SMALL_COLS))
    used = sum(r.shape[0] for r in rows)
    rows.append(jnp.zeros((-used % SMALL_ROWS, SMALL_COLS), F32))
    return jnp.concatenate(rows)


def kernel(x, p, mix_pre_norm, mix_post_norm, w_in, b_forget, gm_v_norm, gm_w_s, gm_b_s, mix_out_norm, w_out, ffn_pre_norm, ffn_post_norm, w_ffn_in, w_ffn_out, w_ple, ple_norm, w_ple_gate, loss_target, m_mix_pre_norm, m_mix_post_norm, m_w_in, m_b_forget, m_gm_v_norm, m_gm_w_s, m_gm_b_s, m_mix_out_norm, m_w_out, m_ffn_pre_norm, m_ffn_post_norm, m_w_ffn_in, m_w_ffn_out, m_w_ple, m_ple_norm, m_w_ple_gate, v_mix_pre_norm, v_mix_post_norm, v_w_in, v_b_forget, v_gm_v_norm, v_gm_w_s, v_gm_b_s, v_mix_out_norm, v_w_out, v_ffn_pre_norm, v_ffn_post_norm, v_w_ffn_in, v_w_ffn_out, v_w_ple, v_ple_norm, v_w_ple_gate):
    given = dict(locals())
    weights = {n: given[n] for n in WEIGHT_ORDER}
    mom_m = {n: given["m_" + n] for n in WEIGHT_ORDER}
    mom_v = {n: given["v_" + n] for n in WEIGHT_ORDER}
    depth = w_in.shape[0]
    s, d = x.shape[1], x.shape[2]
    tq = _tile(s, ATT_BLOCK)
    me = 4 * lax.axis_index("x") + 2 * lax.axis_index("y") + lax.axis_index("c")
    tril = jnp.tril(jnp.ones((CHUNK, CHUNK), F32))

    def landing(block):
        return lax.dynamic_update_index_in_dim(lax.empty((N_DEV,) + block.shape, block.dtype), block, me, 0)

    def mix_weights(i, got):
        wt = gm_w_s[i] * tril[None]
        lw = dict(
            w_in=_w_in_assemble(got["w_in"]),
            b_forget=b_forget[i][:, None], wt=wt.astype(BF16), wt_t=wt.transpose(0, 2, 1).astype(BF16),
            bs_t=gm_b_s[i].T)
        lw.update({n: weights[n][i][None] for n in ("mix_pre_norm", "mix_post_norm", "gm_v_norm", "mix_out_norm",
                                                    "ffn_pre_norm", "ffn_post_norm", "ple_norm")})
        return lw

    def rest_weights(got):
        return dict(w_out=got["w_out"].reshape(-1, d), w_ffn_in=got["w_ffn_in"],
                    w_ffn_out=got["w_ffn_out"].reshape(N_DEV // 2, -1, d),
                    w_ple=jnp.concatenate([got["w_ple"][j] for j in range(N_DEV)], axis=-1),
                    w_ple_gate=got["w_ple_gate"].reshape(-1, d))

    def shard_view(n, a):
        return jnp.transpose(a, (0, 2, 1)) if n in TRANSPOSED_WEIGHTS else a

    shards = {n: shard_view(n, weights[n].astype(BF16)) for n in MATRIX_WEIGHTS}

    def gather_start(i):
        started = {}
        order = jnp.zeros((), BF16)
        for tag, grp in EXCHANGE_GROUPS.items():
            started[tag] = _exchange_start([shards[n] for n in grp], [landing(shards[n][i] + order) for n in grp], i,
                                           False, f"weights_gather_start_{i}_{tag}")
            order = started[tag][4][0, 0].astype(BF16)
        return started

    def gather_finish(i, tag, pending, after):
        srcs, got = _exchange_wait(pending[tag], after, i, False, f"weights_gather_wait_{i}_{tag}")
        shards.update(zip(EXCHANGE_GROUPS[tag], srcs))
        return dict(zip(EXCHANGE_GROUPS[tag], got))

    h = x[0]
    saved, layer_w = [], []
    pending = gather_start(0)
    for i in range(depth):
        lw = mix_weights(i, gather_finish(i, "mix", pending, h))
        if i == 0:
            lw["mix_pre_norm"] = lw["mix_pre_norm"] + pending["rest"][4][:1, :1]
        following = {}

        def late(att, i=i, pending=pending, lw=lw, following=following):
            rest = rest_weights(gather_finish(i, "rest", pending, att))
            lw.update(rest)
            if i + 1 == depth:
                return rest
            following.update(gather_start(i + 1))
            token = following["mix"][4][:1, :1] + following["rest"][4][:1, :1]
            return dict(rest, mix_out_norm=lw["mix_out_norm"] + token)

        h, sv = _layer_fwd(h, p[i, 0], lw, tq, late)
        layer_w.append(lw)
        saved.append(sv)
        pending = following

    def loss_head(y, t):
        err = y - t
        return err * (1.0 / d), _colsum(err * err)

    dh, sq = _rowwise(loss_head, [h, loss_target[0]], [], [(d, F32)], [d], "loss_head", ROW_TILE)
    loss = lax.psum(0.5 * jnp.sum(sq) / d, AXES)

    layer_g = [None] * depth
    shard_out = {n: None for n in MATRIX_WEIGHTS}

    def scatter_start(i, tag, g):
        full_g = [g[n] for n in EXCHANGE_GROUPS[tag]]
        lands = [landing(lax.dynamic_index_in_dim(gf, me, 0, keepdims=False)) for gf in full_g]
        return _exchange_start(full_g, lands, None, True, f"grads_scatter_start_{i}_{tag}")

    def scatter_finish(i, tag, started, after):
        _, parts = _exchange_wait(started[tag], after, None, True, f"grads_scatter_wait_{i}_{tag}")
        for n, part in zip(EXCHANGE_GROUPS[tag], parts):
            shard_out[n] = _adamw_shard(shard_view(n, weights[n]), shard_view(n, mom_m[n]), shard_view(n, mom_v[n]),
                                        part, i, shard_out[n], "adamw_" + n)

    before = None
    for i in reversed(range(depth)):
        lw = layer_w[i]
        if before is not None:
            lw = dict(lw, ple_norm=lw["ple_norm"] + before[1]["mix"][4][:1, :1])
        started = {}

        def mid(g, dmc, i=i, before=before, started=started, lw=lw):
            if before is not None:
                scatter_finish(before[0], "rest", before[1], dmc)
            started["rest"] = scatter_start(i, "rest", g)
            return dict(mix_out_norm=lw["mix_out_norm"] + started["rest"][4][:1, :1])

        dh, layer_g[i] = _layer_bwd(dh, saved[i], lw, tq, mid)
        if before is not None:
            scatter_finish(before[0], "mix", before[1], dh)
        started["mix"] = scatter_start(i, "mix", layer_g[i])
        before = (i, started)
    scatter_finish(before[0], "rest", before[1], before[1]["mix"][4])
    scatter_finish(before[0], "mix", before[1], dh)
    grad_x = dh[None]

    grads, deltas, new_m, new_v = {}, {}, {}, {}
    for n in MATRIX_WEIGHTS:
        grads[n], deltas[n], new_m[n], new_v[n] = (shard_view(n, a) for a in shard_out[n])

    small_g = _pack_small([jnp.stack([layer_g[i][n].reshape(-1) for i in range(depth)]) for n in SMALL_WEIGHTS])
    (gathered,) = _all_gather([small_g.astype(BF16)], None, "small_grads_all_gather")
    g_small = _sum_devices(gathered)
    pack = lambda t: _pack_small([t[n] for n in SMALL_WEIGHTS])
    dl, nm, nv = _rowwise(_adamw_math, [pack(weights), g_small, pack(mom_m), pack(mom_v)], [],
                          [(SMALL_COLS, F32)] * 3, [], "adamw_small", SMALL_ROWS)
    row = 0
    for n in SMALL_WEIGHTS:
        shp, size = weights[n].shape, weights[n].size
        grads[n], deltas[n], new_m[n], new_v[n] = (
            a[row:row + _small_rows(size)].reshape(-1)[:size].reshape(shp) for a in (g_small, dl, nm, nv))
        row += _small_rows(size)

    return (loss, grad_x, *[grads[n] for n in WEIGHT_ORDER], *[deltas[n] for n in WEIGHT_ORDER],
            *[new_m[n] for n in WEIGHT_ORDER], *[new_v[n] for n in WEIGHT_ORDER])
```

```python
import functools
import math

import jax
import jax.numpy as jnp
from jax import lax
from jax.experimental import pallas as pl
from jax.experimental.pallas import tpu as pltpu

F32 = jnp.float32
BF16 = jnp.bfloat16
MESH = pl.DeviceIdType.MESH
AXES = ("x", "y", "c")
N_DEV = 8

EPS = 1e-6
NEG_INF = -1e30
N_HEADS = 8
HEAD_DIM = 64
D_ATT = N_HEADS * HEAD_DIM
N_GROUPS = 8
GROUP_DIM = 64
D_GM = N_GROUPS * GROUP_DIM
CHUNK = 128
ATT_SCALE = HEAD_DIM ** -0.5
ATT_BLOCK = 1024
D_IN = 3 * D_ATT + N_HEADS + 2 * D_GM
D_IN_PAD = 3 * D_ATT + 2 * D_GM + 128
F_OFF = 3 * D_ATT + 2 * D_GM

ADAM_LR = 0.001
ADAM_B1 = 0.9
ADAM_B2 = 0.999
ADAM_EPS = 1e-08
ADAM_WD = 0.01
ADAM_STEP = 10

LANE = 128
VMEM_LIMIT = 48 * 1024 * 1024
ROW_TILE = 512
K_TILE = 4096
SMALL_COLS = 128
SMALL_ROWS = 512

MATRIX_WEIGHTS = ("w_in", "w_out", "w_ffn_in", "w_ffn_out", "w_ple", "w_ple_gate")
EXCHANGE_GROUPS = {"mix": ("w_in",), "rest": ("w_out", "w_ffn_in", "w_ffn_out", "w_ple", "w_ple_gate")}
TRANSPOSED_WEIGHTS = ("w_ffn_in",)
SMALL_WEIGHTS = ("mix_pre_norm", "mix_post_norm", "b_forget", "gm_v_norm", "gm_w_s", "gm_b_s",
                 "mix_out_norm", "ffn_pre_norm", "ffn_post_norm", "ple_norm")
WEIGHT_ORDER = ("mix_pre_norm", "mix_post_norm", "w_in", "b_forget", "gm_v_norm", "gm_w_s", "gm_b_s",
                "mix_out_norm", "w_out", "ffn_pre_norm", "ffn_post_norm", "w_ffn_in", "w_ffn_out",
                "w_ple", "ple_norm", "w_ple_gate")


def _tile(n, pref, unit=LANE):
    best = None
    t = unit
    while t <= min(n, pref):
        if n % t == 0:
            best = t
        t += unit
    return n if best is None else best


def _cparams(*semantics):
    return pltpu.CompilerParams(dimension_semantics=semantics or None, vmem_limit_bytes=VMEM_LIMIT)


NN = (((1,), (0,)), ((), ()))
NT = (((1,), (1,)), ((), ()))
TN = (((0,), (0,)), ((), ()))
_MM_AXES = {
    "nn": ("i", "k", "k", "j"), "nt": ("i", "k", "j", "k"), "tn": ("k", "i", "k", "j")}
_MM_DN = {"nn": NN, "nt": NT, "tn": TN}


def _mm(a, b, dims, out_dtype, name, a3=None, b3=None, o3=None, tm=1024, tn=1024, tk=None):
    ar, ac, br, bc = _MM_AXES[dims]
    letter = {"i": "m", "j": "n", "k": "k"}
    size = {}

    def measure(x, rows, cols, stacked):
        shape = x.shape
        if stacked is None:
            size.setdefault(letter[rows], shape[0])
            size.setdefault(letter[cols], shape[1])
        else:
            for ax, n in ((rows, shape[1]), (cols, shape[2])):
                size.setdefault(letter[ax], n * shape[0] if letter[ax] == stacked else n)

    measure(a, ar, ac, a3)
    measure(b, br, bc, b3)
    m, n, k = size["m"], size["n"], size["k"]
    slab = {}
    for x, stacked, rows, cols in ((a, a3, ar, ac), (b, b3, br, bc)):
        if stacked is not None:
            slab[stacked] = x.shape[1] if letter[rows] == stacked else x.shape[2]
    if o3 is not None:
        slab.setdefault(o3, slab.get(o3, None) or {"m": m, "n": n}[o3] // N_DEV)
    tk = tk or K_TILE
    tile = {"m": slab.get("m") or _tile(m, tm), "n": slab.get("n") or _tile(n, tn), "k": slab.get("k") or _tile(k, tk)}
    group = 1
    if a3 == "k" and b3 == "k":
        group = max(g for g in range(1, a.shape[0] + 1) if a.shape[0] % g == 0 and g * tile["k"] <= max(tk, tile["k"]))
    nk = k // (group * tile["k"])

    def spec(rows, cols, stacked):
        tr, tc = tile[letter[rows]], tile[letter[cols]]
        if stacked is None:
            return pl.BlockSpec((tr, tc), lambda i, j, kk: ({"i": i, "j": j, "k": kk}[rows], {"i": i, "j": j, "k": kk}[cols]))

        def imap(i, j, kk):
            g = {"i": i, "j": j, "k": kk}
            return (g[{"m": "i", "n": "j", "k": "k"}[stacked]],
                    0 if letter[rows] == stacked else g[rows], 0 if letter[cols] == stacked else g[cols])

        return pl.BlockSpec((group if stacked == "k" else 1, tr, tc), imap)

    dn = _MM_DN[dims]

    def body(a_ref, b_ref, o_ref, *acc):
        prod = None
        for g in range(group):
            av = a_ref[...] if a3 is None else a_ref[g]
            bv = b_ref[...] if b3 is None else b_ref[g]
            term = lax.dot_general(av.astype(BF16), bv.astype(BF16), dn, preferred_element_type=F32)
            prod = term if prod is None else prod + term

        def emit(val):
            if o3 is None:
                o_ref[...] = val.astype(out_dtype)
            else:
                o_ref[0] = val.astype(out_dtype)

        if nk == 1:
            emit(prod)
            return
        (acc_ref,) = acc
        kk = pl.program_id(2)

        @pl.when(kk == 0)
        def _():
            acc_ref[...] = prod

        @pl.when(kk > 0)
        def _():
            acc_ref[...] += prod

        @pl.when(kk == nk - 1)
        def _():
            emit(acc_ref[...])

    if o3 is None:
        out_shape = (m, n)
    elif o3 == "m":
        out_shape = (m // tile["m"], tile["m"], n)
    else:
        out_shape = (n // tile["n"], m, tile["n"])
    return pl.pallas_call(
        body, name=name, out_shape=jax.ShapeDtypeStruct(out_shape, out_dtype),
        grid=(m // tile["m"], n // tile["n"], nk),
        in_specs=[spec(ar, ac, a3), spec(br, bc, b3)], out_specs=spec("i", "j", o3),
        scratch_shapes=[] if nk == 1 else [pltpu.VMEM((tile["m"], tile["n"]), F32)],
        compiler_params=_cparams("parallel", "parallel", "arbitrary"),
    )(a, b)


def _rowwise(fn, rows, vecs, outs, reds, name, ts):
    rows = [r if isinstance(r, tuple) else (r, 0, r.shape[1]) for r in rows]
    s = rows[0][0].shape[0]
    ts = _tile(s, ts, 8)
    nr, nv, no = len(rows), len(vecs), len(outs)

    def body(*refs):
        vals = fn(*[r[...] for r in refs[:nr + nv]])
        vals = vals if isinstance(vals, tuple) else (vals,)
        o_refs = refs[nr + nv:nr + nv + no]
        r_refs = refs[nr + nv + no:]
        for o_ref, val in zip(o_refs, vals[:no]):
            o_ref[...] = val.astype(o_ref.dtype)
        if r_refs:
            @pl.when(pl.program_id(0) == 0)
            def _():
                for r_ref in r_refs:
                    r_ref[...] = jnp.zeros_like(r_ref)

            for r_ref, val in zip(r_refs, vals[no:]):
                r_ref[...] += val

    in_specs = [pl.BlockSpec((ts, w), functools.partial(lambda i, cb: (i, cb), cb=cb)) for _, cb, w in rows]
    in_specs += [pl.BlockSpec(v.shape, lambda i: (0, 0)) for v in vecs]
    out_specs = [pl.BlockSpec((ts, c), lambda i: (i, 0)) for c, _ in outs]
    out_specs += [pl.BlockSpec((1, c), lambda i: (0, 0)) for c in reds]
    out_shape = [jax.ShapeDtypeStruct((s, c), dt) for c, dt in outs]
    out_shape += [jax.ShapeDtypeStruct((1, c), F32) for c in reds]
    return pl.pallas_call(
        body, name=name, out_shape=out_shape, grid=(s // ts,), in_specs=in_specs, out_specs=out_specs,
        compiler_params=_cparams("arbitrary" if reds else "parallel"),
    )(*[r[0] for r in rows], *vecs)


def _rms(x):
    r = lax.rsqrt(jnp.mean(x * x, axis=-1, keepdims=True) + EPS)
    return x * r, r


def _rms_bwd(xhat, r, dyg):
    return r * (dyg - xhat * jnp.mean(dyg * xhat, axis=-1, keepdims=True))


def _colsum(x):
    return jnp.sum(x, axis=0, keepdims=True)


def _sigmoid(x):
    return 1.0 / (1.0 + jnp.exp(-x))


GELU_C = math.sqrt(2.0 / math.pi)
GELU_A = 0.044715


def _gelu(x):
    return 0.5 * x * (1.0 + jnp.tanh(GELU_C * (x + GELU_A * x * x * x)))


def _gelu_grad(x):
    t = jnp.tanh(GELU_C * (x + GELU_A * x * x * x))
    return 0.5 * (1.0 + t) + 0.5 * x * (1.0 - t * t) * GELU_C * (1.0 + 3.0 * GELU_A * x * x)


def _ffn_in_swiglu(hn, wg):
    s, d = hn.shape
    g2, n, _ = wg.shape
    g = g2 // 2
    tm = _tile(s, 1024)

    def body(h_ref, wa_ref, wb_ref, ab_ref, t_ref):
        hv = h_ref[...]
        a = lax.dot_general(hv, wa_ref[0], NT, preferred_element_type=F32)
        b = lax.dot_general(hv, wb_ref[0], NT, preferred_element_type=F32)
        ab_ref[0, 0] = a.astype(BF16)
        ab_ref[1, 0] = b.astype(BF16)
        t_ref[0] = (a * _sigmoid(a) * b).astype(BF16)

    return pl.pallas_call(
        body, name="mm_ffn_in_swiglu",
        out_shape=(jax.ShapeDtypeStruct((2, g, s, n), BF16), jax.ShapeDtypeStruct((g, s, n), BF16)),
        grid=(s // tm, g),
        in_specs=[pl.BlockSpec((tm, d), lambda i, j: (i, 0)), pl.BlockSpec((1, n, d), lambda i, j: (j, 0, 0)),
                  pl.BlockSpec((1, n, d), lambda i, j: (j + g, 0, 0))],
        out_specs=(pl.BlockSpec((2, 1, tm, n), lambda i, j: (0, j, i, 0)),
                   pl.BlockSpec((1, tm, n), lambda i, j: (j, i, 0))),
        compiler_params=_cparams("parallel", "parallel"),
    )(hn, wg, wg)


def _ffn_out_swiglu_bwd(dy, w4, ab):
    s, d = dy.shape
    _, g, _, n = ab.shape
    tm = _tile(s, 1024)

    def body(dy_ref, w_ref, ab_ref, t_ref, dab_ref):
        dt = lax.dot_general(dy_ref[...], w_ref[0], NT, preferred_element_type=F32)
        a = ab_ref[0, 0].astype(F32)
        b = ab_ref[1, 0].astype(F32)
        sig = _sigmoid(a)
        silu = a * sig
        t_ref[0] = (silu * b).astype(BF16)
        dab_ref[0, 0] = (dt * b * (sig * (1.0 + a * (1.0 - sig)))).astype(BF16)
        dab_ref[1, 0] = (dt * silu).astype(BF16)

    both = pl.BlockSpec((2, 1, tm, n), lambda i, j: (0, j, i, 0))
    one = pl.BlockSpec((1, tm, n), lambda i, j: (j, i, 0))
    return pl.pallas_call(
        body, name="mm_dt_swiglu_bwd",
        out_shape=(jax.ShapeDtypeStruct((g, s, n), BF16), jax.ShapeDtypeStruct((2, g, s, n), BF16)),
        grid=(s // tm, g),
        in_specs=[pl.BlockSpec((tm, d), lambda i, j: (i, 0)), pl.BlockSpec((1, n, d), lambda i, j: (j, 0, 0)), both],
        out_specs=(one, both),
        compiler_params=_cparams("parallel", "parallel"),
    )(dy, w4, ab)


def _forget_fwd(fl_t, b_col):
    h, s = fl_t.shape
    nb = s // LANE

    def body(fl_ref, b_ref, c_ref):
        upper = (lax.broadcasted_iota(jnp.int32, (LANE, LANE), 0)
                 <= lax.broadcasted_iota(jnp.int32, (LANE, LANE), 1)).astype(F32)

        def step(i, carry):
            x = fl_ref[i] + b_ref[...]
            lf = jnp.minimum(x, 0.0) - jnp.log(1.0 + jnp.exp(-jnp.abs(x)))
            cs = jnp.dot(lf, upper, precision=lax.Precision.HIGHEST, preferred_element_type=F32) + carry
            c_ref[i] = cs
            return cs[:, LANE - 1:LANE]

        lax.fori_loop(0, nb, step, jnp.zeros((h, 1), F32))

    out = pl.pallas_call(
        body, name="forget_fwd", out_shape=jax.ShapeDtypeStruct((nb, h, LANE), F32),
        compiler_params=_cparams(),
    )(fl_t.reshape(h, nb, LANE).transpose(1, 0, 2), b_col)
    return out.transpose(1, 0, 2).reshape(h, s)


def _forget_bwd(dc_t, fl_t, b_col):
    h, s = fl_t.shape
    nb = s // LANE

    def body(dc_ref, fl_ref, b_ref, dfl_ref, db_ref):
        lower = (lax.broadcasted_iota(jnp.int32, (LANE, LANE), 0)
                 >= lax.broadcasted_iota(jnp.int32, (LANE, LANE), 1)).astype(F32)

        def step(t, carry):
            tail, db = carry
            i = nb - 1 - t
            rc = jnp.dot(dc_ref[i], lower, precision=lax.Precision.HIGHEST, preferred_element_type=F32) + tail
            dfl = rc * (1.0 - _sigmoid(fl_ref[i] + b_ref[...]))
            dfl_ref[i] = dfl
            return rc[:, 0:1], db + jnp.sum(dfl, axis=1, keepdims=True)

        _, db = lax.fori_loop(0, nb, step, (jnp.zeros((h, 1), F32), jnp.zeros((h, 1), F32)))
        db_ref[...] = db

    blocked = lambda a: a.reshape(h, nb, LANE).transpose(1, 0, 2)
    dfl, db = pl.pallas_call(
        body, name="forget_bwd",
        out_shape=(jax.ShapeDtypeStruct((nb, h, LANE), F32), jax.ShapeDtypeStruct((h, 1), F32)),
        compiler_params=_cparams(),
    )(blocked(dc_t), blocked(fl_t), b_col)
    return dfl.transpose(1, 0, 2).reshape(h, s), db


N_PAIRS = N_HEADS // 2


def _causal_mask(t):
    return lax.broadcasted_iota(jnp.int32, (t, t), 0) >= lax.broadcasted_iota(jnp.int32, (t, t), 1)


def _head_lanes():
    return lax.broadcasted_iota(jnp.int32, (1, 2 * HEAD_DIM), 1) < HEAD_DIM


def _pick(x2, first, hh):
    zero = jnp.zeros_like(x2)
    return jnp.where(first, x2, zero) if hh == 0 else jnp.where(first, zero, x2)


BIAS_TERMS = 3


def _attn_prep(z, c, ts):
    s = z.shape[0]
    ts = _tile(s, ts, 16)
    w = 2 * HEAD_DIM

    def body(q_ref, k_ref, v_ref, c_ref, qa_ref, ka_ref, vb_ref):
        lane = lax.broadcasted_iota(jnp.int32, (1, w), 1)
        first = lane < HEAD_DIM
        cv = c_ref[...]
        for h in range(N_HEADS):
            pair = slice((h // 2) * w, (h // 2 + 1) * w)
            qh = q_ref[:, pair] * ATT_SCALE
            kh = k_ref[:, pair]
            if h % 2:
                qh = pltpu.roll(qh, HEAD_DIM, 1)
                kh = pltpu.roll(kh, HEAD_DIM, 1)
            rest = cv[:, h:h + 1]
            q_tail = jnp.zeros((1, w), F32)
            k_tail = jnp.zeros((1, w), F32)
            for t in range(BIAS_TERMS):
                term = rest.astype(BF16).astype(F32)
                rest = rest - term
                q_tail = jnp.where(lane == HEAD_DIM + t, term, jnp.where(lane == HEAD_DIM + BIAS_TERMS + t, 1.0, q_tail))
                k_tail = jnp.where(lane == HEAD_DIM + t, 1.0, jnp.where(lane == HEAD_DIM + BIAS_TERMS + t, -term, k_tail))
            qa_ref[:, h * w:(h + 1) * w] = jnp.where(first, qh, q_tail).astype(BF16)
            ka_ref[:, h * w:(h + 1) * w] = jnp.where(first, kh, k_tail).astype(BF16)
        vb_ref[...] = v_ref[...].astype(BF16)

    col = lambda cb: pl.BlockSpec((ts, D_ATT), lambda i: (i, cb))
    wide = pl.BlockSpec((ts, N_HEADS * w), lambda i: (i, 0))
    return pl.pallas_call(
        body, name="attn_prep",
        out_shape=(jax.ShapeDtypeStruct((s, N_HEADS * w), BF16), jax.ShapeDtypeStruct((s, N_HEADS * w), BF16),
                   jax.ShapeDtypeStruct((s, D_ATT), BF16)),
        grid=(s // ts,), in_specs=[col(0), col(1), col(2), pl.BlockSpec((ts, N_HEADS), lambda i: (i, 0))],
        out_specs=(wide, wide, col(0)),
        compiler_params=_cparams("parallel"),
    )(z, z, z, c)


def _attn_fwd(qa, ka, vb, tq):
    s = qa.shape[0]
    nq = s // tq
    w = 2 * HEAD_DIM

    def body(q_ref, k_ref, v_ref, o_ref, lse_ref):
        i = pl.program_id(1)
        first = _head_lanes()
        q2 = q_ref[...]

        def block(j, carry, masked):
            off = pl.multiple_of(j * tq, tq)
            k2 = k_ref[pl.ds(off, tq), :]
            v2 = v_ref[pl.ds(off, tq), :]
            new = []
            for hh in range(2):
                m, l, acc = carry[hh]
                sc = lax.dot_general(q2[:, hh * w:(hh + 1) * w], k2[:, hh * w:(hh + 1) * w], NT,
                                     preferred_element_type=F32)
                if masked:
                    sc = jnp.where(_causal_mask(tq), sc, NEG_INF)
                m_new = jnp.maximum(m, jnp.max(sc, axis=-1, keepdims=True))
                alpha = jnp.exp(m - m_new)
                p = jnp.exp(sc - m_new)
                l = alpha * l + jnp.sum(p, axis=-1, keepdims=True)
                p_hi = p.astype(BF16)
                p_lo = (p - p_hi.astype(F32)).astype(BF16)
                acc = (alpha * acc + jnp.dot(p_hi, v2, preferred_element_type=F32)
                       + jnp.dot(p_lo, v2, preferred_element_type=F32))
                new.append((m_new, l, acc))
            return tuple(new)

        one = (jnp.full((tq, 1), NEG_INF, F32), jnp.zeros((tq, 1), F32), jnp.zeros((tq, w), F32))
        carry = lax.fori_loop(0, i, lambda j, c: block(j, c, False), (one, one))
        (m0, l0, a0), (m1, l1, a1) = block(i, carry, True)
        o_ref[...] = jnp.where(first, a0 / l0, a1 / l1)
        lse_ref[0] = m0 + jnp.log(l0)
        lse_ref[1] = m1 + jnp.log(l1)

    return pl.pallas_call(
        body, name="attn_fwd",
        out_shape=(jax.ShapeDtypeStruct((s, D_ATT), F32), jax.ShapeDtypeStruct((N_HEADS, s, 1), F32)),
        grid=(N_PAIRS, nq),
        in_specs=[pl.BlockSpec((tq, 2 * w), lambda hp, i: (i, hp)),
                  pl.BlockSpec((s, 2 * w), lambda hp, i: (0, hp)),
                  pl.BlockSpec((s, w), lambda hp, i: (0, hp))],
        out_specs=(pl.BlockSpec((tq, w), lambda hp, i: (i, hp)),
                   pl.BlockSpec((2, tq, 1), lambda hp, i: (hp, i, 0))),
        compiler_params=_cparams("parallel", "parallel"),
    )(qa, ka, vb)


def _attn_delta(o, do, tq):
    s = o.shape[0]
    w = 2 * HEAD_DIM

    def body(o_ref, do_ref, d_ref):
        first = _head_lanes()
        prod = o_ref[...] * do_ref[...].astype(F32)
        d_ref[0] = jnp.sum(_pick(prod, first, 0), axis=-1, keepdims=True)
        d_ref[1] = jnp.sum(_pick(prod, first, 1), axis=-1, keepdims=True)

    blk = pl.BlockSpec((tq, w), lambda hp, i: (i, hp))
    return pl.pallas_call(
        body, name="attn_delta", out_shape=jax.ShapeDtypeStruct((N_HEADS, s, 1), F32), grid=(N_PAIRS, s // tq),
        in_specs=[blk, blk], out_specs=pl.BlockSpec((2, tq, 1), lambda hp, i: (hp, i, 0)),
        compiler_params=_cparams("parallel", "parallel"),
    )(o, do)


def _attn_bwd(qa, ka, vb, do, lse, delta, tq):
    s = qa.shape[0]
    nq = s // tq
    w = 2 * HEAD_DIM

    def body(q_ref, do_ref, lse_ref, dl_ref, k_ref, v_ref, dq_ref, dk_ref, dv_ref, dc_ref, dq_acc):
        j = pl.program_id(1)
        first = _head_lanes()

        @pl.when(j == 0)
        def _():
            dq_acc[...] = jnp.zeros_like(dq_acc)

        k2 = k_ref[...]
        v2 = v_ref[...]

        def step(i, carry, masked):
            off = pl.multiple_of(i * tq, tq)
            rows = pl.ds(off, tq)
            q2 = q_ref[rows, :]
            do2 = do_ref[rows, :]
            new, dqs = [], []
            for hh in range(2):
                dk, dv, dcs = carry[hh]
                qh = q2[:, hh * w:(hh + 1) * w]
                kh = k2[:, hh * w:(hh + 1) * w]
                sc = lax.dot_general(qh, kh, NT, preferred_element_type=F32)
                if masked:
                    sc = jnp.where(_causal_mask(tq), sc, NEG_INF)
                p = jnp.exp(sc - lse_ref[hh, rows, :])
                dv = dv + lax.dot_general(do2, p.astype(BF16), TN, preferred_element_type=F32)
                dp = lax.dot_general(_pick(do2, first, hh), v2, NT, preferred_element_type=F32)
                ds = p * (dp - dl_ref[hh, rows, :])
                dsb = ds.astype(BF16)
                dk = dk + lax.dot_general(qh, dsb, TN, preferred_element_type=F32)
                dqs.append(jnp.dot(dsb, kh, preferred_element_type=F32))
                new.append((dk, dv, dcs + jnp.sum(ds, axis=0, keepdims=True)))
            dq_acc[rows, :] += jnp.where(first, dqs[0], pltpu.roll(dqs[1], HEAD_DIM, 1)) * ATT_SCALE
            return tuple(new)

        one = (jnp.zeros((w, tq), F32), jnp.zeros((w, tq), F32), jnp.zeros((1, tq), F32))
        carry = step(j, (one, one), True)
        (dk0, dv0, dc0), (dk1, dv1, dc1) = lax.fori_loop(j + 1, nq, lambda i, c: step(i, c, False), carry)
        dk_ref[...] = jnp.where(first, dk0.T, pltpu.roll(dk1.T, HEAD_DIM, 1)).astype(BF16)
        dv_ref[...] = jnp.where(first, dv0.T, dv1.T).astype(BF16)
        dc_ref[0, 0] = -dc0
        dc_ref[1, 0] = -dc1

        @pl.when(j == nq - 1)
        def _():
            dq_ref[...] = dq_acc[...].astype(BF16)

    whole = lambda width: pl.BlockSpec((s, width), lambda hp, j: (0, hp))
    whole_heads = pl.BlockSpec((2, s, 1), lambda hp, j: (hp, 0, 0))
    blk = lambda width: pl.BlockSpec((tq, width), lambda hp, j: (j, hp))
    crow = pl.BlockSpec((2, 1, 1, tq), lambda hp, j: (hp, j, 0, 0))
    return pl.pallas_call(
        body, name="attn_bwd",
        out_shape=(jax.ShapeDtypeStruct((s, D_ATT), BF16), jax.ShapeDtypeStruct((s, D_ATT), BF16),
                   jax.ShapeDtypeStruct((s, D_ATT), BF16), jax.ShapeDtypeStruct((N_HEADS, nq, 1, tq), F32)),
        grid=(N_PAIRS, nq),
        in_specs=[whole(2 * w), whole(w), whole_heads, whole_heads, blk(2 * w), blk(w)],
        out_specs=(whole(w), blk(w), blk(w), crow),
        scratch_shapes=[pltpu.VMEM((s, w), F32)],
        compiler_params=_cparams("parallel", "arbitrary"),
    )(qa, do, lse, delta, ka, vb)


def _pair_sums(x, first):
    total = jnp.sum(x, axis=-1, keepdims=True)
    head = jnp.sum(jnp.where(first, x, 0.0), axis=-1, keepdims=True)
    return head, total - head


def _pair_mean(x, first):
    head, tail = _pair_sums(x, first)
    return jnp.where(first, head, tail) * (1.0 / GROUP_DIM)


def _gm_pair_norm(v2, first):
    d = v2 - _pair_mean(v2, first)
    rstd = lax.rsqrt(_pair_mean(d * d, first) + EPS)
    return d * rstd, rstd


def _gm_pair_mix(w_ref, pr, rhs, first):
    return jnp.where(first, jnp.dot(w_ref[2 * pr], rhs, preferred_element_type=F32),
                     jnp.dot(w_ref[2 * pr + 1], rhs, preferred_element_type=F32))


def _gmlp_fwd(z, wt, bs_t, vgain):
    s = z.shape[0]

    def body(gu_ref, gv_ref, wt_ref, bs_ref, vg_ref, o_ref):
        first = _head_lanes()
        for pr in range(N_GROUPS // 2):
            sl = slice(2 * pr * GROUP_DIM, 2 * (pr + 1) * GROUP_DIM)
            vhat, _ = _gm_pair_norm(_gelu(gv_ref[:, sl]), first)
            vn = (vhat * vg_ref[:, sl]).astype(BF16)
            bias = jnp.where(first, bs_ref[:, 2 * pr:2 * pr + 1], bs_ref[:, 2 * pr + 1:2 * pr + 2])
            o_ref[:, sl] = _gelu(gu_ref[:, sl]) * (_gm_pair_mix(wt_ref, pr, vn, first) + bias)

    full = lambda a: pl.BlockSpec(a.shape, lambda n: (0,) * a.ndim)
    return pl.pallas_call(
        body, name="gmlp_fwd", out_shape=jax.ShapeDtypeStruct((s, D_GM), F32), grid=(s // CHUNK,),
        in_specs=[pl.BlockSpec((CHUNK, D_GM), lambda n: (n, 3)), pl.BlockSpec((CHUNK, D_GM), lambda n: (n, 4)),
                  full(wt), full(bs_t), full(vgain)],
        out_specs=pl.BlockSpec((CHUNK, D_GM), lambda n: (n, 0)),
        compiler_params=_cparams("parallel"),
    )(z, z, wt, bs_t, vgain)


def _gmlp_bwd(z, dgm, wt, wt_t, bs_t, vgain):
    s = z.shape[0]

    def body(gu_ref, gv_ref, dgm_ref, wt_ref, wtt_ref, bs_ref, vg_ref, dgu_ref, dgv_ref, dwt_ref, dbs_ref, dvg_ref):
        @pl.when(pl.program_id(0) == 0)
        def _():
            dwt_ref[...] = jnp.zeros_like(dwt_ref)
            dbs_ref[...] = jnp.zeros_like(dbs_ref)
            dvg_ref[...] = jnp.zeros_like(dvg_ref)

        first = _head_lanes()
        for pr in range(N_GROUPS // 2):
            g0, g1 = 2 * pr, 2 * pr + 1
            sl = slice(g0 * GROUP_DIM, (g1 + 1) * GROUP_DIM)
            gu = gu_ref[:, sl]
            gv = gv_ref[:, sl]
            dgm = dgm_ref[:, sl]
            vhat, rstd = _gm_pair_norm(_gelu(gv), first)
            gain = vg_ref[:, sl]
            vn = (vhat * gain).astype(BF16)
            bias = jnp.where(first, bs_ref[:, g0:g0 + 1], bs_ref[:, g1:g1 + 1])
            mixed = _gm_pair_mix(wt_ref, pr, vn, first) + bias
            dgu_ref[:, sl] = (dgm * mixed * _gelu_grad(gu)).astype(BF16)
            dmixed = dgm * _gelu(gu)
            db0, db1 = _pair_sums(dmixed, first)
            dbs_ref[:, g0:g0 + 1] += db0
            dbs_ref[:, g1:g1 + 1] += db1
            dwt_ref[g0] += lax.dot_general(_pick(dmixed, first, 0).astype(BF16), vn, NT, preferred_element_type=F32)
            dwt_ref[g1] += lax.dot_general(_pick(dmixed, first, 1).astype(BF16), vn, NT, preferred_element_type=F32)
            dvn = _gm_pair_mix(wtt_ref, pr, dmixed.astype(BF16), first)
            dvg_ref[:, sl] += _colsum(dvn * vhat)
            dvhat = dvn * gain
            dvf = rstd * (dvhat - _pair_mean(dvhat, first) - vhat * _pair_mean(dvhat * vhat, first))
            dgv_ref[:, sl] = (dvf * _gelu_grad(gv)).astype(BF16)

    full = lambda a: pl.BlockSpec(a.shape, lambda n: (0,) * a.ndim)
    chunk = pl.BlockSpec((CHUNK, D_GM), lambda n: (n, 0))
    return pl.pallas_call(
        body, name="gmlp_bwd",
        out_shape=(jax.ShapeDtypeStruct((s, D_GM), BF16), jax.ShapeDtypeStruct((s, D_GM), BF16),
                   jax.ShapeDtypeStruct(wt.shape, F32), jax.ShapeDtypeStruct(bs_t.shape, F32),
                   jax.ShapeDtypeStruct(vgain.shape, F32)),
        grid=(s // CHUNK,),
        in_specs=[pl.BlockSpec((CHUNK, D_GM), lambda n: (n, 3)), pl.BlockSpec((CHUNK, D_GM), lambda n: (n, 4)),
                  chunk, full(wt), full(wt_t), full(bs_t), full(vgain)],
        out_specs=(chunk, chunk, full(wt), full(bs_t), full(vgain)),
        compiler_params=_cparams("arbitrary"),
    )(z, z, dgm, wt, wt_t, bs_t, vgain)


def _dz_concat(wide, dfl, ts):
    s = dfl.shape[0]
    ts = _tile(s, ts, 16)
    n = len(wide)

    def body(*refs):
        o_ref = refs[-1]
        for k in range(n):
            o_ref[:, k * D_ATT:(k + 1) * D_ATT] = refs[k][...]
        o_ref[:, F_OFF:] = jnp.zeros((ts, D_IN_PAD - F_OFF), BF16)
        o_ref[:, F_OFF:F_OFF + N_HEADS] = refs[n][...].astype(BF16)

    return pl.pallas_call(
        body, name="dz_concat", out_shape=jax.ShapeDtypeStruct((s, D_IN_PAD), BF16), grid=(s // ts,),
        in_specs=[pl.BlockSpec((ts, D_ATT), lambda i: (i, 0))] * n + [pl.BlockSpec((ts, N_HEADS), lambda i: (i, 0))],
        out_specs=pl.BlockSpec((ts, D_IN_PAD), lambda i: (i, 0)),
        compiler_params=_cparams("parallel"),
    )(*wide, dfl)


def _layer_fwd(h0, p_i, w, tq, late):
    s, d = h0.shape
    nq = s // tq
    sv = {"h0": h0}

    (hn1,) = _rowwise(lambda h, g: _rms(h)[0] * g, [h0], [w["mix_pre_norm"]], [(d, BF16)], [], "pre_mix", ROW_TILE)
    z = _mm(hn1, w["w_in"], "nn", F32, "mm_in")
    fl_t = z[:, F_OFF:F_OFF + N_HEADS].T
    c_t = _forget_fwd(fl_t, w["b_forget"])
    qa, ka, vb = _attn_prep(z, c_t.T, ROW_TILE)
    att, lse = _attn_fwd(qa, ka, vb, tq)
    gm = _gmlp_fwd(z, w["wt"], w["bs_t"], w["gm_v_norm"])
    w = dict(w, **late(att))

    def mix_out(att, gm, g):
        return jnp.concatenate([_rms(att)[0] * g[:, :D_ATT], _rms(gm)[0] * g[:, D_ATT:]], axis=-1)

    (mc,) = _rowwise(mix_out, [att, gm], [w["mix_out_norm"]], [(D_ATT + D_GM, BF16)], [], "mix_out", ROW_TILE)
    y1 = _mm(mc, w["w_out"], "nn", F32, "mm_out")

    def post_mix(h0, y1, gpost, gpre):
        h1 = h0 + _rms(y1)[0] * gpost
        return h1, _rms(h1)[0] * gpre

    h1, hn2 = _rowwise(post_mix, [h0, y1], [w["mix_post_norm"], w["ffn_pre_norm"]],
                       [(d, F32), (d, BF16)], [], "post_mix", ROW_TILE)
    ab, t = _ffn_in_swiglu(hn2, w["w_ffn_in"])
    y2 = _mm(t, w["w_ffn_out"], "nn", F32, "mm_ffn_out", a3="k", b3="k")

    def post_ffn(h1, y2, g):
        h2 = h1 + _rms(y2)[0] * g
        return h2, _rms(h2)[0]

    h2, hr = _rowwise(post_ffn, [h1, y2], [w["ffn_post_norm"]], [(d, F32), (d, BF16)], [], "post_ffn", ROW_TILE)
    gl = _mm(hr, w["w_ple_gate"], "nn", F32, "mm_gate")
    pe = _mm(p_i, w["w_ple"], "nn", F32, "mm_ple")
    (h3,) = _rowwise(lambda h2, gl, pe, g: h2 + _sigmoid(gl) * (_rms(pe)[0] * g), [h2, gl, pe], [w["ple_norm"]],
                     [(d, F32)], [], "ple_out", ROW_TILE)
    sv.update(hn1=hn1, z=z, fl_t=fl_t, qa=qa, ka=ka, vb=vb, lse=lse, att=att, gm=gm,
              mc=mc, y1=y1, h1=h1, hn2=hn2, ab=ab, y2=y2, h2=h2, hr=hr, gl=gl, pe=pe, p_i=p_i)
    return h3, sv


def _layer_bwd(dh3, sv, w, tq, mid):
    s, d = dh3.shape
    g = {}
    by_rows = lambda a: a.reshape(N_DEV, -1, a.shape[-1])
    by_cols = lambda a: jnp.stack(jnp.split(a, N_DEV, axis=-1))

    def ple_bwd(dh3, gl, pe, gple):
        gate = _sigmoid(gl)
        pehat, rpe = _rms(pe)
        dgl = dh3 * (pehat * gple) * gate * (1.0 - gate)
        de = dh3 * gate
        return dgl, _rms_bwd(pehat, rpe, de * gple), _colsum(de * pehat)

    dgl, dpe, g["ple_norm"] = _rowwise(ple_bwd, [dh3, sv["gl"], sv["pe"]], [w["ple_norm"]],
                                       [(d, BF16), (d, BF16)], [d], "ple_bwd", ROW_TILE)
    g["w_ple_gate"] = by_rows(_mm(sv["hr"], dgl, "tn", BF16, "mm_dgate"))
    dhr = _mm(dgl, w["w_ple_gate"], "nt", BF16, "mm_dhr")
    g["w_ple"] = by_cols(_mm(sv["p_i"], dpe, "tn", BF16, "mm_dple"))

    def ffn_post_bwd(dh3, dhr, h2, y2, gpost):
        h2hat, r2 = _rms(h2)
        dh2 = dh3 + _rms_bwd(h2hat, r2, dhr)
        y2hat, ry = _rms(y2)
        return dh2, _rms_bwd(y2hat, ry, dh2 * gpost), _colsum(dh2 * y2hat)

    dh2, dy2, g["ffn_post_norm"] = _rowwise(ffn_post_bwd, [dh3, dhr, sv["h2"], sv["y2"]], [w["ffn_post_norm"]],
                                            [(d, F32), (d, BF16)], [d], "ffn_post_bwd", ROW_TILE)
    t, dab = _ffn_out_swiglu_bwd(dy2, w["w_ffn_out"], sv["ab"])
    dab = dab.reshape((N_DEV,) + dab.shape[2:])
    g["w_ffn_out"] = by_rows(_mm(t, dy2, "tn", BF16, "mm_dffn_out", a3="m", o3="m"))
    dhn2 = _mm(dab, w["w_ffn_in"], "nn", BF16, "mm_dhn2", a3="k", b3="k")
    g["w_ffn_in"] = _mm(dab, sv["hn2"], "tn", BF16, "mm_dffn_in", a3="m", o3="m")

    def mix_post_bwd(dh2, dhn2, h1, y1, gpre, gpost):
        h1hat, r1 = _rms(h1)
        dh1 = dh2 + _rms_bwd(h1hat, r1, dhn2 * gpre)
        y1hat, ry = _rms(y1)
        return dh1, _rms_bwd(y1hat, ry, dh1 * gpost), _colsum(dhn2 * h1hat), _colsum(dh1 * y1hat)

    dh1, dy1, g["ffn_pre_norm"], g["mix_post_norm"] = _rowwise(
        mix_post_bwd, [dh2, dhn2, sv["h1"], sv["y1"]], [w["ffn_pre_norm"], w["mix_post_norm"]],
        [(d, F32), (d, BF16)], [d, d], "mix_post_bwd", ROW_TILE)
    dmc = _mm(dy1, w["w_out"], "nt", BF16, "mm_dmc")
    g["w_out"] = by_rows(_mm(sv["mc"], dy1, "tn", BF16, "mm_dout"))
    w = dict(w, **mid(g, dmc))

    def mix_out_bwd(da, dg, att, gm, gain):
        atthat, ra = _rms(att)
        gmhat, rg = _rms(gm)
        dgain = jnp.concatenate([_colsum(da * atthat), _colsum(dg * gmhat)], axis=-1)
        return _rms_bwd(atthat, ra, da * gain[:, :D_ATT]), _rms_bwd(gmhat, rg, dg * gain[:, D_ATT:]), dgain

    datt, dgm, g["mix_out_norm"] = _rowwise(
        mix_out_bwd, [(dmc, 0, D_ATT), (dmc, 1, D_GM), sv["att"], sv["gm"]], [w["mix_out_norm"]],
        [(D_ATT, BF16), (D_GM, F32)], [D_ATT + D_GM], "mix_out_bwd", ROW_TILE)

    dgu, dgv, dwt, dbs_t, g["gm_v_norm"] = _gmlp_bwd(sv["z"], dgm, w["wt"], w["wt_t"], w["bs_t"], w["gm_v_norm"])
    g["gm_w_s"] = dwt * jnp.tril(jnp.ones((CHUNK, CHUNK), F32))[None]
    g["gm_b_s"] = dbs_t.T

    delta = _attn_delta(sv["att"], datt, tq)
    dq, dk, dv, dc_row = _attn_bwd(sv["qa"], sv["ka"], sv["vb"], datt, sv["lse"], delta, tq)
    dfl_t, db = _forget_bwd(dc_row.reshape(N_HEADS, s), sv["fl_t"], w["b_forget"])
    g["b_forget"] = db.reshape(1, N_HEADS)
    dz = _dz_concat([dq, dk, dv, dgu, dgv], dfl_t.T, ROW_TILE)
    dhn1 = _mm(dz, w["w_in"], "nt", BF16, "mm_dhn1")
    g["w_in"] = _w_in_split(_mm(sv["hn1"], dz, "tn", BF16, "mm_din"))

    def mix_pre_bwd(dh1, dhn1, h0, gpre):
        h0hat, r0 = _rms(h0)
        return dh1 + _rms_bwd(h0hat, r0, dhn1 * gpre), _colsum(dhn1 * h0hat)

    dh0, g["mix_pre_norm"] = _rowwise(mix_pre_bwd, [dh1, dhn1, sv["h0"]], [w["mix_pre_norm"]],
                                      [(d, F32)], [d], "mix_pre_bwd", ROW_TILE)
    return dh0, g


ANY = pl.BlockSpec(memory_space=pl.ANY)


def _all_gather(xs, layer, name):
    n = len(xs)

    def body(*refs):
        x_refs, out_refs = refs[:n], refs[n:2 * n]
        send_sems, recv_sems, local_sems = refs[2 * n:]
        x, y, c = lax.axis_index("x"), lax.axis_index("y"), lax.axis_index("c")
        me, sibling = (x, y, c), (x, y, 1 - c)
        chips = [(1 - x, y), (x, 1 - y), (1 - x, 1 - y)]

        def shard(a):
            return x_refs[a] if layer is None else x_refs[a].at[layer]

        def rows(a, px, py, pc):
            return out_refs[a].at[4 * px + 2 * py + pc]

        def copy(a, kk, block, to, from_shard=False):
            return pltpu.make_async_remote_copy(
                src_ref=shard(a) if from_shard else rows(a, *block), dst_ref=rows(a, *block),
                send_sem=send_sems.at[7 * a + kk], recv_sem=recv_sems.at[7 * a + kk],
                device_id=to, device_id_type=MESH)

        mine = [pltpu.make_async_copy(shard(a), rows(a, *me), local_sems.at[a]) for a in range(n)]
        for cp in mine:
            cp.start()
        first = []
        for a in range(n):
            first.append(copy(a, 0, me, sibling, from_shard=True))
            first += [copy(a, 1 + j, me, (*chip, c), from_shard=True) for j, chip in enumerate(chips)]
        for cp in first:
            cp.start()
        passed = []
        for j, chip in enumerate(chips):
            for a in range(n):
                copy(a, 1 + j, (*chip, c), me).wait_recv()
                passed.append(copy(a, 4 + j, (*chip, c), sibling))
                passed[-1].start()
        for a in range(n):
            copy(a, 0, sibling, me).wait_recv()
        for j, chip in enumerate(chips):
            for a in range(n):
                copy(a, 4 + j, (*chip, 1 - c), me).wait_recv()
        for cp in first + passed:
            cp.wait_send()
        for cp in mine:
            cp.wait()

    shapes = [x.shape if layer is None else x.shape[1:] for x in xs]
    return pl.pallas_call(
        body, name=name, out_shape=[jax.ShapeDtypeStruct((N_DEV,) + sh, x.dtype) for sh, x in zip(shapes, xs)],
        in_specs=[ANY] * n, out_specs=[ANY] * n,
        scratch_shapes=[pltpu.SemaphoreType.DMA((7 * n,)), pltpu.SemaphoreType.DMA((7 * n,)),
                        pltpu.SemaphoreType.DMA((n,))],
    )(*xs)


HBM = pl.BlockSpec(memory_space=pltpu.HBM)
SEMS = pl.BlockSpec(memory_space=pltpu.SEMAPHORE)
EFFECT = pltpu.SideEffectType.DATAFLOW_SIDE_EFFECTING
FLIPS = tuple((fx, fy, fc) for fx in (0, 1) for fy in (0, 1) for fc in (0, 1))[1:]


def _exchange_copies(src_refs, land_refs, send_sems, recv_sems, layer, scatter):
    x, y, c = lax.axis_index("x"), lax.axis_index("y"), lax.axis_index("c")
    me = 4 * x + 2 * y + c
    copies = []
    for a, (src, land) in enumerate(zip(src_refs, land_refs)):
        for f, (fx, fy, fc) in enumerate(FLIPS):
            px, py, pc = (1 - x if fx else x), (1 - y if fy else y), (1 - c if fc else c)
            if scatter:
                block = src.at[4 * px + 2 * py + pc]
            else:
                block = src if layer is None else src.at[layer]
            copies.append(pltpu.make_async_remote_copy(
                src_ref=block, dst_ref=land.at[me], send_sem=send_sems.at[7 * a + f], recv_sem=recv_sems.at[7 * a + f],
                device_id=(px, py, pc), device_id_type=MESH))
    return copies


def _exchange_start(srcs, lands, layer, scatter, name):
    n = len(srcs)

    def body(*refs):
        for cp in _exchange_copies(refs[:n], refs[n:2 * n], refs[2 * n], refs[2 * n + 1], layer, scatter):
            cp.start()
        token = refs[-1]
        token[...] = jnp.zeros_like(token)

    operands = list(srcs) + list(lands)
    outs = pl.pallas_call(
        body, name=name,
        out_shape=(pltpu.SemaphoreType.DMA((7 * n,)), pltpu.SemaphoreType.DMA((7 * n,)),
                   *[pltpu.HBM(a.shape, a.dtype) for a in operands], jax.ShapeDtypeStruct((8, LANE), F32)),
        in_specs=[HBM] * (2 * n),
        out_specs=(SEMS, SEMS, *[HBM] * (2 * n), pl.BlockSpec(memory_space=pltpu.VMEM)),
        input_output_aliases={i: 2 + i for i in range(2 * n)},
        compiler_params=pltpu.CompilerParams(has_side_effects=EFFECT),
    )(*[pltpu.with_memory_space_constraint(a, pltpu.HBM) for a in operands])
    return outs[0], outs[1], outs[2:2 + n], outs[2 + n:2 + 2 * n], outs[-1]


def _exchange_wait(started, after, layer, scatter, name):
    send_sems, recv_sems, srcs, lands, _ = started
    n = len(srcs)

    def body(*refs):
        for cp in _exchange_copies(refs[:n], refs[n:2 * n], refs[2 * n], refs[2 * n + 1], layer, scatter):
            cp.wait_send()
            cp.wait_recv()

    operands = list(srcs) + list(lands)
    outs = pl.pallas_call(
        body, name=name, out_shape=tuple(pltpu.HBM(a.shape, a.dtype) for a in operands),
        in_specs=[HBM] * (2 * n) + [SEMS, SEMS, ANY], out_specs=[HBM] * (2 * n),
        input_output_aliases={i: i for i in range(2 * n)},
        compiler_params=pltpu.CompilerParams(has_side_effects=EFFECT),
    )(*operands, send_sems, recv_sems, after)
    return outs[:n], outs[n:]


def _sum_devices(parts):
    _, r, c = parts.shape

    def body(p_ref, o_ref):
        acc = p_ref[0].astype(F32)
        for j in range(1, N_DEV):
            acc = acc + p_ref[j].astype(F32)
        o_ref[...] = acc

    return pl.pallas_call(
        body, name="small_sum", out_shape=jax.ShapeDtypeStruct((r, c), F32), grid=(r // SMALL_ROWS,),
        in_specs=[pl.BlockSpec((N_DEV, SMALL_ROWS, c), lambda i: (0, i, 0))],
        out_specs=pl.BlockSpec((SMALL_ROWS, c), lambda i: (i, 0)),
        compiler_params=_cparams("parallel"),
    )(parts)


def _adamw_math(w, g, m, v):
    m = ADAM_B1 * m + (1.0 - ADAM_B1) * g
    v = ADAM_B2 * v + (1.0 - ADAM_B2) * (g * g)
    m_hat = m / (1.0 - ADAM_B1 ** ADAM_STEP)
    v_hat = v / (1.0 - ADAM_B2 ** ADAM_STEP)
    return -ADAM_LR * (m_hat / (jnp.sqrt(v_hat) + ADAM_EPS) + ADAM_WD * w), m, v


def _adamw_shard(w, m, v, parts, layer, outs, name):
    _, a, b = w.shape
    ta = _tile(a, 256, 16)
    if outs is None:
        outs = [lax.empty(w.shape, F32) for _ in range(4)]

    def body(w_ref, m_ref, v_ref, p_ref, *refs):
        g_ref, d_ref, nm_ref, nv_ref = refs[4:]
        g = p_ref[0].astype(F32)
        for j in range(1, N_DEV):
            g = g + p_ref[j].astype(F32)
        g_ref[0] = g
        d_ref[0], nm_ref[0], nv_ref[0] = _adamw_math(w_ref[0], g, m_ref[0], v_ref[0])

    mine = pl.BlockSpec((1, ta, b), lambda i: (layer, i, 0))
    return pl.pallas_call(
        body, name=name, out_shape=[jax.ShapeDtypeStruct(w.shape, F32)] * 4, grid=(a // ta,),
        in_specs=[mine, mine, mine, pl.BlockSpec((N_DEV, ta, b), lambda i: (0, i, 0))] + [ANY] * 4,
        out_specs=[mine] * 4, input_output_aliases={4 + k: k for k in range(4)},
        compiler_params=_cparams("parallel"),
    )(w, m, v, parts, *outs)


def _w_in_moves():
    n = D_IN // N_DEV
    runs = ((0, 3 * D_ATT, 0), (3 * D_ATT, 3 * D_ATT + N_HEADS, F_OFF), (3 * D_ATT + N_HEADS, D_IN, 3 * D_ATT))
    moves = []
    for j in range(N_DEV):
        for lo, hi, padded in runs:
            a, b = max(n * j, lo), min(n * (j + 1), hi)
            if a < b:
                moves.append((j, a - n * j, padded + a - lo, b - a))
    return moves


def _w_in_assemble(blocks):
    _, d, n = blocks.shape
    tr = _tile(d, 256, 16)

    def body(b_ref, o_ref):
        o_ref[:, D_IN:] = jnp.zeros((tr, D_IN_PAD - D_IN), o_ref.dtype)
        for j, col, padded, width in _w_in_moves():
            o_ref[:, padded:padded + width] = b_ref[j, :, col:col + width]

    return pl.pallas_call(
        body, name="w_in_assemble", out_shape=jax.ShapeDtypeStruct((d, D_IN_PAD), blocks.dtype), grid=(d // tr,),
        in_specs=[pl.BlockSpec((N_DEV, tr, n), lambda i: (0, i, 0))],
        out_specs=pl.BlockSpec((tr, D_IN_PAD), lambda i: (i, 0)),
        compiler_params=_cparams("parallel"),
    )(blocks)


def _w_in_split(padded):
    d = padded.shape[0]
    n = D_IN // N_DEV
    tr = _tile(d, 256, 16)

    def body(p_ref, o_ref):
        for j, col, src, width in _w_in_moves():
            o_ref[j, :, col:col + width] = p_ref[:, src:src + width]

    return pl.pallas_call(
        body, name="w_in_split", out_shape=jax.ShapeDtypeStruct((N_DEV, d, n), padded.dtype), grid=(d // tr,),
        in_specs=[pl.BlockSpec((tr, D_IN_PAD), lambda i: (i, 0))],
        out_specs=pl.BlockSpec((N_DEV, tr, n), lambda i: (0, i, 0)),
        compiler_params=_cparams("parallel"),
    )(padded)


def _small_rows(size):
    return -(-size // (8 * SMALL_COLS)) * 8


def _pack_small(pieces):
    rows = []
    for p in pieces:
        flat = p.reshape(-1)
        rows.append(jnp.pad(flat, (0, _small_rows(flat.shape[0]) * SMALL_COLS - flat.shape[0])).reshape(-1, SMALL_COLS))
    used = sum(r.shape[0] for r in rows)
    rows.append(jnp.zeros((-used % SMALL_ROWS, SMALL_COLS), F32))
    return jnp.concatenate(rows)


def kernel(x, p, mix_pre_norm, mix_post_norm, w_in, b_forget, gm_v_norm, gm_w_s, gm_b_s, mix_out_norm, w_out, ffn_pre_norm, ffn_post_norm, w_ffn_in, w_ffn_out, w_ple, ple_norm, w_ple_gate, loss_target, m_mix_pre_norm, m_mix_post_norm, m_w_in, m_b_forget, m_gm_v_norm, m_gm_w_s, m_gm_b_s, m_mix_out_norm, m_w_out, m_ffn_pre_norm, m_ffn_post_norm, m_w_ffn_in, m_w_ffn_out, m_w_ple, m_ple_norm, m_w_ple_gate, v_mix_pre_norm, v_mix_post_norm, v_w_in, v_b_forget, v_gm_v_norm, v_gm_w_s, v_gm_b_s, v_mix_out_norm, v_w_out, v_ffn_pre_norm, v_ffn_post_norm, v_w_ffn_in, v_w_ffn_out, v_w_ple, v_ple_norm, v_w_ple_gate):
    given = dict(locals())
    weights = {n: given[n] for n in WEIGHT_ORDER}
    mom_m = {n: given["m_" + n] for n in WEIGHT_ORDER}
    mom_v = {n: given["v_" + n] for n in WEIGHT_ORDER}
    depth = w_in.shape[0]
    s, d = x.shape[1], x.shape[2]
    tq = _tile(s, ATT_BLOCK)
    me = 4 * lax.axis_index("x") + 2 * lax.axis_index("y") + lax.axis_index("c")
    tril = jnp.tril(jnp.ones((CHUNK, CHUNK), F32))

    def landing(block):
        return lax.dynamic_update_index_in_dim(lax.empty((N_DEV,) + block.shape, block.dtype), block, me, 0)

    def mix_weights(i, got):
        wt = gm_w_s[i] * tril[None]
        lw = dict(
            w_in=_w_in_assemble(got["w_in"]),
            b_forget=b_forget[i][:, None], wt=wt.astype(BF16), wt_t=wt.transpose(0, 2, 1).astype(BF16),
            bs_t=gm_b_s[i].T)
        lw.update({n: weights[n][i][None] for n in ("mix_pre_norm", "mix_post_norm", "gm_v_norm", "mix_out_norm",
                                                    "ffn_pre_norm", "ffn_post_norm", "ple_norm")})
        return lw

    def rest_weights(got):
        return dict(w_out=got["w_out"].reshape(-1, d), w_ffn_in=got["w_ffn_in"],
                    w_ffn_out=got["w_ffn_out"].reshape(N_DEV // 2, -1, d),
                    w_ple=jnp.concatenate([got["w_ple"][j] for j in range(N_DEV)], axis=-1),
                    w_ple_gate=got["w_ple_gate"].reshape(-1, d))

    def shard_view(n, a):
        return jnp.transpose(a, (0, 2, 1)) if n in TRANSPOSED_WEIGHTS else a

    shards = {n: shard_view(n, weights[n].astype(BF16)) for n in MATRIX_WEIGHTS}

    def gather_start(i):
        started = {}
        order = jnp.zeros((), BF16)
        for tag, grp in EXCHANGE_GROUPS.items():
            started[tag] = _exchange_start([shards[n] for n in grp], [landing(shards[n][i] + order) for n in grp], i,
                                           False, f"weights_gather_start_{i}_{tag}")
            order = started[tag][4][0, 0].astype(BF16)
        return started

    def gather_finish(i, tag, pending, after):
        srcs, got = _exchange_wait(pending[tag], after, i, False, f"weights_gather_wait_{i}_{tag}")
        shards.update(zip(EXCHANGE_GROUPS[tag], srcs))
        return dict(zip(EXCHANGE_GROUPS[tag], got))

    h = x[0]
    saved, layer_w = [], []
    pending = gather_start(0)
    for i in range(depth):
        lw = mix_weights(i, gather_finish(i, "mix", pending, h))
        if i == 0:
            lw["mix_pre_norm"] = lw["mix_pre_norm"] + pending["rest"][4][:1, :1]
        following = {}

        def late(att, i=i, pending=pending, lw=lw, following=following):
            rest = rest_weights(gather_finish(i, "rest", pending, att))
            lw.update(rest)
            if i + 1 == depth:
                return rest
            following.update(gather_start(i + 1))
            token = following["mix"][4][:1, :1] + following["rest"][4][:1, :1]
            return dict(rest, mix_out_norm=lw["mix_out_norm"] + token)

        h, sv = _layer_fwd(h, p[i, 0], lw, tq, late)
        layer_w.append(lw)
        saved.append(sv)
        pending = following

    def loss_head(y, t):
        err = y - t
        return err * (1.0 / d), _colsum(err * err)

    dh, sq = _rowwise(loss_head, [h, loss_target[0]], [], [(d, F32)], [d], "loss_head", ROW_TILE)
    loss = lax.psum(0.5 * jnp.sum(sq) / d, AXES)

    layer_g = [None] * depth
    shard_out = {n: None for n in MATRIX_WEIGHTS}

    def scatter_start(i, tag, g):
        full_g = [g[n] for n in EXCHANGE_GROUPS[tag]]
        lands = [landing(lax.dynamic_index_in_dim(gf, me, 0, keepdims=False)) for gf in full_g]
        return _exchange_start(full_g, lands, None, True, f"grads_scatter_start_{i}_{tag}")

    def scatter_finish(i, tag, started, after):
        _, parts = _exchange_wait(started[tag], after, None, True, f"grads_scatter_wait_{i}_{tag}")
        for n, part in zip(EXCHANGE_GROUPS[tag], parts):
            shard_out[n] = _adamw_shard(shard_view(n, weights[n]), shard_view(n, mom_m[n]), shard_view(n, mom_v[n]),
                                        part, i, shard_out[n], "adamw_" + n)

    before = None
    for i in reversed(range(depth)):
        lw = layer_w[i]
        if before is not None:
            lw = dict(lw, ple_norm=lw["ple_norm"] + before[1]["mix"][4][:1, :1])
        started = {}

        def mid(g, dmc, i=i, before=before, started=started, lw=lw):
            if before is not None:
                scatter_finish(before[0], "rest", before[1], dmc)
            started["rest"] = scatter_start(i, "rest", g)
            return dict(mix_out_norm=lw["mix_out_norm"] + started["rest"][4][:1, :1])

        dh, layer_g[i] = _layer_bwd(dh, saved[i], lw, tq, mid)
        if before is not None:
            scatter_finish(before[0], "mix", before[1], dh)
        started["mix"] = scatter_start(i, "mix", layer_g[i])
        before = (i, started)
    scatter_finish(before[0], "rest", before[1], before[1]["mix"][4])
    scatter_finish(before[0], "mix", before[1], dh)
    grad_x = dh[None]

    grads, deltas, new_m, new_v = {}, {}, {}, {}
    for n in MATRIX_WEIGHTS:
        grads[n], deltas[n], new_m[n], new_v[n] = (shard_view(n, a) for a in shard_out[n])

    small_g = _pack_small([jnp.stack([layer_g[i][n].reshape(-1) for i in range(depth)]) for n in SMALL_WEIGHTS])
    (gathered,) = _all_gather([small_g.astype(BF16)], None, "small_grads_all_gather")
    g_small = _sum_devices(gathered)
    pack = lambda t: _pack_small([t[n] for n in SMALL_WEIGHTS])
    dl, nm, nv = _rowwise(_adamw_math, [pack(weights), g_small, pack(mom_m), pack(mom_v)], [],
                          [(SMALL_COLS, F32)] * 3, [], "adamw_small", SMALL_ROWS)
    row = 0
    for n in SMALL_WEIGHTS:
        shp, size = weights[n].shape, weights[n].size
        grads[n], deltas[n], new_m[n], new_v[n] = (
            a[row:row + _small_rows(size)].reshape(-1)[:size].reshape(shp) for a in (g_small, dl, nm, nv))
        row += _small_rows(size)

    return (loss, grad_x, *[grads[n] for n in WEIGHT_ORDER], *[deltas[n] for n in WEIGHT_ORDER],
            *[new_m[n] for n in WEIGHT_ORDER], *[new_v[n] for n in WEIGHT_ORDER])
```

```python
import functools
import math

import jax
import jax.numpy as jnp
from jax import lax
from jax.experimental import pallas as pl
from jax.experimental.pallas import tpu as pltpu

F32 = jnp.float32
BF16 = jnp.bfloat16
MESH = pl.DeviceIdType.MESH
AXES = ("x", "y", "c")
N_DEV = 8

EPS = 1e-6
NEG_INF = -1e30
N_HEADS = 8
HEAD_DIM = 64
D_ATT = N_HEADS * HEAD_DIM
N_GROUPS = 8
GROUP_DIM = 64
D_GM = N_GROUPS * GROUP_DIM
CHUNK = 128
ATT_SCALE = HEAD_DIM ** -0.5
ATT_BLOCK = 1024
D_IN = 3 * D_ATT + N_HEADS + 2 * D_GM
D_IN_PAD = 3 * D_ATT + 2 * D_GM + 128
F_OFF = 3 * D_ATT + 2 * D_GM

ADAM_LR = 0.001
ADAM_B1 = 0.9
ADAM_B2 = 0.999
ADAM_EPS = 1e-08
ADAM_WD = 0.01
ADAM_STEP = 10

LANE = 128
VMEM_LIMIT = 48 * 1024 * 1024
ROW_TILE = 512
K_TILE = 4096
SMALL_COLS = 128
SMALL_ROWS = 512

MATRIX_WEIGHTS = ("w_in", "w_out", "w_ffn_in", "w_ffn_out", "w_ple", "w_ple_gate")
EXCHANGE_GROUPS = {"mix": ("w_in",), "rest": ("w_out", "w_ffn_in", "w_ffn_out", "w_ple", "w_ple_gate")}
TRANSPOSED_WEIGHTS = ("w_ffn_in",)
SMALL_WEIGHTS = ("mix_pre_norm", "mix_post_norm", "b_forget", "gm_v_norm", "gm_w_s", "gm_b_s",
                 "mix_out_norm", "ffn_pre_norm", "ffn_post_norm", "ple_norm")
WEIGHT_ORDER = ("mix_pre_norm", "mix_post_norm", "w_in", "b_forget", "gm_v_norm", "gm_w_s", "gm_b_s",
                "mix_out_norm", "w_out", "ffn_pre_norm", "ffn_post_norm", "w_ffn_in", "w_ffn_out",
                "w_ple", "ple_norm", "w_ple_gate")


def _tile(n, pref, unit=LANE):
    best = None
    t = unit
    while t <= min(n, pref):
        if n % t == 0:
            best = t
        t += unit
    return n if best is None else best


def _cparams(*semantics):
    return pltpu.CompilerParams(dimension_semantics=semantics or None, vmem_limit_bytes=VMEM_LIMIT)


NN = (((1,), (0,)), ((), ()))
NT = (((1,), (1,)), ((), ()))
TN = (((0,), (0,)), ((), ()))
_MM_AXES = {
    "nn": ("i", "k", "k", "j"), "nt": ("i", "k", "j", "k"), "tn": ("k", "i", "k", "j")}
_MM_DN = {"nn": NN, "nt": NT, "tn": TN}


def _mm(a, b, dims, out_dtype, name, a3=None, b3=None, o3=None, tm=1024, tn=1024, tk=None):
    ar, ac, br, bc = _MM_AXES[dims]
    letter = {"i": "m", "j": "n", "k": "k"}
    size = {}

    def measure(x, rows, cols, stacked):
        shape = x.shape
        if stacked is None:
            size.setdefault(letter[rows], shape[0])
            size.setdefault(letter[cols], shape[1])
        else:
            for ax, n in ((rows, shape[1]), (cols, shape[2])):
                size.setdefault(letter[ax], n * shape[0] if letter[ax] == stacked else n)

    measure(a, ar, ac, a3)
    measure(b, br, bc, b3)
    m, n, k = size["m"], size["n"], size["k"]
    slab = {}
    for x, stacked, rows, cols in ((a, a3, ar, ac), (b, b3, br, bc)):
        if stacked is not None:
            slab[stacked] = x.shape[1] if letter[rows] == stacked else x.shape[2]
    if o3 is not None:
        slab.setdefault(o3, slab.get(o3, None) or {"m": m, "n": n}[o3] // N_DEV)
    tk = tk or K_TILE
    tile = {"m": slab.get("m") or _tile(m, tm), "n": slab.get("n") or _tile(n, tn), "k": slab.get("k") or _tile(k, tk)}
    group = 1
    if a3 == "k" and b3 == "k":
        group = max(g for g in range(1, a.shape[0] + 1) if a.shape[0] % g == 0 and g * tile["k"] <= max(tk, tile["k"]))
    nk = k // (group * tile["k"])

    def spec(rows, cols, stacked):
        tr, tc = tile[letter[rows]], tile[letter[cols]]
        if stacked is None:
            return pl.BlockSpec((tr, tc), lambda i, j, kk: ({"i": i, "j": j, "k": kk}[rows], {"i": i, "j": j, "k": kk}[cols]))

        def imap(i, j, kk):
            g = {"i": i, "j": j, "k": kk}
            return (g[{"m": "i", "n": "j", "k": "k"}[stacked]],
                    0 if letter[rows] == stacked else g[rows], 0 if letter[cols] == stacked else g[cols])

        return pl.BlockSpec((group if stacked == "k" else 1, tr, tc), imap)

    dn = _MM_DN[dims]

    def body(a_ref, b_ref, o_ref, *acc):
        prod = None
        for g in range(group):
            av = a_ref[...] if a3 is None else a_ref[g]
            bv = b_ref[...] if b3 is None else b_ref[g]
            term = lax.dot_general(av.astype(BF16), bv.astype(BF16), dn, preferred_element_type=F32)
            prod = term if prod is None else prod + term

        def emit(val):
            if o3 is None:
                o_ref[...] = val.astype(out_dtype)
            else:
                o_ref[0] = val.astype(out_dtype)

        if nk == 1:
            emit(prod)
            return
        (acc_ref,) = acc
        kk = pl.program_id(2)

        @pl.when(kk == 0)
        def _():
            acc_ref[...] = prod

        @pl.when(kk > 0)
        def _():
            acc_ref[...] += prod

        @pl.when(kk == nk - 1)
        def _():
            emit(acc_ref[...])

    if o3 is None:
        out_shape = (m, n)
    elif o3 == "m":
        out_shape = (m // tile["m"], tile["m"], n)
    else:
        out_shape = (n // tile["n"], m, tile["n"])
    return pl.pallas_call(
        body, name=name, out_shape=jax.ShapeDtypeStruct(out_shape, out_dtype),
        grid=(m // tile["m"], n // tile["n"], nk),
        in_specs=[spec(ar, ac, a3), spec(br, bc, b3)], out_specs=spec("i", "j", o3),
        scratch_shapes=[] if nk == 1 else [pltpu.VMEM((tile["m"], tile["n"]), F32)],
        compiler_params=_cparams("parallel", "parallel", "arbitrary"),
    )(a, b)


def _rowwise(fn, rows, vecs, outs, reds, name, ts):
    rows = [r if isinstance(r, tuple) else (r, 0, r.shape[1]) for r in rows]
    s = rows[0][0].shape[0]
    ts = _tile(s, ts, 8)
    nr, nv, no = len(rows), len(vecs), len(outs)

    def body(*refs):
        vals = fn(*[r[...] for r in refs[:nr + nv]])
        vals = vals if isinstance(vals, tuple) else (vals,)
        o_refs = refs[nr + nv:nr + nv + no]
        r_refs = refs[nr + nv + no:]
        for o_ref, val in zip(o_refs, vals[:no]):
            o_ref[...] = val.astype(o_ref.dtype)
        if r_refs:
            @pl.when(pl.program_id(0) == 0)
            def _():
                for r_ref in r_refs:
                    r_ref[...] = jnp.zeros_like(r_ref)

            for r_ref, val in zip(r_refs, vals[no:]):
                r_ref[...] += val

    in_specs = [pl.BlockSpec((ts, w), functools.partial(lambda i, cb: (i, cb), cb=cb)) for _, cb, w in rows]
    in_specs += [pl.BlockSpec(v.shape, lambda i: (0, 0)) for v in vecs]
    out_specs = [pl.BlockSpec((ts, c), lambda i: (i, 0)) for c, _ in outs]
    out_specs += [pl.BlockSpec((1, c), lambda i: (0, 0)) for c in reds]
    out_shape = [jax.ShapeDtypeStruct((s, c), dt) for c, dt in outs]
    out_shape += [jax.ShapeDtypeStruct((1, c), F32) for c in reds]
    return pl.pallas_call(
        body, name=name, out_shape=out_shape, grid=(s // ts,), in_specs=in_specs, out_specs=out_specs,
        compiler_params=_cparams("arbitrary" if reds else "parallel"),
    )(*[r[0] for r in rows], *vecs)


def _rms(x):
    x = x.astype(F32)
    r = lax.rsqrt(jnp.mean(x * x, axis=-1, keepdims=True) + EPS)
    return x * r, r


def _rms_bwd(xhat, r, dyg):
    return r * (dyg - xhat * jnp.mean(dyg * xhat, axis=-1, keepdims=True))


def _colsum(x):
    return jnp.sum(x, axis=0, keepdims=True)


def _sigmoid(x):
    return 1.0 / (1.0 + jnp.exp(-x.astype(F32)))


GELU_C = math.sqrt(2.0 / math.pi)
GELU_A = 0.044715


def _gelu(x):
    return 0.5 * x * (1.0 + jnp.tanh(GELU_C * (x + GELU_A * x * x * x)))


def _gelu_grad(x):
    t = jnp.tanh(GELU_C * (x + GELU_A * x * x * x))
    return 0.5 * (1.0 + t) + 0.5 * x * (1.0 - t * t) * GELU_C * (1.0 + 3.0 * GELU_A * x * x)


def _ffn_in_swiglu(hn, wg):
    s, d = hn.shape
    g2, n, _ = wg.shape
    g = g2 // 2
    tm = _tile(s, 1024)

    def body(h_ref, wa_ref, wb_ref, ab_ref, t_ref):
        hv = h_ref[...]
        a = lax.dot_general(hv, wa_ref[0], NT, preferred_element_type=F32)
        b = lax.dot_general(hv, wb_ref[0], NT, preferred_element_type=F32)
        ab_ref[0, 0] = a.astype(BF16)
        ab_ref[1, 0] = b.astype(BF16)
        t_ref[0] = (a * _sigmoid(a) * b).astype(BF16)

    return pl.pallas_call(
        body, name="mm_ffn_in_swiglu",
        out_shape=(jax.ShapeDtypeStruct((2, g, s, n), BF16), jax.ShapeDtypeStruct((g, s, n), BF16)),
        grid=(s // tm, g),
        in_specs=[pl.BlockSpec((tm, d), lambda i, j: (i, 0)), pl.BlockSpec((1, n, d), lambda i, j: (j, 0, 0)),
                  pl.BlockSpec((1, n, d), lambda i, j: (j + g, 0, 0))],
        out_specs=(pl.BlockSpec((2, 1, tm, n), lambda i, j: (0, j, i, 0)),
                   pl.BlockSpec((1, tm, n), lambda i, j: (j, i, 0))),
        compiler_params=_cparams("parallel", "parallel"),
    )(hn, wg, wg)


def _ffn_out_swiglu_bwd(dy, w4, ab):
    s, d = dy.shape
    _, g, _, n = ab.shape
    tm = _tile(s, 1024)

    def body(dy_ref, w_ref, ab_ref, t_ref, dab_ref):
        dt = lax.dot_general(dy_ref[...], w_ref[0], NT, preferred_element_type=F32)
        a = ab_ref[0, 0].astype(F32)
        b = ab_ref[1, 0].astype(F32)
        sig = _sigmoid(a)
        silu = a * sig
        t_ref[0] = (silu * b).astype(BF16)
        dab_ref[0, 0] = (dt * b * (sig * (1.0 + a * (1.0 - sig)))).astype(BF16)
        dab_ref[1, 0] = (dt * silu).astype(BF16)

    both = pl.BlockSpec((2, 1, tm, n), lambda i, j: (0, j, i, 0))
    one = pl.BlockSpec((1, tm, n), lambda i, j: (j, i, 0))
    return pl.pallas_call(
        body, name="mm_dt_swiglu_bwd",
        out_shape=(jax.ShapeDtypeStruct((g, s, n), BF16), jax.ShapeDtypeStruct((2, g, s, n), BF16)),
        grid=(s // tm, g),
        in_specs=[pl.BlockSpec((tm, d), lambda i, j: (i, 0)), pl.BlockSpec((1, n, d), lambda i, j: (j, 0, 0)), both],
        out_specs=(one, both),
        compiler_params=_cparams("parallel", "parallel"),
    )(dy, w4, ab)


def _forget_fwd(fl_t, b_col):
    h, s = fl_t.shape
    nb = s // LANE

    def body(fl_ref, b_ref, c_ref):
        upper = (lax.broadcasted_iota(jnp.int32, (LANE, LANE), 0)
                 <= lax.broadcasted_iota(jnp.int32, (LANE, LANE), 1)).astype(F32)

        def step(i, carry):
            x = fl_ref[i] + b_ref[...]
            lf = jnp.minimum(x, 0.0) - jnp.log(1.0 + jnp.exp(-jnp.abs(x)))
            cs = jnp.dot(lf, upper, precision=lax.Precision.HIGHEST, preferred_element_type=F32) + carry
            c_ref[i] = cs
            return cs[:, LANE - 1:LANE]

        lax.fori_loop(0, nb, step, jnp.zeros((h, 1), F32))

    out = pl.pallas_call(
        body, name="forget_fwd", out_shape=jax.ShapeDtypeStruct((nb, h, LANE), F32),
        compiler_params=_cparams(),
    )(fl_t.reshape(h, nb, LANE).transpose(1, 0, 2), b_col)
    return out.transpose(1, 0, 2).reshape(h, s)


def _forget_bwd(dc_t, fl_t, b_col):
    h, s = fl_t.shape
    nb = s // LANE

    def body(dc_ref, fl_ref, b_ref, dfl_ref, db_ref):
        lower = (lax.broadcasted_iota(jnp.int32, (LANE, LANE), 0)
                 >= lax.broadcasted_iota(jnp.int32, (LANE, LANE), 1)).astype(F32)

        def step(t, carry):
            tail, db = carry
            i = nb - 1 - t
            rc = jnp.dot(dc_ref[i], lower, precision=lax.Precision.HIGHEST, preferred_element_type=F32) + tail
            dfl = rc * (1.0 - _sigmoid(fl_ref[i] + b_ref[...]))
            dfl_ref[i] = dfl
            return rc[:, 0:1], db + jnp.sum(dfl, axis=1, keepdims=True)

        _, db = lax.fori_loop(0, nb, step, (jnp.zeros((h, 1), F32), jnp.zeros((h, 1), F32)))
        db_ref[...] = db

    blocked = lambda a: a.reshape(h, nb, LANE).transpose(1, 0, 2)
    dfl, db = pl.pallas_call(
        body, name="forget_bwd",
        out_shape=(jax.ShapeDtypeStruct((nb, h, LANE), F32), jax.ShapeDtypeStruct((h, 1), F32)),
        compiler_params=_cparams(),
    )(blocked(dc_t), blocked(fl_t), b_col)
    return dfl.transpose(1, 0, 2).reshape(h, s), db


N_PAIRS = N_HEADS // 2


def _causal_mask(t):
    return lax.broadcasted_iota(jnp.int32, (t, t), 0) >= lax.broadcasted_iota(jnp.int32, (t, t), 1)


def _head_lanes():
    return lax.broadcasted_iota(jnp.int32, (1, 2 * HEAD_DIM), 1) < HEAD_DIM


def _pick(x2, first, hh):
    zero = jnp.zeros_like(x2)
    return jnp.where(first, x2, zero) if hh == 0 else jnp.where(first, zero, x2)


BIAS_TERMS = 3


def _attn_prep(z, c, ts):
    s = z.shape[0]
    ts = _tile(s, ts, 16)
    w = 2 * HEAD_DIM

    def body(q_ref, k_ref, v_ref, c_ref, qa_ref, ka_ref, vb_ref):
        lane = lax.broadcasted_iota(jnp.int32, (1, w), 1)
        first = lane < HEAD_DIM
        cv = c_ref[...]
        for h in range(N_HEADS):
            pair = slice((h // 2) * w, (h // 2 + 1) * w)
            qh = q_ref[:, pair] * ATT_SCALE
            kh = k_ref[:, pair]
            if h % 2:
                qh = pltpu.roll(qh, HEAD_DIM, 1)
                kh = pltpu.roll(kh, HEAD_DIM, 1)
            rest = cv[:, h:h + 1]
            q_tail = jnp.zeros((1, w), F32)
            k_tail = jnp.zeros((1, w), F32)
            for t in range(BIAS_TERMS):
                term = rest.astype(BF16).astype(F32)
                rest = rest - term
                q_tail = jnp.where(lane == HEAD_DIM + t, term, jnp.where(lane == HEAD_DIM + BIAS_TERMS + t, 1.0, q_tail))
                k_tail = jnp.where(lane == HEAD_DIM + t, 1.0, jnp.where(lane == HEAD_DIM + BIAS_TERMS + t, -term, k_tail))
            qa_ref[:, h * w:(h + 1) * w] = jnp.where(first, qh, q_tail).astype(BF16)
            ka_ref[:, h * w:(h + 1) * w] = jnp.where(first, kh, k_tail).astype(BF16)
        vb_ref[...] = v_ref[...].astype(BF16)

    col = lambda cb: pl.BlockSpec((ts, D_ATT), lambda i: (i, cb))
    wide = pl.BlockSpec((ts, N_HEADS * w), lambda i: (i, 0))
    return pl.pallas_call(
        body, name="attn_prep",
        out_shape=(jax.ShapeDtypeStruct((s, N_HEADS * w), BF16), jax.ShapeDtypeStruct((s, N_HEADS * w), BF16),
                   jax.ShapeDtypeStruct((s, D_ATT), BF16)),
        grid=(s // ts,), in_specs=[col(0), col(1), col(2), pl.BlockSpec((ts, N_HEADS), lambda i: (i, 0))],
        out_specs=(wide, wide, col(0)),
        compiler_params=_cparams("parallel"),
    )(z, z, z, c)


def _attn_fwd(qa, ka, vb, tq):
    s = qa.shape[0]
    nq = s // tq
    w = 2 * HEAD_DIM

    def body(q_ref, k_ref, v_ref, o_ref, lse_ref):
        i = pl.program_id(1)
        first = _head_lanes()
        q2 = q_ref[...]

        def block(j, carry, masked):
            off = pl.multiple_of(j * tq, tq)
            k2 = k_ref[pl.ds(off, tq), :]
            v2 = v_ref[pl.ds(off, tq), :]
            new = []
            for hh in range(2):
                m, l, acc = carry[hh]
                sc = lax.dot_general(q2[:, hh * w:(hh + 1) * w], k2[:, hh * w:(hh + 1) * w], NT,
                                     preferred_element_type=F32)
                if masked:
                    sc = jnp.where(_causal_mask(tq), sc, NEG_INF)
                m_new = jnp.maximum(m, jnp.max(sc, axis=-1, keepdims=True))
                alpha = jnp.exp(m - m_new)
                p = jnp.exp(sc - m_new)
                l = alpha * l + jnp.sum(p, axis=-1, keepdims=True)
                p_hi = p.astype(BF16)
                p_lo = (p - p_hi.astype(F32)).astype(BF16)
                acc = (alpha * acc + jnp.dot(p_hi, v2, preferred_element_type=F32)
                       + jnp.dot(p_lo, v2, preferred_element_type=F32))
                new.append((m_new, l, acc))
            return tuple(new)

        one = (jnp.full((tq, 1), NEG_INF, F32), jnp.zeros((tq, 1), F32), jnp.zeros((tq, w), F32))
        carry = lax.fori_loop(0, i, lambda j, c: block(j, c, False), (one, one))
        (m0, l0, a0), (m1, l1, a1) = block(i, carry, True)
        o_ref[...] = jnp.where(first, a0 / l0, a1 / l1)
        lse_ref[0] = m0 + jnp.log(l0)
        lse_ref[1] = m1 + jnp.log(l1)

    return pl.pallas_call(
        body, name="attn_fwd",
        out_shape=(jax.ShapeDtypeStruct((s, D_ATT), F32), jax.ShapeDtypeStruct((N_HEADS, s, 1), F32)),
        grid=(N_PAIRS, nq),
        in_specs=[pl.BlockSpec((tq, 2 * w), lambda hp, i: (i, hp)),
                  pl.BlockSpec((s, 2 * w), lambda hp, i: (0, hp)),
                  pl.BlockSpec((s, w), lambda hp, i: (0, hp))],
        out_specs=(pl.BlockSpec((tq, w), lambda hp, i: (i, hp)),
                   pl.BlockSpec((2, tq, 1), lambda hp, i: (hp, i, 0))),
        compiler_params=_cparams("parallel", "parallel"),
    )(qa, ka, vb)


def _attn_delta(o, do, tq):
    s = o.shape[0]
    w = 2 * HEAD_DIM

    def body(o_ref, do_ref, d_ref):
        first = _head_lanes()
        prod = o_ref[...] * do_ref[...].astype(F32)
        d_ref[0] = jnp.sum(_pick(prod, first, 0), axis=-1, keepdims=True)
        d_ref[1] = jnp.sum(_pick(prod, first, 1), axis=-1, keepdims=True)

    blk = pl.BlockSpec((tq, w), lambda hp, i: (i, hp))
    return pl.pallas_call(
        body, name="attn_delta", out_shape=jax.ShapeDtypeStruct((N_HEADS, s, 1), F32), grid=(N_PAIRS, s // tq),
        in_specs=[blk, blk], out_specs=pl.BlockSpec((2, tq, 1), lambda hp, i: (hp, i, 0)),
        compiler_params=_cparams("parallel", "parallel"),
    )(o, do)


def _attn_bwd(qa, ka, vb, do, lse, delta, tq):
    s = qa.shape[0]
    nq = s // tq
    w = 2 * HEAD_DIM

    def body(q_ref, do_ref, lse_ref, dl_ref, k_ref, v_ref, dq_ref, dk_ref, dv_ref, dc_ref, dq_acc):
        j = pl.program_id(1)
        first = _head_lanes()

        @pl.when(j == 0)
        def _():
            dq_acc[...] = jnp.zeros_like(dq_acc)

        k2 = k_ref[...]
        v2 = v_ref[...]

        def step(i, carry, masked):
            off = pl.multiple_of(i * tq, tq)
            rows = pl.ds(off, tq)
            q2 = q_ref[rows, :]
            do2 = do_ref[rows, :]
            new, dqs = [], []
            for hh in range(2):
                dk, dv, dcs = carry[hh]
                qh = q2[:, hh * w:(hh + 1) * w]
                kh = k2[:, hh * w:(hh + 1) * w]
                sc = lax.dot_general(qh, kh, NT, preferred_element_type=F32)
                if masked:
                    sc = jnp.where(_causal_mask(tq), sc, NEG_INF)
                p = jnp.exp(sc - lse_ref[hh, rows, :])
                dv = dv + lax.dot_general(do2, p.astype(BF16), TN, preferred_element_type=F32)
                dp = lax.dot_general(_pick(do2, first, hh), v2, NT, preferred_element_type=F32)
                ds = p * (dp - dl_ref[hh, rows, :])
                dsb = ds.astype(BF16)
                dk = dk + lax.dot_general(qh, dsb, TN, preferred_element_type=F32)
                dqs.append(jnp.dot(dsb, kh, preferred_element_type=F32))
                new.append((dk, dv, dcs + jnp.sum(ds, axis=0, keepdims=True)))
            dq_acc[rows, :] += jnp.where(first, dqs[0], pltpu.roll(dqs[1], HEAD_DIM, 1)) * ATT_SCALE
            return tuple(new)

        one = (jnp.zeros((w, tq), F32), jnp.zeros((w, tq), F32), jnp.zeros((1, tq), F32))
        carry = step(j, (one, one), True)
        (dk0, dv0, dc0), (dk1, dv1, dc1) = lax.fori_loop(j + 1, nq, lambda i, c: step(i, c, False), carry)
        dk_ref[...] = jnp.where(first, dk0.T, pltpu.roll(dk1.T, HEAD_DIM, 1)).astype(BF16)
        dv_ref[...] = jnp.where(first, dv0.T, dv1.T).astype(BF16)
        dc_ref[0, 0] = -dc0
        dc_ref[1, 0] = -dc1

        @pl.when(j == nq - 1)
        def _():
            dq_ref[...] = dq_acc[...].astype(BF16)

    whole = lambda width: pl.BlockSpec((s, width), lambda hp, j: (0, hp))
    whole_heads = pl.BlockSpec((2, s, 1), lambda hp, j: (hp, 0, 0))
    blk = lambda width: pl.BlockSpec((tq, width), lambda hp, j: (j, hp))
    crow = pl.BlockSpec((2, 1, 1, tq), lambda hp, j: (hp, j, 0, 0))
    return pl.pallas_call(
        body, name="attn_bwd",
        out_shape=(jax.ShapeDtypeStruct((s, D_ATT), BF16), jax.ShapeDtypeStruct((s, D_ATT), BF16),
                   jax.ShapeDtypeStruct((s, D_ATT), BF16), jax.ShapeDtypeStruct((N_HEADS, nq, 1, tq), F32)),
        grid=(N_PAIRS, nq),
        in_specs=[whole(2 * w), whole(w), whole_heads, whole_heads, blk(2 * w), blk(w)],
        out_specs=(whole(w), blk(w), blk(w), crow),
        scratch_shapes=[pltpu.VMEM((s, w), F32)],
        compiler_params=_cparams("parallel", "arbitrary"),
    )(qa, do, lse, delta, ka, vb)


def _pair_sums(x, first):
    total = jnp.sum(x, axis=-1, keepdims=True)
    head = jnp.sum(jnp.where(first, x, 0.0), axis=-1, keepdims=True)
    return head, total - head


def _pair_mean(x, first):
    head, tail = _pair_sums(x, first)
    return jnp.where(first, head, tail) * (1.0 / GROUP_DIM)


def _gm_pair_norm(v2, first):
    d = v2 - _pair_mean(v2, first)
    rstd = lax.rsqrt(_pair_mean(d * d, first) + EPS)
    return d * rstd, rstd


def _gm_pair_mix(w_ref, pr, rhs, first):
    return jnp.where(first, jnp.dot(w_ref[2 * pr], rhs, preferred_element_type=F32),
                     jnp.dot(w_ref[2 * pr + 1], rhs, preferred_element_type=F32))


def _gmlp_fwd(z, wt, bs_t, vgain):
    s = z.shape[0]

    def body(gu_ref, gv_ref, wt_ref, bs_ref, vg_ref, o_ref):
        first = _head_lanes()
        for pr in range(N_GROUPS // 2):
            sl = slice(2 * pr * GROUP_DIM, 2 * (pr + 1) * GROUP_DIM)
            vhat, _ = _gm_pair_norm(_gelu(gv_ref[:, sl]), first)
            vn = (vhat * vg_ref[:, sl]).astype(BF16)
            bias = jnp.where(first, bs_ref[:, 2 * pr:2 * pr + 1], bs_ref[:, 2 * pr + 1:2 * pr + 2])
            o_ref[:, sl] = _gelu(gu_ref[:, sl]) * (_gm_pair_mix(wt_ref, pr, vn, first) + bias)

    full = lambda a: pl.BlockSpec(a.shape, lambda n: (0,) * a.ndim)
    return pl.pallas_call(
        body, name="gmlp_fwd", out_shape=jax.ShapeDtypeStruct((s, D_GM), F32), grid=(s // CHUNK,),
        in_specs=[pl.BlockSpec((CHUNK, D_GM), lambda n: (n, 3)), pl.BlockSpec((CHUNK, D_GM), lambda n: (n, 4)),
                  full(wt), full(bs_t), full(vgain)],
        out_specs=pl.BlockSpec((CHUNK, D_GM), lambda n: (n, 0)),
        compiler_params=_cparams("parallel"),
    )(z, z, wt, bs_t, vgain)


def _gmlp_bwd(z, dgm, wt, wt_t, bs_t, vgain):
    s = z.shape[0]

    def body(gu_ref, gv_ref, dgm_ref, wt_ref, wtt_ref, bs_ref, vg_ref, dgu_ref, dgv_ref, dwt_ref, dbs_ref, dvg_ref):
        @pl.when(pl.program_id(0) == 0)
        def _():
            dwt_ref[...] = jnp.zeros_like(dwt_ref)
            dbs_ref[...] = jnp.zeros_like(dbs_ref)
            dvg_ref[...] = jnp.zeros_like(dvg_ref)

        first = _head_lanes()
        for pr in range(N_GROUPS // 2):
            g0, g1 = 2 * pr, 2 * pr + 1
            sl = slice(g0 * GROUP_DIM, (g1 + 1) * GROUP_DIM)
            gu = gu_ref[:, sl]
            gv = gv_ref[:, sl]
            dgm = dgm_ref[:, sl]
            vhat, rstd = _gm_pair_norm(_gelu(gv), first)
            gain = vg_ref[:, sl]
            vn = (vhat * gain).astype(BF16)
            bias = jnp.where(first, bs_ref[:, g0:g0 + 1], bs_ref[:, g1:g1 + 1])
            mixed = _gm_pair_mix(wt_ref, pr, vn, first) + bias
            dgu_ref[:, sl] = (dgm * mixed * _gelu_grad(gu)).astype(BF16)
            dmixed = dgm * _gelu(gu)
            db0, db1 = _pair_sums(dmixed, first)
            dbs_ref[:, g0:g0 + 1] += db0
            dbs_ref[:, g1:g1 + 1] += db1
            dwt_ref[g0] += lax.dot_general(_pick(dmixed, first, 0).astype(BF16), vn, NT, preferred_element_type=F32)
            dwt_ref[g1] += lax.dot_general(_pick(dmixed, first, 1).astype(BF16), vn, NT, preferred_element_type=F32)
            dvn = _gm_pair_mix(wtt_ref, pr, dmixed.astype(BF16), first)
            dvg_ref[:, sl] += _colsum(dvn * vhat)
            dvhat = dvn * gain
            dvf = rstd * (dvhat - _pair_mean(dvhat, first) - vhat * _pair_mean(dvhat * vhat, first))
            dgv_ref[:, sl] = (dvf * _gelu_grad(gv)).astype(BF16)

    full = lambda a: pl.BlockSpec(a.shape, lambda n: (0,) * a.ndim)
    chunk = pl.BlockSpec((CHUNK, D_GM), lambda n: (n, 0))
    return pl.pallas_call(
        body, name="gmlp_bwd",
        out_shape=(jax.ShapeDtypeStruct((s, D_GM), BF16), jax.ShapeDtypeStruct((s, D_GM), BF16),
                   jax.ShapeDtypeStruct(wt.shape, F32), jax.ShapeDtypeStruct(bs_t.shape, F32),
                   jax.ShapeDtypeStruct(vgain.shape, F32)),
        grid=(s // CHUNK,),
        in_specs=[pl.BlockSpec((CHUNK, D_GM), lambda n: (n, 3)), pl.BlockSpec((CHUNK, D_GM), lambda n: (n, 4)),
                  chunk, full(wt), full(wt_t), full(bs_t), full(vgain)],
        out_specs=(chunk, chunk, full(wt), full(bs_t), full(vgain)),
        compiler_params=_cparams("arbitrary"),
    )(z, z, dgm, wt, wt_t, bs_t, vgain)


def _dz_concat(wide, dfl, ts):
    s = dfl.shape[0]
    ts = _tile(s, ts, 16)
    n = len(wide)

    def body(*refs):
        o_ref = refs[-1]
        for k in range(n):
            o_ref[:, k * D_ATT:(k + 1) * D_ATT] = refs[k][...]
        o_ref[:, F_OFF:] = jnp.zeros((ts, D_IN_PAD - F_OFF), BF16)
        o_ref[:, F_OFF:F_OFF + N_HEADS] = refs[n][...].astype(BF16)

    return pl.pallas_call(
        body, name="dz_concat", out_shape=jax.ShapeDtypeStruct((s, D_IN_PAD), BF16), grid=(s // ts,),
        in_specs=[pl.BlockSpec((ts, D_ATT), lambda i: (i, 0))] * n + [pl.BlockSpec((ts, N_HEADS), lambda i: (i, 0))],
        out_specs=pl.BlockSpec((ts, D_IN_PAD), lambda i: (i, 0)),
        compiler_params=_cparams("parallel"),
    )(*wide, dfl)


def _layer_fwd(h0, p_i, w, tq, late):
    s, d = h0.shape
    nq = s // tq
    sv = {"h0": h0}

    (hn1,) = _rowwise(lambda h, g: _rms(h)[0] * g, [h0], [w["mix_pre_norm"]], [(d, BF16)], [], "pre_mix", ROW_TILE)
    z = _mm(hn1, w["w_in"], "nn", F32, "mm_in")
    fl_t = z[:, F_OFF:F_OFF + N_HEADS].T
    c_t = _forget_fwd(fl_t, w["b_forget"])
    qa, ka, vb = _attn_prep(z, c_t.T, ROW_TILE)
    att, lse = _attn_fwd(qa, ka, vb, tq)
    gm = _gmlp_fwd(z, w["wt"], w["bs_t"], w["gm_v_norm"])
    w = dict(w, **late(att))

    def mix_out(att, gm, g):
        return jnp.concatenate([_rms(att)[0] * g[:, :D_ATT], _rms(gm)[0] * g[:, D_ATT:]], axis=-1)

    (mc,) = _rowwise(mix_out, [att, gm], [w["mix_out_norm"]], [(D_ATT + D_GM, BF16)], [], "mix_out", ROW_TILE)
    y1 = _mm(mc, w["w_out"], "nn", BF16, "mm_out")

    def post_mix(h0, y1, gpost, gpre):
        h1 = h0 + _rms(y1)[0] * gpost
        return h1, _rms(h1)[0] * gpre

    h1, hn2 = _rowwise(post_mix, [h0, y1], [w["mix_post_norm"], w["ffn_pre_norm"]],
                       [(d, F32), (d, BF16)], [], "post_mix", ROW_TILE)
    ab, t = _ffn_in_swiglu(hn2, w["w_ffn_in"])
    y2 = _mm(t, w["w_ffn_out"], "nn", BF16, "mm_ffn_out", a3="k", b3="k")

    def post_ffn(h1, y2, g):
        h2 = h1 + _rms(y2)[0] * g
        return h2, _rms(h2)[0]

    h2, hr = _rowwise(post_ffn, [h1, y2], [w["ffn_post_norm"]], [(d, F32), (d, BF16)], [], "post_ffn", ROW_TILE)
    gl = _mm(hr, w["w_ple_gate"], "nn", BF16, "mm_gate")
    pe = _mm(p_i, w["w_ple"], "nn", BF16, "mm_ple")
    (h3,) = _rowwise(lambda h2, gl, pe, g: h2 + _sigmoid(gl) * (_rms(pe)[0] * g), [h2, gl, pe], [w["ple_norm"]],
                     [(d, F32)], [], "ple_out", ROW_TILE)
    sv.update(hn1=hn1, z=z, fl_t=fl_t, qa=qa, ka=ka, vb=vb, lse=lse, att=att, gm=gm,
              mc=mc, y1=y1, h1=h1, hn2=hn2, ab=ab, y2=y2, h2=h2, hr=hr, gl=gl, pe=pe, p_i=p_i)
    return h3, sv


def _layer_bwd(dh3, sv, w, tq, mid):
    s, d = dh3.shape
    g = {}
    by_rows = lambda a: a.reshape(N_DEV, -1, a.shape[-1])
    by_cols = lambda a: jnp.stack(jnp.split(a, N_DEV, axis=-1))

    def ple_bwd(dh3, gl, pe, gple):
        gate = _sigmoid(gl)
        pehat, rpe = _rms(pe)
        dgl = dh3 * (pehat * gple) * gate * (1.0 - gate)
        de = dh3 * gate
        return dgl, _rms_bwd(pehat, rpe, de * gple), _colsum(de * pehat)

    dgl, dpe, g["ple_norm"] = _rowwise(ple_bwd, [dh3, sv["gl"], sv["pe"]], [w["ple_norm"]],
                                       [(d, BF16), (d, BF16)], [d], "ple_bwd", ROW_TILE)
    g["w_ple_gate"] = by_rows(_mm(sv["hr"], dgl, "tn", BF16, "mm_dgate"))
    dhr = _mm(dgl, w["w_ple_gate"], "nt", BF16, "mm_dhr")
    g["w_ple"] = by_cols(_mm(sv["p_i"], dpe, "tn", BF16, "mm_dple"))

    def ffn_post_bwd(dh3, dhr, h2, y2, gpost):
        h2hat, r2 = _rms(h2)
        dh2 = dh3 + _rms_bwd(h2hat, r2, dhr)
        y2hat, ry = _rms(y2)
        return dh2, _rms_bwd(y2hat, ry, dh2 * gpost), _colsum(dh2 * y2hat)

    dh2, dy2, g["ffn_post_norm"] = _rowwise(ffn_post_bwd, [dh3, dhr, sv["h2"], sv["y2"]], [w["ffn_post_norm"]],
                                            [(d, F32), (d, BF16)], [d], "ffn_post_bwd", ROW_TILE)
    t, dab = _ffn_out_swiglu_bwd(dy2, w["w_ffn_out"], sv["ab"])
    dab = dab.reshape((N_DEV,) + dab.shape[2:])
    g["w_ffn_out"] = by_rows(_mm(t, dy2, "tn", BF16, "mm_dffn_out", a3="m", o3="m"))
    dhn2 = _mm(dab, w["w_ffn_in"], "nn", BF16, "mm_dhn2", a3="k", b3="k")
    g["w_ffn_in"] = _mm(dab, sv["hn2"], "tn", BF16, "mm_dffn_in", a3="m", o3="m")

    def mix_post_bwd(dh2, dhn2, h1, y1, gpre, gpost):
        h1hat, r1 = _rms(h1)
        dh1 = dh2 + _rms_bwd(h1hat, r1, dhn2 * gpre)
        y1hat, ry = _rms(y1)
        return dh1, _rms_bwd(y1hat, ry, dh1 * gpost), _colsum(dhn2 * h1hat), _colsum(dh1 * y1hat)

    dh1, dy1, g["ffn_pre_norm"], g["mix_post_norm"] = _rowwise(
        mix_post_bwd, [dh2, dhn2, sv["h1"], sv["y1"]], [w["ffn_pre_norm"], w["mix_post_norm"]],
        [(d, F32), (d, BF16)], [d, d], "mix_post_bwd", ROW_TILE)
    dmc = _mm(dy1, w["w_out"], "nt", BF16, "mm_dmc")
    g["w_out"] = by_rows(_mm(sv["mc"], dy1, "tn", BF16, "mm_dout"))
    w = dict(w, **mid(g, dmc))

    def mix_out_bwd(da, dg, att, gm, gain):
        atthat, ra = _rms(att)
        gmhat, rg = _rms(gm)
        dgain = jnp.concatenate([_colsum(da * atthat), _colsum(dg * gmhat)], axis=-1)
        return _rms_bwd(atthat, ra, da * gain[:, :D_ATT]), _rms_bwd(gmhat, rg, dg * gain[:, D_ATT:]), dgain

    datt, dgm, g["mix_out_norm"] = _rowwise(
        mix_out_bwd, [(dmc, 0, D_ATT), (dmc, 1, D_GM), sv["att"], sv["gm"]], [w["mix_out_norm"]],
        [(D_ATT, BF16), (D_GM, F32)], [D_ATT + D_GM], "mix_out_bwd", ROW_TILE)

    dgu, dgv, dwt, dbs_t, g["gm_v_norm"] = _gmlp_bwd(sv["z"], dgm, w["wt"], w["wt_t"], w["bs_t"], w["gm_v_norm"])
    g["gm_w_s"] = dwt * jnp.tril(jnp.ones((CHUNK, CHUNK), F32))[None]
    g["gm_b_s"] = dbs_t.T

    delta = _attn_delta(sv["att"], datt, tq)
    dq, dk, dv, dc_row = _attn_bwd(sv["qa"], sv["ka"], sv["vb"], datt, sv["lse"], delta, tq)
    dfl_t, db = _forget_bwd(dc_row.reshape(N_HEADS, s), sv["fl_t"], w["b_forget"])
    g["b_forget"] = db.reshape(1, N_HEADS)
    dz = _dz_concat([dq, dk, dv, dgu, dgv], dfl_t.T, ROW_TILE)
    dhn1 = _mm(dz, w["w_in"], "nt", BF16, "mm_dhn1")
    g["w_in"] = _w_in_split(_mm(sv["hn1"], dz, "tn", BF16, "mm_din"))

    def mix_pre_bwd(dh1, dhn1, h0, gpre):
        h0hat, r0 = _rms(h0)
        return dh1 + _rms_bwd(h0hat, r0, dhn1 * gpre), _colsum(dhn1 * h0hat)

    dh0, g["mix_pre_norm"] = _rowwise(mix_pre_bwd, [dh1, dhn1, sv["h0"]], [w["mix_pre_norm"]],
                                      [(d, F32)], [d], "mix_pre_bwd", ROW_TILE)
    return dh0, g


ANY = pl.BlockSpec(memory_space=pl.ANY)


def _all_gather(xs, layer, name):
    n = len(xs)

    def body(*refs):
        x_refs, out_refs = refs[:n], refs[n:2 * n]
        send_sems, recv_sems, local_sems = refs[2 * n:]
        x, y, c = lax.axis_index("x"), lax.axis_index("y"), lax.axis_index("c")
        me, sibling = (x, y, c), (x, y, 1 - c)
        chips = [(1 - x, y), (x, 1 - y), (1 - x, 1 - y)]

        def shard(a):
            return x_refs[a] if layer is None else x_refs[a].at[layer]

        def rows(a, px, py, pc):
            return out_refs[a].at[4 * px + 2 * py + pc]

        def copy(a, kk, block, to, from_shard=False):
            return pltpu.make_async_remote_copy(
                src_ref=shard(a) if from_shard else rows(a, *block), dst_ref=rows(a, *block),
                send_sem=send_sems.at[7 * a + kk], recv_sem=recv_sems.at[7 * a + kk],
                device_id=to, device_id_type=MESH)

        mine = [pltpu.make_async_copy(shard(a), rows(a, *me), local_sems.at[a]) for a in range(n)]
        for cp in mine:
            cp.start()
        first = []
        for a in range(n):
            first.append(copy(a, 0, me, sibling, from_shard=True))
            first += [copy(a, 1 + j, me, (*chip, c), from_shard=True) for j, chip in enumerate(chips)]
        for cp in first:
            cp.start()
        passed = []
        for j, chip in enumerate(chips):
            for a in range(n):
                copy(a, 1 + j, (*chip, c), me).wait_recv()
                passed.append(copy(a, 4 + j, (*chip, c), sibling))
                passed[-1].start()
        for a in range(n):
            copy(a, 0, sibling, me).wait_recv()
        for j, chip in enumerate(chips):
            for a in range(n):
                copy(a, 4 + j, (*chip, 1 - c), me).wait_recv()
        for cp in first + passed:
            cp.wait_send()
        for cp in mine:
            cp.wait()

    shapes = [x.shape if layer is None else x.shape[1:] for x in xs]
    return pl.pallas_call(
        body, name=name, out_shape=[jax.ShapeDtypeStruct((N_DEV,) + sh, x.dtype) for sh, x in zip(shapes, xs)],
        in_specs=[ANY] * n, out_specs=[ANY] * n,
        scratch_shapes=[pltpu.SemaphoreType.DMA((7 * n,)), pltpu.SemaphoreType.DMA((7 * n,)),
                        pltpu.SemaphoreType.DMA((n,))],
    )(*xs)


HBM = pl.BlockSpec(memory_space=pltpu.HBM)
SEMS = pl.BlockSpec(memory_space=pltpu.SEMAPHORE)
EFFECT = pltpu.SideEffectType.DATAFLOW_SIDE_EFFECTING
FLIPS = tuple((fx, fy, fc) for fx in (0, 1) for fy in (0, 1) for fc in (0, 1))[1:]


def _exchange_copies(src_refs, land_refs, send_sems, recv_sems, layer, scatter):
    x, y, c = lax.axis_index("x"), lax.axis_index("y"), lax.axis_index("c")
    me = 4 * x + 2 * y + c
    copies = []
    for a, (src, land) in enumerate(zip(src_refs, land_refs)):
        for f, (fx, fy, fc) in enumerate(FLIPS):
            px, py, pc = (1 - x if fx else x), (1 - y if fy else y), (1 - c if fc else c)
            if scatter:
                block = src.at[4 * px + 2 * py + pc]
            else:
                block = src if layer is None else src.at[layer]
            copies.append(pltpu.make_async_remote_copy(
                src_ref=block, dst_ref=land.at[me], send_sem=send_sems.at[7 * a + f], recv_sem=recv_sems.at[7 * a + f],
                device_id=(px, py, pc), device_id_type=MESH))
    return copies


def _exchange_start(srcs, lands, layer, scatter, name):
    n = len(srcs)

    def body(*refs):
        for cp in _exchange_copies(refs[:n], refs[n:2 * n], refs[2 * n], refs[2 * n + 1], layer, scatter):
            cp.start()
        token = refs[-1]
        token[...] = jnp.zeros_like(token)

    operands = list(srcs) + list(lands)
    outs = pl.pallas_call(
        body, name=name,
        out_shape=(pltpu.SemaphoreType.DMA((7 * n,)), pltpu.SemaphoreType.DMA((7 * n,)),
                   *[pltpu.HBM(a.shape, a.dtype) for a in operands], jax.ShapeDtypeStruct((8, LANE), F32)),
        in_specs=[HBM] * (2 * n),
        out_specs=(SEMS, SEMS, *[HBM] * (2 * n), pl.BlockSpec(memory_space=pltpu.VMEM)),
        input_output_aliases={i: 2 + i for i in range(2 * n)},
        compiler_params=pltpu.CompilerParams(has_side_effects=EFFECT),
    )(*[pltpu.with_memory_space_constraint(a, pltpu.HBM) for a in operands])
    return outs[0], outs[1], outs[2:2 + n], outs[2 + n:2 + 2 * n], outs[-1]


def _exchange_wait(started, after, layer, scatter, name):
    send_sems, recv_sems, srcs, lands, _ = started
    n = len(srcs)

    def body(*refs):
        for cp in _exchange_copies(refs[:n], refs[n:2 * n], refs[2 * n], refs[2 * n + 1], layer, scatter):
            cp.wait_send()
            cp.wait_recv()

    operands = list(srcs) + list(lands)
    outs = pl.pallas_call(
        body, name=name, out_shape=tuple(pltpu.HBM(a.shape, a.dtype) for a in operands),
        in_specs=[HBM] * (2 * n) + [SEMS, SEMS, ANY], out_specs=[HBM] * (2 * n),
        input_output_aliases={i: i for i in range(2 * n)},
        compiler_params=pltpu.CompilerParams(has_side_effects=EFFECT),
    )(*operands, send_sems, recv_sems, after)
    return outs[:n], outs[n:]


def _sum_devices(parts):
    _, r, c = parts.shape

    def body(p_ref, o_ref):
        acc = p_ref[0].astype(F32)
        for j in range(1, N_DEV):
            acc = acc + p_ref[j].astype(F32)
        o_ref[...] = acc

    return pl.pallas_call(
        body, name="small_sum", out_shape=jax.ShapeDtypeStruct((r, c), F32), grid=(r // SMALL_ROWS,),
        in_specs=[pl.BlockSpec((N_DEV, SMALL_ROWS, c), lambda i: (0, i, 0))],
        out_specs=pl.BlockSpec((SMALL_ROWS, c), lambda i: (i, 0)),
        compiler_params=_cparams("parallel"),
    )(parts)


def _adamw_math(w, g, m, v):
    m = ADAM_B1 * m + (1.0 - ADAM_B1) * g
    v = ADAM_B2 * v + (1.0 - ADAM_B2) * (g * g)
    m_hat = m / (1.0 - ADAM_B1 ** ADAM_STEP)
    v_hat = v / (1.0 - ADAM_B2 ** ADAM_STEP)
    return -ADAM_LR * (m_hat / (jnp.sqrt(v_hat) + ADAM_EPS) + ADAM_WD * w), m, v


def _adamw_shard(w, m, v, parts, layer, outs, name):
    _, a, b = w.shape
    ta = _tile(a, 256, 16)
    if outs is None:
        outs = [lax.empty(w.shape, F32) for _ in range(4)]

    def body(w_ref, m_ref, v_ref, p_ref, *refs):
        g_ref, d_ref, nm_ref, nv_ref = refs[4:]
        g = p_ref[0].astype(F32)
        for j in range(1, N_DEV):
            g = g + p_ref[j].astype(F32)
        g_ref[0] = g
        d_ref[0], nm_ref[0], nv_ref[0] = _adamw_math(w_ref[0], g, m_ref[0], v_ref[0])

    mine = pl.BlockSpec((1, ta, b), lambda i: (layer, i, 0))
    return pl.pallas_call(
        body, name=name, out_shape=[jax.ShapeDtypeStruct(w.shape, F32)] * 4, grid=(a // ta,),
        in_specs=[mine, mine, mine, pl.BlockSpec((N_DEV, ta, b), lambda i: (0, i, 0))] + [ANY] * 4,
        out_specs=[mine] * 4, input_output_aliases={4 + k: k for k in range(4)},
        compiler_params=_cparams("parallel"),
    )(w, m, v, parts, *outs)


def _w_in_moves():
    n = D_IN // N_DEV
    runs = ((0, 3 * D_ATT, 0), (3 * D_ATT, 3 * D_ATT + N_HEADS, F_OFF), (3 * D_ATT + N_HEADS, D_IN, 3 * D_ATT))
    moves = []
    for j in range(N_DEV):
        for lo, hi, padded in runs:
            a, b = max(n * j, lo), min(n * (j + 1), hi)
            if a < b:
                moves.append((j, a - n * j, padded + a - lo, b - a))
    return moves


def _w_in_assemble(blocks):
    _, d, n = blocks.shape
    tr = _tile(d, 256, 16)

    def body(b_ref, o_ref):
        o_ref[:, D_IN:] = jnp.zeros((tr, D_IN_PAD - D_IN), o_ref.dtype)
        for j, col, padded, width in _w_in_moves():
            o_ref[:, padded:padded + width] = b_ref[j, :, col:col + width]

    return pl.pallas_call(
        body, name="w_in_assemble", out_shape=jax.ShapeDtypeStruct((d, D_IN_PAD), blocks.dtype), grid=(d // tr,),
        in_specs=[pl.BlockSpec((N_DEV, tr, n), lambda i: (0, i, 0))],
        out_specs=pl.BlockSpec((tr, D_IN_PAD), lambda i: (i, 0)),
        compiler_params=_cparams("parallel"),
    )(blocks)


def _w_in_split(padded):
    d = padded.shape[0]
    n = D_IN // N_DEV
    tr = _tile(d, 256, 16)

    def body(p_ref, o_ref):
        for j, col, src, width in _w_in_moves():
            o_ref[j, :, col:col + width] = p_ref[:, src:src + width]

    return pl.pallas_call(
        body, name="w_in_split", out_shape=jax.ShapeDtypeStruct((N_DEV, d, n), padded.dtype), grid=(d // tr,),
        in_specs=[pl.BlockSpec((tr, D_IN_PAD), lambda i: (i, 0))],
        out_specs=pl.BlockSpec((N_DEV, tr, n), lambda i: (0, i, 0)),
        compiler_params=_cparams("parallel"),
    )(padded)


def _small_rows(size):
    return -(-size // (8 * SMALL_COLS)) * 8


def _pack_small(pieces):
    rows = []
    for p in pieces:
        flat = p.reshape(-1)
        rows.append(jnp.pad(flat, (0, _small_rows(flat.shape[0]) * SMALL_COLS - flat.shape[0])).reshape(-1, SMALL_COLS))
    used = sum(r.shape[0] for r in rows)
    rows.append(jnp.zeros((-used % SMALL_ROWS, SMALL_COLS), F32))
    return jnp.concatenate(rows)


def kernel(x, p, mix_pre_norm, mix_post_norm, w_in, b_forget, gm_v_norm, gm_w_s, gm_b_s, mix_out_norm, w_out, ffn_pre_norm, ffn_post_norm, w_ffn_in, w_ffn_out, w_ple, ple_norm, w_ple_gate, loss_target, m_mix_pre_norm, m_mix_post_norm, m_w_in, m_b_forget, m_gm_v_norm, m_gm_w_s, m_gm_b_s, m_mix_out_norm, m_w_out, m_ffn_pre_norm, m_ffn_post_norm, m_w_ffn_in, m_w_ffn_out, m_w_ple, m_ple_norm, m_w_ple_gate, v_mix_pre_norm, v_mix_post_norm, v_w_in, v_b_forget, v_gm_v_norm, v_gm_w_s, v_gm_b_s, v_mix_out_norm, v_w_out, v_ffn_pre_norm, v_ffn_post_norm, v_w_ffn_in, v_w_ffn_out, v_w_ple, v_ple_norm, v_w_ple_gate):
    given = dict(locals())
    weights = {n: given[n] for n in WEIGHT_ORDER}
    mom_m = {n: given["m_" + n] for n in WEIGHT_ORDER}
    mom_v = {n: given["v_" + n] for n in WEIGHT_ORDER}
    depth = w_in.shape[0]
    s, d = x.shape[1], x.shape[2]
    tq = _tile(s, ATT_BLOCK)
    me = 4 * lax.axis_index("x") + 2 * lax.axis_index("y") + lax.axis_index("c")
    tril = jnp.tril(jnp.ones((CHUNK, CHUNK), F32))

    def landing(block):
        return lax.dynamic_update_index_in_dim(lax.empty((N_DEV,) + block.shape, block.dtype), block, me, 0)

    def mix_weights(i, got):
        wt = gm_w_s[i] * tril[None]
        lw = dict(
            w_in=_w_in_assemble(got["w_in"]),
            b_forget=b_forget[i][:, None], wt=wt.astype(BF16), wt_t=wt.transpose(0, 2, 1).astype(BF16),
            bs_t=gm_b_s[i].T)
        lw.update({n: weights[n][i][None] for n in ("mix_pre_norm", "mix_post_norm", "gm_v_norm", "mix_out_norm",
                                                    "ffn_pre_norm", "ffn_post_norm", "ple_norm")})
        return lw

    def rest_weights(got):
        return dict(w_out=got["w_out"].reshape(-1, d), w_ffn_in=got["w_ffn_in"],
                    w_ffn_out=got["w_ffn_out"].reshape(N_DEV // 2, -1, d),
                    w_ple=jnp.concatenate([got["w_ple"][j] for j in range(N_DEV)], axis=-1),
                    w_ple_gate=got["w_ple_gate"].reshape(-1, d))

    def shard_view(n, a):
        return jnp.transpose(a, (0, 2, 1)) if n in TRANSPOSED_WEIGHTS else a

    shards = {n: shard_view(n, weights[n].astype(BF16)) for n in MATRIX_WEIGHTS}

    def gather_start(i):
        started = {}
        order = jnp.zeros((), BF16)
        for tag, grp in EXCHANGE_GROUPS.items():
            started[tag] = _exchange_start([shards[n] for n in grp], [landing(shards[n][i] + order) for n in grp], i,
                                           False, f"weights_gather_start_{i}_{tag}")
            order = started[tag][4][0, 0].astype(BF16)
        return started

    def gather_finish(i, tag, pending, after):
        srcs, got = _exchange_wait(pending[tag], after, i, False, f"weights_gather_wait_{i}_{tag}")
        shards.update(zip(EXCHANGE_GROUPS[tag], srcs))
        return dict(zip(EXCHANGE_GROUPS[tag], got))

    h = x[0]
    saved, layer_w = [], []
    pending = gather_start(0)
    for i in range(depth):
        lw = mix_weights(i, gather_finish(i, "mix", pending, h))
        if i == 0:
            lw["mix_pre_norm"] = lw["mix_pre_norm"] + pending["rest"][4][:1, :1]
        following = {}

        def late(att, i=i, pending=pending, lw=lw, following=following):
            rest = rest_weights(gather_finish(i, "rest", pending, att))
            lw.update(rest)
            if i + 1 == depth:
                return rest
            following.update(gather_start(i + 1))
            token = following["mix"][4][:1, :1] + following["rest"][4][:1, :1]
            return dict(rest, mix_out_norm=lw["mix_out_norm"] + token)

        h, sv = _layer_fwd(h, p[i, 0], lw, tq, late)
        layer_w.append(lw)
        saved.append(sv)
        pending = following

    def loss_head(y, t):
        err = y - t
        return err * (1.0 / d), _colsum(err * err)

    dh, sq = _rowwise(loss_head, [h, loss_target[0]], [], [(d, F32)], [d], "loss_head", ROW_TILE)
    loss = lax.psum(0.5 * jnp.sum(sq) / d, AXES)

    layer_g = [None] * depth
    shard_out = {n: None for n in MATRIX_WEIGHTS}

    def scatter_start(i, tag, g):
        full_g = [g[n] for n in EXCHANGE_GROUPS[tag]]
        lands = [landing(lax.dynamic_index_in_dim(gf, me, 0, keepdims=False)) for gf in full_g]
        return _exchange_start(full_g, lands, None, True, f"grads_scatter_start_{i}_{tag}")

    def scatter_finish(i, tag, started, after):
        _, parts = _exchange_wait(started[tag], after, None, True, f"grads_scatter_wait_{i}_{tag}")
        for n, part in zip(EXCHANGE_GROUPS[tag], parts):
            shard_out[n] = _adamw_shard(shard_view(n, weights[n]), shard_view(n, mom_m[n]), shard_view(n, mom_v[n]),
                                        part, i, shard_out[n], "adamw_" + n)

    before = None
    for i in reversed(range(depth)):
        lw = layer_w[i]
        if before is not None:
            lw = dict(lw, ple_norm=lw["ple_norm"] + before[1]["mix"][4][:1, :1])
        started = {}

        def mid(g, dmc, i=i, before=before, started=started, lw=lw):
            if before is not None:
                scatter_finish(before[0], "rest", before[1], dmc)
            started["rest"] = scatter_start(i, "rest", g)
            return dict(mix_out_norm=lw["mix_out_norm"] + started["rest"][4][:1, :1])

        dh, layer_g[i] = _layer_bwd(dh, saved[i], lw, tq, mid)
        if before is not None:
            scatter_finish(before[0], "mix", before[1], dh)
        started["mix"] = scatter_start(i, "mix", layer_g[i])
        before = (i, started)
    scatter_finish(before[0], "rest", before[1], before[1]["mix"][4])
    scatter_finish(before[0], "mix", before[1], dh)
    grad_x = dh[None]

    grads, deltas, new_m, new_v = {}, {}, {}, {}
    for n in MATRIX_WEIGHTS:
        grads[n], deltas[n], new_m[n], new_v[n] = (shard_view(n, a) for a in shard_out[n])

    small_g = _pack_small([jnp.stack([layer_g[i][n].reshape(-1) for i in range(depth)]) for n in SMALL_WEIGHTS])
    (gathered,) = _all_gather([small_g.astype(BF16)], None, "small_grads_all_gather")
    g_small = _sum_devices(gathered)
    pack = lambda t: _pack_small([t[n] for n in SMALL_WEIGHTS])
    dl, nm, nv = _rowwise(_adamw_math, [pack(weights), g_small, pack(mom_m), pack(mom_v)], [],
                          [(SMALL_COLS, F32)] * 3, [], "adamw_small", SMALL_ROWS)
    row = 0
    for n in SMALL_WEIGHTS:
        shp, size = weights[n].shape, weights[n].size
        grads[n], deltas[n], new_m[n], new_v[n] = (
            a[row:row + _small_rows(size)].reshape(-1)[:size].reshape(shp) for a in (g_small, dl, nm, nv))
        row += _small_rows(size)

    return (loss, grad_x, *[grads[n] for n in WEIGHT_ORDER], *[deltas[n] for n in WEIGHT_ORDER],
            *[new_m[n] for n in WEIGHT_ORDER], *[new_v[n] for n in WEIGHT_ORDER])
```

```python
import functools
import math

import jax
import jax.numpy as jnp
from jax import lax
from jax.experimental import pallas as pl
from jax.experimental.pallas import tpu as pltpu

F32 = jnp.float32
BF16 = jnp.bfloat16
MESH = pl.DeviceIdType.MESH
AXES = ("x", "y", "c")
N_DEV = 8

EPS = 1e-6
NEG_INF = -1e30
N_HEADS = 8
HEAD_DIM = 64
D_ATT = N_HEADS * HEAD_DIM
N_GROUPS = 8
GROUP_DIM = 64
D_GM = N_GROUPS * GROUP_DIM
CHUNK = 128
ATT_SCALE = HEAD_DIM ** -0.5
ATT_BLOCK = 1024
D_IN = 3 * D_ATT + N_HEADS + 2 * D_GM
D_IN_PAD = 3 * D_ATT + 2 * D_GM + 128
F_OFF = 3 * D_ATT + 2 * D_GM

ADAM_LR = 0.001
ADAM_B1 = 0.9
ADAM_B2 = 0.999
ADAM_EPS = 1e-08
ADAM_WD = 0.01
ADAM_STEP = 10

LANE = 128
VMEM_LIMIT = 48 * 1024 * 1024
ROW_TILE = 512
K_TILE = 4096
SMALL_COLS = 128
SMALL_ROWS = 512

MATRIX_WEIGHTS = ("w_in", "w_out", "w_ffn_in", "w_ffn_out", "w_ple", "w_ple_gate")
EXCHANGE_GROUPS = {"mix": ("w_in",), "rest": ("w_out", "w_ffn_in", "w_ffn_out", "w_ple", "w_ple_gate")}
TRANSPOSED_WEIGHTS = ("w_ffn_in",)
SMALL_WEIGHTS = ("mix_pre_norm", "mix_post_norm", "b_forget", "gm_v_norm", "gm_w_s", "gm_b_s",
                 "mix_out_norm", "ffn_pre_norm", "ffn_post_norm", "ple_norm")
WEIGHT_ORDER = ("mix_pre_norm", "mix_post_norm", "w_in", "b_forget", "gm_v_norm", "gm_w_s", "gm_b_s",
                "mix_out_norm", "w_out", "ffn_pre_norm", "ffn_post_norm", "w_ffn_in", "w_ffn_out",
                "w_ple", "ple_norm", "w_ple_gate")


def _tile(n, pref, unit=LANE):
    best = None
    t = unit
    while t <= min(n, pref):
        if n % t == 0:
            best = t
        t += unit
    return n if best is None else best


def _cparams(*semantics):
    return pltpu.CompilerParams(dimension_semantics=semantics or None, vmem_limit_bytes=VMEM_LIMIT)


NN = (((1,), (0,)), ((), ()))
NT = (((1,), (1,)), ((), ()))
TN = (((0,), (0,)), ((), ()))
_MM_AXES = {
    "nn": ("i", "k", "k", "j"), "nt": ("i", "k", "j", "k"), "tn": ("k", "i", "k", "j")}
_MM_DN = {"nn": NN, "nt": NT, "tn": TN}


def _mm(a, b, dims, out_dtype, name, a3=None, b3=None, o3=None, tm=1024, tn=1024, tk=None):
    ar, ac, br, bc = _MM_AXES[dims]
    letter = {"i": "m", "j": "n", "k": "k"}
    size = {}

    def measure(x, rows, cols, stacked):
        shape = x.shape
        if stacked is None:
            size.setdefault(letter[rows], shape[0])
            size.setdefault(letter[cols], shape[1])
        else:
            for ax, n in ((rows, shape[1]), (cols, shape[2])):
                size.setdefault(letter[ax], n * shape[0] if letter[ax] == stacked else n)

    measure(a, ar, ac, a3)
    measure(b, br, bc, b3)
    m, n, k = size["m"], size["n"], size["k"]
    slab = {}
    for x, stacked, rows, cols in ((a, a3, ar, ac), (b, b3, br, bc)):
        if stacked is not None:
            slab[stacked] = x.shape[1] if letter[rows] == stacked else x.shape[2]
    if o3 is not None:
        slab.setdefault(o3, slab.get(o3, None) or {"m": m, "n": n}[o3] // N_DEV)
    tk = tk or K_TILE
    tile = {"m": slab.get("m") or _tile(m, tm), "n": slab.get("n") or _tile(n, tn), "k": slab.get("k") or _tile(k, tk)}
    group = 1
    if a3 == "k" and b3 == "k":
        group = max(g for g in range(1, a.shape[0] + 1) if a.shape[0] % g == 0 and g * tile["k"] <= max(tk, tile["k"]))
    nk = k // (group * tile["k"])

    def spec(rows, cols, stacked):
        tr, tc = tile[letter[rows]], tile[letter[cols]]
        if stacked is None:
            return pl.BlockSpec((tr, tc), lambda i, j, kk: ({"i": i, "j": j, "k": kk}[rows], {"i": i, "j": j, "k": kk}[cols]))

        def imap(i, j, kk):
            g = {"i": i, "j": j, "k": kk}
            return (g[{"m": "i", "n": "j", "k": "k"}[stacked]],
                    0 if letter[rows] == stacked else g[rows], 0 if letter[cols] == stacked else g[cols])

        return pl.BlockSpec((group if stacked == "k" else 1, tr, tc), imap)

    dn = _MM_DN[dims]

    def body(a_ref, b_ref, o_ref, *acc):
        prod = None
        for g in range(group):
            av = a_ref[...] if a3 is None else a_ref[g]
            bv = b_ref[...] if b3 is None else b_ref[g]
            term = lax.dot_general(av.astype(BF16), bv.astype(BF16), dn, preferred_element_type=F32)
            prod = term if prod is None else prod + term

        def emit(val):
            if o3 is None:
                o_ref[...] = val.astype(out_dtype)
            else:
                o_ref[0] = val.astype(out_dtype)

        if nk == 1:
            emit(prod)
            return
        (acc_ref,) = acc
        kk = pl.program_id(2)

        @pl.when(kk == 0)
        def _():
            acc_ref[...] = prod

        @pl.when(kk > 0)
        def _():
            acc_ref[...] += prod

        @pl.when(kk == nk - 1)
        def _():
            emit(acc_ref[...])

    if o3 is None:
        out_shape = (m, n)
    elif o3 == "m":
        out_shape = (m // tile["m"], tile["m"], n)
    else:
        out_shape = (n // tile["n"], m, tile["n"])
    return pl.pallas_call(
        body, name=name, out_shape=jax.ShapeDtypeStruct(out_shape, out_dtype),
        grid=(m // tile["m"], n // tile["n"], nk),
        in_specs=[spec(ar, ac, a3), spec(br, bc, b3)], out_specs=spec("i", "j", o3),
        scratch_shapes=[] if nk == 1 else [pltpu.VMEM((tile["m"], tile["n"]), F32)],
        compiler_params=_cparams("parallel", "parallel", "arbitrary"),
    )(a, b)


def _rowwise(fn, rows, vecs, outs, reds, name, ts):
    rows = [r if isinstance(r, tuple) else (r, 0, r.shape[1]) for r in rows]
    s = rows[0][0].shape[0]
    ts = _tile(s, ts, 8)
    nr, nv, no = len(rows), len(vecs), len(outs)

    def body(*refs):
        vals = fn(*[r[...] for r in refs[:nr + nv]])
        vals = vals if isinstance(vals, tuple) else (vals,)
        o_refs = refs[nr + nv:nr + nv + no]
        r_refs = refs[nr + nv + no:]
        for o_ref, val in zip(o_refs, vals[:no]):
            o_ref[...] = val.astype(o_ref.dtype)
        if r_refs:
            @pl.when(pl.program_id(0) == 0)
            def _():
                for r_ref in r_refs:
                    r_ref[...] = jnp.zeros_like(r_ref)

            for r_ref, val in zip(r_refs, vals[no:]):
                r_ref[...] += val

    in_specs = [pl.BlockSpec((ts, w), functools.partial(lambda i, cb: (i, cb), cb=cb)) for _, cb, w in rows]
    in_specs += [pl.BlockSpec(v.shape, lambda i: (0, 0)) for v in vecs]
    out_specs = [pl.BlockSpec((ts, c), lambda i: (i, 0)) for c, _ in outs]
    out_specs += [pl.BlockSpec((1, c), lambda i: (0, 0)) for c in reds]
    out_shape = [jax.ShapeDtypeStruct((s, c), dt) for c, dt in outs]
    out_shape += [jax.ShapeDtypeStruct((1, c), F32) for c in reds]
    return pl.pallas_call(
        body, name=name, out_shape=out_shape, grid=(s // ts,), in_specs=in_specs, out_specs=out_specs,
        compiler_params=_cparams("arbitrary" if reds else "parallel"),
    )(*[r[0] for r in rows], *vecs)


def _rms(x):
    x = x.astype(F32)
    r = lax.rsqrt(jnp.mean(x * x, axis=-1, keepdims=True) + EPS)
    return x * r, r


def _rms_bwd(xhat, r, dyg):
    return r * (dyg - xhat * jnp.mean(dyg * xhat, axis=-1, keepdims=True))


def _colsum(x):
    return jnp.sum(x, axis=0, keepdims=True)


def _sigmoid(x):
    return 1.0 / (1.0 + jnp.exp(-x.astype(F32)))


GELU_C = math.sqrt(2.0 / math.pi)
GELU_A = 0.044715


def _gelu(x):
    return 0.5 * x * (1.0 + jnp.tanh(GELU_C * (x + GELU_A * x * x * x)))


def _gelu_grad(x):
    t = jnp.tanh(GELU_C * (x + GELU_A * x * x * x))
    return 0.5 * (1.0 + t) + 0.5 * x * (1.0 - t * t) * GELU_C * (1.0 + 3.0 * GELU_A * x * x)


def _ffn_in_swiglu(hn, wg):
    s, d = hn.shape
    g2, n, _ = wg.shape
    g = g2 // 2
    tm = _tile(s, 1024)

    def body(h_ref, wa_ref, wb_ref, ab_ref, t_ref):
        hv = h_ref[...]
        a = lax.dot_general(hv, wa_ref[0], NT, preferred_element_type=F32)
        b = lax.dot_general(hv, wb_ref[0], NT, preferred_element_type=F32)
        ab_ref[0, 0] = a.astype(BF16)
        ab_ref[1, 0] = b.astype(BF16)
        t_ref[0] = (a * _sigmoid(a) * b).astype(BF16)

    return pl.pallas_call(
        body, name="mm_ffn_in_swiglu",
        out_shape=(jax.ShapeDtypeStruct((2, g, s, n), BF16), jax.ShapeDtypeStruct((g, s, n), BF16)),
        grid=(s // tm, g),
        in_specs=[pl.BlockSpec((tm, d), lambda i, j: (i, 0)), pl.BlockSpec((1, n, d), lambda i, j: (j, 0, 0)),
                  pl.BlockSpec((1, n, d), lambda i, j: (j + g, 0, 0))],
        out_specs=(pl.BlockSpec((2, 1, tm, n), lambda i, j: (0, j, i, 0)),
                   pl.BlockSpec((1, tm, n), lambda i, j: (j, i, 0))),
        compiler_params=_cparams("parallel", "parallel"),
    )(hn, wg, wg)


def _ffn_out_swiglu_bwd(dy, w4, ab):
    s, d = dy.shape
    _, g, _, n = ab.shape
    tm = _tile(s, 1024)

    def body(dy_ref, w_ref, ab_ref, t_ref, dab_ref):
        dt = lax.dot_general(dy_ref[...], w_ref[0], NT, preferred_element_type=F32)
        a = ab_ref[0, 0].astype(F32)
        b = ab_ref[1, 0].astype(F32)
        sig = _sigmoid(a)
        silu = a * sig
        t_ref[0] = (silu * b).astype(BF16)
        dab_ref[0, 0] = (dt * b * (sig * (1.0 + a * (1.0 - sig)))).astype(BF16)
        dab_ref[1, 0] = (dt * silu).astype(BF16)

    both = pl.BlockSpec((2, 1, tm, n), lambda i, j: (0, j, i, 0))
    one = pl.BlockSpec((1, tm, n), lambda i, j: (j, i, 0))
    return pl.pallas_call(
        body, name="mm_dt_swiglu_bwd",
        out_shape=(jax.ShapeDtypeStruct((g, s, n), BF16), jax.ShapeDtypeStruct((2, g, s, n), BF16)),
        grid=(s // tm, g),
        in_specs=[pl.BlockSpec((tm, d), lambda i, j: (i, 0)), pl.BlockSpec((1, n, d), lambda i, j: (j, 0, 0)), both],
        out_specs=(one, both),
        compiler_params=_cparams("parallel", "parallel"),
    )(dy, w4, ab)


def _forget_fwd(fl_t, b_col):
    h, s = fl_t.shape
    nb = s // LANE

    def body(fl_ref, b_ref, c_ref):
        upper = (lax.broadcasted_iota(jnp.int32, (LANE, LANE), 0)
                 <= lax.broadcasted_iota(jnp.int32, (LANE, LANE), 1)).astype(F32)

        def step(i, carry):
            x = fl_ref[i] + b_ref[...]
            lf = jnp.minimum(x, 0.0) - jnp.log(1.0 + jnp.exp(-jnp.abs(x)))
            cs = jnp.dot(lf, upper, precision=lax.Precision.HIGHEST, preferred_element_type=F32) + carry
            c_ref[i] = cs
            return cs[:, LANE - 1:LANE]

        lax.fori_loop(0, nb, step, jnp.zeros((h, 1), F32))

    out = pl.pallas_call(
        body, name="forget_fwd", out_shape=jax.ShapeDtypeStruct((nb, h, LANE), F32),
        compiler_params=_cparams(),
    )(fl_t.reshape(h, nb, LANE).transpose(1, 0, 2), b_col)
    return out.transpose(1, 0, 2).reshape(h, s)


def _forget_bwd(dc_t, fl_t, b_col):
    h, s = fl_t.shape
    nb = s // LANE

    def body(dc_ref, fl_ref, b_ref, dfl_ref, db_ref):
        lower = (lax.broadcasted_iota(jnp.int32, (LANE, LANE), 0)
                 >= lax.broadcasted_iota(jnp.int32, (LANE, LANE), 1)).astype(F32)

        def step(t, carry):
            tail, db = carry
            i = nb - 1 - t
            rc = jnp.dot(dc_ref[i], lower, precision=lax.Precision.HIGHEST, preferred_element_type=F32) + tail
            dfl = rc * (1.0 - _sigmoid(fl_ref[i] + b_ref[...]))
            dfl_ref[i] = dfl
            return rc[:, 0:1], db + jnp.sum(dfl, axis=1, keepdims=True)

        _, db = lax.fori_loop(0, nb, step, (jnp.zeros((h, 1), F32), jnp.zeros((h, 1), F32)))
        db_ref[...] = db

    blocked = lambda a: a.reshape(h, nb, LANE).transpose(1, 0, 2)
    dfl, db = pl.pallas_call(
        body, name="forget_bwd",
        out_shape=(jax.ShapeDtypeStruct((nb, h, LANE), F32), jax.ShapeDtypeStruct((h, 1), F32)),
        compiler_params=_cparams(),
    )(blocked(dc_t), blocked(fl_t), b_col)
    return dfl.transpose(1, 0, 2).reshape(h, s), db


N_PAIRS = N_HEADS // 2


def _causal_mask(t):
    return lax.broadcasted_iota(jnp.int32, (t, t), 0) >= lax.broadcasted_iota(jnp.int32, (t, t), 1)


def _head_lanes():
    return lax.broadcasted_iota(jnp.int32, (1, 2 * HEAD_DIM), 1) < HEAD_DIM


def _pick(x2, first, hh):
    zero = jnp.zeros_like(x2)
    return jnp.where(first, x2, zero) if hh == 0 else jnp.where(first, zero, x2)


BIAS_TERMS = 3


def _attn_prep(z, c, ts):
    s = z.shape[0]
    ts = _tile(s, ts, 16)
    w = 2 * HEAD_DIM

    def body(q_ref, k_ref, v_ref, c_ref, qa_ref, ka_ref, vb_ref):
        lane = lax.broadcasted_iota(jnp.int32, (1, w), 1)
        first = lane < HEAD_DIM
        cv = c_ref[...]
        for h in range(N_HEADS):
            pair = slice((h // 2) * w, (h // 2 + 1) * w)
            qh = q_ref[:, pair] * ATT_SCALE
            kh = k_ref[:, pair]
            if h % 2:
                qh = pltpu.roll(qh, HEAD_DIM, 1)
                kh = pltpu.roll(kh, HEAD_DIM, 1)
            rest = cv[:, h:h + 1]
            q_tail = jnp.zeros((1, w), F32)
            k_tail = jnp.zeros((1, w), F32)
            for t in range(BIAS_TERMS):
                term = rest.astype(BF16).astype(F32)
                rest = rest - term
                q_tail = jnp.where(lane == HEAD_DIM + t, term, jnp.where(lane == HEAD_DIM + BIAS_TERMS + t, 1.0, q_tail))
                k_tail = jnp.where(lane == HEAD_DIM + t, 1.0, jnp.where(lane == HEAD_DIM + BIAS_TERMS + t, -term, k_tail))
            qa_ref[:, h * w:(h + 1) * w] = jnp.where(first, qh, q_tail).astype(BF16)
            ka_ref[:, h * w:(h + 1) * w] = jnp.where(first, kh, k_tail).astype(BF16)
        vb_ref[...] = v_ref[...].astype(BF16)

    col = lambda cb: pl.BlockSpec((ts, D_ATT), lambda i: (i, cb))
    wide = pl.BlockSpec((ts, N_HEADS * w), lambda i: (i, 0))
    return pl.pallas_call(
        body, name="attn_prep",
        out_shape=(jax.ShapeDtypeStruct((s, N_HEADS * w), BF16), jax.ShapeDtypeStruct((s, N_HEADS * w), BF16),
                   jax.ShapeDtypeStruct((s, D_ATT), BF16)),
        grid=(s // ts,), in_specs=[col(0), col(1), col(2), pl.BlockSpec((ts, N_HEADS), lambda i: (i, 0))],
        out_specs=(wide, wide, col(0)),
        compiler_params=_cparams("parallel"),
    )(z, z, z, c)


def _attn_fwd(qa, ka, vb, tq):
    s = qa.shape[0]
    nq = s // tq
    w = 2 * HEAD_DIM

    def body(q_ref, k_ref, v_ref, o_ref, lse_ref):
        i = pl.program_id(1)
        first = _head_lanes()
        q2 = q_ref[...]

        def block(j, carry, masked):
            off = pl.multiple_of(j * tq, tq)
            k2 = k_ref[pl.ds(off, tq), :]
            v2 = v_ref[pl.ds(off, tq), :]
            new = []
            for hh in range(2):
                m, l, acc = carry[hh]
                sc = lax.dot_general(q2[:, hh * w:(hh + 1) * w], k2[:, hh * w:(hh + 1) * w], NT,
                                     preferred_element_type=F32)
                if masked:
                    sc = jnp.where(_causal_mask(tq), sc, NEG_INF)
                m_new = jnp.maximum(m, jnp.max(sc, axis=-1, keepdims=True))
                alpha = jnp.exp(m - m_new)
                p = jnp.exp(sc - m_new)
                l = alpha * l + jnp.sum(p, axis=-1, keepdims=True)
                p_hi = p.astype(BF16)
                p_lo = (p - p_hi.astype(F32)).astype(BF16)
                acc = (alpha * acc + jnp.dot(p_hi, v2, preferred_element_type=F32)
                       + jnp.dot(p_lo, v2, preferred_element_type=F32))
                new.append((m_new, l, acc))
            return tuple(new)

        one = (jnp.full((tq, 1), NEG_INF, F32), jnp.zeros((tq, 1), F32), jnp.zeros((tq, w), F32))
        carry = lax.fori_loop(0, i, lambda j, c: block(j, c, False), (one, one))
        (m0, l0, a0), (m1, l1, a1) = block(i, carry, True)
        o_ref[...] = jnp.where(first, a0 / l0, a1 / l1)
        lse_ref[0] = m0 + jnp.log(l0)
        lse_ref[1] = m1 + jnp.log(l1)

    return pl.pallas_call(
        body, name="attn_fwd",
        out_shape=(jax.ShapeDtypeStruct((s, D_ATT), F32), jax.ShapeDtypeStruct((N_HEADS, s, 1), F32)),
        grid=(N_PAIRS, nq),
        in_specs=[pl.BlockSpec((tq, 2 * w), lambda hp, i: (i, hp)),
                  pl.BlockSpec((s, 2 * w), lambda hp, i: (0, hp)),
                  pl.BlockSpec((s, w), lambda hp, i: (0, hp))],
        out_specs=(pl.BlockSpec((tq, w), lambda hp, i: (i, hp)),
                   pl.BlockSpec((2, tq, 1), lambda hp, i: (hp, i, 0))),
        compiler_params=_cparams("parallel", "parallel"),
    )(qa, ka, vb)


def _attn_delta(o, do, tq):
    s = o.shape[0]
    w = 2 * HEAD_DIM

    def body(o_ref, do_ref, d_ref):
        first = _head_lanes()
        prod = o_ref[...] * do_ref[...].astype(F32)
        d_ref[0] = jnp.sum(_pick(prod, first, 0), axis=-1, keepdims=True)
        d_ref[1] = jnp.sum(_pick(prod, first, 1), axis=-1, keepdims=True)

    blk = pl.BlockSpec((tq, w), lambda hp, i: (i, hp))
    return pl.pallas_call(
        body, name="attn_delta", out_shape=jax.ShapeDtypeStruct((N_HEADS, s, 1), F32), grid=(N_PAIRS, s // tq),
        in_specs=[blk, blk], out_specs=pl.BlockSpec((2, tq, 1), lambda hp, i: (hp, i, 0)),
        compiler_params=_cparams("parallel", "parallel"),
    )(o, do)


def _attn_bwd(qa, ka, vb, do, lse, delta, tq):
    s = qa.shape[0]
    nq = s // tq
    w = 2 * HEAD_DIM

    def body(q_ref, do_ref, lse_ref, dl_ref, k_ref, v_ref, dq_ref, dk_ref, dv_ref, dc_ref, dq_acc):
        j = pl.program_id(1)
        first = _head_lanes()

        @pl.when(j == 0)
        def _():
            dq_acc[...] = jnp.zeros_like(dq_acc)

        k2 = k_ref[...]
        v2 = v_ref[...]

        def step(i, carry, masked):
            off = pl.multiple_of(i * tq, tq)
            rows = pl.ds(off, tq)
            q2 = q_ref[rows, :]
            do2 = do_ref[rows, :]
            new, dqs = [], []
            for hh in range(2):
                dk, dv, dcs = carry[hh]
                qh = q2[:, hh * w:(hh + 1) * w]
                kh = k2[:, hh * w:(hh + 1) * w]
                sc = lax.dot_general(qh, kh, NT, preferred_element_type=F32)
                if masked:
                    sc = jnp.where(_causal_mask(tq), sc, NEG_INF)
                p = jnp.exp(sc - lse_ref[hh, rows, :])
                dv = dv + lax.dot_general(do2, p.astype(BF16), TN, preferred_element_type=F32)
                dp = lax.dot_general(_pick(do2, first, hh), v2, NT, preferred_element_type=F32)
                ds = p * (dp - dl_ref[hh, rows, :])
                dsb = ds.astype(BF16)
                dk = dk + lax.dot_general(qh, dsb, TN, preferred_element_type=F32)
                dqs.append(jnp.dot(dsb, kh, preferred_element_type=F32))
                new.append((dk, dv, dcs + jnp.sum(ds, axis=0, keepdims=True)))
            dq_acc[rows, :] += jnp.where(first, dqs[0], pltpu.roll(dqs[1], HEAD_DIM, 1)) * ATT_SCALE
            return tuple(new)

        one = (jnp.zeros((w, tq), F32), jnp.zeros((w, tq), F32), jnp.zeros((1, tq), F32))
        carry = step(j, (one, one), True)
        (dk0, dv0, dc0), (dk1, dv1, dc1) = lax.fori_loop(j + 1, nq, lambda i, c: step(i, c, False), carry)
        dk_ref[...] = jnp.where(first, dk0.T, pltpu.roll(dk1.T, HEAD_DIM, 1)).astype(BF16)
        dv_ref[...] = jnp.where(first, dv0.T, dv1.T).astype(BF16)
        dc_ref[0, 0] = -dc0
        dc_ref[1, 0] = -dc1

        @pl.when(j == nq - 1)
        def _():
            dq_ref[...] = dq_acc[...].astype(BF16)

    whole = lambda width: pl.BlockSpec((s, width), lambda hp, j: (0, hp))
    whole_heads = pl.BlockSpec((2, s, 1), lambda hp, j: (hp, 0, 0))
    blk = lambda width: pl.BlockSpec((tq, width), lambda hp, j: (j, hp))
    crow = pl.BlockSpec((2, 1, 1, tq), lambda hp, j: (hp, j, 0, 0))
    return pl.pallas_call(
        body, name="attn_bwd",
        out_shape=(jax.ShapeDtypeStruct((s, D_ATT), BF16), jax.ShapeDtypeStruct((s, D_ATT), BF16),
                   jax.ShapeDtypeStruct((s, D_ATT), BF16), jax.ShapeDtypeStruct((N_HEADS, nq, 1, tq), F32)),
        grid=(N_PAIRS, nq),
        in_specs=[whole(2 * w), whole(w), whole_heads, whole_heads, blk(2 * w), blk(w)],
        out_specs=(whole(w), blk(w), blk(w), crow),
        scratch_shapes=[pltpu.VMEM((s, w), F32)],
        compiler_params=_cparams("parallel", "arbitrary"),
    )(qa, do, lse, delta, ka, vb)


def _pair_sums(x, first):
    total = jnp.sum(x, axis=-1, keepdims=True)
    head = jnp.sum(jnp.where(first, x, 0.0), axis=-1, keepdims=True)
    return head, total - head


def _pair_mean(x, first):
    head, tail = _pair_sums(x, first)
    return jnp.where(first, head, tail) * (1.0 / GROUP_DIM)


def _gm_pair_norm(v2, first):
    d = v2 - _pair_mean(v2, first)
    rstd = lax.rsqrt(_pair_mean(d * d, first) + EPS)
    return d * rstd, rstd


def _gm_pair_mix(w_ref, pr, rhs, first):
    return jnp.where(first, jnp.dot(w_ref[2 * pr], rhs, preferred_element_type=F32),
                     jnp.dot(w_ref[2 * pr + 1], rhs, preferred_element_type=F32))


GM_STEP = 4 * CHUNK


def _gmlp_fwd(z, wt, bs_t, vgain):
    s = z.shape[0]
    step = _tile(s, GM_STEP, CHUNK)

    def body(gu_ref, gv_ref, wt_ref, bs_ref, vg_ref, o_ref):
        first = _head_lanes()
        for c in range(step // CHUNK):
            rows = slice(c * CHUNK, (c + 1) * CHUNK)
            for pr in range(N_GROUPS // 2):
                sl = slice(2 * pr * GROUP_DIM, 2 * (pr + 1) * GROUP_DIM)
                vhat, _ = _gm_pair_norm(_gelu(gv_ref[rows, sl]), first)
                vn = (vhat * vg_ref[:, sl]).astype(BF16)
                bias = jnp.where(first, bs_ref[:, 2 * pr:2 * pr + 1], bs_ref[:, 2 * pr + 1:2 * pr + 2])
                o_ref[rows, sl] = _gelu(gu_ref[rows, sl]) * (_gm_pair_mix(wt_ref, pr, vn, first) + bias)

    full = lambda a: pl.BlockSpec(a.shape, lambda n: (0,) * a.ndim)
    return pl.pallas_call(
        body, name="gmlp_fwd", out_shape=jax.ShapeDtypeStruct((s, D_GM), F32), grid=(s // step,),
        in_specs=[pl.BlockSpec((step, D_GM), lambda n: (n, 3)), pl.BlockSpec((step, D_GM), lambda n: (n, 4)),
                  full(wt), full(bs_t), full(vgain)],
        out_specs=pl.BlockSpec((step, D_GM), lambda n: (n, 0)),
        compiler_params=_cparams("parallel"),
    )(z, z, wt, bs_t, vgain)


def _gmlp_bwd(z, dgm, wt, wt_t, bs_t, vgain):
    s = z.shape[0]
    step = _tile(s, GM_STEP, CHUNK)

    def body(gu_ref, gv_ref, dgm_ref, wt_ref, wtt_ref, bs_ref, vg_ref, dgu_ref, dgv_ref, dwt_ref, dbs_ref, dvg_ref):
        @pl.when(pl.program_id(0) == 0)
        def _():
            dwt_ref[...] = jnp.zeros_like(dwt_ref)
            dbs_ref[...] = jnp.zeros_like(dbs_ref)
            dvg_ref[...] = jnp.zeros_like(dvg_ref)

        first = _head_lanes()
        for c in range(step // CHUNK):
            rows = slice(c * CHUNK, (c + 1) * CHUNK)
            for pr in range(N_GROUPS // 2):
                g0, g1 = 2 * pr, 2 * pr + 1
                sl = slice(g0 * GROUP_DIM, (g1 + 1) * GROUP_DIM)
                gu = gu_ref[rows, sl]
                gv = gv_ref[rows, sl]
                dgm = dgm_ref[rows, sl]
                vhat, rstd = _gm_pair_norm(_gelu(gv), first)
                gain = vg_ref[:, sl]
                vn = (vhat * gain).astype(BF16)
                bias = jnp.where(first, bs_ref[:, g0:g0 + 1], bs_ref[:, g1:g1 + 1])
                mixed = _gm_pair_mix(wt_ref, pr, vn, first) + bias
                dgu_ref[rows, sl] = (dgm * mixed * _gelu_grad(gu)).astype(BF16)
                dmixed = dgm * _gelu(gu)
                db0, db1 = _pair_sums(dmixed, first)
                dbs_ref[:, g0:g0 + 1] += db0
                dbs_ref[:, g1:g1 + 1] += db1
                dwt_ref[g0] += lax.dot_general(_pick(dmixed, first, 0).astype(BF16), vn, NT, preferred_element_type=F32)
                dwt_ref[g1] += lax.dot_general(_pick(dmixed, first, 1).astype(BF16), vn, NT, preferred_element_type=F32)
                dvn = _gm_pair_mix(wtt_ref, pr, dmixed.astype(BF16), first)
                dvg_ref[:, sl] += _colsum(dvn * vhat)
                dvhat = dvn * gain
                dvf = rstd * (dvhat - _pair_mean(dvhat, first) - vhat * _pair_mean(dvhat * vhat, first))
                dgv_ref[rows, sl] = (dvf * _gelu_grad(gv)).astype(BF16)

    full = lambda a: pl.BlockSpec(a.shape, lambda n: (0,) * a.ndim)
    rows_spec = pl.BlockSpec((step, D_GM), lambda n: (n, 0))
    return pl.pallas_call(
        body, name="gmlp_bwd",
        out_shape=(jax.ShapeDtypeStruct((s, D_GM), BF16), jax.ShapeDtypeStruct((s, D_GM), BF16),
                   jax.ShapeDtypeStruct(wt.shape, F32), jax.ShapeDtypeStruct(bs_t.shape, F32),
                   jax.ShapeDtypeStruct(vgain.shape, F32)),
        grid=(s // step,),
        in_specs=[pl.BlockSpec((step, D_GM), lambda n: (n, 3)), pl.BlockSpec((step, D_GM), lambda n: (n, 4)),
                  rows_spec, full(wt), full(wt_t), full(bs_t), full(vgain)],
        out_specs=(rows_spec, rows_spec, full(wt), full(bs_t), full(vgain)),
        compiler_params=_cparams("arbitrary"),
    )(z, z, dgm, wt, wt_t, bs_t, vgain)


def _dz_concat(wide, dfl, ts):
    s = dfl.shape[0]
    ts = _tile(s, ts, 16)
    n = len(wide)

    def body(*refs):
        o_ref = refs[-1]
        for k in range(n):
            o_ref[:, k * D_ATT:(k + 1) * D_ATT] = refs[k][...]
        o_ref[:, F_OFF:] = jnp.zeros((ts, D_IN_PAD - F_OFF), BF16)
        o_ref[:, F_OFF:F_OFF + N_HEADS] = refs[n][...].astype(BF16)

    return pl.pallas_call(
        body, name="dz_concat", out_shape=jax.ShapeDtypeStruct((s, D_IN_PAD), BF16), grid=(s // ts,),
        in_specs=[pl.BlockSpec((ts, D_ATT), lambda i: (i, 0))] * n + [pl.BlockSpec((ts, N_HEADS), lambda i: (i, 0))],
        out_specs=pl.BlockSpec((ts, D_IN_PAD), lambda i: (i, 0)),
        compiler_params=_cparams("parallel"),
    )(*wide, dfl)


def _layer_fwd(h0, p_i, w, tq, late):
    s, d = h0.shape
    sv = {"h0": h0}

    (hn1,) = _rowwise(lambda h, g: _rms(h)[0] * g, [h0], [w["mix_pre_norm"]], [(d, BF16)], [], "pre_mix", ROW_TILE)
    z = _mm(hn1, w["w_in"], "nn", F32, "mm_in")
    fl_t = z[:, F_OFF:F_OFF + N_HEADS].T
    c_t = _forget_fwd(fl_t, w["b_forget"])
    qa, ka, vb = _attn_prep(z, c_t.T, ROW_TILE)
    att, lse = _attn_fwd(qa, ka, vb, tq)
    gm = _gmlp_fwd(z, w["wt"], w["bs_t"], w["gm_v_norm"])
    w = dict(w, **late(att))

    def mix_out(att, gm, g):
        return jnp.concatenate([_rms(att)[0] * g[:, :D_ATT], _rms(gm)[0] * g[:, D_ATT:]], axis=-1)

    (mc,) = _rowwise(mix_out, [att, gm], [w["mix_out_norm"]], [(D_ATT + D_GM, BF16)], [], "mix_out", ROW_TILE)
    y1 = _mm(mc, w["w_out"], "nn", BF16, "mm_out")

    def post_mix(h0, y1, gpost, gpre):
        h1 = h0 + _rms(y1)[0] * gpost
        return h1, _rms(h1)[0] * gpre

    h1, hn2 = _rowwise(post_mix, [h0, y1], [w["mix_post_norm"], w["ffn_pre_norm"]],
                       [(d, F32), (d, BF16)], [], "post_mix", ROW_TILE)
    ab, t = _ffn_in_swiglu(hn2, w["w_ffn_in"])
    y2 = _mm(t, w["w_ffn_out"], "nn", BF16, "mm_ffn_out", a3="k", b3="k")

    def post_ffn(h1, y2, g):
        h2 = h1 + _rms(y2)[0] * g
        return h2, _rms(h2)[0]

    h2, hr = _rowwise(post_ffn, [h1, y2], [w["ffn_post_norm"]], [(d, F32), (d, BF16)], [], "post_ffn", ROW_TILE)
    gl = _mm(hr, w["w_ple_gate"], "nn", BF16, "mm_gate")
    pe = _mm(p_i, w["w_ple"], "nn", BF16, "mm_ple")
    (h3,) = _rowwise(lambda h2, gl, pe, g: h2 + _sigmoid(gl) * (_rms(pe)[0] * g), [h2, gl, pe], [w["ple_norm"]],
                     [(d, F32)], [], "ple_out", ROW_TILE)
    sv.update(hn1=hn1, z=z, fl_t=fl_t, qa=qa, ka=ka, vb=vb, lse=lse, att=att, gm=gm,
              mc=mc, y1=y1, h1=h1, hn2=hn2, ab=ab, y2=y2, h2=h2, hr=hr, gl=gl, pe=pe, p_i=p_i)
    return h3, sv


def _layer_bwd(dh3, sv, w, tq, mid):
    s, d = dh3.shape
    g = {}
    by_rows = lambda a: a.reshape(N_DEV, -1, a.shape[-1])
    by_cols = lambda a: jnp.stack(jnp.split(a, N_DEV, axis=-1))

    def ple_bwd(dh3, gl, pe, gple):
        gate = _sigmoid(gl)
        pehat, rpe = _rms(pe)
        dgl = dh3 * (pehat * gple) * gate * (1.0 - gate)
        de = dh3 * gate
        return dgl, _rms_bwd(pehat, rpe, de * gple), _colsum(de * pehat)

    dgl, dpe, g["ple_norm"] = _rowwise(ple_bwd, [dh3, sv["gl"], sv["pe"]], [w["ple_norm"]],
                                       [(d, BF16), (d, BF16)], [d], "ple_bwd", ROW_TILE)
    g["w_ple_gate"] = by_rows(_mm(sv["hr"], dgl, "tn", BF16, "mm_dgate"))
    dhr = _mm(dgl, w["w_ple_gate"], "nt", BF16, "mm_dhr")
    g["w_ple"] = by_cols(_mm(sv["p_i"], dpe, "tn", BF16, "mm_dple"))

    def ffn_post_bwd(dh3, dhr, h2, y2, gpost):
        h2hat, r2 = _rms(h2)
        dh2 = dh3 + _rms_bwd(h2hat, r2, dhr)
        y2hat, ry = _rms(y2)
        return dh2, _rms_bwd(y2hat, ry, dh2 * gpost), _colsum(dh2 * y2hat)

    dh2, dy2, g["ffn_post_norm"] = _rowwise(ffn_post_bwd, [dh3, dhr, sv["h2"], sv["y2"]], [w["ffn_post_norm"]],
                                            [(d, F32), (d, BF16)], [d], "ffn_post_bwd", ROW_TILE)
    t, dab = _ffn_out_swiglu_bwd(dy2, w["w_ffn_out"], sv["ab"])
    dab = dab.reshape((N_DEV,) + dab.shape[2:])
    g["w_ffn_out"] = by_rows(_mm(t, dy2, "tn", BF16, "mm_dffn_out", a3="m", o3="m"))
    dhn2 = _mm(dab, w["w_ffn_in"], "nn", BF16, "mm_dhn2", a3="k", b3="k")
    g["w_ffn_in"] = _mm(dab, sv["hn2"], "tn", BF16, "mm_dffn_in", a3="m", o3="m")

    def mix_post_bwd(dh2, dhn2, h1, y1, gpre, gpost):
        h1hat, r1 = _rms(h1)
        dh1 = dh2 + _rms_bwd(h1hat, r1, dhn2 * gpre)
        y1hat, ry = _rms(y1)
        return dh1, _rms_bwd(y1hat, ry, dh1 * gpost), _colsum(dhn2 * h1hat), _colsum(dh1 * y1hat)

    dh1, dy1, g["ffn_pre_norm"], g["mix_post_norm"] = _rowwise(
        mix_post_bwd, [dh2, dhn2, sv["h1"], sv["y1"]], [w["ffn_pre_norm"], w["mix_post_norm"]],
        [(d, F32), (d, BF16)], [d, d], "mix_post_bwd", ROW_TILE)
    dmc = _mm(dy1, w["w_out"], "nt", BF16, "mm_dmc")
    g["w_out"] = by_rows(_mm(sv["mc"], dy1, "tn", BF16, "mm_dout"))
    w = dict(w, **mid(g, dmc))

    def mix_out_bwd(da, dg, att, gm, gain):
        atthat, ra = _rms(att)
        gmhat, rg = _rms(gm)
        dgain = jnp.concatenate([_colsum(da * atthat), _colsum(dg * gmhat)], axis=-1)
        return _rms_bwd(atthat, ra, da * gain[:, :D_ATT]), _rms_bwd(gmhat, rg, dg * gain[:, D_ATT:]), dgain

    datt, dgm, g["mix_out_norm"] = _rowwise(
        mix_out_bwd, [(dmc, 0, D_ATT), (dmc, 1, D_GM), sv["att"], sv["gm"]], [w["mix_out_norm"]],
        [(D_ATT, BF16), (D_GM, F32)], [D_ATT + D_GM], "mix_out_bwd", ROW_TILE)

    dgu, dgv, dwt, dbs_t, g["gm_v_norm"] = _gmlp_bwd(sv["z"], dgm, w["wt"], w["wt_t"], w["bs_t"], w["gm_v_norm"])
    g["gm_w_s"] = dwt * jnp.tril(jnp.ones((CHUNK, CHUNK), F32))[None]
    g["gm_b_s"] = dbs_t.T

    delta = _attn_delta(sv["att"], datt, tq)
    dq, dk, dv, dc_row = _attn_bwd(sv["qa"], sv["ka"], sv["vb"], datt, sv["lse"], delta, tq)
    dfl_t, db = _forget_bwd(dc_row.reshape(N_HEADS, s), sv["fl_t"], w["b_forget"])
    g["b_forget"] = db.reshape(1, N_HEADS)
    dz = _dz_concat([dq, dk, dv, dgu, dgv], dfl_t.T, ROW_TILE)
    dhn1 = _mm(dz, w["w_in"], "nt", BF16, "mm_dhn1")
    g["w_in"] = _w_in_split(_mm(sv["hn1"], dz, "tn", BF16, "mm_din"))

    def mix_pre_bwd(dh1, dhn1, h0, gpre):
        h0hat, r0 = _rms(h0)
        return dh1 + _rms_bwd(h0hat, r0, dhn1 * gpre), _colsum(dhn1 * h0hat)

    dh0, g["mix_pre_norm"] = _rowwise(mix_pre_bwd, [dh1, dhn1, sv["h0"]], [w["mix_pre_norm"]],
                                      [(d, F32)], [d], "mix_pre_bwd", ROW_TILE)
    return dh0, g


ANY = pl.BlockSpec(memory_space=pl.ANY)


def _all_gather(xs, layer, name):
    n = len(xs)

    def body(*refs):
        x_refs, out_refs = refs[:n], refs[n:2 * n]
        send_sems, recv_sems, local_sems = refs[2 * n:]
        x, y, c = lax.axis_index("x"), lax.axis_index("y"), lax.axis_index("c")
        me, sibling = (x, y, c), (x, y, 1 - c)
        chips = [(1 - x, y), (x, 1 - y), (1 - x, 1 - y)]

        def shard(a):
            return x_refs[a] if layer is None else x_refs[a].at[layer]

        def rows(a, px, py, pc):
            return out_refs[a].at[4 * px + 2 * py + pc]

        def copy(a, kk, block, to, from_shard=False):
            return pltpu.make_async_remote_copy(
                src_ref=shard(a) if from_shard else rows(a, *block), dst_ref=rows(a, *block),
                send_sem=send_sems.at[7 * a + kk], recv_sem=recv_sems.at[7 * a + kk],
                device_id=to, device_id_type=MESH)

        mine = [pltpu.make_async_copy(shard(a), rows(a, *me), local_sems.at[a]) for a in range(n)]
        for cp in mine:
            cp.start()
        first = []
        for a in range(n):
            first.append(copy(a, 0, me, sibling, from_shard=True))
            first += [copy(a, 1 + j, me, (*chip, c), from_shard=True) for j, chip in enumerate(chips)]
        for cp in first:
            cp.start()
        passed = []
        for j, chip in enumerate(chips):
            for a in range(n):
                copy(a, 1 + j, (*chip, c), me).wait_recv()
                passed.append(copy(a, 4 + j, (*chip, c), sibling))
                passed[-1].start()
        for a in range(n):
            copy(a, 0, sibling, me).wait_recv()
        for j, chip in enumerate(chips):
            for a in range(n):
                copy(a, 4 + j, (*chip, 1 - c), me).wait_recv()
        for cp in first + passed:
            cp.wait_send()
        for cp in mine:
            cp.wait()

    shapes = [x.shape if layer is None else x.shape[1:] for x in xs]
    return pl.pallas_call(
        body, name=name, out_shape=[jax.ShapeDtypeStruct((N_DEV,) + sh, x.dtype) for sh, x in zip(shapes, xs)],
        in_specs=[ANY] * n, out_specs=[ANY] * n,
        scratch_shapes=[pltpu.SemaphoreType.DMA((7 * n,)), pltpu.SemaphoreType.DMA((7 * n,)),
                        pltpu.SemaphoreType.DMA((n,))],
    )(*xs)


HBM = pl.BlockSpec(memory_space=pltpu.HBM)
SEMS = pl.BlockSpec(memory_space=pltpu.SEMAPHORE)
EFFECT = pltpu.SideEffectType.DATAFLOW_SIDE_EFFECTING
FLIPS = tuple((fx, fy, fc) for fx in (0, 1) for fy in (0, 1) for fc in (0, 1))[1:]


def _exchange_copies(src_refs, land_refs, send_sems, recv_sems, layer, scatter):
    x, y, c = lax.axis_index("x"), lax.axis_index("y"), lax.axis_index("c")
    me = 4 * x + 2 * y + c
    copies = []
    for a, (src, land) in enumerate(zip(src_refs, land_refs)):
        for f, (fx, fy, fc) in enumerate(FLIPS):
            px, py, pc = (1 - x if fx else x), (1 - y if fy else y), (1 - c if fc else c)
            if scatter:
                block = src.at[4 * px + 2 * py + pc]
            else:
                block = src if layer is None else src.at[layer]
            copies.append(pltpu.make_async_remote_copy(
                src_ref=block, dst_ref=land.at[me], send_sem=send_sems.at[7 * a + f], recv_sem=recv_sems.at[7 * a + f],
                device_id=(px, py, pc), device_id_type=MESH))
    return copies


def _exchange_start(srcs, lands, layer, scatter, name):
    n = len(srcs)

    def body(*refs):
        for cp in _exchange_copies(refs[:n], refs[n:2 * n], refs[2 * n], refs[2 * n + 1], layer, scatter):
            cp.start()
        token = refs[-1]
        token[...] = jnp.zeros_like(token)

    operands = list(srcs) + list(lands)
    outs = pl.pallas_call(
        body, name=name,
        out_shape=(pltpu.SemaphoreType.DMA((7 * n,)), pltpu.SemaphoreType.DMA((7 * n,)),
                   *[pltpu.HBM(a.shape, a.dtype) for a in operands], jax.ShapeDtypeStruct((8, LANE), F32)),
        in_specs=[HBM] * (2 * n),
        out_specs=(SEMS, SEMS, *[HBM] * (2 * n), pl.BlockSpec(memory_space=pltpu.VMEM)),
        input_output_aliases={i: 2 + i for i in range(2 * n)},
        compiler_params=pltpu.CompilerParams(has_side_effects=EFFECT),
    )(*[pltpu.with_memory_space_constraint(a, pltpu.HBM) for a in operands])
    return outs[0], outs[1], outs[2:2 + n], outs[2 + n:2 + 2 * n], outs[-1]


def _exchange_wait(started, after, layer, scatter, name):
    send_sems, recv_sems, srcs, lands, _ = started
    n = len(srcs)

    def body(*refs):
        for cp in _exchange_copies(refs[:n], refs[n:2 * n], refs[2 * n], refs[2 * n + 1], layer, scatter):
            cp.wait_send()
            cp.wait_recv()

    operands = list(srcs) + list(lands)
    outs = pl.pallas_call(
        body, name=name, out_shape=tuple(pltpu.HBM(a.shape, a.dtype) for a in operands),
        in_specs=[HBM] * (2 * n) + [SEMS, SEMS, ANY], out_specs=[HBM] * (2 * n),
        input_output_aliases={i: i for i in range(2 * n)},
        compiler_params=pltpu.CompilerParams(has_side_effects=EFFECT),
    )(*operands, send_sems, recv_sems, after)
    return outs[:n], outs[n:]


def _sum_devices(parts):
    _, r, c = parts.shape

    def body(p_ref, o_ref):
        acc = p_ref[0].astype(F32)
        for j in range(1, N_DEV):
            acc = acc + p_ref[j].astype(F32)
        o_ref[...] = acc

    return pl.pallas_call(
        body, name="small_sum", out_shape=jax.ShapeDtypeStruct((r, c), F32), grid=(r // SMALL_ROWS,),
        in_specs=[pl.BlockSpec((N_DEV, SMALL_ROWS, c), lambda i: (0, i, 0))],
        out_specs=pl.BlockSpec((SMALL_ROWS, c), lambda i: (i, 0)),
        compiler_params=_cparams("parallel"),
    )(parts)


def _adamw_math(w, g, m, v):
    m = ADAM_B1 * m + (1.0 - ADAM_B1) * g
    v = ADAM_B2 * v + (1.0 - ADAM_B2) * (g * g)
    m_hat = m / (1.0 - ADAM_B1 ** ADAM_STEP)
    v_hat = v / (1.0 - ADAM_B2 ** ADAM_STEP)
    return -ADAM_LR * (m_hat / (jnp.sqrt(v_hat) + ADAM_EPS) + ADAM_WD * w), m, v


def _adamw_shard(w, m, v, parts, layer, outs, name):
    _, a, b = w.shape
    ta = _tile(a, 256, 16)
    if outs is None:
        outs = [lax.empty(w.shape, F32) for _ in range(4)]

    def body(w_ref, m_ref, v_ref, p_ref, *refs):
        g_ref, d_ref, nm_ref, nv_ref = refs[4:]
        g = p_ref[0].astype(F32)
        for j in range(1, N_DEV):
            g = g + p_ref[j].astype(F32)
        g_ref[0] = g
        d_ref[0], nm_ref[0], nv_ref[0] = _adamw_math(w_ref[0], g, m_ref[0], v_ref[0])

    mine = pl.BlockSpec((1, ta, b), lambda i: (layer, i, 0))
    return pl.pallas_call(
        body, name=name, out_shape=[jax.ShapeDtypeStruct(w.shape, F32)] * 4, grid=(a // ta,),
        in_specs=[mine, mine, mine, pl.BlockSpec((N_DEV, ta, b), lambda i: (0, i, 0))] + [ANY] * 4,
        out_specs=[mine] * 4, input_output_aliases={4 + k: k for k in range(4)},
        compiler_params=_cparams("parallel"),
    )(w, m, v, parts, *outs)


def _w_in_moves():
    n = D_IN // N_DEV
    runs = ((0, 3 * D_ATT, 0), (3 * D_ATT, 3 * D_ATT + N_HEADS, F_OFF), (3 * D_ATT + N_HEADS, D_IN, 3 * D_ATT))
    moves = []
    for j in range(N_DEV):
        for lo, hi, padded in runs:
            a, b = max(n * j, lo), min(n * (j + 1), hi)
            if a < b:
                moves.append((j, a - n * j, padded + a - lo, b - a))
    return moves


def _w_in_assemble(blocks):
    _, d, n = blocks.shape
    tr = _tile(d, 256, 16)

    def body(b_ref, o_ref):
        o_ref[:, D_IN:] = jnp.zeros((tr, D_IN_PAD - D_IN), o_ref.dtype)
        for j, col, padded, width in _w_in_moves():
            o_ref[:, padded:padded + width] = b_ref[j, :, col:col + width]

    return pl.pallas_call(
        body, name="w_in_assemble", out_shape=jax.ShapeDtypeStruct((d, D_IN_PAD), blocks.dtype), grid=(d // tr,),
        in_specs=[pl.BlockSpec((N_DEV, tr, n), lambda i: (0, i, 0))],
        out_specs=pl.BlockSpec((tr, D_IN_PAD), lambda i: (i, 0)),
        compiler_params=_cparams("parallel"),
    )(blocks)


def _w_in_split(padded):
    d = padded.shape[0]
    n = D_IN // N_DEV
    tr = _tile(d, 256, 16)

    def body(p_ref, o_ref):
        for j, col, src, width in _w_in_moves():
            o_ref[j, :, col:col + width] = p_ref[:, src:src + width]

    return pl.pallas_call(
        body, name="w_in_split", out_shape=jax.ShapeDtypeStruct((N_DEV, d, n), padded.dtype), grid=(d // tr,),
        in_specs=[pl.BlockSpec((tr, D_IN_PAD), lambda i: (i, 0))],
        out_specs=pl.BlockSpec((N_DEV, tr, n), lambda i: (0, i, 0)),
        compiler_params=_cparams("parallel"),
    )(padded)


def _small_rows(size):
    return -(-size // (8 * SMALL_COLS)) * 8


def _pack_small(pieces):
    rows = []
    for p in pieces:
        flat = p.reshape(-1)
        rows.append(jnp.pad(flat, (0, _small_rows(flat.shape[0]) * SMALL_COLS - flat.shape[0])).reshape(-1, SMALL_COLS))
    used = sum(r.shape[0] for r in rows)
    rows.append(jnp.zeros((-used % SMALL_ROWS, SMALL_COLS), F32))
    return jnp.concatenate(rows)


def kernel(x, p, mix_pre_norm, mix_post_norm, w_in, b_forget, gm_v_norm, gm_w_s, gm_b_s, mix_out_norm, w_out, ffn_pre_norm, ffn_post_norm, w_ffn_in, w_ffn_out, w_ple, ple_norm, w_ple_gate, loss_target, m_mix_pre_norm, m_mix_post_norm, m_w_in, m_b_forget, m_gm_v_norm, m_gm_w_s, m_gm_b_s, m_mix_out_norm, m_w_out, m_ffn_pre_norm, m_ffn_post_norm, m_w_ffn_in, m_w_ffn_out, m_w_ple, m_ple_norm, m_w_ple_gate, v_mix_pre_norm, v_mix_post_norm, v_w_in, v_b_forget, v_gm_v_norm, v_gm_w_s, v_gm_b_s, v_mix_out_norm, v_w_out, v_ffn_pre_norm, v_ffn_post_norm, v_w_ffn_in, v_w_ffn_out, v_w_ple, v_ple_norm, v_w_ple_gate):
    given = dict(locals())
    weights = {n: given[n] for n in WEIGHT_ORDER}
    mom_m = {n: given["m_" + n] for n in WEIGHT_ORDER}
    mom_v = {n: given["v_" + n] for n in WEIGHT_ORDER}
    depth = w_in.shape[0]
    s, d = x.shape[1], x.shape[2]
    tq = _tile(s, ATT_BLOCK)
    me = 4 * lax.axis_index("x") + 2 * lax.axis_index("y") + lax.axis_index("c")
    tril = jnp.tril(jnp.ones((CHUNK, CHUNK), F32))

    def landing(block):
        return lax.dynamic_update_index_in_dim(lax.empty((N_DEV,) + block.shape, block.dtype), block, me, 0)

    def mix_weights(i, got):
        wt = gm_w_s[i] * tril[None]
        lw = dict(
            w_in=_w_in_assemble(got["w_in"]),
            b_forget=b_forget[i][:, None], wt=wt.astype(BF16), wt_t=wt.transpose(0, 2, 1).astype(BF16),
            bs_t=gm_b_s[i].T)
        lw.update({n: weights[n][i][None] for n in ("mix_pre_norm", "mix_post_norm", "gm_v_norm", "mix_out_norm",
                                                    "ffn_pre_norm", "ffn_post_norm", "ple_norm")})
        return lw

    def rest_weights(got):
        return dict(w_out=got["w_out"].reshape(-1, d), w_ffn_in=got["w_ffn_in"],
                    w_ffn_out=got["w_ffn_out"].reshape(N_DEV // 2, -1, d),
                    w_ple=jnp.concatenate([got["w_ple"][j] for j in range(N_DEV)], axis=-1),
                    w_ple_gate=got["w_ple_gate"].reshape(-1, d))

    def shard_view(n, a):
        return jnp.transpose(a, (0, 2, 1)) if n in TRANSPOSED_WEIGHTS else a

    shards = {n: shard_view(n, weights[n].astype(BF16)) for n in MATRIX_WEIGHTS}

    def gather_start(i):
        started = {}
        order = jnp.zeros((), BF16)
        for tag, grp in EXCHANGE_GROUPS.items():
            started[tag] = _exchange_start([shards[n] for n in grp], [landing(shards[n][i] + order) for n in grp], i,
                                           False, f"weights_gather_start_{i}_{tag}")
            order = started[tag][4][0, 0].astype(BF16)
        return started

    def gather_finish(i, tag, pending, after):
        srcs, got = _exchange_wait(pending[tag], after, i, False, f"weights_gather_wait_{i}_{tag}")
        shards.update(zip(EXCHANGE_GROUPS[tag], srcs))
        return dict(zip(EXCHANGE_GROUPS[tag], got))

    h = x[0]
    saved, layer_w = [], []
    pending = gather_start(0)
    for i in range(depth):
        lw = mix_weights(i, gather_finish(i, "mix", pending, h))
        if i == 0:
            lw["mix_pre_norm"] = lw["mix_pre_norm"] + pending["rest"][4][:1, :1]
        following = {}

        def late(att, i=i, pending=pending, lw=lw, following=following):
            rest = rest_weights(gather_finish(i, "rest", pending, att))
            lw.update(rest)
            if i + 1 == depth:
                return rest
            following.update(gather_start(i + 1))
            token = following["mix"][4][:1, :1] + following["rest"][4][:1, :1]
            return dict(rest, mix_out_norm=lw["mix_out_norm"] + token)

        h, sv = _layer_fwd(h, p[i, 0], lw, tq, late)
        layer_w.append(lw)
        saved.append(sv)
        pending = following

    def loss_head(y, t):
        err = y - t
        return err * (1.0 / d), _colsum(err * err)

    dh, sq = _rowwise(loss_head, [h, loss_target[0]], [], [(d, F32)], [d], "loss_head", ROW_TILE)
    loss = lax.psum(0.5 * jnp.sum(sq) / d, AXES)

    layer_g = [None] * depth
    shard_out = {n: None for n in MATRIX_WEIGHTS}

    def scatter_start(i, tag, g):
        full_g = [g[n] for n in EXCHANGE_GROUPS[tag]]
        lands = [landing(lax.dynamic_index_in_dim(gf, me, 0, keepdims=False)) for gf in full_g]
        return _exchange_start(full_g, lands, None, True, f"grads_scatter_start_{i}_{tag}")

    def scatter_finish(i, tag, started, after):
        _, parts = _exchange_wait(started[tag], after, None, True, f"grads_scatter_wait_{i}_{tag}")
        for n, part in zip(EXCHANGE_GROUPS[tag], parts):
            shard_out[n] = _adamw_shard(shard_view(n, weights[n]), shard_view(n, mom_m[n]), shard_view(n, mom_v[n]),
                                        part, i, shard_out[n], "adamw_" + n)

    before = None
    for i in reversed(range(depth)):
        lw = layer_w[i]
        if before is not None:
            lw = dict(lw, ple_norm=lw["ple_norm"] + before[1]["mix"][4][:1, :1])
        started = {}

        def mid(g, dmc, i=i, before=before, started=started, lw=lw):
            if before is not None:
                scatter_finish(before[0], "rest", before[1], dmc)
            started["rest"] = scatter_start(i, "rest", g)
            return dict(mix_out_norm=lw["mix_out_norm"] + started["rest"][4][:1, :1])

        dh, layer_g[i] = _layer_bwd(dh, saved[i], lw, tq, mid)
        if before is not None:
            scatter_finish(before[0], "mix", before[1], dh)
        started["mix"] = scatter_start(i, "mix", layer_g[i])
        before = (i, started)
    scatter_finish(before[0], "rest", before[1], before[1]["mix"][4])
    scatter_finish(before[0], "mix", before[1], dh)
    grad_x = dh[None]

    grads, deltas, new_m, new_v = {}, {}, {}, {}
    for n in MATRIX_WEIGHTS:
        grads[n], deltas[n], new_m[n], new_v[n] = (shard_view(n, a) for a in shard_out[n])

    small_g = _pack_small([jnp.stack([layer_g[i][n].reshape(-1) for i in range(depth)]) for n in SMALL_WEIGHTS])
    (gathered,) = _all_gather([small_g.astype(BF16)], None, "small_grads_all_gather")
    g_small = _sum_devices(gathered)
    pack = lambda t: _pack_small([t[n] for n in SMALL_WEIGHTS])
    dl, nm, nv = _rowwise(_adamw_math, [pack(weights), g_small, pack(mom_m), pack(mom_v)], [],
                          [(SMALL_COLS, F32)] * 3, [], "adamw_small", SMALL_ROWS)
    row = 0
    for n in SMALL_WEIGHTS:
        shp, size = weights[n].shape, weights[n].size
        grads[n], deltas[n], new_m[n], new_v[n] = (
            a[row:row + _small_rows(size)].reshape(-1)[:size].reshape(shp) for a in (g_small, dl, nm, nv))
        row += _small_rows(size)

    return (loss, grad_x, *[grads[n] for n in WEIGHT_ORDER], *[deltas[n] for n in WEIGHT_ORDER],
            *[new_m[n] for n in WEIGHT_ORDER], *[new_v[n] for n in WEIGHT_ORDER])
```

```python
import functools
import math

import jax
import jax.numpy as jnp
from jax import lax
from jax.experimental import pallas as pl
from jax.experimental.pallas import tpu as pltpu

F32 = jnp.float32
BF16 = jnp.bfloat16
MESH = pl.DeviceIdType.MESH
AXES = ("x", "y", "c")
N_DEV = 8

EPS = 1e-6
NEG_INF = -1e30
N_HEADS = 8
HEAD_DIM = 64
D_ATT = N_HEADS * HEAD_DIM
N_GROUPS = 8
GROUP_DIM = 64
D_GM = N_GROUPS * GROUP_DIM
CHUNK = 128
ATT_SCALE = HEAD_DIM ** -0.5
ATT_BLOCK = 1024
D_IN = 3 * D_ATT + N_HEADS + 2 * D_GM
D_IN_PAD = 3 * D_ATT + 2 * D_GM + 128
F_OFF = 3 * D_ATT + 2 * D_GM

ADAM_LR = 0.001
ADAM_B1 = 0.9
ADAM_B2 = 0.999
ADAM_EPS = 1e-08
ADAM_WD = 0.01
ADAM_STEP = 10

LANE = 128
VMEM_LIMIT = 48 * 1024 * 1024
ROW_TILE = 512
K_TILE = 4096
SMALL_COLS = 128
SMALL_ROWS = 512

MATRIX_WEIGHTS = ("w_in", "w_out", "w_ffn_in", "w_ffn_out", "w_ple", "w_ple_gate")
EXCHANGE_GROUPS = {"mix": ("w_in",), "rest": ("w_out", "w_ffn_in", "w_ffn_out", "w_ple", "w_ple_gate")}
TRANSPOSED_WEIGHTS = ("w_ffn_in",)
SMALL_WEIGHTS = ("mix_pre_norm", "mix_post_norm", "b_forget", "gm_v_norm", "gm_w_s", "gm_b_s",
                 "mix_out_norm", "ffn_pre_norm", "ffn_post_norm", "ple_norm")
WEIGHT_ORDER = ("mix_pre_norm", "mix_post_norm", "w_in", "b_forget", "gm_v_norm", "gm_w_s", "gm_b_s",
                "mix_out_norm", "w_out", "ffn_pre_norm", "ffn_post_norm", "w_ffn_in", "w_ffn_out",
                "w_ple", "ple_norm", "w_ple_gate")


def _tile(n, pref, unit=LANE):
    best = None
    t = unit
    while t <= min(n, pref):
        if n % t == 0:
            best = t
        t += unit
    return n if best is None else best


def _cparams(*semantics):
    return pltpu.CompilerParams(dimension_semantics=semantics or None, vmem_limit_bytes=VMEM_LIMIT)


NN = (((1,), (0,)), ((), ()))
NT = (((1,), (1,)), ((), ()))
TN = (((0,), (0,)), ((), ()))
_MM_AXES = {
    "nn": ("i", "k", "k", "j"), "nt": ("i", "k", "j", "k"), "tn": ("k", "i", "k", "j")}
_MM_DN = {"nn": NN, "nt": NT, "tn": TN}


def _mm(a, b, dims, out_dtype, name, a3=None, b3=None, o3=None, tm=1024, tn=1024, tk=None):
    ar, ac, br, bc = _MM_AXES[dims]
    letter = {"i": "m", "j": "n", "k": "k"}
    size = {}

    def measure(x, rows, cols, stacked):
        shape = x.shape
        if stacked is None:
            size.setdefault(letter[rows], shape[0])
            size.setdefault(letter[cols], shape[1])
        else:
            for ax, n in ((rows, shape[1]), (cols, shape[2])):
                size.setdefault(letter[ax], n * shape[0] if letter[ax] == stacked else n)

    measure(a, ar, ac, a3)
    measure(b, br, bc, b3)
    m, n, k = size["m"], size["n"], size["k"]
    slab = {}
    for x, stacked, rows, cols in ((a, a3, ar, ac), (b, b3, br, bc)):
        if stacked is not None:
            slab[stacked] = x.shape[1] if letter[rows] == stacked else x.shape[2]
    if o3 is not None:
        slab.setdefault(o3, slab.get(o3, None) or {"m": m, "n": n}[o3] // N_DEV)
    tk = tk or K_TILE
    tile = {"m": slab.get("m") or _tile(m, tm), "n": slab.get("n") or _tile(n, tn), "k": slab.get("k") or _tile(k, tk)}
    group = 1
    if a3 == "k" and b3 == "k":
        group = max(g for g in range(1, a.shape[0] + 1) if a.shape[0] % g == 0 and g * tile["k"] <= max(tk, tile["k"]))
    nk = k // (group * tile["k"])

    def spec(rows, cols, stacked):
        tr, tc = tile[letter[rows]], tile[letter[cols]]
        if stacked is None:
            return pl.BlockSpec((tr, tc), lambda i, j, kk: ({"i": i, "j": j, "k": kk}[rows], {"i": i, "j": j, "k": kk}[cols]))

        def imap(i, j, kk):
            g = {"i": i, "j": j, "k": kk}
            return (g[{"m": "i", "n": "j", "k": "k"}[stacked]],
                    0 if letter[rows] == stacked else g[rows], 0 if letter[cols] == stacked else g[cols])

        return pl.BlockSpec((group if stacked == "k" else 1, tr, tc), imap)

    dn = _MM_DN[dims]

    def body(a_ref, b_ref, o_ref, *acc):
        prod = None
        for g in range(group):
            av = a_ref[...] if a3 is None else a_ref[g]
            bv = b_ref[...] if b3 is None else b_ref[g]
            term = lax.dot_general(av.astype(BF16), bv.astype(BF16), dn, preferred_element_type=F32)
            prod = term if prod is None else prod + term

        def emit(val):
            if o3 is None:
                o_ref[...] = val.astype(out_dtype)
            else:
                o_ref[0] = val.astype(out_dtype)

        if nk == 1:
            emit(prod)
            return
        (acc_ref,) = acc
        kk = pl.program_id(2)

        @pl.when(kk == 0)
        def _():
            acc_ref[...] = prod

        @pl.when(kk > 0)
        def _():
            acc_ref[...] += prod

        @pl.when(kk == nk - 1)
        def _():
            emit(acc_ref[...])

    if o3 is None:
        out_shape = (m, n)
    elif o3 == "m":
        out_shape = (m // tile["m"], tile["m"], n)
    else:
        out_shape = (n // tile["n"], m, tile["n"])
    return pl.pallas_call(
        body, name=name, out_shape=jax.ShapeDtypeStruct(out_shape, out_dtype),
        grid=(m // tile["m"], n // tile["n"], nk),
        in_specs=[spec(ar, ac, a3), spec(br, bc, b3)], out_specs=spec("i", "j", o3),
        scratch_shapes=[] if nk == 1 else [pltpu.VMEM((tile["m"], tile["n"]), F32)],
        compiler_params=_cparams("parallel", "parallel", "arbitrary"),
    )(a, b)


def _rowwise(fn, rows, vecs, outs, reds, name, ts):
    rows = [r if isinstance(r, tuple) else (r, 0, r.shape[1]) for r in rows]
    s = rows[0][0].shape[0]
    ts = _tile(s, ts, 8)
    nr, nv, no = len(rows), len(vecs), len(outs)

    def body(*refs):
        vals = fn(*[r[...] for r in refs[:nr + nv]])
        vals = vals if isinstance(vals, tuple) else (vals,)
        o_refs = refs[nr + nv:nr + nv + no]
        r_refs = refs[nr + nv + no:]
        for o_ref, val in zip(o_refs, vals[:no]):
            o_ref[...] = val.astype(o_ref.dtype)
        if r_refs:
            @pl.when(pl.program_id(0) == 0)
            def _():
                for r_ref in r_refs:
                    r_ref[...] = jnp.zeros_like(r_ref)

            for r_ref, val in zip(r_refs, vals[no:]):
                r_ref[...] += val

    in_specs = [pl.BlockSpec((ts, w), functools.partial(lambda i, cb: (i, cb), cb=cb)) for _, cb, w in rows]
    in_specs += [pl.BlockSpec(v.shape, lambda i: (0, 0)) for v in vecs]
    out_specs = [pl.BlockSpec((ts, c), lambda i: (i, 0)) for c, _ in outs]
    out_specs += [pl.BlockSpec((1, c), lambda i: (0, 0)) for c in reds]
    out_shape = [jax.ShapeDtypeStruct((s, c), dt) for c, dt in outs]
    out_shape += [jax.ShapeDtypeStruct((1, c), F32) for c in reds]
    return pl.pallas_call(
        body, name=name, out_shape=out_shape, grid=(s // ts,), in_specs=in_specs, out_specs=out_specs,
        compiler_params=_cparams("arbitrary" if reds else "parallel"),
    )(*[r[0] for r in rows], *vecs)


def _rms(x):
    x = x.astype(F32)
    r = lax.rsqrt(jnp.mean(x * x, axis=-1, keepdims=True) + EPS)
    return x * r, r


def _rms_bwd(xhat, r, dyg):
    return r * (dyg - xhat * jnp.mean(dyg * xhat, axis=-1, keepdims=True))


def _colsum(x):
    return jnp.sum(x, axis=0, keepdims=True)


def _sigmoid(x):
    return 1.0 / (1.0 + jnp.exp(-x.astype(F32)))


GELU_C = math.sqrt(2.0 / math.pi)
GELU_A = 0.044715


def _gelu(x):
    return 0.5 * x * (1.0 + jnp.tanh(GELU_C * (x + GELU_A * x * x * x)))


def _gelu_grad(x):
    t = jnp.tanh(GELU_C * (x + GELU_A * x * x * x))
    return 0.5 * (1.0 + t) + 0.5 * x * (1.0 - t * t) * GELU_C * (1.0 + 3.0 * GELU_A * x * x)


def _ffn_in_swiglu(hn, wg):
    s, d = hn.shape
    g2, n, _ = wg.shape
    g = g2 // 2
    tm = _tile(s, 1024)

    def body(h_ref, wa_ref, wb_ref, ab_ref, t_ref):
        hv = h_ref[...]
        a = lax.dot_general(hv, wa_ref[0], NT, preferred_element_type=F32)
        b = lax.dot_general(hv, wb_ref[0], NT, preferred_element_type=F32)
        ab_ref[0, 0] = a.astype(BF16)
        ab_ref[1, 0] = b.astype(BF16)
        t_ref[0] = (a * _sigmoid(a) * b).astype(BF16)

    return pl.pallas_call(
        body, name="mm_ffn_in_swiglu",
        out_shape=(jax.ShapeDtypeStruct((2, g, s, n), BF16), jax.ShapeDtypeStruct((g, s, n), BF16)),
        grid=(s // tm, g),
        in_specs=[pl.BlockSpec((tm, d), lambda i, j: (i, 0)), pl.BlockSpec((1, n, d), lambda i, j: (j, 0, 0)),
                  pl.BlockSpec((1, n, d), lambda i, j: (j + g, 0, 0))],
        out_specs=(pl.BlockSpec((2, 1, tm, n), lambda i, j: (0, j, i, 0)),
                   pl.BlockSpec((1, tm, n), lambda i, j: (j, i, 0))),
        compiler_params=_cparams("parallel", "parallel"),
    )(hn, wg, wg)


def _ffn_out_swiglu_bwd(dy, w4, ab):
    s, d = dy.shape
    _, g, _, n = ab.shape
    tm = _tile(s, 1024)

    def body(dy_ref, w_ref, ab_ref, t_ref, dab_ref):
        dt = lax.dot_general(dy_ref[...], w_ref[0], NT, preferred_element_type=F32)
        a = ab_ref[0, 0].astype(F32)
        b = ab_ref[1, 0].astype(F32)
        sig = _sigmoid(a)
        silu = a * sig
        t_ref[0] = (silu * b).astype(BF16)
        dab_ref[0, 0] = (dt * b * (sig * (1.0 + a * (1.0 - sig)))).astype(BF16)
        dab_ref[1, 0] = (dt * silu).astype(BF16)

    both = pl.BlockSpec((2, 1, tm, n), lambda i, j: (0, j, i, 0))
    one = pl.BlockSpec((1, tm, n), lambda i, j: (j, i, 0))
    return pl.pallas_call(
        body, name="mm_dt_swiglu_bwd",
        out_shape=(jax.ShapeDtypeStruct((g, s, n), BF16), jax.ShapeDtypeStruct((2, g, s, n), BF16)),
        grid=(s // tm, g),
        in_specs=[pl.BlockSpec((tm, d), lambda i, j: (i, 0)), pl.BlockSpec((1, n, d), lambda i, j: (j, 0, 0)), both],
        out_specs=(one, both),
        compiler_params=_cparams("parallel", "parallel"),
    )(dy, w4, ab)


def _forget_fwd(fl_t, b_col):
    h, s = fl_t.shape
    nb = s // LANE

    def body(fl_ref, b_ref, c_ref):
        upper = (lax.broadcasted_iota(jnp.int32, (LANE, LANE), 0)
                 <= lax.broadcasted_iota(jnp.int32, (LANE, LANE), 1)).astype(F32)

        def step(i, carry):
            x = fl_ref[i] + b_ref[...]
            lf = jnp.minimum(x, 0.0) - jnp.log(1.0 + jnp.exp(-jnp.abs(x)))
            cs = jnp.dot(lf, upper, precision=lax.Precision.HIGHEST, preferred_element_type=F32) + carry
            c_ref[i] = cs
            return cs[:, LANE - 1:LANE]

        lax.fori_loop(0, nb, step, jnp.zeros((h, 1), F32))

    out = pl.pallas_call(
        body, name="forget_fwd", out_shape=jax.ShapeDtypeStruct((nb, h, LANE), F32),
        compiler_params=_cparams(),
    )(fl_t.reshape(h, nb, LANE).transpose(1, 0, 2), b_col)
    return out.transpose(1, 0, 2).reshape(h, s)


def _forget_bwd(dc_t, fl_t, b_col):
    h, s = fl_t.shape
    nb = s // LANE

    def body(dc_ref, fl_ref, b_ref, dfl_ref, db_ref):
        lower = (lax.broadcasted_iota(jnp.int32, (LANE, LANE), 0)
                 >= lax.broadcasted_iota(jnp.int32, (LANE, LANE), 1)).astype(F32)

        def step(t, carry):
            tail, db = carry
            i = nb - 1 - t
            rc = jnp.dot(dc_ref[i], lower, precision=lax.Precision.HIGHEST, preferred_element_type=F32) + tail
            dfl = rc * (1.0 - _sigmoid(fl_ref[i] + b_ref[...]))
            dfl_ref[i] = dfl
            return rc[:, 0:1], db + jnp.sum(dfl, axis=1, keepdims=True)

        _, db = lax.fori_loop(0, nb, step, (jnp.zeros((h, 1), F32), jnp.zeros((h, 1), F32)))
        db_ref[...] = db

    blocked = lambda a: a.reshape(h, nb, LANE).transpose(1, 0, 2)
    dfl, db = pl.pallas_call(
        body, name="forget_bwd",
        out_shape=(jax.ShapeDtypeStruct((nb, h, LANE), F32), jax.ShapeDtypeStruct((h, 1), F32)),
        compiler_params=_cparams(),
    )(blocked(dc_t), blocked(fl_t), b_col)
    return dfl.transpose(1, 0, 2).reshape(h, s), db


N_PAIRS = N_HEADS // 2


def _causal_mask(t):
    return lax.broadcasted_iota(jnp.int32, (t, t), 0) >= lax.broadcasted_iota(jnp.int32, (t, t), 1)


def _head_lanes():
    return lax.broadcasted_iota(jnp.int32, (1, 2 * HEAD_DIM), 1) < HEAD_DIM


def _pick(x2, first, hh):
    zero = jnp.zeros_like(x2)
    return jnp.where(first, x2, zero) if hh == 0 else jnp.where(first, zero, x2)


BIAS_TERMS = 3


def _attn_prep(z, c, ts):
    s = z.shape[0]
    ts = _tile(s, ts, 16)
    w = 2 * HEAD_DIM

    def body(q_ref, k_ref, v_ref, c_ref, qa_ref, ka_ref, vb_ref):
        lane = lax.broadcasted_iota(jnp.int32, (1, w), 1)
        first = lane < HEAD_DIM
        cv = c_ref[...]
        for h in range(N_HEADS):
            pair = slice((h // 2) * w, (h // 2 + 1) * w)
            qh = q_ref[:, pair] * ATT_SCALE
            kh = k_ref[:, pair]
            if h % 2:
                qh = pltpu.roll(qh, HEAD_DIM, 1)
                kh = pltpu.roll(kh, HEAD_DIM, 1)
            rest = cv[:, h:h + 1]
            q_tail = jnp.zeros((1, w), F32)
            k_tail = jnp.zeros((1, w), F32)
            for t in range(BIAS_TERMS):
                term = rest.astype(BF16).astype(F32)
                rest = rest - term
                q_tail = jnp.where(lane == HEAD_DIM + t, term, jnp.where(lane == HEAD_DIM + BIAS_TERMS + t, 1.0, q_tail))
                k_tail = jnp.where(lane == HEAD_DIM + t, 1.0, jnp.where(lane == HEAD_DIM + BIAS_TERMS + t, -term, k_tail))
            qa_ref[:, h * w:(h + 1) * w] = jnp.where(first, qh, q_tail).astype(BF16)
            ka_ref[:, h * w:(h + 1) * w] = jnp.where(first, kh, k_tail).astype(BF16)
        vb_ref[...] = v_ref[...].astype(BF16)

    col = lambda cb: pl.BlockSpec((ts, D_ATT), lambda i: (i, cb))
    wide = pl.BlockSpec((ts, N_HEADS * w), lambda i: (i, 0))
    return pl.pallas_call(
        body, name="attn_prep",
        out_shape=(jax.ShapeDtypeStruct((s, N_HEADS * w), BF16), jax.ShapeDtypeStruct((s, N_HEADS * w), BF16),
                   jax.ShapeDtypeStruct((s, D_ATT), BF16)),
        grid=(s // ts,), in_specs=[col(0), col(1), col(2), pl.BlockSpec((ts, N_HEADS), lambda i: (i, 0))],
        out_specs=(wide, wide, col(0)),
        compiler_params=_cparams("parallel"),
    )(z, z, z, c)


def _attn_fwd(qa, ka, vb, tq):
    s = qa.shape[0]
    nq = s // tq
    w = 2 * HEAD_DIM

    def body(q_ref, k_ref, v_ref, o_ref, lse_ref):
        i = pl.program_id(1)
        first = _head_lanes()
        q2 = q_ref[...]

        def block(j, carry, masked):
            off = pl.multiple_of(j * tq, tq)
            k2 = k_ref[pl.ds(off, tq), :]
            v2 = v_ref[pl.ds(off, tq), :]
            new = []
            for hh in range(2):
                m, l, acc = carry[hh]
                sc = lax.dot_general(q2[:, hh * w:(hh + 1) * w], k2[:, hh * w:(hh + 1) * w], NT,
                                     preferred_element_type=F32)
                if masked:
                    sc = jnp.where(_causal_mask(tq), sc, NEG_INF)
                m_new = jnp.maximum(m, jnp.max(sc, axis=-1, keepdims=True))
                alpha = jnp.exp(m - m_new)
                p = jnp.exp(sc - m_new)
                l = alpha * l + jnp.sum(p, axis=-1, keepdims=True)
                p_hi = p.astype(BF16)
                p_lo = (p - p_hi.astype(F32)).astype(BF16)
                acc = (alpha * acc + jnp.dot(p_hi, v2, preferred_element_type=F32)
                       + jnp.dot(p_lo, v2, preferred_element_type=F32))
                new.append((m_new, l, acc))
            return tuple(new)

        one = (jnp.full((tq, 1), NEG_INF, F32), jnp.zeros((tq, 1), F32), jnp.zeros((tq, w), F32))
        carry = lax.fori_loop(0, i, lambda j, c: block(j, c, False), (one, one))
        (m0, l0, a0), (m1, l1, a1) = block(i, carry, True)
        o_ref[...] = jnp.where(first, a0 / l0, a1 / l1)
        lse_ref[0] = m0 + jnp.log(l0)
        lse_ref[1] = m1 + jnp.log(l1)

    return pl.pallas_call(
        body, name="attn_fwd",
        out_shape=(jax.ShapeDtypeStruct((s, D_ATT), F32), jax.ShapeDtypeStruct((N_HEADS, s, 1), F32)),
        grid=(N_PAIRS, nq),
        in_specs=[pl.BlockSpec((tq, 2 * w), lambda hp, i: (i, hp)),
                  pl.BlockSpec((s, 2 * w), lambda hp, i: (0, hp)),
                  pl.BlockSpec((s, w), lambda hp, i: (0, hp))],
        out_specs=(pl.BlockSpec((tq, w), lambda hp, i: (i, hp)),
                   pl.BlockSpec((2, tq, 1), lambda hp, i: (hp, i, 0))),
        compiler_params=_cparams("parallel", "parallel"),
    )(qa, ka, vb)


def _attn_delta(o, do, tq):
    s = o.shape[0]
    w = 2 * HEAD_DIM

    def body(o_ref, do_ref, d_ref):
        first = _head_lanes()
        prod = o_ref[...] * do_ref[...].astype(F32)
        d_ref[0] = jnp.sum(_pick(prod, first, 0), axis=-1, keepdims=True)
        d_ref[1] = jnp.sum(_pick(prod, first, 1), axis=-1, keepdims=True)

    blk = pl.BlockSpec((tq, w), lambda hp, i: (i, hp))
    return pl.pallas_call(
        body, name="attn_delta", out_shape=jax.ShapeDtypeStruct((N_HEADS, s, 1), F32), grid=(N_PAIRS, s // tq),
        in_specs=[blk, blk], out_specs=pl.BlockSpec((2, tq, 1), lambda hp, i: (hp, i, 0)),
        compiler_params=_cparams("parallel", "parallel"),
    )(o, do)


def _attn_bwd(qa, ka, vb, do, lse, delta, tq):
    s = qa.shape[0]
    nq = s // tq
    w = 2 * HEAD_DIM

    def body(q_ref, do_ref, lse_ref, dl_ref, k_ref, v_ref, dq_ref, dk_ref, dv_ref, dc_ref, dq_acc):
        j = pl.program_id(1)
        first = _head_lanes()

        @pl.when(j == 0)
        def _():
            dq_acc[...] = jnp.zeros_like(dq_acc)

        k2 = k_ref[...]
        v2 = v_ref[...]

        def step(i, carry, masked):
            off = pl.multiple_of(i * tq, tq)
            rows = pl.ds(off, tq)
            q2 = q_ref[rows, :]
            do2 = do_ref[rows, :]
            new, dqs = [], []
            for hh in range(2):
                dk, dv, dcs = carry[hh]
                qh = q2[:, hh * w:(hh + 1) * w]
                kh = k2[:, hh * w:(hh + 1) * w]
                sc = lax.dot_general(qh, kh, NT, preferred_element_type=F32)
                if masked:
                    sc = jnp.where(_causal_mask(tq), sc, NEG_INF)
                p = jnp.exp(sc - lse_ref[hh, rows, :])
                dv = dv + lax.dot_general(do2, p.astype(BF16), TN, preferred_element_type=F32)
                dp = lax.dot_general(_pick(do2, first, hh), v2, NT, preferred_element_type=F32)
                ds = p * (dp - dl_ref[hh, rows, :])
                dsb = ds.astype(BF16)
                dk = dk + lax.dot_general(qh, dsb, TN, preferred_element_type=F32)
                dqs.append(jnp.dot(dsb, kh, preferred_element_type=F32))
                new.append((dk, dv, dcs + jnp.sum(ds, axis=0, keepdims=True)))
            dq_acc[rows, :] += jnp.where(first, dqs[0], pltpu.roll(dqs[1], HEAD_DIM, 1)) * ATT_SCALE
            return tuple(new)

        one = (jnp.zeros((w, tq), F32), jnp.zeros((w, tq), F32), jnp.zeros((1, tq), F32))
        carry = step(j, (one, one), True)
        (dk0, dv0, dc0), (dk1, dv1, dc1) = lax.fori_loop(j + 1, nq, lambda i, c: step(i, c, False), carry)
        dk_ref[...] = jnp.where(first, dk0.T, pltpu.roll(dk1.T, HEAD_DIM, 1)).astype(BF16)
        dv_ref[...] = jnp.where(first, dv0.T, dv1.T).astype(BF16)
        dc_ref[0, 0] = -dc0
        dc_ref[1, 0] = -dc1

        @pl.when(j == nq - 1)
        def _():
            dq_ref[...] = dq_acc[...].astype(BF16)

    whole = lambda width: pl.BlockSpec((s, width), lambda hp, j: (0, hp))
    whole_heads = pl.BlockSpec((2, s, 1), lambda hp, j: (hp, 0, 0))
    blk = lambda width: pl.BlockSpec((tq, width), lambda hp, j: (j, hp))
    crow = pl.BlockSpec((2, 1, 1, tq), lambda hp, j: (hp, j, 0, 0))
    return pl.pallas_call(
        body, name="attn_bwd",
        out_shape=(jax.ShapeDtypeStruct((s, D_ATT), BF16), jax.ShapeDtypeStruct((s, D_ATT), BF16),
                   jax.ShapeDtypeStruct((s, D_ATT), BF16), jax.ShapeDtypeStruct((N_HEADS, nq, 1, tq), F32)),
        grid=(N_PAIRS, nq),
        in_specs=[whole(2 * w), whole(w), whole_heads, whole_heads, blk(2 * w), blk(w)],
        out_specs=(whole(w), blk(w), blk(w), crow),
        scratch_shapes=[pltpu.VMEM((s, w), F32)],
        compiler_params=_cparams("parallel", "arbitrary"),
    )(qa, do, lse, delta, ka, vb)


def _pair_sums(x, first):
    total = jnp.sum(x, axis=-1, keepdims=True)
    head = jnp.sum(jnp.where(first, x, 0.0), axis=-1, keepdims=True)
    return head, total - head


def _pair_mean(x, first):
    head, tail = _pair_sums(x, first)
    return jnp.where(first, head, tail) * (1.0 / GROUP_DIM)


def _gm_pair_norm(v2, first):
    d = v2 - _pair_mean(v2, first)
    rstd = lax.rsqrt(_pair_mean(d * d, first) + EPS)
    return d * rstd, rstd


def _gm_pair_mix(w_ref, pr, rhs, first):
    return jnp.where(first, jnp.dot(w_ref[2 * pr], rhs, preferred_element_type=F32),
                     jnp.dot(w_ref[2 * pr + 1], rhs, preferred_element_type=F32))


GM_STEP = 4 * CHUNK


def _gmlp_fwd(z, wt, bs_t, vgain):
    s = z.shape[0]
    step = _tile(s, GM_STEP, CHUNK)

    def body(gu_ref, gv_ref, wt_ref, bs_ref, vg_ref, o_ref):
        first = _head_lanes()
        for c in range(step // CHUNK):
            rows = slice(c * CHUNK, (c + 1) * CHUNK)
            for pr in range(N_GROUPS // 2):
                sl = slice(2 * pr * GROUP_DIM, 2 * (pr + 1) * GROUP_DIM)
                vhat, _ = _gm_pair_norm(_gelu(gv_ref[rows, sl]), first)
                vn = (vhat * vg_ref[:, sl]).astype(BF16)
                bias = jnp.where(first, bs_ref[:, 2 * pr:2 * pr + 1], bs_ref[:, 2 * pr + 1:2 * pr + 2])
                o_ref[rows, sl] = _gelu(gu_ref[rows, sl]) * (_gm_pair_mix(wt_ref, pr, vn, first) + bias)

    full = lambda a: pl.BlockSpec(a.shape, lambda n: (0,) * a.ndim)
    return pl.pallas_call(
        body, name="gmlp_fwd", out_shape=jax.ShapeDtypeStruct((s, D_GM), F32), grid=(s // step,),
        in_specs=[pl.BlockSpec((step, D_GM), lambda n: (n, 3)), pl.BlockSpec((step, D_GM), lambda n: (n, 4)),
                  full(wt), full(bs_t), full(vgain)],
        out_specs=pl.BlockSpec((step, D_GM), lambda n: (n, 0)),
        compiler_params=_cparams("parallel"),
    )(z, z, wt, bs_t, vgain)


def _gmlp_bwd(z, dgm, wt, wt_t, bs_t, vgain):
    s = z.shape[0]
    step = CHUNK

    def body(gu_ref, gv_ref, dgm_ref, wt_ref, wtt_ref, bs_ref, vg_ref, dgu_ref, dgv_ref, dwt_ref, dbs_ref, dvg_ref):
        @pl.when(pl.program_id(0) == 0)
        def _():
            dwt_ref[...] = jnp.zeros_like(dwt_ref)
            dbs_ref[...] = jnp.zeros_like(dbs_ref)
            dvg_ref[...] = jnp.zeros_like(dvg_ref)

        first = _head_lanes()
        for c in range(step // CHUNK):
            rows = slice(c * CHUNK, (c + 1) * CHUNK)
            for pr in range(N_GROUPS // 2):
                g0, g1 = 2 * pr, 2 * pr + 1
                sl = slice(g0 * GROUP_DIM, (g1 + 1) * GROUP_DIM)
                gu = gu_ref[rows, sl]
                gv = gv_ref[rows, sl]
                dgm = dgm_ref[rows, sl]
                vhat, rstd = _gm_pair_norm(_gelu(gv), first)
                gain = vg_ref[:, sl]
                vn = (vhat * gain).astype(BF16)
                bias = jnp.where(first, bs_ref[:, g0:g0 + 1], bs_ref[:, g1:g1 + 1])
                mixed = _gm_pair_mix(wt_ref, pr, vn, first) + bias
                dgu_ref[rows, sl] = (dgm * mixed * _gelu_grad(gu)).astype(BF16)
                dmixed = dgm * _gelu(gu)
                db0, db1 = _pair_sums(dmixed, first)
                dbs_ref[:, g0:g0 + 1] += db0
                dbs_ref[:, g1:g1 + 1] += db1
                dwt_ref[g0] += lax.dot_general(_pick(dmixed, first, 0).astype(BF16), vn, NT, preferred_element_type=F32)
                dwt_ref[g1] += lax.dot_general(_pick(dmixed, first, 1).astype(BF16), vn, NT, preferred_element_type=F32)
                dvn = _gm_pair_mix(wtt_ref, pr, dmixed.astype(BF16), first)
                dvg_ref[:, sl] += _colsum(dvn * vhat)
                dvhat = dvn * gain
                dvf = rstd * (dvhat - _pair_mean(dvhat, first) - vhat * _pair_mean(dvhat * vhat, first))
                dgv_ref[rows, sl] = (dvf * _gelu_grad(gv)).astype(BF16)

    full = lambda a: pl.BlockSpec(a.shape, lambda n: (0,) * a.ndim)
    rows_spec = pl.BlockSpec((step, D_GM), lambda n: (n, 0))
    return pl.pallas_call(
        body, name="gmlp_bwd",
        out_shape=(jax.ShapeDtypeStruct((s, D_GM), BF16), jax.ShapeDtypeStruct((s, D_GM), BF16),
                   jax.ShapeDtypeStruct(wt.shape, F32), jax.ShapeDtypeStruct(bs_t.shape, F32),
                   jax.ShapeDtypeStruct(vgain.shape, F32)),
        grid=(s // step,),
        in_specs=[pl.BlockSpec((step, D_GM), lambda n: (n, 3)), pl.BlockSpec((step, D_GM), lambda n: (n, 4)),
                  rows_spec, full(wt), full(wt_t), full(bs_t), full(vgain)],
        out_specs=(rows_spec, rows_spec, full(wt), full(bs_t), full(vgain)),
        compiler_params=_cparams("arbitrary"),
    )(z, z, dgm, wt, wt_t, bs_t, vgain)


def _dz_concat(wide, dfl, ts):
    s = dfl.shape[0]
    ts = _tile(s, ts, 16)
    n = len(wide)

    def body(*refs):
        o_ref = refs[-1]
        for k in range(n):
            o_ref[:, k * D_ATT:(k + 1) * D_ATT] = refs[k][...]
        o_ref[:, F_OFF:] = jnp.zeros((ts, D_IN_PAD - F_OFF), BF16)
        o_ref[:, F_OFF:F_OFF + N_HEADS] = refs[n][...].astype(BF16)

    return pl.pallas_call(
        body, name="dz_concat", out_shape=jax.ShapeDtypeStruct((s, D_IN_PAD), BF16), grid=(s // ts,),
        in_specs=[pl.BlockSpec((ts, D_ATT), lambda i: (i, 0))] * n + [pl.BlockSpec((ts, N_HEADS), lambda i: (i, 0))],
        out_specs=pl.BlockSpec((ts, D_IN_PAD), lambda i: (i, 0)),
        compiler_params=_cparams("parallel"),
    )(*wide, dfl)


def _layer_fwd(h0, p_i, w, tq, late):
    s, d = h0.shape
    sv = {"h0": h0}

    (hn1,) = _rowwise(lambda h, g: _rms(h)[0] * g, [h0], [w["mix_pre_norm"]], [(d, BF16)], [], "pre_mix", ROW_TILE)
    z = _mm(hn1, w["w_in"], "nn", F32, "mm_in")
    fl_t = z[:, F_OFF:F_OFF + N_HEADS].T
    c_t = _forget_fwd(fl_t, w["b_forget"])
    qa, ka, vb = _attn_prep(z, c_t.T, ROW_TILE)
    att, lse = _attn_fwd(qa, ka, vb, tq)
    gm = _gmlp_fwd(z, w["wt"], w["bs_t"], w["gm_v_norm"])
    w = dict(w, **late(att))

    def mix_out(att, gm, g):
        return jnp.concatenate([_rms(att)[0] * g[:, :D_ATT], _rms(gm)[0] * g[:, D_ATT:]], axis=-1)

    (mc,) = _rowwise(mix_out, [att, gm], [w["mix_out_norm"]], [(D_ATT + D_GM, BF16)], [], "mix_out", ROW_TILE)
    y1 = _mm(mc, w["w_out"], "nn", BF16, "mm_out")

    def post_mix(h0, y1, gpost, gpre):
        h1 = h0 + _rms(y1)[0] * gpost
        return h1, _rms(h1)[0] * gpre

    h1, hn2 = _rowwise(post_mix, [h0, y1], [w["mix_post_norm"], w["ffn_pre_norm"]],
                       [(d, F32), (d, BF16)], [], "post_mix", ROW_TILE)
    ab, t = _ffn_in_swiglu(hn2, w["w_ffn_in"])
    y2 = _mm(t, w["w_ffn_out"], "nn", BF16, "mm_ffn_out", a3="k", b3="k")

    def post_ffn(h1, y2, g):
        h2 = h1 + _rms(y2)[0] * g
        return h2, _rms(h2)[0]

    h2, hr = _rowwise(post_ffn, [h1, y2], [w["ffn_post_norm"]], [(d, F32), (d, BF16)], [], "post_ffn", ROW_TILE)
    gl = _mm(hr, w["w_ple_gate"], "nn", BF16, "mm_gate")
    pe = _mm(p_i, w["w_ple"], "nn", BF16, "mm_ple")
    (h3,) = _rowwise(lambda h2, gl, pe, g: h2 + _sigmoid(gl) * (_rms(pe)[0] * g), [h2, gl, pe], [w["ple_norm"]],
                     [(d, F32)], [], "ple_out", ROW_TILE)
    sv.update(hn1=hn1, z=z, fl_t=fl_t, qa=qa, ka=ka, vb=vb, lse=lse, att=att, gm=gm,
              mc=mc, y1=y1, h1=h1, hn2=hn2, ab=ab, y2=y2, h2=h2, hr=hr, gl=gl, pe=pe, p_i=p_i)
    return h3, sv


def _layer_bwd(dh3, sv, w, tq, mid):
    s, d = dh3.shape
    g = {}
    by_rows = lambda a: a.reshape(N_DEV, -1, a.shape[-1])
    by_cols = lambda a: jnp.stack(jnp.split(a, N_DEV, axis=-1))

    def ple_bwd(dh3, gl, pe, gple):
        gate = _sigmoid(gl)
        pehat, rpe = _rms(pe)
        dgl = dh3 * (pehat * gple) * gate * (1.0 - gate)
        de = dh3 * gate
        return dgl, _rms_bwd(pehat, rpe, de * gple), _colsum(de * pehat)

    dgl, dpe, g["ple_norm"] = _rowwise(ple_bwd, [dh3, sv["gl"], sv["pe"]], [w["ple_norm"]],
                                       [(d, BF16), (d, BF16)], [d], "ple_bwd", ROW_TILE)
    g["w_ple_gate"] = by_rows(_mm(sv["hr"], dgl, "tn", BF16, "mm_dgate"))
    dhr = _mm(dgl, w["w_ple_gate"], "nt", BF16, "mm_dhr")
    g["w_ple"] = by_cols(_mm(sv["p_i"], dpe, "tn", BF16, "mm_dple"))

    def ffn_post_bwd(dh3, dhr, h2, y2, gpost):
        h2hat, r2 = _rms(h2)
        dh2 = dh3 + _rms_bwd(h2hat, r2, dhr)
        y2hat, ry = _rms(y2)
        return dh2, _rms_bwd(y2hat, ry, dh2 * gpost), _colsum(dh2 * y2hat)

    dh2, dy2, g["ffn_post_norm"] = _rowwise(ffn_post_bwd, [dh3, dhr, sv["h2"], sv["y2"]], [w["ffn_post_norm"]],
                                            [(d, F32), (d, BF16)], [d], "ffn_post_bwd", ROW_TILE)
    t, dab = _ffn_out_swiglu_bwd(dy2, w["w_ffn_out"], sv["ab"])
    dab = dab.reshape((N_DEV,) + dab.shape[2:])
    g["w_ffn_out"] = by_rows(_mm(t, dy2, "tn", BF16, "mm_dffn_out", a3="m", o3="m"))
    dhn2 = _mm(dab, w["w_ffn_in"], "nn", BF16, "mm_dhn2", a3="k", b3="k")
    g["w_ffn_in"] = _mm(dab, sv["hn2"], "tn", BF16, "mm_dffn_in", a3="m", o3="m")

    def mix_post_bwd(dh2, dhn2, h1, y1, gpre, gpost):
        h1hat, r1 = _rms(h1)
        dh1 = dh2 + _rms_bwd(h1hat, r1, dhn2 * gpre)
        y1hat, ry = _rms(y1)
        return dh1, _rms_bwd(y1hat, ry, dh1 * gpost), _colsum(dhn2 * h1hat), _colsum(dh1 * y1hat)

    dh1, dy1, g["ffn_pre_norm"], g["mix_post_norm"] = _rowwise(
        mix_post_bwd, [dh2, dhn2, sv["h1"], sv["y1"]], [w["ffn_pre_norm"], w["mix_post_norm"]],
        [(d, F32), (d, BF16)], [d, d], "mix_post_bwd", ROW_TILE)
    dmc = _mm(dy1, w["w_out"], "nt", BF16, "mm_dmc")
    g["w_out"] = by_rows(_mm(sv["mc"], dy1, "tn", BF16, "mm_dout"))
    w = dict(w, **mid(g, dmc))

    def mix_out_bwd(da, dg, att, gm, gain):
        atthat, ra = _rms(att)
        gmhat, rg = _rms(gm)
        dgain = jnp.concatenate([_colsum(da * atthat), _colsum(dg * gmhat)], axis=-1)
        return _rms_bwd(atthat, ra, da * gain[:, :D_ATT]), _rms_bwd(gmhat, rg, dg * gain[:, D_ATT:]), dgain

    datt, dgm, g["mix_out_norm"] = _rowwise(
        mix_out_bwd, [(dmc, 0, D_ATT), (dmc, 1, D_GM), sv["att"], sv["gm"]], [w["mix_out_norm"]],
        [(D_ATT, BF16), (D_GM, F32)], [D_ATT + D_GM], "mix_out_bwd", ROW_TILE)

    dgu, dgv, dwt, dbs_t, g["gm_v_norm"] = _gmlp_bwd(sv["z"], dgm, w["wt"], w["wt_t"], w["bs_t"], w["gm_v_norm"])
    g["gm_w_s"] = dwt * jnp.tril(jnp.ones((CHUNK, CHUNK), F32))[None]
    g["gm_b_s"] = dbs_t.T

    delta = _attn_delta(sv["att"], datt, tq)
    dq, dk, dv, dc_row = _attn_bwd(sv["qa"], sv["ka"], sv["vb"], datt, sv["lse"], delta, tq)
    dfl_t, db = _forget_bwd(dc_row.reshape(N_HEADS, s), sv["fl_t"], w["b_forget"])
    g["b_forget"] = db.reshape(1, N_HEADS)
    dz = _dz_concat([dq, dk, dv, dgu, dgv], dfl_t.T, ROW_TILE)
    dhn1 = _mm(dz, w["w_in"], "nt", BF16, "mm_dhn1")
    g["w_in"] = _w_in_split(_mm(sv["hn1"], dz, "tn", BF16, "mm_din"))

    def mix_pre_bwd(dh1, dhn1, h0, gpre):
        h0hat, r0 = _rms(h0)
        return dh1 + _rms_bwd(h0hat, r0, dhn1 * gpre), _colsum(dhn1 * h0hat)

    dh0, g["mix_pre_norm"] = _rowwise(mix_pre_bwd, [dh1, dhn1, sv["h0"]], [w["mix_pre_norm"]],
                                      [(d, F32)], [d], "mix_pre_bwd", ROW_TILE)
    return dh0, g


ANY = pl.BlockSpec(memory_space=pl.ANY)


def _all_gather(xs, layer, name):
    n = len(xs)

    def body(*refs):
        x_refs, out_refs = refs[:n], refs[n:2 * n]
        send_sems, recv_sems, local_sems = refs[2 * n:]
        x, y, c = lax.axis_index("x"), lax.axis_index("y"), lax.axis_index("c")
        me, sibling = (x, y, c), (x, y, 1 - c)
        chips = [(1 - x, y), (x, 1 - y), (1 - x, 1 - y)]

        def shard(a):
            return x_refs[a] if layer is None else x_refs[a].at[layer]

        def rows(a, px, py, pc):
            return out_refs[a].at[4 * px + 2 * py + pc]

        def copy(a, kk, block, to, from_shard=False):
            return pltpu.make_async_remote_copy(
                src_ref=shard(a) if from_shard else rows(a, *block), dst_ref=rows(a, *block),
                send_sem=send_sems.at[7 * a + kk], recv_sem=recv_sems.at[7 * a + kk],
                device_id=to, device_id_type=MESH)

        mine = [pltpu.make_async_copy(shard(a), rows(a, *me), local_sems.at[a]) for a in range(n)]
        for cp in mine:
            cp.start()
        first = []
        for a in range(n):
            first.append(copy(a, 0, me, sibling, from_shard=True))
            first += [copy(a, 1 + j, me, (*chip, c), from_shard=True) for j, chip in enumerate(chips)]
        for cp in first:
            cp.start()
        passed = []
        for j, chip in enumerate(chips):
            for a in range(n):
                copy(a, 1 + j, (*chip, c), me).wait_recv()
                passed.append(copy(a, 4 + j, (*chip, c), sibling))
                passed[-1].start()
        for a in range(n):
            copy(a, 0, sibling, me).wait_recv()
        for j, chip in enumerate(chips):
            for a in range(n):
                copy(a, 4 + j, (*chip, 1 - c), me).wait_recv()
        for cp in first + passed:
            cp.wait_send()
        for cp in mine:
            cp.wait()

    shapes = [x.shape if layer is None else x.shape[1:] for x in xs]
    return pl.pallas_call(
        body, name=name, out_shape=[jax.ShapeDtypeStruct((N_DEV,) + sh, x.dtype) for sh, x in zip(shapes, xs)],
        in_specs=[ANY] * n, out_specs=[ANY] * n,
        scratch_shapes=[pltpu.SemaphoreType.DMA((7 * n,)), pltpu.SemaphoreType.DMA((7 * n,)),
                        pltpu.SemaphoreType.DMA((n,))],
    )(*xs)


HBM = pl.BlockSpec(memory_space=pltpu.HBM)
SEMS = pl.BlockSpec(memory_space=pltpu.SEMAPHORE)
EFFECT = pltpu.SideEffectType.DATAFLOW_SIDE_EFFECTING
FLIPS = tuple((fx, fy, fc) for fx in (0, 1) for fy in (0, 1) for fc in (0, 1))[1:]


def _exchange_copies(src_refs, land_refs, send_sems, recv_sems, layer, scatter):
    x, y, c = lax.axis_index("x"), lax.axis_index("y"), lax.axis_index("c")
    me = 4 * x + 2 * y + c
    copies = []
    for a, (src, land) in enumerate(zip(src_refs, land_refs)):
        for f, (fx, fy, fc) in enumerate(FLIPS):
            px, py, pc = (1 - x if fx else x), (1 - y if fy else y), (1 - c if fc else c)
            if scatter:
                block = src.at[4 * px + 2 * py + pc]
            else:
                block = src if layer is None else src.at[layer]
            copies.append(pltpu.make_async_remote_copy(
                src_ref=block, dst_ref=land.at[me], send_sem=send_sems.at[7 * a + f], recv_sem=recv_sems.at[7 * a + f],
                device_id=(px, py, pc), device_id_type=MESH))
    return copies


def _exchange_start(srcs, lands, layer, scatter, name):
    n = len(srcs)

    def body(*refs):
        for cp in _exchange_copies(refs[:n], refs[n:2 * n], refs[2 * n], refs[2 * n + 1], layer, scatter):
            cp.start()
        token = refs[-1]
        token[...] = jnp.zeros_like(token)

    operands = list(srcs) + list(lands)
    outs = pl.pallas_call(
        body, name=name,
        out_shape=(pltpu.SemaphoreType.DMA((7 * n,)), pltpu.SemaphoreType.DMA((7 * n,)),
                   *[pltpu.HBM(a.shape, a.dtype) for a in operands], jax.ShapeDtypeStruct((8, LANE), F32)),
        in_specs=[HBM] * (2 * n),
        out_specs=(SEMS, SEMS, *[HBM] * (2 * n), pl.BlockSpec(memory_space=pltpu.VMEM)),
        input_output_aliases={i: 2 + i for i in range(2 * n)},
        compiler_params=pltpu.CompilerParams(has_side_effects=EFFECT),
    )(*[pltpu.with_memory_space_constraint(a, pltpu.HBM) for a in operands])
    return outs[0], outs[1], outs[2:2 + n], outs[2 + n:2 + 2 * n], outs[-1]


def _exchange_wait(started, after, layer, scatter, name):
    send_sems, recv_sems, srcs, lands, _ = started
    n = len(srcs)

    def body(*refs):
        for cp in _exchange_copies(refs[:n], refs[n:2 * n], refs[2 * n], refs[2 * n + 1], layer, scatter):
            cp.wait_send()
            cp.wait_recv()

    operands = list(srcs) + list(lands)
    outs = pl.pallas_call(
        body, name=name, out_shape=tuple(pltpu.HBM(a.shape, a.dtype) for a in operands),
        in_specs=[HBM] * (2 * n) + [SEMS, SEMS, ANY], out_specs=[HBM] * (2 * n),
        input_output_aliases={i: i for i in range(2 * n)},
        compiler_params=pltpu.CompilerParams(has_side_effects=EFFECT),
    )(*operands, send_sems, recv_sems, after)
    return outs[:n], outs[n:]


def _sum_devices(parts):
    _, r, c = parts.shape

    def body(p_ref, o_ref):
        acc = p_ref[0].astype(F32)
        for j in range(1, N_DEV):
            acc = acc + p_ref[j].astype(F32)
        o_ref[...] = acc

    return pl.pallas_call(
        body, name="small_sum", out_shape=jax.ShapeDtypeStruct((r, c), F32), grid=(r // SMALL_ROWS,),
        in_specs=[pl.BlockSpec((N_DEV, SMALL_ROWS, c), lambda i: (0, i, 0))],
        out_specs=pl.BlockSpec((SMALL_ROWS, c), lambda i: (i, 0)),
        compiler_params=_cparams("parallel"),
    )(parts)


def _adamw_math(w, g, m, v):
    m = ADAM_B1 * m + (1.0 - ADAM_B1) * g
    v = ADAM_B2 * v + (1.0 - ADAM_B2) * (g * g)
    m_hat = m / (1.0 - ADAM_B1 ** ADAM_STEP)
    v_hat = v / (1.0 - ADAM_B2 ** ADAM_STEP)
    return -ADAM_LR * (m_hat / (jnp.sqrt(v_hat) + ADAM_EPS) + ADAM_WD * w), m, v


def _adamw_shard(w, m, v, parts, layer, outs, name):
    _, a, b = w.shape
    ta = _tile(a, 256, 16)
    if outs is None:
        outs = [lax.empty(w.shape, F32) for _ in range(4)]

    def body(w_ref, m_ref, v_ref, p_ref, *refs):
        g_ref, d_ref, nm_ref, nv_ref = refs[4:]
        g = p_ref[0].astype(F32)
        for j in range(1, N_DEV):
            g = g + p_ref[j].astype(F32)
        g_ref[0] = g
        d_ref[0], nm_ref[0], nv_ref[0] = _adamw_math(w_ref[0], g, m_ref[0], v_ref[0])

    mine = pl.BlockSpec((1, ta, b), lambda i: (layer, i, 0))
    return pl.pallas_call(
        body, name=name, out_shape=[jax.ShapeDtypeStruct(w.shape, F32)] * 4, grid=(a // ta,),
        in_specs=[mine, mine, mine, pl.BlockSpec((N_DEV, ta, b), lambda i: (0, i, 0))] + [ANY] * 4,
        out_specs=[mine] * 4, input_output_aliases={4 + k: k for k in range(4)},
        compiler_params=_cparams("parallel"),
    )(w, m, v, parts, *outs)


def _w_in_moves():
    n = D_IN // N_DEV
    runs = ((0, 3 * D_ATT, 0), (3 * D_ATT, 3 * D_ATT + N_HEADS, F_OFF), (3 * D_ATT + N_HEADS, D_IN, 3 * D_ATT))
    moves = []
    for j in range(N_DEV):
        for lo, hi, padded in runs:
            a, b = max(n * j, lo), min(n * (j + 1), hi)
            if a < b:
                moves.append((j, a - n * j, padded + a - lo, b - a))
    return moves


def _w_in_assemble(blocks):
    _, d, n = blocks.shape
    tr = _tile(d, 256, 16)

    def body(b_ref, o_ref):
        o_ref[:, D_IN:] = jnp.zeros((tr, D_IN_PAD - D_IN), o_ref.dtype)
        for j, col, padded, width in _w_in_moves():
            o_ref[:, padded:padded + width] = b_ref[j, :, col:col + width]

    return pl.pallas_call(
        body, name="w_in_assemble", out_shape=jax.ShapeDtypeStruct((d, D_IN_PAD), blocks.dtype), grid=(d // tr,),
        in_specs=[pl.BlockSpec((N_DEV, tr, n), lambda i: (0, i, 0))],
        out_specs=pl.BlockSpec((tr, D_IN_PAD), lambda i: (i, 0)),
        compiler_params=_cparams("parallel"),
    )(blocks)


def _w_in_split(padded):
    d = padded.shape[0]
    n = D_IN // N_DEV
    tr = _tile(d, 256, 16)

    def body(p_ref, o_ref):
        for j, col, src, width in _w_in_moves():
            o_ref[j, :, col:col + width] = p_ref[:, src:src + width]

    return pl.pallas_call(
        body, name="w_in_split", out_shape=jax.ShapeDtypeStruct((N_DEV, d, n), padded.dtype), grid=(d // tr,),
        in_specs=[pl.BlockSpec((tr, D_IN_PAD), lambda i: (i, 0))],
        out_specs=pl.BlockSpec((N_DEV, tr, n), lambda i: (0, i, 0)),
        compiler_params=_cparams("parallel"),
    )(padded)


def _small_rows(size):
    return -(-size // (8 * SMALL_COLS)) * 8


def _pack_small(pieces):
    rows = []
    for p in pieces:
        flat = p.reshape(-1)
        rows.append(jnp.pad(flat, (0, _small_rows(flat.shape[0]) * SMALL_COLS - flat.shape[0])).reshape(-1, SMALL_COLS))
    used = sum(r.shape[0] for r in rows)
    rows.append(jnp.zeros((-used % SMALL_ROWS, SMALL_COLS), F32))
    return jnp.concatenate(rows)


def kernel(x, p, mix_pre_norm, mix_post_norm, w_in, b_forget, gm_v_norm, gm_w_s, gm_b_s, mix_out_norm, w_out, ffn_pre_norm, ffn_post_norm, w_ffn_in, w_ffn_out, w_ple, ple_norm, w_ple_gate, loss_target, m_mix_pre_norm, m_mix_post_norm, m_w_in, m_b_forget, m_gm_v_norm, m_gm_w_s, m_gm_b_s, m_mix_out_norm, m_w_out, m_ffn_pre_norm, m_ffn_post_norm, m_w_ffn_in, m_w_ffn_out, m_w_ple, m_ple_norm, m_w_ple_gate, v_mix_pre_norm, v_mix_post_norm, v_w_in, v_b_forget, v_gm_v_norm, v_gm_w_s, v_gm_b_s, v_mix_out_norm, v_w_out, v_ffn_pre_norm, v_ffn_post_norm, v_w_ffn_in, v_w_ffn_out, v_w_ple, v_ple_norm, v_w_ple_gate):
    given = dict(locals())
    weights = {n: given[n] for n in WEIGHT_ORDER}
    mom_m = {n: given["m_" + n] for n in WEIGHT_ORDER}
    mom_v = {n: given["v_" + n] for n in WEIGHT_ORDER}
    depth = w_in.shape[0]
    s, d = x.shape[1], x.shape[2]
    tq = _tile(s, ATT_BLOCK)
    me = 4 * lax.axis_index("x") + 2 * lax.axis_index("y") + lax.axis_index("c")
    tril = jnp.tril(jnp.ones((CHUNK, CHUNK), F32))

    def landing(block):
        return lax.dynamic_update_index_in_dim(lax.empty((N_DEV,) + block.shape, block.dtype), block, me, 0)

    def mix_weights(i, got):
        wt = gm_w_s[i] * tril[None]
        lw = dict(
            w_in=_w_in_assemble(got["w_in"]),
            b_forget=b_forget[i][:, None], wt=wt.astype(BF16), wt_t=wt.transpose(0, 2, 1).astype(BF16),
            bs_t=gm_b_s[i].T)
        lw.update({n: weights[n][i][None] for n in ("mix_pre_norm", "mix_post_norm", "gm_v_norm", "mix_out_norm",
                                                    "ffn_pre_norm", "ffn_post_norm", "ple_norm")})
        return lw

    def rest_weights(got):
        return dict(w_out=got["w_out"].reshape(-1, d), w_ffn_in=got["w_ffn_in"],
                    w_ffn_out=got["w_ffn_out"].reshape(N_DEV // 2, -1, d),
                    w_ple=jnp.concatenate([got["w_ple"][j] for j in range(N_DEV)], axis=-1),
                    w_ple_gate=got["w_ple_gate"].reshape(-1, d))

    def shard_view(n, a):
        return jnp.transpose(a, (0, 2, 1)) if n in TRANSPOSED_WEIGHTS else a

    shards = {n: shard_view(n, weights[n].astype(BF16)) for n in MATRIX_WEIGHTS}

    def gather_start(i):
        started = {}
        order = jnp.zeros((), BF16)
        for tag, grp in EXCHANGE_GROUPS.items():
            started[tag] = _exchange_start([shards[n] for n in grp], [landing(shards[n][i] + order) for n in grp], i,
                                           False, f"weights_gather_start_{i}_{tag}")
            order = started[tag][4][0, 0].astype(BF16)
        return started

    def gather_finish(i, tag, pending, after):
        srcs, got = _exchange_wait(pending[tag], after, i, False, f"weights_gather_wait_{i}_{tag}")
        shards.update(zip(EXCHANGE_GROUPS[tag], srcs))
        return dict(zip(EXCHANGE_GROUPS[tag], got))

    h = x[0]
    saved, layer_w = [], []
    pending = gather_start(0)
    for i in range(depth):
        lw = mix_weights(i, gather_finish(i, "mix", pending, h))
        if i == 0:
            lw["mix_pre_norm"] = lw["mix_pre_norm"] + pending["rest"][4][:1, :1]
        following = {}

        def late(att, i=i, pending=pending, lw=lw, following=following):
            rest = rest_weights(gather_finish(i, "rest", pending, att))
            lw.update(rest)
            if i + 1 == depth:
                return rest
            following.update(gather_start(i + 1))
            token = following["mix"][4][:1, :1] + following["rest"][4][:1, :1]
            return dict(rest, mix_out_norm=lw["mix_out_norm"] + token)

        h, sv = _layer_fwd(h, p[i, 0], lw, tq, late)
        layer_w.append(lw)
        saved.append(sv)
        pending = following

    def loss_head(y, t):
        err = y - t
        return err * (1.0 / d), _colsum(err * err)

    dh, sq = _rowwise(loss_head, [h, loss_target[0]], [], [(d, F32)], [d], "loss_head", ROW_TILE)
    loss = lax.psum(0.5 * jnp.sum(sq) / d, AXES)

    layer_g = [None] * depth
    shard_out = {n: None for n in MATRIX_WEIGHTS}

    def scatter_start(i, tag, g):
        full_g = [g[n] for n in EXCHANGE_GROUPS[tag]]
        lands = [landing(lax.dynamic_index_in_dim(gf, me, 0, keepdims=False)) for gf in full_g]
        return _exchange_start(full_g, lands, None, True, f"grads_scatter_start_{i}_{tag}")

    def scatter_finish(i, tag, started, after):
        _, parts = _exchange_wait(started[tag], after, None, True, f"grads_scatter_wait_{i}_{tag}")
        for n, part in zip(EXCHANGE_GROUPS[tag], parts):
            shard_out[n] = _adamw_shard(shard_view(n, weights[n]), shard_view(n, mom_m[n]), shard_view(n, mom_v[n]),
                                        part, i, shard_out[n], "adamw_" + n)

    before = None
    for i in reversed(range(depth)):
        lw = layer_w[i]
        if before is not None:
            lw = dict(lw, ple_norm=lw["ple_norm"] + before[1]["mix"][4][:1, :1])
        started = {}

        def mid(g, dmc, i=i, before=before, started=started, lw=lw):
            if before is not None:
                scatter_finish(before[0], "rest", before[1], dmc)
            started["rest"] = scatter_start(i, "rest", g)
            return dict(mix_out_norm=lw["mix_out_norm"] + started["rest"][4][:1, :1])

        dh, layer_g[i] = _layer_bwd(dh, saved[i], lw, tq, mid)
        if before is not None:
            scatter_finish(before[0], "mix", before[1], dh)
        started["mix"] = scatter_start(i, "mix", layer_g[i])
        before = (i, started)
    scatter_finish(before[0], "rest", before[1], before[1]["mix"][4])
    scatter_finish(before[0], "mix", before[1], dh)
    grad_x = dh[None]

    grads, deltas, new_m, new_v = {}, {}, {}, {}
    for n in MATRIX_WEIGHTS:
        grads[n], deltas[n], new_m[n], new_v[n] = (shard_view(n, a) for a in shard_out[n])

    small_g = _pack_small([jnp.stack([layer_g[i][n].reshape(-1) for i in range(depth)]) for n in SMALL_WEIGHTS])
    (gathered,) = _all_gather([small_g.astype(BF16)], None, "small_grads_all_gather")
    g_small = _sum_devices(gathered)
    pack = lambda t: _pack_small([t[n] for n in SMALL_WEIGHTS])
    dl, nm, nv = _rowwise(_adamw_math, [pack(weights), g_small, pack(mom_m), pack(mom_v)], [],
                          [(SMALL_COLS, F32)] * 3, [], "adamw_small", SMALL_ROWS)
    row = 0
    for n in SMALL_WEIGHTS:
        shp, size = weights[n].shape, weights[n].size
        grads[n], deltas[n], new_m[n], new_v[n] = (
            a[row:row + _small_rows(size)].reshape(-1)[:size].reshape(shp) for a in (g_small, dl, nm, nv))
        row += _small_rows(size)

    return (loss, grad_x, *[grads[n] for n in WEIGHT_ORDER], *[deltas[n] for n in WEIGHT_ORDER],
            *[new_m[n] for n in WEIGHT_ORDER], *[new_v[n] for n in WEIGHT_ORDER])
```

```python
import functools
import math

import jax
import jax.numpy as jnp
from jax import lax
from jax.experimental import pallas as pl
from jax.experimental.pallas import tpu as pltpu

F32 = jnp.float32
BF16 = jnp.bfloat16
MESH = pl.DeviceIdType.MESH
AXES = ("x", "y", "c")
N_DEV = 8

EPS = 1e-6
NEG_INF = -1e30
N_HEADS = 8
HEAD_DIM = 64
D_ATT = N_HEADS * HEAD_DIM
N_GROUPS = 8
GROUP_DIM = 64
D_GM = N_GROUPS * GROUP_DIM
CHUNK = 128
ATT_SCALE = HEAD_DIM ** -0.5
ATT_BLOCK = 1024
D_IN = 3 * D_ATT + N_HEADS + 2 * D_GM
D_IN_PAD = 3 * D_ATT + 2 * D_GM + 128
F_OFF = 3 * D_ATT + 2 * D_GM

ADAM_LR = 0.001
ADAM_B1 = 0.9
ADAM_B2 = 0.999
ADAM_EPS = 1e-08
ADAM_WD = 0.01
ADAM_STEP = 10

LANE = 128
VMEM_LIMIT = 48 * 1024 * 1024
ROW_TILE = 512
K_TILE = 4096
SMALL_COLS = 128
SMALL_ROWS = 512

MATRIX_WEIGHTS = ("w_in", "w_out", "w_ffn_in", "w_ffn_out", "w_ple", "w_ple_gate")
EXCHANGE_GROUPS = {"mix": ("w_in",), "rest": ("w_out", "w_ffn_in", "w_ffn_out", "w_ple", "w_ple_gate")}
TRANSPOSED_WEIGHTS = ("w_ffn_in",)
SMALL_WEIGHTS = ("mix_pre_norm", "mix_post_norm", "b_forget", "gm_v_norm", "gm_w_s", "gm_b_s",
                 "mix_out_norm", "ffn_pre_norm", "ffn_post_norm", "ple_norm")
WEIGHT_ORDER = ("mix_pre_norm", "mix_post_norm", "w_in", "b_forget", "gm_v_norm", "gm_w_s", "gm_b_s",
                "mix_out_norm", "w_out", "ffn_pre_norm", "ffn_post_norm", "w_ffn_in", "w_ffn_out",
                "w_ple", "ple_norm", "w_ple_gate")


def _tile(n, pref, unit=LANE):
    best = None
    t = unit
    while t <= min(n, pref):
        if n % t == 0:
            best = t
        t += unit
    return n if best is None else best


def _cparams(*semantics):
    return pltpu.CompilerParams(dimension_semantics=semantics or None, vmem_limit_bytes=VMEM_LIMIT)


NN = (((1,), (0,)), ((), ()))
NT = (((1,), (1,)), ((), ()))
TN = (((0,), (0,)), ((), ()))
_MM_AXES = {
    "nn": ("i", "k", "k", "j"), "nt": ("i", "k", "j", "k"), "tn": ("k", "i", "k", "j")}
_MM_DN = {"nn": NN, "nt": NT, "tn": TN}


def _mm(a, b, dims, out_dtype, name, a3=None, b3=None, o3=None, tm=1024, tn=1024, tk=None):
    ar, ac, br, bc = _MM_AXES[dims]
    letter = {"i": "m", "j": "n", "k": "k"}
    size = {}

    def measure(x, rows, cols, stacked):
        shape = x.shape
        if stacked is None:
            size.setdefault(letter[rows], shape[0])
            size.setdefault(letter[cols], shape[1])
        else:
            for ax, n in ((rows, shape[1]), (cols, shape[2])):
                size.setdefault(letter[ax], n * shape[0] if letter[ax] == stacked else n)

    measure(a, ar, ac, a3)
    measure(b, br, bc, b3)
    m, n, k = size["m"], size["n"], size["k"]
    slab = {}
    for x, stacked, rows, cols in ((a, a3, ar, ac), (b, b3, br, bc)):
        if stacked is not None:
            slab[stacked] = x.shape[1] if letter[rows] == stacked else x.shape[2]
    if o3 is not None:
        slab.setdefault(o3, slab.get(o3, None) or {"m": m, "n": n}[o3] // N_DEV)
    tk = tk or K_TILE
    tile = {"m": slab.get("m") or _tile(m, tm), "n": slab.get("n") or _tile(n, tn), "k": slab.get("k") or _tile(k, tk)}
    group = 1
    if a3 == "k" and b3 == "k":
        group = max(g for g in range(1, a.shape[0] + 1) if a.shape[0] % g == 0 and g * tile["k"] <= max(tk, tile["k"]))
    nk = k // (group * tile["k"])

    def spec(rows, cols, stacked):
        tr, tc = tile[letter[rows]], tile[letter[cols]]
        if stacked is None:
            return pl.BlockSpec((tr, tc), lambda i, j, kk: ({"i": i, "j": j, "k": kk}[rows], {"i": i, "j": j, "k": kk}[cols]))

        def imap(i, j, kk):
            g = {"i": i, "j": j, "k": kk}
            return (g[{"m": "i", "n": "j", "k": "k"}[stacked]],
                    0 if letter[rows] == stacked else g[rows], 0 if letter[cols] == stacked else g[cols])

        return pl.BlockSpec((group if stacked == "k" else 1, tr, tc), imap)

    dn = _MM_DN[dims]

    def body(a_ref, b_ref, o_ref, *acc):
        prod = None
        for g in range(group):
            av = a_ref[...] if a3 is None else a_ref[g]
            bv = b_ref[...] if b3 is None else b_ref[g]
            term = lax.dot_general(av.astype(BF16), bv.astype(BF16), dn, preferred_element_type=F32)
            prod = term if prod is None else prod + term

        def emit(val):
            if o3 is None:
                o_ref[...] = val.astype(out_dtype)
            else:
                o_ref[0] = val.astype(out_dtype)

        if nk == 1:
            emit(prod)
            return
        (acc_ref,) = acc
        kk = pl.program_id(2)

        @pl.when(kk == 0)
        def _():
            acc_ref[...] = prod

        @pl.when(kk > 0)
        def _():
            acc_ref[...] += prod

        @pl.when(kk == nk - 1)
        def _():
            emit(acc_ref[...])

    if o3 is None:
        out_shape = (m, n)
    elif o3 == "m":
        out_shape = (m // tile["m"], tile["m"], n)
    else:
        out_shape = (n // tile["n"], m, tile["n"])
    return pl.pallas_call(
        body, name=name, out_shape=jax.ShapeDtypeStruct(out_shape, out_dtype),
        grid=(m // tile["m"], n // tile["n"], nk),
        in_specs=[spec(ar, ac, a3), spec(br, bc, b3)], out_specs=spec("i", "j", o3),
        scratch_shapes=[] if nk == 1 else [pltpu.VMEM((tile["m"], tile["n"]), F32)],
        compiler_params=_cparams("parallel", "parallel", "arbitrary"),
    )(a, b)


def _rowwise(fn, rows, vecs, outs, reds, name, ts):
    rows = [r if isinstance(r, tuple) else (r, 0, r.shape[1]) for r in rows]
    s = rows[0][0].shape[0]
    ts = _tile(s, ts if reds or len(rows) > 3 else 2 * ts, 8)
    nr, nv, no = len(rows), len(vecs), len(outs)

    def body(*refs):
        vals = fn(*[r[...] for r in refs[:nr + nv]])
        vals = vals if isinstance(vals, tuple) else (vals,)
        o_refs = refs[nr + nv:nr + nv + no]
        r_refs = refs[nr + nv + no:]
        for o_ref, val in zip(o_refs, vals[:no]):
            o_ref[...] = val.astype(o_ref.dtype)
        if r_refs:
            @pl.when(pl.program_id(0) == 0)
            def _():
                for r_ref in r_refs:
                    r_ref[...] = jnp.zeros_like(r_ref)

            for r_ref, val in zip(r_refs, vals[no:]):
                r_ref[...] += val

    in_specs = [pl.BlockSpec((ts, w), functools.partial(lambda i, cb: (i, cb), cb=cb)) for _, cb, w in rows]
    in_specs += [pl.BlockSpec(v.shape, lambda i: (0, 0)) for v in vecs]
    out_specs = [pl.BlockSpec((ts, c), lambda i: (i, 0)) for c, _ in outs]
    out_specs += [pl.BlockSpec((1, c), lambda i: (0, 0)) for c in reds]
    out_shape = [jax.ShapeDtypeStruct((s, c), dt) for c, dt in outs]
    out_shape += [jax.ShapeDtypeStruct((1, c), F32) for c in reds]
    return pl.pallas_call(
        body, name=name, out_shape=out_shape, grid=(s // ts,), in_specs=in_specs, out_specs=out_specs,
        compiler_params=_cparams("arbitrary" if reds else "parallel"),
    )(*[r[0] for r in rows], *vecs)


def _rms(x):
    x = x.astype(F32)
    r = lax.rsqrt(jnp.mean(x * x, axis=-1, keepdims=True) + EPS)
    return x * r, r


def _rms_bwd(xhat, r, dyg):
    return r * (dyg - xhat * jnp.mean(dyg * xhat, axis=-1, keepdims=True))


def _colsum(x):
    return jnp.sum(x, axis=0, keepdims=True)


def _sigmoid(x):
    return 1.0 / (1.0 + jnp.exp(-x.astype(F32)))


GELU_C = math.sqrt(2.0 / math.pi)
GELU_A = 0.044715


def _gelu(x):
    return 0.5 * x * (1.0 + jnp.tanh(GELU_C * (x + GELU_A * x * x * x)))


def _gelu_grad(x):
    t = jnp.tanh(GELU_C * (x + GELU_A * x * x * x))
    return 0.5 * (1.0 + t) + 0.5 * x * (1.0 - t * t) * GELU_C * (1.0 + 3.0 * GELU_A * x * x)


def _ffn_in_swiglu(hn, wg):
    s, d = hn.shape
    g2, n, _ = wg.shape
    g = g2 // 2
    tm = _tile(s, 1024)

    def body(h_ref, wa_ref, wb_ref, ab_ref, t_ref):
        hv = h_ref[...]
        a = lax.dot_general(hv, wa_ref[0], NT, preferred_element_type=F32)
        b = lax.dot_general(hv, wb_ref[0], NT, preferred_element_type=F32)
        ab_ref[0, 0] = a.astype(BF16)
        ab_ref[1, 0] = b.astype(BF16)
        t_ref[0] = (a * _sigmoid(a) * b).astype(BF16)

    return pl.pallas_call(
        body, name="mm_ffn_in_swiglu",
        out_shape=(jax.ShapeDtypeStruct((2, g, s, n), BF16), jax.ShapeDtypeStruct((g, s, n), BF16)),
        grid=(s // tm, g),
        in_specs=[pl.BlockSpec((tm, d), lambda i, j: (i, 0)), pl.BlockSpec((1, n, d), lambda i, j: (j, 0, 0)),
                  pl.BlockSpec((1, n, d), lambda i, j: (j + g, 0, 0))],
        out_specs=(pl.BlockSpec((2, 1, tm, n), lambda i, j: (0, j, i, 0)),
                   pl.BlockSpec((1, tm, n), lambda i, j: (j, i, 0))),
        compiler_params=_cparams("parallel", "parallel"),
    )(hn, wg, wg)


def _ffn_out_swiglu_bwd(dy, w4, ab):
    s, d = dy.shape
    _, g, _, n = ab.shape
    tm = _tile(s, 1024)

    def body(dy_ref, w_ref, ab_ref, t_ref, dab_ref):
        dt = lax.dot_general(dy_ref[...], w_ref[0], NT, preferred_element_type=F32)
        a = ab_ref[0, 0].astype(F32)
        b = ab_ref[1, 0].astype(F32)
        sig = _sigmoid(a)
        silu = a * sig
        t_ref[0] = (silu * b).astype(BF16)
        dab_ref[0, 0] = (dt * b * (sig * (1.0 + a * (1.0 - sig)))).astype(BF16)
        dab_ref[1, 0] = (dt * silu).astype(BF16)

    both = pl.BlockSpec((2, 1, tm, n), lambda i, j: (0, j, i, 0))
    one = pl.BlockSpec((1, tm, n), lambda i, j: (j, i, 0))
    return pl.pallas_call(
        body, name="mm_dt_swiglu_bwd",
        out_shape=(jax.ShapeDtypeStruct((g, s, n), BF16), jax.ShapeDtypeStruct((2, g, s, n), BF16)),
        grid=(s // tm, g),
        in_specs=[pl.BlockSpec((tm, d), lambda i, j: (i, 0)), pl.BlockSpec((1, n, d), lambda i, j: (j, 0, 0)), both],
        out_specs=(one, both),
        compiler_params=_cparams("parallel", "parallel"),
    )(dy, w4, ab)


def _forget_fwd(fl_t, b_col):
    h, s = fl_t.shape
    nb = s // LANE

    def body(fl_ref, b_ref, c_ref):
        upper = (lax.broadcasted_iota(jnp.int32, (LANE, LANE), 0)
                 <= lax.broadcasted_iota(jnp.int32, (LANE, LANE), 1)).astype(F32)

        def step(i, carry):
            x = fl_ref[i] + b_ref[...]
            lf = jnp.minimum(x, 0.0) - jnp.log(1.0 + jnp.exp(-jnp.abs(x)))
            cs = jnp.dot(lf, upper, precision=lax.Precision.HIGHEST, preferred_element_type=F32) + carry
            c_ref[i] = cs
            return cs[:, LANE - 1:LANE]

        lax.fori_loop(0, nb, step, jnp.zeros((h, 1), F32))

    out = pl.pallas_call(
        body, name="forget_fwd", out_shape=jax.ShapeDtypeStruct((nb, h, LANE), F32),
        compiler_params=_cparams(),
    )(fl_t.reshape(h, nb, LANE).transpose(1, 0, 2), b_col)
    return out.transpose(1, 0, 2).reshape(h, s)


def _forget_bwd(dc_t, fl_t, b_col):
    h, s = fl_t.shape
    nb = s // LANE

    def body(dc_ref, fl_ref, b_ref, dfl_ref, db_ref):
        lower = (lax.broadcasted_iota(jnp.int32, (LANE, LANE), 0)
                 >= lax.broadcasted_iota(jnp.int32, (LANE, LANE), 1)).astype(F32)

        def step(t, carry):
            tail, db = carry
            i = nb - 1 - t
            rc = jnp.dot(dc_ref[i], lower, precision=lax.Precision.HIGHEST, preferred_element_type=F32) + tail
            dfl = rc * (1.0 - _sigmoid(fl_ref[i] + b_ref[...]))
            dfl_ref[i] = dfl
            return rc[:, 0:1], db + jnp.sum(dfl, axis=1, keepdims=True)

        _, db = lax.fori_loop(0, nb, step, (jnp.zeros((h, 1), F32), jnp.zeros((h, 1), F32)))
        db_ref[...] = db

    blocked = lambda a: a.reshape(h, nb, LANE).transpose(1, 0, 2)
    dfl, db = pl.pallas_call(
        body, name="forget_bwd",
        out_shape=(jax.ShapeDtypeStruct((nb, h, LANE), F32), jax.ShapeDtypeStruct((h, 1), F32)),
        compiler_params=_cparams(),
    )(blocked(dc_t), blocked(fl_t), b_col)
    return dfl.transpose(1, 0, 2).reshape(h, s), db


N_PAIRS = N_HEADS // 2


def _causal_mask(t):
    return lax.broadcasted_iota(jnp.int32, (t, t), 0) >= lax.broadcasted_iota(jnp.int32, (t, t), 1)


def _head_lanes():
    return lax.broadcasted_iota(jnp.int32, (1, 2 * HEAD_DIM), 1) < HEAD_DIM


def _pick(x2, first, hh):
    zero = jnp.zeros_like(x2)
    return jnp.where(first, x2, zero) if hh == 0 else jnp.where(first, zero, x2)


BIAS_TERMS = 3


def _attn_prep(z, c, ts):
    s = z.shape[0]
    ts = _tile(s, ts, 16)
    w = 2 * HEAD_DIM

    def body(q_ref, k_ref, v_ref, c_ref, qa_ref, ka_ref, vb_ref):
        lane = lax.broadcasted_iota(jnp.int32, (1, w), 1)
        first = lane < HEAD_DIM
        cv = c_ref[...]
        for h in range(N_HEADS):
            pair = slice((h // 2) * w, (h // 2 + 1) * w)
            qh = q_ref[:, pair] * ATT_SCALE
            kh = k_ref[:, pair]
            if h % 2:
                qh = pltpu.roll(qh, HEAD_DIM, 1)
                kh = pltpu.roll(kh, HEAD_DIM, 1)
            rest = cv[:, h:h + 1]
            q_tail = jnp.zeros((1, w), F32)
            k_tail = jnp.zeros((1, w), F32)
            for t in range(BIAS_TERMS):
                term = rest.astype(BF16).astype(F32)
                rest = rest - term
                q_tail = jnp.where(lane == HEAD_DIM + t, term, jnp.where(lane == HEAD_DIM + BIAS_TERMS + t, 1.0, q_tail))
                k_tail = jnp.where(lane == HEAD_DIM + t, 1.0, jnp.where(lane == HEAD_DIM + BIAS_TERMS + t, -term, k_tail))
            qa_ref[:, h * w:(h + 1) * w] = jnp.where(first, qh, q_tail).astype(BF16)
            ka_ref[:, h * w:(h + 1) * w] = jnp.where(first, kh, k_tail).astype(BF16)
        vb_ref[...] = v_ref[...].astype(BF16)

    col = lambda cb: pl.BlockSpec((ts, D_ATT), lambda i: (i, cb))
    wide = pl.BlockSpec((ts, N_HEADS * w), lambda i: (i, 0))
    return pl.pallas_call(
        body, name="attn_prep",
        out_shape=(jax.ShapeDtypeStruct((s, N_HEADS * w), BF16), jax.ShapeDtypeStruct((s, N_HEADS * w), BF16),
                   jax.ShapeDtypeStruct((s, D_ATT), BF16)),
        grid=(s // ts,), in_specs=[col(0), col(1), col(2), pl.BlockSpec((ts, N_HEADS), lambda i: (i, 0))],
        out_specs=(wide, wide, col(0)),
        compiler_params=_cparams("parallel"),
    )(z, z, z, c)


def _attn_fwd(qa, ka, vb, tq):
    s = qa.shape[0]
    nq = s // tq
    w = 2 * HEAD_DIM

    def body(q_ref, k_ref, v_ref, o_ref, lse_ref):
        i = pl.program_id(1)
        first = _head_lanes()
        q2 = q_ref[...]

        def block(j, carry, masked):
            off = pl.multiple_of(j * tq, tq)
            k2 = k_ref[pl.ds(off, tq), :]
            v2 = v_ref[pl.ds(off, tq), :]
            new = []
            for hh in range(2):
                m, l, acc = carry[hh]
                sc = lax.dot_general(q2[:, hh * w:(hh + 1) * w], k2[:, hh * w:(hh + 1) * w], NT,
                                     preferred_element_type=F32)
                if masked:
                    sc = jnp.where(_causal_mask(tq), sc, NEG_INF)
                m_new = jnp.maximum(m, jnp.max(sc, axis=-1, keepdims=True))
                alpha = jnp.exp(m - m_new)
                p = jnp.exp(sc - m_new)
                l = alpha * l + jnp.sum(p, axis=-1, keepdims=True)
                p_hi = p.astype(BF16)
                p_lo = (p - p_hi.astype(F32)).astype(BF16)
                acc = (alpha * acc + jnp.dot(p_hi, v2, preferred_element_type=F32)
                       + jnp.dot(p_lo, v2, preferred_element_type=F32))
                new.append((m_new, l, acc))
            return tuple(new)

        one = (jnp.full((tq, 1), NEG_INF, F32), jnp.zeros((tq, 1), F32), jnp.zeros((tq, w), F32))
        carry = lax.fori_loop(0, i, lambda j, c: block(j, c, False), (one, one))
        (m0, l0, a0), (m1, l1, a1) = block(i, carry, True)
        o_ref[...] = jnp.where(first, a0 / l0, a1 / l1)
        lse_ref[0] = m0 + jnp.log(l0)
        lse_ref[1] = m1 + jnp.log(l1)

    return pl.pallas_call(
        body, name="attn_fwd",
        out_shape=(jax.ShapeDtypeStruct((s, D_ATT), F32), jax.ShapeDtypeStruct((N_HEADS, s, 1), F32)),
        grid=(N_PAIRS, nq),
        in_specs=[pl.BlockSpec((tq, 2 * w), lambda hp, i: (i, hp)),
                  pl.BlockSpec((s, 2 * w), lambda hp, i: (0, hp)),
                  pl.BlockSpec((s, w), lambda hp, i: (0, hp))],
        out_specs=(pl.BlockSpec((tq, w), lambda hp, i: (i, hp)),
                   pl.BlockSpec((2, tq, 1), lambda hp, i: (hp, i, 0))),
        compiler_params=_cparams("parallel", "parallel"),
    )(qa, ka, vb)


def _attn_delta(o, do, tq):
    s = o.shape[0]
    w = 2 * HEAD_DIM

    def body(o_ref, do_ref, d_ref):
        first = _head_lanes()
        prod = o_ref[...] * do_ref[...].astype(F32)
        d_ref[0] = jnp.sum(_pick(prod, first, 0), axis=-1, keepdims=True)
        d_ref[1] = jnp.sum(_pick(prod, first, 1), axis=-1, keepdims=True)

    blk = pl.BlockSpec((tq, w), lambda hp, i: (i, hp))
    return pl.pallas_call(
        body, name="attn_delta", out_shape=jax.ShapeDtypeStruct((N_HEADS, s, 1), F32), grid=(N_PAIRS, s // tq),
        in_specs=[blk, blk], out_specs=pl.BlockSpec((2, tq, 1), lambda hp, i: (hp, i, 0)),
        compiler_params=_cparams("parallel", "parallel"),
    )(o, do)


def _attn_bwd(qa, ka, vb, do, lse, delta, tq):
    s = qa.shape[0]
    nq = s // tq
    w = 2 * HEAD_DIM

    def body(q_ref, do_ref, lse_ref, dl_ref, k_ref, v_ref, dq_ref, dk_ref, dv_ref, dc_ref, dq_acc):
        j = pl.program_id(1)
        first = _head_lanes()

        @pl.when(j == 0)
        def _():
            dq_acc[...] = jnp.zeros_like(dq_acc)

        k2 = k_ref[...]
        v2 = v_ref[...]

        def step(i, carry, masked):
            off = pl.multiple_of(i * tq, tq)
            rows = pl.ds(off, tq)
            q2 = q_ref[rows, :]
            do2 = do_ref[rows, :]
            new, dqs = [], []
            for hh in range(2):
                dk, dv, dcs = carry[hh]
                qh = q2[:, hh * w:(hh + 1) * w]
                kh = k2[:, hh * w:(hh + 1) * w]
                sc = lax.dot_general(qh, kh, NT, preferred_element_type=F32)
                if masked:
                    sc = jnp.where(_causal_mask(tq), sc, NEG_INF)
                p = jnp.exp(sc - lse_ref[hh, rows, :])
                dv = dv + lax.dot_general(do2, p.astype(BF16), TN, preferred_element_type=F32)
                dp = lax.dot_general(_pick(do2, first, hh), v2, NT, preferred_element_type=F32)
                ds = p * (dp - dl_ref[hh, rows, :])
                dsb = ds.astype(BF16)
                dk = dk + lax.dot_general(qh, dsb, TN, preferred_element_type=F32)
                dqs.append(jnp.dot(dsb, kh, preferred_element_type=F32))
                new.append((dk, dv, dcs + jnp.sum(ds, axis=0, keepdims=True)))
            dq_acc[rows, :] += jnp.where(first, dqs[0], pltpu.roll(dqs[1], HEAD_DIM, 1)) * ATT_SCALE
            return tuple(new)

        one = (jnp.zeros((w, tq), F32), jnp.zeros((w, tq), F32), jnp.zeros((1, tq), F32))
        carry = step(j, (one, one), True)
        (dk0, dv0, dc0), (dk1, dv1, dc1) = lax.fori_loop(j + 1, nq, lambda i, c: step(i, c, False), carry)
        dk_ref[...] = jnp.where(first, dk0.T, pltpu.roll(dk1.T, HEAD_DIM, 1)).astype(BF16)
        dv_ref[...] = jnp.where(first, dv0.T, dv1.T).astype(BF16)
        dc_ref[0, 0] = -dc0
        dc_ref[1, 0] = -dc1

        @pl.when(j == nq - 1)
        def _():
            dq_ref[...] = dq_acc[...].astype(BF16)

    whole = lambda width: pl.BlockSpec((s, width), lambda hp, j: (0, hp))
    whole_heads = pl.BlockSpec((2, s, 1), lambda hp, j: (hp, 0, 0))
    blk = lambda width: pl.BlockSpec((tq, width), lambda hp, j: (j, hp))
    crow = pl.BlockSpec((2, 1, 1, tq), lambda hp, j: (hp, j, 0, 0))
    return pl.pallas_call(
        body, name="attn_bwd",
        out_shape=(jax.ShapeDtypeStruct((s, D_ATT), BF16), jax.ShapeDtypeStruct((s, D_ATT), BF16),
                   jax.ShapeDtypeStruct((s, D_ATT), BF16), jax.ShapeDtypeStruct((N_HEADS, nq, 1, tq), F32)),
        grid=(N_PAIRS, nq),
        in_specs=[whole(2 * w), whole(w), whole_heads, whole_heads, blk(2 * w), blk(w)],
        out_specs=(whole(w), blk(w), blk(w), crow),
        scratch_shapes=[pltpu.VMEM((s, w), F32)],
        compiler_params=_cparams("parallel", "arbitrary"),
    )(qa, do, lse, delta, ka, vb)


def _pair_sums(x, first):
    total = jnp.sum(x, axis=-1, keepdims=True)
    head = jnp.sum(jnp.where(first, x, 0.0), axis=-1, keepdims=True)
    return head, total - head


def _pair_mean(x, first):
    head, tail = _pair_sums(x, first)
    return jnp.where(first, head, tail) * (1.0 / GROUP_DIM)


def _gm_pair_norm(v2, first):
    d = v2 - _pair_mean(v2, first)
    rstd = lax.rsqrt(_pair_mean(d * d, first) + EPS)
    return d * rstd, rstd


def _gm_pair_mix(w_ref, pr, rhs, first):
    return jnp.where(first, jnp.dot(w_ref[2 * pr], rhs, preferred_element_type=F32),
                     jnp.dot(w_ref[2 * pr + 1], rhs, preferred_element_type=F32))


GM_STEP = 4 * CHUNK


def _gmlp_fwd(z, wt, bs_t, vgain):
    s = z.shape[0]
    step = _tile(s, GM_STEP, CHUNK)

    def body(gu_ref, gv_ref, wt_ref, bs_ref, vg_ref, o_ref):
        first = _head_lanes()
        for c in range(step // CHUNK):
            rows = slice(c * CHUNK, (c + 1) * CHUNK)
            for pr in range(N_GROUPS // 2):
                sl = slice(2 * pr * GROUP_DIM, 2 * (pr + 1) * GROUP_DIM)
                vhat, _ = _gm_pair_norm(_gelu(gv_ref[rows, sl]), first)
                vn = (vhat * vg_ref[:, sl]).astype(BF16)
                bias = jnp.where(first, bs_ref[:, 2 * pr:2 * pr + 1], bs_ref[:, 2 * pr + 1:2 * pr + 2])
                o_ref[rows, sl] = _gelu(gu_ref[rows, sl]) * (_gm_pair_mix(wt_ref, pr, vn, first) + bias)

    full = lambda a: pl.BlockSpec(a.shape, lambda n: (0,) * a.ndim)
    return pl.pallas_call(
        body, name="gmlp_fwd", out_shape=jax.ShapeDtypeStruct((s, D_GM), F32), grid=(s // step,),
        in_specs=[pl.BlockSpec((step, D_GM), lambda n: (n, 3)), pl.BlockSpec((step, D_GM), lambda n: (n, 4)),
                  full(wt), full(bs_t), full(vgain)],
        out_specs=pl.BlockSpec((step, D_GM), lambda n: (n, 0)),
        compiler_params=_cparams("parallel"),
    )(z, z, wt, bs_t, vgain)


def _gmlp_bwd(z, dgm, wt, wt_t, bs_t, vgain):
    s = z.shape[0]
    step = CHUNK

    def body(gu_ref, gv_ref, dgm_ref, wt_ref, wtt_ref, bs_ref, vg_ref, dgu_ref, dgv_ref, dwt_ref, dbs_ref, dvg_ref):
        @pl.when(pl.program_id(0) == 0)
        def _():
            dwt_ref[...] = jnp.zeros_like(dwt_ref)
            dbs_ref[...] = jnp.zeros_like(dbs_ref)
            dvg_ref[...] = jnp.zeros_like(dvg_ref)

        first = _head_lanes()
        for c in range(step // CHUNK):
            rows = slice(c * CHUNK, (c + 1) * CHUNK)
            for pr in range(N_GROUPS // 2):
                g0, g1 = 2 * pr, 2 * pr + 1
                sl = slice(g0 * GROUP_DIM, (g1 + 1) * GROUP_DIM)
                gu = gu_ref[rows, sl]
                gv = gv_ref[rows, sl]
                dgm = dgm_ref[rows, sl]
                vhat, rstd = _gm_pair_norm(_gelu(gv), first)
                gain = vg_ref[:, sl]
                vn = (vhat * gain).astype(BF16)
                bias = jnp.where(first, bs_ref[:, g0:g0 + 1], bs_ref[:, g1:g1 + 1])
                mixed = _gm_pair_mix(wt_ref, pr, vn, first) + bias
                dgu_ref[rows, sl] = (dgm * mixed * _gelu_grad(gu)).astype(BF16)
                dmixed = dgm * _gelu(gu)
                db0, db1 = _pair_sums(dmixed, first)
                dbs_ref[:, g0:g0 + 1] += db0
                dbs_ref[:, g1:g1 + 1] += db1
                dwt_ref[g0] += lax.dot_general(_pick(dmixed, first, 0).astype(BF16), vn, NT, preferred_element_type=F32)
                dwt_ref[g1] += lax.dot_general(_pick(dmixed, first, 1).astype(BF16), vn, NT, preferred_element_type=F32)
                dvn = _gm_pair_mix(wtt_ref, pr, dmixed.astype(BF16), first)
                dvg_ref[:, sl] += _colsum(dvn * vhat)
                dvhat = dvn * gain
                dvf = rstd * (dvhat - _pair_mean(dvhat, first) - vhat * _pair_mean(dvhat * vhat, first))
                dgv_ref[rows, sl] = (dvf * _gelu_grad(gv)).astype(BF16)

    full = lambda a: pl.BlockSpec(a.shape, lambda n: (0,) * a.ndim)
    rows_spec = pl.BlockSpec((step, D_GM), lambda n: (n, 0))
    return pl.pallas_call(
        body, name="gmlp_bwd",
        out_shape=(jax.ShapeDtypeStruct((s, D_GM), BF16), jax.ShapeDtypeStruct((s, D_GM), BF16),
                   jax.ShapeDtypeStruct(wt.shape, F32), jax.ShapeDtypeStruct(bs_t.shape, F32),
                   jax.ShapeDtypeStruct(vgain.shape, F32)),
        grid=(s // step,),
        in_specs=[pl.BlockSpec((step, D_GM), lambda n: (n, 3)), pl.BlockSpec((step, D_GM), lambda n: (n, 4)),
                  rows_spec, full(wt), full(wt_t), full(bs_t), full(vgain)],
        out_specs=(rows_spec, rows_spec, full(wt), full(bs_t), full(vgain)),
        compiler_params=_cparams("arbitrary"),
    )(z, z, dgm, wt, wt_t, bs_t, vgain)


def _dz_concat(wide, dfl, ts):
    s = dfl.shape[0]
    ts = _tile(s, ts, 16)
    n = len(wide)

    def body(*refs):
        o_ref = refs[-1]
        for k in range(n):
            o_ref[:, k * D_ATT:(k + 1) * D_ATT] = refs[k][...]
        o_ref[:, F_OFF:] = jnp.zeros((ts, D_IN_PAD - F_OFF), BF16)
        o_ref[:, F_OFF:F_OFF + N_HEADS] = refs[n][...].astype(BF16)

    return pl.pallas_call(
        body, name="dz_concat", out_shape=jax.ShapeDtypeStruct((s, D_IN_PAD), BF16), grid=(s // ts,),
        in_specs=[pl.BlockSpec((ts, D_ATT), lambda i: (i, 0))] * n + [pl.BlockSpec((ts, N_HEADS), lambda i: (i, 0))],
        out_specs=pl.BlockSpec((ts, D_IN_PAD), lambda i: (i, 0)),
        compiler_params=_cparams("parallel"),
    )(*wide, dfl)


def _layer_fwd(h0, p_i, w, tq, late):
    s, d = h0.shape
    sv = {"h0": h0}

    (hn1,) = _rowwise(lambda h, g: _rms(h)[0] * g, [h0], [w["mix_pre_norm"]], [(d, BF16)], [], "pre_mix", ROW_TILE)
    z = _mm(hn1, w["w_in"], "nn", F32, "mm_in")
    fl_t = z[:, F_OFF:F_OFF + N_HEADS].T
    c_t = _forget_fwd(fl_t, w["b_forget"])
    qa, ka, vb = _attn_prep(z, c_t.T, ROW_TILE)
    att, lse = _attn_fwd(qa, ka, vb, tq)
    gm = _gmlp_fwd(z, w["wt"], w["bs_t"], w["gm_v_norm"])
    w = dict(w, **late(att))

    def mix_out(att, gm, g):
        return jnp.concatenate([_rms(att)[0] * g[:, :D_ATT], _rms(gm)[0] * g[:, D_ATT:]], axis=-1)

    (mc,) = _rowwise(mix_out, [att, gm], [w["mix_out_norm"]], [(D_ATT + D_GM, BF16)], [], "mix_out", ROW_TILE)
    y1 = _mm(mc, w["w_out"], "nn", BF16, "mm_out")

    def post_mix(h0, y1, gpost, gpre):
        h1 = h0 + _rms(y1)[0] * gpost
        return h1, _rms(h1)[0] * gpre

    h1, hn2 = _rowwise(post_mix, [h0, y1], [w["mix_post_norm"], w["ffn_pre_norm"]],
                       [(d, F32), (d, BF16)], [], "post_mix", ROW_TILE)
    ab, t = _ffn_in_swiglu(hn2, w["w_ffn_in"])
    y2 = _mm(t, w["w_ffn_out"], "nn", BF16, "mm_ffn_out", a3="k", b3="k")

    def post_ffn(h1, y2, g):
        h2 = h1 + _rms(y2)[0] * g
        return h2, _rms(h2)[0]

    h2, hr = _rowwise(post_ffn, [h1, y2], [w["ffn_post_norm"]], [(d, F32), (d, BF16)], [], "post_ffn", ROW_TILE)
    gl = _mm(hr, w["w_ple_gate"], "nn", BF16, "mm_gate")
    pe = _mm(p_i, w["w_ple"], "nn", BF16, "mm_ple")
    (h3,) = _rowwise(lambda h2, gl, pe, g: h2 + _sigmoid(gl) * (_rms(pe)[0] * g), [h2, gl, pe], [w["ple_norm"]],
                     [(d, F32)], [], "ple_out", ROW_TILE)
    sv.update(hn1=hn1, z=z, fl_t=fl_t, qa=qa, ka=ka, vb=vb, lse=lse, att=att, gm=gm,
              mc=mc, y1=y1, h1=h1, hn2=hn2, ab=ab, y2=y2, h2=h2, hr=hr, gl=gl, pe=pe, p_i=p_i)
    return h3, sv


def _layer_bwd(dh3, sv, w, tq, mid):
    s, d = dh3.shape
    g = {}
    by_rows = lambda a: a.reshape(N_DEV, -1, a.shape[-1])
    by_cols = lambda a: jnp.stack(jnp.split(a, N_DEV, axis=-1))

    def ple_bwd(dh3, gl, pe, gple):
        gate = _sigmoid(gl)
        pehat, rpe = _rms(pe)
        dgl = dh3 * (pehat * gple) * gate * (1.0 - gate)
        de = dh3 * gate
        return dgl, _rms_bwd(pehat, rpe, de * gple), _colsum(de * pehat)

    dgl, dpe, g["ple_norm"] = _rowwise(ple_bwd, [dh3, sv["gl"], sv["pe"]], [w["ple_norm"]],
                                       [(d, BF16), (d, BF16)], [d], "ple_bwd", ROW_TILE)
    g["w_ple_gate"] = by_rows(_mm(sv["hr"], dgl, "tn", BF16, "mm_dgate"))
    dhr = _mm(dgl, w["w_ple_gate"], "nt", BF16, "mm_dhr")
    g["w_ple"] = by_cols(_mm(sv["p_i"], dpe, "tn", BF16, "mm_dple"))

    def ffn_post_bwd(dh3, dhr, h2, y2, gpost):
        h2hat, r2 = _rms(h2)
        dh2 = dh3 + _rms_bwd(h2hat, r2, dhr)
        y2hat, ry = _rms(y2)
        return dh2, _rms_bwd(y2hat, ry, dh2 * gpost), _colsum(dh2 * y2hat)

    dh2, dy2, g["ffn_post_norm"] = _rowwise(ffn_post_bwd, [dh3, dhr, sv["h2"], sv["y2"]], [w["ffn_post_norm"]],
                                            [(d, F32), (d, BF16)], [d], "ffn_post_bwd", ROW_TILE)
    t, dab = _ffn_out_swiglu_bwd(dy2, w["w_ffn_out"], sv["ab"])
    dab = dab.reshape((N_DEV,) + dab.shape[2:])
    g["w_ffn_out"] = by_rows(_mm(t, dy2, "tn", BF16, "mm_dffn_out", a3="m", o3="m"))
    dhn2 = _mm(dab, w["w_ffn_in"], "nn", BF16, "mm_dhn2", a3="k", b3="k")
    g["w_ffn_in"] = _mm(dab, sv["hn2"], "tn", BF16, "mm_dffn_in", a3="m", o3="m")

    def mix_post_bwd(dh2, dhn2, h1, y1, gpre, gpost):
        h1hat, r1 = _rms(h1)
        dh1 = dh2 + _rms_bwd(h1hat, r1, dhn2 * gpre)
        y1hat, ry = _rms(y1)
        return dh1, _rms_bwd(y1hat, ry, dh1 * gpost), _colsum(dhn2 * h1hat), _colsum(dh1 * y1hat)

    dh1, dy1, g["ffn_pre_norm"], g["mix_post_norm"] = _rowwise(
        mix_post_bwd, [dh2, dhn2, sv["h1"], sv["y1"]], [w["ffn_pre_norm"], w["mix_post_norm"]],
        [(d, F32), (d, BF16)], [d, d], "mix_post_bwd", ROW_TILE)
    dmc = _mm(dy1, w["w_out"], "nt", BF16, "mm_dmc")
    g["w_out"] = by_rows(_mm(sv["mc"], dy1, "tn", BF16, "mm_dout"))
    w = dict(w, **mid(g, dmc))

    def mix_out_bwd(da, dg, att, gm, gain):
        atthat, ra = _rms(att)
        gmhat, rg = _rms(gm)
        dgain = jnp.concatenate([_colsum(da * atthat), _colsum(dg * gmhat)], axis=-1)
        return _rms_bwd(atthat, ra, da * gain[:, :D_ATT]), _rms_bwd(gmhat, rg, dg * gain[:, D_ATT:]), dgain

    datt, dgm, g["mix_out_norm"] = _rowwise(
        mix_out_bwd, [(dmc, 0, D_ATT), (dmc, 1, D_GM), sv["att"], sv["gm"]], [w["mix_out_norm"]],
        [(D_ATT, BF16), (D_GM, F32)], [D_ATT + D_GM], "mix_out_bwd", ROW_TILE)

    dgu, dgv, dwt, dbs_t, g["gm_v_norm"] = _gmlp_bwd(sv["z"], dgm, w["wt"], w["wt_t"], w["bs_t"], w["gm_v_norm"])
    g["gm_w_s"] = dwt * jnp.tril(jnp.ones((CHUNK, CHUNK), F32))[None]
    g["gm_b_s"] = dbs_t.T

    delta = _attn_delta(sv["att"], datt, tq)
    dq, dk, dv, dc_row = _attn_bwd(sv["qa"], sv["ka"], sv["vb"], datt, sv["lse"], delta, tq)
    dfl_t, db = _forget_bwd(dc_row.reshape(N_HEADS, s), sv["fl_t"], w["b_forget"])
    g["b_forget"] = db.reshape(1, N_HEADS)
    dz = _dz_concat([dq, dk, dv, dgu, dgv], dfl_t.T, ROW_TILE)
    dhn1 = _mm(dz, w["w_in"], "nt", BF16, "mm_dhn1")
    g["w_in"] = _w_in_split(_mm(sv["hn1"], dz, "tn", BF16, "mm_din"))

    def mix_pre_bwd(dh1, dhn1, h0, gpre):
        h0hat, r0 = _rms(h0)
        return dh1 + _rms_bwd(h0hat, r0, dhn1 * gpre), _colsum(dhn1 * h0hat)

    dh0, g["mix_pre_norm"] = _rowwise(mix_pre_bwd, [dh1, dhn1, sv["h0"]], [w["mix_pre_norm"]],
                                      [(d, F32)], [d], "mix_pre_bwd", ROW_TILE)
    return dh0, g


ANY = pl.BlockSpec(memory_space=pl.ANY)


def _all_gather(xs, layer, name):
    n = len(xs)

    def body(*refs):
        x_refs, out_refs = refs[:n], refs[n:2 * n]
        send_sems, recv_sems, local_sems = refs[2 * n:]
        x, y, c = lax.axis_index("x"), lax.axis_index("y"), lax.axis_index("c")
        me, sibling = (x, y, c), (x, y, 1 - c)
        chips = [(1 - x, y), (x, 1 - y), (1 - x, 1 - y)]

        def shard(a):
            return x_refs[a] if layer is None else x_refs[a].at[layer]

        def rows(a, px, py, pc):
            return out_refs[a].at[4 * px + 2 * py + pc]

        def copy(a, kk, block, to, from_shard=False):
            return pltpu.make_async_remote_copy(
                src_ref=shard(a) if from_shard else rows(a, *block), dst_ref=rows(a, *block),
                send_sem=send_sems.at[7 * a + kk], recv_sem=recv_sems.at[7 * a + kk],
                device_id=to, device_id_type=MESH)

        mine = [pltpu.make_async_copy(shard(a), rows(a, *me), local_sems.at[a]) for a in range(n)]
        for cp in mine:
            cp.start()
        first = []
        for a in range(n):
            first.append(copy(a, 0, me, sibling, from_shard=True))
            first += [copy(a, 1 + j, me, (*chip, c), from_shard=True) for j, chip in enumerate(chips)]
        for cp in first:
            cp.start()
        passed = []
        for j, chip in enumerate(chips):
            for a in range(n):
                copy(a, 1 + j, (*chip, c), me).wait_recv()
                passed.append(copy(a, 4 + j, (*chip, c), sibling))
                passed[-1].start()
        for a in range(n):
            copy(a, 0, sibling, me).wait_recv()
        for j, chip in enumerate(chips):
            for a in range(n):
                copy(a, 4 + j, (*chip, 1 - c), me).wait_recv()
        for cp in first + passed:
            cp.wait_send()
        for cp in mine:
            cp.wait()

    shapes = [x.shape if layer is None else x.shape[1:] for x in xs]
    return pl.pallas_call(
        body, name=name, out_shape=[jax.ShapeDtypeStruct((N_DEV,) + sh, x.dtype) for sh, x in zip(shapes, xs)],
        in_specs=[ANY] * n, out_specs=[ANY] * n,
        scratch_shapes=[pltpu.SemaphoreType.DMA((7 * n,)), pltpu.SemaphoreType.DMA((7 * n,)),
                        pltpu.SemaphoreType.DMA((n,))],
    )(*xs)


HBM = pl.BlockSpec(memory_space=pltpu.HBM)
SEMS = pl.BlockSpec(memory_space=pltpu.SEMAPHORE)
EFFECT = pltpu.SideEffectType.DATAFLOW_SIDE_EFFECTING
FLIPS = tuple((fx, fy, fc) for fx in (0, 1) for fy in (0, 1) for fc in (0, 1))[1:]


def _exchange_copies(src_refs, land_refs, send_sems, recv_sems, layer, scatter):
    x, y, c = lax.axis_index("x"), lax.axis_index("y"), lax.axis_index("c")
    me = 4 * x + 2 * y + c
    copies = []
    for a, (src, land) in enumerate(zip(src_refs, land_refs)):
        for f, (fx, fy, fc) in enumerate(FLIPS):
            px, py, pc = (1 - x if fx else x), (1 - y if fy else y), (1 - c if fc else c)
            if scatter:
                block = src.at[4 * px + 2 * py + pc]
            else:
                block = src if layer is None else src.at[layer]
            copies.append(pltpu.make_async_remote_copy(
                src_ref=block, dst_ref=land.at[me], send_sem=send_sems.at[7 * a + f], recv_sem=recv_sems.at[7 * a + f],
                device_id=(px, py, pc), device_id_type=MESH))
    return copies


def _exchange_start(srcs, lands, layer, scatter, name):
    n = len(srcs)

    def body(*refs):
        for cp in _exchange_copies(refs[:n], refs[n:2 * n], refs[2 * n], refs[2 * n + 1], layer, scatter):
            cp.start()
        token = refs[-1]
        token[...] = jnp.zeros_like(token)

    operands = list(srcs) + list(lands)
    outs = pl.pallas_call(
        body, name=name,
        out_shape=(pltpu.SemaphoreType.DMA((7 * n,)), pltpu.SemaphoreType.DMA((7 * n,)),
                   *[pltpu.HBM(a.shape, a.dtype) for a in operands], jax.ShapeDtypeStruct((8, LANE), F32)),
        in_specs=[HBM] * (2 * n),
        out_specs=(SEMS, SEMS, *[HBM] * (2 * n), pl.BlockSpec(memory_space=pltpu.VMEM)),
        input_output_aliases={i: 2 + i for i in range(2 * n)},
        compiler_params=pltpu.CompilerParams(has_side_effects=EFFECT),
    )(*[pltpu.with_memory_space_constraint(a, pltpu.HBM) for a in operands])
    return outs[0], outs[1], outs[2:2 + n], outs[2 + n:2 + 2 * n], outs[-1]


def _exchange_wait(started, after, layer, scatter, name):
    send_sems, recv_sems, srcs, lands, _ = started
    n = len(srcs)

    def body(*refs):
        for cp in _exchange_copies(refs[:n], refs[n:2 * n], refs[2 * n], refs[2 * n + 1], layer, scatter):
            cp.wait_send()
            cp.wait_recv()

    operands = list(srcs) + list(lands)
    outs = pl.pallas_call(
        body, name=name, out_shape=tuple(pltpu.HBM(a.shape, a.dtype) for a in operands),
        in_specs=[HBM] * (2 * n) + [SEMS, SEMS, ANY], out_specs=[HBM] * (2 * n),
        input_output_aliases={i: i for i in range(2 * n)},
        compiler_params=pltpu.CompilerParams(has_side_effects=EFFECT),
    )(*operands, send_sems, recv_sems, after)
    return outs[:n], outs[n:]


def _sum_devices(parts):
    _, r, c = parts.shape

    def body(p_ref, o_ref):
        acc = p_ref[0].astype(F32)
        for j in range(1, N_DEV):
            acc = acc + p_ref[j].astype(F32)
        o_ref[...] = acc

    return pl.pallas_call(
        body, name="small_sum", out_shape=jax.ShapeDtypeStruct((r, c), F32), grid=(r // SMALL_ROWS,),
        in_specs=[pl.BlockSpec((N_DEV, SMALL_ROWS, c), lambda i: (0, i, 0))],
        out_specs=pl.BlockSpec((SMALL_ROWS, c), lambda i: (i, 0)),
        compiler_params=_cparams("parallel"),
    )(parts)


def _adamw_math(w, g, m, v):
    m = ADAM_B1 * m + (1.0 - ADAM_B1) * g
    v = ADAM_B2 * v + (1.0 - ADAM_B2) * (g * g)
    m_hat = m / (1.0 - ADAM_B1 ** ADAM_STEP)
    v_hat = v / (1.0 - ADAM_B2 ** ADAM_STEP)
    return -ADAM_LR * (m_hat / (jnp.sqrt(v_hat) + ADAM_EPS) + ADAM_WD * w), m, v


def _adamw_shard(w, m, v, parts, layer, outs, name):
    _, a, b = w.shape
    ta = _tile(a, 256, 16)
    if outs is None:
        outs = [lax.empty(w.shape, F32) for _ in range(4)]

    def body(w_ref, m_ref, v_ref, p_ref, *refs):
        g_ref, d_ref, nm_ref, nv_ref = refs[4:]
        g = p_ref[0].astype(F32)
        for j in range(1, N_DEV):
            g = g + p_ref[j].astype(F32)
        g_ref[0] = g
        d_ref[0], nm_ref[0], nv_ref[0] = _adamw_math(w_ref[0], g, m_ref[0], v_ref[0])

    mine = pl.BlockSpec((1, ta, b), lambda i: (layer, i, 0))
    return pl.pallas_call(
        body, name=name, out_shape=[jax.ShapeDtypeStruct(w.shape, F32)] * 4, grid=(a // ta,),
        in_specs=[mine, mine, mine, pl.BlockSpec((N_DEV, ta, b), lambda i: (0, i, 0))] + [ANY] * 4,
        out_specs=[mine] * 4, input_output_aliases={4 + k: k for k in range(4)},
        compiler_params=_cparams("parallel"),
    )(w, m, v, parts, *outs)


def _w_in_moves():
    n = D_IN // N_DEV
    runs = ((0, 3 * D_ATT, 0), (3 * D_ATT, 3 * D_ATT + N_HEADS, F_OFF), (3 * D_ATT + N_HEADS, D_IN, 3 * D_ATT))
    moves = []
    for j in range(N_DEV):
        for lo, hi, padded in runs:
            a, b = max(n * j, lo), min(n * (j + 1), hi)
            if a < b:
                moves.append((j, a - n * j, padded + a - lo, b - a))
    return moves


def _w_in_assemble(blocks):
    _, d, n = blocks.shape
    tr = _tile(d, 256, 16)

    def body(b_ref, o_ref):
        o_ref[:, D_IN:] = jnp.zeros((tr, D_IN_PAD - D_IN), o_ref.dtype)
        for j, col, padded, width in _w_in_moves():
            o_ref[:, padded:padded + width] = b_ref[j, :, col:col + width]

    return pl.pallas_call(
        body, name="w_in_assemble", out_shape=jax.ShapeDtypeStruct((d, D_IN_PAD), blocks.dtype), grid=(d // tr,),
        in_specs=[pl.BlockSpec((N_DEV, tr, n), lambda i: (0, i, 0))],
        out_specs=pl.BlockSpec((tr, D_IN_PAD), lambda i: (i, 0)),
        compiler_params=_cparams("parallel"),
    )(blocks)


def _w_in_split(padded):
    d = padded.shape[0]
    n = D_IN // N_DEV
    tr = _tile(d, 256, 16)

    def body(p_ref, o_ref):
        for j, col, src, width in _w_in_moves():
            o_ref[j, :, col:col + width] = p_ref[:, src:src + width]

    return pl.pallas_call(
        body, name="w_in_split", out_shape=jax.ShapeDtypeStruct((N_DEV, d, n), padded.dtype), grid=(d // tr,),
        in_specs=[pl.BlockSpec((tr, D_IN_PAD), lambda i: (i, 0))],
        out_specs=pl.BlockSpec((N_DEV, tr, n), lambda i: (0, i, 0)),
        compiler_params=_cparams("parallel"),
    )(padded)


def _small_rows(size):
    return -(-size // (8 * SMALL_COLS)) * 8


def _pack_small(pieces):
    rows = []
    for p in pieces:
        flat = p.reshape(-1)
        rows.append(jnp.pad(flat, (0, _small_rows(flat.shape[0]) * SMALL_COLS - flat.shape[0])).reshape(-1, SMALL_COLS))
    used = sum(r.shape[0] for r in rows)
    rows.append(jnp.zeros((-used % SMALL_ROWS, SMALL_COLS), F32))
    return jnp.concatenate(rows)


def kernel(x, p, mix_pre_norm, mix_post_norm, w_in, b_forget, gm_v_norm, gm_w_s, gm_b_s, mix_out_norm, w_out, ffn_pre_norm, ffn_post_norm, w_ffn_in, w_ffn_out, w_ple, ple_norm, w_ple_gate, loss_target, m_mix_pre_norm, m_mix_post_norm, m_w_in, m_b_forget, m_gm_v_norm, m_gm_w_s, m_gm_b_s, m_mix_out_norm, m_w_out, m_ffn_pre_norm, m_ffn_post_norm, m_w_ffn_in, m_w_ffn_out, m_w_ple, m_ple_norm, m_w_ple_gate, v_mix_pre_norm, v_mix_post_norm, v_w_in, v_b_forget, v_gm_v_norm, v_gm_w_s, v_gm_b_s, v_mix_out_norm, v_w_out, v_ffn_pre_norm, v_ffn_post_norm, v_w_ffn_in, v_w_ffn_out, v_w_ple, v_ple_norm, v_w_ple_gate):
    given = dict(locals())
    weights = {n: given[n] for n in WEIGHT_ORDER}
    mom_m = {n: given["m_" + n] for n in WEIGHT_ORDER}
    mom_v = {n: given["v_" + n] for n in WEIGHT_ORDER}
    depth = w_in.shape[0]
    s, d = x.shape[1], x.shape[2]
    tq = _tile(s, ATT_BLOCK)
    me = 4 * lax.axis_index("x") + 2 * lax.axis_index("y") + lax.axis_index("c")
    tril = jnp.tril(jnp.ones((CHUNK, CHUNK), F32))

    def landing(block):
        return lax.dynamic_update_index_in_dim(lax.empty((N_DEV,) + block.shape, block.dtype), block, me, 0)

    def mix_weights(i, got):
        wt = gm_w_s[i] * tril[None]
        lw = dict(
            w_in=_w_in_assemble(got["w_in"]),
            b_forget=b_forget[i][:, None], wt=wt.astype(BF16), wt_t=wt.transpose(0, 2, 1).astype(BF16),
            bs_t=gm_b_s[i].T)
        lw.update({n: weights[n][i][None] for n in ("mix_pre_norm", "mix_post_norm", "gm_v_norm", "mix_out_norm",
                                                    "ffn_pre_norm", "ffn_post_norm", "ple_norm")})
        return lw

    def rest_weights(got):
        return dict(w_out=got["w_out"].reshape(-1, d), w_ffn_in=got["w_ffn_in"],
                    w_ffn_out=got["w_ffn_out"].reshape(N_DEV // 2, -1, d),
                    w_ple=jnp.concatenate([got["w_ple"][j] for j in range(N_DEV)], axis=-1),
                    w_ple_gate=got["w_ple_gate"].reshape(-1, d))

    def shard_view(n, a):
        return jnp.transpose(a, (0, 2, 1)) if n in TRANSPOSED_WEIGHTS else a

    shards = {n: shard_view(n, weights[n].astype(BF16)) for n in MATRIX_WEIGHTS}

    def gather_start(i):
        started = {}
        order = jnp.zeros((), BF16)
        for tag, grp in EXCHANGE_GROUPS.items():
            started[tag] = _exchange_start([shards[n] for n in grp], [landing(shards[n][i] + order) for n in grp], i,
                                           False, f"weights_gather_start_{i}_{tag}")
            order = started[tag][4][0, 0].astype(BF16)
        return started

    def gather_finish(i, tag, pending, after):
        srcs, got = _exchange_wait(pending[tag], after, i, False, f"weights_gather_wait_{i}_{tag}")
        shards.update(zip(EXCHANGE_GROUPS[tag], srcs))
        return dict(zip(EXCHANGE_GROUPS[tag], got))

    h = x[0]
    saved, layer_w = [], []
    pending = gather_start(0)
    for i in range(depth):
        lw = mix_weights(i, gather_finish(i, "mix", pending, h))
        if i == 0:
            lw["mix_pre_norm"] = lw["mix_pre_norm"] + pending["rest"][4][:1, :1]
        following = {}

        def late(att, i=i, pending=pending, lw=lw, following=following):
            rest = rest_weights(gather_finish(i, "rest", pending, att))
            lw.update(rest)
            if i + 1 == depth:
                return rest
            following.update(gather_start(i + 1))
            token = following["mix"][4][:1, :1] + following["rest"][4][:1, :1]
            return dict(rest, mix_out_norm=lw["mix_out_norm"] + token)

        h, sv = _layer_fwd(h, p[i, 0], lw, tq, late)
        layer_w.append(lw)
        saved.append(sv)
        pending = following

    def loss_head(y, t):
        err = y - t
        return err * (1.0 / d), _colsum(err * err)

    dh, sq = _rowwise(loss_head, [h, loss_target[0]], [], [(d, F32)], [d], "loss_head", ROW_TILE)
    loss = lax.psum(0.5 * jnp.sum(sq) / d, AXES)

    layer_g = [None] * depth
    shard_out = {n: None for n in MATRIX_WEIGHTS}

    def scatter_start(i, tag, g):
        full_g = [g[n] for n in EXCHANGE_GROUPS[tag]]
        lands = [landing(lax.dynamic_index_in_dim(gf, me, 0, keepdims=False)) for gf in full_g]
        return _exchange_start(full_g, lands, None, True, f"grads_scatter_start_{i}_{tag}")

    def scatter_finish(i, tag, started, after):
        _, parts = _exchange_wait(started[tag], after, None, True, f"grads_scatter_wait_{i}_{tag}")
        for n, part in zip(EXCHANGE_GROUPS[tag], parts):
            shard_out[n] = _adamw_shard(shard_view(n, weights[n]), shard_view(n, mom_m[n]), shard_view(n, mom_v[n]),
                                        part, i, shard_out[n], "adamw_" + n)

    before = None
    for i in reversed(range(depth)):
        lw = layer_w[i]
        if before is not None:
            lw = dict(lw, ple_norm=lw["ple_norm"] + before[1]["mix"][4][:1, :1])
        started = {}

        def mid(g, dmc, i=i, before=before, started=started, lw=lw):
            if before is not None:
                scatter_finish(before[0], "rest", before[1], dmc)
            started["rest"] = scatter_start(i, "rest", g)
            return dict(mix_out_norm=lw["mix_out_norm"] + started["rest"][4][:1, :1])

        dh, layer_g[i] = _layer_bwd(dh, saved[i], lw, tq, mid)
        if before is not None:
            scatter_finish(before[0], "mix", before[1], dh)
        started["mix"] = scatter_start(i, "mix", layer_g[i])
        before = (i, started)
    scatter_finish(before[0], "rest", before[1], before[1]["mix"][4])
    scatter_finish(before[0], "mix", before[1], dh)
    grad_x = dh[None]

    grads, deltas, new_m, new_v = {}, {}, {}, {}
    for n in MATRIX_WEIGHTS:
        grads[n], deltas[n], new_m[n], new_v[n] = (shard_view(n, a) for a in shard_out[n])

    small_g = _pack_small([jnp.stack([layer_g[i][n].reshape(-1) for i in range(depth)]) for n in SMALL_WEIGHTS])
    (gathered,) = _all_gather([small_g.astype(BF16)], None, "small_grads_all_gather")
    g_small = _sum_devices(gathered)
    pack = lambda t: _pack_small([t[n] for n in SMALL_WEIGHTS])
    dl, nm, nv = _rowwise(_adamw_math, [pack(weights), g_small, pack(mom_m), pack(mom_v)], [],
                          [(SMALL_COLS, F32)] * 3, [], "adamw_small", SMALL_ROWS)
    row = 0
    for n in SMALL_WEIGHTS:
        shp, size = weights[n].shape, weights[n].size
        grads[n], deltas[n], new_m[n], new_v[n] = (
            a[row:row + _small_rows(size)].reshape(-1)[:size].reshape(shp) for a in (g_small, dl, nm, nv))
        row += _small_rows(size)

    return (loss, grad_x, *[grads[n] for n in WEIGHT_ORDER], *[deltas[n] for n in WEIGHT_ORDER],
            *[new_m[n] for n in WEIGHT_ORDER], *[new_v[n] for n in WEIGHT_ORDER])
```
